```python
import jax, jax.numpy as jnp
from jax import lax
import numpy as np

D_MODEL = 1024
BATCH = 8
SEQ = 8192
DEPTH = 2

N_MEM = 256
D_A = D_MODEL // 2
D_B = D_MODEL // 2
A_KERNEL = 31
B_KERNEL = 3
D_EVEN_IN = 2 * D_A + 3 * D_B
CHUNK = 128
C_GROUPS = 8
D_C = D_MODEL
C_GROUP_DIM = D_C // C_GROUPS
XA_HEADS = 4
XA_HEAD_DIM = D_MODEL // XA_HEADS
D_FF = ((8 * D_MODEL // 3 + 255) // 256) * 256
N_EVEN = (DEPTH + 1) // 2
N_ODD = DEPTH // 2
RMS_EPS = 1e-6
LN_EPS = 1e-5

kernel_name = 'hybrid_conv_sgu_memxattn_encoder'


def rms_norm(x, g):
    xf = x.astype(jnp.float32)
    y = xf * lax.rsqrt(jnp.mean(xf * xf, axis=-1, keepdims=True) + RMS_EPS)
    return (y * g.astype(jnp.float32)).astype(x.dtype)


def layer_norm(x, g, b):
    xf = x.astype(jnp.float32)
    mu = jnp.mean(xf, axis=-1, keepdims=True)
    var = jnp.mean(jnp.square(xf - mu), axis=-1, keepdims=True)
    y = (xf - mu) * lax.rsqrt(var + LN_EPS)
    return (y * g.astype(jnp.float32) + b.astype(jnp.float32)).astype(x.dtype)


def depthwise_conv(x, w, b):
    k = w.shape[0]
    pad = k // 2
    y = lax.conv_general_dilated(
        x, w[:, None, :].astype(x.dtype), window_strides=(1,),
        padding=[(pad, pad)], dimension_numbers=('NWC', 'WIO', 'NWC'),
        feature_group_count=x.shape[-1])
    return y + b


def conv_pair_mixer(n, w_in, a_conv_w, a_conv_b, a_ln_g, a_ln_b, b_conv_w, b_conv_b, w_out):
    z = n @ w_in
    a_val, a_gate, b_h, b_gb, b_gc = jnp.split(
        z, [D_A, 2 * D_A, 2 * D_A + D_B, 2 * D_A + 2 * D_B], axis=-1)
    a = a_val * jax.nn.sigmoid(a_gate)
    a = depthwise_conv(a, a_conv_w, a_conv_b)
    a = jax.nn.silu(layer_norm(a, a_ln_g, a_ln_b))
    b = b_gb * depthwise_conv(b_gc * b_h, b_conv_w, b_conv_b)
    return jnp.concatenate([a, b], axis=-1) @ w_out


def chunked_sgu_mixer(n, w_in, c_ln_g, c_ln_b, w_s, b_s, w_out):
    bsz, s, _ = n.shape
    z = jax.nn.gelu(n @ w_in)
    u, v = jnp.split(z, 2, axis=-1)
    v = layer_norm(v, c_ln_g, c_ln_b)
    v = v.reshape(bsz, s // CHUNK, CHUNK, C_GROUPS, C_GROUP_DIM)
    sv = jnp.einsum('gpq,bnqgc->bnpgc', w_s, v) + jnp.transpose(b_s)[:, :, None]
    y = u * sv.reshape(bsz, s, D_C)
    return y @ w_out


def memory_cross_attention(n, mem_n, w_q, w_k, w_v, w_o):
    bsz, s, _ = n.shape
    m = mem_n.shape[1]
    q = (n @ w_q).reshape(bsz, s, XA_HEADS, XA_HEAD_DIM)
    k = (mem_n @ w_k).reshape(bsz, m, XA_HEADS, XA_HEAD_DIM)
    v = (mem_n @ w_v).reshape(bsz, m, XA_HEADS, XA_HEAD_DIM)
    scores = jnp.einsum('bshd,bmhd->bhsm', q, k).astype(jnp.float32) * (XA_HEAD_DIM ** -0.5)
    p = jax.nn.softmax(scores, axis=-1).astype(v.dtype)
    o = jnp.einsum('bhsm,bmhd->bshd', p, v).reshape(bsz, s, D_MODEL)
    return o @ w_o


def swiglu(n, w_gate, w_up, w_down):
    return (jax.nn.silu(n @ w_gate) * (n @ w_up)) @ w_down


def _fwd_setup_inputs(seed: int = 0) -> dict:
    key = jax.random.key(seed)
    ks = iter(jax.random.split(key, 40))

    def nrm(shape, scale):
        return jax.random.normal(next(ks), shape, jnp.float32) * scale

    def gain(shape):
        return 1.0 + 0.1 * jax.random.normal(next(ks), shape, jnp.float32)

    D = D_MODEL
    return {
        'x': nrm((BATCH, SEQ, D), 1.0),
        'mem': nrm((BATCH, N_MEM, D), 1.0),
        'g_mix': gain((DEPTH, D)),
        'g_xattn': gain((DEPTH, D)),
        'g_mem': gain((DEPTH, D)),
        'g_ffn': gain((DEPTH, D)),
        'g_final': gain((D,)),
        'ev_w_in': nrm((N_EVEN, D, D_EVEN_IN), D ** -0.5),
        'ev_a_conv_w': nrm((N_EVEN, A_KERNEL, D_A), A_KERNEL ** -0.5),
        'ev_a_conv_b': nrm((N_EVEN, D_A), 0.02),
        'ev_a_ln_g': gain((N_EVEN, D_A)),
        'ev_a_ln_b': nrm((N_EVEN, D_A), 0.02),
        'ev_b_conv_w': nrm((N_EVEN, B_KERNEL, D_B), B_KERNEL ** -0.5),
        'ev_b_conv_b': nrm((N_EVEN, D_B), 0.02),
        'ev_w_out': nrm((N_EVEN, D_A + D_B, D), (D_A + D_B) ** -0.5),
        'od_w_in': nrm((N_ODD, D, 2 * D_C), D ** -0.5),
        'od_c_ln_g': gain((N_ODD, D_C)),
        'od_c_ln_b': nrm((N_ODD, D_C), 0.02),
        'od_w_s': nrm((N_ODD, C_GROUPS, CHUNK, CHUNK), CHUNK ** -0.5),
        'od_b_s': gain((N_ODD, C_GROUPS, CHUNK)),
        'od_w_out': nrm((N_ODD, D_C, D), D_C ** -0.5),
        'xa_w_q': nrm((DEPTH, D, D), D ** -0.5),
        'xa_w_k': nrm((DEPTH, D, D), D ** -0.5),
        'xa_w_v': nrm((DEPTH, D, D), D ** -0.5),
        'xa_w_o': nrm((DEPTH, D, D), D ** -0.5),
        'ffn_w_gate': nrm((DEPTH, D, D_FF), D ** -0.5),
        'ffn_w_up': nrm((DEPTH, D, D_FF), D ** -0.5),
        'ffn_w_down': nrm((DEPTH, D_FF, D), D_FF ** -0.5),
    }


def _fwd_reference(x, mem, g_mix, g_xattn, g_mem, g_ffn, g_final,
              ev_w_in, ev_a_conv_w, ev_a_conv_b, ev_a_ln_g, ev_a_ln_b,
              ev_b_conv_w, ev_b_conv_b, ev_w_out,
              od_w_in, od_c_ln_g, od_c_ln_b, od_w_s, od_b_s, od_w_out,
              xa_w_q, xa_w_k, xa_w_v, xa_w_o,
              ffn_w_gate, ffn_w_up, ffn_w_down):
    h = x
    for i in range(DEPTH):
        j = i // 2
        n = rms_norm(h, g_mix[i])
        if i % 2 == 0:
            h = h + conv_pair_mixer(n, ev_w_in[j], ev_a_conv_w[j], ev_a_conv_b[j],
                                    ev_a_ln_g[j], ev_a_ln_b[j], ev_b_conv_w[j],
                                    ev_b_conv_b[j], ev_w_out[j])
        else:
            h = h + chunked_sgu_mixer(n, od_w_in[j], od_c_ln_g[j], od_c_ln_b[j],
                                      od_w_s[j], od_b_s[j], od_w_out[j])
        mem_n = rms_norm(mem, g_mem[i])
        h = h + memory_cross_attention(rms_norm(h, g_xattn[i]), mem_n,
                                       xa_w_q[i], xa_w_k[i], xa_w_v[i], xa_w_o[i])
        h = h + swiglu(rms_norm(h, g_ffn[i]), ffn_w_gate[i], ffn_w_up[i], ffn_w_down[i])
    return rms_norm(h, g_final)


import jax as _jax
import jax.numpy as _jnp

TWIN_FORMAT = 'train_step'
FWD_PARAMS = ['x', 'mem', 'g_mix', 'g_xattn', 'g_mem', 'g_ffn', 'g_final', 'ev_w_in', 'ev_a_conv_w', 'ev_a_conv_b', 'ev_a_ln_g', 'ev_a_ln_b', 'ev_b_conv_w', 'ev_b_conv_b', 'ev_w_out', 'od_w_in', 'od_c_ln_g', 'od_c_ln_b', 'od_w_s', 'od_b_s', 'od_w_out', 'xa_w_q', 'xa_w_k', 'xa_w_v', 'xa_w_o', 'ffn_w_gate', 'ffn_w_up', 'ffn_w_down']
TWIN_WEIGHTS = ['g_mix', 'g_xattn', 'g_mem', 'g_ffn', 'g_final', 'ev_w_in', 'ev_a_conv_w', 'ev_a_conv_b', 'ev_a_ln_g', 'ev_a_ln_b', 'ev_b_conv_w', 'ev_b_conv_b', 'ev_w_out', 'od_w_in', 'od_c_ln_g', 'od_c_ln_b', 'od_w_s', 'od_b_s', 'od_w_out', 'xa_w_q', 'xa_w_k', 'xa_w_v', 'xa_w_o', 'ffn_w_gate', 'ffn_w_up', 'ffn_w_down']
TWIN_DIFF_INPUT = 'x'
TWIN_INPUTS = ['x', 'mem', 'g_mix', 'g_xattn', 'g_mem', 'g_ffn', 'g_final', 'ev_w_in', 'ev_a_conv_w', 'ev_a_conv_b', 'ev_a_ln_g', 'ev_a_ln_b', 'ev_b_conv_w', 'ev_b_conv_b', 'ev_w_out', 'od_w_in', 'od_c_ln_g', 'od_c_ln_b', 'od_w_s', 'od_b_s', 'od_w_out', 'xa_w_q', 'xa_w_k', 'xa_w_v', 'xa_w_o', 'ffn_w_gate', 'ffn_w_up', 'ffn_w_down', 'loss_target', 'm_g_mix', 'm_g_xattn', 'm_g_mem', 'm_g_ffn', 'm_g_final', 'm_ev_w_in', 'm_ev_a_conv_w', 'm_ev_a_conv_b', 'm_ev_a_ln_g', 'm_ev_a_ln_b', 'm_ev_b_conv_w', 'm_ev_b_conv_b', 'm_ev_w_out', 'm_od_w_in', 'm_od_c_ln_g', 'm_od_c_ln_b', 'm_od_w_s', 'm_od_b_s', 'm_od_w_out', 'm_xa_w_q', 'm_xa_w_k', 'm_xa_w_v', 'm_xa_w_o', 'm_ffn_w_gate', 'm_ffn_w_up', 'm_ffn_w_down', 'v_g_mix', 'v_g_xattn', 'v_g_mem', 'v_g_ffn', 'v_g_final', 'v_ev_w_in', 'v_ev_a_conv_w', 'v_ev_a_conv_b', 'v_ev_a_ln_g', 'v_ev_a_ln_b', 'v_ev_b_conv_w', 'v_ev_b_conv_b', 'v_ev_w_out', 'v_od_w_in', 'v_od_c_ln_g', 'v_od_c_ln_b', 'v_od_w_s', 'v_od_b_s', 'v_od_w_out', 'v_xa_w_q', 'v_xa_w_k', 'v_xa_w_v', 'v_xa_w_o', 'v_ffn_w_gate', 'v_ffn_w_up', 'v_ffn_w_down']
TWIN_OUTPUTS = ['loss', 'grad_x', 'grad_g_mix', 'grad_g_xattn', 'grad_g_mem', 'grad_g_ffn', 'grad_g_final', 'grad_ev_w_in', 'grad_ev_a_conv_w', 'grad_ev_a_conv_b', 'grad_ev_a_ln_g', 'grad_ev_a_ln_b', 'grad_ev_b_conv_w', 'grad_ev_b_conv_b', 'grad_ev_w_out', 'grad_od_w_in', 'grad_od_c_ln_g', 'grad_od_c_ln_b', 'grad_od_w_s', 'grad_od_b_s', 'grad_od_w_out', 'grad_xa_w_q', 'grad_xa_w_k', 'grad_xa_w_v', 'grad_xa_w_o', 'grad_ffn_w_gate', 'grad_ffn_w_up', 'grad_ffn_w_down', 'delta_g_mix', 'delta_g_xattn', 'delta_g_mem', 'delta_g_ffn', 'delta_g_final', 'delta_ev_w_in', 'delta_ev_a_conv_w', 'delta_ev_a_conv_b', 'delta_ev_a_ln_g', 'delta_ev_a_ln_b', 'delta_ev_b_conv_w', 'delta_ev_b_conv_b', 'delta_ev_w_out', 'delta_od_w_in', 'delta_od_c_ln_g', 'delta_od_c_ln_b', 'delta_od_w_s', 'delta_od_b_s', 'delta_od_w_out', 'delta_xa_w_q', 'delta_xa_w_k', 'delta_xa_w_v', 'delta_xa_w_o', 'delta_ffn_w_gate', 'delta_ffn_w_up', 'delta_ffn_w_down', 'new_m_g_mix', 'new_m_g_xattn', 'new_m_g_mem', 'new_m_g_ffn', 'new_m_g_final', 'new_m_ev_w_in', 'new_m_ev_a_conv_w', 'new_m_ev_a_conv_b', 'new_m_ev_a_ln_g', 'new_m_ev_a_ln_b', 'new_m_ev_b_conv_w', 'new_m_ev_b_conv_b', 'new_m_ev_w_out', 'new_m_od_w_in', 'new_m_od_c_ln_g', 'new_m_od_c_ln_b', 'new_m_od_w_s', 'new_m_od_b_s', 'new_m_od_w_out', 'new_m_xa_w_q', 'new_m_xa_w_k', 'new_m_xa_w_v', 'new_m_xa_w_o', 'new_m_ffn_w_gate', 'new_m_ffn_w_up', 'new_m_ffn_w_down', 'new_v_g_mix', 'new_v_g_xattn', 'new_v_g_mem', 'new_v_g_ffn', 'new_v_g_final', 'new_v_ev_w_in', 'new_v_ev_a_conv_w', 'new_v_ev_a_conv_b', 'new_v_ev_a_ln_g', 'new_v_ev_a_ln_b', 'new_v_ev_b_conv_w', 'new_v_ev_b_conv_b', 'new_v_ev_w_out', 'new_v_od_w_in', 'new_v_od_c_ln_g', 'new_v_od_c_ln_b', 'new_v_od_w_s', 'new_v_od_b_s', 'new_v_od_w_out', 'new_v_xa_w_q', 'new_v_xa_w_k', 'new_v_xa_w_v', 'new_v_xa_w_o', 'new_v_ffn_w_gate', 'new_v_ffn_w_up', 'new_v_ffn_w_down']
TWIN_LEAF_KINDS = {'loss': 'loss', 'grad_x': 'grad_x', 'grad_g_mix': 'grad_w', 'grad_g_xattn': 'grad_w', 'grad_g_mem': 'grad_w', 'grad_g_ffn': 'grad_w', 'grad_g_final': 'grad_w', 'grad_ev_w_in': 'grad_w', 'grad_ev_a_conv_w': 'grad_w', 'grad_ev_a_conv_b': 'grad_w', 'grad_ev_a_ln_g': 'grad_w', 'grad_ev_a_ln_b': 'grad_w', 'grad_ev_b_conv_w': 'grad_w', 'grad_ev_b_conv_b': 'grad_w', 'grad_ev_w_out': 'grad_w', 'grad_od_w_in': 'grad_w', 'grad_od_c_ln_g': 'grad_w', 'grad_od_c_ln_b': 'grad_w', 'grad_od_w_s': 'grad_w', 'grad_od_b_s': 'grad_w', 'grad_od_w_out': 'grad_w', 'grad_xa_w_q': 'grad_w', 'grad_xa_w_k': 'grad_w', 'grad_xa_w_v': 'grad_w', 'grad_xa_w_o': 'grad_w', 'grad_ffn_w_gate': 'grad_w', 'grad_ffn_w_up': 'grad_w', 'grad_ffn_w_down': 'grad_w', 'delta_g_mix': 'delta_w', 'delta_g_xattn': 'delta_w', 'delta_g_mem': 'delta_w', 'delta_g_ffn': 'delta_w', 'delta_g_final': 'delta_w', 'delta_ev_w_in': 'delta_w', 'delta_ev_a_conv_w': 'delta_w', 'delta_ev_a_conv_b': 'delta_w', 'delta_ev_a_ln_g': 'delta_w', 'delta_ev_a_ln_b': 'delta_w', 'delta_ev_b_conv_w': 'delta_w', 'delta_ev_b_conv_b': 'delta_w', 'delta_ev_w_out': 'delta_w', 'delta_od_w_in': 'delta_w', 'delta_od_c_ln_g': 'delta_w', 'delta_od_c_ln_b': 'delta_w', 'delta_od_w_s': 'delta_w', 'delta_od_b_s': 'delta_w', 'delta_od_w_out': 'delta_w', 'delta_xa_w_q': 'delta_w', 'delta_xa_w_k': 'delta_w', 'delta_xa_w_v': 'delta_w', 'delta_xa_w_o': 'delta_w', 'delta_ffn_w_gate': 'delta_w', 'delta_ffn_w_up': 'delta_w', 'delta_ffn_w_down': 'delta_w', 'new_m_g_mix': 'new_m', 'new_m_g_xattn': 'new_m', 'new_m_g_mem': 'new_m', 'new_m_g_ffn': 'new_m', 'new_m_g_final': 'new_m', 'new_m_ev_w_in': 'new_m', 'new_m_ev_a_conv_w': 'new_m', 'new_m_ev_a_conv_b': 'new_m', 'new_m_ev_a_ln_g': 'new_m', 'new_m_ev_a_ln_b': 'new_m', 'new_m_ev_b_conv_w': 'new_m', 'new_m_ev_b_conv_b': 'new_m', 'new_m_ev_w_out': 'new_m', 'new_m_od_w_in': 'new_m', 'new_m_od_c_ln_g': 'new_m', 'new_m_od_c_ln_b': 'new_m', 'new_m_od_w_s': 'new_m', 'new_m_od_b_s': 'new_m', 'new_m_od_w_out': 'new_m', 'new_m_xa_w_q': 'new_m', 'new_m_xa_w_k': 'new_m', 'new_m_xa_w_v': 'new_m', 'new_m_xa_w_o': 'new_m', 'new_m_ffn_w_gate': 'new_m', 'new_m_ffn_w_up': 'new_m', 'new_m_ffn_w_down': 'new_m', 'new_v_g_mix': 'new_v', 'new_v_g_xattn': 'new_v', 'new_v_g_mem': 'new_v', 'new_v_g_ffn': 'new_v', 'new_v_g_final': 'new_v', 'new_v_ev_w_in': 'new_v', 'new_v_ev_a_conv_w': 'new_v', 'new_v_ev_a_conv_b': 'new_v', 'new_v_ev_a_ln_g': 'new_v', 'new_v_ev_a_ln_b': 'new_v', 'new_v_ev_b_conv_w': 'new_v', 'new_v_ev_b_conv_b': 'new_v', 'new_v_ev_w_out': 'new_v', 'new_v_od_w_in': 'new_v', 'new_v_od_c_ln_g': 'new_v', 'new_v_od_c_ln_b': 'new_v', 'new_v_od_w_s': 'new_v', 'new_v_od_b_s': 'new_v', 'new_v_od_w_out': 'new_v', 'new_v_xa_w_q': 'new_v', 'new_v_xa_w_k': 'new_v', 'new_v_xa_w_v': 'new_v', 'new_v_xa_w_o': 'new_v', 'new_v_ffn_w_gate': 'new_v', 'new_v_ffn_w_up': 'new_v', 'new_v_ffn_w_down': 'new_v'}


def _forward(args):
    return _fwd_reference(*[args[k] for k in FWD_PARAMS])


def _output_shape():
    def fwd():
        inp = _fwd_setup_inputs(0)
        return _fwd_reference(*[inp[k] for k in FWD_PARAMS])
    out = _jax.eval_shape(fwd)
    return out.shape, out.dtype

N_MICROBATCH = 1
ADAM_LR = 0.001
ADAM_B1 = 0.9
ADAM_B2 = 0.999
ADAM_EPS = 1e-08
ADAM_WD = 0.01
ADAM_STEP = 10
PER_EXAMPLE_BATCH_AXIS = {'x': 0, 'mem': 0, 'loss_target': 0}
SHARED_INPUTS = []
_WEIGHT_DTYPES = {'g_mix': _jnp.float32, 'g_xattn': _jnp.float32, 'g_mem': _jnp.float32, 'g_ffn': _jnp.float32, 'g_final': _jnp.float32, 'ev_w_in': _jnp.float32, 'ev_a_conv_w': _jnp.float32, 'ev_a_conv_b': _jnp.float32, 'ev_a_ln_g': _jnp.float32, 'ev_a_ln_b': _jnp.float32, 'ev_b_conv_w': _jnp.float32, 'ev_b_conv_b': _jnp.float32, 'ev_w_out': _jnp.float32, 'od_w_in': _jnp.float32, 'od_c_ln_g': _jnp.float32, 'od_c_ln_b': _jnp.float32, 'od_w_s': _jnp.float32, 'od_b_s': _jnp.float32, 'od_w_out': _jnp.float32, 'xa_w_q': _jnp.float32, 'xa_w_k': _jnp.float32, 'xa_w_v': _jnp.float32, 'xa_w_o': _jnp.float32, 'ffn_w_gate': _jnp.float32, 'ffn_w_up': _jnp.float32, 'ffn_w_down': _jnp.float32}
MOMENT_SCALE = {'g_mix': 2.783348e-01, 'g_xattn': 2.285098e-02, 'g_mem': 4.134969e-02, 'g_ffn': 1.663350e-01, 'g_final': 6.431559e+01, 'ev_w_in': 2.100282e-01, 'ev_a_conv_w': 1.669348e-01, 'ev_a_conv_b': 9.380303e-01, 'ev_a_ln_g': 3.701588e-01, 'ev_a_ln_b': 5.330938e-01, 'ev_b_conv_w': 2.792380e-01, 'ev_b_conv_b': 3.031321e-01, 'ev_w_out': 2.438128e-01, 'od_w_in': 1.451364e-01, 'od_c_ln_g': 1.104338e-01, 'od_c_ln_b': 1.068902e-01, 'od_w_s': 1.041357e-01, 'od_b_s': 1.082415e-01, 'od_w_out': 2.791607e-01, 'xa_w_q': 2.306231e-02, 'xa_w_k': 2.318377e-02, 'xa_w_v': 2.818203e-02, 'xa_w_o': 2.833706e-02, 'ffn_w_gate': 6.803652e-02, 'ffn_w_up': 6.892955e-02, 'ffn_w_down': 1.153156e-01}


def _to_microbatches(a, axis):
    t = _jnp.moveaxis(a, axis, 0)
    t = t.reshape((N_MICROBATCH, t.shape[0] // N_MICROBATCH) + t.shape[1:])
    return _jnp.moveaxis(t, 1, axis + 1)


def setup_inputs(seed: int = 0) -> dict:
    inp = _fwd_setup_inputs(seed)
    key = _jax.random.fold_in(_jax.random.key(seed), 7919)
    shape, _ = _output_shape()
    out = dict(inp)
    out["loss_target"] = _jax.random.normal(_jax.random.fold_in(key, 0), shape, _jnp.float32)
    for i, name in enumerate(TWIN_WEIGHTS):
        w = inp[name].astype(_jnp.float32)
        if MOMENT_SCALE is None:
            s = _jnp.sqrt(_jnp.mean(_jnp.square(w)) + 1e-30)
        else:
            s = MOMENT_SCALE[name]
        km, kv = _jax.random.split(_jax.random.fold_in(key, i + 1))
        out[name] = w
        out["m_" + name] = s * _jax.random.normal(km, w.shape, _jnp.float32)
        out["v_" + name] = (s * s) * _jax.random.uniform(kv, w.shape, _jnp.float32, 0.5, 1.5)
    if N_MICROBATCH > 1:
        for name, axis in PER_EXAMPLE_BATCH_AXIS.items():
            out[name] = _to_microbatches(out[name], axis)
    return {'x': out['x'], 'mem': out['mem'], 'g_mix': out['g_mix'], 'g_xattn': out['g_xattn'], 'g_mem': out['g_mem'], 'g_ffn': out['g_ffn'], 'g_final': out['g_final'], 'ev_w_in': out['ev_w_in'], 'ev_a_conv_w': out['ev_a_conv_w'], 'ev_a_conv_b': out['ev_a_conv_b'], 'ev_a_ln_g': out['ev_a_ln_g'], 'ev_a_ln_b': out['ev_a_ln_b'], 'ev_b_conv_w': out['ev_b_conv_w'], 'ev_b_conv_b': out['ev_b_conv_b'], 'ev_w_out': out['ev_w_out'], 'od_w_in': out['od_w_in'], 'od_c_ln_g': out['od_c_ln_g'], 'od_c_ln_b': out['od_c_ln_b'], 'od_w_s': out['od_w_s'], 'od_b_s': out['od_b_s'], 'od_w_out': out['od_w_out'], 'xa_w_q': out['xa_w_q'], 'xa_w_k': out['xa_w_k'], 'xa_w_v': out['xa_w_v'], 'xa_w_o': out['xa_w_o'], 'ffn_w_gate': out['ffn_w_gate'], 'ffn_w_up': out['ffn_w_up'], 'ffn_w_down': out['ffn_w_down'], 'loss_target': out['loss_target'], 'm_g_mix': out['m_g_mix'], 'm_g_xattn': out['m_g_xattn'], 'm_g_mem': out['m_g_mem'], 'm_g_ffn': out['m_g_ffn'], 'm_g_final': out['m_g_final'], 'm_ev_w_in': out['m_ev_w_in'], 'm_ev_a_conv_w': out['m_ev_a_conv_w'], 'm_ev_a_conv_b': out['m_ev_a_conv_b'], 'm_ev_a_ln_g': out['m_ev_a_ln_g'], 'm_ev_a_ln_b': out['m_ev_a_ln_b'], 'm_ev_b_conv_w': out['m_ev_b_conv_w'], 'm_ev_b_conv_b': out['m_ev_b_conv_b'], 'm_ev_w_out': out['m_ev_w_out'], 'm_od_w_in': out['m_od_w_in'], 'm_od_c_ln_g': out['m_od_c_ln_g'], 'm_od_c_ln_b': out['m_od_c_ln_b'], 'm_od_w_s': out['m_od_w_s'], 'm_od_b_s': out['m_od_b_s'], 'm_od_w_out': out['m_od_w_out'], 'm_xa_w_q': out['m_xa_w_q'], 'm_xa_w_k': out['m_xa_w_k'], 'm_xa_w_v': out['m_xa_w_v'], 'm_xa_w_o': out['m_xa_w_o'], 'm_ffn_w_gate': out['m_ffn_w_gate'], 'm_ffn_w_up': out['m_ffn_w_up'], 'm_ffn_w_down': out['m_ffn_w_down'], 'v_g_mix': out['v_g_mix'], 'v_g_xattn': out['v_g_xattn'], 'v_g_mem': out['v_g_mem'], 'v_g_ffn': out['v_g_ffn'], 'v_g_final': out['v_g_final'], 'v_ev_w_in': out['v_ev_w_in'], 'v_ev_a_conv_w': out['v_ev_a_conv_w'], 'v_ev_a_conv_b': out['v_ev_a_conv_b'], 'v_ev_a_ln_g': out['v_ev_a_ln_g'], 'v_ev_a_ln_b': out['v_ev_a_ln_b'], 'v_ev_b_conv_w': out['v_ev_b_conv_w'], 'v_ev_b_conv_b': out['v_ev_b_conv_b'], 'v_ev_w_out': out['v_ev_w_out'], 'v_od_w_in': out['v_od_w_in'], 'v_od_c_ln_g': out['v_od_c_ln_g'], 'v_od_c_ln_b': out['v_od_c_ln_b'], 'v_od_w_s': out['v_od_w_s'], 'v_od_b_s': out['v_od_b_s'], 'v_od_w_out': out['v_od_w_out'], 'v_xa_w_q': out['v_xa_w_q'], 'v_xa_w_k': out['v_xa_w_k'], 'v_xa_w_v': out['v_xa_w_v'], 'v_xa_w_o': out['v_xa_w_o'], 'v_ffn_w_gate': out['v_ffn_w_gate'], 'v_ffn_w_up': out['v_ffn_w_up'], 'v_ffn_w_down': out['v_ffn_w_down']}


def _loss(weights, diff, rest, loss_target):
    with _jax.named_scope("forward"):
        args = {**rest, TWIN_DIFF_INPUT: diff, **{k: w.astype(_WEIGHT_DTYPES[k]) for k, w in weights.items()}}
        y = _forward(args)
    with _jax.named_scope("loss_head"):
        err = _jnp.square(y.astype(_jnp.float32) - loss_target)
        return 0.5 * _jnp.sum(_jnp.mean(err, axis=-1)) if err.ndim else 0.5 * err


def _adamw(w, g, m, v):
    m = ADAM_B1 * m + (1.0 - ADAM_B1) * g
    v = ADAM_B2 * v + (1.0 - ADAM_B2) * _jnp.square(g)
    m_hat = m / (1.0 - ADAM_B1 ** ADAM_STEP)
    v_hat = v / (1.0 - ADAM_B2 ** ADAM_STEP)
    delta = -ADAM_LR * (m_hat / (_jnp.sqrt(v_hat) + ADAM_EPS) + ADAM_WD * w)
    return delta, m, v


def reference(x, mem, g_mix, g_xattn, g_mem, g_ffn, g_final, ev_w_in, ev_a_conv_w, ev_a_conv_b, ev_a_ln_g, ev_a_ln_b, ev_b_conv_w, ev_b_conv_b, ev_w_out, od_w_in, od_c_ln_g, od_c_ln_b, od_w_s, od_b_s, od_w_out, xa_w_q, xa_w_k, xa_w_v, xa_w_o, ffn_w_gate, ffn_w_up, ffn_w_down, loss_target, m_g_mix, m_g_xattn, m_g_mem, m_g_ffn, m_g_final, m_ev_w_in, m_ev_a_conv_w, m_ev_a_conv_b, m_ev_a_ln_g, m_ev_a_ln_b, m_ev_b_conv_w, m_ev_b_conv_b, m_ev_w_out, m_od_w_in, m_od_c_ln_g, m_od_c_ln_b, m_od_w_s, m_od_b_s, m_od_w_out, m_xa_w_q, m_xa_w_k, m_xa_w_v, m_xa_w_o, m_ffn_w_gate, m_ffn_w_up, m_ffn_w_down, v_g_mix, v_g_xattn, v_g_mem, v_g_ffn, v_g_final, v_ev_w_in, v_ev_a_conv_w, v_ev_a_conv_b, v_ev_a_ln_g, v_ev_a_ln_b, v_ev_b_conv_w, v_ev_b_conv_b, v_ev_w_out, v_od_w_in, v_od_c_ln_g, v_od_c_ln_b, v_od_w_s, v_od_b_s, v_od_w_out, v_xa_w_q, v_xa_w_k, v_xa_w_v, v_xa_w_o, v_ffn_w_gate, v_ffn_w_up, v_ffn_w_down):
    given = dict(x=x, mem=mem, g_mix=g_mix, g_xattn=g_xattn, g_mem=g_mem, g_ffn=g_ffn, g_final=g_final, ev_w_in=ev_w_in, ev_a_conv_w=ev_a_conv_w, ev_a_conv_b=ev_a_conv_b, ev_a_ln_g=ev_a_ln_g, ev_a_ln_b=ev_a_ln_b, ev_b_conv_w=ev_b_conv_w, ev_b_conv_b=ev_b_conv_b, ev_w_out=ev_w_out, od_w_in=od_w_in, od_c_ln_g=od_c_ln_g, od_c_ln_b=od_c_ln_b, od_w_s=od_w_s, od_b_s=od_b_s, od_w_out=od_w_out, xa_w_q=xa_w_q, xa_w_k=xa_w_k, xa_w_v=xa_w_v, xa_w_o=xa_w_o, ffn_w_gate=ffn_w_gate, ffn_w_up=ffn_w_up, ffn_w_down=ffn_w_down, loss_target=loss_target, m_g_mix=m_g_mix, m_g_xattn=m_g_xattn, m_g_mem=m_g_mem, m_g_ffn=m_g_ffn, m_g_final=m_g_final, m_ev_w_in=m_ev_w_in, m_ev_a_conv_w=m_ev_a_conv_w, m_ev_a_conv_b=m_ev_a_conv_b, m_ev_a_ln_g=m_ev_a_ln_g, m_ev_a_ln_b=m_ev_a_ln_b, m_ev_b_conv_w=m_ev_b_conv_w, m_ev_b_conv_b=m_ev_b_conv_b, m_ev_w_out=m_ev_w_out, m_od_w_in=m_od_w_in, m_od_c_ln_g=m_od_c_ln_g, m_od_c_ln_b=m_od_c_ln_b, m_od_w_s=m_od_w_s, m_od_b_s=m_od_b_s, m_od_w_out=m_od_w_out, m_xa_w_q=m_xa_w_q, m_xa_w_k=m_xa_w_k, m_xa_w_v=m_xa_w_v, m_xa_w_o=m_xa_w_o, m_ffn_w_gate=m_ffn_w_gate, m_ffn_w_up=m_ffn_w_up, m_ffn_w_down=m_ffn_w_down, v_g_mix=v_g_mix, v_g_xattn=v_g_xattn, v_g_mem=v_g_mem, v_g_ffn=v_g_ffn, v_g_final=v_g_final, v_ev_w_in=v_ev_w_in, v_ev_a_conv_w=v_ev_a_conv_w, v_ev_a_conv_b=v_ev_a_conv_b, v_ev_a_ln_g=v_ev_a_ln_g, v_ev_a_ln_b=v_ev_a_ln_b, v_ev_b_conv_w=v_ev_b_conv_w, v_ev_b_conv_b=v_ev_b_conv_b, v_ev_w_out=v_ev_w_out, v_od_w_in=v_od_w_in, v_od_c_ln_g=v_od_c_ln_g, v_od_c_ln_b=v_od_c_ln_b, v_od_w_s=v_od_w_s, v_od_b_s=v_od_b_s, v_od_w_out=v_od_w_out, v_xa_w_q=v_xa_w_q, v_xa_w_k=v_xa_w_k, v_xa_w_v=v_xa_w_v, v_xa_w_o=v_xa_w_o, v_ffn_w_gate=v_ffn_w_gate, v_ffn_w_up=v_ffn_w_up, v_ffn_w_down=v_ffn_w_down)
    weights = {n: given[n] for n in TWIN_WEIGHTS}
    shared = {n: given[n] for n in SHARED_INPUTS}
    per_example = {n: given[n] for n in ['x', 'mem']}
    grad_fn = _jax.value_and_grad(_loss, argnums=(0, 1))

    def one_microbatch(ex, loss_target):
        ex = dict(ex)
        diff = ex.pop(TWIN_DIFF_INPUT)
        return grad_fn(weights, diff, {**shared, **ex}, loss_target)

    if N_MICROBATCH == 1:
        loss, (grad_w, grad_x) = one_microbatch(per_example, given["loss_target"])
    else:
        def body(carry, xs):
            loss_sum, grad_sum = carry
            l_k, (gw_k, gx_k) = one_microbatch(xs[0], xs[1])
            with _jax.named_scope("update"):
                return (loss_sum + l_k, _jax.tree.map(_jnp.add, grad_sum, gw_k)), gx_k

        init = (_jnp.zeros((), _jnp.float32), _jax.tree.map(_jnp.zeros_like, weights))
        (loss, grad_w), grad_x = _jax.lax.scan(body, init, (per_example, given["loss_target"]))
    with _jax.named_scope("update"):
        delta_w, new_m, new_v = {}, {}, {}
        for n in TWIN_WEIGHTS:
            delta_w[n], new_m[n], new_v[n] = _adamw(weights[n], grad_w[n], given["m_" + n], given["v_" + n])
    return (loss, grad_x, *[grad_w[n] for n in TWIN_WEIGHTS], *[delta_w[n] for n in TWIN_WEIGHTS],
            *[new_m[n] for n in TWIN_WEIGHTS], *[new_v[n] for n in TWIN_WEIGHTS])
```

```python
import jax
import jax.numpy as jnp
from jax import lax
from jax.experimental import pallas as pl
from jax.experimental.pallas import tpu as pltpu

F32, BF16 = jnp.float32, jnp.bfloat16
NDEV = 8
RMS_EPS = 1e-6
LN_EPS = 1e-5
CHUNK = 128
C_GROUPS = 8
XA_HEADS = 4
ADAM_LR, ADAM_B1, ADAM_B2, ADAM_EPS, ADAM_WD, ADAM_STEP = 0.001, 0.9, 0.999, 1e-08, 0.01, 10
HALO = 16
ROW_CHUNK = 32
V7X_VMEM_LIMIT = 56 * 1024 * 1024
MESH = pl.DeviceIdType.MESH

TS_ROW = 512
TS_MM = 1024
TN_MM = 1408
TS_CONV = 512
TS_SGU = 512
TS_ATTN = 512


def _cp(*sem):
    return pltpu.CompilerParams(dimension_semantics=sem, vmem_limit_bytes=V7X_VMEM_LIMIT)


def _pick(n, pref, align):
    for t in range(min(n, pref), 0, -1):
        if n % t == 0 and (t % align == 0 or t == n):
            return t
    return n


def _sigmoid(x):
    return 1.0 / (1.0 + jnp.exp(-x))


def _dot(a, b):
    return jnp.dot(a, b, preferred_element_type=F32)


def _dot_nt(a, b):
    return lax.dot_general(a, b, (((1,), (1,)), ((), ())), preferred_element_type=F32)


def _dot_tn(a, b):
    return lax.dot_general(a, b, (((0,), (0,)), ((), ())), preferred_element_type=F32)


def _rms_fwd(h, g, name):
    S, D = h.shape
    ts = _pick(S, TS_MM, 16)

    def body(h_ref, g_ref, o_ref):
        x = h_ref[...]
        r = lax.rsqrt(jnp.mean(x * x, axis=-1, keepdims=True) + RMS_EPS)
        o_ref[...] = ((x * r) * g_ref[...]).astype(o_ref.dtype)

    return pl.pallas_call(
        body, grid=(S // ts,),
        in_specs=[pl.BlockSpec((ts, D), lambda i: (i, 0)), pl.BlockSpec((1, D), lambda i: (0, 0))],
        out_specs=pl.BlockSpec((ts, D), lambda i: (i, 0)),
        out_shape=jax.ShapeDtypeStruct((S, D), BF16), compiler_params=_cp("parallel"), name=name)(h, g)


def _rms_bwd(dn, h, g, dres, name):
    S, D = h.shape
    ts = _pick(S, TS_ROW, 8)
    has_res = dres is not None

    def body(*refs):
        if has_res:
            dn_ref, h_ref, g_ref, dres_ref, dh_ref, dg_ref = refs
        else:
            dn_ref, h_ref, g_ref, dh_ref, dg_ref = refs
        x = h_ref[...]
        dn_ = dn_ref[...].astype(F32)
        r = lax.rsqrt(jnp.mean(x * x, axis=-1, keepdims=True) + RMS_EPS)
        xr = x * r

        @pl.when(pl.program_id(0) == 0)
        def _():
            dg_ref[...] = jnp.zeros_like(dg_ref)

        dg_ref[...] += jnp.sum(dn_ * xr, axis=0, keepdims=True)
        u = dn_ * g_ref[...]
        dh = r * u - xr * (r * jnp.mean(u * xr, axis=-1, keepdims=True))
        if has_res:
            dh = dh + dres_ref[...]
        dh_ref[...] = dh

    tile = pl.BlockSpec((ts, D), lambda i: (i, 0))
    vec = pl.BlockSpec((1, D), lambda i: (0, 0))
    ins = [dn, h, g] + ([dres] if has_res else [])
    return pl.pallas_call(
        body, grid=(S // ts,),
        in_specs=[tile, tile, vec] + ([tile] if has_res else []),
        out_specs=[tile, vec],
        out_shape=[jax.ShapeDtypeStruct((S, D), F32), jax.ShapeDtypeStruct((1, D), F32)],
        compiler_params=_cp("arbitrary"), name=name)(*ins)


def _loss_bwd(h, g, target, name):
    S, D = h.shape
    ts = _pick(S, TS_ROW, 8)

    def body(h_ref, g_ref, t_ref, loss_ref, dh_ref, dg_ref):
        x = h_ref[...]
        r = lax.rsqrt(jnp.mean(x * x, axis=-1, keepdims=True) + RMS_EPS)
        xr = x * r
        gg = g_ref[...]
        e = xr * gg - t_ref[...]

        @pl.when(pl.program_id(0) == 0)
        def _():
            dg_ref[...] = jnp.zeros_like(dg_ref)
            loss_ref[...] = jnp.zeros_like(loss_ref)

        tile_loss = jnp.sum(jnp.sum(e * e, axis=0, keepdims=True), axis=1, keepdims=True) * (0.5 / D)
        loss_ref[...] += jnp.broadcast_to(tile_loss, loss_ref.shape)
        dy = e * (1.0 / D)
        dg_ref[...] += jnp.sum(dy * xr, axis=0, keepdims=True)
        u = dy * gg
        dh_ref[...] = r * u - xr * (r * jnp.mean(u * xr, axis=-1, keepdims=True))

    tile = pl.BlockSpec((ts, D), lambda i: (i, 0))
    vec = pl.BlockSpec((1, D), lambda i: (0, 0))
    return pl.pallas_call(
        body, grid=(S // ts,),
        in_specs=[tile, vec, tile],
        out_specs=[pl.BlockSpec((1, 128), lambda i: (0, 0)), tile, vec],
        out_shape=[jax.ShapeDtypeStruct((1, 128), F32), jax.ShapeDtypeStruct((S, D), F32),
                   jax.ShapeDtypeStruct((1, D), F32)],
        compiler_params=_cp("arbitrary"), name=name)(h, g, target)


def _mm_nn(a, w, out_dtype, name, res=None, tm=None, tn=None):
    M, K = a.shape
    N = w.shape[1]
    tm = _pick(M, tm or TS_MM, 16)
    tn = _pick(N, tn or TN_MM, 128)
    has_res = res is not None

    def body(*refs):
        if has_res:
            a_ref, w_ref, r_ref, o_ref = refs
        else:
            a_ref, w_ref, o_ref = refs
        acc = _dot(a_ref[...].astype(BF16), w_ref[...])
        if has_res:
            acc = acc + r_ref[...]
        o_ref[...] = acc.astype(o_ref.dtype)

    out_tile = pl.BlockSpec((tm, tn), lambda i, j: (i, j))
    return pl.pallas_call(
        body, grid=(M // tm, N // tn),
        in_specs=[pl.BlockSpec((tm, K), lambda i, j: (i, 0)), pl.BlockSpec((K, tn), lambda i, j: (0, j))]
        + ([out_tile] if has_res else []),
        out_specs=out_tile, out_shape=jax.ShapeDtypeStruct((M, N), out_dtype),
        compiler_params=_cp("parallel", "parallel"), name=name)(*([a, w] + ([res] if has_res else [])))


def _mm_nt(pairs, out_dtype, name, tm=None, tn=None):
    M = pairs[0][0].shape[0]
    N = pairs[0][1].shape[0]
    tm = _pick(M, tm or TS_MM, 16)
    tn = _pick(N, tn or TN_MM, 128)
    npair = len(pairs)

    def body(*refs):
        o_ref = refs[-1]
        acc = None
        for p in range(npair):
            d = _dot_nt(refs[2 * p][...].astype(BF16), refs[2 * p + 1][...])
            acc = d if acc is None else acc + d
        o_ref[...] = acc.astype(o_ref.dtype)

    in_specs, ins = [], []
    for a, w in pairs:
        K = a.shape[1]
        in_specs += [pl.BlockSpec((tm, K), lambda i, j: (i, 0)), pl.BlockSpec((tn, K), lambda i, j: (j, 0))]
        ins += [a, w]
    return pl.pallas_call(
        body, grid=(M // tm, N // tn), in_specs=in_specs,
        out_specs=pl.BlockSpec((tm, tn), lambda i, j: (i, j)), out_shape=jax.ShapeDtypeStruct((M, N), out_dtype),
        compiler_params=_cp("parallel", "parallel"), name=name)(*ins)


def _mm_tn(a, b, name, ts=None, tn=None):
    S, K = a.shape
    N = b.shape[1]
    ts = _pick(S, ts or TS_MM, 16)
    tn = _pick(N, tn or TN_MM, 128)
    nsteps = S // ts

    def body(a_ref, b_ref, o_ref, acc_ref):
        s = pl.program_id(1)

        @pl.when(s == 0)
        def _():
            acc_ref[...] = jnp.zeros_like(acc_ref)

        acc_ref[...] += _dot_tn(a_ref[...].astype(BF16), b_ref[...].astype(BF16))

        @pl.when(s == nsteps - 1)
        def _():
            o_ref[...] = acc_ref[...].astype(o_ref.dtype)

    return pl.pallas_call(
        body, grid=(N // tn, nsteps),
        in_specs=[pl.BlockSpec((ts, K), lambda j, s: (s, 0)), pl.BlockSpec((ts, tn), lambda j, s: (s, j))],
        out_specs=pl.BlockSpec((K, tn), lambda j, s: (0, j)), out_shape=jax.ShapeDtypeStruct((K, N), BF16),
        scratch_shapes=[pltpu.VMEM((K, tn), F32)],
        compiler_params=_cp("parallel", "arbitrary"), name=name)(a, b)


def _ffn_up(n, wg, wu, name):
    S, D = n.shape
    F = wg.shape[1]
    tm = _pick(S, TS_ROW, 16)
    tf = _pick(F, TN_MM, 128)

    def body(n_ref, wg_ref, wu_ref, a_ref, b_ref, hid_ref):
        x = n_ref[...]
        a = _dot(x, wg_ref[...])
        b = _dot(x, wu_ref[...])
        a_ref[...] = a.astype(BF16)
        b_ref[...] = b.astype(BF16)
        hid_ref[...] = (a * _sigmoid(a) * b).astype(BF16)

    wspec = pl.BlockSpec((D, tf), lambda i, j: (0, j))
    ospec = pl.BlockSpec((tm, tf), lambda i, j: (i, j))
    osh = jax.ShapeDtypeStruct((S, F), BF16)
    return pl.pallas_call(
        body, grid=(S // tm, F // tf),
        in_specs=[pl.BlockSpec((tm, D), lambda i, j: (i, 0)), wspec, wspec],
        out_specs=[ospec, ospec, ospec], out_shape=[osh, osh, osh],
        compiler_params=_cp("parallel", "parallel"), name=name)(n, wg, wu)


def _ffn_dhid(dh, wd, a, b, name):
    S, D = dh.shape
    F = wd.shape[0]
    tm = _pick(S, TS_ROW, 16)
    tf = _pick(F, TN_MM, 128)

    def body(dh_ref, wd_ref, a_ref, b_ref, da_ref, db_ref):
        g = _dot_nt(dh_ref[...].astype(BF16), wd_ref[...])
        a_ = a_ref[...].astype(F32)
        b_ = b_ref[...].astype(F32)
        sg = _sigmoid(a_)
        da_ref[...] = (g * b_ * (sg * (1.0 + a_ * (1.0 - sg)))).astype(BF16)
        db_ref[...] = (g * (a_ * sg)).astype(BF16)

    tile = pl.BlockSpec((tm, tf), lambda i, j: (i, j))
    osh = jax.ShapeDtypeStruct((S, F), BF16)
    return pl.pallas_call(
        body, grid=(S // tm, F // tf),
        in_specs=[pl.BlockSpec((tm, D), lambda i, j: (i, 0)), pl.BlockSpec((tf, D), lambda i, j: (j, 0)), tile, tile],
        out_specs=[tile, tile], out_shape=[osh, osh],
        compiler_params=_cp("parallel", "parallel"), name=name)(dh, wd, a, b)


def _softmax_rows(s):
    m = jnp.max(s, axis=-1, keepdims=True)
    p = jnp.exp(s - m)
    return p / jnp.sum(p, axis=-1, keepdims=True)


def _attn_fwd(q, k, v, name):
    S, D = q.shape
    M = k.shape[0]
    hd = D // XA_HEADS
    scale = hd ** -0.5
    ts = _pick(S, TS_ATTN, 16)

    def body(q_ref, k_ref, v_ref, o_ref):
        for h in range(XA_HEADS):
            sl = slice(h * hd, (h + 1) * hd)
            p = _softmax_rows(_dot_nt(q_ref[:, sl], k_ref[:, sl]) * scale)
            o_ref[:, sl] = _dot(p.astype(BF16), v_ref[:, sl]).astype(BF16)

    tile = pl.BlockSpec((ts, D), lambda i: (i, 0))
    memspec = pl.BlockSpec((M, D), lambda i: (0, 0))
    return pl.pallas_call(
        body, grid=(S // ts,), in_specs=[tile, memspec, memspec], out_specs=tile,
        out_shape=jax.ShapeDtypeStruct((S, D), BF16), compiler_params=_cp("parallel"), name=name)(q, k, v)


def _attn_bwd(q, k, v, do, name):
    S, D = q.shape
    M = k.shape[0]
    hd = D // XA_HEADS
    scale = hd ** -0.5
    ts = _pick(S, TS_ATTN, 16)

    def body(q_ref, k_ref, v_ref, do_ref, dq_ref, dk_ref, dv_ref):
        @pl.when(pl.program_id(0) == 0)
        def _():
            dk_ref[...] = jnp.zeros_like(dk_ref)
            dv_ref[...] = jnp.zeros_like(dv_ref)

        for h in range(XA_HEADS):
            sl = slice(h * hd, (h + 1) * hd)
            qh, kh, vh, doh = q_ref[:, sl], k_ref[:, sl], v_ref[:, sl], do_ref[:, sl]
            p = _softmax_rows(_dot_nt(qh, kh) * scale)
            dp = _dot_nt(doh, vh)
            dv_ref[:, sl] += _dot_tn(p.astype(BF16), doh)
            delta = jnp.sum(dp * p, axis=-1, keepdims=True)
            ds = (p * (dp - delta) * scale).astype(BF16)
            dq_ref[:, sl] = _dot(ds, kh).astype(BF16)
            dk_ref[:, sl] += _dot_tn(ds, qh)

    tile = pl.BlockSpec((ts, D), lambda i: (i, 0))
    memspec = pl.BlockSpec((M, D), lambda i: (0, 0))
    return pl.pallas_call(
        body, grid=(S // ts,), in_specs=[tile, memspec, memspec, tile], out_specs=[tile, memspec, memspec],
        out_shape=[jax.ShapeDtypeStruct((S, D), BF16), jax.ShapeDtypeStruct((M, D), F32),
                   jax.ShapeDtypeStruct((M, D), F32)],
        compiler_params=_cp("arbitrary"), name=name)(q, k, v, do)


def _halo_specs(ts, width, col):
    per = ts // HALO

    def prev(i):
        return (jnp.maximum(i * per - 1, 0), col)

    def nxt(i, n_tiles):
        return (jnp.minimum((i + 1) * per, n_tiles * per - 1), col)

    return prev, nxt


def _fill_ext(ext_ref, prev_val, main_val, next_val, first, last, ts):
    ext_ref[pl.ds(0, HALO), :] = jnp.where(first, 0.0, prev_val)
    ext_ref[pl.ds(HALO, ts), :] = main_val
    ext_ref[pl.ds(HALO + ts, HALO), :] = jnp.where(last, 0.0, next_val)


def _conv_fwd(z, wa, ba, lng, lnb, wb, bb, name):
    S = z.shape[0]
    C = z.shape[1] // 5
    KA, KB = wa.shape[0], wb.shape[0]
    pa, pb = KA // 2, KB // 2
    assert pa <= HALO and pb <= HALO
    ts = _pick(S, TS_CONV, ROW_CHUNK)
    nt = S // ts
    rc = ROW_CHUNK
    prev, nxt = _halo_specs(ts, 5 * C, 0)

    def body(z_ref, zp_ref, zn_ref, wa_ref, ba_ref, lng_ref, lnb_ref, wb_ref, bb_ref, ab_ref, ca_ref,
             ga_ext, tb_ext, win_a, win_b):
        i = pl.program_id(0)
        first, last = i == 0, i == nt - 1

        def glu(r):
            return r[:, 0:C] * _sigmoid(r[:, C:2 * C])

        def gcb(r):
            return r[:, 4 * C:5 * C] * r[:, 2 * C:3 * C]

        _fill_ext(ga_ext, glu(zp_ref), glu(z_ref), glu(zn_ref), first, last, ts)
        _fill_ext(tb_ext, gcb(zp_ref), gcb(z_ref), gcb(zn_ref), first, last, ts)

        def chunk(c, carry):
            r0 = pl.multiple_of(c * rc, rc)
            win_a[...] = ga_ext[pl.ds(r0, rc + 2 * HALO), :]
            win_b[...] = tb_ext[pl.ds(r0, rc + 2 * HALO), :]
            acc = jnp.zeros((rc, C), F32)
            for k in range(KA):
                acc = acc + wa_ref[k:k + 1, :] * win_a[pl.ds(HALO - pa + k, rc), :]
            ca = acc + ba_ref[...]
            ca_ref[pl.ds(r0, rc), :] = ca
            mu = jnp.mean(ca, axis=-1, keepdims=True)
            xc = ca - mu
            var = jnp.mean(xc * xc, axis=-1, keepdims=True)
            ln = xc * lax.rsqrt(var + LN_EPS) * lng_ref[...] + lnb_ref[...]
            ab_ref[pl.ds(r0, rc), 0:C] = (ln * _sigmoid(ln)).astype(BF16)
            cb = jnp.zeros((rc, C), F32) + bb_ref[...]
            for k in range(KB):
                cb = cb + wb_ref[k:k + 1, :] * win_b[pl.ds(HALO - pb + k, rc), :]
            ab_ref[pl.ds(r0, rc), C:2 * C] = (z_ref[pl.ds(r0, rc), 3 * C:4 * C] * cb).astype(BF16)
            return carry

        lax.fori_loop(0, ts // rc, chunk, 0)

    zspec = pl.BlockSpec((ts, 5 * C), lambda i: (i, 0))
    zprev = pl.BlockSpec((HALO, 5 * C), prev)
    znext = pl.BlockSpec((HALO, 5 * C), lambda i: nxt(i, nt))

    def full(a):
        return pl.BlockSpec(a.shape, lambda i: (0, 0))

    return pl.pallas_call(
        body, grid=(nt,),
        in_specs=[zspec, zprev, znext, full(wa), full(ba), full(lng), full(lnb), full(wb), full(bb)],
        out_specs=[pl.BlockSpec((ts, 2 * C), lambda i: (i, 0)), pl.BlockSpec((ts, C), lambda i: (i, 0))],
        out_shape=[jax.ShapeDtypeStruct((S, 2 * C), BF16), jax.ShapeDtypeStruct((S, C), F32)],
        scratch_shapes=[pltpu.VMEM((ts + 2 * HALO, C), F32), pltpu.VMEM((ts + 2 * HALO, C), F32),
                        pltpu.VMEM((rc + 2 * HALO, C), F32), pltpu.VMEM((rc + 2 * HALO, C), F32)],
        compiler_params=_cp("parallel"), name=name)(z, z, z, wa, ba, lng, lnb, wb, bb)


def _conv_bwd_ln(dab, ca, lng, lnb, name):
    S, C = ca.shape
    ts = _pick(S, TS_ROW, 8)

    def body(da_ref, ca_ref, lng_ref, lnb_ref, dca_ref, dg_ref, db_ref, dbias_ref):
        @pl.when(pl.program_id(0) == 0)
        def _():
            dg_ref[...] = jnp.zeros_like(dg_ref)
            db_ref[...] = jnp.zeros_like(db_ref)
            dbias_ref[...] = jnp.zeros_like(dbias_ref)

        ca_ = ca_ref[...]
        mu = jnp.mean(ca_, axis=-1, keepdims=True)
        xc = ca_ - mu
        rstd = lax.rsqrt(jnp.mean(xc * xc, axis=-1, keepdims=True) + LN_EPS)
        xh = xc * rstd
        ln = xh * lng_ref[...] + lnb_ref[...]
        sg = _sigmoid(ln)
        dln = da_ref[...].astype(F32) * (sg * (1.0 + ln * (1.0 - sg)))
        dg_ref[...] += jnp.sum(dln * xh, axis=0, keepdims=True)
        db_ref[...] += jnp.sum(dln, axis=0, keepdims=True)
        dxh = dln * lng_ref[...]
        dca = rstd * (dxh - jnp.mean(dxh, axis=-1, keepdims=True) - xh * jnp.mean(dxh * xh, axis=-1, keepdims=True))
        dca_ref[...] = dca
        dbias_ref[...] += jnp.sum(dca, axis=0, keepdims=True)

    tile = pl.BlockSpec((ts, C), lambda i: (i, 0))
    vec = pl.BlockSpec((1, C), lambda i: (0, 0))
    vsh = jax.ShapeDtypeStruct((1, C), F32)
    return pl.pallas_call(
        body, grid=(S // ts,), in_specs=[tile, tile, vec, vec], out_specs=[tile, vec, vec, vec],
        out_shape=[jax.ShapeDtypeStruct((S, C), F32), vsh, vsh, vsh],
        compiler_params=_cp("arbitrary"), name=name)(dab, ca, lng, lnb)


def _conv_bwd(z, dca, dab, wa, wb, bb, name):
    S = z.shape[0]
    C = z.shape[1] // 5
    KA, KB = wa.shape[0], wb.shape[0]
    pa, pb = KA // 2, KB // 2
    ts = _pick(S, TS_CONV, ROW_CHUNK)
    nt = S // ts
    rc = ROW_CHUNK
    prev0, nxt0 = _halo_specs(ts, C, 0)
    prev1, nxt1 = _halo_specs(ts, C, 1)

    def body(z_ref, zp_ref, zn_ref, dca_ref, dcap_ref, dcan_ref, db_ref, dbp_ref, dbn_ref, wa_ref, wb_ref, bb_ref,
             dz_ref, dwa_ref, dwb_ref, dbb_ref,
             ga_ext, dca_ext, tb_ext, dcb_ext, win_ga, win_dca, win_tb, win_dcb, acc_a, acc_b, acc_bias):
        i = pl.program_id(0)
        first, last = i == 0, i == nt - 1

        @pl.when(first)
        def _():
            acc_a[...] = jnp.zeros_like(acc_a)
            acc_b[...] = jnp.zeros_like(acc_b)
            acc_bias[...] = jnp.zeros_like(acc_bias)

        def glu(r):
            return r[:, 0:C] * _sigmoid(r[:, C:2 * C])

        def gcb(r):
            return r[:, 4 * C:5 * C] * r[:, 2 * C:3 * C]

        def dcb(d, r):
            return d[...].astype(F32) * r[:, 3 * C:4 * C]

        _fill_ext(ga_ext, glu(zp_ref), glu(z_ref), glu(zn_ref), first, last, ts)
        _fill_ext(tb_ext, gcb(zp_ref), gcb(z_ref), gcb(zn_ref), first, last, ts)
        _fill_ext(dca_ext, dcap_ref[...], dca_ref[...], dcan_ref[...], first, last, ts)
        _fill_ext(dcb_ext, dcb(dbp_ref, zp_ref), dcb(db_ref, z_ref), dcb(dbn_ref, zn_ref), first, last, ts)

        def fold(x):
            return jnp.sum(x.reshape(rc // 8, 8, C), axis=0)

        def chunk(c, carry):
            r0 = pl.multiple_of(c * rc, rc)
            win_ga[...] = ga_ext[pl.ds(r0, rc + 2 * HALO), :]
            win_dca[...] = dca_ext[pl.ds(r0, rc + 2 * HALO), :]
            win_tb[...] = tb_ext[pl.ds(r0, rc + 2 * HALO), :]
            win_dcb[...] = dcb_ext[pl.ds(r0, rc + 2 * HALO), :]
            dca_c = win_dca[pl.ds(HALO, rc), :]
            dglu = jnp.zeros((rc, C), F32)
            for k in range(KA):
                dglu = dglu + wa_ref[k:k + 1, :] * win_dca[pl.ds(HALO + pa - k, rc), :]
                acc_a[k] += fold(dca_c * win_ga[pl.ds(HALO - pa + k, rc), :])
            val = z_ref[pl.ds(r0, rc), 0:C]
            sg = _sigmoid(z_ref[pl.ds(r0, rc), C:2 * C])
            dz_ref[pl.ds(r0, rc), 0:C] = (dglu * sg).astype(BF16)
            dz_ref[pl.ds(r0, rc), C:2 * C] = (dglu * val * sg * (1.0 - sg)).astype(BF16)
            dcb_c = win_dcb[pl.ds(HALO, rc), :]
            cb = jnp.zeros((rc, C), F32) + bb_ref[...]
            dt = jnp.zeros((rc, C), F32)
            for k in range(KB):
                tb_k = win_tb[pl.ds(HALO - pb + k, rc), :]
                cb = cb + wb_ref[k:k + 1, :] * tb_k
                dt = dt + wb_ref[k:k + 1, :] * win_dcb[pl.ds(HALO + pb - k, rc), :]
                acc_b[k] += fold(dcb_c * tb_k)
            acc_bias[...] += fold(dcb_c)
            db_c = db_ref[pl.ds(r0, rc), :].astype(F32)
            dz_ref[pl.ds(r0, rc), 2 * C:3 * C] = (dt * z_ref[pl.ds(r0, rc), 4 * C:5 * C]).astype(BF16)
            dz_ref[pl.ds(r0, rc), 3 * C:4 * C] = (db_c * cb).astype(BF16)
            dz_ref[pl.ds(r0, rc), 4 * C:5 * C] = (dt * z_ref[pl.ds(r0, rc), 2 * C:3 * C]).astype(BF16)
            return carry

        lax.fori_loop(0, ts // rc, chunk, 0)

        @pl.when(last)
        def _():
            dwa_ref[...] = jnp.sum(acc_a[...], axis=1)
            dwb_ref[...] = jnp.sum(acc_b[...], axis=1)
            dbb_ref[...] = jnp.sum(acc_bias[...], axis=0, keepdims=True)

    zspec = pl.BlockSpec((ts, 5 * C), lambda i: (i, 0))
    zprev = pl.BlockSpec((HALO, 5 * C), prev0)
    znext = pl.BlockSpec((HALO, 5 * C), lambda i: nxt0(i, nt))
    dspec = pl.BlockSpec((ts, C), lambda i: (i, 0))
    dprev = pl.BlockSpec((HALO, C), prev0)
    dnext = pl.BlockSpec((HALO, C), lambda i: nxt0(i, nt))
    bspec = pl.BlockSpec((ts, C), lambda i: (i, 1))
    bprev = pl.BlockSpec((HALO, C), prev1)
    bnext = pl.BlockSpec((HALO, C), lambda i: nxt1(i, nt))

    def full(shape):
        return pl.BlockSpec(shape, lambda i: (0,) * len(shape))

    ext = pltpu.VMEM((ts + 2 * HALO, C), F32)
    win = pltpu.VMEM((rc + 2 * HALO, C), F32)
    return pl.pallas_call(
        body, grid=(nt,),
        in_specs=[zspec, zprev, znext, dspec, dprev, dnext, bspec, bprev, bnext,
                  full(wa.shape), full(wb.shape), full(bb.shape)],
        out_specs=[pl.BlockSpec((ts, 5 * C), lambda i: (i, 0)), full((KA, C)), full((KB, C)), full((1, C))],
        out_shape=[jax.ShapeDtypeStruct((S, 5 * C), BF16), jax.ShapeDtypeStruct((KA, C), F32),
                   jax.ShapeDtypeStruct((KB, C), F32), jax.ShapeDtypeStruct((1, C), F32)],
        scratch_shapes=[ext, ext, ext, ext, win, win, win, win,
                        pltpu.VMEM((KA, 8, C), F32), pltpu.VMEM((KB, 8, C), F32), pltpu.VMEM((8, C), F32)],
        compiler_params=_cp("arbitrary"), name=name)(z, z, z, dca, dca, dca, dab, dab, dab, wa, wb, bb)


_GELU_C = 0.7978845608028654
_GELU_A = 0.044715


def _gelu(x):
    return 0.5 * x * (1.0 + jnp.tanh(_GELU_C * (x + _GELU_A * (x * x * x))))


def _gelu_grad(x):
    t = jnp.tanh(_GELU_C * (x + _GELU_A * (x * x * x)))
    return 0.5 * (1.0 + t) + 0.5 * x * (1.0 - t * t) * (_GELU_C * (1.0 + 3.0 * _GELU_A * x * x))


def _sgu_fwd(zp, lng, lnb, ws, bsb, name):
    S = zp.shape[0]
    D = zp.shape[1] // 2
    G = ws.shape[0]
    gd = D // G
    ts = _pick(S, TS_SGU, CHUNK)
    ncs = ts // CHUNK

    def body(zp_ref, lng_ref, lnb_ref, ws_ref, bsb_ref, y_ref, vb_ref):
        v = _gelu(zp_ref[:, D:2 * D])
        mu = jnp.mean(v, axis=-1, keepdims=True)
        xc = v - mu
        rstd = lax.rsqrt(jnp.mean(xc * xc, axis=-1, keepdims=True) + LN_EPS)
        vb_ref[...] = (xc * rstd * lng_ref[...] + lnb_ref[...]).astype(BF16)
        for c in range(ncs):
            rows = slice(c * CHUNK, (c + 1) * CHUNK)
            for g in range(G):
                cols = slice(g * gd, (g + 1) * gd)
                sv = _dot(ws_ref[g], vb_ref[rows, cols]) + bsb_ref[:, cols]
                y_ref[rows, cols] = (_gelu(zp_ref[rows, cols]) * sv).astype(BF16)

    def full(a):
        return pl.BlockSpec(a.shape, lambda i: (0,) * a.ndim)

    return pl.pallas_call(
        body, grid=(S // ts,),
        in_specs=[pl.BlockSpec((ts, 2 * D), lambda i: (i, 0)), full(lng), full(lnb), full(ws), full(bsb)],
        out_specs=pl.BlockSpec((ts, D), lambda i: (i, 0)), out_shape=jax.ShapeDtypeStruct((S, D), BF16),
        scratch_shapes=[pltpu.VMEM((ts, D), BF16)],
        compiler_params=_cp("parallel"), name=name)(zp, lng, lnb, ws, bsb)


def _sgu_bwd(dy, zp, lng, lnb, ws, wst, bsb, name):
    S = zp.shape[0]
    D = zp.shape[1] // 2
    G = ws.shape[0]
    gd = D // G
    ts = _pick(S, TS_SGU, CHUNK)
    ncs = ts // CHUNK

    def body(dy_ref, zp_ref, lng_ref, lnb_ref, ws_ref, wst_ref, bsb_ref,
             dzp_ref, dws_ref, dbs_ref, dg_ref, db_ref, vb_ref, dvln_ref, acc_bs):
        i = pl.program_id(0)

        @pl.when(i == 0)
        def _():
            dws_ref[...] = jnp.zeros_like(dws_ref)
            acc_bs[...] = jnp.zeros_like(acc_bs)
            dg_ref[...] = jnp.zeros_like(dg_ref)
            db_ref[...] = jnp.zeros_like(db_ref)

        zv = zp_ref[:, D:2 * D]
        v = _gelu(zv)
        mu = jnp.mean(v, axis=-1, keepdims=True)
        xc = v - mu
        rstd = lax.rsqrt(jnp.mean(xc * xc, axis=-1, keepdims=True) + LN_EPS)
        xh = xc * rstd
        vb_ref[...] = (xh * lng_ref[...] + lnb_ref[...]).astype(BF16)
        for c in range(ncs):
            rows = slice(c * CHUNK, (c + 1) * CHUNK)
            for g in range(G):
                cols = slice(g * gd, (g + 1) * gd)
                zu = zp_ref[rows, cols]
                u = _gelu(zu)
                dy_ = dy_ref[rows, cols].astype(F32)
                sv = _dot(ws_ref[g], vb_ref[rows, cols]) + bsb_ref[:, cols]
                dzp_ref[rows, cols] = (dy_ * sv * _gelu_grad(zu)).astype(BF16)
                dsv = dy_ * u
                acc_bs[:, cols] += dsv
                dsvb = dsv.astype(BF16)
                dws_ref[g] += _dot_nt(dsvb, vb_ref[rows, cols])
                dvln_ref[rows, cols] = _dot(wst_ref[g], dsvb)
        dvln = dvln_ref[...]
        dg_ref[...] += jnp.sum(dvln * xh, axis=0, keepdims=True)
        db_ref[...] += jnp.sum(dvln, axis=0, keepdims=True)
        dxh = dvln * lng_ref[...]
        dv = rstd * (dxh - jnp.mean(dxh, axis=-1, keepdims=True) - xh * jnp.mean(dxh * xh, axis=-1, keepdims=True))
        dzp_ref[:, D:2 * D] = (dv * _gelu_grad(zv)).astype(BF16)

        @pl.when(i == pl.num_programs(0) - 1)
        def _():
            dbs_ref[...] = acc_bs[...]

    def full(shape):
        return pl.BlockSpec(shape, lambda i: (0,) * len(shape))

    return pl.pallas_call(
        body, grid=(S // ts,),
        in_specs=[pl.BlockSpec((ts, D), lambda i: (i, 0)), pl.BlockSpec((ts, 2 * D), lambda i: (i, 0)),
                  full(lng.shape), full(lnb.shape), full(ws.shape), full(wst.shape), full(bsb.shape)],
        out_specs=[pl.BlockSpec((ts, 2 * D), lambda i: (i, 0)), full(ws.shape), full(bsb.shape),
                   full((1, D)), full((1, D))],
        out_shape=[jax.ShapeDtypeStruct((S, 2 * D), BF16), jax.ShapeDtypeStruct(ws.shape, F32),
                   jax.ShapeDtypeStruct(bsb.shape, F32), jax.ShapeDtypeStruct((1, D), F32),
                   jax.ShapeDtypeStruct((1, D), F32)],
        scratch_shapes=[pltpu.VMEM((ts, D), BF16), pltpu.VMEM((ts, D), F32),
                        pltpu.VMEM(bsb.shape, F32)],
        compiler_params=_cp("arbitrary"), name=name)(dy, zp, lng, lnb, ws, wst, bsb)


def _group_sum(x, groups, name):
    P, D = x.shape
    gd = D // groups

    def body(x_ref, o_ref):
        for g in range(groups):
            o_ref[:, g:g + 1] = jnp.sum(x_ref[:, g * gd:(g + 1) * gd], axis=1, keepdims=True)

    return pl.pallas_call(body, out_shape=jax.ShapeDtypeStruct((P, groups), F32), name=name)(x)


def _adamw(w, g, m, v, name):
    shape = w.shape
    C = shape[-1]
    R = w.size // C
    tr = _pick(R, 1024, 8)
    bc1 = 1.0 - ADAM_B1 ** ADAM_STEP
    bc2 = 1.0 - ADAM_B2 ** ADAM_STEP

    def body(w_ref, g_ref, m_ref, v_ref, d_ref, nm_ref, nv_ref):
        g_ = g_ref[...]
        nm = ADAM_B1 * m_ref[...] + (1.0 - ADAM_B1) * g_
        nv = ADAM_B2 * v_ref[...] + (1.0 - ADAM_B2) * (g_ * g_)
        nm_ref[...] = nm
        nv_ref[...] = nv
        d_ref[...] = -ADAM_LR * ((nm / bc1) / (jnp.sqrt(nv / bc2) + ADAM_EPS) + ADAM_WD * w_ref[...])

    tile = pl.BlockSpec((tr, C), lambda i: (i, 0))
    sh = jax.ShapeDtypeStruct((R, C), F32)
    outs = pl.pallas_call(
        body, grid=(R // tr,), in_specs=[tile] * 4, out_specs=[tile] * 3, out_shape=[sh] * 3,
        compiler_params=_cp("parallel"), name=name)(*(a.reshape(R, C) for a in (w, g, m, v)))
    return tuple(o.reshape(shape) for o in outs)


_HBM = pl.BlockSpec(memory_space=pltpu.HBM)


def _remote(src, dst, send_sem, recv_sem, to):
    return pltpu.make_async_remote_copy(src_ref=src, dst_ref=dst, send_sem=send_sem, recv_sem=recv_sem,
                                        device_id=to, device_id_type=MESH)


def _all_gather(block, name):
    R, C = block.shape

    def body(x_ref, out_ref, send_sems, recv_sems, local_sem):
        x, y, c = lax.axis_index("x"), lax.axis_index("y"), lax.axis_index("c")
        me, sibling = (x, y, c), (x, y, 1 - c)
        chips = [(1 - x, y), (x, 1 - y), (1 - x, 1 - y)]

        def slot(px, py, pc):
            return out_ref.at[4 * px + 2 * py + pc]

        def copy(k, blk, to, src=None):
            return _remote(slot(*blk) if src is None else src, slot(*blk), send_sems.at[k], recv_sems.at[k], to)

        mine = pltpu.make_async_copy(x_ref, slot(*me), local_sem)
        mine.start()
        first = [copy(0, me, sibling, src=x_ref)]
        first += [copy(1 + j, me, (*chip, c), src=x_ref) for j, chip in enumerate(chips)]
        for cp in first:
            cp.start()
        passed = [copy(4 + j, (*chip, c), sibling) for j, chip in enumerate(chips)]
        for j, chip in enumerate(chips):
            copy(1 + j, (*chip, c), me).wait_recv()
            passed[j].start()
        copy(0, sibling, me).wait_recv()
        for j, chip in enumerate(chips):
            copy(4 + j, (*chip, 1 - c), me).wait_recv()
        for cp in first + passed:
            cp.wait_send()
        mine.wait()

    return pl.pallas_call(
        body, out_shape=jax.ShapeDtypeStruct((NDEV, R, C), block.dtype), in_specs=[_HBM], out_specs=_HBM,
        scratch_shapes=[pltpu.SemaphoreType.DMA((7,)), pltpu.SemaphoreType.DMA((7,)), pltpu.SemaphoreType.DMA],
        name=name)(block)


def _exchange_core(g, name):
    _, R, C = g.shape

    def body(g_ref, out_ref, send_sems, recv_sems):
        x, y, c = lax.axis_index("x"), lax.axis_index("y"), lax.axis_index("c")
        copies = [_remote(g_ref.at[2 * j + (1 - c)], out_ref.at[j], send_sems.at[j], recv_sems.at[j], (x, y, 1 - c))
                  for j in range(4)]
        for cp in copies:
            cp.start()
        for cp in copies:
            cp.wait_recv()
        for cp in copies:
            cp.wait_send()

    return pl.pallas_call(
        body, out_shape=jax.ShapeDtypeStruct((4, R, C), g.dtype), in_specs=[_HBM], out_specs=_HBM,
        scratch_shapes=[pltpu.SemaphoreType.DMA((4,)), pltpu.SemaphoreType.DMA((4,))], name=name)(g)


def _exchange_chips(p, name):
    _, R, C = p.shape

    def body(p_ref, out_ref, send_sems, recv_sems):
        x, y, c = lax.axis_index("x"), lax.axis_index("y"), lax.axis_index("c")
        chips = [(1 - x, y), (x, 1 - y), (1 - x, 1 - y)]
        copies = [_remote(p_ref.at[2 * px + py], out_ref.at[k], send_sems.at[k], recv_sems.at[k], (px, py, c))
                  for k, (px, py) in enumerate(chips)]
        for cp in copies:
            cp.start()
        for cp in copies:
            cp.wait_recv()
        for cp in copies:
            cp.wait_send()

    return pl.pallas_call(
        body, out_shape=jax.ShapeDtypeStruct((3, R, C), p.dtype), in_specs=[_HBM], out_specs=_HBM,
        scratch_shapes=[pltpu.SemaphoreType.DMA((3,)), pltpu.SemaphoreType.DMA((3,))], name=name)(p)


def _pair_sum(g, got, c_idx, name):
    _, R, C = g.shape
    tr = _pick(R, 512, 16)

    def body(c_ref, g_ref, got_ref, o_ref):
        o_ref[...] = (g_ref[...].astype(F32) + got_ref[...].astype(F32)).astype(o_ref.dtype)

    grid_spec = pltpu.PrefetchScalarGridSpec(
        num_scalar_prefetch=1, grid=(4, R // tr),
        in_specs=[pl.BlockSpec((1, tr, C), lambda j, r, c_ref: (2 * j + c_ref[0], r, 0)),
                  pl.BlockSpec((1, tr, C), lambda j, r, c_ref: (j, r, 0))],
        out_specs=pl.BlockSpec((1, tr, C), lambda j, r, c_ref: (j, r, 0)))
    return pl.pallas_call(
        body, grid_spec=grid_spec, out_shape=jax.ShapeDtypeStruct((4, R, C), g.dtype),
        compiler_params=_cp("parallel", "parallel"), name=name)(c_idx, g, got)


def _chip_sum(p, got, chip_idx, name):
    _, R, C = p.shape
    tr = _pick(R, 512, 16)

    def body(i_ref, p_ref, got_ref, o_ref):
        acc = p_ref[0].astype(F32)
        for k in range(3):
            acc = acc + got_ref[k].astype(F32)
        o_ref[...] = acc

    grid_spec = pltpu.PrefetchScalarGridSpec(
        num_scalar_prefetch=1, grid=(R // tr,),
        in_specs=[pl.BlockSpec((1, tr, C), lambda r, i_ref: (i_ref[0], r, 0)),
                  pl.BlockSpec((3, tr, C), lambda r, i_ref: (0, r, 0))],
        out_specs=pl.BlockSpec((tr, C), lambda r, i_ref: (r, 0)))
    return pl.pallas_call(
        body, grid_spec=grid_spec, out_shape=jax.ShapeDtypeStruct((R, C), F32),
        compiler_params=_cp("parallel"), name=name)(chip_idx, p, got)


def _sum_slots(a, name):
    n, R, C = a.shape

    def body(a_ref, o_ref):
        acc = a_ref[0]
        for k in range(1, n):
            acc = acc + a_ref[k]
        o_ref[...] = acc

    return pl.pallas_call(body, out_shape=jax.ShapeDtypeStruct((R, C), F32), name=name)(a)


def _reduce_scatter(g, name):
    x, y, c = lax.axis_index("x"), lax.axis_index("y"), lax.axis_index("c")
    got = _exchange_core(g, name + "_core")
    p = _pair_sum(g, got, jnp.reshape(c, (1,)).astype(jnp.int32), name + "_pair")
    got2 = _exchange_chips(p, name + "_chips")
    return _chip_sum(p, got2, jnp.reshape(2 * x + y, (1,)).astype(jnp.int32), name + "_sum")


def _shard_axis(name):
    return {"ev_w_in": 2, "ev_a_conv_w": 2, "ev_b_conv_w": 2, "ev_w_out": 1, "od_w_in": 2, "od_c_ln_g": 1,
            "od_c_ln_b": 1, "od_w_out": 1, "xa_w_q": 1, "xa_w_k": 1, "xa_w_v": 1, "xa_w_o": 1,
            "ffn_w_gate": 2, "ffn_w_up": 2, "ffn_w_down": 1}[name]


BIG = ["ev_w_in", "ev_w_out", "od_w_in", "od_w_out", "xa_w_q", "xa_w_k", "xa_w_v", "xa_w_o",
       "ffn_w_gate", "ffn_w_up", "ffn_w_down"]
SMALL_SHARDED = ["ev_a_conv_w", "ev_b_conv_w", "od_c_ln_g", "od_c_ln_b"]
REPLICATED = ["g_mix", "g_xattn", "g_mem", "g_ffn", "g_final", "ev_a_conv_b", "ev_a_ln_g", "ev_a_ln_b",
              "ev_b_conv_b", "od_w_s", "od_b_s"]
WEIGHTS = ["g_mix", "g_xattn", "g_mem", "g_ffn", "g_final", "ev_w_in", "ev_a_conv_w", "ev_a_conv_b", "ev_a_ln_g",
           "ev_a_ln_b", "ev_b_conv_w", "ev_b_conv_b", "ev_w_out", "od_w_in", "od_c_ln_g", "od_c_ln_b", "od_w_s",
           "od_b_s", "od_w_out", "xa_w_q", "xa_w_k", "xa_w_v", "xa_w_o", "ffn_w_gate", "ffn_w_up", "ffn_w_down"]


def _full_from_blocks(blocks, axis):
    shard = blocks.shape[1:]
    full = jnp.moveaxis(blocks, 0, axis)
    return full.reshape(shard[:axis] + (NDEV * shard[axis],) + shard[axis + 1:])


def _blocks_from_full(full, axis):
    shp = full.shape
    split = full.reshape(shp[:axis] + (NDEV, shp[axis] // NDEV) + shp[axis + 1:])
    return jnp.moveaxis(split, axis, 0)


def _pad_rows(flat, width, row_align):
    per = width * row_align
    n = -(-flat.shape[0] // per) * per
    return jnp.pad(flat, (0, n - flat.shape[0])).reshape(n // width, width)


def _row(v):
    return v.reshape(1, -1)


def _xattn_fwd(h, mem, g_x, g_m, wq, wk, wv, wo, tag):
    nq = _rms_fwd(h, _row(g_x), f"xa_rms_{tag}")
    mem_n = _rms_fwd(mem, _row(g_m), f"xa_mem_rms_{tag}")
    q = _mm_nn(nq, wq, BF16, f"xa_q_{tag}")
    k = _mm_nn(mem_n, wk, BF16, f"xa_k_{tag}")
    v = _mm_nn(mem_n, wv, BF16, f"xa_v_{tag}")
    o = _attn_fwd(q, k, v, f"xa_attn_{tag}")
    h_new = _mm_nn(o, wo, F32, f"xa_o_{tag}", res=h)
    return h_new, (h, nq, mem_n, q, k, v, o)


def _xattn_bwd(dh_new, saved, mem, g_x, g_m, wq, wk, wv, wo, tag):
    h, nq, mem_n, q, k, v, o = saved
    do = _mm_nt([(dh_new, wo)], BF16, f"xa_do_{tag}")
    d_wo = _mm_tn(o, dh_new, f"xa_dwo_{tag}")
    dq, dk, dv = _attn_bwd(q, k, v, do, f"xa_attn_bwd_{tag}")
    dnq = _mm_nt([(dq, wq)], F32, f"xa_dnq_{tag}")
    d_wq = _mm_tn(nq, dq, f"xa_dwq_{tag}")
    d_wk = _mm_tn(mem_n, dk, f"xa_dwk_{tag}")
    d_wv = _mm_tn(mem_n, dv, f"xa_dwv_{tag}")
    dmem_n = _mm_nt([(dk, wk), (dv, wv)], F32, f"xa_dmem_{tag}")
    _, d_gm = _rms_bwd(dmem_n, mem, _row(g_m), None, f"xa_mem_rms_bwd_{tag}")
    dh, d_gx = _rms_bwd(dnq, h, _row(g_x), dh_new, f"xa_rms_bwd_{tag}")
    return dh, dict(g_xattn=d_gx, g_mem=d_gm, xa_w_q=d_wq, xa_w_k=d_wk, xa_w_v=d_wv, xa_w_o=d_wo)


def _ffn_fwd(h, g_f, wg, wu, wd, tag):
    n = _rms_fwd(h, _row(g_f), f"ffn_rms_{tag}")
    a, b, hid = _ffn_up(n, wg, wu, f"ffn_up_{tag}")
    h_new = _mm_nn(hid, wd, F32, f"ffn_down_{tag}", res=h, tm=512, tn=1024)
    return h_new, (h, n, a, b, hid)


def _ffn_bwd(dh_new, saved, g_f, wg, wu, wd, tag):
    h, n, a, b, hid = saved
    da, db = _ffn_dhid(dh_new, wd, a, b, f"ffn_dhid_{tag}")
    d_wd = _mm_tn(hid, dh_new, f"ffn_dwd_{tag}", tn=512)
    dn = _mm_nt([(da, wg), (db, wu)], F32, f"ffn_dn_{tag}", tm=512, tn=512)
    d_wg = _mm_tn(n, da, f"ffn_dwg_{tag}")
    d_wu = _mm_tn(n, db, f"ffn_dwu_{tag}")
    dh, d_gf = _rms_bwd(dn, h, _row(g_f), dh_new, f"ffn_rms_bwd_{tag}")
    return dh, dict(g_ffn=d_gf, ffn_w_gate=d_wg, ffn_w_up=d_wu, ffn_w_down=d_wd)


def _local_step(x, mem, loss_target, W):
    grads = {}

    h0 = x
    n0 = _rms_fwd(h0, _row(W["g_mix"][0]), "ev_rms")
    z = _mm_nn(n0, W["ev_w_in"][0], F32, "ev_in", tn=1280)
    ab, ca = _conv_fwd(z, W["ev_a_conv_w"][0], W["ev_a_conv_b"], W["ev_a_ln_g"], W["ev_a_ln_b"],
                       W["ev_b_conv_w"][0], W["ev_b_conv_b"], "ev_conv")
    h1 = _mm_nn(ab, W["ev_w_out"][0], F32, "ev_out", res=h0, tn=1024)
    h2, xa0 = _xattn_fwd(h1, mem, W["g_xattn"][0], W["g_mem"][0], W["xa_w_q"][0], W["xa_w_k"][0], W["xa_w_v"][0],
                         W["xa_w_o"][0], "l0")
    h3, ff0 = _ffn_fwd(h2, W["g_ffn"][0], W["ffn_w_gate"][0], W["ffn_w_up"][0], W["ffn_w_down"][0], "l0")

    n3 = _rms_fwd(h3, _row(W["g_mix"][1]), "od_rms")
    zp = _mm_nn(n3, W["od_w_in"][0], F32, "od_in", tn=1024)
    D = x.shape[1]
    ws = W["od_w_s"][0].astype(BF16)
    wst = jnp.swapaxes(ws, 1, 2)
    bsb = jnp.repeat(jnp.transpose(W["od_b_s"][0]), D // C_GROUPS, axis=1)
    y_sgu = _sgu_fwd(zp, W["od_c_ln_g"], W["od_c_ln_b"], ws, bsb, "od_sgu")
    h4 = _mm_nn(y_sgu, W["od_w_out"][0], F32, "od_out", res=h3, tn=1024)
    h5, xa1 = _xattn_fwd(h4, mem, W["g_xattn"][1], W["g_mem"][1], W["xa_w_q"][1], W["xa_w_k"][1], W["xa_w_v"][1],
                         W["xa_w_o"][1], "l1")
    h6, ff1 = _ffn_fwd(h5, W["g_ffn"][1], W["ffn_w_gate"][1], W["ffn_w_up"][1], W["ffn_w_down"][1], "l1")

    loss_row, dh6, d_gfinal = _loss_bwd(h6, _row(W["g_final"]), loss_target, "loss")
    grads["g_final"] = d_gfinal.reshape(-1)

    dh5, g_ff1 = _ffn_bwd(dh6, ff1, W["g_ffn"][1], W["ffn_w_gate"][1], W["ffn_w_up"][1], W["ffn_w_down"][1], "l1")
    dh4, g_xa1 = _xattn_bwd(dh5, xa1, mem, W["g_xattn"][1], W["g_mem"][1], W["xa_w_q"][1], W["xa_w_k"][1],
                            W["xa_w_v"][1], W["xa_w_o"][1], "l1")
    dy_sgu = _mm_nt([(dh4, W["od_w_out"][0])], F32, "od_dy", tn=1024)
    grads["od_w_out"] = _mm_tn(y_sgu, dh4, "od_dwout", tn=1024)[None]
    dzp, d_ws, d_bsb, d_clng, d_clnb = _sgu_bwd(dy_sgu, zp, W["od_c_ln_g"], W["od_c_ln_b"], ws, wst, bsb, "od_sgu_bwd")
    grads["od_w_s"] = d_ws[None]
    grads["od_b_s"] = jnp.transpose(_group_sum(d_bsb, C_GROUPS, "od_dbs"))[None]
    grads["od_c_ln_g"], grads["od_c_ln_b"] = d_clng, d_clnb
    dn3 = _mm_nt([(dzp, W["od_w_in"][0])], F32, "od_dn", tn=1024)
    grads["od_w_in"] = _mm_tn(n3, dzp, "od_dwin", tn=1024)[None]
    dh3, d_gmix1 = _rms_bwd(dn3, h3, _row(W["g_mix"][1]), dh4, "od_rms_bwd")

    dh2, g_ff0 = _ffn_bwd(dh3, ff0, W["g_ffn"][0], W["ffn_w_gate"][0], W["ffn_w_up"][0], W["ffn_w_down"][0], "l0")
    dh1, g_xa0 = _xattn_bwd(dh2, xa0, mem, W["g_xattn"][0], W["g_mem"][0], W["xa_w_q"][0], W["xa_w_k"][0],
                            W["xa_w_v"][0], W["xa_w_o"][0], "l0")
    dab = _mm_nt([(dh1, W["ev_w_out"][0])], F32, "ev_dab", tn=1024)
    grads["ev_w_out"] = _mm_tn(ab, dh1, "ev_dwout", tn=1024)[None]
    dca, d_lng, d_lnb, d_ba = _conv_bwd_ln(dab, ca, W["ev_a_ln_g"], W["ev_a_ln_b"], "ev_conv_bwd_ln")
    dz, d_wa, d_wb, d_bb = _conv_bwd(z, dca, dab, W["ev_a_conv_w"][0], W["ev_b_conv_w"][0], W["ev_b_conv_b"],
                                     "ev_conv_bwd")
    grads.update(ev_a_ln_g=d_lng, ev_a_ln_b=d_lnb, ev_a_conv_b=d_ba, ev_b_conv_b=d_bb,
                 ev_a_conv_w=d_wa[None], ev_b_conv_w=d_wb[None])
    dn0 = _mm_nt([(dz, W["ev_w_in"][0])], F32, "ev_dn", tn=1024)
    grads["ev_w_in"] = _mm_tn(n0, dz, "ev_dwin", tn=1280)[None]
    grad_x, d_gmix0 = _rms_bwd(dn0, h0, _row(W["g_mix"][0]), dh1, "ev_rms_bwd")

    grads["g_mix"] = jnp.concatenate([d_gmix0, d_gmix1], axis=0)
    for key in ("g_xattn", "g_mem", "g_ffn"):
        src = {"g_xattn": (g_xa0, g_xa1), "g_mem": (g_xa0, g_xa1), "g_ffn": (g_ff0, g_ff1)}[key]
        grads[key] = jnp.concatenate([src[0][key], src[1][key]], axis=0)
    for key in ("xa_w_q", "xa_w_k", "xa_w_v", "xa_w_o"):
        grads[key] = jnp.stack([g_xa0[key], g_xa1[key]])
    for key in ("ffn_w_gate", "ffn_w_up", "ffn_w_down"):
        grads[key] = jnp.stack([g_ff0[key], g_ff1[key]])
    return loss_row, grad_x, grads


def kernel(x, mem, g_mix, g_xattn, g_mem, g_ffn, g_final, ev_w_in, ev_a_conv_w, ev_a_conv_b, ev_a_ln_g, ev_a_ln_b, ev_b_conv_w, ev_b_conv_b, ev_w_out, od_w_in, od_c_ln_g, od_c_ln_b, od_w_s, od_b_s, od_w_out, xa_w_q, xa_w_k, xa_w_v, xa_w_o, ffn_w_gate, ffn_w_up, ffn_w_down, loss_target, m_g_mix, m_g_xattn, m_g_mem, m_g_ffn, m_g_final, m_ev_w_in, m_ev_a_conv_w, m_ev_a_conv_b, m_ev_a_ln_g, m_ev_a_ln_b, m_ev_b_conv_w, m_ev_b_conv_b, m_ev_w_out, m_od_w_in, m_od_c_ln_g, m_od_c_ln_b, m_od_w_s, m_od_b_s, m_od_w_out, m_xa_w_q, m_xa_w_k, m_xa_w_v, m_xa_w_o, m_ffn_w_gate, m_ffn_w_up, m_ffn_w_down, v_g_mix, v_g_xattn, v_g_mem, v_g_ffn, v_g_final, v_ev_w_in, v_ev_a_conv_w, v_ev_a_conv_b, v_ev_a_ln_g, v_ev_a_ln_b, v_ev_b_conv_w, v_ev_b_conv_b, v_ev_w_out, v_od_w_in, v_od_c_ln_g, v_od_c_ln_b, v_od_w_s, v_od_b_s, v_od_w_out, v_xa_w_q, v_xa_w_k, v_xa_w_v, v_xa_w_o, v_ffn_w_gate, v_ffn_w_up, v_ffn_w_down):
    local = dict(g_mix=g_mix, g_xattn=g_xattn, g_mem=g_mem, g_ffn=g_ffn, g_final=g_final, ev_w_in=ev_w_in, ev_a_conv_w=ev_a_conv_w, ev_a_conv_b=ev_a_conv_b, ev_a_ln_g=ev_a_ln_g, ev_a_ln_b=ev_a_ln_b, ev_b_conv_w=ev_b_conv_w, ev_b_conv_b=ev_b_conv_b, ev_w_out=ev_w_out, od_w_in=od_w_in, od_c_ln_g=od_c_ln_g, od_c_ln_b=od_c_ln_b, od_w_s=od_w_s, od_b_s=od_b_s, od_w_out=od_w_out, xa_w_q=xa_w_q, xa_w_k=xa_w_k, xa_w_v=xa_w_v, xa_w_o=xa_w_o, ffn_w_gate=ffn_w_gate, ffn_w_up=ffn_w_up, ffn_w_down=ffn_w_down)
    mom = dict(g_mix=m_g_mix, g_xattn=m_g_xattn, g_mem=m_g_mem, g_ffn=m_g_ffn, g_final=m_g_final, ev_w_in=m_ev_w_in, ev_a_conv_w=m_ev_a_conv_w, ev_a_conv_b=m_ev_a_conv_b, ev_a_ln_g=m_ev_a_ln_g, ev_a_ln_b=m_ev_a_ln_b, ev_b_conv_w=m_ev_b_conv_w, ev_b_conv_b=m_ev_b_conv_b, ev_w_out=m_ev_w_out, od_w_in=m_od_w_in, od_c_ln_g=m_od_c_ln_g, od_c_ln_b=m_od_c_ln_b, od_w_s=m_od_w_s, od_b_s=m_od_b_s, od_w_out=m_od_w_out, xa_w_q=m_xa_w_q, xa_w_k=m_xa_w_k, xa_w_v=m_xa_w_v, xa_w_o=m_xa_w_o, ffn_w_gate=m_ffn_w_gate, ffn_w_up=m_ffn_w_up, ffn_w_down=m_ffn_w_down)
    vel = dict(g_mix=v_g_mix, g_xattn=v_g_xattn, g_mem=v_g_mem, g_ffn=v_g_ffn, g_final=v_g_final, ev_w_in=v_ev_w_in, ev_a_conv_w=v_ev_a_conv_w, ev_a_conv_b=v_ev_a_conv_b, ev_a_ln_g=v_ev_a_ln_g, ev_a_ln_b=v_ev_a_ln_b, ev_b_conv_w=v_ev_b_conv_w, ev_b_conv_b=v_ev_b_conv_b, ev_w_out=v_ev_w_out, od_w_in=v_od_w_in, od_c_ln_g=v_od_c_ln_g, od_c_ln_b=v_od_c_ln_b, od_w_s=v_od_w_s, od_b_s=v_od_b_s, od_w_out=v_od_w_out, xa_w_q=v_xa_w_q, xa_w_k=v_xa_w_k, xa_w_v=v_xa_w_v, xa_w_o=v_xa_w_o, ffn_w_gate=v_ffn_w_gate, ffn_w_up=v_ffn_w_up, ffn_w_down=v_ffn_w_down)
    D = x.shape[-1]
    dev = 4 * lax.axis_index("x") + 2 * lax.axis_index("y") + lax.axis_index("c")

    big_rows = [local[n].size // D for n in BIG]
    big_block = jnp.concatenate([local[n].astype(BF16).reshape(-1, D) for n in BIG], axis=0)
    big_all = _all_gather(big_block, "ag_weights")
    small_sizes = [local[n].size for n in SMALL_SHARDED]
    small_block = _pad_rows(jnp.concatenate([local[n].reshape(-1) for n in SMALL_SHARDED]), 128, 8)
    small_all = _all_gather(small_block, "ag_small").reshape(NDEV, -1)

    W = {n: local[n] for n in REPLICATED}
    r0 = 0
    for n, nr in zip(BIG, big_rows):
        blocks = big_all[:, r0:r0 + nr].reshape((NDEV,) + local[n].shape)
        W[n] = _full_from_blocks(blocks, _shard_axis(n))
        r0 += nr
    o0 = 0
    for n, sz in zip(SMALL_SHARDED, small_sizes):
        blocks = small_all[:, o0:o0 + sz].reshape((NDEV,) + local[n].shape)
        W[n] = _full_from_blocks(blocks, _shard_axis(n))
        o0 += sz

    loss_row, grad_x, grads = _local_step(x[0], mem[0], loss_target[0], W)

    g_big = jnp.concatenate(
        [_blocks_from_full(grads[n], _shard_axis(n)).astype(BF16).reshape(NDEV, -1, D) for n in BIG], axis=1)
    g_red = _reduce_scatter(g_big, "rs_grads")
    gsh = {}
    r0 = 0
    for n, nr in zip(BIG, big_rows):
        gsh[n] = g_red[r0:r0 + nr].reshape(local[n].shape)
        r0 += nr

    rest = REPLICATED + SMALL_SHARDED
    rest_full_shapes = [grads[n].shape for n in rest]
    g_rest = _pad_rows(jnp.concatenate([grads[n].astype(F32).reshape(-1) for n in rest]), D, 8)
    g_rest = _sum_slots(_all_gather(g_rest, "ag_small_grads"), "sum_small_grads").reshape(-1)
    o0 = 0
    for n, shp in zip(rest, rest_full_shapes):
        sz = 1
        for s in shp:
            sz *= s
        full = g_rest[o0:o0 + sz].reshape(shp)
        o0 += sz
        if n in SMALL_SHARDED:
            full = lax.dynamic_index_in_dim(_blocks_from_full(full, _shard_axis(n)), dev, 0, keepdims=False)
        gsh[n] = full.reshape(local[n].shape)

    delta, new_m, new_v = {}, {}, {}
    for n in WEIGHTS:
        delta[n], new_m[n], new_v[n] = _adamw(local[n], gsh[n], mom[n], vel[n], f"adamw_{n}")

    loss = lax.psum(loss_row[0, 0], ("x", "y", "c"))
    return (loss, grad_x[None], *[gsh[n] for n in WEIGHTS], *[delta[n] for n in WEIGHTS],
            *[new_m[n] for n in WEIGHTS], *[new_v[n] for n in WEIGHTS])
```

```python
import jax
import jax.numpy as jnp
from jax import lax
from jax.experimental import pallas as pl
from jax.experimental.pallas import tpu as pltpu

F32, BF16 = jnp.float32, jnp.bfloat16
NDEV = 8
RMS_EPS = 1e-6
LN_EPS = 1e-5
CHUNK = 128
C_GROUPS = 8
XA_HEADS = 4
ADAM_LR, ADAM_B1, ADAM_B2, ADAM_EPS, ADAM_WD, ADAM_STEP = 0.001, 0.9, 0.999, 1e-08, 0.01, 10
HALO = 16
ROW_CHUNK = 32
V7X_VMEM_LIMIT = 56 * 1024 * 1024
MESH = pl.DeviceIdType.MESH

TS_ROW = 512
TS_MM = 1024
TN_MM = 1408
TS_FFN = 256
TS_CONV = 512
TS_SGU = 512
TS_ATTN = 512


def _cp(*sem):
    return pltpu.CompilerParams(dimension_semantics=sem, vmem_limit_bytes=V7X_VMEM_LIMIT)


def _pick(n, pref, align):
    for t in range(min(n, pref), 0, -1):
        if n % t == 0 and (t % align == 0 or t == n):
            return t
    return n


def _sigmoid(x):
    return 1.0 / (1.0 + jnp.exp(-x))


def _dot(a, b):
    return jnp.dot(a, b, preferred_element_type=F32)


def _dot_nt(a, b):
    return lax.dot_general(a, b, (((1,), (1,)), ((), ())), preferred_element_type=F32)


def _dot_tn(a, b):
    return lax.dot_general(a, b, (((0,), (0,)), ((), ())), preferred_element_type=F32)


def _rms_fwd(h, g, name):
    S, D = h.shape
    ts = _pick(S, TS_MM, 16)

    def body(h_ref, g_ref, o_ref):
        x = h_ref[...]
        r = lax.rsqrt(jnp.mean(x * x, axis=-1, keepdims=True) + RMS_EPS)
        o_ref[...] = ((x * r) * g_ref[...]).astype(o_ref.dtype)

    return pl.pallas_call(
        body, grid=(S // ts,),
        in_specs=[pl.BlockSpec((ts, D), lambda i: (i, 0)), pl.BlockSpec((1, D), lambda i: (0, 0))],
        out_specs=pl.BlockSpec((ts, D), lambda i: (i, 0)),
        out_shape=jax.ShapeDtypeStruct((S, D), BF16), compiler_params=_cp("parallel"), name=name)(h, g)


def _rms_bwd(dn, h, g, dres, name):
    S, D = h.shape
    ts = _pick(S, TS_ROW, 8)
    has_res = dres is not None

    def body(*refs):
        if has_res:
            dn_ref, h_ref, g_ref, dres_ref, dh_ref, dg_ref = refs
        else:
            dn_ref, h_ref, g_ref, dh_ref, dg_ref = refs
        x = h_ref[...]
        dn_ = dn_ref[...].astype(F32)
        r = lax.rsqrt(jnp.mean(x * x, axis=-1, keepdims=True) + RMS_EPS)
        xr = x * r

        @pl.when(pl.program_id(0) == 0)
        def _():
            dg_ref[...] = jnp.zeros_like(dg_ref)

        dg_ref[...] += jnp.sum(dn_ * xr, axis=0, keepdims=True)
        u = dn_ * g_ref[...]
        dh = r * u - xr * (r * jnp.mean(u * xr, axis=-1, keepdims=True))
        if has_res:
            dh = dh + dres_ref[...]
        dh_ref[...] = dh

    tile = pl.BlockSpec((ts, D), lambda i: (i, 0))
    vec = pl.BlockSpec((1, D), lambda i: (0, 0))
    ins = [dn, h, g] + ([dres] if has_res else [])
    return pl.pallas_call(
        body, grid=(S // ts,),
        in_specs=[tile, tile, vec] + ([tile] if has_res else []),
        out_specs=[tile, vec],
        out_shape=[jax.ShapeDtypeStruct((S, D), F32), jax.ShapeDtypeStruct((1, D), F32)],
        compiler_params=_cp("arbitrary"), name=name)(*ins)


def _loss_bwd(h, g, target, name):
    S, D = h.shape
    ts = _pick(S, TS_ROW, 8)

    def body(h_ref, g_ref, t_ref, loss_ref, dh_ref, dg_ref):
        x = h_ref[...]
        r = lax.rsqrt(jnp.mean(x * x, axis=-1, keepdims=True) + RMS_EPS)
        xr = x * r
        gg = g_ref[...]
        e = xr * gg - t_ref[...]

        @pl.when(pl.program_id(0) == 0)
        def _():
            dg_ref[...] = jnp.zeros_like(dg_ref)
            loss_ref[...] = jnp.zeros_like(loss_ref)

        tile_loss = jnp.sum(jnp.sum(e * e, axis=0, keepdims=True), axis=1, keepdims=True) * (0.5 / D)
        loss_ref[...] += jnp.broadcast_to(tile_loss, loss_ref.shape)
        dy = e * (1.0 / D)
        dg_ref[...] += jnp.sum(dy * xr, axis=0, keepdims=True)
        u = dy * gg
        dh_ref[...] = r * u - xr * (r * jnp.mean(u * xr, axis=-1, keepdims=True))

    tile = pl.BlockSpec((ts, D), lambda i: (i, 0))
    vec = pl.BlockSpec((1, D), lambda i: (0, 0))
    return pl.pallas_call(
        body, grid=(S // ts,),
        in_specs=[tile, vec, tile],
        out_specs=[pl.BlockSpec((1, 128), lambda i: (0, 0)), tile, vec],
        out_shape=[jax.ShapeDtypeStruct((1, 128), F32), jax.ShapeDtypeStruct((S, D), F32),
                   jax.ShapeDtypeStruct((1, D), F32)],
        compiler_params=_cp("arbitrary"), name=name)(h, g, target)


def _mm(pairs, name, out_dtype=F32, res=None, rms_g=None, rms_bwd=None, tm=None, tn=None):
    M = pairs[0][0].shape[0]
    N = pairs[0][1].shape[1 if pairs[0][2] == "nn" else 0]
    whole_rows = rms_g is not None or rms_bwd is not None
    tm = _pick(M, tm or TS_MM, 16)
    tn = N if whole_rows else _pick(N, tn or TN_MM, 128)
    npair = len(pairs)
    modes = [p[2] for p in pairs]

    def body(*refs):
        acc = None
        for p in range(npair):
            a_ = refs[2 * p][...].astype(BF16)
            d = _dot(a_, refs[2 * p + 1][...]) if modes[p] == "nn" else _dot_nt(a_, refs[2 * p + 1][...])
            acc = d if acc is None else acc + d
        rest = refs[2 * npair:]
        if res is not None:
            acc = acc + rest[0][...]
            rest = rest[1:]
        if rms_bwd is not None:
            h_ref, g_ref, dres_ref, dh_ref, dg_ref = rest
            x = h_ref[...]
            r = lax.rsqrt(jnp.mean(x * x, axis=-1, keepdims=True) + RMS_EPS)
            xr = x * r

            @pl.when(pl.program_id(0) == 0)
            def _():
                dg_ref[...] = jnp.zeros_like(dg_ref)

            dg_ref[...] += jnp.sum(acc * xr, axis=0, keepdims=True)
            u = acc * g_ref[...]
            dh_ref[...] = r * u - xr * (r * jnp.mean(u * xr, axis=-1, keepdims=True)) + dres_ref[...]
        elif rms_g is not None:
            g_ref, o_ref, n_ref = rest
            o_ref[...] = acc
            r = lax.rsqrt(jnp.mean(acc * acc, axis=-1, keepdims=True) + RMS_EPS)
            n_ref[...] = ((acc * r) * g_ref[...]).astype(BF16)
        else:
            rest[0][...] = acc.astype(rest[0].dtype)

    in_specs, ins = [], []
    for a, w, mode in pairs:
        K = a.shape[1]
        in_specs.append(pl.BlockSpec((tm, K), lambda i, j: (i, 0)))
        in_specs.append(pl.BlockSpec((K, tn), lambda i, j: (0, j)) if mode == "nn"
                        else pl.BlockSpec((tn, K), lambda i, j: (j, 0)))
        ins += [a, w]
    tile = pl.BlockSpec((tm, tn), lambda i, j: (i, j))
    vec = pl.BlockSpec((1, tn), lambda i, j: (0, j))
    if res is not None:
        in_specs.append(tile)
        ins.append(res)
    sem = ("parallel", "parallel")
    if rms_bwd is not None:
        in_specs += [tile, vec, tile]
        ins += list(rms_bwd)
        out_specs = [tile, vec]
        out_shape = [jax.ShapeDtypeStruct((M, N), F32), jax.ShapeDtypeStruct((1, N), F32)]
        sem = ("arbitrary", "arbitrary")
    elif rms_g is not None:
        in_specs.append(vec)
        ins.append(rms_g)
        out_specs = [tile, tile]
        out_shape = [jax.ShapeDtypeStruct((M, N), F32), jax.ShapeDtypeStruct((M, N), BF16)]
    else:
        out_specs = tile
        out_shape = jax.ShapeDtypeStruct((M, N), out_dtype)
    return pl.pallas_call(
        body, grid=(M // tm, N // tn), in_specs=in_specs, out_specs=out_specs, out_shape=out_shape,
        compiler_params=_cp(*sem), name=name)(*ins)


def _mm_tn(a, b, name, ts=None, tn=None):
    S, K = a.shape
    N = b.shape[1]
    ts = _pick(S, ts or TS_MM, 16)
    tn = _pick(N, tn or TN_MM, 128)
    nsteps = S // ts

    def body(a_ref, b_ref, o_ref, acc_ref):
        s = pl.program_id(1)

        @pl.when(s == 0)
        def _():
            acc_ref[...] = jnp.zeros_like(acc_ref)

        acc_ref[...] += _dot_tn(a_ref[...].astype(BF16), b_ref[...].astype(BF16))

        @pl.when(s == nsteps - 1)
        def _():
            o_ref[...] = acc_ref[...].astype(o_ref.dtype)

    return pl.pallas_call(
        body, grid=(N // tn, nsteps),
        in_specs=[pl.BlockSpec((ts, K), lambda j, s: (s, 0)), pl.BlockSpec((ts, tn), lambda j, s: (s, j))],
        out_specs=pl.BlockSpec((K, tn), lambda j, s: (0, j)), out_shape=jax.ShapeDtypeStruct((K, N), BF16),
        scratch_shapes=[pltpu.VMEM((K, tn), F32)],
        compiler_params=_cp("parallel", "arbitrary"), name=name)(a, b)


def _col_chunk(n):
    return 256 if n % 256 == 0 else 128


def _ffn_up(n, wgt, wut, name):
    S, D = n.shape
    F = wgt.shape[0]
    tm = _pick(S, TS_FFN, 16)
    ce = _col_chunk(F)

    def body(n_ref, wg_ref, wu_ref, a_ref, b_ref, hid_ref):
        x = n_ref[...]
        for c0 in range(0, F, ce):
            a = _dot_nt(x, wg_ref[c0:c0 + ce, :])
            b = _dot_nt(x, wu_ref[c0:c0 + ce, :])
            a_ref[:, c0:c0 + ce] = a.astype(BF16)
            b_ref[:, c0:c0 + ce] = b.astype(BF16)
            hid_ref[:, c0:c0 + ce] = (a * _sigmoid(a) * b).astype(BF16)

    wspec = pl.BlockSpec((F, D), lambda i: (0, 0))
    ospec = pl.BlockSpec((tm, F), lambda i: (i, 0))
    osh = jax.ShapeDtypeStruct((S, F), BF16)
    return pl.pallas_call(
        body, grid=(S // tm,),
        in_specs=[pl.BlockSpec((tm, D), lambda i: (i, 0)), wspec, wspec],
        out_specs=[ospec, ospec, ospec], out_shape=[osh, osh, osh],
        compiler_params=_cp("parallel"), name=name)(n, wgt, wut)


def _ffn_dhid(dh, wd, a, b, name):
    S, D = dh.shape
    F = wd.shape[0]
    tm = _pick(S, TS_FFN, 16)
    ce = _col_chunk(F)

    def body(dh_ref, wd_ref, a_ref, b_ref, da_ref, db_ref):
        x = dh_ref[...].astype(BF16)
        for c0 in range(0, F, ce):
            g = _dot_nt(x, wd_ref[c0:c0 + ce, :])
            a_ = a_ref[:, c0:c0 + ce].astype(F32)
            b_ = b_ref[:, c0:c0 + ce].astype(F32)
            sg = _sigmoid(a_)
            da_ref[:, c0:c0 + ce] = (g * b_ * (sg * (1.0 + a_ * (1.0 - sg)))).astype(BF16)
            db_ref[:, c0:c0 + ce] = (g * (a_ * sg)).astype(BF16)

    tile = pl.BlockSpec((tm, F), lambda i: (i, 0))
    osh = jax.ShapeDtypeStruct((S, F), BF16)
    return pl.pallas_call(
        body, grid=(S // tm,),
        in_specs=[pl.BlockSpec((tm, D), lambda i: (i, 0)), pl.BlockSpec((F, D), lambda i: (0, 0)), tile, tile],
        out_specs=[tile, tile], out_shape=[osh, osh],
        compiler_params=_cp("parallel"), name=name)(dh, wd, a, b)


def _softmax_rows(s):
    m = jnp.max(s, axis=-1, keepdims=True)
    p = jnp.exp(s - m)
    return p / jnp.sum(p, axis=-1, keepdims=True)


def _attn_fwd(q, k, v, name):
    S, D = q.shape
    M = k.shape[0]
    hd = D // XA_HEADS
    scale = hd ** -0.5
    ts = _pick(S, TS_ATTN, 16)

    def body(q_ref, k_ref, v_ref, o_ref):
        for h in range(XA_HEADS):
            sl = slice(h * hd, (h + 1) * hd)
            p = _softmax_rows(_dot_nt(q_ref[:, sl], k_ref[:, sl]) * scale)
            o_ref[:, sl] = _dot(p.astype(BF16), v_ref[:, sl]).astype(BF16)

    tile = pl.BlockSpec((ts, D), lambda i: (i, 0))
    memspec = pl.BlockSpec((M, D), lambda i: (0, 0))
    return pl.pallas_call(
        body, grid=(S // ts,), in_specs=[tile, memspec, memspec], out_specs=tile,
        out_shape=jax.ShapeDtypeStruct((S, D), BF16), compiler_params=_cp("parallel"), name=name)(q, k, v)


def _attn_bwd(q, k, v, do, name):
    S, D = q.shape
    M = k.shape[0]
    hd = D // XA_HEADS
    scale = hd ** -0.5
    ts = _pick(S, TS_ATTN, 16)

    def body(q_ref, k_ref, v_ref, do_ref, dq_ref, dk_ref, dv_ref):
        @pl.when(pl.program_id(0) == 0)
        def _():
            dk_ref[...] = jnp.zeros_like(dk_ref)
            dv_ref[...] = jnp.zeros_like(dv_ref)

        for h in range(XA_HEADS):
            sl = slice(h * hd, (h + 1) * hd)
            qh, kh, vh, doh = q_ref[:, sl], k_ref[:, sl], v_ref[:, sl], do_ref[:, sl]
            p = _softmax_rows(_dot_nt(qh, kh) * scale)
            dp = _dot_nt(doh, vh)
            dv_ref[:, sl] += _dot_tn(p.astype(BF16), doh)
            delta = jnp.sum(dp * p, axis=-1, keepdims=True)
            ds = (p * (dp - delta) * scale).astype(BF16)
            dq_ref[:, sl] = _dot(ds, kh).astype(BF16)
            dk_ref[:, sl] += _dot_tn(ds, qh)

    tile = pl.BlockSpec((ts, D), lambda i: (i, 0))
    memspec = pl.BlockSpec((M, D), lambda i: (0, 0))
    return pl.pallas_call(
        body, grid=(S // ts,), in_specs=[tile, memspec, memspec, tile], out_specs=[tile, memspec, memspec],
        out_shape=[jax.ShapeDtypeStruct((S, D), BF16), jax.ShapeDtypeStruct((M, D), F32),
                   jax.ShapeDtypeStruct((M, D), F32)],
        compiler_params=_cp("arbitrary"), name=name)(q, k, v, do)


def _halo_specs(ts, width, col):
    per = ts // HALO

    def prev(i):
        return (jnp.maximum(i * per - 1, 0), col)

    def nxt(i, n_tiles):
        return (jnp.minimum((i + 1) * per, n_tiles * per - 1), col)

    return prev, nxt


def _fill_ext(ext_ref, prev_val, main_val, next_val, first, last, ts):
    ext_ref[pl.ds(0, HALO), :] = jnp.where(first, 0.0, prev_val)
    ext_ref[pl.ds(HALO, ts), :] = main_val
    ext_ref[pl.ds(HALO + ts, HALO), :] = jnp.where(last, 0.0, next_val)


def _conv_fwd(z, wa, ba, lng, lnb, wb, bb, name):
    S = z.shape[0]
    C = z.shape[1] // 5
    KA, KB = wa.shape[0], wb.shape[0]
    pa, pb = KA // 2, KB // 2
    assert pa <= HALO and pb <= HALO
    ts = _pick(S, TS_CONV, ROW_CHUNK)
    nt = S // ts
    rc = ROW_CHUNK
    prev, nxt = _halo_specs(ts, 5 * C, 0)

    def body(z_ref, zp_ref, zn_ref, wa_ref, ba_ref, lng_ref, lnb_ref, wb_ref, bb_ref, ab_ref, ca_ref,
             ga_ext, tb_ext, win_a, win_b):
        i = pl.program_id(0)
        first, last = i == 0, i == nt - 1

        def glu(r):
            return r[:, 0:C] * _sigmoid(r[:, C:2 * C])

        def gcb(r):
            return r[:, 4 * C:5 * C] * r[:, 2 * C:3 * C]

        _fill_ext(ga_ext, glu(zp_ref), glu(z_ref), glu(zn_ref), first, last, ts)
        _fill_ext(tb_ext, gcb(zp_ref), gcb(z_ref), gcb(zn_ref), first, last, ts)

        def chunk(c, carry):
            r0 = pl.multiple_of(c * rc, rc)
            win_a[...] = ga_ext[pl.ds(r0, rc + 2 * HALO), :]
            win_b[...] = tb_ext[pl.ds(r0, rc + 2 * HALO), :]
            acc = jnp.zeros((rc, C), F32)
            for k in range(KA):
                acc = acc + wa_ref[k:k + 1, :] * win_a[pl.ds(HALO - pa + k, rc), :]
            ca = acc + ba_ref[...]
            ca_ref[pl.ds(r0, rc), :] = ca
            mu = jnp.mean(ca, axis=-1, keepdims=True)
            xc = ca - mu
            var = jnp.mean(xc * xc, axis=-1, keepdims=True)
            ln = xc * lax.rsqrt(var + LN_EPS) * lng_ref[...] + lnb_ref[...]
            ab_ref[pl.ds(r0, rc), 0:C] = (ln * _sigmoid(ln)).astype(BF16)
            cb = jnp.zeros((rc, C), F32) + bb_ref[...]
            for k in range(KB):
                cb = cb + wb_ref[k:k + 1, :] * win_b[pl.ds(HALO - pb + k, rc), :]
            ab_ref[pl.ds(r0, rc), C:2 * C] = (z_ref[pl.ds(r0, rc), 3 * C:4 * C] * cb).astype(BF16)
            return carry

        lax.fori_loop(0, ts // rc, chunk, 0)

    zspec = pl.BlockSpec((ts, 5 * C), lambda i: (i, 0))
    zprev = pl.BlockSpec((HALO, 5 * C), prev)
    znext = pl.BlockSpec((HALO, 5 * C), lambda i: nxt(i, nt))

    def full(a):
        return pl.BlockSpec(a.shape, lambda i: (0, 0))

    return pl.pallas_call(
        body, grid=(nt,),
        in_specs=[zspec, zprev, znext, full(wa), full(ba), full(lng), full(lnb), full(wb), full(bb)],
        out_specs=[pl.BlockSpec((ts, 2 * C), lambda i: (i, 0)), pl.BlockSpec((ts, C), lambda i: (i, 0))],
        out_shape=[jax.ShapeDtypeStruct((S, 2 * C), BF16), jax.ShapeDtypeStruct((S, C), F32)],
        scratch_shapes=[pltpu.VMEM((ts + 2 * HALO, C), F32), pltpu.VMEM((ts + 2 * HALO, C), F32),
                        pltpu.VMEM((rc + 2 * HALO, C), F32), pltpu.VMEM((rc + 2 * HALO, C), F32)],
        compiler_params=_cp("parallel"), name=name)(z, z, z, wa, ba, lng, lnb, wb, bb)


def _conv_bwd_ln(dab, ca, lng, lnb, name):
    S, C = ca.shape
    ts = _pick(S, TS_ROW, 8)

    def body(da_ref, ca_ref, lng_ref, lnb_ref, dca_ref, dg_ref, db_ref, dbias_ref):
        @pl.when(pl.program_id(0) == 0)
        def _():
            dg_ref[...] = jnp.zeros_like(dg_ref)
            db_ref[...] = jnp.zeros_like(db_ref)
            dbias_ref[...] = jnp.zeros_like(dbias_ref)

        ca_ = ca_ref[...]
        mu = jnp.mean(ca_, axis=-1, keepdims=True)
        xc = ca_ - mu
        rstd = lax.rsqrt(jnp.mean(xc * xc, axis=-1, keepdims=True) + LN_EPS)
        xh = xc * rstd
        ln = xh * lng_ref[...] + lnb_ref[...]
        sg = _sigmoid(ln)
        dln = da_ref[...].astype(F32) * (sg * (1.0 + ln * (1.0 - sg)))
        dg_ref[...] += jnp.sum(dln * xh, axis=0, keepdims=True)
        db_ref[...] += jnp.sum(dln, axis=0, keepdims=True)
        dxh = dln * lng_ref[...]
        dca = rstd * (dxh - jnp.mean(dxh, axis=-1, keepdims=True) - xh * jnp.mean(dxh * xh, axis=-1, keepdims=True))
        dca_ref[...] = dca
        dbias_ref[...] += jnp.sum(dca, axis=0, keepdims=True)

    tile = pl.BlockSpec((ts, C), lambda i: (i, 0))
    vec = pl.BlockSpec((1, C), lambda i: (0, 0))
    vsh = jax.ShapeDtypeStruct((1, C), F32)
    return pl.pallas_call(
        body, grid=(S // ts,), in_specs=[tile, tile, vec, vec], out_specs=[tile, vec, vec, vec],
        out_shape=[jax.ShapeDtypeStruct((S, C), F32), vsh, vsh, vsh],
        compiler_params=_cp("arbitrary"), name=name)(dab, ca, lng, lnb)


def _conv_bwd(z, dca, dab, wa, wb, bb, name):
    S = z.shape[0]
    C = z.shape[1] // 5
    KA, KB = wa.shape[0], wb.shape[0]
    pa, pb = KA // 2, KB // 2
    ts = _pick(S, TS_CONV, ROW_CHUNK)
    nt = S // ts
    rc = ROW_CHUNK
    prev0, nxt0 = _halo_specs(ts, C, 0)
    prev1, nxt1 = _halo_specs(ts, C, 1)

    def body(z_ref, zp_ref, zn_ref, dca_ref, dcap_ref, dcan_ref, db_ref, dbp_ref, dbn_ref, wa_ref, wb_ref, bb_ref,
             dz_ref, dwa_ref, dwb_ref, dbb_ref,
             ga_ext, dca_ext, tb_ext, dcb_ext, win_ga, win_dca, win_tb, win_dcb, acc_a, acc_b, acc_bias):
        i = pl.program_id(0)
        first, last = i == 0, i == nt - 1

        @pl.when(first)
        def _():
            acc_a[...] = jnp.zeros_like(acc_a)
            acc_b[...] = jnp.zeros_like(acc_b)
            acc_bias[...] = jnp.zeros_like(acc_bias)

        def glu(r):
            return r[:, 0:C] * _sigmoid(r[:, C:2 * C])

        def gcb(r):
            return r[:, 4 * C:5 * C] * r[:, 2 * C:3 * C]

        def dcb(d, r):
            return d[...].astype(F32) * r[:, 3 * C:4 * C]

        _fill_ext(ga_ext, glu(zp_ref), glu(z_ref), glu(zn_ref), first, last, ts)
        _fill_ext(tb_ext, gcb(zp_ref), gcb(z_ref), gcb(zn_ref), first, last, ts)
        _fill_ext(dca_ext, dcap_ref[...], dca_ref[...], dcan_ref[...], first, last, ts)
        _fill_ext(dcb_ext, dcb(dbp_ref, zp_ref), dcb(db_ref, z_ref), dcb(dbn_ref, zn_ref), first, last, ts)

        def fold(x):
            return jnp.sum(x.reshape(rc // 8, 8, C), axis=0)

        def chunk(c, carry):
            r0 = pl.multiple_of(c * rc, rc)
            win_ga[...] = ga_ext[pl.ds(r0, rc + 2 * HALO), :]
            win_dca[...] = dca_ext[pl.ds(r0, rc + 2 * HALO), :]
            win_tb[...] = tb_ext[pl.ds(r0, rc + 2 * HALO), :]
            win_dcb[...] = dcb_ext[pl.ds(r0, rc + 2 * HALO), :]
            dca_c = win_dca[pl.ds(HALO, rc), :]
            dglu = jnp.zeros((rc, C), F32)
            for k in range(KA):
                dglu = dglu + wa_ref[k:k + 1, :] * win_dca[pl.ds(HALO + pa - k, rc), :]
                acc_a[k] += fold(dca_c * win_ga[pl.ds(HALO - pa + k, rc), :])
            val = z_ref[pl.ds(r0, rc), 0:C]
            sg = _sigmoid(z_ref[pl.ds(r0, rc), C:2 * C])
            dz_ref[pl.ds(r0, rc), 0:C] = (dglu * sg).astype(BF16)
            dz_ref[pl.ds(r0, rc), C:2 * C] = (dglu * val * sg * (1.0 - sg)).astype(BF16)
            dcb_c = win_dcb[pl.ds(HALO, rc), :]
            cb = jnp.zeros((rc, C), F32) + bb_ref[...]
            dt = jnp.zeros((rc, C), F32)
            for k in range(KB):
                tb_k = win_tb[pl.ds(HALO - pb + k, rc), :]
                cb = cb + wb_ref[k:k + 1, :] * tb_k
                dt = dt + wb_ref[k:k + 1, :] * win_dcb[pl.ds(HALO + pb - k, rc), :]
                acc_b[k] += fold(dcb_c * tb_k)
            acc_bias[...] += fold(dcb_c)
            db_c = db_ref[pl.ds(r0, rc), :].astype(F32)
            dz_ref[pl.ds(r0, rc), 2 * C:3 * C] = (dt * z_ref[pl.ds(r0, rc), 4 * C:5 * C]).astype(BF16)
            dz_ref[pl.ds(r0, rc), 3 * C:4 * C] = (db_c * cb).astype(BF16)
            dz_ref[pl.ds(r0, rc), 4 * C:5 * C] = (dt * z_ref[pl.ds(r0, rc), 2 * C:3 * C]).astype(BF16)
            return carry

        lax.fori_loop(0, ts // rc, chunk, 0)

        @pl.when(last)
        def _():
            dwa_ref[...] = jnp.sum(acc_a[...], axis=1)
            dwb_ref[...] = jnp.sum(acc_b[...], axis=1)
            dbb_ref[...] = jnp.sum(acc_bias[...], axis=0, keepdims=True)

    zspec = pl.BlockSpec((ts, 5 * C), lambda i: (i, 0))
    zprev = pl.BlockSpec((HALO, 5 * C), prev0)
    znext = pl.BlockSpec((HALO, 5 * C), lambda i: nxt0(i, nt))
    dspec = pl.BlockSpec((ts, C), lambda i: (i, 0))
    dprev = pl.BlockSpec((HALO, C), prev0)
    dnext = pl.BlockSpec((HALO, C), lambda i: nxt0(i, nt))
    bspec = pl.BlockSpec((ts, C), lambda i: (i, 1))
    bprev = pl.BlockSpec((HALO, C), prev1)
    bnext = pl.BlockSpec((HALO, C), lambda i: nxt1(i, nt))

    def full(shape):
        return pl.BlockSpec(shape, lambda i: (0,) * len(shape))

    ext = pltpu.VMEM((ts + 2 * HALO, C), F32)
    win = pltpu.VMEM((rc + 2 * HALO, C), F32)
    return pl.pallas_call(
        body, grid=(nt,),
        in_specs=[zspec, zprev, znext, dspec, dprev, dnext, bspec, bprev, bnext,
                  full(wa.shape), full(wb.shape), full(bb.shape)],
        out_specs=[pl.BlockSpec((ts, 5 * C), lambda i: (i, 0)), full((KA, C)), full((KB, C)), full((1, C))],
        out_shape=[jax.ShapeDtypeStruct((S, 5 * C), BF16), jax.ShapeDtypeStruct((KA, C), F32),
                   jax.ShapeDtypeStruct((KB, C), F32), jax.ShapeDtypeStruct((1, C), F32)],
        scratch_shapes=[ext, ext, ext, ext, win, win, win, win,
                        pltpu.VMEM((KA, 8, C), F32), pltpu.VMEM((KB, 8, C), F32), pltpu.VMEM((8, C), F32)],
        compiler_params=_cp("arbitrary"), name=name)(z, z, z, dca, dca, dca, dab, dab, dab, wa, wb, bb)


_GELU_C = 0.7978845608028654
_GELU_A = 0.044715


def _gelu(x):
    return 0.5 * x * (1.0 + jnp.tanh(_GELU_C * (x + _GELU_A * (x * x * x))))


def _gelu_grad(x):
    t = jnp.tanh(_GELU_C * (x + _GELU_A * (x * x * x)))
    return 0.5 * (1.0 + t) + 0.5 * x * (1.0 - t * t) * (_GELU_C * (1.0 + 3.0 * _GELU_A * x * x))


def _sgu_fwd(zp, lng, lnb, ws, bsb, name):
    S = zp.shape[0]
    D = zp.shape[1] // 2
    G = ws.shape[0]
    gd = D // G
    ts = _pick(S, TS_SGU, CHUNK)
    ncs = ts // CHUNK

    def body(zp_ref, lng_ref, lnb_ref, ws_ref, bsb_ref, y_ref, vb_ref):
        v = _gelu(zp_ref[:, D:2 * D])
        mu = jnp.mean(v, axis=-1, keepdims=True)
        xc = v - mu
        rstd = lax.rsqrt(jnp.mean(xc * xc, axis=-1, keepdims=True) + LN_EPS)
        vb_ref[...] = (xc * rstd * lng_ref[...] + lnb_ref[...]).astype(BF16)
        for c in range(ncs):
            rows = slice(c * CHUNK, (c + 1) * CHUNK)
            for g in range(G):
                cols = slice(g * gd, (g + 1) * gd)
                sv = _dot(ws_ref[g], vb_ref[rows, cols]) + bsb_ref[:, cols]
                y_ref[rows, cols] = (_gelu(zp_ref[rows, cols]) * sv).astype(BF16)

    def full(a):
        return pl.BlockSpec(a.shape, lambda i: (0,) * a.ndim)

    return pl.pallas_call(
        body, grid=(S // ts,),
        in_specs=[pl.BlockSpec((ts, 2 * D), lambda i: (i, 0)), full(lng), full(lnb), full(ws), full(bsb)],
        out_specs=pl.BlockSpec((ts, D), lambda i: (i, 0)), out_shape=jax.ShapeDtypeStruct((S, D), BF16),
        scratch_shapes=[pltpu.VMEM((ts, D), BF16)],
        compiler_params=_cp("parallel"), name=name)(zp, lng, lnb, ws, bsb)


def _sgu_bwd(dy, zp, lng, lnb, ws, wst, bsb, name):
    S = zp.shape[0]
    D = zp.shape[1] // 2
    G = ws.shape[0]
    gd = D // G
    ts = _pick(S, TS_SGU, CHUNK)
    ncs = ts // CHUNK

    def body(dy_ref, zp_ref, lng_ref, lnb_ref, ws_ref, wst_ref, bsb_ref,
             dzp_ref, dws_ref, dbs_ref, dg_ref, db_ref, vb_ref, dvln_ref, acc_bs):
        i = pl.program_id(0)

        @pl.when(i == 0)
        def _():
            dws_ref[...] = jnp.zeros_like(dws_ref)
            acc_bs[...] = jnp.zeros_like(acc_bs)
            dg_ref[...] = jnp.zeros_like(dg_ref)
            db_ref[...] = jnp.zeros_like(db_ref)

        zv = zp_ref[:, D:2 * D]
        v = _gelu(zv)
        mu = jnp.mean(v, axis=-1, keepdims=True)
        xc = v - mu
        rstd = lax.rsqrt(jnp.mean(xc * xc, axis=-1, keepdims=True) + LN_EPS)
        xh = xc * rstd
        vb_ref[...] = (xh * lng_ref[...] + lnb_ref[...]).astype(BF16)
        for c in range(ncs):
            rows = slice(c * CHUNK, (c + 1) * CHUNK)
            for g in range(G):
                cols = slice(g * gd, (g + 1) * gd)
                zu = zp_ref[rows, cols]
                u = _gelu(zu)
                dy_ = dy_ref[rows, cols].astype(F32)
                sv = _dot(ws_ref[g], vb_ref[rows, cols]) + bsb_ref[:, cols]
                dzp_ref[rows, cols] = (dy_ * sv * _gelu_grad(zu)).astype(BF16)
                dsv = dy_ * u
                acc_bs[:, cols] += dsv
                dsvb = dsv.astype(BF16)
                dws_ref[g] += _dot_nt(dsvb, vb_ref[rows, cols])
                dvln_ref[rows, cols] = _dot(wst_ref[g], dsvb)
        dvln = dvln_ref[...]
        dg_ref[...] += jnp.sum(dvln * xh, axis=0, keepdims=True)
        db_ref[...] += jnp.sum(dvln, axis=0, keepdims=True)
        dxh = dvln * lng_ref[...]
        dv = rstd * (dxh - jnp.mean(dxh, axis=-1, keepdims=True) - xh * jnp.mean(dxh * xh, axis=-1, keepdims=True))
        dzp_ref[:, D:2 * D] = (dv * _gelu_grad(zv)).astype(BF16)

        @pl.when(i == pl.num_programs(0) - 1)
        def _():
            dbs_ref[...] = acc_bs[...]

    def full(shape):
        return pl.BlockSpec(shape, lambda i: (0,) * len(shape))

    return pl.pallas_call(
        body, grid=(S // ts,),
        in_specs=[pl.BlockSpec((ts, D), lambda i: (i, 0)), pl.BlockSpec((ts, 2 * D), lambda i: (i, 0)),
                  full(lng.shape), full(lnb.shape), full(ws.shape), full(wst.shape), full(bsb.shape)],
        out_specs=[pl.BlockSpec((ts, 2 * D), lambda i: (i, 0)), full(ws.shape), full(bsb.shape),
                   full((1, D)), full((1, D))],
        out_shape=[jax.ShapeDtypeStruct((S, 2 * D), BF16), jax.ShapeDtypeStruct(ws.shape, F32),
                   jax.ShapeDtypeStruct(bsb.shape, F32), jax.ShapeDtypeStruct((1, D), F32),
                   jax.ShapeDtypeStruct((1, D), F32)],
        scratch_shapes=[pltpu.VMEM((ts, D), BF16), pltpu.VMEM((ts, D), F32),
                        pltpu.VMEM(bsb.shape, F32)],
        compiler_params=_cp("arbitrary"), name=name)(dy, zp, lng, lnb, ws, wst, bsb)


def _group_sum(x, groups, name):
    P, D = x.shape
    gd = D // groups

    def body(x_ref, o_ref):
        for g in range(groups):
            o_ref[:, g:g + 1] = jnp.sum(x_ref[:, g * gd:(g + 1) * gd], axis=1, keepdims=True)

    return pl.pallas_call(body, out_shape=jax.ShapeDtypeStruct((P, groups), F32), name=name)(x)


def _adamw(w, g, m, v, name):
    shape = w.shape
    C = shape[-1]
    R = w.size // C
    tr = _pick(R, 1024, 8)
    bc1 = 1.0 - ADAM_B1 ** ADAM_STEP
    bc2 = 1.0 - ADAM_B2 ** ADAM_STEP

    def body(w_ref, g_ref, m_ref, v_ref, d_ref, nm_ref, nv_ref):
        g_ = g_ref[...]
        nm = ADAM_B1 * m_ref[...] + (1.0 - ADAM_B1) * g_
        nv = ADAM_B2 * v_ref[...] + (1.0 - ADAM_B2) * (g_ * g_)
        nm_ref[...] = nm
        nv_ref[...] = nv
        d_ref[...] = -ADAM_LR * ((nm / bc1) / (jnp.sqrt(nv / bc2) + ADAM_EPS) + ADAM_WD * w_ref[...])

    tile = pl.BlockSpec((tr, C), lambda i: (i, 0))
    sh = jax.ShapeDtypeStruct((R, C), F32)
    outs = pl.pallas_call(
        body, grid=(R // tr,), in_specs=[tile] * 4, out_specs=[tile] * 3, out_shape=[sh] * 3,
        compiler_params=_cp("parallel"), name=name)(*(a.reshape(R, C) for a in (w, g, m, v)))
    return tuple(o.reshape(shape) for o in outs)


_HBM = pl.BlockSpec(memory_space=pltpu.HBM)


def _remote(src, dst, send_sem, recv_sem, to):
    return pltpu.make_async_remote_copy(src_ref=src, dst_ref=dst, send_sem=send_sem, recv_sem=recv_sem,
                                        device_id=to, device_id_type=MESH)


def _all_gather(block, name):
    R, C = block.shape

    def body(x_ref, out_ref, send_sems, recv_sems, local_sem):
        x, y, c = lax.axis_index("x"), lax.axis_index("y"), lax.axis_index("c")
        me, sibling = (x, y, c), (x, y, 1 - c)
        chips = [(1 - x, y), (x, 1 - y), (1 - x, 1 - y)]

        def slot(px, py, pc):
            return out_ref.at[4 * px + 2 * py + pc]

        def copy(k, blk, to, src=None):
            return _remote(slot(*blk) if src is None else src, slot(*blk), send_sems.at[k], recv_sems.at[k], to)

        mine = pltpu.make_async_copy(x_ref, slot(*me), local_sem)
        mine.start()
        first = [copy(0, me, sibling, src=x_ref)]
        first += [copy(1 + j, me, (*chip, c), src=x_ref) for j, chip in enumerate(chips)]
        for cp in first:
            cp.start()
        passed = [copy(4 + j, (*chip, c), sibling) for j, chip in enumerate(chips)]
        for j, chip in enumerate(chips):
            copy(1 + j, (*chip, c), me).wait_recv()
            passed[j].start()
        copy(0, sibling, me).wait_recv()
        for j, chip in enumerate(chips):
            copy(4 + j, (*chip, 1 - c), me).wait_recv()
        for cp in first + passed:
            cp.wait_send()
        mine.wait()

    return pl.pallas_call(
        body, out_shape=jax.ShapeDtypeStruct((NDEV, R, C), block.dtype), in_specs=[_HBM], out_specs=_HBM,
        scratch_shapes=[pltpu.SemaphoreType.DMA((7,)), pltpu.SemaphoreType.DMA((7,)), pltpu.SemaphoreType.DMA],
        name=name)(block)


def _all_gather_weights(pack, rows, name):
    C = pack.shape[1]
    nw = len(rows)
    starts = [sum(rows[:w]) for w in range(nw)]

    def body(pack_ref, *rest):
        outs = rest[:nw]
        send_sems, recv_sems, local_sem = rest[nw:]
        x, y, c = lax.axis_index("x"), lax.axis_index("y"), lax.axis_index("c")
        me, sibling = (x, y, c), (x, y, 1 - c)
        chips = [(1 - x, y), (x, 1 - y), (1 - x, 1 - y)]

        def block(w, px, py, pc):
            return outs[w].at[pl.ds((4 * px + 2 * py + pc) * rows[w], rows[w])]

        def mine(w):
            return pack_ref.at[pl.ds(starts[w], rows[w])]

        def all_of(k):
            return _remote(pack_ref, pack_ref, send_sems.at[k], recv_sems.at[k], me)

        for w in range(nw):
            pltpu.make_async_copy(mine(w), block(w, *me), local_sem).start()
        for k, to in enumerate([sibling] + [(*chip, c) for chip in chips]):
            for w in range(nw):
                _remote(mine(w), block(w, *me), send_sems.at[k], recv_sems.at[k], to).start()
        for j, chip in enumerate(chips):
            all_of(1 + j).wait_recv()
            for w in range(nw):
                _remote(block(w, *chip, c), block(w, *chip, c), send_sems.at[4 + j], recv_sems.at[4 + j], sibling).start()
        all_of(0).wait_recv()
        for j in range(3):
            all_of(4 + j).wait_recv()
        for k in range(7):
            all_of(k).wait_send()
        pltpu.make_async_copy(pack_ref, pack_ref, local_sem).wait()

    return pl.pallas_call(
        body, out_shape=[jax.ShapeDtypeStruct((NDEV * r, C), pack.dtype) for r in rows],
        in_specs=[_HBM], out_specs=[_HBM] * nw,
        scratch_shapes=[pltpu.SemaphoreType.DMA((7,)), pltpu.SemaphoreType.DMA((7,)), pltpu.SemaphoreType.DMA],
        name=name)(pack)


def _exchange_core(g, name):
    _, R, C = g.shape

    def body(g_ref, out_ref, send_sems, recv_sems):
        x, y, c = lax.axis_index("x"), lax.axis_index("y"), lax.axis_index("c")
        copies = [_remote(g_ref.at[2 * j + (1 - c)], out_ref.at[j], send_sems.at[j], recv_sems.at[j], (x, y, 1 - c))
                  for j in range(4)]
        for cp in copies:
            cp.start()
        for cp in copies:
            cp.wait_recv()
        for cp in copies:
            cp.wait_send()

    return pl.pallas_call(
        body, out_shape=jax.ShapeDtypeStruct((4, R, C), g.dtype), in_specs=[_HBM], out_specs=_HBM,
        scratch_shapes=[pltpu.SemaphoreType.DMA((4,)), pltpu.SemaphoreType.DMA((4,))], name=name)(g)


def _exchange_chips(p, name):
    _, R, C = p.shape

    def body(p_ref, out_ref, send_sems, recv_sems):
        x, y, c = lax.axis_index("x"), lax.axis_index("y"), lax.axis_index("c")
        chips = [(1 - x, y), (x, 1 - y), (1 - x, 1 - y)]
        copies = [_remote(p_ref.at[2 * px + py], out_ref.at[k], send_sems.at[k], recv_sems.at[k], (px, py, c))
                  for k, (px, py) in enumerate(chips)]
        for cp in copies:
            cp.start()
        for cp in copies:
            cp.wait_recv()
        for cp in copies:
            cp.wait_send()

    return pl.pallas_call(
        body, out_shape=jax.ShapeDtypeStruct((3, R, C), p.dtype), in_specs=[_HBM], out_specs=_HBM,
        scratch_shapes=[pltpu.SemaphoreType.DMA((3,)), pltpu.SemaphoreType.DMA((3,))], name=name)(p)


def _pair_sum(g, got, c_idx, name):
    _, R, C = g.shape
    tr = _pick(R, 512, 16)

    def body(c_ref, g_ref, got_ref, o_ref):
        o_ref[...] = (g_ref[...].astype(F32) + got_ref[...].astype(F32)).astype(o_ref.dtype)

    grid_spec = pltpu.PrefetchScalarGridSpec(
        num_scalar_prefetch=1, grid=(4, R // tr),
        in_specs=[pl.BlockSpec((1, tr, C), lambda j, r, c_ref: (2 * j + c_ref[0], r, 0)),
                  pl.BlockSpec((1, tr, C), lambda j, r, c_ref: (j, r, 0))],
        out_specs=pl.BlockSpec((1, tr, C), lambda j, r, c_ref: (j, r, 0)))
    return pl.pallas_call(
        body, grid_spec=grid_spec, out_shape=jax.ShapeDtypeStruct((4, R, C), g.dtype),
        compiler_params=_cp("parallel", "parallel"), name=name)(c_idx, g, got)


def _chip_sum(p, got, chip_idx, name):
    _, R, C = p.shape
    tr = _pick(R, 512, 16)

    def body(i_ref, p_ref, got_ref, o_ref):
        acc = p_ref[0].astype(F32)
        for k in range(3):
            acc = acc + got_ref[k].astype(F32)
        o_ref[...] = acc

    grid_spec = pltpu.PrefetchScalarGridSpec(
        num_scalar_prefetch=1, grid=(R // tr,),
        in_specs=[pl.BlockSpec((1, tr, C), lambda r, i_ref: (i_ref[0], r, 0)),
                  pl.BlockSpec((3, tr, C), lambda r, i_ref: (0, r, 0))],
        out_specs=pl.BlockSpec((tr, C), lambda r, i_ref: (r, 0)))
    return pl.pallas_call(
        body, grid_spec=grid_spec, out_shape=jax.ShapeDtypeStruct((R, C), F32),
        compiler_params=_cp("parallel"), name=name)(chip_idx, p, got)


def _sum_slots(a, name):
    n, R, C = a.shape

    def body(a_ref, o_ref):
        acc = a_ref[0]
        for k in range(1, n):
            acc = acc + a_ref[k]
        o_ref[...] = acc

    return pl.pallas_call(body, out_shape=jax.ShapeDtypeStruct((R, C), F32), name=name)(a)


def _reduce_scatter(g, name):
    x, y, c = lax.axis_index("x"), lax.axis_index("y"), lax.axis_index("c")
    got = _exchange_core(g, name + "_core")
    p = _pair_sum(g, got, jnp.reshape(c, (1,)).astype(jnp.int32), name + "_pair")
    got2 = _exchange_chips(p, name + "_chips")
    return _chip_sum(p, got2, jnp.reshape(2 * x + y, (1,)).astype(jnp.int32), name + "_sum")


def _shard_axis(name):
    return {"ev_w_in": 2, "ev_a_conv_w": 2, "ev_b_conv_w": 2, "ev_w_out": 1, "od_w_in": 2, "od_c_ln_g": 1,
            "od_c_ln_b": 1, "od_w_out": 1, "xa_w_q": 1, "xa_w_k": 1, "xa_w_v": 1, "xa_w_o": 1,
            "ffn_w_gate": 2, "ffn_w_up": 2, "ffn_w_down": 1}[name]


BIG = ["ev_w_in", "ev_w_out", "od_w_in", "od_w_out", "xa_w_q", "xa_w_k", "xa_w_v", "xa_w_o",
       "ffn_w_gate", "ffn_w_up", "ffn_w_down"]
SMALL_SHARDED = ["ev_a_conv_w", "ev_b_conv_w", "od_c_ln_g", "od_c_ln_b"]
REPLICATED = ["g_mix", "g_xattn", "g_mem", "g_ffn", "g_final", "ev_a_conv_b", "ev_a_ln_g", "ev_a_ln_b",
              "ev_b_conv_b", "od_w_s", "od_b_s"]
WEIGHTS = ["g_mix", "g_xattn", "g_mem", "g_ffn", "g_final", "ev_w_in", "ev_a_conv_w", "ev_a_conv_b", "ev_a_ln_g",
           "ev_a_ln_b", "ev_b_conv_w", "ev_b_conv_b", "ev_w_out", "od_w_in", "od_c_ln_g", "od_c_ln_b", "od_w_s",
           "od_b_s", "od_w_out", "xa_w_q", "xa_w_k", "xa_w_v", "xa_w_o", "ffn_w_gate", "ffn_w_up", "ffn_w_down"]


def _full_from_blocks(blocks, axis):
    shard = blocks.shape[1:]
    full = jnp.moveaxis(blocks, 0, axis)
    return full.reshape(shard[:axis] + (NDEV * shard[axis],) + shard[axis + 1:])


def _blocks_from_full(full, axis):
    shp = full.shape
    split = full.reshape(shp[:axis] + (NDEV, shp[axis] // NDEV) + shp[axis + 1:])
    return jnp.moveaxis(split, axis, 0)


def _pad_rows(flat, width, row_align):
    per = width * row_align
    n = -(-flat.shape[0] // per) * per
    return jnp.pad(flat, (0, n - flat.shape[0])).reshape(n // width, width)


def _row(v):
    return v.reshape(1, -1)


def _xattn_fwd(h, nq, mem, g_m, wq, wk, wv, wo, g_next, tag):
    mem_n = _rms_fwd(mem, _row(g_m), f"xa_mem_rms_{tag}")
    q = _mm([(nq, wq, "nn")], f"xa_q_{tag}", out_dtype=BF16)
    k = _mm([(mem_n, wk, "nn")], f"xa_k_{tag}", out_dtype=BF16)
    v = _mm([(mem_n, wv, "nn")], f"xa_v_{tag}", out_dtype=BF16)
    o = _attn_fwd(q, k, v, f"xa_attn_{tag}")
    h_new, n_next = _mm([(o, wo, "nn")], f"xa_o_{tag}", res=h, rms_g=_row(g_next))
    return h_new, n_next, (h, nq, mem_n, q, k, v, o)


def _xattn_bwd(dh_new, saved, mem, g_x, g_m, wq, wk, wv, wo, tag):
    h, nq, mem_n, q, k, v, o = saved
    do = _mm([(dh_new, wo, "nt")], f"xa_do_{tag}", out_dtype=BF16)
    d_wo = _mm_tn(o, dh_new, f"xa_dwo_{tag}")
    dq, dk, dv = _attn_bwd(q, k, v, do, f"xa_attn_bwd_{tag}")
    d_wq = _mm_tn(nq, dq, f"xa_dwq_{tag}")
    d_wk = _mm_tn(mem_n, dk, f"xa_dwk_{tag}")
    d_wv = _mm_tn(mem_n, dv, f"xa_dwv_{tag}")
    dmem_n = _mm([(dk, wk, "nt"), (dv, wv, "nt")], f"xa_dmem_{tag}")
    _, d_gm = _rms_bwd(dmem_n, mem, _row(g_m), None, f"xa_mem_rms_bwd_{tag}")
    dh, d_gx = _mm([(dq, wq, "nt")], f"xa_dnq_{tag}", rms_bwd=(h, _row(g_x), dh_new), tm=512)
    return dh, dict(g_xattn=d_gx, g_mem=d_gm), dict(xa_w_q=d_wq, xa_w_k=d_wk, xa_w_v=d_wv, xa_w_o=d_wo)


def _ffn_fwd(h, n, wgt, wut, wd, g_next, tag):
    a, b, hid = _ffn_up(n, wgt, wut, f"ffn_up_{tag}")
    if g_next is None:
        h_new, n_next = _mm([(hid, wd, "nn")], f"ffn_down_{tag}", res=h, tm=512, tn=1024), None
    else:
        h_new, n_next = _mm([(hid, wd, "nn")], f"ffn_down_{tag}", res=h, rms_g=_row(g_next), tm=512)
    return h_new, n_next, (h, n, a, b, hid)


def _ffn_bwd(dh_new, saved, g_f, wgt, wut, wd, tag):
    h, n, a, b, hid = saved
    da, db = _ffn_dhid(dh_new, wd, a, b, f"ffn_dhid_{tag}")
    d_wd = _mm_tn(hid, dh_new, f"ffn_dwd_{tag}", tn=512)
    d_wgt = _mm_tn(da, n, f"ffn_dwg_{tag}", tn=512)
    d_wut = _mm_tn(db, n, f"ffn_dwu_{tag}", tn=512)
    dh, d_gf = _mm([(da, wgt, "nn"), (db, wut, "nn")], f"ffn_dn_{tag}", rms_bwd=(h, _row(g_f), dh_new), tm=256)
    return dh, dict(g_ffn=d_gf), dict(ffn_w_gate=d_wgt, ffn_w_up=d_wut, ffn_w_down=d_wd)


def _local_step(x, mem, loss_target, W):
    grads, big = {}, {}

    def xa_w(i):
        return [W["xa_w_q", i], W["xa_w_k", i], W["xa_w_v", i], W["xa_w_o", i]]

    def ffn_w(i):
        return [W["ffn_w_gate", i], W["ffn_w_up", i], W["ffn_w_down", i]]

    h0 = x
    n0 = _rms_fwd(h0, _row(W["g_mix"][0]), "ev_rms")
    z = _mm([(n0, W["ev_w_in", 0], "nt")], "ev_in", tn=1280)
    ab, ca = _conv_fwd(z, W["ev_a_conv_w"][0], W["ev_a_conv_b"], W["ev_a_ln_g"], W["ev_a_ln_b"],
                       W["ev_b_conv_w"][0], W["ev_b_conv_b"], "ev_conv")
    h1, nq0 = _mm([(ab, W["ev_w_out", 0], "nn")], "ev_out", res=h0, rms_g=_row(W["g_xattn"][0]))
    h2, nf0, xa0 = _xattn_fwd(h1, nq0, mem, W["g_mem"][0], *xa_w(0), W["g_ffn"][0], "l0")
    h3, n3, ff0 = _ffn_fwd(h2, nf0, *ffn_w(0), W["g_mix"][1], "l0")

    zp = _mm([(n3, W["od_w_in", 0], "nt")], "od_in", tn=1024)
    D = x.shape[1]
    ws = W["od_w_s"][0].astype(BF16)
    wst = jnp.swapaxes(ws, 1, 2)
    bsb = jnp.repeat(jnp.transpose(W["od_b_s"][0]), D // C_GROUPS, axis=1)
    y_sgu = _sgu_fwd(zp, W["od_c_ln_g"], W["od_c_ln_b"], ws, bsb, "od_sgu")
    h4, nq1 = _mm([(y_sgu, W["od_w_out", 0], "nn")], "od_out", res=h3, rms_g=_row(W["g_xattn"][1]))
    h5, nf1, xa1 = _xattn_fwd(h4, nq1, mem, W["g_mem"][1], *xa_w(1), W["g_ffn"][1], "l1")
    h6, _, ff1 = _ffn_fwd(h5, nf1, *ffn_w(1), None, "l1")

    loss_row, dh6, d_gfinal = _loss_bwd(h6, _row(W["g_final"]), loss_target, "loss")
    grads["g_final"] = d_gfinal.reshape(-1)

    dh5, g_ff1, b_ff1 = _ffn_bwd(dh6, ff1, W["g_ffn"][1], *ffn_w(1), "l1")
    dh4, g_xa1, b_xa1 = _xattn_bwd(dh5, xa1, mem, W["g_xattn"][1], W["g_mem"][1], *xa_w(1), "l1")
    dy_sgu = _mm([(dh4, W["od_w_out", 0], "nt")], "od_dy", tn=1024)
    big["od_w_out", 0] = _mm_tn(y_sgu, dh4, "od_dwout", tn=1024)
    dzp, d_ws, d_bsb, d_clng, d_clnb = _sgu_bwd(dy_sgu, zp, W["od_c_ln_g"], W["od_c_ln_b"], ws, wst, bsb, "od_sgu_bwd")
    grads["od_w_s"] = d_ws[None]
    grads["od_b_s"] = jnp.transpose(_group_sum(d_bsb, C_GROUPS, "od_dbs"))[None]
    grads["od_c_ln_g"], grads["od_c_ln_b"] = d_clng, d_clnb
    big["od_w_in", 0] = _mm_tn(dzp, n3, "od_dwin", tn=512)
    dh3, d_gmix1 = _mm([(dzp, W["od_w_in", 0], "nn")], "od_dn", rms_bwd=(h3, _row(W["g_mix"][1]), dh4), tm=512)

    dh2, g_ff0, b_ff0 = _ffn_bwd(dh3, ff0, W["g_ffn"][0], *ffn_w(0), "l0")
    dh1, g_xa0, b_xa0 = _xattn_bwd(dh2, xa0, mem, W["g_xattn"][0], W["g_mem"][0], *xa_w(0), "l0")
    dab = _mm([(dh1, W["ev_w_out", 0], "nt")], "ev_dab", tn=1024)
    big["ev_w_out", 0] = _mm_tn(ab, dh1, "ev_dwout", tn=1024)
    dca, d_lng, d_lnb, d_ba = _conv_bwd_ln(dab, ca, W["ev_a_ln_g"], W["ev_a_ln_b"], "ev_conv_bwd_ln")
    dz, d_wa, d_wb, d_bb = _conv_bwd(z, dca, dab, W["ev_a_conv_w"][0], W["ev_b_conv_w"][0], W["ev_b_conv_b"],
                                     "ev_conv_bwd")
    grads.update(ev_a_ln_g=d_lng, ev_a_ln_b=d_lnb, ev_a_conv_b=d_ba, ev_b_conv_b=d_bb,
                 ev_a_conv_w=d_wa[None], ev_b_conv_w=d_wb[None])
    big["ev_w_in", 0] = _mm_tn(dz, n0, "ev_dwin", tn=512)
    grad_x, d_gmix0 = _mm([(dz, W["ev_w_in", 0], "nn")], "ev_dn", rms_bwd=(h0, _row(W["g_mix"][0]), dh1), tm=512)

    grads["g_mix"] = jnp.concatenate([d_gmix0, d_gmix1], axis=0)
    for key in ("g_xattn", "g_mem"):
        grads[key] = jnp.concatenate([g_xa0[key], g_xa1[key]], axis=0)
    grads["g_ffn"] = jnp.concatenate([g_ff0["g_ffn"], g_ff1["g_ffn"]], axis=0)
    for i, parts in enumerate([(b_xa0, b_ff0), (b_xa1, b_ff1)]):
        for part in parts:
            for key, val in part.items():
                big[key, i] = val
    return loss_row, grad_x, grads, big


def kernel(x, mem, g_mix, g_xattn, g_mem, g_ffn, g_final, ev_w_in, ev_a_conv_w, ev_a_conv_b, ev_a_ln_g, ev_a_ln_b, ev_b_conv_w, ev_b_conv_b, ev_w_out, od_w_in, od_c_ln_g, od_c_ln_b, od_w_s, od_b_s, od_w_out, xa_w_q, xa_w_k, xa_w_v, xa_w_o, ffn_w_gate, ffn_w_up, ffn_w_down, loss_target, m_g_mix, m_g_xattn, m_g_mem, m_g_ffn, m_g_final, m_ev_w_in, m_ev_a_conv_w, m_ev_a_conv_b, m_ev_a_ln_g, m_ev_a_ln_b, m_ev_b_conv_w, m_ev_b_conv_b, m_ev_w_out, m_od_w_in, m_od_c_ln_g, m_od_c_ln_b, m_od_w_s, m_od_b_s, m_od_w_out, m_xa_w_q, m_xa_w_k, m_xa_w_v, m_xa_w_o, m_ffn_w_gate, m_ffn_w_up, m_ffn_w_down, v_g_mix, v_g_xattn, v_g_mem, v_g_ffn, v_g_final, v_ev_w_in, v_ev_a_conv_w, v_ev_a_conv_b, v_ev_a_ln_g, v_ev_a_ln_b, v_ev_b_conv_w, v_ev_b_conv_b, v_ev_w_out, v_od_w_in, v_od_c_ln_g, v_od_c_ln_b, v_od_w_s, v_od_b_s, v_od_w_out, v_xa_w_q, v_xa_w_k, v_xa_w_v, v_xa_w_o, v_ffn_w_gate, v_ffn_w_up, v_ffn_w_down):
    local = dict(g_mix=g_mix, g_xattn=g_xattn, g_mem=g_mem, g_ffn=g_ffn, g_final=g_final, ev_w_in=ev_w_in, ev_a_conv_w=ev_a_conv_w, ev_a_conv_b=ev_a_conv_b, ev_a_ln_g=ev_a_ln_g, ev_a_ln_b=ev_a_ln_b, ev_b_conv_w=ev_b_conv_w, ev_b_conv_b=ev_b_conv_b, ev_w_out=ev_w_out, od_w_in=od_w_in, od_c_ln_g=od_c_ln_g, od_c_ln_b=od_c_ln_b, od_w_s=od_w_s, od_b_s=od_b_s, od_w_out=od_w_out, xa_w_q=xa_w_q, xa_w_k=xa_w_k, xa_w_v=xa_w_v, xa_w_o=xa_w_o, ffn_w_gate=ffn_w_gate, ffn_w_up=ffn_w_up, ffn_w_down=ffn_w_down)
    mom = dict(g_mix=m_g_mix, g_xattn=m_g_xattn, g_mem=m_g_mem, g_ffn=m_g_ffn, g_final=m_g_final, ev_w_in=m_ev_w_in, ev_a_conv_w=m_ev_a_conv_w, ev_a_conv_b=m_ev_a_conv_b, ev_a_ln_g=m_ev_a_ln_g, ev_a_ln_b=m_ev_a_ln_b, ev_b_conv_w=m_ev_b_conv_w, ev_b_conv_b=m_ev_b_conv_b, ev_w_out=m_ev_w_out, od_w_in=m_od_w_in, od_c_ln_g=m_od_c_ln_g, od_c_ln_b=m_od_c_ln_b, od_w_s=m_od_w_s, od_b_s=m_od_b_s, od_w_out=m_od_w_out, xa_w_q=m_xa_w_q, xa_w_k=m_xa_w_k, xa_w_v=m_xa_w_v, xa_w_o=m_xa_w_o, ffn_w_gate=m_ffn_w_gate, ffn_w_up=m_ffn_w_up, ffn_w_down=m_ffn_w_down)
    vel = dict(g_mix=v_g_mix, g_xattn=v_g_xattn, g_mem=v_g_mem, g_ffn=v_g_ffn, g_final=v_g_final, ev_w_in=v_ev_w_in, ev_a_conv_w=v_ev_a_conv_w, ev_a_conv_b=v_ev_a_conv_b, ev_a_ln_g=v_ev_a_ln_g, ev_a_ln_b=v_ev_a_ln_b, ev_b_conv_w=v_ev_b_conv_w, ev_b_conv_b=v_ev_b_conv_b, ev_w_out=v_ev_w_out, od_w_in=v_od_w_in, od_c_ln_g=v_od_c_ln_g, od_c_ln_b=v_od_c_ln_b, od_w_s=v_od_w_s, od_b_s=v_od_b_s, od_w_out=v_od_w_out, xa_w_q=v_xa_w_q, xa_w_k=v_xa_w_k, xa_w_v=v_xa_w_v, xa_w_o=v_xa_w_o, ffn_w_gate=v_ffn_w_gate, ffn_w_up=v_ffn_w_up, ffn_w_down=v_ffn_w_down)
    D = x.shape[-1]
    dev = 4 * lax.axis_index("x") + 2 * lax.axis_index("y") + lax.axis_index("c")

    entries = [(n, i) for n in BIG for i in range(local[n].shape[0])]

    def comm_layout(n, a):
        return jnp.transpose(a) if _shard_axis(n) == 2 else a

    shards = [comm_layout(n, local[n][i]).astype(BF16) for n, i in entries]
    rows = [s.shape[0] for s in shards]
    fulls = _all_gather_weights(jnp.concatenate(shards, axis=0), rows, "ag_weights")
    small_sizes = [local[n].size for n in SMALL_SHARDED]
    small_block = _pad_rows(jnp.concatenate([local[n].reshape(-1) for n in SMALL_SHARDED]), 128, 8)
    small_all = _all_gather(small_block, "ag_small").reshape(NDEV, -1)

    W = {n: local[n] for n in REPLICATED}
    for e, full in zip(entries, fulls):
        W[e] = full
    o0 = 0
    for n, sz in zip(SMALL_SHARDED, small_sizes):
        blocks = small_all[:, o0:o0 + sz].reshape((NDEV,) + local[n].shape)
        W[n] = _full_from_blocks(blocks, _shard_axis(n))
        o0 += sz

    loss_row, grad_x, grads, big = _local_step(x[0], mem[0], loss_target[0], W)

    g_big = jnp.concatenate([big[e].reshape(NDEV, r, D) for e, r in zip(entries, rows)], axis=1)
    g_red = _reduce_scatter(g_big, "rs_grads")
    parts = {}
    r0 = 0
    for (n, i), r in zip(entries, rows):
        parts[n, i] = comm_layout(n, g_red[r0:r0 + r])
        r0 += r
    gsh = {n: jnp.stack([parts[n, i] for i in range(local[n].shape[0])]) for n in BIG}

    rest = REPLICATED + SMALL_SHARDED
    rest_full_shapes = [grads[n].shape for n in rest]
    g_rest = _pad_rows(jnp.concatenate([grads[n].astype(F32).reshape(-1) for n in rest]), D, 8)
    g_rest = _sum_slots(_all_gather(g_rest, "ag_small_grads"), "sum_small_grads").reshape(-1)
    o0 = 0
    for n, shp in zip(rest, rest_full_shapes):
        sz = 1
        for s in shp:
            sz *= s
        full = g_rest[o0:o0 + sz].reshape(shp)
        o0 += sz
        if n in SMALL_SHARDED:
            full = lax.dynamic_index_in_dim(_blocks_from_full(full, _shard_axis(n)), dev, 0, keepdims=False)
        gsh[n] = full.reshape(local[n].shape)

    delta, new_m, new_v = {}, {}, {}
    for n in WEIGHTS:
        delta[n], new_m[n], new_v[n] = _adamw(local[n], gsh[n], mom[n], vel[n], f"adamw_{n}")

    loss = lax.psum(loss_row[0, 0], ("x", "y", "c"))
    return (loss, grad_x[None], *[gsh[n] for n in WEIGHTS], *[delta[n] for n in WEIGHTS],
            *[new_m[n] for n in WEIGHTS], *[new_v[n] for n in WEIGHTS])
```

```python
import jax
import jax.numpy as jnp
from jax import lax
from jax.experimental import pallas as pl
from jax.experimental.pallas import tpu as pltpu

F32, BF16 = jnp.float32, jnp.bfloat16
NDEV = 8
RMS_EPS = 1e-6
LN_EPS = 1e-5
CHUNK = 128
C_GROUPS = 8
XA_HEADS = 4
ADAM_LR, ADAM_B1, ADAM_B2, ADAM_EPS, ADAM_WD, ADAM_STEP = 0.001, 0.9, 0.999, 1e-08, 0.01, 10
HALO = 16
ROW_CHUNK = 32
V7X_VMEM_LIMIT = 56 * 1024 * 1024
MESH = pl.DeviceIdType.MESH

TS_ROW = 512
TS_MM = 1024
TN_MM = 1408
TS_FFN = 256
TS_CONV = 512
TS_SGU = 512
TS_ATTN = 512


def _cp(*sem):
    return pltpu.CompilerParams(dimension_semantics=sem, vmem_limit_bytes=V7X_VMEM_LIMIT)


def _pick(n, pref, align):
    for t in range(min(n, pref), 0, -1):
        if n % t == 0 and (t % align == 0 or t == n):
            return t
    return n


def _sigmoid(x):
    return 1.0 / (1.0 + jnp.exp(-x))


def _dot(a, b):
    return jnp.dot(a, b, preferred_element_type=F32)


def _dot_nt(a, b):
    return lax.dot_general(a, b, (((1,), (1,)), ((), ())), preferred_element_type=F32)


def _dot_tn(a, b):
    return lax.dot_general(a, b, (((0,), (0,)), ((), ())), preferred_element_type=F32)


_ANY = pl.BlockSpec(memory_space=pl.ANY)


def _after(after):
    return ([], []) if after is None else ([_ANY], [after])


def _rms_fwd(h, g, name, after=None):
    S, D = h.shape
    ts = _pick(S, TS_MM, 16)
    after_specs, after_ops = _after(after)

    def body(h_ref, g_ref, *rest):
        o_ref = rest[-1]
        x = h_ref[...]
        r = lax.rsqrt(jnp.mean(x * x, axis=-1, keepdims=True) + RMS_EPS)
        o_ref[...] = ((x * r) * g_ref[...]).astype(o_ref.dtype)

    return pl.pallas_call(
        body, grid=(S // ts,),
        in_specs=[pl.BlockSpec((ts, D), lambda i: (i, 0)), pl.BlockSpec((1, D), lambda i: (0, 0))] + after_specs,
        out_specs=pl.BlockSpec((ts, D), lambda i: (i, 0)),
        out_shape=jax.ShapeDtypeStruct((S, D), BF16), compiler_params=_cp("parallel"), name=name)(h, g, *after_ops)


def _rms_bwd(dn, h, g, dres, name):
    S, D = h.shape
    ts = _pick(S, TS_ROW, 8)
    has_res = dres is not None

    def body(*refs):
        if has_res:
            dn_ref, h_ref, g_ref, dres_ref, dh_ref, dg_ref = refs
        else:
            dn_ref, h_ref, g_ref, dh_ref, dg_ref = refs
        x = h_ref[...]
        dn_ = dn_ref[...].astype(F32)
        r = lax.rsqrt(jnp.mean(x * x, axis=-1, keepdims=True) + RMS_EPS)
        xr = x * r

        @pl.when(pl.program_id(0) == 0)
        def _():
            dg_ref[...] = jnp.zeros_like(dg_ref)

        dg_ref[...] += jnp.sum(dn_ * xr, axis=0, keepdims=True)
        u = dn_ * g_ref[...]
        dh = r * u - xr * (r * jnp.mean(u * xr, axis=-1, keepdims=True))
        if has_res:
            dh = dh + dres_ref[...]
        dh_ref[...] = dh

    tile = pl.BlockSpec((ts, D), lambda i: (i, 0))
    vec = pl.BlockSpec((1, D), lambda i: (0, 0))
    ins = [dn, h, g] + ([dres] if has_res else [])
    return pl.pallas_call(
        body, grid=(S // ts,),
        in_specs=[tile, tile, vec] + ([tile] if has_res else []),
        out_specs=[tile, vec],
        out_shape=[jax.ShapeDtypeStruct((S, D), F32), jax.ShapeDtypeStruct((1, D), F32)],
        compiler_params=_cp("arbitrary"), name=name)(*ins)


def _loss_bwd(h, g, target, name):
    S, D = h.shape
    ts = _pick(S, TS_ROW, 8)

    def body(h_ref, g_ref, t_ref, loss_ref, dh_ref, dg_ref):
        x = h_ref[...]
        r = lax.rsqrt(jnp.mean(x * x, axis=-1, keepdims=True) + RMS_EPS)
        xr = x * r
        gg = g_ref[...]
        e = xr * gg - t_ref[...]

        @pl.when(pl.program_id(0) == 0)
        def _():
            dg_ref[...] = jnp.zeros_like(dg_ref)
            loss_ref[...] = jnp.zeros_like(loss_ref)

        tile_loss = jnp.sum(jnp.sum(e * e, axis=0, keepdims=True), axis=1, keepdims=True) * (0.5 / D)
        loss_ref[...] += jnp.broadcast_to(tile_loss, loss_ref.shape)
        dy = e * (1.0 / D)
        dg_ref[...] += jnp.sum(dy * xr, axis=0, keepdims=True)
        u = dy * gg
        dh_ref[...] = r * u - xr * (r * jnp.mean(u * xr, axis=-1, keepdims=True))

    tile = pl.BlockSpec((ts, D), lambda i: (i, 0))
    vec = pl.BlockSpec((1, D), lambda i: (0, 0))
    return pl.pallas_call(
        body, grid=(S // ts,),
        in_specs=[tile, vec, tile],
        out_specs=[pl.BlockSpec((1, 128), lambda i: (0, 0)), tile, vec],
        out_shape=[jax.ShapeDtypeStruct((1, 128), F32), jax.ShapeDtypeStruct((S, D), F32),
                   jax.ShapeDtypeStruct((1, D), F32)],
        compiler_params=_cp("arbitrary"), name=name)(h, g, target)


def _mm(pairs, name, out_dtype=F32, res=None, rms_g=None, rms_bwd=None, tm=None, tn=None, after=None):
    M = pairs[0][0].shape[0]
    N = pairs[0][1].shape[1 if pairs[0][2] == "nn" else 0]
    whole_rows = rms_g is not None or rms_bwd is not None
    tm = _pick(M, tm or TS_MM, 16)
    tn = N if whole_rows else _pick(N, tn or TN_MM, 128)
    npair = len(pairs)
    modes = [p[2] for p in pairs]
    after_specs, after_ops = _after(after)

    def body(*refs):
        acc = None
        for p in range(npair):
            a_ = refs[2 * p][...].astype(BF16)
            d = _dot(a_, refs[2 * p + 1][...]) if modes[p] == "nn" else _dot_nt(a_, refs[2 * p + 1][...])
            acc = d if acc is None else acc + d
        rest = refs[2 * npair + len(after_ops):]
        if res is not None:
            acc = acc + rest[0][...]
            rest = rest[1:]
        if rms_bwd is not None:
            h_ref, g_ref, dres_ref, dh_ref, dg_ref = rest
            x = h_ref[...]
            r = lax.rsqrt(jnp.mean(x * x, axis=-1, keepdims=True) + RMS_EPS)
            xr = x * r

            @pl.when(pl.program_id(0) == 0)
            def _():
                dg_ref[...] = jnp.zeros_like(dg_ref)

            dg_ref[...] += jnp.sum(acc * xr, axis=0, keepdims=True)
            u = acc * g_ref[...]
            dh_ref[...] = r * u - xr * (r * jnp.mean(u * xr, axis=-1, keepdims=True)) + dres_ref[...]
        elif rms_g is not None:
            g_ref, o_ref, n_ref = rest
            o_ref[...] = acc
            r = lax.rsqrt(jnp.mean(acc * acc, axis=-1, keepdims=True) + RMS_EPS)
            n_ref[...] = ((acc * r) * g_ref[...]).astype(BF16)
        else:
            rest[0][...] = acc.astype(rest[0].dtype)

    in_specs, ins = [], []
    for a, w, mode in pairs:
        K = a.shape[1]
        in_specs.append(pl.BlockSpec((tm, K), lambda i, j: (i, 0)))
        in_specs.append(pl.BlockSpec((K, tn), lambda i, j: (0, j)) if mode == "nn"
                        else pl.BlockSpec((tn, K), lambda i, j: (j, 0)))
        ins += [a, w]
    in_specs += after_specs
    ins += after_ops
    tile = pl.BlockSpec((tm, tn), lambda i, j: (i, j))
    vec = pl.BlockSpec((1, tn), lambda i, j: (0, j))
    if res is not None:
        in_specs.append(tile)
        ins.append(res)
    sem = ("parallel", "parallel")
    if rms_bwd is not None:
        in_specs += [tile, vec, tile]
        ins += list(rms_bwd)
        out_specs = [tile, vec]
        out_shape = [jax.ShapeDtypeStruct((M, N), F32), jax.ShapeDtypeStruct((1, N), F32)]
        sem = ("arbitrary", "arbitrary")
    elif rms_g is not None:
        in_specs.append(vec)
        ins.append(rms_g)
        out_specs = [tile, tile]
        out_shape = [jax.ShapeDtypeStruct((M, N), F32), jax.ShapeDtypeStruct((M, N), BF16)]
    else:
        out_specs = tile
        out_shape = jax.ShapeDtypeStruct((M, N), out_dtype)
    return pl.pallas_call(
        body, grid=(M // tm, N // tn), in_specs=in_specs, out_specs=out_specs, out_shape=out_shape,
        compiler_params=_cp(*sem), name=name)(*ins)


def _mm_tn(a, b, name, ts=None, tn=None):
    S, K = a.shape
    N = b.shape[1]
    ts = _pick(S, ts or TS_MM, 16)
    tn = _pick(N, tn or TN_MM, 128)
    nsteps = S // ts

    def body(a_ref, b_ref, o_ref, acc_ref):
        s = pl.program_id(1)

        @pl.when(s == 0)
        def _():
            acc_ref[...] = jnp.zeros_like(acc_ref)

        acc_ref[...] += _dot_tn(a_ref[...].astype(BF16), b_ref[...].astype(BF16))

        @pl.when(s == nsteps - 1)
        def _():
            o_ref[...] = acc_ref[...].astype(o_ref.dtype)

    return pl.pallas_call(
        body, grid=(N // tn, nsteps),
        in_specs=[pl.BlockSpec((ts, K), lambda j, s: (s, 0)), pl.BlockSpec((ts, tn), lambda j, s: (s, j))],
        out_specs=pl.BlockSpec((K, tn), lambda j, s: (0, j)), out_shape=jax.ShapeDtypeStruct((K, N), BF16),
        scratch_shapes=[pltpu.VMEM((K, tn), F32)],
        compiler_params=_cp("parallel", "arbitrary"), name=name)(a, b)


def _col_chunk(n):
    return 256 if n % 256 == 0 else 128


def _ffn_up(n, wgt, wut, name, after=None):
    S, D = n.shape
    F = wgt.shape[0]
    tm = _pick(S, TS_FFN, 16)
    ce = _col_chunk(F)
    after_specs, after_ops = _after(after)

    def body(n_ref, wg_ref, wu_ref, *rest):
        a_ref, b_ref, hid_ref = rest[-3:]
        x = n_ref[...]
        for c0 in range(0, F, ce):
            a = _dot_nt(x, wg_ref[c0:c0 + ce, :])
            b = _dot_nt(x, wu_ref[c0:c0 + ce, :])
            a_ref[:, c0:c0 + ce] = a.astype(BF16)
            b_ref[:, c0:c0 + ce] = b.astype(BF16)
            hid_ref[:, c0:c0 + ce] = (a * _sigmoid(a) * b).astype(BF16)

    wspec = pl.BlockSpec((F, D), lambda i: (0, 0))
    ospec = pl.BlockSpec((tm, F), lambda i: (i, 0))
    osh = jax.ShapeDtypeStruct((S, F), BF16)
    return pl.pallas_call(
        body, grid=(S // tm,),
        in_specs=[pl.BlockSpec((tm, D), lambda i: (i, 0)), wspec, wspec] + after_specs,
        out_specs=[ospec, ospec, ospec], out_shape=[osh, osh, osh],
        compiler_params=_cp("parallel"), name=name)(n, wgt, wut, *after_ops)


def _ffn_dhid(dh, wd, a, b, name):
    S, D = dh.shape
    F = wd.shape[0]
    tm = _pick(S, TS_FFN, 16)
    ce = _col_chunk(F)

    def body(dh_ref, wd_ref, a_ref, b_ref, da_ref, db_ref):
        x = dh_ref[...].astype(BF16)
        for c0 in range(0, F, ce):
            g = _dot_nt(x, wd_ref[c0:c0 + ce, :])
            a_ = a_ref[:, c0:c0 + ce].astype(F32)
            b_ = b_ref[:, c0:c0 + ce].astype(F32)
            sg = _sigmoid(a_)
            da_ref[:, c0:c0 + ce] = (g * b_ * (sg * (1.0 + a_ * (1.0 - sg)))).astype(BF16)
            db_ref[:, c0:c0 + ce] = (g * (a_ * sg)).astype(BF16)

    tile = pl.BlockSpec((tm, F), lambda i: (i, 0))
    osh = jax.ShapeDtypeStruct((S, F), BF16)
    return pl.pallas_call(
        body, grid=(S // tm,),
        in_specs=[pl.BlockSpec((tm, D), lambda i: (i, 0)), pl.BlockSpec((F, D), lambda i: (0, 0)), tile, tile],
        out_specs=[tile, tile], out_shape=[osh, osh],
        compiler_params=_cp("parallel"), name=name)(dh, wd, a, b)


def _softmax_rows(s):
    m = jnp.max(s, axis=-1, keepdims=True)
    p = jnp.exp(s - m)
    return p / jnp.sum(p, axis=-1, keepdims=True)


def _attn_fwd(q, k, v, name):
    S, D = q.shape
    M = k.shape[0]
    hd = D // XA_HEADS
    scale = hd ** -0.5
    ts = _pick(S, TS_ATTN, 16)

    def body(q_ref, k_ref, v_ref, o_ref):
        for h in range(XA_HEADS):
            sl = slice(h * hd, (h + 1) * hd)
            p = _softmax_rows(_dot_nt(q_ref[:, sl], k_ref[:, sl]) * scale)
            o_ref[:, sl] = _dot(p.astype(BF16), v_ref[:, sl]).astype(BF16)

    tile = pl.BlockSpec((ts, D), lambda i: (i, 0))
    memspec = pl.BlockSpec((M, D), lambda i: (0, 0))
    return pl.pallas_call(
        body, grid=(S // ts,), in_specs=[tile, memspec, memspec], out_specs=tile,
        out_shape=jax.ShapeDtypeStruct((S, D), BF16), compiler_params=_cp("parallel"), name=name)(q, k, v)


def _attn_bwd(q, k, v, do, name):
    S, D = q.shape
    M = k.shape[0]
    hd = D // XA_HEADS
    scale = hd ** -0.5
    ts = _pick(S, TS_ATTN, 16)

    def body(q_ref, k_ref, v_ref, do_ref, dq_ref, dk_ref, dv_ref):
        @pl.when(pl.program_id(0) == 0)
        def _():
            dk_ref[...] = jnp.zeros_like(dk_ref)
            dv_ref[...] = jnp.zeros_like(dv_ref)

        for h in range(XA_HEADS):
            sl = slice(h * hd, (h + 1) * hd)
            qh, kh, vh, doh = q_ref[:, sl], k_ref[:, sl], v_ref[:, sl], do_ref[:, sl]
            p = _softmax_rows(_dot_nt(qh, kh) * scale)
            dp = _dot_nt(doh, vh)
            dv_ref[:, sl] += _dot_tn(p.astype(BF16), doh)
            delta = jnp.sum(dp * p, axis=-1, keepdims=True)
            ds = (p * (dp - delta) * scale).astype(BF16)
            dq_ref[:, sl] = _dot(ds, kh).astype(BF16)
            dk_ref[:, sl] += _dot_tn(ds, qh)

    tile = pl.BlockSpec((ts, D), lambda i: (i, 0))
    memspec = pl.BlockSpec((M, D), lambda i: (0, 0))
    return pl.pallas_call(
        body, grid=(S // ts,), in_specs=[tile, memspec, memspec, tile], out_specs=[tile, memspec, memspec],
        out_shape=[jax.ShapeDtypeStruct((S, D), BF16), jax.ShapeDtypeStruct((M, D), F32),
                   jax.ShapeDtypeStruct((M, D), F32)],
        compiler_params=_cp("arbitrary"), name=name)(q, k, v, do)


def _halo_specs(ts, width, col):
    per = ts // HALO

    def prev(i):
        return (jnp.maximum(i * per - 1, 0), col)

    def nxt(i, n_tiles):
        return (jnp.minimum((i + 1) * per, n_tiles * per - 1), col)

    return prev, nxt


def _fill_ext(ext_ref, prev_val, main_val, next_val, first, last, ts):
    ext_ref[pl.ds(0, HALO), :] = jnp.where(first, 0.0, prev_val)
    ext_ref[pl.ds(HALO, ts), :] = main_val
    ext_ref[pl.ds(HALO + ts, HALO), :] = jnp.where(last, 0.0, next_val)


def _conv_fwd(z, wa, ba, lng, lnb, wb, bb, name):
    S = z.shape[0]
    C = z.shape[1] // 5
    KA, KB = wa.shape[0], wb.shape[0]
    pa, pb = KA // 2, KB // 2
    assert pa <= HALO and pb <= HALO
    ts = _pick(S, TS_CONV, ROW_CHUNK)
    nt = S // ts
    rc = ROW_CHUNK
    prev, nxt = _halo_specs(ts, 5 * C, 0)

    def body(z_ref, zp_ref, zn_ref, wa_ref, ba_ref, lng_ref, lnb_ref, wb_ref, bb_ref, ab_ref, ca_ref,
             ga_ext, tb_ext, win_a, win_b):
        i = pl.program_id(0)
        first, last = i == 0, i == nt - 1

        def glu(r):
            return r[:, 0:C] * _sigmoid(r[:, C:2 * C])

        def gcb(r):
            return r[:, 4 * C:5 * C] * r[:, 2 * C:3 * C]

        _fill_ext(ga_ext, glu(zp_ref), glu(z_ref), glu(zn_ref), first, last, ts)
        _fill_ext(tb_ext, gcb(zp_ref), gcb(z_ref), gcb(zn_ref), first, last, ts)

        def chunk(c, carry):
            r0 = pl.multiple_of(c * rc, rc)
            win_a[...] = ga_ext[pl.ds(r0, rc + 2 * HALO), :]
            win_b[...] = tb_ext[pl.ds(r0, rc + 2 * HALO), :]
            acc = jnp.zeros((rc, C), F32)
            for k in range(KA):
                acc = acc + wa_ref[k:k + 1, :] * win_a[pl.ds(HALO - pa + k, rc), :]
            ca = acc + ba_ref[...]
            ca_ref[pl.ds(r0, rc), :] = ca
            mu = jnp.mean(ca, axis=-1, keepdims=True)
            xc = ca - mu
            var = jnp.mean(xc * xc, axis=-1, keepdims=True)
            ln = xc * lax.rsqrt(var + LN_EPS) * lng_ref[...] + lnb_ref[...]
            ab_ref[pl.ds(r0, rc), 0:C] = (ln * _sigmoid(ln)).astype(BF16)
            cb = jnp.zeros((rc, C), F32) + bb_ref[...]
            for k in range(KB):
                cb = cb + wb_ref[k:k + 1, :] * win_b[pl.ds(HALO - pb + k, rc), :]
            ab_ref[pl.ds(r0, rc), C:2 * C] = (z_ref[pl.ds(r0, rc), 3 * C:4 * C] * cb).astype(BF16)
            return carry

        lax.fori_loop(0, ts // rc, chunk, 0)

    zspec = pl.BlockSpec((ts, 5 * C), lambda i: (i, 0))
    zprev = pl.BlockSpec((HALO, 5 * C), prev)
    znext = pl.BlockSpec((HALO, 5 * C), lambda i: nxt(i, nt))

    def full(a):
        return pl.BlockSpec(a.shape, lambda i: (0, 0))

    return pl.pallas_call(
        body, grid=(nt,),
        in_specs=[zspec, zprev, znext, full(wa), full(ba), full(lng), full(lnb), full(wb), full(bb)],
        out_specs=[pl.BlockSpec((ts, 2 * C), lambda i: (i, 0)), pl.BlockSpec((ts, C), lambda i: (i, 0))],
        out_shape=[jax.ShapeDtypeStruct((S, 2 * C), BF16), jax.ShapeDtypeStruct((S, C), F32)],
        scratch_shapes=[pltpu.VMEM((ts + 2 * HALO, C), F32), pltpu.VMEM((ts + 2 * HALO, C), F32),
                        pltpu.VMEM((rc + 2 * HALO, C), F32), pltpu.VMEM((rc + 2 * HALO, C), F32)],
        compiler_params=_cp("parallel"), name=name)(z, z, z, wa, ba, lng, lnb, wb, bb)


def _conv_bwd_ln(dab, ca, lng, lnb, name):
    S, C = ca.shape
    ts = _pick(S, TS_ROW, 8)

    def body(da_ref, ca_ref, lng_ref, lnb_ref, dca_ref, dg_ref, db_ref, dbias_ref):
        @pl.when(pl.program_id(0) == 0)
        def _():
            dg_ref[...] = jnp.zeros_like(dg_ref)
            db_ref[...] = jnp.zeros_like(db_ref)
            dbias_ref[...] = jnp.zeros_like(dbias_ref)

        ca_ = ca_ref[...]
        mu = jnp.mean(ca_, axis=-1, keepdims=True)
        xc = ca_ - mu
        rstd = lax.rsqrt(jnp.mean(xc * xc, axis=-1, keepdims=True) + LN_EPS)
        xh = xc * rstd
        ln = xh * lng_ref[...] + lnb_ref[...]
        sg = _sigmoid(ln)
        dln = da_ref[...].astype(F32) * (sg * (1.0 + ln * (1.0 - sg)))
        dg_ref[...] += jnp.sum(dln * xh, axis=0, keepdims=True)
        db_ref[...] += jnp.sum(dln, axis=0, keepdims=True)
        dxh = dln * lng_ref[...]
        dca = rstd * (dxh - jnp.mean(dxh, axis=-1, keepdims=True) - xh * jnp.mean(dxh * xh, axis=-1, keepdims=True))
        dca_ref[...] = dca
        dbias_ref[...] += jnp.sum(dca, axis=0, keepdims=True)

    tile = pl.BlockSpec((ts, C), lambda i: (i, 0))
    vec = pl.BlockSpec((1, C), lambda i: (0, 0))
    vsh = jax.ShapeDtypeStruct((1, C), F32)
    return pl.pallas_call(
        body, grid=(S // ts,), in_specs=[tile, tile, vec, vec], out_specs=[tile, vec, vec, vec],
        out_shape=[jax.ShapeDtypeStruct((S, C), F32), vsh, vsh, vsh],
        compiler_params=_cp("arbitrary"), name=name)(dab, ca, lng, lnb)


def _conv_bwd(z, dca, dab, wa, wb, bb, name):
    S = z.shape[0]
    C = z.shape[1] // 5
    KA, KB = wa.shape[0], wb.shape[0]
    pa, pb = KA // 2, KB // 2
    ts = _pick(S, TS_CONV, ROW_CHUNK)
    nt = S // ts
    rc = ROW_CHUNK
    prev0, nxt0 = _halo_specs(ts, C, 0)
    prev1, nxt1 = _halo_specs(ts, C, 1)

    def body(z_ref, zp_ref, zn_ref, dca_ref, dcap_ref, dcan_ref, db_ref, dbp_ref, dbn_ref, wa_ref, wb_ref, bb_ref,
             dz_ref, dwa_ref, dwb_ref, dbb_ref,
             ga_ext, dca_ext, tb_ext, dcb_ext, win_ga, win_dca, win_tb, win_dcb, acc_a, acc_b, acc_bias):
        i = pl.program_id(0)
        first, last = i == 0, i == nt - 1

        @pl.when(first)
        def _():
            acc_a[...] = jnp.zeros_like(acc_a)
            acc_b[...] = jnp.zeros_like(acc_b)
            acc_bias[...] = jnp.zeros_like(acc_bias)

        def glu(r):
            return r[:, 0:C] * _sigmoid(r[:, C:2 * C])

        def gcb(r):
            return r[:, 4 * C:5 * C] * r[:, 2 * C:3 * C]

        def dcb(d, r):
            return d[...].astype(F32) * r[:, 3 * C:4 * C]

        _fill_ext(ga_ext, glu(zp_ref), glu(z_ref), glu(zn_ref), first, last, ts)
        _fill_ext(tb_ext, gcb(zp_ref), gcb(z_ref), gcb(zn_ref), first, last, ts)
        _fill_ext(dca_ext, dcap_ref[...], dca_ref[...], dcan_ref[...], first, last, ts)
        _fill_ext(dcb_ext, dcb(dbp_ref, zp_ref), dcb(db_ref, z_ref), dcb(dbn_ref, zn_ref), first, last, ts)

        def fold(x):
            return jnp.sum(x.reshape(rc // 8, 8, C), axis=0)

        def chunk(c, carry):
            r0 = pl.multiple_of(c * rc, rc)
            win_ga[...] = ga_ext[pl.ds(r0, rc + 2 * HALO), :]
            win_dca[...] = dca_ext[pl.ds(r0, rc + 2 * HALO), :]
            win_tb[...] = tb_ext[pl.ds(r0, rc + 2 * HALO), :]
            win_dcb[...] = dcb_ext[pl.ds(r0, rc + 2 * HALO), :]
            dca_c = win_dca[pl.ds(HALO, rc), :]
            dglu = jnp.zeros((rc, C), F32)
            for k in range(KA):
                dglu = dglu + wa_ref[k:k + 1, :] * win_dca[pl.ds(HALO + pa - k, rc), :]
                acc_a[k] += fold(dca_c * win_ga[pl.ds(HALO - pa + k, rc), :])
            val = z_ref[pl.ds(r0, rc), 0:C]
            sg = _sigmoid(z_ref[pl.ds(r0, rc), C:2 * C])
            dz_ref[pl.ds(r0, rc), 0:C] = (dglu * sg).astype(BF16)
            dz_ref[pl.ds(r0, rc), C:2 * C] = (dglu * val * sg * (1.0 - sg)).astype(BF16)
            dcb_c = win_dcb[pl.ds(HALO, rc), :]
            cb = jnp.zeros((rc, C), F32) + bb_ref[...]
            dt = jnp.zeros((rc, C), F32)
            for k in range(KB):
                tb_k = win_tb[pl.ds(HALO - pb + k, rc), :]
                cb = cb + wb_ref[k:k + 1, :] * tb_k
                dt = dt + wb_ref[k:k + 1, :] * win_dcb[pl.ds(HALO + pb - k, rc), :]
                acc_b[k] += fold(dcb_c * tb_k)
            acc_bias[...] += fold(dcb_c)
            db_c = db_ref[pl.ds(r0, rc), :].astype(F32)
            dz_ref[pl.ds(r0, rc), 2 * C:3 * C] = (dt * z_ref[pl.ds(r0, rc), 4 * C:5 * C]).astype(BF16)
            dz_ref[pl.ds(r0, rc), 3 * C:4 * C] = (db_c * cb).astype(BF16)
            dz_ref[pl.ds(r0, rc), 4 * C:5 * C] = (dt * z_ref[pl.ds(r0, rc), 2 * C:3 * C]).astype(BF16)
            return carry

        lax.fori_loop(0, ts // rc, chunk, 0)

        @pl.when(last)
        def _():
            dwa_ref[...] = jnp.sum(acc_a[...], axis=1)
            dwb_ref[...] = jnp.sum(acc_b[...], axis=1)
            dbb_ref[...] = jnp.sum(acc_bias[...], axis=0, keepdims=True)

    zspec = pl.BlockSpec((ts, 5 * C), lambda i: (i, 0))
    zprev = pl.BlockSpec((HALO, 5 * C), prev0)
    znext = pl.BlockSpec((HALO, 5 * C), lambda i: nxt0(i, nt))
    dspec = pl.BlockSpec((ts, C), lambda i: (i, 0))
    dprev = pl.BlockSpec((HALO, C), prev0)
    dnext = pl.BlockSpec((HALO, C), lambda i: nxt0(i, nt))
    bspec = pl.BlockSpec((ts, C), lambda i: (i, 1))
    bprev = pl.BlockSpec((HALO, C), prev1)
    bnext = pl.BlockSpec((HALO, C), lambda i: nxt1(i, nt))

    def full(shape):
        return pl.BlockSpec(shape, lambda i: (0,) * len(shape))

    ext = pltpu.VMEM((ts + 2 * HALO, C), F32)
    win = pltpu.VMEM((rc + 2 * HALO, C), F32)
    return pl.pallas_call(
        body, grid=(nt,),
        in_specs=[zspec, zprev, znext, dspec, dprev, dnext, bspec, bprev, bnext,
                  full(wa.shape), full(wb.shape), full(bb.shape)],
        out_specs=[pl.BlockSpec((ts, 5 * C), lambda i: (i, 0)), full((KA, C)), full((KB, C)), full((1, C))],
        out_shape=[jax.ShapeDtypeStruct((S, 5 * C), BF16), jax.ShapeDtypeStruct((KA, C), F32),
                   jax.ShapeDtypeStruct((KB, C), F32), jax.ShapeDtypeStruct((1, C), F32)],
        scratch_shapes=[ext, ext, ext, ext, win, win, win, win,
                        pltpu.VMEM((KA, 8, C), F32), pltpu.VMEM((KB, 8, C), F32), pltpu.VMEM((8, C), F32)],
        compiler_params=_cp("arbitrary"), name=name)(z, z, z, dca, dca, dca, dab, dab, dab, wa, wb, bb)


_GELU_C = 0.7978845608028654
_GELU_A = 0.044715


def _gelu(x):
    return 0.5 * x * (1.0 + jnp.tanh(_GELU_C * (x + _GELU_A * (x * x * x))))


def _gelu_grad(x):
    t = jnp.tanh(_GELU_C * (x + _GELU_A * (x * x * x)))
    return 0.5 * (1.0 + t) + 0.5 * x * (1.0 - t * t) * (_GELU_C * (1.0 + 3.0 * _GELU_A * x * x))


def _sgu_fwd(zp, lng, lnb, ws, bsb, name):
    S = zp.shape[0]
    D = zp.shape[1] // 2
    G = ws.shape[0]
    gd = D // G
    ts = _pick(S, TS_SGU, CHUNK)
    ncs = ts // CHUNK

    def body(zp_ref, lng_ref, lnb_ref, ws_ref, bsb_ref, y_ref, vb_ref):
        v = _gelu(zp_ref[:, D:2 * D])
        mu = jnp.mean(v, axis=-1, keepdims=True)
        xc = v - mu
        rstd = lax.rsqrt(jnp.mean(xc * xc, axis=-1, keepdims=True) + LN_EPS)
        vb_ref[...] = (xc * rstd * lng_ref[...] + lnb_ref[...]).astype(BF16)
        for c in range(ncs):
            rows = slice(c * CHUNK, (c + 1) * CHUNK)
            for g in range(G):
                cols = slice(g * gd, (g + 1) * gd)
                sv = _dot(ws_ref[g], vb_ref[rows, cols]) + bsb_ref[:, cols]
                y_ref[rows, cols] = (_gelu(zp_ref[rows, cols]) * sv).astype(BF16)

    def full(a):
        return pl.BlockSpec(a.shape, lambda i: (0,) * a.ndim)

    return pl.pallas_call(
        body, grid=(S // ts,),
        in_specs=[pl.BlockSpec((ts, 2 * D), lambda i: (i, 0)), full(lng), full(lnb), full(ws), full(bsb)],
        out_specs=pl.BlockSpec((ts, D), lambda i: (i, 0)), out_shape=jax.ShapeDtypeStruct((S, D), BF16),
        scratch_shapes=[pltpu.VMEM((ts, D), BF16)],
        compiler_params=_cp("parallel"), name=name)(zp, lng, lnb, ws, bsb)


def _sgu_bwd(dy, zp, lng, lnb, ws, wst, bsb, name):
    S = zp.shape[0]
    D = zp.shape[1] // 2
    G = ws.shape[0]
    gd = D // G
    ts = _pick(S, TS_SGU, CHUNK)
    ncs = ts // CHUNK

    def body(dy_ref, zp_ref, lng_ref, lnb_ref, ws_ref, wst_ref, bsb_ref,
             dzp_ref, dws_ref, dbs_ref, dg_ref, db_ref, vb_ref, dvln_ref, acc_bs):
        i = pl.program_id(0)

        @pl.when(i == 0)
        def _():
            dws_ref[...] = jnp.zeros_like(dws_ref)
            acc_bs[...] = jnp.zeros_like(acc_bs)
            dg_ref[...] = jnp.zeros_like(dg_ref)
            db_ref[...] = jnp.zeros_like(db_ref)

        zv = zp_ref[:, D:2 * D]
        v = _gelu(zv)
        mu = jnp.mean(v, axis=-1, keepdims=True)
        xc = v - mu
        rstd = lax.rsqrt(jnp.mean(xc * xc, axis=-1, keepdims=True) + LN_EPS)
        xh = xc * rstd
        vb_ref[...] = (xh * lng_ref[...] + lnb_ref[...]).astype(BF16)
        for c in range(ncs):
            rows = slice(c * CHUNK, (c + 1) * CHUNK)
            for g in range(G):
                cols = slice(g * gd, (g + 1) * gd)
                zu = zp_ref[rows, cols]
                u = _gelu(zu)
                dy_ = dy_ref[rows, cols].astype(F32)
                sv = _dot(ws_ref[g], vb_ref[rows, cols]) + bsb_ref[:, cols]
                dzp_ref[rows, cols] = (dy_ * sv * _gelu_grad(zu)).astype(BF16)
                dsv = dy_ * u
                acc_bs[:, cols] += dsv
                dsvb = dsv.astype(BF16)
                dws_ref[g] += _dot_nt(dsvb, vb_ref[rows, cols])
                dvln_ref[rows, cols] = _dot(wst_ref[g], dsvb)
        dvln = dvln_ref[...]
        dg_ref[...] += jnp.sum(dvln * xh, axis=0, keepdims=True)
        db_ref[...] += jnp.sum(dvln, axis=0, keepdims=True)
        dxh = dvln * lng_ref[...]
        dv = rstd * (dxh - jnp.mean(dxh, axis=-1, keepdims=True) - xh * jnp.mean(dxh * xh, axis=-1, keepdims=True))
        dzp_ref[:, D:2 * D] = (dv * _gelu_grad(zv)).astype(BF16)

        @pl.when(i == pl.num_programs(0) - 1)
        def _():
            dbs_ref[...] = acc_bs[...]

    def full(shape):
        return pl.BlockSpec(shape, lambda i: (0,) * len(shape))

    return pl.pallas_call(
        body, grid=(S // ts,),
        in_specs=[pl.BlockSpec((ts, D), lambda i: (i, 0)), pl.BlockSpec((ts, 2 * D), lambda i: (i, 0)),
                  full(lng.shape), full(lnb.shape), full(ws.shape), full(wst.shape), full(bsb.shape)],
        out_specs=[pl.BlockSpec((ts, 2 * D), lambda i: (i, 0)), full(ws.shape), full(bsb.shape),
                   full((1, D)), full((1, D))],
        out_shape=[jax.ShapeDtypeStruct((S, 2 * D), BF16), jax.ShapeDtypeStruct(ws.shape, F32),
                   jax.ShapeDtypeStruct(bsb.shape, F32), jax.ShapeDtypeStruct((1, D), F32),
                   jax.ShapeDtypeStruct((1, D), F32)],
        scratch_shapes=[pltpu.VMEM((ts, D), BF16), pltpu.VMEM((ts, D), F32),
                        pltpu.VMEM(bsb.shape, F32)],
        compiler_params=_cp("arbitrary"), name=name)(dy, zp, lng, lnb, ws, wst, bsb)


def _group_sum(x, groups, name):
    P, D = x.shape
    gd = D // groups

    def body(x_ref, o_ref):
        for g in range(groups):
            o_ref[:, g:g + 1] = jnp.sum(x_ref[:, g * gd:(g + 1) * gd], axis=1, keepdims=True)

    return pl.pallas_call(body, out_shape=jax.ShapeDtypeStruct((P, groups), F32), name=name)(x)


def _adamw(w, g, m, v, name):
    shape = w.shape
    C = shape[-1]
    R = w.size // C
    tr = _pick(R, 1024, 8)
    bc1 = 1.0 - ADAM_B1 ** ADAM_STEP
    bc2 = 1.0 - ADAM_B2 ** ADAM_STEP

    def body(w_ref, g_ref, m_ref, v_ref, d_ref, nm_ref, nv_ref):
        g_ = g_ref[...]
        nm = ADAM_B1 * m_ref[...] + (1.0 - ADAM_B1) * g_
        nv = ADAM_B2 * v_ref[...] + (1.0 - ADAM_B2) * (g_ * g_)
        nm_ref[...] = nm
        nv_ref[...] = nv
        d_ref[...] = -ADAM_LR * ((nm / bc1) / (jnp.sqrt(nv / bc2) + ADAM_EPS) + ADAM_WD * w_ref[...])

    tile = pl.BlockSpec((tr, C), lambda i: (i, 0))
    sh = jax.ShapeDtypeStruct((R, C), F32)
    outs = pl.pallas_call(
        body, grid=(R // tr,), in_specs=[tile] * 4, out_specs=[tile] * 3, out_shape=[sh] * 3,
        compiler_params=_cp("parallel"), name=name)(*(a.reshape(R, C) for a in (w, g, m, v)))
    return tuple(o.reshape(shape) for o in outs)


_HBM = pl.BlockSpec(memory_space=pltpu.HBM)


def _remote(src, dst, send_sem, recv_sem, to):
    return pltpu.make_async_remote_copy(src_ref=src, dst_ref=dst, send_sem=send_sem, recv_sem=recv_sem,
                                        device_id=to, device_id_type=MESH)


def _all_gather(block, name):
    R, C = block.shape

    def body(x_ref, out_ref, send_sems, recv_sems, local_sem):
        x, y, c = lax.axis_index("x"), lax.axis_index("y"), lax.axis_index("c")
        me, sibling = (x, y, c), (x, y, 1 - c)
        chips = [(1 - x, y), (x, 1 - y), (1 - x, 1 - y)]

        def slot(px, py, pc):
            return out_ref.at[4 * px + 2 * py + pc]

        def copy(k, blk, to, src=None):
            return _remote(slot(*blk) if src is None else src, slot(*blk), send_sems.at[k], recv_sems.at[k], to)

        mine = pltpu.make_async_copy(x_ref, slot(*me), local_sem)
        mine.start()
        first = [copy(0, me, sibling, src=x_ref)]
        first += [copy(1 + j, me, (*chip, c), src=x_ref) for j, chip in enumerate(chips)]
        for cp in first:
            cp.start()
        passed = [copy(4 + j, (*chip, c), sibling) for j, chip in enumerate(chips)]
        for j, chip in enumerate(chips):
            copy(1 + j, (*chip, c), me).wait_recv()
            passed[j].start()
        copy(0, sibling, me).wait_recv()
        for j, chip in enumerate(chips):
            copy(4 + j, (*chip, 1 - c), me).wait_recv()
        for cp in first + passed:
            cp.wait_send()
        mine.wait()

    return pl.pallas_call(
        body, out_shape=jax.ShapeDtypeStruct((NDEV, R, C), block.dtype), in_specs=[_HBM], out_specs=_HBM,
        scratch_shapes=[pltpu.SemaphoreType.DMA((7,)), pltpu.SemaphoreType.DMA((7,)), pltpu.SemaphoreType.DMA],
        name=name)(block)


def _all_gather_weights(pack, rows, name):
    C = pack.shape[1]
    nw = len(rows)
    starts = [sum(rows[:w]) for w in range(nw)]

    def body(pack_ref, *rest):
        outs = rest[:nw]
        send_sems, recv_sems, local_sem = rest[nw:]
        x, y, c = lax.axis_index("x"), lax.axis_index("y"), lax.axis_index("c")
        me, sibling = (x, y, c), (x, y, 1 - c)
        chips = [(1 - x, y), (x, 1 - y), (1 - x, 1 - y)]

        def block(w, px, py, pc):
            return outs[w].at[pl.ds((4 * px + 2 * py + pc) * rows[w], rows[w])]

        def mine(w):
            return pack_ref.at[pl.ds(starts[w], rows[w])]

        def all_of(k):
            return _remote(pack_ref, pack_ref, send_sems.at[k], recv_sems.at[k], me)

        for w in range(nw):
            pltpu.make_async_copy(mine(w), block(w, *me), local_sem).start()
        for k, to in enumerate([sibling] + [(*chip, c) for chip in chips]):
            for w in range(nw):
                _remote(mine(w), block(w, *me), send_sems.at[k], recv_sems.at[k], to).start()
        for j, chip in enumerate(chips):
            all_of(1 + j).wait_recv()
            for w in range(nw):
                _remote(block(w, *chip, c), block(w, *chip, c), send_sems.at[4 + j], recv_sems.at[4 + j], sibling).start()
        all_of(0).wait_recv()
        for j in range(3):
            all_of(4 + j).wait_recv()
        for k in range(7):
            all_of(k).wait_send()
        pltpu.make_async_copy(pack_ref, pack_ref, local_sem).wait()

    return pl.pallas_call(
        body, out_shape=[jax.ShapeDtypeStruct((NDEV * r, C), pack.dtype) for r in rows],
        in_specs=[_HBM], out_specs=[_HBM] * nw,
        scratch_shapes=[pltpu.SemaphoreType.DMA((7,)), pltpu.SemaphoreType.DMA((7,)), pltpu.SemaphoreType.DMA],
        name=name)(pack)


_SEM = pl.BlockSpec(memory_space=pltpu.SEMAPHORE)
_DATAFLOW = pltpu.SideEffectType.DATAFLOW_SIDE_EFFECTING


def _split_start(srcs, lands, plan, n, after, name):
    nbuf = len(srcs) + len(lands)
    after_specs, after_ops = _after(after)

    def body(*refs):
        src_refs, land_refs = refs[:len(srcs)], refs[len(srcs):nbuf]
        send_sems, recv_sems = refs[nbuf + len(after_ops)], refs[nbuf + len(after_ops) + 1]
        for k, (src, dst, to) in enumerate(plan(src_refs, land_refs)):
            _remote(src, dst, send_sems.at[k], recv_sems.at[k], to).start()
        refs[-1][...] = jnp.zeros_like(refs[-1])

    bufs = [pltpu.with_memory_space_constraint(a, pltpu.HBM) for a in list(srcs) + list(lands)]
    outs = pl.pallas_call(
        body, name=name,
        out_shape=(pltpu.SemaphoreType.DMA((n,)), pltpu.SemaphoreType.DMA((n,)),
                   *[pltpu.HBM(a.shape, a.dtype) for a in bufs], jax.ShapeDtypeStruct((8, 128), F32)),
        in_specs=[_HBM] * nbuf + after_specs,
        out_specs=(_SEM, _SEM, *[_HBM] * nbuf, pl.BlockSpec(memory_space=pltpu.VMEM)),
        input_output_aliases={i: 2 + i for i in range(nbuf)},
        compiler_params=pltpu.CompilerParams(has_side_effects=_DATAFLOW))(*bufs, *after_ops)
    return outs[0], outs[1], list(outs[2:2 + len(srcs)]), list(outs[2 + len(srcs):2 + nbuf]), outs[-1]


def _split_wait(send_sems, recv_sems, srcs, lands, plan, after, name):
    nbuf = len(srcs) + len(lands)
    after_specs, after_ops = _after(after)

    def body(*refs):
        src_refs, land_refs = refs[:len(srcs)], refs[len(srcs):nbuf]
        send_sems_ref, recv_sems_ref = refs[nbuf], refs[nbuf + 1]
        for k, (src, dst, to) in enumerate(plan(src_refs, land_refs)):
            copy = _remote(src, dst, send_sems_ref.at[k], recv_sems_ref.at[k], to)
            copy.wait_send()
            copy.wait_recv()

    outs = pl.pallas_call(
        body, name=name, out_shape=tuple(pltpu.HBM(a.shape, a.dtype) for a in list(srcs) + list(lands)),
        in_specs=[_HBM] * nbuf + [_SEM, _SEM] + after_specs, out_specs=tuple([_HBM] * nbuf),
        input_output_aliases={i: i for i in range(nbuf)},
        compiler_params=pltpu.CompilerParams(has_side_effects=_DATAFLOW))(*srcs, *lands, send_sems, recv_sems, *after_ops)
    return list(outs[:len(srcs)]), list(outs[len(srcs):])


def _gather_plan(rows):
    starts = [sum(rows[:w]) for w in range(len(rows))]

    def plan(src_refs, land_refs):
        x, y, c = lax.axis_index("x"), lax.axis_index("y"), lax.axis_index("c")
        copies = []
        for w, r in enumerate(rows):
            mine = src_refs[0].at[pl.ds(starts[w], r)]
            dst = land_refs[w].at[pl.ds((4 * x + 2 * y + c) * r, r)]
            copies += [(mine, dst, (px, py, c)) for px, py in [(1 - x, y), (x, 1 - y), (1 - x, 1 - y)]]
        return copies

    return plan, 3 * len(rows)


def _gather_finish(pack, fulls, rows, name):
    nw = len(rows)
    starts = [sum(rows[:w]) for w in range(nw)]

    def body(pack_ref, *rest):
        lands = rest[nw:2 * nw]
        send_sems, recv_sems, local_sem = rest[2 * nw:]
        x, y, c = lax.axis_index("x"), lax.axis_index("y"), lax.axis_index("c")
        sibling = (x, y, 1 - c)
        copies = []
        for w, r in enumerate(rows):
            mine = pack_ref.at[pl.ds(starts[w], r)]
            own = lands[w].at[pl.ds((4 * x + 2 * y + c) * r, r)]
            pltpu.make_async_copy(mine, own, local_sem).start()
            copies.append(_remote(mine, own, send_sems.at[4 * w], recv_sems.at[4 * w], sibling))
            for j, (px, py) in enumerate([(1 - x, y), (x, 1 - y), (1 - x, 1 - y)]):
                blk = lands[w].at[pl.ds((4 * px + 2 * py + c) * r, r)]
                copies.append(_remote(blk, blk, send_sems.at[4 * w + 1 + j], recv_sems.at[4 * w + 1 + j], sibling))
        for cp in copies:
            cp.start()
        for cp in copies:
            cp.wait_recv()
        for cp in copies:
            cp.wait_send()
        pltpu.make_async_copy(pack_ref, pack_ref, local_sem).wait()

    outs = pl.pallas_call(
        body, out_shape=[jax.ShapeDtypeStruct(f.shape, f.dtype) for f in fulls],
        in_specs=[_HBM] * (1 + nw), out_specs=[_HBM] * nw,
        input_output_aliases={1 + w: w for w in range(nw)},
        scratch_shapes=[pltpu.SemaphoreType.DMA((4 * nw,)), pltpu.SemaphoreType.DMA((4 * nw,)),
                        pltpu.SemaphoreType.DMA],
        name=name)(pack, *fulls)
    return list(outs)


def _scatter_plan(rows):
    def plan(src_refs, land_refs):
        x, y, c = lax.axis_index("x"), lax.axis_index("y"), lax.axis_index("c")
        copies = []
        for w, r in enumerate(rows):
            for mask in range(1, NDEV):
                px = 1 - x if mask & 4 else x
                py = 1 - y if mask & 2 else y
                pc = 1 - c if mask & 1 else c
                src = src_refs[w].at[pl.ds((4 * px + 2 * py + pc) * r, r)]
                copies.append((src, land_refs[w].at[mask - 1], (px, py, pc)))
        return copies

    return plan, (NDEV - 1) * len(rows)


def _scatter_sum(g, got, dev_idx, name):
    n1, r, C = got.shape
    tr = _pick(r, 512, 16)
    per = r // tr

    def body(i_ref, g_ref, got_ref, o_ref):
        acc = g_ref[...].astype(F32)
        for k in range(n1):
            acc = acc + got_ref[k].astype(F32)
        o_ref[...] = acc

    grid_spec = pltpu.PrefetchScalarGridSpec(
        num_scalar_prefetch=1, grid=(per,),
        in_specs=[pl.BlockSpec((tr, C), lambda t, i_ref: (i_ref[0] * per + t, 0)),
                  pl.BlockSpec((n1, tr, C), lambda t, i_ref: (0, t, 0))],
        out_specs=pl.BlockSpec((tr, C), lambda t, i_ref: (t, 0)))
    return pl.pallas_call(
        body, grid_spec=grid_spec, out_shape=jax.ShapeDtypeStruct((r, C), F32),
        compiler_params=_cp("parallel"), name=name)(dev_idx, g, got)


def _sum_slots(a, name):
    n, R, C = a.shape

    def body(a_ref, o_ref):
        acc = a_ref[0]
        for k in range(1, n):
            acc = acc + a_ref[k]
        o_ref[...] = acc

    return pl.pallas_call(body, out_shape=jax.ShapeDtypeStruct((R, C), F32), name=name)(a)


def _shard_axis(name):
    return {"ev_w_in": 2, "ev_a_conv_w": 2, "ev_b_conv_w": 2, "ev_w_out": 1, "od_w_in": 2, "od_c_ln_g": 1,
            "od_c_ln_b": 1, "od_w_out": 1, "xa_w_q": 1, "xa_w_k": 1, "xa_w_v": 1, "xa_w_o": 1,
            "ffn_w_gate": 2, "ffn_w_up": 2, "ffn_w_down": 1}[name]


BIG = ["ev_w_in", "ev_w_out", "od_w_in", "od_w_out", "xa_w_q", "xa_w_k", "xa_w_v", "xa_w_o",
       "ffn_w_gate", "ffn_w_up", "ffn_w_down"]
SMALL_SHARDED = ["ev_a_conv_w", "ev_b_conv_w", "od_c_ln_g", "od_c_ln_b"]
REPLICATED = ["g_mix", "g_xattn", "g_mem", "g_ffn", "g_final", "ev_a_conv_b", "ev_a_ln_g", "ev_a_ln_b",
              "ev_b_conv_b", "od_w_s", "od_b_s"]
WEIGHTS = ["g_mix", "g_xattn", "g_mem", "g_ffn", "g_final", "ev_w_in", "ev_a_conv_w", "ev_a_conv_b", "ev_a_ln_g",
           "ev_a_ln_b", "ev_b_conv_w", "ev_b_conv_b", "ev_w_out", "od_w_in", "od_c_ln_g", "od_c_ln_b", "od_w_s",
           "od_b_s", "od_w_out", "xa_w_q", "xa_w_k", "xa_w_v", "xa_w_o", "ffn_w_gate", "ffn_w_up", "ffn_w_down"]


def _full_from_blocks(blocks, axis):
    shard = blocks.shape[1:]
    full = jnp.moveaxis(blocks, 0, axis)
    return full.reshape(shard[:axis] + (NDEV * shard[axis],) + shard[axis + 1:])


def _blocks_from_full(full, axis):
    shp = full.shape
    split = full.reshape(shp[:axis] + (NDEV, shp[axis] // NDEV) + shp[axis + 1:])
    return jnp.moveaxis(split, axis, 0)


def _pad_rows(flat, width, row_align):
    per = width * row_align
    n = -(-flat.shape[0] // per) * per
    return jnp.pad(flat, (0, n - flat.shape[0])).reshape(n // width, width)


def _row(v):
    return v.reshape(1, -1)


def _xattn_fwd(h, nq, mem, g_m, wq, wk, wv, wo, g_next, tag, after):
    mem_n = _rms_fwd(mem, _row(g_m), f"xa_mem_rms_{tag}")
    q = _mm([(nq, wq, "nn")], f"xa_q_{tag}", out_dtype=BF16, after=after)
    k = _mm([(mem_n, wk, "nn")], f"xa_k_{tag}", out_dtype=BF16)
    v = _mm([(mem_n, wv, "nn")], f"xa_v_{tag}", out_dtype=BF16)
    o = _attn_fwd(q, k, v, f"xa_attn_{tag}")
    h_new, n_next = _mm([(o, wo, "nn")], f"xa_o_{tag}", res=h, rms_g=_row(g_next))
    return h_new, n_next, (h, nq, mem_n, q, k, v, o)


def _xattn_bwd(dh_new, saved, mem, g_x, g_m, wq, wk, wv, wo, tag, push):
    h, nq, mem_n, q, k, v, o = saved
    do = _mm([(dh_new, wo, "nt")], f"xa_do_{tag}", out_dtype=BF16)
    d_wo = _mm_tn(o, dh_new, f"xa_dwo_{tag}")
    dq, dk, dv = _attn_bwd(q, k, v, do, f"xa_attn_bwd_{tag}")
    d_wq = _mm_tn(nq, dq, f"xa_dwq_{tag}")
    d_wk = _mm_tn(mem_n, dk, f"xa_dwk_{tag}")
    d_wv = _mm_tn(mem_n, dv, f"xa_dwv_{tag}")
    token = push([d_wq, d_wk, d_wv, d_wo])
    dmem_n = _mm([(dk, wk, "nt"), (dv, wv, "nt")], f"xa_dmem_{tag}", after=token)
    _, d_gm = _rms_bwd(dmem_n, mem, _row(g_m), None, f"xa_mem_rms_bwd_{tag}")
    dh, d_gx = _mm([(dq, wq, "nt")], f"xa_dnq_{tag}", rms_bwd=(h, _row(g_x), dh_new), tm=512, after=token)
    return dh, dict(g_xattn=d_gx, g_mem=d_gm)


def _ffn_fwd(h, n, wgt, wut, wd, g_next, tag, after):
    a, b, hid = _ffn_up(n, wgt, wut, f"ffn_up_{tag}", after=after)
    if g_next is None:
        h_new, n_next = _mm([(hid, wd, "nn")], f"ffn_down_{tag}", res=h, tm=512, tn=1024), None
    else:
        h_new, n_next = _mm([(hid, wd, "nn")], f"ffn_down_{tag}", res=h, rms_g=_row(g_next), tm=512)
    return h_new, n_next, (h, n, a, b, hid)


def _ffn_bwd(dh_new, saved, g_f, wgt, wut, wd, tag, push):
    h, n, a, b, hid = saved
    da, db = _ffn_dhid(dh_new, wd, a, b, f"ffn_dhid_{tag}")
    d_wd = _mm_tn(hid, dh_new, f"ffn_dwd_{tag}", tn=512)
    d_wgt = _mm_tn(da, n, f"ffn_dwg_{tag}", tn=512)
    d_wut = _mm_tn(db, n, f"ffn_dwu_{tag}", tn=512)
    token = push([d_wgt, d_wut, d_wd])
    dh, d_gf = _mm([(da, wgt, "nn"), (db, wut, "nn")], f"ffn_dn_{tag}", rms_bwd=(h, _row(g_f), dh_new), tm=256,
                   after=token)
    return dh, dict(g_ffn=d_gf)


GROUPS = {
    "ev": [("ev_w_in", 0), ("ev_w_out", 0)],
    "xa0": [("xa_w_q", 0), ("xa_w_k", 0), ("xa_w_v", 0), ("xa_w_o", 0)],
    "ffn0": [("ffn_w_gate", 0), ("ffn_w_up", 0), ("ffn_w_down", 0)],
    "od": [("od_w_in", 0), ("od_w_out", 0)],
    "xa1": [("xa_w_q", 1), ("xa_w_k", 1), ("xa_w_v", 1), ("xa_w_o", 1)],
    "ffn1": [("ffn_w_gate", 1), ("ffn_w_up", 1), ("ffn_w_down", 1)],
}


def _local_step(x, mem, loss_target, W, comm):
    grads = {}

    h0 = x
    (ev_w_in_t, ev_w_out), token = comm.weights("ev", None)
    n0 = _rms_fwd(h0, _row(W["g_mix"][0]), "ev_rms", after=token)
    z = _mm([(n0, ev_w_in_t, "nt")], "ev_in", tn=1280)
    ab, ca = _conv_fwd(z, W["ev_a_conv_w"][0], W["ev_a_conv_b"], W["ev_a_ln_g"], W["ev_a_ln_b"],
                       W["ev_b_conv_w"][0], W["ev_b_conv_b"], "ev_conv")
    h1, nq0 = _mm([(ab, ev_w_out, "nn")], "ev_out", res=h0, rms_g=_row(W["g_xattn"][0]))
    xa_w0, token = comm.weights("xa0", nq0)
    h2, nf0, xa0 = _xattn_fwd(h1, nq0, mem, W["g_mem"][0], *xa_w0, W["g_ffn"][0], "l0", token)
    ffn_w0, token = comm.weights("ffn0", nf0)
    h3, n3, ff0 = _ffn_fwd(h2, nf0, *ffn_w0, W["g_mix"][1], "l0", token)

    (od_w_in_t, od_w_out), token = comm.weights("od", n3)
    zp = _mm([(n3, od_w_in_t, "nt")], "od_in", tn=1024, after=token)
    D = x.shape[1]
    ws = W["od_w_s"][0].astype(BF16)
    wst = jnp.swapaxes(ws, 1, 2)
    bsb = jnp.repeat(jnp.transpose(W["od_b_s"][0]), D // C_GROUPS, axis=1)
    y_sgu = _sgu_fwd(zp, W["od_c_ln_g"], W["od_c_ln_b"], ws, bsb, "od_sgu")
    h4, nq1 = _mm([(y_sgu, od_w_out, "nn")], "od_out", res=h3, rms_g=_row(W["g_xattn"][1]))
    xa_w1, token = comm.weights("xa1", nq1)
    h5, nf1, xa1 = _xattn_fwd(h4, nq1, mem, W["g_mem"][1], *xa_w1, W["g_ffn"][1], "l1", token)
    ffn_w1, token = comm.weights("ffn1", nf1)
    h6, _, ff1 = _ffn_fwd(h5, nf1, *ffn_w1, None, "l1", token)

    loss_row, dh6, d_gfinal = _loss_bwd(h6, _row(W["g_final"]), loss_target, "loss")
    grads["g_final"] = d_gfinal.reshape(-1)

    dh5, g_ff1 = _ffn_bwd(dh6, ff1, W["g_ffn"][1], *ffn_w1, "l1", lambda dws: comm.grads("ffn1", dws))
    dh4, g_xa1 = _xattn_bwd(dh5, xa1, mem, W["g_xattn"][1], W["g_mem"][1], *xa_w1, "l1",
                            lambda dws: comm.grads("xa1", dws))
    dy_sgu = _mm([(dh4, od_w_out, "nt")], "od_dy", tn=1024)
    d_od_out = _mm_tn(y_sgu, dh4, "od_dwout", tn=1024)
    dzp, d_ws, d_bsb, d_clng, d_clnb = _sgu_bwd(dy_sgu, zp, W["od_c_ln_g"], W["od_c_ln_b"], ws, wst, bsb, "od_sgu_bwd")
    grads["od_w_s"] = d_ws[None]
    grads["od_b_s"] = jnp.transpose(_group_sum(d_bsb, C_GROUPS, "od_dbs"))[None]
    grads["od_c_ln_g"], grads["od_c_ln_b"] = d_clng, d_clnb
    token = comm.grads("od", [_mm_tn(dzp, n3, "od_dwin", tn=512), d_od_out])
    dh3, d_gmix1 = _mm([(dzp, od_w_in_t, "nn")], "od_dn", rms_bwd=(h3, _row(W["g_mix"][1]), dh4), tm=512, after=token)

    dh2, g_ff0 = _ffn_bwd(dh3, ff0, W["g_ffn"][0], *ffn_w0, "l0", lambda dws: comm.grads("ffn0", dws))
    dh1, g_xa0 = _xattn_bwd(dh2, xa0, mem, W["g_xattn"][0], W["g_mem"][0], *xa_w0, "l0",
                            lambda dws: comm.grads("xa0", dws))
    dab = _mm([(dh1, ev_w_out, "nt")], "ev_dab", tn=1024)
    d_ev_out = _mm_tn(ab, dh1, "ev_dwout", tn=1024)
    dca, d_lng, d_lnb, d_ba = _conv_bwd_ln(dab, ca, W["ev_a_ln_g"], W["ev_a_ln_b"], "ev_conv_bwd_ln")
    dz, d_wa, d_wb, d_bb = _conv_bwd(z, dca, dab, W["ev_a_conv_w"][0], W["ev_b_conv_w"][0], W["ev_b_conv_b"],
                                     "ev_conv_bwd")
    grads.update(ev_a_ln_g=d_lng, ev_a_ln_b=d_lnb, ev_a_conv_b=d_ba, ev_b_conv_b=d_bb,
                 ev_a_conv_w=d_wa[None], ev_b_conv_w=d_wb[None])
    token = comm.grads("ev", [_mm_tn(dz, n0, "ev_dwin", tn=512), d_ev_out])
    grad_x, d_gmix0 = _mm([(dz, ev_w_in_t, "nn")], "ev_dn", rms_bwd=(h0, _row(W["g_mix"][0]), dh1), tm=512, after=token)

    grads["g_mix"] = jnp.concatenate([d_gmix0, d_gmix1], axis=0)
    for key in ("g_xattn", "g_mem"):
        grads[key] = jnp.concatenate([g_xa0[key], g_xa1[key]], axis=0)
    grads["g_ffn"] = jnp.concatenate([g_ff0["g_ffn"], g_ff1["g_ffn"]], axis=0)
    return loss_row, grad_x, grads


class _Exchanges:
    def __init__(self, shards, dev_idx):
        self.shards, self.dev_idx = shards, dev_idx
        self.gathering, self.scattering = {}, {}
        rows = self._rows("ev")
        self.ev = _all_gather_weights(self._pack("ev"), rows, "ag_ev")
        token = self._start_gather("xa0", self.ev[0])
        self.first_token = self._start_gather("ffn0", token)

    def _rows(self, group):
        return [self.shards[e].shape[0] for e in GROUPS[group]]

    def _pack(self, group):
        return jnp.concatenate([self.shards[e] for e in GROUPS[group]], axis=0)

    def _start_gather(self, group, after):
        rows = self._rows(group)
        pack = self._pack(group)
        lands = [lax.empty((NDEV * r, pack.shape[1]), pack.dtype) for r in rows]
        plan, n = _gather_plan(rows)
        send, recv, srcs, lands, token = _split_start([pack], lands, plan, n, after, f"ag_{group}_start")
        self.gathering[group] = (send, recv, srcs, lands, plan, rows)
        return token

    def weights(self, group, after):
        if group == "ev":
            return self.ev, self.first_token
        send, recv, srcs, lands, plan, rows = self.gathering.pop(group)
        srcs, lands = _split_wait(send, recv, srcs, lands, plan, after, f"ag_{group}_wait")
        fulls = _gather_finish(srcs[0], lands, rows, f"ag_{group}_finish")
        token = None
        if group == "xa0":
            token = self._start_gather("xa1", self._start_gather("od", fulls[0]))
        elif group == "ffn0":
            token = self._start_gather("ffn1", fulls[0])
        return fulls, token

    def grads(self, group, dws):
        rows = self._rows(group)
        lands = [lax.empty((NDEV - 1, r, d.shape[1]), d.dtype) for r, d in zip(rows, dws)]
        plan, n = _scatter_plan(rows)
        send, recv, srcs, lands, token = _split_start(dws, lands, plan, n, None, f"rs_{group}_start")
        self.scattering[group] = (send, recv, srcs, lands, plan)
        return token

    def reduced(self, after):
        out = {}
        for group, (send, recv, srcs, lands, plan) in self.scattering.items():
            srcs, lands = _split_wait(send, recv, srcs, lands, plan, after, f"rs_{group}_wait")
            for (n, i), g, got in zip(GROUPS[group], srcs, lands):
                out[n, i] = _scatter_sum(g, got, self.dev_idx, f"rs_sum_{n}_{i}")
        return out


def kernel(x, mem, g_mix, g_xattn, g_mem, g_ffn, g_final, ev_w_in, ev_a_conv_w, ev_a_conv_b, ev_a_ln_g, ev_a_ln_b, ev_b_conv_w, ev_b_conv_b, ev_w_out, od_w_in, od_c_ln_g, od_c_ln_b, od_w_s, od_b_s, od_w_out, xa_w_q, xa_w_k, xa_w_v, xa_w_o, ffn_w_gate, ffn_w_up, ffn_w_down, loss_target, m_g_mix, m_g_xattn, m_g_mem, m_g_ffn, m_g_final, m_ev_w_in, m_ev_a_conv_w, m_ev_a_conv_b, m_ev_a_ln_g, m_ev_a_ln_b, m_ev_b_conv_w, m_ev_b_conv_b, m_ev_w_out, m_od_w_in, m_od_c_ln_g, m_od_c_ln_b, m_od_w_s, m_od_b_s, m_od_w_out, m_xa_w_q, m_xa_w_k, m_xa_w_v, m_xa_w_o, m_ffn_w_gate, m_ffn_w_up, m_ffn_w_down, v_g_mix, v_g_xattn, v_g_mem, v_g_ffn, v_g_final, v_ev_w_in, v_ev_a_conv_w, v_ev_a_conv_b, v_ev_a_ln_g, v_ev_a_ln_b, v_ev_b_conv_w, v_ev_b_conv_b, v_ev_w_out, v_od_w_in, v_od_c_ln_g, v_od_c_ln_b, v_od_w_s, v_od_b_s, v_od_w_out, v_xa_w_q, v_xa_w_k, v_xa_w_v, v_xa_w_o, v_ffn_w_gate, v_ffn_w_up, v_ffn_w_down):
    local = dict(g_mix=g_mix, g_xattn=g_xattn, g_mem=g_mem, g_ffn=g_ffn, g_final=g_final, ev_w_in=ev_w_in, ev_a_conv_w=ev_a_conv_w, ev_a_conv_b=ev_a_conv_b, ev_a_ln_g=ev_a_ln_g, ev_a_ln_b=ev_a_ln_b, ev_b_conv_w=ev_b_conv_w, ev_b_conv_b=ev_b_conv_b, ev_w_out=ev_w_out, od_w_in=od_w_in, od_c_ln_g=od_c_ln_g, od_c_ln_b=od_c_ln_b, od_w_s=od_w_s, od_b_s=od_b_s, od_w_out=od_w_out, xa_w_q=xa_w_q, xa_w_k=xa_w_k, xa_w_v=xa_w_v, xa_w_o=xa_w_o, ffn_w_gate=ffn_w_gate, ffn_w_up=ffn_w_up, ffn_w_down=ffn_w_down)
    mom = dict(g_mix=m_g_mix, g_xattn=m_g_xattn, g_mem=m_g_mem, g_ffn=m_g_ffn, g_final=m_g_final, ev_w_in=m_ev_w_in, ev_a_conv_w=m_ev_a_conv_w, ev_a_conv_b=m_ev_a_conv_b, ev_a_ln_g=m_ev_a_ln_g, ev_a_ln_b=m_ev_a_ln_b, ev_b_conv_w=m_ev_b_conv_w, ev_b_conv_b=m_ev_b_conv_b, ev_w_out=m_ev_w_out, od_w_in=m_od_w_in, od_c_ln_g=m_od_c_ln_g, od_c_ln_b=m_od_c_ln_b, od_w_s=m_od_w_s, od_b_s=m_od_b_s, od_w_out=m_od_w_out, xa_w_q=m_xa_w_q, xa_w_k=m_xa_w_k, xa_w_v=m_xa_w_v, xa_w_o=m_xa_w_o, ffn_w_gate=m_ffn_w_gate, ffn_w_up=m_ffn_w_up, ffn_w_down=m_ffn_w_down)
    vel = dict(g_mix=v_g_mix, g_xattn=v_g_xattn, g_mem=v_g_mem, g_ffn=v_g_ffn, g_final=v_g_final, ev_w_in=v_ev_w_in, ev_a_conv_w=v_ev_a_conv_w, ev_a_conv_b=v_ev_a_conv_b, ev_a_ln_g=v_ev_a_ln_g, ev_a_ln_b=v_ev_a_ln_b, ev_b_conv_w=v_ev_b_conv_w, ev_b_conv_b=v_ev_b_conv_b, ev_w_out=v_ev_w_out, od_w_in=v_od_w_in, od_c_ln_g=v_od_c_ln_g, od_c_ln_b=v_od_c_ln_b, od_w_s=v_od_w_s, od_b_s=v_od_b_s, od_w_out=v_od_w_out, xa_w_q=v_xa_w_q, xa_w_k=v_xa_w_k, xa_w_v=v_xa_w_v, xa_w_o=v_xa_w_o, ffn_w_gate=v_ffn_w_gate, ffn_w_up=v_ffn_w_up, ffn_w_down=v_ffn_w_down)
    D = x.shape[-1]
    dev = 4 * lax.axis_index("x") + 2 * lax.axis_index("y") + lax.axis_index("c")

    def comm_layout(n, a):
        return jnp.transpose(a) if _shard_axis(n) == 2 else a

    shards = {(n, i): comm_layout(n, local[n][i]).astype(BF16) for n in BIG for i in range(local[n].shape[0])}
    small_sizes = [local[n].size for n in SMALL_SHARDED]
    small_block = _pad_rows(jnp.concatenate([local[n].reshape(-1) for n in SMALL_SHARDED]), 128, 8)
    small_all = _all_gather(small_block, "ag_small").reshape(NDEV, -1)
    comm = _Exchanges(shards, jnp.reshape(dev, (1,)).astype(jnp.int32))

    W = {n: local[n] for n in REPLICATED}
    o0 = 0
    for n, sz in zip(SMALL_SHARDED, small_sizes):
        blocks = small_all[:, o0:o0 + sz].reshape((NDEV,) + local[n].shape)
        W[n] = _full_from_blocks(blocks, _shard_axis(n))
        o0 += sz

    loss_row, grad_x, grads = _local_step(x[0], mem[0], loss_target[0], W, comm)

    reduced = comm.reduced(grad_x)
    gsh = {n: jnp.stack([comm_layout(n, reduced[n, i]) for i in range(local[n].shape[0])]) for n in BIG}

    rest = REPLICATED + SMALL_SHARDED
    rest_full_shapes = [grads[n].shape for n in rest]
    g_rest = _pad_rows(jnp.concatenate([grads[n].astype(F32).reshape(-1) for n in rest]), D, 8)
    g_rest = _sum_slots(_all_gather(g_rest, "ag_small_grads"), "sum_small_grads").reshape(-1)
    o0 = 0
    for n, shp in zip(rest, rest_full_shapes):
        sz = 1
        for s in shp:
            sz *= s
        full = g_rest[o0:o0 + sz].reshape(shp)
        o0 += sz
        if n in SMALL_SHARDED:
            full = lax.dynamic_index_in_dim(_blocks_from_full(full, _shard_axis(n)), dev, 0, keepdims=False)
        gsh[n] = full.reshape(local[n].shape)

    delta, new_m, new_v = {}, {}, {}
    for n in WEIGHTS:
        delta[n], new_m[n], new_v[n] = _adamw(local[n], gsh[n], mom[n], vel[n], f"adamw_{n}")

    loss = lax.psum(loss_row[0, 0], ("x", "y", "c"))
    return (loss, grad_x[None], *[gsh[n] for n in WEIGHTS], *[delta[n] for n in WEIGHTS],
            *[new_m[n] for n in WEIGHTS], *[new_v[n] for n in WEIGHTS])
```

```python
import jax
import jax.numpy as jnp
from jax import lax
from jax.experimental import pallas as pl
from jax.experimental.pallas import tpu as pltpu

F32, BF16 = jnp.float32, jnp.bfloat16
NDEV = 8
RMS_EPS = 1e-6
LN_EPS = 1e-5
CHUNK = 128
C_GROUPS = 8
XA_HEADS = 4
ADAM_LR, ADAM_B1, ADAM_B2, ADAM_EPS, ADAM_WD, ADAM_STEP = 0.001, 0.9, 0.999, 1e-08, 0.01, 10
HALO = 16
ROW_CHUNK = 32
V7X_VMEM_LIMIT = 56 * 1024 * 1024
MESH = pl.DeviceIdType.MESH

TS_ROW = 512
TS_MM = 1024
TN_MM = 1408
TS_FFN = 256
TS_CONV = 512
TS_SGU = 512
TS_ATTN = 512


def _cp(*sem):
    return pltpu.CompilerParams(dimension_semantics=sem, vmem_limit_bytes=V7X_VMEM_LIMIT)


def _pick(n, pref, align):
    for t in range(min(n, pref), 0, -1):
        if n % t == 0 and (t % align == 0 or t == n):
            return t
    return n


def _sigmoid(x):
    return 0.5 * jnp.tanh(0.5 * x) + 0.5


def _dot(a, b):
    return jnp.dot(a, b, preferred_element_type=F32)


def _dot_nt(a, b):
    return lax.dot_general(a, b, (((1,), (1,)), ((), ())), preferred_element_type=F32)


def _dot_tn(a, b):
    return lax.dot_general(a, b, (((0,), (0,)), ((), ())), preferred_element_type=F32)


_ANY = pl.BlockSpec(memory_space=pl.ANY)


def _after(after):
    return ([], []) if after is None else ([_ANY], [after])


def _rms_fwd(h, g, name, after=None):
    S, D = h.shape
    ts = _pick(S, TS_MM, 16)
    after_specs, after_ops = _after(after)

    def body(h_ref, g_ref, *rest):
        o_ref = rest[-1]
        x = h_ref[...]
        r = lax.rsqrt(jnp.mean(x * x, axis=-1, keepdims=True) + RMS_EPS)
        o_ref[...] = ((x * r) * g_ref[...]).astype(o_ref.dtype)

    return pl.pallas_call(
        body, grid=(S // ts,),
        in_specs=[pl.BlockSpec((ts, D), lambda i: (i, 0)), pl.BlockSpec((1, D), lambda i: (0, 0))] + after_specs,
        out_specs=pl.BlockSpec((ts, D), lambda i: (i, 0)),
        out_shape=jax.ShapeDtypeStruct((S, D), BF16), compiler_params=_cp("parallel"), name=name)(h, g, *after_ops)


def _rms_bwd(dn, h, g, dres, name):
    S, D = h.shape
    ts = _pick(S, TS_ROW, 8)
    has_res = dres is not None

    def body(*refs):
        if has_res:
            dn_ref, h_ref, g_ref, dres_ref, dh_ref, dg_ref = refs
        else:
            dn_ref, h_ref, g_ref, dh_ref, dg_ref = refs
        x = h_ref[...]
        dn_ = dn_ref[...].astype(F32)
        r = lax.rsqrt(jnp.mean(x * x, axis=-1, keepdims=True) + RMS_EPS)
        xr = x * r

        @pl.when(pl.program_id(0) == 0)
        def _():
            dg_ref[...] = jnp.zeros_like(dg_ref)

        dg_ref[...] += jnp.sum(dn_ * xr, axis=0, keepdims=True)
        u = dn_ * g_ref[...]
        dh = r * u - xr * (r * jnp.mean(u * xr, axis=-1, keepdims=True))
        if has_res:
            dh = dh + dres_ref[...]
        dh_ref[...] = dh

    tile = pl.BlockSpec((ts, D), lambda i: (i, 0))
    vec = pl.BlockSpec((1, D), lambda i: (0, 0))
    ins = [dn, h, g] + ([dres] if has_res else [])
    return pl.pallas_call(
        body, grid=(S // ts,),
        in_specs=[tile, tile, vec] + ([tile] if has_res else []),
        out_specs=[tile, vec],
        out_shape=[jax.ShapeDtypeStruct((S, D), F32), jax.ShapeDtypeStruct((1, D), F32)],
        compiler_params=_cp("arbitrary"), name=name)(*ins)


def _loss_bwd(h, g, target, name):
    S, D = h.shape
    ts = _pick(S, TS_ROW, 8)

    def body(h_ref, g_ref, t_ref, loss_ref, dh_ref, dg_ref):
        x = h_ref[...]
        r = lax.rsqrt(jnp.mean(x * x, axis=-1, keepdims=True) + RMS_EPS)
        xr = x * r
        gg = g_ref[...]
        e = xr * gg - t_ref[...]

        @pl.when(pl.program_id(0) == 0)
        def _():
            dg_ref[...] = jnp.zeros_like(dg_ref)
            loss_ref[...] = jnp.zeros_like(loss_ref)

        tile_loss = jnp.sum(jnp.sum(e * e, axis=0, keepdims=True), axis=1, keepdims=True) * (0.5 / D)
        loss_ref[...] += jnp.broadcast_to(tile_loss, loss_ref.shape)
        dy = e * (1.0 / D)
        dg_ref[...] += jnp.sum(dy * xr, axis=0, keepdims=True)
        u = dy * gg
        dh_ref[...] = r * u - xr * (r * jnp.mean(u * xr, axis=-1, keepdims=True))

    tile = pl.BlockSpec((ts, D), lambda i: (i, 0))
    vec = pl.BlockSpec((1, D), lambda i: (0, 0))
    return pl.pallas_call(
        body, grid=(S // ts,),
        in_specs=[tile, vec, tile],
        out_specs=[pl.BlockSpec((1, 128), lambda i: (0, 0)), tile, vec],
        out_shape=[jax.ShapeDtypeStruct((1, 128), F32), jax.ShapeDtypeStruct((S, D), F32),
                   jax.ShapeDtypeStruct((1, D), F32)],
        compiler_params=_cp("arbitrary"), name=name)(h, g, target)


def _mm(pairs, name, out_dtype=F32, res=None, rms_g=None, rms_bwd=None, tm=None, tn=None, after=None):
    M = pairs[0][0].shape[0]
    N = pairs[0][1].shape[1 if pairs[0][2] == "nn" else 0]
    whole_rows = rms_g is not None or rms_bwd is not None
    tm = _pick(M, tm or TS_MM, 16)
    tn = N if whole_rows else _pick(N, tn or TN_MM, 128)
    npair = len(pairs)
    modes = [p[2] for p in pairs]
    after_specs, after_ops = _after(after)

    def body(*refs):
        acc = None
        for p in range(npair):
            a_ = refs[2 * p][...].astype(BF16)
            d = _dot(a_, refs[2 * p + 1][...]) if modes[p] == "nn" else _dot_nt(a_, refs[2 * p + 1][...])
            acc = d if acc is None else acc + d
        rest = refs[2 * npair + len(after_ops):]
        if res is not None:
            acc = acc + rest[0][...]
            rest = rest[1:]
        if rms_bwd is not None:
            h_ref, g_ref, dres_ref, dh_ref, dg_ref = rest
            x = h_ref[...]
            r = lax.rsqrt(jnp.mean(x * x, axis=-1, keepdims=True) + RMS_EPS)
            xr = x * r

            @pl.when(pl.program_id(0) == 0)
            def _():
                dg_ref[...] = jnp.zeros_like(dg_ref)

            dg_ref[...] += jnp.sum(acc * xr, axis=0, keepdims=True)
            u = acc * g_ref[...]
            dh_ref[...] = r * u - xr * (r * jnp.mean(u * xr, axis=-1, keepdims=True)) + dres_ref[...]
        elif rms_g is not None:
            g_ref, o_ref, n_ref = rest
            o_ref[...] = acc
            r = lax.rsqrt(jnp.mean(acc * acc, axis=-1, keepdims=True) + RMS_EPS)
            n_ref[...] = ((acc * r) * g_ref[...]).astype(BF16)
        else:
            rest[0][...] = acc.astype(rest[0].dtype)

    in_specs, ins = [], []
    for a, w, mode in pairs:
        K = a.shape[1]
        in_specs.append(pl.BlockSpec((tm, K), lambda i, j: (i, 0)))
        in_specs.append(pl.BlockSpec((K, tn), lambda i, j: (0, j)) if mode == "nn"
                        else pl.BlockSpec((tn, K), lambda i, j: (j, 0)))
        ins += [a, w]
    in_specs += after_specs
    ins += after_ops
    tile = pl.BlockSpec((tm, tn), lambda i, j: (i, j))
    vec = pl.BlockSpec((1, tn), lambda i, j: (0, j))
    if res is not None:
        in_specs.append(tile)
        ins.append(res)
    sem = ("parallel", "parallel")
    if rms_bwd is not None:
        in_specs += [tile, vec, tile]
        ins += list(rms_bwd)
        out_specs = [tile, vec]
        out_shape = [jax.ShapeDtypeStruct((M, N), F32), jax.ShapeDtypeStruct((1, N), F32)]
        sem = ("arbitrary", "arbitrary")
    elif rms_g is not None:
        in_specs.append(vec)
        ins.append(rms_g)
        out_specs = [tile, tile]
        out_shape = [jax.ShapeDtypeStruct((M, N), F32), jax.ShapeDtypeStruct((M, N), BF16)]
    else:
        out_specs = tile
        out_shape = jax.ShapeDtypeStruct((M, N), out_dtype)
    return pl.pallas_call(
        body, grid=(M // tm, N // tn), in_specs=in_specs, out_specs=out_specs, out_shape=out_shape,
        compiler_params=_cp(*sem), name=name)(*ins)


def _mm_tn(a, b, name, ts=None, tn=None):
    S, K = a.shape
    N = b.shape[1]
    ts = _pick(S, ts or TS_MM, 16)
    tn = _pick(N, tn or TN_MM, 128)
    nsteps = S // ts

    def body(a_ref, b_ref, o_ref, acc_ref):
        s = pl.program_id(1)

        @pl.when(s == 0)
        def _():
            acc_ref[...] = jnp.zeros_like(acc_ref)

        acc_ref[...] += _dot_tn(a_ref[...].astype(BF16), b_ref[...].astype(BF16))

        @pl.when(s == nsteps - 1)
        def _():
            o_ref[...] = acc_ref[...].astype(o_ref.dtype)

    return pl.pallas_call(
        body, grid=(N // tn, nsteps),
        in_specs=[pl.BlockSpec((ts, K), lambda j, s: (s, 0)), pl.BlockSpec((ts, tn), lambda j, s: (s, j))],
        out_specs=pl.BlockSpec((K, tn), lambda j, s: (0, j)), out_shape=jax.ShapeDtypeStruct((K, N), BF16),
        scratch_shapes=[pltpu.VMEM((K, tn), F32)],
        compiler_params=_cp("parallel", "arbitrary"), name=name)(a, b)


def _col_chunk(n):
    return 256 if n % 256 == 0 else 128


def _ffn_up(n, wgt, wut, name, after=None):
    S, D = n.shape
    F = wgt.shape[0]
    tm = _pick(S, TS_FFN, 16)
    ce = _col_chunk(F)
    after_specs, after_ops = _after(after)

    def body(n_ref, wg_ref, wu_ref, *rest):
        a_ref, b_ref, hid_ref = rest[-3:]
        x = n_ref[...]
        for c0 in range(0, F, ce):
            a = _dot_nt(x, wg_ref[c0:c0 + ce, :])
            b = _dot_nt(x, wu_ref[c0:c0 + ce, :])
            a_ref[:, c0:c0 + ce] = a.astype(BF16)
            b_ref[:, c0:c0 + ce] = b.astype(BF16)
            hid_ref[:, c0:c0 + ce] = (a * _sigmoid(a) * b).astype(BF16)

    wspec = pl.BlockSpec((F, D), lambda i: (0, 0))
    ospec = pl.BlockSpec((tm, F), lambda i: (i, 0))
    osh = jax.ShapeDtypeStruct((S, F), BF16)
    return pl.pallas_call(
        body, grid=(S // tm,),
        in_specs=[pl.BlockSpec((tm, D), lambda i: (i, 0)), wspec, wspec] + after_specs,
        out_specs=[ospec, ospec, ospec], out_shape=[osh, osh, osh],
        compiler_params=_cp("parallel"), name=name)(n, wgt, wut, *after_ops)


def _ffn_dhid(dh, wd, a, b, name):
    S, D = dh.shape
    F = wd.shape[0]
    tm = _pick(S, TS_FFN, 16)
    ce = _col_chunk(F)

    def body(dh_ref, wd_ref, a_ref, b_ref, da_ref, db_ref):
        x = dh_ref[...].astype(BF16)
        for c0 in range(0, F, ce):
            g = _dot_nt(x, wd_ref[c0:c0 + ce, :])
            a_ = a_ref[:, c0:c0 + ce].astype(F32)
            b_ = b_ref[:, c0:c0 + ce].astype(F32)
            sg = _sigmoid(a_)
            da_ref[:, c0:c0 + ce] = (g * b_ * (sg * (1.0 + a_ * (1.0 - sg)))).astype(BF16)
            db_ref[:, c0:c0 + ce] = (g * (a_ * sg)).astype(BF16)

    tile = pl.BlockSpec((tm, F), lambda i: (i, 0))
    osh = jax.ShapeDtypeStruct((S, F), BF16)
    return pl.pallas_call(
        body, grid=(S // tm,),
        in_specs=[pl.BlockSpec((tm, D), lambda i: (i, 0)), pl.BlockSpec((F, D), lambda i: (0, 0)), tile, tile],
        out_specs=[tile, tile], out_shape=[osh, osh],
        compiler_params=_cp("parallel"), name=name)(dh, wd, a, b)


def _softmax_rows(s):
    m = jnp.max(s, axis=-1, keepdims=True)
    p = jnp.exp(s - m)
    return p / jnp.sum(p, axis=-1, keepdims=True)


def _attn_fwd(q, k, v, name):
    S, D = q.shape
    M = k.shape[0]
    hd = D // XA_HEADS
    scale = hd ** -0.5
    ts = _pick(S, TS_ATTN, 16)

    def body(q_ref, k_ref, v_ref, o_ref):
        for h in range(XA_HEADS):
            sl = slice(h * hd, (h + 1) * hd)
            p = _softmax_rows(_dot_nt(q_ref[:, sl], k_ref[:, sl]) * scale)
            o_ref[:, sl] = _dot(p.astype(BF16), v_ref[:, sl]).astype(BF16)

    tile = pl.BlockSpec((ts, D), lambda i: (i, 0))
    memspec = pl.BlockSpec((M, D), lambda i: (0, 0))
    return pl.pallas_call(
        body, grid=(S // ts,), in_specs=[tile, memspec, memspec], out_specs=tile,
        out_shape=jax.ShapeDtypeStruct((S, D), BF16), compiler_params=_cp("parallel"), name=name)(q, k, v)


def _attn_bwd(q, k, v, do, name):
    S, D = q.shape
    M = k.shape[0]
    hd = D // XA_HEADS
    scale = hd ** -0.5
    ts = _pick(S, TS_ATTN, 16)

    def body(q_ref, k_ref, v_ref, do_ref, dq_ref, dk_ref, dv_ref):
        @pl.when(pl.program_id(0) == 0)
        def _():
            dk_ref[...] = jnp.zeros_like(dk_ref)
            dv_ref[...] = jnp.zeros_like(dv_ref)

        for h in range(XA_HEADS):
            sl = slice(h * hd, (h + 1) * hd)
            qh, kh, vh, doh = q_ref[:, sl], k_ref[:, sl], v_ref[:, sl], do_ref[:, sl]
            p = _softmax_rows(_dot_nt(qh, kh) * scale)
            dp = _dot_nt(doh, vh)
            dv_ref[:, sl] += _dot_tn(p.astype(BF16), doh)
            delta = jnp.sum(dp * p, axis=-1, keepdims=True)
            ds = (p * (dp - delta) * scale).astype(BF16)
            dq_ref[:, sl] = _dot(ds, kh).astype(BF16)
            dk_ref[:, sl] += _dot_tn(ds, qh)

    tile = pl.BlockSpec((ts, D), lambda i: (i, 0))
    memspec = pl.BlockSpec((M, D), lambda i: (0, 0))
    return pl.pallas_call(
        body, grid=(S // ts,), in_specs=[tile, memspec, memspec, tile], out_specs=[tile, memspec, memspec],
        out_shape=[jax.ShapeDtypeStruct((S, D), BF16), jax.ShapeDtypeStruct((M, D), F32),
                   jax.ShapeDtypeStruct((M, D), F32)],
        compiler_params=_cp("arbitrary"), name=name)(q, k, v, do)


def _halo_specs(ts, width, col):
    per = ts // HALO

    def prev(i):
        return (jnp.maximum(i * per - 1, 0), col)

    def nxt(i, n_tiles):
        return (jnp.minimum((i + 1) * per, n_tiles * per - 1), col)

    return prev, nxt


def _fill_ext(ext_ref, prev_val, main_val, next_val, first, last, ts):
    ext_ref[pl.ds(0, HALO), :] = jnp.where(first, 0.0, prev_val)
    ext_ref[pl.ds(HALO, ts), :] = main_val
    ext_ref[pl.ds(HALO + ts, HALO), :] = jnp.where(last, 0.0, next_val)


def _conv_fwd(z, wa, ba, lng, lnb, wb, bb, name, after=None):
    S = z.shape[0]
    C = z.shape[1] // 5
    KA, KB = wa.shape[0], wb.shape[0]
    pa, pb = KA // 2, KB // 2
    assert pa <= HALO and pb <= HALO
    ts = _pick(S, TS_CONV, ROW_CHUNK)
    nt = S // ts
    rc = ROW_CHUNK
    prev, nxt = _halo_specs(ts, 5 * C, 0)
    after_specs, after_ops = _after(after)

    def body(*refs):
        compute(*refs[:9], *refs[9 + len(after_ops):])

    def compute(z_ref, zp_ref, zn_ref, wa_ref, ba_ref, lng_ref, lnb_ref, wb_ref, bb_ref, ab_ref, ca_ref,
                ga_ext, tb_ext, win_a, win_b):
        i = pl.program_id(0)
        first, last = i == 0, i == nt - 1

        def glu(r):
            return r[:, 0:C] * _sigmoid(r[:, C:2 * C])

        def gcb(r):
            return r[:, 4 * C:5 * C] * r[:, 2 * C:3 * C]

        _fill_ext(ga_ext, glu(zp_ref), glu(z_ref), glu(zn_ref), first, last, ts)
        _fill_ext(tb_ext, gcb(zp_ref), gcb(z_ref), gcb(zn_ref), first, last, ts)

        def chunk(c, carry):
            r0 = pl.multiple_of(c * rc, rc)
            win_a[...] = ga_ext[pl.ds(r0, rc + 2 * HALO), :]
            win_b[...] = tb_ext[pl.ds(r0, rc + 2 * HALO), :]
            acc = jnp.zeros((rc, C), F32)
            for k in range(KA):
                acc = acc + wa_ref[k:k + 1, :] * win_a[pl.ds(HALO - pa + k, rc), :]
            ca = acc + ba_ref[...]
            ca_ref[pl.ds(r0, rc), :] = ca
            mu = jnp.mean(ca, axis=-1, keepdims=True)
            xc = ca - mu
            var = jnp.mean(xc * xc, axis=-1, keepdims=True)
            ln = xc * lax.rsqrt(var + LN_EPS) * lng_ref[...] + lnb_ref[...]
            ab_ref[pl.ds(r0, rc), 0:C] = (ln * _sigmoid(ln)).astype(BF16)
            cb = jnp.zeros((rc, C), F32) + bb_ref[...]
            for k in range(KB):
                cb = cb + wb_ref[k:k + 1, :] * win_b[pl.ds(HALO - pb + k, rc), :]
            ab_ref[pl.ds(r0, rc), C:2 * C] = (z_ref[pl.ds(r0, rc), 3 * C:4 * C] * cb).astype(BF16)
            return carry

        lax.fori_loop(0, ts // rc, chunk, 0)

    zspec = pl.BlockSpec((ts, 5 * C), lambda i: (i, 0))
    zprev = pl.BlockSpec((HALO, 5 * C), prev)
    znext = pl.BlockSpec((HALO, 5 * C), lambda i: nxt(i, nt))

    def full(a):
        return pl.BlockSpec(a.shape, lambda i: (0, 0))

    return pl.pallas_call(
        body, grid=(nt,),
        in_specs=[zspec, zprev, znext, full(wa), full(ba), full(lng), full(lnb), full(wb), full(bb)] + after_specs,
        out_specs=[pl.BlockSpec((ts, 2 * C), lambda i: (i, 0)), pl.BlockSpec((ts, C), lambda i: (i, 0))],
        out_shape=[jax.ShapeDtypeStruct((S, 2 * C), BF16), jax.ShapeDtypeStruct((S, C), F32)],
        scratch_shapes=[pltpu.VMEM((ts + 2 * HALO, C), F32), pltpu.VMEM((ts + 2 * HALO, C), F32),
                        pltpu.VMEM((rc + 2 * HALO, C), F32), pltpu.VMEM((rc + 2 * HALO, C), F32)],
        compiler_params=_cp("parallel"), name=name)(z, z, z, wa, ba, lng, lnb, wb, bb, *after_ops)


def _conv_bwd_ln(dab, ca, lng, lnb, name):
    S, C = ca.shape
    ts = _pick(S, TS_ROW, 8)

    def body(da_ref, ca_ref, lng_ref, lnb_ref, dca_ref, dg_ref, db_ref, dbias_ref):
        @pl.when(pl.program_id(0) == 0)
        def _():
            dg_ref[...] = jnp.zeros_like(dg_ref)
            db_ref[...] = jnp.zeros_like(db_ref)
            dbias_ref[...] = jnp.zeros_like(dbias_ref)

        ca_ = ca_ref[...]
        mu = jnp.mean(ca_, axis=-1, keepdims=True)
        xc = ca_ - mu
        rstd = lax.rsqrt(jnp.mean(xc * xc, axis=-1, keepdims=True) + LN_EPS)
        xh = xc * rstd
        ln = xh * lng_ref[...] + lnb_ref[...]
        sg = _sigmoid(ln)
        dln = da_ref[...].astype(F32) * (sg * (1.0 + ln * (1.0 - sg)))
        dg_ref[...] += jnp.sum(dln * xh, axis=0, keepdims=True)
        db_ref[...] += jnp.sum(dln, axis=0, keepdims=True)
        dxh = dln * lng_ref[...]
        dca = rstd * (dxh - jnp.mean(dxh, axis=-1, keepdims=True) - xh * jnp.mean(dxh * xh, axis=-1, keepdims=True))
        dca_ref[...] = dca
        dbias_ref[...] += jnp.sum(dca, axis=0, keepdims=True)

    tile = pl.BlockSpec((ts, C), lambda i: (i, 0))
    vec = pl.BlockSpec((1, C), lambda i: (0, 0))
    vsh = jax.ShapeDtypeStruct((1, C), F32)
    return pl.pallas_call(
        body, grid=(S // ts,), in_specs=[tile, tile, vec, vec], out_specs=[tile, vec, vec, vec],
        out_shape=[jax.ShapeDtypeStruct((S, C), F32), vsh, vsh, vsh],
        compiler_params=_cp("arbitrary"), name=name)(dab, ca, lng, lnb)


def _conv_bwd(z, dca, dab, wa, wb, bb, name):
    S = z.shape[0]
    C = z.shape[1] // 5
    KA, KB = wa.shape[0], wb.shape[0]
    pa, pb = KA // 2, KB // 2
    ts = _pick(S, TS_CONV, ROW_CHUNK)
    nt = S // ts
    rc = ROW_CHUNK
    prev0, nxt0 = _halo_specs(ts, C, 0)
    prev1, nxt1 = _halo_specs(ts, C, 1)

    def body(z_ref, zp_ref, zn_ref, dca_ref, dcap_ref, dcan_ref, db_ref, dbp_ref, dbn_ref, wa_ref, wb_ref, bb_ref,
             dz_ref, dwa_ref, dwb_ref, dbb_ref,
             ga_ext, dca_ext, tb_ext, dcb_ext, win_ga, win_dca, win_tb, win_dcb, acc_a, acc_b, acc_bias):
        i = pl.program_id(0)
        first, last = i == 0, i == nt - 1

        @pl.when(first)
        def _():
            acc_a[...] = jnp.zeros_like(acc_a)
            acc_b[...] = jnp.zeros_like(acc_b)
            acc_bias[...] = jnp.zeros_like(acc_bias)

        def glu(r):
            return r[:, 0:C] * _sigmoid(r[:, C:2 * C])

        def gcb(r):
            return r[:, 4 * C:5 * C] * r[:, 2 * C:3 * C]

        def dcb(d, r):
            return d[...].astype(F32) * r[:, 3 * C:4 * C]

        _fill_ext(ga_ext, glu(zp_ref), glu(z_ref), glu(zn_ref), first, last, ts)
        _fill_ext(tb_ext, gcb(zp_ref), gcb(z_ref), gcb(zn_ref), first, last, ts)
        _fill_ext(dca_ext, dcap_ref[...], dca_ref[...], dcan_ref[...], first, last, ts)
        _fill_ext(dcb_ext, dcb(dbp_ref, zp_ref), dcb(db_ref, z_ref), dcb(dbn_ref, zn_ref), first, last, ts)

        def fold(x):
            return jnp.sum(x.reshape(rc // 8, 8, C), axis=0)

        def chunk(c, carry):
            r0 = pl.multiple_of(c * rc, rc)
            win_ga[...] = ga_ext[pl.ds(r0, rc + 2 * HALO), :]
            win_dca[...] = dca_ext[pl.ds(r0, rc + 2 * HALO), :]
            win_tb[...] = tb_ext[pl.ds(r0, rc + 2 * HALO), :]
            win_dcb[...] = dcb_ext[pl.ds(r0, rc + 2 * HALO), :]
            dca_c = win_dca[pl.ds(HALO, rc), :]
            dglu = jnp.zeros((rc, C), F32)
            for k in range(KA):
                dglu = dglu + wa_ref[k:k + 1, :] * win_dca[pl.ds(HALO + pa - k, rc), :]
                acc_a[k] += fold(dca_c * win_ga[pl.ds(HALO - pa + k, rc), :])
            val = z_ref[pl.ds(r0, rc), 0:C]
            sg = _sigmoid(z_ref[pl.ds(r0, rc), C:2 * C])
            dz_ref[pl.ds(r0, rc), 0:C] = (dglu * sg).astype(BF16)
            dz_ref[pl.ds(r0, rc), C:2 * C] = (dglu * val * sg * (1.0 - sg)).astype(BF16)
            dcb_c = win_dcb[pl.ds(HALO, rc), :]
            cb = jnp.zeros((rc, C), F32) + bb_ref[...]
            dt = jnp.zeros((rc, C), F32)
            for k in range(KB):
                tb_k = win_tb[pl.ds(HALO - pb + k, rc), :]
                cb = cb + wb_ref[k:k + 1, :] * tb_k
                dt = dt + wb_ref[k:k + 1, :] * win_dcb[pl.ds(HALO + pb - k, rc), :]
                acc_b[k] += fold(dcb_c * tb_k)
            acc_bias[...] += fold(dcb_c)
            db_c = db_ref[pl.ds(r0, rc), :].astype(F32)
            dz_ref[pl.ds(r0, rc), 2 * C:3 * C] = (dt * z_ref[pl.ds(r0, rc), 4 * C:5 * C]).astype(BF16)
            dz_ref[pl.ds(r0, rc), 3 * C:4 * C] = (db_c * cb).astype(BF16)
            dz_ref[pl.ds(r0, rc), 4 * C:5 * C] = (dt * z_ref[pl.ds(r0, rc), 2 * C:3 * C]).astype(BF16)
            return carry

        lax.fori_loop(0, ts // rc, chunk, 0)

        @pl.when(last)
        def _():
            dwa_ref[...] = jnp.sum(acc_a[...], axis=1)
            dwb_ref[...] = jnp.sum(acc_b[...], axis=1)
            dbb_ref[...] = jnp.sum(acc_bias[...], axis=0, keepdims=True)

    zspec = pl.BlockSpec((ts, 5 * C), lambda i: (i, 0))
    zprev = pl.BlockSpec((HALO, 5 * C), prev0)
    znext = pl.BlockSpec((HALO, 5 * C), lambda i: nxt0(i, nt))
    dspec = pl.BlockSpec((ts, C), lambda i: (i, 0))
    dprev = pl.BlockSpec((HALO, C), prev0)
    dnext = pl.BlockSpec((HALO, C), lambda i: nxt0(i, nt))
    bspec = pl.BlockSpec((ts, C), lambda i: (i, 1))
    bprev = pl.BlockSpec((HALO, C), prev1)
    bnext = pl.BlockSpec((HALO, C), lambda i: nxt1(i, nt))

    def full(shape):
        return pl.BlockSpec(shape, lambda i: (0,) * len(shape))

    ext = pltpu.VMEM((ts + 2 * HALO, C), F32)
    win = pltpu.VMEM((rc + 2 * HALO, C), F32)
    return pl.pallas_call(
        body, grid=(nt,),
        in_specs=[zspec, zprev, znext, dspec, dprev, dnext, bspec, bprev, bnext,
                  full(wa.shape), full(wb.shape), full(bb.shape)],
        out_specs=[pl.BlockSpec((ts, 5 * C), lambda i: (i, 0)), full((KA, C)), full((KB, C)), full((1, C))],
        out_shape=[jax.ShapeDtypeStruct((S, 5 * C), BF16), jax.ShapeDtypeStruct((KA, C), F32),
                   jax.ShapeDtypeStruct((KB, C), F32), jax.ShapeDtypeStruct((1, C), F32)],
        scratch_shapes=[ext, ext, ext, ext, win, win, win, win,
                        pltpu.VMEM((KA, 8, C), F32), pltpu.VMEM((KB, 8, C), F32), pltpu.VMEM((8, C), F32)],
        compiler_params=_cp("arbitrary"), name=name)(z, z, z, dca, dca, dca, dab, dab, dab, wa, wb, bb)


_GELU_C = 0.7978845608028654
_GELU_A = 0.044715


def _gelu(x):
    return 0.5 * x * (1.0 + jnp.tanh(_GELU_C * (x + _GELU_A * (x * x * x))))


def _gelu_grad(x):
    t = jnp.tanh(_GELU_C * (x + _GELU_A * (x * x * x)))
    return 0.5 * (1.0 + t) + 0.5 * x * (1.0 - t * t) * (_GELU_C * (1.0 + 3.0 * _GELU_A * x * x))


def _sgu_fwd(zp, lng, lnb, ws, bsb, name):
    S = zp.shape[0]
    D = zp.shape[1] // 2
    G = ws.shape[0]
    gd = D // G
    ts = _pick(S, TS_SGU, CHUNK)
    ncs = ts // CHUNK

    def body(zp_ref, lng_ref, lnb_ref, ws_ref, bsb_ref, y_ref, vb_ref):
        v = _gelu(zp_ref[:, D:2 * D])
        mu = jnp.mean(v, axis=-1, keepdims=True)
        xc = v - mu
        rstd = lax.rsqrt(jnp.mean(xc * xc, axis=-1, keepdims=True) + LN_EPS)
        vb_ref[...] = (xc * rstd * lng_ref[...] + lnb_ref[...]).astype(BF16)
        for c in range(ncs):
            rows = slice(c * CHUNK, (c + 1) * CHUNK)
            for g in range(G):
                cols = slice(g * gd, (g + 1) * gd)
                sv = _dot(ws_ref[g], vb_ref[rows, cols]) + bsb_ref[:, cols]
                y_ref[rows, cols] = (_gelu(zp_ref[rows, cols]) * sv).astype(BF16)

    def full(a):
        return pl.BlockSpec(a.shape, lambda i: (0,) * a.ndim)

    return pl.pallas_call(
        body, grid=(S // ts,),
        in_specs=[pl.BlockSpec((ts, 2 * D), lambda i: (i, 0)), full(lng), full(lnb), full(ws), full(bsb)],
        out_specs=pl.BlockSpec((ts, D), lambda i: (i, 0)), out_shape=jax.ShapeDtypeStruct((S, D), BF16),
        scratch_shapes=[pltpu.VMEM((ts, D), BF16)],
        compiler_params=_cp("parallel"), name=name)(zp, lng, lnb, ws, bsb)


def _sgu_bwd(dy, zp, lng, lnb, ws, wst, bsb, name):
    S = zp.shape[0]
    D = zp.shape[1] // 2
    G = ws.shape[0]
    gd = D // G
    ts = _pick(S, TS_SGU, CHUNK)
    ncs = ts // CHUNK

    def body(dy_ref, zp_ref, lng_ref, lnb_ref, ws_ref, wst_ref, bsb_ref,
             dzp_ref, dws_ref, dbs_ref, dg_ref, db_ref, vb_ref, dvln_ref, acc_bs):
        i = pl.program_id(0)

        @pl.when(i == 0)
        def _():
            dws_ref[...] = jnp.zeros_like(dws_ref)
            acc_bs[...] = jnp.zeros_like(acc_bs)
            dg_ref[...] = jnp.zeros_like(dg_ref)
            db_ref[...] = jnp.zeros_like(db_ref)

        zv = zp_ref[:, D:2 * D]
        v = _gelu(zv)
        mu = jnp.mean(v, axis=-1, keepdims=True)
        xc = v - mu
        rstd = lax.rsqrt(jnp.mean(xc * xc, axis=-1, keepdims=True) + LN_EPS)
        xh = xc * rstd
        vb_ref[...] = (xh * lng_ref[...] + lnb_ref[...]).astype(BF16)
        for c in range(ncs):
            rows = slice(c * CHUNK, (c + 1) * CHUNK)
            for g in range(G):
                cols = slice(g * gd, (g + 1) * gd)
                zu = zp_ref[rows, cols]
                u = _gelu(zu)
                dy_ = dy_ref[rows, cols].astype(F32)
                sv = _dot(ws_ref[g], vb_ref[rows, cols]) + bsb_ref[:, cols]
                dzp_ref[rows, cols] = (dy_ * sv * _gelu_grad(zu)).astype(BF16)
                dsv = dy_ * u
                acc_bs[:, cols] += dsv
                dsvb = dsv.astype(BF16)
                dws_ref[g] += _dot_nt(dsvb, vb_ref[rows, cols])
                dvln_ref[rows, cols] = _dot(wst_ref[g], dsvb)
        dvln = dvln_ref[...]
        dg_ref[...] += jnp.sum(dvln * xh, axis=0, keepdims=True)
        db_ref[...] += jnp.sum(dvln, axis=0, keepdims=True)
        dxh = dvln * lng_ref[...]
        dv = rstd * (dxh - jnp.mean(dxh, axis=-1, keepdims=True) - xh * jnp.mean(dxh * xh, axis=-1, keepdims=True))
        dzp_ref[:, D:2 * D] = (dv * _gelu_grad(zv)).astype(BF16)

        @pl.when(i == pl.num_programs(0) - 1)
        def _():
            dbs_ref[...] = acc_bs[...]

    def full(shape):
        return pl.BlockSpec(shape, lambda i: (0,) * len(shape))

    return pl.pallas_call(
        body, grid=(S // ts,),
        in_specs=[pl.BlockSpec((ts, D), lambda i: (i, 0)), pl.BlockSpec((ts, 2 * D), lambda i: (i, 0)),
                  full(lng.shape), full(lnb.shape), full(ws.shape), full(wst.shape), full(bsb.shape)],
        out_specs=[pl.BlockSpec((ts, 2 * D), lambda i: (i, 0)), full(ws.shape), full(bsb.shape),
                   full((1, D)), full((1, D))],
        out_shape=[jax.ShapeDtypeStruct((S, 2 * D), BF16), jax.ShapeDtypeStruct(ws.shape, F32),
                   jax.ShapeDtypeStruct(bsb.shape, F32), jax.ShapeDtypeStruct((1, D), F32),
                   jax.ShapeDtypeStruct((1, D), F32)],
        scratch_shapes=[pltpu.VMEM((ts, D), BF16), pltpu.VMEM((ts, D), F32),
                        pltpu.VMEM(bsb.shape, F32)],
        compiler_params=_cp("arbitrary"), name=name)(dy, zp, lng, lnb, ws, wst, bsb)


def _group_sum(x, groups, name):
    P, D = x.shape
    gd = D // groups

    def body(x_ref, o_ref):
        for g in range(groups):
            o_ref[:, g:g + 1] = jnp.sum(x_ref[:, g * gd:(g + 1) * gd], axis=1, keepdims=True)

    return pl.pallas_call(body, out_shape=jax.ShapeDtypeStruct((P, groups), F32), name=name)(x)


def _adamw(w, g, m, v, name):
    shape = w.shape
    C = shape[-1]
    R = w.size // C
    tr = _pick(R, 1024, 8)
    bc1 = 1.0 - ADAM_B1 ** ADAM_STEP
    bc2 = 1.0 - ADAM_B2 ** ADAM_STEP

    def body(w_ref, g_ref, m_ref, v_ref, d_ref, nm_ref, nv_ref):
        g_ = g_ref[...]
        nm = ADAM_B1 * m_ref[...] + (1.0 - ADAM_B1) * g_
        nv = ADAM_B2 * v_ref[...] + (1.0 - ADAM_B2) * (g_ * g_)
        nm_ref[...] = nm
        nv_ref[...] = nv
        d_ref[...] = -ADAM_LR * ((nm / bc1) / (jnp.sqrt(nv / bc2) + ADAM_EPS) + ADAM_WD * w_ref[...])

    tile = pl.BlockSpec((tr, C), lambda i: (i, 0))
    sh = jax.ShapeDtypeStruct((R, C), F32)
    outs = pl.pallas_call(
        body, grid=(R // tr,), in_specs=[tile] * 4, out_specs=[tile] * 3, out_shape=[sh] * 3,
        compiler_params=_cp("parallel"), name=name)(*(a.reshape(R, C) for a in (w, g, m, v)))
    return tuple(o.reshape(shape) for o in outs)


_HBM = pl.BlockSpec(memory_space=pltpu.HBM)


def _remote(src, dst, send_sem, recv_sem, to):
    return pltpu.make_async_remote_copy(src_ref=src, dst_ref=dst, send_sem=send_sem, recv_sem=recv_sem,
                                        device_id=to, device_id_type=MESH)


def _all_gather(block, name):
    R, C = block.shape

    def body(x_ref, out_ref, send_sems, recv_sems, local_sem):
        x, y, c = lax.axis_index("x"), lax.axis_index("y"), lax.axis_index("c")
        me, sibling = (x, y, c), (x, y, 1 - c)
        chips = [(1 - x, y), (x, 1 - y), (1 - x, 1 - y)]

        def slot(px, py, pc):
            return out_ref.at[4 * px + 2 * py + pc]

        def copy(k, blk, to, src=None):
            return _remote(slot(*blk) if src is None else src, slot(*blk), send_sems.at[k], recv_sems.at[k], to)

        mine = pltpu.make_async_copy(x_ref, slot(*me), local_sem)
        mine.start()
        first = [copy(0, me, sibling, src=x_ref)]
        first += [copy(1 + j, me, (*chip, c), src=x_ref) for j, chip in enumerate(chips)]
        for cp in first:
            cp.start()
        passed = [copy(4 + j, (*chip, c), sibling) for j, chip in enumerate(chips)]
        for j, chip in enumerate(chips):
            copy(1 + j, (*chip, c), me).wait_recv()
            passed[j].start()
        copy(0, sibling, me).wait_recv()
        for j, chip in enumerate(chips):
            copy(4 + j, (*chip, 1 - c), me).wait_recv()
        for cp in first + passed:
            cp.wait_send()
        mine.wait()

    return pl.pallas_call(
        body, out_shape=jax.ShapeDtypeStruct((NDEV, R, C), block.dtype), in_specs=[_HBM], out_specs=_HBM,
        scratch_shapes=[pltpu.SemaphoreType.DMA((7,)), pltpu.SemaphoreType.DMA((7,)), pltpu.SemaphoreType.DMA],
        name=name)(block)


def _all_gather_weights(pack, rows, name):
    C = pack.shape[1]
    nw = len(rows)
    starts = [sum(rows[:w]) for w in range(nw)]

    def body(pack_ref, *rest):
        outs = rest[:nw]
        send_sems, recv_sems, local_sem = rest[nw:]
        x, y, c = lax.axis_index("x"), lax.axis_index("y"), lax.axis_index("c")
        me, sibling = (x, y, c), (x, y, 1 - c)
        chips = [(1 - x, y), (x, 1 - y), (1 - x, 1 - y)]

        def block(w, px, py, pc):
            return outs[w].at[pl.ds((4 * px + 2 * py + pc) * rows[w], rows[w])]

        def mine(w):
            return pack_ref.at[pl.ds(starts[w], rows[w])]

        def all_of(k):
            return _remote(pack_ref, pack_ref, send_sems.at[k], recv_sems.at[k], me)

        for w in range(nw):
            pltpu.make_async_copy(mine(w), block(w, *me), local_sem).start()
        for k, to in enumerate([sibling] + [(*chip, c) for chip in chips]):
            for w in range(nw):
                _remote(mine(w), block(w, *me), send_sems.at[k], recv_sems.at[k], to).start()
        for j, chip in enumerate(chips):
            all_of(1 + j).wait_recv()
            for w in range(nw):
                _remote(block(w, *chip, c), block(w, *chip, c), send_sems.at[4 + j], recv_sems.at[4 + j], sibling).start()
        all_of(0).wait_recv()
        for j in range(3):
            all_of(4 + j).wait_recv()
        for k in range(7):
            all_of(k).wait_send()
        pltpu.make_async_copy(pack_ref, pack_ref, local_sem).wait()

    return pl.pallas_call(
        body, out_shape=[jax.ShapeDtypeStruct((NDEV * r, C), pack.dtype) for r in rows],
        in_specs=[_HBM], out_specs=[_HBM] * nw,
        scratch_shapes=[pltpu.SemaphoreType.DMA((7,)), pltpu.SemaphoreType.DMA((7,)), pltpu.SemaphoreType.DMA],
        name=name)(pack)


_SEM = pl.BlockSpec(memory_space=pltpu.SEMAPHORE)
_DATAFLOW = pltpu.SideEffectType.DATAFLOW_SIDE_EFFECTING


def _split_start(srcs, lands, plan, n, after, name):
    nbuf = len(srcs) + len(lands)
    after_specs, after_ops = _after(after)

    def body(*refs):
        src_refs, land_refs = refs[:len(srcs)], refs[len(srcs):nbuf]
        send_sems, recv_sems = refs[nbuf + len(after_ops)], refs[nbuf + len(after_ops) + 1]
        for k, (src, dst, to) in enumerate(plan(src_refs, land_refs)):
            _remote(src, dst, send_sems.at[k], recv_sems.at[k], to).start()
        refs[-1][...] = jnp.zeros_like(refs[-1])

    bufs = [pltpu.with_memory_space_constraint(a, pltpu.HBM) for a in list(srcs) + list(lands)]
    outs = pl.pallas_call(
        body, name=name,
        out_shape=(pltpu.SemaphoreType.DMA((n,)), pltpu.SemaphoreType.DMA((n,)),
                   *[pltpu.HBM(a.shape, a.dtype) for a in bufs], jax.ShapeDtypeStruct((8, 128), F32)),
        in_specs=[_HBM] * nbuf + after_specs,
        out_specs=(_SEM, _SEM, *[_HBM] * nbuf, pl.BlockSpec(memory_space=pltpu.VMEM)),
        input_output_aliases={i: 2 + i for i in range(nbuf)},
        compiler_params=pltpu.CompilerParams(has_side_effects=_DATAFLOW))(*bufs, *after_ops)
    return outs[0], outs[1], list(outs[2:2 + len(srcs)]), list(outs[2 + len(srcs):2 + nbuf]), outs[-1]


def _split_wait(send_sems, recv_sems, srcs, lands, plan, after, name):
    nbuf = len(srcs) + len(lands)
    after_specs, after_ops = _after(after)

    def body(*refs):
        src_refs, land_refs = refs[:len(srcs)], refs[len(srcs):nbuf]
        send_sems_ref, recv_sems_ref = refs[nbuf], refs[nbuf + 1]
        for k, (src, dst, to) in enumerate(plan(src_refs, land_refs)):
            copy = _remote(src, dst, send_sems_ref.at[k], recv_sems_ref.at[k], to)
            copy.wait_send()
            copy.wait_recv()

    outs = pl.pallas_call(
        body, name=name, out_shape=tuple(pltpu.HBM(a.shape, a.dtype) for a in list(srcs) + list(lands)),
        in_specs=[_HBM] * nbuf + [_SEM, _SEM] + after_specs, out_specs=tuple([_HBM] * nbuf),
        input_output_aliases={i: i for i in range(nbuf)},
        compiler_params=pltpu.CompilerParams(has_side_effects=_DATAFLOW))(*srcs, *lands, send_sems, recv_sems, *after_ops)
    return list(outs[:len(srcs)]), list(outs[len(srcs):])


def _peers(x, y, c):
    return [(mask, (1 - x if mask & 4 else x, 1 - y if mask & 2 else y, 1 - c if mask & 1 else c))
            for mask in range(1, NDEV)]


def _gather_plan(rows):
    starts = [sum(rows[:w]) for w in range(len(rows))]

    def plan(src_refs, land_refs):
        x, y, c = lax.axis_index("x"), lax.axis_index("y"), lax.axis_index("c")
        copies = []
        for w, r in enumerate(rows):
            mine = src_refs[0].at[pl.ds(starts[w], r)]
            dst = land_refs[w].at[pl.ds((4 * x + 2 * y + c) * r, r)]
            copies += [(mine, dst, peer) for _, peer in _peers(x, y, c)]
        return copies

    return plan, (NDEV - 1) * len(rows)


def _place_own(pack, fulls, rows, name):
    nw = len(rows)
    starts = [sum(rows[:w]) for w in range(nw)]

    def body(pack_ref, *rest):
        outs, sem = rest[nw:2 * nw], rest[2 * nw]
        me = 4 * lax.axis_index("x") + 2 * lax.axis_index("y") + lax.axis_index("c")
        for w, r in enumerate(rows):
            pltpu.make_async_copy(pack_ref.at[pl.ds(starts[w], r)], outs[w].at[pl.ds(me * r, r)], sem).start()
        pltpu.make_async_copy(pack_ref, pack_ref, sem).wait()

    outs = pl.pallas_call(
        body, out_shape=[jax.ShapeDtypeStruct(f.shape, f.dtype) for f in fulls],
        in_specs=[_HBM] * (1 + nw), out_specs=[_HBM] * nw, input_output_aliases={1 + w: w for w in range(nw)},
        scratch_shapes=[pltpu.SemaphoreType.DMA], name=name)(pack, *fulls)
    return list(outs)


def _scatter_plan(rows):
    def plan(src_refs, land_refs):
        x, y, c = lax.axis_index("x"), lax.axis_index("y"), lax.axis_index("c")
        copies = []
        for w, r in enumerate(rows):
            for mask, (px, py, pc) in _peers(x, y, c):
                src = src_refs[w].at[pl.ds((4 * px + 2 * py + pc) * r, r)]
                copies.append((src, land_refs[w].at[mask - 1], (px, py, pc)))
        return copies

    return plan, (NDEV - 1) * len(rows)


def _scatter_sum(g, got, dev_idx, name):
    n1, r, C = got.shape
    tr = _pick(r, 512, 16)
    per = r // tr

    def body(i_ref, g_ref, got_ref, o_ref):
        acc = g_ref[...].astype(F32)
        for k in range(n1):
            acc = acc + got_ref[k].astype(F32)
        o_ref[...] = acc

    grid_spec = pltpu.PrefetchScalarGridSpec(
        num_scalar_prefetch=1, grid=(per,),
        in_specs=[pl.BlockSpec((tr, C), lambda t, i_ref: (i_ref[0] * per + t, 0)),
                  pl.BlockSpec((n1, tr, C), lambda t, i_ref: (0, t, 0))],
        out_specs=pl.BlockSpec((tr, C), lambda t, i_ref: (t, 0)))
    return pl.pallas_call(
        body, grid_spec=grid_spec, out_shape=jax.ShapeDtypeStruct((r, C), F32),
        compiler_params=_cp("parallel"), name=name)(dev_idx, g, got)


def _sum_slots(a, name):
    n, R, C = a.shape

    def body(a_ref, o_ref):
        acc = a_ref[0]
        for k in range(1, n):
            acc = acc + a_ref[k]
        o_ref[...] = acc

    return pl.pallas_call(body, out_shape=jax.ShapeDtypeStruct((R, C), F32), name=name)(a)


def _shard_axis(name):
    return {"ev_w_in": 2, "ev_a_conv_w": 2, "ev_b_conv_w": 2, "ev_w_out": 1, "od_w_in": 2, "od_c_ln_g": 1,
            "od_c_ln_b": 1, "od_w_out": 1, "xa_w_q": 1, "xa_w_k": 1, "xa_w_v": 1, "xa_w_o": 1,
            "ffn_w_gate": 2, "ffn_w_up": 2, "ffn_w_down": 1}[name]


BIG = ["ev_w_in", "ev_w_out", "od_w_in", "od_w_out", "xa_w_q", "xa_w_k", "xa_w_v", "xa_w_o",
       "ffn_w_gate", "ffn_w_up", "ffn_w_down"]
SMALL_SHARDED = ["ev_a_conv_w", "ev_b_conv_w", "od_c_ln_g", "od_c_ln_b"]
REPLICATED = ["g_mix", "g_xattn", "g_mem", "g_ffn", "g_final", "ev_a_conv_b", "ev_a_ln_g", "ev_a_ln_b",
              "ev_b_conv_b", "od_w_s", "od_b_s"]
WEIGHTS = ["g_mix", "g_xattn", "g_mem", "g_ffn", "g_final", "ev_w_in", "ev_a_conv_w", "ev_a_conv_b", "ev_a_ln_g",
           "ev_a_ln_b", "ev_b_conv_w", "ev_b_conv_b", "ev_w_out", "od_w_in", "od_c_ln_g", "od_c_ln_b", "od_w_s",
           "od_b_s", "od_w_out", "xa_w_q", "xa_w_k", "xa_w_v", "xa_w_o", "ffn_w_gate", "ffn_w_up", "ffn_w_down"]


def _full_from_blocks(blocks, axis):
    shard = blocks.shape[1:]
    full = jnp.moveaxis(blocks, 0, axis)
    return full.reshape(shard[:axis] + (NDEV * shard[axis],) + shard[axis + 1:])


def _blocks_from_full(full, axis):
    shp = full.shape
    split = full.reshape(shp[:axis] + (NDEV, shp[axis] // NDEV) + shp[axis + 1:])
    return jnp.moveaxis(split, axis, 0)


def _pad_rows(flat, width, row_align):
    per = width * row_align
    n = -(-flat.shape[0] // per) * per
    return jnp.pad(flat, (0, n - flat.shape[0])).reshape(n // width, width)


def _row(v):
    return v.reshape(1, -1)


def _xattn_fwd(h, nq, mem, g_m, wq, wk, wv, wo, g_next, tag, after):
    mem_n = _rms_fwd(mem, _row(g_m), f"xa_mem_rms_{tag}")
    q = _mm([(nq, wq, "nn")], f"xa_q_{tag}", out_dtype=BF16, after=after)
    k = _mm([(mem_n, wk, "nn")], f"xa_k_{tag}", out_dtype=BF16)
    v = _mm([(mem_n, wv, "nn")], f"xa_v_{tag}", out_dtype=BF16)
    o = _attn_fwd(q, k, v, f"xa_attn_{tag}")
    h_new, n_next = _mm([(o, wo, "nn")], f"xa_o_{tag}", res=h, rms_g=_row(g_next))
    return h_new, n_next, (h, nq, mem_n, q, k, v, o)


def _xattn_bwd(dh_new, saved, mem, g_x, g_m, wq, wk, wv, wo, tag, push):
    h, nq, mem_n, q, k, v, o = saved
    do = _mm([(dh_new, wo, "nt")], f"xa_do_{tag}", out_dtype=BF16)
    d_wo = _mm_tn(o, dh_new, f"xa_dwo_{tag}")
    dq, dk, dv = _attn_bwd(q, k, v, do, f"xa_attn_bwd_{tag}")
    d_wq = _mm_tn(nq, dq, f"xa_dwq_{tag}")
    d_wk = _mm_tn(mem_n, dk, f"xa_dwk_{tag}")
    d_wv = _mm_tn(mem_n, dv, f"xa_dwv_{tag}")
    token = push([d_wq, d_wk, d_wv, d_wo])
    dmem_n = _mm([(dk, wk, "nt"), (dv, wv, "nt")], f"xa_dmem_{tag}", after=token)
    _, d_gm = _rms_bwd(dmem_n, mem, _row(g_m), None, f"xa_mem_rms_bwd_{tag}")
    dh, d_gx = _mm([(dq, wq, "nt")], f"xa_dnq_{tag}", rms_bwd=(h, _row(g_x), dh_new), tm=512, after=token)
    return dh, dict(g_xattn=d_gx, g_mem=d_gm)


def _ffn_fwd(h, n, wgt, wut, wd, g_next, tag, after):
    a, b, hid = _ffn_up(n, wgt, wut, f"ffn_up_{tag}", after=after)
    if g_next is None:
        h_new, n_next = _mm([(hid, wd, "nn")], f"ffn_down_{tag}", res=h, tm=512, tn=1024), None
    else:
        h_new, n_next = _mm([(hid, wd, "nn")], f"ffn_down_{tag}", res=h, rms_g=_row(g_next), tm=512)
    return h_new, n_next, (h, n, a, b, hid)


def _ffn_bwd(dh_new, saved, g_f, wgt, wut, wd, tag, push):
    h, n, a, b, hid = saved
    da, db = _ffn_dhid(dh_new, wd, a, b, f"ffn_dhid_{tag}")
    d_wd = _mm_tn(hid, dh_new, f"ffn_dwd_{tag}", tn=512)
    d_wgt = _mm_tn(da, n, f"ffn_dwg_{tag}", tn=512)
    d_wut = _mm_tn(db, n, f"ffn_dwu_{tag}", tn=512)
    token = push([d_wgt, d_wut, d_wd])
    dh, d_gf = _mm([(da, wgt, "nn"), (db, wut, "nn")], f"ffn_dn_{tag}", rms_bwd=(h, _row(g_f), dh_new), tm=256,
                   after=token)
    return dh, dict(g_ffn=d_gf)


_XA = ["xa_w_q", "xa_w_k", "xa_w_v", "xa_w_o"]
_FFN = ["ffn_w_gate", "ffn_w_up", "ffn_w_down"]
GATHERS = {
    "ev_in": [("ev_w_in", 0)],
    "xa0": [("ev_w_out", 0)] + [(n, 0) for n in _XA],
    "ffn0": [(n, 0) for n in _FFN],
    "od": [("od_w_in", 0), ("od_w_out", 0)],
    "xa1": [(n, 1) for n in _XA],
    "ffn1": [(n, 1) for n in _FFN],
}
SCATTERS = {
    "ffn1": [(n, 1) for n in _FFN],
    "xa1": [(n, 1) for n in _XA],
    "od": [("od_w_in", 0), ("od_w_out", 0)],
    "ffn0": [(n, 0) for n in _FFN],
    "xa0": [(n, 0) for n in _XA],
    "ev": [("ev_w_in", 0), ("ev_w_out", 0)],
}


def _local_step(x, mem, loss_target, W, comm):
    grads = {}

    h0 = x
    (ev_w_in_t,), token = comm.weights("ev_in", None)
    n0 = _rms_fwd(h0, _row(W["g_mix"][0]), "ev_rms", after=token)
    z = _mm([(n0, ev_w_in_t, "nt")], "ev_in", tn=1280)
    token = comm.prefetch(["ffn0"], z)
    ab, ca = _conv_fwd(z, W["ev_a_conv_w"][0], W["ev_a_conv_b"], W["ev_a_ln_g"], W["ev_a_ln_b"],
                       W["ev_b_conv_w"][0], W["ev_b_conv_b"], "ev_conv", after=token)
    (ev_w_out, *xa_w0), _ = comm.weights("xa0", ab)
    h1, nq0 = _mm([(ab, ev_w_out, "nn")], "ev_out", res=h0, rms_g=_row(W["g_xattn"][0]))
    token = comm.prefetch(["od", "xa1"], nq0)
    h2, nf0, xa0 = _xattn_fwd(h1, nq0, mem, W["g_mem"][0], *xa_w0, W["g_ffn"][0], "l0", token)
    ffn_w0, _ = comm.weights("ffn0", nf0)
    token = comm.prefetch(["ffn1"], nf0)
    h3, n3, ff0 = _ffn_fwd(h2, nf0, *ffn_w0, W["g_mix"][1], "l0", token)

    (od_w_in_t, od_w_out), _ = comm.weights("od", n3)
    zp = _mm([(n3, od_w_in_t, "nt")], "od_in", tn=1024)
    D = x.shape[1]
    ws = W["od_w_s"][0].astype(BF16)
    wst = jnp.swapaxes(ws, 1, 2)
    bsb = jnp.repeat(jnp.transpose(W["od_b_s"][0]), D // C_GROUPS, axis=1)
    y_sgu = _sgu_fwd(zp, W["od_c_ln_g"], W["od_c_ln_b"], ws, bsb, "od_sgu")
    h4, nq1 = _mm([(y_sgu, od_w_out, "nn")], "od_out", res=h3, rms_g=_row(W["g_xattn"][1]))
    xa_w1, _ = comm.weights("xa1", nq1)
    h5, nf1, xa1 = _xattn_fwd(h4, nq1, mem, W["g_mem"][1], *xa_w1, W["g_ffn"][1], "l1", None)
    ffn_w1, _ = comm.weights("ffn1", nf1)
    h6, _, ff1 = _ffn_fwd(h5, nf1, *ffn_w1, None, "l1", None)

    loss_row, dh6, d_gfinal = _loss_bwd(h6, _row(W["g_final"]), loss_target, "loss")
    grads["g_final"] = d_gfinal.reshape(-1)

    dh5, g_ff1 = _ffn_bwd(dh6, ff1, W["g_ffn"][1], *ffn_w1, "l1", lambda dws: comm.grads("ffn1", dws))
    dh4, g_xa1 = _xattn_bwd(dh5, xa1, mem, W["g_xattn"][1], W["g_mem"][1], *xa_w1, "l1",
                            lambda dws: comm.grads("xa1", dws))
    dy_sgu = _mm([(dh4, od_w_out, "nt")], "od_dy", tn=1024)
    d_od_out = _mm_tn(y_sgu, dh4, "od_dwout", tn=1024)
    dzp, d_ws, d_bsb, d_clng, d_clnb = _sgu_bwd(dy_sgu, zp, W["od_c_ln_g"], W["od_c_ln_b"], ws, wst, bsb, "od_sgu_bwd")
    grads["od_w_s"] = d_ws[None]
    grads["od_b_s"] = jnp.transpose(_group_sum(d_bsb, C_GROUPS, "od_dbs"))[None]
    grads["od_c_ln_g"], grads["od_c_ln_b"] = d_clng, d_clnb
    token = comm.grads("od", [_mm_tn(dzp, n3, "od_dwin", tn=512), d_od_out])
    dh3, d_gmix1 = _mm([(dzp, od_w_in_t, "nn")], "od_dn", rms_bwd=(h3, _row(W["g_mix"][1]), dh4), tm=512, after=token)

    dh2, g_ff0 = _ffn_bwd(dh3, ff0, W["g_ffn"][0], *ffn_w0, "l0", lambda dws: comm.grads("ffn0", dws))
    dh1, g_xa0 = _xattn_bwd(dh2, xa0, mem, W["g_xattn"][0], W["g_mem"][0], *xa_w0, "l0",
                            lambda dws: comm.grads("xa0", dws))
    dab = _mm([(dh1, ev_w_out, "nt")], "ev_dab", tn=1024)
    d_ev_out = _mm_tn(ab, dh1, "ev_dwout", tn=1024)
    dca, d_lng, d_lnb, d_ba = _conv_bwd_ln(dab, ca, W["ev_a_ln_g"], W["ev_a_ln_b"], "ev_conv_bwd_ln")
    dz, d_wa, d_wb, d_bb = _conv_bwd(z, dca, dab, W["ev_a_conv_w"][0], W["ev_b_conv_w"][0], W["ev_b_conv_b"],
                                     "ev_conv_bwd")
    grads.update(ev_a_ln_g=d_lng, ev_a_ln_b=d_lnb, ev_a_conv_b=d_ba, ev_b_conv_b=d_bb,
                 ev_a_conv_w=d_wa[None], ev_b_conv_w=d_wb[None])
    token = comm.grads("ev", [_mm_tn(dz, n0, "ev_dwin", tn=512), d_ev_out])
    grad_x, d_gmix0 = _mm([(dz, ev_w_in_t, "nn")], "ev_dn", rms_bwd=(h0, _row(W["g_mix"][0]), dh1), tm=512, after=token)

    grads["g_mix"] = jnp.concatenate([d_gmix0, d_gmix1], axis=0)
    for key in ("g_xattn", "g_mem"):
        grads[key] = jnp.concatenate([g_xa0[key], g_xa1[key]], axis=0)
    grads["g_ffn"] = jnp.concatenate([g_ff0["g_ffn"], g_ff1["g_ffn"]], axis=0)
    return loss_row, grad_x, grads


class _Exchanges:
    def __init__(self, shards, dev_idx):
        self.shards, self.dev_idx = shards, dev_idx
        self.gathering, self.scattering = {}, {}
        self.first = _all_gather_weights(self._pack(GATHERS["ev_in"]), self._rows(GATHERS["ev_in"]), "ag_ev_in")
        self.first_token = self.prefetch(["xa0"], self.first[0])

    def _rows(self, entries):
        return [self.shards[e].shape[0] for e in entries]

    def _pack(self, entries):
        return jnp.concatenate([self.shards[e] for e in entries], axis=0)

    def prefetch(self, gathers, after):
        for name in gathers:
            rows = self._rows(GATHERS[name])
            pack = self._pack(GATHERS[name])
            lands = [lax.empty((NDEV * r, pack.shape[1]), pack.dtype) for r in rows]
            plan, n = _gather_plan(rows)
            send, recv, srcs, lands, after = _split_start([pack], lands, plan, n, after, f"ag_{name}_start")
            self.gathering[name] = (send, recv, srcs, lands, plan, rows)
        return after

    def weights(self, name, after):
        if name == "ev_in":
            return self.first, self.first_token
        send, recv, srcs, lands, plan, rows = self.gathering.pop(name)
        srcs, lands = _split_wait(send, recv, srcs, lands, plan, after, f"ag_{name}_wait")
        return _place_own(srcs[0], lands, rows, f"ag_{name}_own"), None

    def grads(self, name, dws):
        rows = self._rows(SCATTERS[name])
        lands = [lax.empty((NDEV - 1, r, d.shape[1]), d.dtype) for r, d in zip(rows, dws)]
        plan, n = _scatter_plan(rows)
        send, recv, srcs, lands, token = _split_start(dws, lands, plan, n, None, f"rs_{name}_start")
        self.scattering[name] = (send, recv, srcs, lands, plan)
        return token

    def reduced(self, after):
        out = {}
        for name, (send, recv, srcs, lands, plan) in self.scattering.items():
            srcs, lands = _split_wait(send, recv, srcs, lands, plan, after, f"rs_{name}_wait")
            for (w, i), g, got in zip(SCATTERS[name], srcs, lands):
                out[w, i] = _scatter_sum(g, got, self.dev_idx, f"rs_sum_{w}_{i}")
        return out


def kernel(x, mem, g_mix, g_xattn, g_mem, g_ffn, g_final, ev_w_in, ev_a_conv_w, ev_a_conv_b, ev_a_ln_g, ev_a_ln_b, ev_b_conv_w, ev_b_conv_b, ev_w_out, od_w_in, od_c_ln_g, od_c_ln_b, od_w_s, od_b_s, od_w_out, xa_w_q, xa_w_k, xa_w_v, xa_w_o, ffn_w_gate, ffn_w_up, ffn_w_down, loss_target, m_g_mix, m_g_xattn, m_g_mem, m_g_ffn, m_g_final, m_ev_w_in, m_ev_a_conv_w, m_ev_a_conv_b, m_ev_a_ln_g, m_ev_a_ln_b, m_ev_b_conv_w, m_ev_b_conv_b, m_ev_w_out, m_od_w_in, m_od_c_ln_g, m_od_c_ln_b, m_od_w_s, m_od_b_s, m_od_w_out, m_xa_w_q, m_xa_w_k, m_xa_w_v, m_xa_w_o, m_ffn_w_gate, m_ffn_w_up, m_ffn_w_down, v_g_mix, v_g_xattn, v_g_mem, v_g_ffn, v_g_final, v_ev_w_in, v_ev_a_conv_w, v_ev_a_conv_b, v_ev_a_ln_g, v_ev_a_ln_b, v_ev_b_conv_w, v_ev_b_conv_b, v_ev_w_out, v_od_w_in, v_od_c_ln_g, v_od_c_ln_b, v_od_w_s, v_od_b_s, v_od_w_out, v_xa_w_q, v_xa_w_k, v_xa_w_v, v_xa_w_o, v_ffn_w_gate, v_ffn_w_up, v_ffn_w_down):
    local = dict(g_mix=g_mix, g_xattn=g_xattn, g_mem=g_mem, g_ffn=g_ffn, g_final=g_final, ev_w_in=ev_w_in, ev_a_conv_w=ev_a_conv_w, ev_a_conv_b=ev_a_conv_b, ev_a_ln_g=ev_a_ln_g, ev_a_ln_b=ev_a_ln_b, ev_b_conv_w=ev_b_conv_w, ev_b_conv_b=ev_b_conv_b, ev_w_out=ev_w_out, od_w_in=od_w_in, od_c_ln_g=od_c_ln_g, od_c_ln_b=od_c_ln_b, od_w_s=od_w_s, od_b_s=od_b_s, od_w_out=od_w_out, xa_w_q=xa_w_q, xa_w_k=xa_w_k, xa_w_v=xa_w_v, xa_w_o=xa_w_o, ffn_w_gate=ffn_w_gate, ffn_w_up=ffn_w_up, ffn_w_down=ffn_w_down)
    mom = dict(g_mix=m_g_mix, g_xattn=m_g_xattn, g_mem=m_g_mem, g_ffn=m_g_ffn, g_final=m_g_final, ev_w_in=m_ev_w_in, ev_a_conv_w=m_ev_a_conv_w, ev_a_conv_b=m_ev_a_conv_b, ev_a_ln_g=m_ev_a_ln_g, ev_a_ln_b=m_ev_a_ln_b, ev_b_conv_w=m_ev_b_conv_w, ev_b_conv_b=m_ev_b_conv_b, ev_w_out=m_ev_w_out, od_w_in=m_od_w_in, od_c_ln_g=m_od_c_ln_g, od_c_ln_b=m_od_c_ln_b, od_w_s=m_od_w_s, od_b_s=m_od_b_s, od_w_out=m_od_w_out, xa_w_q=m_xa_w_q, xa_w_k=m_xa_w_k, xa_w_v=m_xa_w_v, xa_w_o=m_xa_w_o, ffn_w_gate=m_ffn_w_gate, ffn_w_up=m_ffn_w_up, ffn_w_down=m_ffn_w_down)
    vel = dict(g_mix=v_g_mix, g_xattn=v_g_xattn, g_mem=v_g_mem, g_ffn=v_g_ffn, g_final=v_g_final, ev_w_in=v_ev_w_in, ev_a_conv_w=v_ev_a_conv_w, ev_a_conv_b=v_ev_a_conv_b, ev_a_ln_g=v_ev_a_ln_g, ev_a_ln_b=v_ev_a_ln_b, ev_b_conv_w=v_ev_b_conv_w, ev_b_conv_b=v_ev_b_conv_b, ev_w_out=v_ev_w_out, od_w_in=v_od_w_in, od_c_ln_g=v_od_c_ln_g, od_c_ln_b=v_od_c_ln_b, od_w_s=v_od_w_s, od_b_s=v_od_b_s, od_w_out=v_od_w_out, xa_w_q=v_xa_w_q, xa_w_k=v_xa_w_k, xa_w_v=v_xa_w_v, xa_w_o=v_xa_w_o, ffn_w_gate=v_ffn_w_gate, ffn_w_up=v_ffn_w_up, ffn_w_down=v_ffn_w_down)
    D = x.shape[-1]
    dev = 4 * lax.axis_index("x") + 2 * lax.axis_index("y") + lax.axis_index("c")

    def comm_layout(n, a):
        return jnp.transpose(a) if _shard_axis(n) == 2 else a

    shards = {(n, i): comm_layout(n, local[n][i]).astype(BF16) for n in BIG for i in range(local[n].shape[0])}
    small_sizes = [local[n].size for n in SMALL_SHARDED]
    small_block = _pad_rows(jnp.concatenate([local[n].reshape(-1) for n in SMALL_SHARDED]), 128, 8)
    small_all = _all_gather(small_block, "ag_small").reshape(NDEV, -1)
    comm = _Exchanges(shards, jnp.reshape(dev, (1,)).astype(jnp.int32))

    W = {n: local[n] for n in REPLICATED}
    o0 = 0
    for n, sz in zip(SMALL_SHARDED, small_sizes):
        blocks = small_all[:, o0:o0 + sz].reshape((NDEV,) + local[n].shape)
        W[n] = _full_from_blocks(blocks, _shard_axis(n))
        o0 += sz

    loss_row, grad_x, grads = _local_step(x[0], mem[0], loss_target[0], W, comm)

    reduced = comm.reduced(grad_x)
    gsh = {n: jnp.stack([comm_layout(n, reduced[n, i]) for i in range(local[n].shape[0])]) for n in BIG}

    rest = REPLICATED + SMALL_SHARDED
    rest_full_shapes = [grads[n].shape for n in rest]
    g_rest = _pad_rows(jnp.concatenate([grads[n].astype(F32).reshape(-1) for n in rest]), D, 8)
    g_rest = _sum_slots(_all_gather(g_rest, "ag_small_grads"), "sum_small_grads").reshape(-1)
    o0 = 0
    for n, shp in zip(rest, rest_full_shapes):
        sz = 1
        for s in shp:
            sz *= s
        full = g_rest[o0:o0 + sz].reshape(shp)
        o0 += sz
        if n in SMALL_SHARDED:
            full = lax.dynamic_index_in_dim(_blocks_from_full(full, _shard_axis(n)), dev, 0, keepdims=False)
        gsh[n] = full.reshape(local[n].shape)

    delta, new_m, new_v = {}, {}, {}
    for n in WEIGHTS:
        delta[n], new_m[n], new_v[n] = _adamw(local[n], gsh[n], mom[n], vel[n], f"adamw_{n}")

    loss = lax.psum(loss_row[0, 0], ("x", "y", "c"))
    return (loss, grad_x[None], *[gsh[n] for n in WEIGHTS], *[delta[n] for n in WEIGHTS],
            *[new_m[n] for n in WEIGHTS], *[new_v[n] for n in WEIGHTS])
```

```python
import jax
import jax.numpy as jnp
from jax import lax
from jax.experimental import pallas as pl
from jax.experimental.pallas import tpu as pltpu

F32, BF16 = jnp.float32, jnp.bfloat16
NDEV = 8
RMS_EPS = 1e-6
LN_EPS = 1e-5
CHUNK = 128
C_GROUPS = 8
XA_HEADS = 4
ADAM_LR, ADAM_B1, ADAM_B2, ADAM_EPS, ADAM_WD, ADAM_STEP = 0.001, 0.9, 0.999, 1e-08, 0.01, 10
HALO = 16
ROW_CHUNK = 32
V7X_VMEM_LIMIT = 56 * 1024 * 1024
MESH = pl.DeviceIdType.MESH

TS_ROW = 512
TS_MM = 1024
TN_MM = 1408
TS_FFN = 256
TS_CONV = 512
TS_SGU = 512
TS_ATTN = 512


def _cp(*sem):
    return pltpu.CompilerParams(dimension_semantics=sem, vmem_limit_bytes=V7X_VMEM_LIMIT)


def _pick(n, pref, align):
    for t in range(min(n, pref), 0, -1):
        if n % t == 0 and (t % align == 0 or t == n):
            return t
    return n


def _sigmoid(x):
    return 0.5 * jnp.tanh(0.5 * x) + 0.5


def _dot(a, b):
    return jnp.dot(a, b, preferred_element_type=F32)


def _dot_nt(a, b):
    return lax.dot_general(a, b, (((1,), (1,)), ((), ())), preferred_element_type=F32)


def _dot_tn(a, b):
    return lax.dot_general(a, b, (((0,), (0,)), ((), ())), preferred_element_type=F32)


_ANY = pl.BlockSpec(memory_space=pl.ANY)


def _after(after):
    return ([], []) if after is None else ([_ANY], [after])


def _rms_fwd(h, g, name, after=None):
    S, D = h.shape
    ts = _pick(S, TS_MM, 16)
    after_specs, after_ops = _after(after)

    def body(h_ref, g_ref, *rest):
        o_ref = rest[-1]
        x = h_ref[...]
        r = lax.rsqrt(jnp.mean(x * x, axis=-1, keepdims=True) + RMS_EPS)
        o_ref[...] = ((x * r) * g_ref[...]).astype(o_ref.dtype)

    return pl.pallas_call(
        body, grid=(S // ts,),
        in_specs=[pl.BlockSpec((ts, D), lambda i: (i, 0)), pl.BlockSpec((1, D), lambda i: (0, 0))] + after_specs,
        out_specs=pl.BlockSpec((ts, D), lambda i: (i, 0)),
        out_shape=jax.ShapeDtypeStruct((S, D), BF16), compiler_params=_cp("parallel"), name=name)(h, g, *after_ops)


def _rms_bwd(dn, h, g, dres, name):
    S, D = h.shape
    ts = _pick(S, TS_ROW, 8)
    has_res = dres is not None

    def body(*refs):
        if has_res:
            dn_ref, h_ref, g_ref, dres_ref, dh_ref, dg_ref = refs
        else:
            dn_ref, h_ref, g_ref, dh_ref, dg_ref = refs
        x = h_ref[...]
        dn_ = dn_ref[...].astype(F32)
        r = lax.rsqrt(jnp.mean(x * x, axis=-1, keepdims=True) + RMS_EPS)
        xr = x * r

        @pl.when(pl.program_id(0) == 0)
        def _():
            dg_ref[...] = jnp.zeros_like(dg_ref)

        dg_ref[...] += jnp.sum(dn_ * xr, axis=0, keepdims=True)
        u = dn_ * g_ref[...]
        dh = r * u - xr * (r * jnp.mean(u * xr, axis=-1, keepdims=True))
        if has_res:
            dh = dh + dres_ref[...]
        dh_ref[...] = dh

    tile = pl.BlockSpec((ts, D), lambda i: (i, 0))
    vec = pl.BlockSpec((1, D), lambda i: (0, 0))
    ins = [dn, h, g] + ([dres] if has_res else [])
    return pl.pallas_call(
        body, grid=(S // ts,),
        in_specs=[tile, tile, vec] + ([tile] if has_res else []),
        out_specs=[tile, vec],
        out_shape=[jax.ShapeDtypeStruct((S, D), F32), jax.ShapeDtypeStruct((1, D), F32)],
        compiler_params=_cp("arbitrary"), name=name)(*ins)


def _loss_bwd(h, g, target, name):
    S, D = h.shape
    ts = _pick(S, TS_ROW, 8)

    def body(h_ref, g_ref, t_ref, loss_ref, dh_ref, dg_ref):
        x = h_ref[...]
        r = lax.rsqrt(jnp.mean(x * x, axis=-1, keepdims=True) + RMS_EPS)
        xr = x * r
        gg = g_ref[...]
        e = xr * gg - t_ref[...]

        @pl.when(pl.program_id(0) == 0)
        def _():
            dg_ref[...] = jnp.zeros_like(dg_ref)
            loss_ref[...] = jnp.zeros_like(loss_ref)

        tile_loss = jnp.sum(jnp.sum(e * e, axis=0, keepdims=True), axis=1, keepdims=True) * (0.5 / D)
        loss_ref[...] += jnp.broadcast_to(tile_loss, loss_ref.shape)
        dy = e * (1.0 / D)
        dg_ref[...] += jnp.sum(dy * xr, axis=0, keepdims=True)
        u = dy * gg
        dh_ref[...] = r * u - xr * (r * jnp.mean(u * xr, axis=-1, keepdims=True))

    tile = pl.BlockSpec((ts, D), lambda i: (i, 0))
    vec = pl.BlockSpec((1, D), lambda i: (0, 0))
    return pl.pallas_call(
        body, grid=(S // ts,),
        in_specs=[tile, vec, tile],
        out_specs=[pl.BlockSpec((1, 128), lambda i: (0, 0)), tile, vec],
        out_shape=[jax.ShapeDtypeStruct((1, 128), F32), jax.ShapeDtypeStruct((S, D), F32),
                   jax.ShapeDtypeStruct((1, D), F32)],
        compiler_params=_cp("arbitrary"), name=name)(h, g, target)


def _mm(pairs, name, out_dtype=F32, res=None, rms_g=None, rms_bwd=None, tm=None, tn=None, after=None):
    M = pairs[0][0].shape[0]
    N = pairs[0][1].shape[1 if pairs[0][2] == "nn" else 0]
    whole_rows = rms_g is not None or rms_bwd is not None
    tm = _pick(M, tm or TS_MM, 16)
    tn = N if whole_rows else _pick(N, tn or TN_MM, 128)
    npair = len(pairs)
    modes = [p[2] for p in pairs]
    after_specs, after_ops = _after(after)

    def body(*refs):
        acc = None
        for p in range(npair):
            a_ = refs[2 * p][...].astype(BF16)
            d = _dot(a_, refs[2 * p + 1][...]) if modes[p] == "nn" else _dot_nt(a_, refs[2 * p + 1][...])
            acc = d if acc is None else acc + d
        rest = refs[2 * npair + len(after_ops):]
        if res is not None:
            acc = acc + rest[0][...]
            rest = rest[1:]
        if rms_bwd is not None:
            h_ref, g_ref, dres_ref, dh_ref, dg_ref = rest
            x = h_ref[...]
            r = lax.rsqrt(jnp.mean(x * x, axis=-1, keepdims=True) + RMS_EPS)
            xr = x * r

            @pl.when(pl.program_id(0) == 0)
            def _():
                dg_ref[...] = jnp.zeros_like(dg_ref)

            dg_ref[...] += jnp.sum(acc * xr, axis=0, keepdims=True)
            u = acc * g_ref[...]
            dh_ref[...] = r * u - xr * (r * jnp.mean(u * xr, axis=-1, keepdims=True)) + dres_ref[...]
        elif rms_g is not None:
            g_ref, o_ref, n_ref = rest
            o_ref[...] = acc
            r = lax.rsqrt(jnp.mean(acc * acc, axis=-1, keepdims=True) + RMS_EPS)
            n_ref[...] = ((acc * r) * g_ref[...]).astype(BF16)
        else:
            rest[0][...] = acc.astype(rest[0].dtype)

    in_specs, ins = [], []
    for a, w, mode in pairs:
        K = a.shape[1]
        in_specs.append(pl.BlockSpec((tm, K), lambda i, j: (i, 0)))
        in_specs.append(pl.BlockSpec((K, tn), lambda i, j: (0, j)) if mode == "nn"
                        else pl.BlockSpec((tn, K), lambda i, j: (j, 0)))
        ins += [a, w]
    in_specs += after_specs
    ins += after_ops
    tile = pl.BlockSpec((tm, tn), lambda i, j: (i, j))
    vec = pl.BlockSpec((1, tn), lambda i, j: (0, j))
    if res is not None:
        in_specs.append(tile)
        ins.append(res)
    sem = ("parallel", "parallel")
    if rms_bwd is not None:
        in_specs += [tile, vec, tile]
        ins += list(rms_bwd)
        out_specs = [tile, vec]
        out_shape = [jax.ShapeDtypeStruct((M, N), F32), jax.ShapeDtypeStruct((1, N), F32)]
        sem = ("arbitrary", "arbitrary")
    elif rms_g is not None:
        in_specs.append(vec)
        ins.append(rms_g)
        out_specs = [tile, tile]
        out_shape = [jax.ShapeDtypeStruct((M, N), F32), jax.ShapeDtypeStruct((M, N), BF16)]
    else:
        out_specs = tile
        out_shape = jax.ShapeDtypeStruct((M, N), out_dtype)
    return pl.pallas_call(
        body, grid=(M // tm, N // tn), in_specs=in_specs, out_specs=out_specs, out_shape=out_shape,
        compiler_params=_cp(*sem), name=name)(*ins)


def _mm_tn(a, b, name, ts=None, tn=None):
    S, K = a.shape
    N = b.shape[1]
    ts = _pick(S, ts or TS_MM, 16)
    tn = _pick(N, tn or TN_MM, 128)
    nsteps = S // ts

    def body(a_ref, b_ref, o_ref, acc_ref):
        s = pl.program_id(1)

        @pl.when(s == 0)
        def _():
            acc_ref[...] = jnp.zeros_like(acc_ref)

        acc_ref[...] += _dot_tn(a_ref[...].astype(BF16), b_ref[...].astype(BF16))

        @pl.when(s == nsteps - 1)
        def _():
            o_ref[...] = acc_ref[...].astype(o_ref.dtype)

    return pl.pallas_call(
        body, grid=(N // tn, nsteps),
        in_specs=[pl.BlockSpec((ts, K), lambda j, s: (s, 0)), pl.BlockSpec((ts, tn), lambda j, s: (s, j))],
        out_specs=pl.BlockSpec((K, tn), lambda j, s: (0, j)), out_shape=jax.ShapeDtypeStruct((K, N), BF16),
        scratch_shapes=[pltpu.VMEM((K, tn), F32)],
        compiler_params=_cp("parallel", "arbitrary"), name=name)(a, b)


def _col_chunk(n):
    return 256 if n % 256 == 0 else 128


def _ffn_up(n, wgt, wut, name, after=None):
    S, D = n.shape
    F = wgt.shape[0]
    tm = _pick(S, TS_FFN, 16)
    ce = _col_chunk(F)
    after_specs, after_ops = _after(after)

    def body(n_ref, wg_ref, wu_ref, *rest):
        a_ref, b_ref, hid_ref = rest[-3:]
        x = n_ref[...]
        for c0 in range(0, F, ce):
            a = _dot_nt(x, wg_ref[c0:c0 + ce, :])
            b = _dot_nt(x, wu_ref[c0:c0 + ce, :])
            a_ref[:, c0:c0 + ce] = a.astype(BF16)
            b_ref[:, c0:c0 + ce] = b.astype(BF16)
            hid_ref[:, c0:c0 + ce] = (a * _sigmoid(a) * b).astype(BF16)

    wspec = pl.BlockSpec((F, D), lambda i: (0, 0))
    ospec = pl.BlockSpec((tm, F), lambda i: (i, 0))
    osh = jax.ShapeDtypeStruct((S, F), BF16)
    return pl.pallas_call(
        body, grid=(S // tm,),
        in_specs=[pl.BlockSpec((tm, D), lambda i: (i, 0)), wspec, wspec] + after_specs,
        out_specs=[ospec, ospec, ospec], out_shape=[osh, osh, osh],
        compiler_params=_cp("parallel"), name=name)(n, wgt, wut, *after_ops)


def _ffn_dhid(dh, wd, a, b, name):
    S, D = dh.shape
    F = wd.shape[0]
    tm = _pick(S, TS_FFN, 16)
    ce = _col_chunk(F)

    def body(dh_ref, wd_ref, a_ref, b_ref, da_ref, db_ref):
        x = dh_ref[...].astype(BF16)
        for c0 in range(0, F, ce):
            g = _dot_nt(x, wd_ref[c0:c0 + ce, :])
            a_ = a_ref[:, c0:c0 + ce].astype(F32)
            b_ = b_ref[:, c0:c0 + ce].astype(F32)
            sg = _sigmoid(a_)
            da_ref[:, c0:c0 + ce] = (g * b_ * (sg * (1.0 + a_ * (1.0 - sg)))).astype(BF16)
            db_ref[:, c0:c0 + ce] = (g * (a_ * sg)).astype(BF16)

    tile = pl.BlockSpec((tm, F), lambda i: (i, 0))
    osh = jax.ShapeDtypeStruct((S, F), BF16)
    return pl.pallas_call(
        body, grid=(S // tm,),
        in_specs=[pl.BlockSpec((tm, D), lambda i: (i, 0)), pl.BlockSpec((F, D), lambda i: (0, 0)), tile, tile],
        out_specs=[tile, tile], out_shape=[osh, osh],
        compiler_params=_cp("parallel"), name=name)(dh, wd, a, b)


def _softmax_rows(s):
    m = jnp.max(s, axis=-1, keepdims=True)
    p = jnp.exp(s - m)
    return p / jnp.sum(p, axis=-1, keepdims=True)


def _attn_fwd(q, k, v, name):
    S, D = q.shape
    M = k.shape[0]
    hd = D // XA_HEADS
    scale = hd ** -0.5
    ts = _pick(S, TS_ATTN, 16)

    def body(q_ref, k_ref, v_ref, o_ref):
        for h in range(XA_HEADS):
            sl = slice(h * hd, (h + 1) * hd)
            p = _softmax_rows(_dot_nt(q_ref[:, sl], k_ref[:, sl]) * scale)
            o_ref[:, sl] = _dot(p.astype(BF16), v_ref[:, sl]).astype(BF16)

    tile = pl.BlockSpec((ts, D), lambda i: (i, 0))
    memspec = pl.BlockSpec((M, D), lambda i: (0, 0))
    return pl.pallas_call(
        body, grid=(S // ts,), in_specs=[tile, memspec, memspec], out_specs=tile,
        out_shape=jax.ShapeDtypeStruct((S, D), BF16), compiler_params=_cp("parallel"), name=name)(q, k, v)


def _attn_bwd(q, k, v, do, name):
    S, D = q.shape
    M = k.shape[0]
    hd = D // XA_HEADS
    scale = hd ** -0.5
    ts = _pick(S, TS_ATTN, 16)

    def body(q_ref, k_ref, v_ref, do_ref, dq_ref, dk_ref, dv_ref):
        @pl.when(pl.program_id(0) == 0)
        def _():
            dk_ref[...] = jnp.zeros_like(dk_ref)
            dv_ref[...] = jnp.zeros_like(dv_ref)

        for h in range(XA_HEADS):
            sl = slice(h * hd, (h + 1) * hd)
            qh, kh, vh, doh = q_ref[:, sl], k_ref[:, sl], v_ref[:, sl], do_ref[:, sl]
            p = _softmax_rows(_dot_nt(qh, kh) * scale)
            dp = _dot_nt(doh, vh)
            dv_ref[:, sl] += _dot_tn(p.astype(BF16), doh)
            delta = jnp.sum(dp * p, axis=-1, keepdims=True)
            ds = (p * (dp - delta) * scale).astype(BF16)
            dq_ref[:, sl] = _dot(ds, kh).astype(BF16)
            dk_ref[:, sl] += _dot_tn(ds, qh)

    tile = pl.BlockSpec((ts, D), lambda i: (i, 0))
    memspec = pl.BlockSpec((M, D), lambda i: (0, 0))
    return pl.pallas_call(
        body, grid=(S // ts,), in_specs=[tile, memspec, memspec, tile], out_specs=[tile, memspec, memspec],
        out_shape=[jax.ShapeDtypeStruct((S, D), BF16), jax.ShapeDtypeStruct((M, D), F32),
                   jax.ShapeDtypeStruct((M, D), F32)],
        compiler_params=_cp("arbitrary"), name=name)(q, k, v, do)


def _halo_specs(ts, width, col):
    per = ts // HALO

    def prev(i):
        return (jnp.maximum(i * per - 1, 0), col)

    def nxt(i, n_tiles):
        return (jnp.minimum((i + 1) * per, n_tiles * per - 1), col)

    return prev, nxt


def _fill_ext(ext_ref, prev_val, main_val, next_val, first, last, ts):
    ext_ref[pl.ds(0, HALO), :] = jnp.where(first, 0.0, prev_val)
    ext_ref[pl.ds(HALO, ts), :] = main_val
    ext_ref[pl.ds(HALO + ts, HALO), :] = jnp.where(last, 0.0, next_val)


def _conv_fwd(z, wa, ba, lng, lnb, wb, bb, name, after=None):
    S = z.shape[0]
    C = z.shape[1] // 5
    KA, KB = wa.shape[0], wb.shape[0]
    pa, pb = KA // 2, KB // 2
    assert pa <= HALO and pb <= HALO
    ts = _pick(S, TS_CONV, ROW_CHUNK)
    nt = S // ts
    rc = ROW_CHUNK
    prev, nxt = _halo_specs(ts, 5 * C, 0)
    after_specs, after_ops = _after(after)

    def body(*refs):
        compute(*refs[:9], *refs[9 + len(after_ops):])

    def compute(z_ref, zp_ref, zn_ref, wa_ref, ba_ref, lng_ref, lnb_ref, wb_ref, bb_ref, ab_ref, ca_ref,
                ga_ext, tb_ext, win_a, win_b):
        i = pl.program_id(0)
        first, last = i == 0, i == nt - 1

        def glu(r):
            return r[:, 0:C] * _sigmoid(r[:, C:2 * C])

        def gcb(r):
            return r[:, 4 * C:5 * C] * r[:, 2 * C:3 * C]

        _fill_ext(ga_ext, glu(zp_ref), glu(z_ref), glu(zn_ref), first, last, ts)
        _fill_ext(tb_ext, gcb(zp_ref), gcb(z_ref), gcb(zn_ref), first, last, ts)

        def chunk(c, carry):
            r0 = pl.multiple_of(c * rc, rc)
            win_a[...] = ga_ext[pl.ds(r0, rc + 2 * HALO), :]
            win_b[...] = tb_ext[pl.ds(r0, rc + 2 * HALO), :]
            acc = jnp.zeros((rc, C), F32)
            for k in range(KA):
                acc = acc + wa_ref[k:k + 1, :] * win_a[pl.ds(HALO - pa + k, rc), :]
            ca = acc + ba_ref[...]
            ca_ref[pl.ds(r0, rc), :] = ca
            mu = jnp.mean(ca, axis=-1, keepdims=True)
            xc = ca - mu
            var = jnp.mean(xc * xc, axis=-1, keepdims=True)
            ln = xc * lax.rsqrt(var + LN_EPS) * lng_ref[...] + lnb_ref[...]
            ab_ref[pl.ds(r0, rc), 0:C] = (ln * _sigmoid(ln)).astype(BF16)
            cb = jnp.zeros((rc, C), F32) + bb_ref[...]
            for k in range(KB):
                cb = cb + wb_ref[k:k + 1, :] * win_b[pl.ds(HALO - pb + k, rc), :]
            ab_ref[pl.ds(r0, rc), C:2 * C] = (z_ref[pl.ds(r0, rc), 3 * C:4 * C] * cb).astype(BF16)
            return carry

        lax.fori_loop(0, ts // rc, chunk, 0)

    zspec = pl.BlockSpec((ts, 5 * C), lambda i: (i, 0))
    zprev = pl.BlockSpec((HALO, 5 * C), prev)
    znext = pl.BlockSpec((HALO, 5 * C), lambda i: nxt(i, nt))

    def full(a):
        return pl.BlockSpec(a.shape, lambda i: (0, 0))

    return pl.pallas_call(
        body, grid=(nt,),
        in_specs=[zspec, zprev, znext, full(wa), full(ba), full(lng), full(lnb), full(wb), full(bb)] + after_specs,
        out_specs=[pl.BlockSpec((ts, 2 * C), lambda i: (i, 0)), pl.BlockSpec((ts, C), lambda i: (i, 0))],
        out_shape=[jax.ShapeDtypeStruct((S, 2 * C), BF16), jax.ShapeDtypeStruct((S, C), F32)],
        scratch_shapes=[pltpu.VMEM((ts + 2 * HALO, C), F32), pltpu.VMEM((ts + 2 * HALO, C), F32),
                        pltpu.VMEM((rc + 2 * HALO, C), F32), pltpu.VMEM((rc + 2 * HALO, C), F32)],
        compiler_params=_cp("parallel"), name=name)(z, z, z, wa, ba, lng, lnb, wb, bb, *after_ops)


def _conv_bwd_ln(dab, ca, lng, lnb, name):
    S, C = ca.shape
    ts = _pick(S, TS_ROW, 8)

    def body(da_ref, ca_ref, lng_ref, lnb_ref, dca_ref, dg_ref, db_ref, dbias_ref):
        @pl.when(pl.program_id(0) == 0)
        def _():
            dg_ref[...] = jnp.zeros_like(dg_ref)
            db_ref[...] = jnp.zeros_like(db_ref)
            dbias_ref[...] = jnp.zeros_like(dbias_ref)

        ca_ = ca_ref[...]
        mu = jnp.mean(ca_, axis=-1, keepdims=True)
        xc = ca_ - mu
        rstd = lax.rsqrt(jnp.mean(xc * xc, axis=-1, keepdims=True) + LN_EPS)
        xh = xc * rstd
        ln = xh * lng_ref[...] + lnb_ref[...]
        sg = _sigmoid(ln)
        dln = da_ref[...].astype(F32) * (sg * (1.0 + ln * (1.0 - sg)))
        dg_ref[...] += jnp.sum(dln * xh, axis=0, keepdims=True)
        db_ref[...] += jnp.sum(dln, axis=0, keepdims=True)
        dxh = dln * lng_ref[...]
        dca = rstd * (dxh - jnp.mean(dxh, axis=-1, keepdims=True) - xh * jnp.mean(dxh * xh, axis=-1, keepdims=True))
        dca_ref[...] = dca
        dbias_ref[...] += jnp.sum(dca, axis=0, keepdims=True)

    tile = pl.BlockSpec((ts, C), lambda i: (i, 0))
    vec = pl.BlockSpec((1, C), lambda i: (0, 0))
    vsh = jax.ShapeDtypeStruct((1, C), F32)
    return pl.pallas_call(
        body, grid=(S // ts,), in_specs=[tile, tile, vec, vec], out_specs=[tile, vec, vec, vec],
        out_shape=[jax.ShapeDtypeStruct((S, C), F32), vsh, vsh, vsh],
        compiler_params=_cp("arbitrary"), name=name)(dab, ca, lng, lnb)


def _conv_bwd(z, dca, dab, wa, wb, bb, name):
    S = z.shape[0]
    C = z.shape[1] // 5
    KA, KB = wa.shape[0], wb.shape[0]
    pa, pb = KA // 2, KB // 2
    ts = _pick(S, TS_CONV, ROW_CHUNK)
    nt = S // ts
    rc = ROW_CHUNK
    prev0, nxt0 = _halo_specs(ts, C, 0)
    prev1, nxt1 = _halo_specs(ts, C, 1)

    def body(z_ref, zp_ref, zn_ref, dca_ref, dcap_ref, dcan_ref, db_ref, dbp_ref, dbn_ref, wa_ref, wb_ref, bb_ref,
             dz_ref, dwa_ref, dwb_ref, dbb_ref,
             ga_ext, dca_ext, tb_ext, dcb_ext, win_ga, win_dca, win_tb, win_dcb, acc_a, acc_b, acc_bias):
        i = pl.program_id(0)
        first, last = i == 0, i == nt - 1

        @pl.when(first)
        def _():
            acc_a[...] = jnp.zeros_like(acc_a)
            acc_b[...] = jnp.zeros_like(acc_b)
            acc_bias[...] = jnp.zeros_like(acc_bias)

        def glu(r):
            return r[:, 0:C] * _sigmoid(r[:, C:2 * C])

        def gcb(r):
            return r[:, 4 * C:5 * C] * r[:, 2 * C:3 * C]

        def dcb(d, r):
            return d[...].astype(F32) * r[:, 3 * C:4 * C]

        _fill_ext(ga_ext, glu(zp_ref), glu(z_ref), glu(zn_ref), first, last, ts)
        _fill_ext(tb_ext, gcb(zp_ref), gcb(z_ref), gcb(zn_ref), first, last, ts)
        _fill_ext(dca_ext, dcap_ref[...], dca_ref[...], dcan_ref[...], first, last, ts)
        _fill_ext(dcb_ext, dcb(dbp_ref, zp_ref), dcb(db_ref, z_ref), dcb(dbn_ref, zn_ref), first, last, ts)

        def fold(x):
            return jnp.sum(x.reshape(rc // 8, 8, C), axis=0)

        def chunk(c, carry):
            r0 = pl.multiple_of(c * rc, rc)
            win_ga[...] = ga_ext[pl.ds(r0, rc + 2 * HALO), :]
            win_dca[...] = dca_ext[pl.ds(r0, rc + 2 * HALO), :]
            win_tb[...] = tb_ext[pl.ds(r0, rc + 2 * HALO), :]
            win_dcb[...] = dcb_ext[pl.ds(r0, rc + 2 * HALO), :]
            dca_c = win_dca[pl.ds(HALO, rc), :]
            dglu = jnp.zeros((rc, C), F32)
            for k in range(KA):
                dglu = dglu + wa_ref[k:k + 1, :] * win_dca[pl.ds(HALO + pa - k, rc), :]
                acc_a[k] += fold(dca_c * win_ga[pl.ds(HALO - pa + k, rc), :])
            val = z_ref[pl.ds(r0, rc), 0:C]
            sg = _sigmoid(z_ref[pl.ds(r0, rc), C:2 * C])
            dz_ref[pl.ds(r0, rc), 0:C] = (dglu * sg).astype(BF16)
            dz_ref[pl.ds(r0, rc), C:2 * C] = (dglu * val * sg * (1.0 - sg)).astype(BF16)
            dcb_c = win_dcb[pl.ds(HALO, rc), :]
            cb = jnp.zeros((rc, C), F32) + bb_ref[...]
            dt = jnp.zeros((rc, C), F32)
            for k in range(KB):
                tb_k = win_tb[pl.ds(HALO - pb + k, rc), :]
                cb = cb + wb_ref[k:k + 1, :] * tb_k
                dt = dt + wb_ref[k:k + 1, :] * win_dcb[pl.ds(HALO + pb - k, rc), :]
                acc_b[k] += fold(dcb_c * tb_k)
            acc_bias[...] += fold(dcb_c)
            db_c = db_ref[pl.ds(r0, rc), :].astype(F32)
            dz_ref[pl.ds(r0, rc), 2 * C:3 * C] = (dt * z_ref[pl.ds(r0, rc), 4 * C:5 * C]).astype(BF16)
            dz_ref[pl.ds(r0, rc), 3 * C:4 * C] = (db_c * cb).astype(BF16)
            dz_ref[pl.ds(r0, rc), 4 * C:5 * C] = (dt * z_ref[pl.ds(r0, rc), 2 * C:3 * C]).astype(BF16)
            return carry

        lax.fori_loop(0, ts // rc, chunk, 0)

        @pl.when(last)
        def _():
            dwa_ref[...] = jnp.sum(acc_a[...], axis=1)
            dwb_ref[...] = jnp.sum(acc_b[...], axis=1)
            dbb_ref[...] = jnp.sum(acc_bias[...], axis=0, keepdims=True)

    zspec = pl.BlockSpec((ts, 5 * C), lambda i: (i, 0))
    zprev = pl.BlockSpec((HALO, 5 * C), prev0)
    znext = pl.BlockSpec((HALO, 5 * C), lambda i: nxt0(i, nt))
    dspec = pl.BlockSpec((ts, C), lambda i: (i, 0))
    dprev = pl.BlockSpec((HALO, C), prev0)
    dnext = pl.BlockSpec((HALO, C), lambda i: nxt0(i, nt))
    bspec = pl.BlockSpec((ts, C), lambda i: (i, 1))
    bprev = pl.BlockSpec((HALO, C), prev1)
    bnext = pl.BlockSpec((HALO, C), lambda i: nxt1(i, nt))

    def full(shape):
        return pl.BlockSpec(shape, lambda i: (0,) * len(shape))

    ext = pltpu.VMEM((ts + 2 * HALO, C), F32)
    win = pltpu.VMEM((rc + 2 * HALO, C), F32)
    return pl.pallas_call(
        body, grid=(nt,),
        in_specs=[zspec, zprev, znext, dspec, dprev, dnext, bspec, bprev, bnext,
                  full(wa.shape), full(wb.shape), full(bb.shape)],
        out_specs=[pl.BlockSpec((ts, 5 * C), lambda i: (i, 0)), full((KA, C)), full((KB, C)), full((1, C))],
        out_shape=[jax.ShapeDtypeStruct((S, 5 * C), BF16), jax.ShapeDtypeStruct((KA, C), F32),
                   jax.ShapeDtypeStruct((KB, C), F32), jax.ShapeDtypeStruct((1, C), F32)],
        scratch_shapes=[ext, ext, ext, ext, win, win, win, win,
                        pltpu.VMEM((KA, 8, C), F32), pltpu.VMEM((KB, 8, C), F32), pltpu.VMEM((8, C), F32)],
        compiler_params=_cp("arbitrary"), name=name)(z, z, z, dca, dca, dca, dab, dab, dab, wa, wb, bb)


_GELU_C = 0.7978845608028654
_GELU_A = 0.044715


def _gelu(x):
    return 0.5 * x * (1.0 + jnp.tanh(_GELU_C * (x + _GELU_A * (x * x * x))))


def _gelu_grad(x):
    t = jnp.tanh(_GELU_C * (x + _GELU_A * (x * x * x)))
    return 0.5 * (1.0 + t) + 0.5 * x * (1.0 - t * t) * (_GELU_C * (1.0 + 3.0 * _GELU_A * x * x))


def _sgu_fwd(zp, lng, lnb, ws, bsb, name):
    S = zp.shape[0]
    D = zp.shape[1] // 2
    G = ws.shape[0]
    gd = D // G
    ts = _pick(S, TS_SGU, CHUNK)
    ncs = ts // CHUNK

    def body(zp_ref, lng_ref, lnb_ref, ws_ref, bsb_ref, y_ref, vb_ref):
        v = _gelu(zp_ref[:, D:2 * D])
        mu = jnp.mean(v, axis=-1, keepdims=True)
        xc = v - mu
        rstd = lax.rsqrt(jnp.mean(xc * xc, axis=-1, keepdims=True) + LN_EPS)
        vb_ref[...] = (xc * rstd * lng_ref[...] + lnb_ref[...]).astype(BF16)
        for c in range(ncs):
            rows = slice(c * CHUNK, (c + 1) * CHUNK)
            for g in range(G):
                cols = slice(g * gd, (g + 1) * gd)
                sv = _dot(ws_ref[g], vb_ref[rows, cols]) + bsb_ref[:, cols]
                y_ref[rows, cols] = (_gelu(zp_ref[rows, cols]) * sv).astype(BF16)

    def full(a):
        return pl.BlockSpec(a.shape, lambda i: (0,) * a.ndim)

    return pl.pallas_call(
        body, grid=(S // ts,),
        in_specs=[pl.BlockSpec((ts, 2 * D), lambda i: (i, 0)), full(lng), full(lnb), full(ws), full(bsb)],
        out_specs=pl.BlockSpec((ts, D), lambda i: (i, 0)), out_shape=jax.ShapeDtypeStruct((S, D), BF16),
        scratch_shapes=[pltpu.VMEM((ts, D), BF16)],
        compiler_params=_cp("parallel"), name=name)(zp, lng, lnb, ws, bsb)


def _sgu_bwd(dy, zp, lng, lnb, ws, wst, bsb, name):
    S = zp.shape[0]
    D = zp.shape[1] // 2
    G = ws.shape[0]
    gd = D // G
    ts = _pick(S, TS_SGU, CHUNK)
    ncs = ts // CHUNK

    def body(dy_ref, zp_ref, lng_ref, lnb_ref, ws_ref, wst_ref, bsb_ref,
             dzp_ref, dws_ref, dbs_ref, dg_ref, db_ref, vb_ref, dvln_ref, acc_bs):
        i = pl.program_id(0)

        @pl.when(i == 0)
        def _():
            dws_ref[...] = jnp.zeros_like(dws_ref)
            acc_bs[...] = jnp.zeros_like(acc_bs)
            dg_ref[...] = jnp.zeros_like(dg_ref)
            db_ref[...] = jnp.zeros_like(db_ref)

        zv = zp_ref[:, D:2 * D]
        v = _gelu(zv)
        mu = jnp.mean(v, axis=-1, keepdims=True)
        xc = v - mu
        rstd = lax.rsqrt(jnp.mean(xc * xc, axis=-1, keepdims=True) + LN_EPS)
        xh = xc * rstd
        vb_ref[...] = (xh * lng_ref[...] + lnb_ref[...]).astype(BF16)
        for c in range(ncs):
            rows = slice(c * CHUNK, (c + 1) * CHUNK)
            for g in range(G):
                cols = slice(g * gd, (g + 1) * gd)
                zu = zp_ref[rows, cols]
                u = _gelu(zu)
                dy_ = dy_ref[rows, cols].astype(F32)
                sv = _dot(ws_ref[g], vb_ref[rows, cols]) + bsb_ref[:, cols]
                dzp_ref[rows, cols] = (dy_ * sv * _gelu_grad(zu)).astype(BF16)
                dsv = dy_ * u
                acc_bs[:, cols] += dsv
                dsvb = dsv.astype(BF16)
                dws_ref[g] += _dot_nt(dsvb, vb_ref[rows, cols])
                dvln_ref[rows, cols] = _dot(wst_ref[g], dsvb)
        dvln = dvln_ref[...]
        dg_ref[...] += jnp.sum(dvln * xh, axis=0, keepdims=True)
        db_ref[...] += jnp.sum(dvln, axis=0, keepdims=True)
        dxh = dvln * lng_ref[...]
        dv = rstd * (dxh - jnp.mean(dxh, axis=-1, keepdims=True) - xh * jnp.mean(dxh * xh, axis=-1, keepdims=True))
        dzp_ref[:, D:2 * D] = (dv * _gelu_grad(zv)).astype(BF16)

        @pl.when(i == pl.num_programs(0) - 1)
        def _():
            dbs_ref[...] = acc_bs[...]

    def full(shape):
        return pl.BlockSpec(shape, lambda i: (0,) * len(shape))

    return pl.pallas_call(
        body, grid=(S // ts,),
        in_specs=[pl.BlockSpec((ts, D), lambda i: (i, 0)), pl.BlockSpec((ts, 2 * D), lambda i: (i, 0)),
                  full(lng.shape), full(lnb.shape), full(ws.shape), full(wst.shape), full(bsb.shape)],
        out_specs=[pl.BlockSpec((ts, 2 * D), lambda i: (i, 0)), full(ws.shape), full(bsb.shape),
                   full((1, D)), full((1, D))],
        out_shape=[jax.ShapeDtypeStruct((S, 2 * D), BF16), jax.ShapeDtypeStruct(ws.shape, F32),
                   jax.ShapeDtypeStruct(bsb.shape, F32), jax.ShapeDtypeStruct((1, D), F32),
                   jax.ShapeDtypeStruct((1, D), F32)],
        scratch_shapes=[pltpu.VMEM((ts, D), BF16), pltpu.VMEM((ts, D), F32),
                        pltpu.VMEM(bsb.shape, F32)],
        compiler_params=_cp("arbitrary"), name=name)(dy, zp, lng, lnb, ws, wst, bsb)


def _group_sum(x, groups, name):
    P, D = x.shape
    gd = D // groups

    def body(x_ref, o_ref):
        for g in range(groups):
            o_ref[:, g:g + 1] = jnp.sum(x_ref[:, g * gd:(g + 1) * gd], axis=1, keepdims=True)

    return pl.pallas_call(body, out_shape=jax.ShapeDtypeStruct((P, groups), F32), name=name)(x)


def _adamw(w, g, m, v, name):
    shape = w.shape
    C = shape[-1]
    R = w.size // C
    tr = _pick(R, 1024, 8)
    bc1 = 1.0 - ADAM_B1 ** ADAM_STEP
    bc2 = 1.0 - ADAM_B2 ** ADAM_STEP

    def body(w_ref, g_ref, m_ref, v_ref, d_ref, nm_ref, nv_ref):
        g_ = g_ref[...]
        nm = ADAM_B1 * m_ref[...] + (1.0 - ADAM_B1) * g_
        nv = ADAM_B2 * v_ref[...] + (1.0 - ADAM_B2) * (g_ * g_)
        nm_ref[...] = nm
        nv_ref[...] = nv
        d_ref[...] = -ADAM_LR * ((nm / bc1) / (jnp.sqrt(nv / bc2) + ADAM_EPS) + ADAM_WD * w_ref[...])

    tile = pl.BlockSpec((tr, C), lambda i: (i, 0))
    sh = jax.ShapeDtypeStruct((R, C), F32)
    outs = pl.pallas_call(
        body, grid=(R // tr,), in_specs=[tile] * 4, out_specs=[tile] * 3, out_shape=[sh] * 3,
        compiler_params=_cp("parallel"), name=name)(*(a.reshape(R, C) for a in (w, g, m, v)))
    return tuple(o.reshape(shape) for o in outs)


_HBM = pl.BlockSpec(memory_space=pltpu.HBM)


def _remote(src, dst, send_sem, recv_sem, to):
    return pltpu.make_async_remote_copy(src_ref=src, dst_ref=dst, send_sem=send_sem, recv_sem=recv_sem,
                                        device_id=to, device_id_type=MESH)


def _all_gather(block, name):
    R, C = block.shape

    def body(x_ref, out_ref, send_sems, recv_sems, local_sem):
        x, y, c = lax.axis_index("x"), lax.axis_index("y"), lax.axis_index("c")
        me, sibling = (x, y, c), (x, y, 1 - c)
        chips = [(1 - x, y), (x, 1 - y), (1 - x, 1 - y)]

        def slot(px, py, pc):
            return out_ref.at[4 * px + 2 * py + pc]

        def copy(k, blk, to, src=None):
            return _remote(slot(*blk) if src is None else src, slot(*blk), send_sems.at[k], recv_sems.at[k], to)

        mine = pltpu.make_async_copy(x_ref, slot(*me), local_sem)
        mine.start()
        first = [copy(0, me, sibling, src=x_ref)]
        first += [copy(1 + j, me, (*chip, c), src=x_ref) for j, chip in enumerate(chips)]
        for cp in first:
            cp.start()
        passed = [copy(4 + j, (*chip, c), sibling) for j, chip in enumerate(chips)]
        for j, chip in enumerate(chips):
            copy(1 + j, (*chip, c), me).wait_recv()
            passed[j].start()
        copy(0, sibling, me).wait_recv()
        for j, chip in enumerate(chips):
            copy(4 + j, (*chip, 1 - c), me).wait_recv()
        for cp in first + passed:
            cp.wait_send()
        mine.wait()

    return pl.pallas_call(
        body, out_shape=jax.ShapeDtypeStruct((NDEV, R, C), block.dtype), in_specs=[_HBM], out_specs=_HBM,
        scratch_shapes=[pltpu.SemaphoreType.DMA((7,)), pltpu.SemaphoreType.DMA((7,)), pltpu.SemaphoreType.DMA],
        name=name)(block)


def _all_gather_weights(pack, rows, name, after=None):
    C = pack.shape[1]
    nw = len(rows)
    starts = [sum(rows[:w]) for w in range(nw)]
    after_specs, after_ops = _after(after)

    def body(pack_ref, *rest):
        rest = rest[len(after_ops):]
        outs = rest[:nw]
        send_sems, recv_sems, local_sem = rest[nw:]
        x, y, c = lax.axis_index("x"), lax.axis_index("y"), lax.axis_index("c")
        me, sibling = (x, y, c), (x, y, 1 - c)
        chips = [(1 - x, y), (x, 1 - y), (1 - x, 1 - y)]

        def block(w, px, py, pc):
            return outs[w].at[pl.ds((4 * px + 2 * py + pc) * rows[w], rows[w])]

        def mine(w):
            return pack_ref.at[pl.ds(starts[w], rows[w])]

        def all_of(k):
            return _remote(pack_ref, pack_ref, send_sems.at[k], recv_sems.at[k], me)

        for w in range(nw):
            pltpu.make_async_copy(mine(w), block(w, *me), local_sem).start()
        for k, to in enumerate([sibling] + [(*chip, c) for chip in chips]):
            for w in range(nw):
                _remote(mine(w), block(w, *me), send_sems.at[k], recv_sems.at[k], to).start()
        for j, chip in enumerate(chips):
            all_of(1 + j).wait_recv()
            for w in range(nw):
                _remote(block(w, *chip, c), block(w, *chip, c), send_sems.at[4 + j], recv_sems.at[4 + j], sibling).start()
        all_of(0).wait_recv()
        for j in range(3):
            all_of(4 + j).wait_recv()
        for k in range(7):
            all_of(k).wait_send()
        pltpu.make_async_copy(pack_ref, pack_ref, local_sem).wait()

    return pl.pallas_call(
        body, out_shape=[jax.ShapeDtypeStruct((NDEV * r, C), pack.dtype) for r in rows],
        in_specs=[_HBM] + after_specs, out_specs=[_HBM] * nw,
        scratch_shapes=[pltpu.SemaphoreType.DMA((7,)), pltpu.SemaphoreType.DMA((7,)), pltpu.SemaphoreType.DMA],
        name=name)(pack, *after_ops)


_SEM = pl.BlockSpec(memory_space=pltpu.SEMAPHORE)
_DATAFLOW = pltpu.SideEffectType.DATAFLOW_SIDE_EFFECTING


def _split_start(srcs, lands, plan, n, after, name):
    nbuf = len(srcs) + len(lands)
    after_specs, after_ops = _after(after)

    def body(*refs):
        src_refs, land_refs = refs[:len(srcs)], refs[len(srcs):nbuf]
        send_sems, recv_sems = refs[nbuf + len(after_ops)], refs[nbuf + len(after_ops) + 1]
        for k, (src, dst, to) in enumerate(plan(src_refs, land_refs)):
            _remote(src, dst, send_sems.at[k], recv_sems.at[k], to).start()
        refs[-1][...] = jnp.zeros_like(refs[-1])

    bufs = [pltpu.with_memory_space_constraint(a, pltpu.HBM) for a in list(srcs) + list(lands)]
    outs = pl.pallas_call(
        body, name=name,
        out_shape=(pltpu.SemaphoreType.DMA((n,)), pltpu.SemaphoreType.DMA((n,)),
                   *[pltpu.HBM(a.shape, a.dtype) for a in bufs], jax.ShapeDtypeStruct((8, 128), F32)),
        in_specs=[_HBM] * nbuf + after_specs,
        out_specs=(_SEM, _SEM, *[_HBM] * nbuf, pl.BlockSpec(memory_space=pltpu.VMEM)),
        input_output_aliases={i: 2 + i for i in range(nbuf)},
        compiler_params=pltpu.CompilerParams(has_side_effects=_DATAFLOW))(*bufs, *after_ops)
    return outs[0], outs[1], list(outs[2:2 + len(srcs)]), list(outs[2 + len(srcs):2 + nbuf]), outs[-1]


def _split_wait(send_sems, recv_sems, srcs, lands, plan, after, name):
    nbuf = len(srcs) + len(lands)
    after_specs, after_ops = _after(after)

    def body(*refs):
        src_refs, land_refs = refs[:len(srcs)], refs[len(srcs):nbuf]
        send_sems_ref, recv_sems_ref = refs[nbuf], refs[nbuf + 1]
        for k, (src, dst, to) in enumerate(plan(src_refs, land_refs)):
            copy = _remote(src, dst, send_sems_ref.at[k], recv_sems_ref.at[k], to)
            copy.wait_send()
            copy.wait_recv()

    outs = pl.pallas_call(
        body, name=name, out_shape=tuple(pltpu.HBM(a.shape, a.dtype) for a in list(srcs) + list(lands)),
        in_specs=[_HBM] * nbuf + [_SEM, _SEM] + after_specs, out_specs=tuple([_HBM] * nbuf),
        input_output_aliases={i: i for i in range(nbuf)},
        compiler_params=pltpu.CompilerParams(has_side_effects=_DATAFLOW))(*srcs, *lands, send_sems, recv_sems, *after_ops)
    return list(outs[:len(srcs)]), list(outs[len(srcs):])


def _peers(x, y, c):
    return [(mask, (1 - x if mask & 4 else x, 1 - y if mask & 2 else y, 1 - c if mask & 1 else c))
            for mask in range(1, NDEV)]


def _gather_plan(rows):
    starts = [sum(rows[:w]) for w in range(len(rows))]

    def plan(src_refs, land_refs):
        x, y, c = lax.axis_index("x"), lax.axis_index("y"), lax.axis_index("c")
        copies = []
        for w, r in enumerate(rows):
            mine = src_refs[0].at[pl.ds(starts[w], r)]
            dst = land_refs[w].at[pl.ds((4 * x + 2 * y + c) * r, r)]
            copies += [(mine, dst, peer) for _, peer in _peers(x, y, c)]
        return copies

    return plan, (NDEV - 1) * len(rows)


def _place_own(shards, fulls, dev_idx, name):
    nw = len(shards)

    def body(i_ref, *refs):
        for w in range(nw):
            refs[2 * nw + w][...] = refs[w][...]

    grid_spec = pltpu.PrefetchScalarGridSpec(
        num_scalar_prefetch=1, grid=(1,),
        in_specs=[pl.BlockSpec(s.shape, lambda t, i_ref: (0, 0)) for s in shards] + [_ANY] * nw,
        out_specs=[pl.BlockSpec(s.shape, lambda t, i_ref: (i_ref[0], 0)) for s in shards])
    outs = pl.pallas_call(
        body, grid_spec=grid_spec, out_shape=[jax.ShapeDtypeStruct(f.shape, f.dtype) for f in fulls],
        input_output_aliases={1 + nw + w: w for w in range(nw)}, name=name)(dev_idx, *shards, *fulls)
    return list(outs)


def _scatter_plan(rows):
    def plan(src_refs, land_refs):
        x, y, c = lax.axis_index("x"), lax.axis_index("y"), lax.axis_index("c")
        copies = []
        for w, r in enumerate(rows):
            for mask, (px, py, pc) in _peers(x, y, c):
                src = src_refs[w].at[pl.ds((4 * px + 2 * py + pc) * r, r)]
                copies.append((src, land_refs[w].at[mask - 1], (px, py, pc)))
        return copies

    return plan, (NDEV - 1) * len(rows)


def _scatter_sum(g, got, dev_idx, name):
    n1, r, C = got.shape
    tr = _pick(r, 512, 16)
    per = r // tr

    def body(i_ref, g_ref, got_ref, o_ref):
        acc = g_ref[...].astype(F32)
        for k in range(n1):
            acc = acc + got_ref[k].astype(F32)
        o_ref[...] = acc

    grid_spec = pltpu.PrefetchScalarGridSpec(
        num_scalar_prefetch=1, grid=(per,),
        in_specs=[pl.BlockSpec((tr, C), lambda t, i_ref: (i_ref[0] * per + t, 0)),
                  pl.BlockSpec((n1, tr, C), lambda t, i_ref: (0, t, 0))],
        out_specs=pl.BlockSpec((tr, C), lambda t, i_ref: (t, 0)))
    return pl.pallas_call(
        body, grid_spec=grid_spec, out_shape=jax.ShapeDtypeStruct((r, C), F32),
        compiler_params=_cp("parallel"), name=name)(dev_idx, g, got)


def _sum_slots(a, name):
    n, R, C = a.shape

    def body(a_ref, o_ref):
        acc = a_ref[0]
        for k in range(1, n):
            acc = acc + a_ref[k]
        o_ref[...] = acc

    return pl.pallas_call(body, out_shape=jax.ShapeDtypeStruct((R, C), F32), name=name)(a)


def _shard_axis(name):
    return {"ev_w_in": 2, "ev_a_conv_w": 2, "ev_b_conv_w": 2, "ev_w_out": 1, "od_w_in": 2, "od_c_ln_g": 1,
            "od_c_ln_b": 1, "od_w_out": 1, "xa_w_q": 1, "xa_w_k": 1, "xa_w_v": 1, "xa_w_o": 1,
            "ffn_w_gate": 2, "ffn_w_up": 2, "ffn_w_down": 1}[name]


BIG = ["ev_w_in", "ev_w_out", "od_w_in", "od_w_out", "xa_w_q", "xa_w_k", "xa_w_v", "xa_w_o",
       "ffn_w_gate", "ffn_w_up", "ffn_w_down"]
SMALL_SHARDED = ["ev_a_conv_w", "ev_b_conv_w", "od_c_ln_g", "od_c_ln_b"]
REPLICATED = ["g_mix", "g_xattn", "g_mem", "g_ffn", "g_final", "ev_a_conv_b", "ev_a_ln_g", "ev_a_ln_b",
              "ev_b_conv_b", "od_w_s", "od_b_s"]
WEIGHTS = ["g_mix", "g_xattn", "g_mem", "g_ffn", "g_final", "ev_w_in", "ev_a_conv_w", "ev_a_conv_b", "ev_a_ln_g",
           "ev_a_ln_b", "ev_b_conv_w", "ev_b_conv_b", "ev_w_out", "od_w_in", "od_c_ln_g", "od_c_ln_b", "od_w_s",
           "od_b_s", "od_w_out", "xa_w_q", "xa_w_k", "xa_w_v", "xa_w_o", "ffn_w_gate", "ffn_w_up", "ffn_w_down"]


def _full_from_blocks(blocks, axis):
    shard = blocks.shape[1:]
    full = jnp.moveaxis(blocks, 0, axis)
    return full.reshape(shard[:axis] + (NDEV * shard[axis],) + shard[axis + 1:])


def _blocks_from_full(full, axis):
    shp = full.shape
    split = full.reshape(shp[:axis] + (NDEV, shp[axis] // NDEV) + shp[axis + 1:])
    return jnp.moveaxis(split, axis, 0)


def _pad_rows(flat, width, row_align):
    per = width * row_align
    n = -(-flat.shape[0] // per) * per
    return jnp.pad(flat, (0, n - flat.shape[0])).reshape(n // width, width)


def _row(v):
    return v.reshape(1, -1)


def _xattn_fwd(h, nq, mem, g_m, wq, wk, wv, wo, g_next, tag, after):
    mem_n = _rms_fwd(mem, _row(g_m), f"xa_mem_rms_{tag}")
    q = _mm([(nq, wq, "nn")], f"xa_q_{tag}", out_dtype=BF16, after=after)
    k = _mm([(mem_n, wk, "nn")], f"xa_k_{tag}", out_dtype=BF16)
    v = _mm([(mem_n, wv, "nn")], f"xa_v_{tag}", out_dtype=BF16)
    o = _attn_fwd(q, k, v, f"xa_attn_{tag}")
    h_new, n_next = _mm([(o, wo, "nn")], f"xa_o_{tag}", res=h, rms_g=_row(g_next))
    return h_new, n_next, (h, nq, mem_n, q, k, v, o)


def _xattn_bwd(dh_new, saved, mem, g_x, g_m, wq, wk, wv, wo, tag, push):
    h, nq, mem_n, q, k, v, o = saved
    do = _mm([(dh_new, wo, "nt")], f"xa_do_{tag}", out_dtype=BF16)
    d_wo = _mm_tn(o, dh_new, f"xa_dwo_{tag}")
    dq, dk, dv = _attn_bwd(q, k, v, do, f"xa_attn_bwd_{tag}")
    d_wq = _mm_tn(nq, dq, f"xa_dwq_{tag}")
    d_wk = _mm_tn(mem_n, dk, f"xa_dwk_{tag}")
    d_wv = _mm_tn(mem_n, dv, f"xa_dwv_{tag}")
    token = push([d_wq, d_wk, d_wv, d_wo])
    dmem_n = _mm([(dk, wk, "nt"), (dv, wv, "nt")], f"xa_dmem_{tag}", after=token)
    _, d_gm = _rms_bwd(dmem_n, mem, _row(g_m), None, f"xa_mem_rms_bwd_{tag}")
    dh, d_gx = _mm([(dq, wq, "nt")], f"xa_dnq_{tag}", rms_bwd=(h, _row(g_x), dh_new), tm=512, after=token)
    return dh, dict(g_xattn=d_gx, g_mem=d_gm)


def _ffn_fwd(h, n, wgt, wut, wd, g_next, tag, after):
    a, b, hid = _ffn_up(n, wgt, wut, f"ffn_up_{tag}", after=after)
    if g_next is None:
        h_new, n_next = _mm([(hid, wd, "nn")], f"ffn_down_{tag}", res=h, tm=512, tn=1024), None
    else:
        h_new, n_next = _mm([(hid, wd, "nn")], f"ffn_down_{tag}", res=h, rms_g=_row(g_next), tm=512)
    return h_new, n_next, (h, n, a, b, hid)


def _ffn_bwd(dh_new, saved, g_f, wgt, wut, wd, tag, push):
    h, n, a, b, hid = saved
    da, db = _ffn_dhid(dh_new, wd, a, b, f"ffn_dhid_{tag}")
    d_wd = _mm_tn(hid, dh_new, f"ffn_dwd_{tag}", tn=512)
    d_wgt = _mm_tn(da, n, f"ffn_dwg_{tag}", tn=512)
    d_wut = _mm_tn(db, n, f"ffn_dwu_{tag}", tn=512)
    token = push([d_wgt, d_wut, d_wd])
    dh, d_gf = _mm([(da, wgt, "nn"), (db, wut, "nn")], f"ffn_dn_{tag}", rms_bwd=(h, _row(g_f), dh_new), tm=256,
                   after=token)
    return dh, dict(g_ffn=d_gf)


_XA = ["xa_w_q", "xa_w_k", "xa_w_v", "xa_w_o"]
_FFN = ["ffn_w_gate", "ffn_w_up", "ffn_w_down"]
GATHERS = {
    "ev_in": [("ev_w_in", 0)],
    "xa0": [("ev_w_out", 0)] + [(n, 0) for n in _XA],
    "ffn0": [(n, 0) for n in _FFN],
    "od": [("od_w_in", 0), ("od_w_out", 0)],
    "xa1": [(n, 1) for n in _XA],
    "ffn1": [(n, 1) for n in _FFN],
}
SCATTERS = {
    "ffn1": [(n, 1) for n in _FFN],
    "xa1": [(n, 1) for n in _XA],
    "od": [("od_w_in", 0), ("od_w_out", 0)],
    "ffn0": [(n, 0) for n in _FFN],
    "xa0": [(n, 0) for n in _XA],
    "ev": [("ev_w_in", 0), ("ev_w_out", 0)],
}


def _local_step(x, mem, loss_target, W, comm):
    grads = {}

    h0 = x
    (ev_w_in_t,), token = comm.weights("ev_in", None)
    n0 = _rms_fwd(h0, _row(W["g_mix"][0]), "ev_rms", after=token)
    z = _mm([(n0, ev_w_in_t, "nt")], "ev_in", tn=1280)
    token = comm.prefetch(["ffn0"], z)
    ab, ca = _conv_fwd(z, W["ev_a_conv_w"][0], W["ev_a_conv_b"], W["ev_a_ln_g"], W["ev_a_ln_b"],
                       W["ev_b_conv_w"][0], W["ev_b_conv_b"], "ev_conv", after=token)
    (ev_w_out, *xa_w0), _ = comm.weights("xa0", ab)
    h1, nq0 = _mm([(ab, ev_w_out, "nn")], "ev_out", res=h0, rms_g=_row(W["g_xattn"][0]))
    token = comm.prefetch(["od", "xa1"], nq0)
    h2, nf0, xa0 = _xattn_fwd(h1, nq0, mem, W["g_mem"][0], *xa_w0, W["g_ffn"][0], "l0", token)
    ffn_w0, _ = comm.weights("ffn0", nf0)
    token = comm.prefetch(["ffn1"], nf0)
    h3, n3, ff0 = _ffn_fwd(h2, nf0, *ffn_w0, W["g_mix"][1], "l0", token)

    (od_w_in_t, od_w_out), _ = comm.weights("od", n3)
    zp = _mm([(n3, od_w_in_t, "nt")], "od_in", tn=1024)
    D = x.shape[1]
    ws = W["od_w_s"][0].astype(BF16)
    wst = jnp.swapaxes(ws, 1, 2)
    bsb = jnp.repeat(jnp.transpose(W["od_b_s"][0]), D // C_GROUPS, axis=1)
    y_sgu = _sgu_fwd(zp, W["od_c_ln_g"], W["od_c_ln_b"], ws, bsb, "od_sgu")
    h4, nq1 = _mm([(y_sgu, od_w_out, "nn")], "od_out", res=h3, rms_g=_row(W["g_xattn"][1]))
    xa_w1, _ = comm.weights("xa1", nq1)
    h5, nf1, xa1 = _xattn_fwd(h4, nq1, mem, W["g_mem"][1], *xa_w1, W["g_ffn"][1], "l1", None)
    ffn_w1, _ = comm.weights("ffn1", nf1)
    h6, _, ff1 = _ffn_fwd(h5, nf1, *ffn_w1, None, "l1", None)

    loss_row, dh6, d_gfinal = _loss_bwd(h6, _row(W["g_final"]), loss_target, "loss")
    grads["g_final"] = d_gfinal.reshape(-1)

    dh5, g_ff1 = _ffn_bwd(dh6, ff1, W["g_ffn"][1], *ffn_w1, "l1", lambda dws: comm.grads("ffn1", dws))
    dh4, g_xa1 = _xattn_bwd(dh5, xa1, mem, W["g_xattn"][1], W["g_mem"][1], *xa_w1, "l1",
                            lambda dws: comm.grads("xa1", dws))
    dy_sgu = _mm([(dh4, od_w_out, "nt")], "od_dy", tn=1024)
    d_od_out = _mm_tn(y_sgu, dh4, "od_dwout", tn=1024)
    dzp, d_ws, d_bsb, d_clng, d_clnb = _sgu_bwd(dy_sgu, zp, W["od_c_ln_g"], W["od_c_ln_b"], ws, wst, bsb, "od_sgu_bwd")
    grads["od_w_s"] = d_ws[None]
    grads["od_b_s"] = jnp.transpose(_group_sum(d_bsb, C_GROUPS, "od_dbs"))[None]
    grads["od_c_ln_g"], grads["od_c_ln_b"] = d_clng, d_clnb
    token = comm.grads("od", [_mm_tn(dzp, n3, "od_dwin", tn=512), d_od_out])
    dh3, d_gmix1 = _mm([(dzp, od_w_in_t, "nn")], "od_dn", rms_bwd=(h3, _row(W["g_mix"][1]), dh4), tm=512, after=token)

    dh2, g_ff0 = _ffn_bwd(dh3, ff0, W["g_ffn"][0], *ffn_w0, "l0", lambda dws: comm.grads("ffn0", dws))
    dh1, g_xa0 = _xattn_bwd(dh2, xa0, mem, W["g_xattn"][0], W["g_mem"][0], *xa_w0, "l0",
                            lambda dws: comm.grads("xa0", dws))
    dab = _mm([(dh1, ev_w_out, "nt")], "ev_dab", tn=1024)
    d_ev_out = _mm_tn(ab, dh1, "ev_dwout", tn=1024)
    dca, d_lng, d_lnb, d_ba = _conv_bwd_ln(dab, ca, W["ev_a_ln_g"], W["ev_a_ln_b"], "ev_conv_bwd_ln")
    dz, d_wa, d_wb, d_bb = _conv_bwd(z, dca, dab, W["ev_a_conv_w"][0], W["ev_b_conv_w"][0], W["ev_b_conv_b"],
                                     "ev_conv_bwd")
    grads.update(ev_a_ln_g=d_lng, ev_a_ln_b=d_lnb, ev_a_conv_b=d_ba, ev_b_conv_b=d_bb,
                 ev_a_conv_w=d_wa[None], ev_b_conv_w=d_wb[None])
    token = comm.grads("ev", [_mm_tn(dz, n0, "ev_dwin", tn=512), d_ev_out])
    grad_x, d_gmix0 = _mm([(dz, ev_w_in_t, "nn")], "ev_dn", rms_bwd=(h0, _row(W["g_mix"][0]), dh1), tm=512, after=token)

    grads["g_mix"] = jnp.concatenate([d_gmix0, d_gmix1], axis=0)
    for key in ("g_xattn", "g_mem"):
        grads[key] = jnp.concatenate([g_xa0[key], g_xa1[key]], axis=0)
    grads["g_ffn"] = jnp.concatenate([g_ff0["g_ffn"], g_ff1["g_ffn"]], axis=0)
    return loss_row, grad_x, grads


class _Exchanges:
    def __init__(self, shards, dev_idx, after):
        self.shards, self.dev_idx = shards, dev_idx
        self.gathering, self.scattering = {}, {}
        self.first = _all_gather_weights(self._pack(GATHERS["ev_in"]), self._rows(GATHERS["ev_in"]), "ag_ev_in",
                                         after=after)
        self.first_token = self.prefetch(["xa0"], self.first[0])

    def _rows(self, entries):
        return [self.shards[e].shape[0] for e in entries]

    def _pack(self, entries):
        return jnp.concatenate([self.shards[e] for e in entries], axis=0)

    def prefetch(self, gathers, after):
        for name in gathers:
            rows = self._rows(GATHERS[name])
            pack = self._pack(GATHERS[name])
            lands = [lax.empty((NDEV * r, pack.shape[1]), pack.dtype) for r in rows]
            plan, n = _gather_plan(rows)
            send, recv, srcs, lands, after = _split_start([pack], lands, plan, n, after, f"ag_{name}_start")
            self.gathering[name] = (send, recv, srcs, lands, plan, rows)
        return after

    def weights(self, name, after):
        if name == "ev_in":
            return self.first, self.first_token
        send, recv, srcs, lands, plan, rows = self.gathering.pop(name)
        _, lands = _split_wait(send, recv, srcs, lands, plan, after, f"ag_{name}_wait")
        return _place_own([self.shards[e] for e in GATHERS[name]], lands, self.dev_idx, f"ag_{name}_own"), None

    def grads(self, name, dws):
        rows = self._rows(SCATTERS[name])
        lands = [lax.empty((NDEV - 1, r, d.shape[1]), d.dtype) for r, d in zip(rows, dws)]
        plan, n = _scatter_plan(rows)
        send, recv, srcs, lands, token = _split_start(dws, lands, plan, n, None, f"rs_{name}_start")
        self.scattering[name] = (send, recv, srcs, lands, plan)
        return token

    def reduced(self, after):
        out = {}
        for name, (send, recv, srcs, lands, plan) in self.scattering.items():
            srcs, lands = _split_wait(send, recv, srcs, lands, plan, after, f"rs_{name}_wait")
            for (w, i), g, got in zip(SCATTERS[name], srcs, lands):
                out[w, i] = _scatter_sum(g, got, self.dev_idx, f"rs_sum_{w}_{i}")
        return out


def kernel(x, mem, g_mix, g_xattn, g_mem, g_ffn, g_final, ev_w_in, ev_a_conv_w, ev_a_conv_b, ev_a_ln_g, ev_a_ln_b, ev_b_conv_w, ev_b_conv_b, ev_w_out, od_w_in, od_c_ln_g, od_c_ln_b, od_w_s, od_b_s, od_w_out, xa_w_q, xa_w_k, xa_w_v, xa_w_o, ffn_w_gate, ffn_w_up, ffn_w_down, loss_target, m_g_mix, m_g_xattn, m_g_mem, m_g_ffn, m_g_final, m_ev_w_in, m_ev_a_conv_w, m_ev_a_conv_b, m_ev_a_ln_g, m_ev_a_ln_b, m_ev_b_conv_w, m_ev_b_conv_b, m_ev_w_out, m_od_w_in, m_od_c_ln_g, m_od_c_ln_b, m_od_w_s, m_od_b_s, m_od_w_out, m_xa_w_q, m_xa_w_k, m_xa_w_v, m_xa_w_o, m_ffn_w_gate, m_ffn_w_up, m_ffn_w_down, v_g_mix, v_g_xattn, v_g_mem, v_g_ffn, v_g_final, v_ev_w_in, v_ev_a_conv_w, v_ev_a_conv_b, v_ev_a_ln_g, v_ev_a_ln_b, v_ev_b_conv_w, v_ev_b_conv_b, v_ev_w_out, v_od_w_in, v_od_c_ln_g, v_od_c_ln_b, v_od_w_s, v_od_b_s, v_od_w_out, v_xa_w_q, v_xa_w_k, v_xa_w_v, v_xa_w_o, v_ffn_w_gate, v_ffn_w_up, v_ffn_w_down):
    local = dict(g_mix=g_mix, g_xattn=g_xattn, g_mem=g_mem, g_ffn=g_ffn, g_final=g_final, ev_w_in=ev_w_in, ev_a_conv_w=ev_a_conv_w, ev_a_conv_b=ev_a_conv_b, ev_a_ln_g=ev_a_ln_g, ev_a_ln_b=ev_a_ln_b, ev_b_conv_w=ev_b_conv_w, ev_b_conv_b=ev_b_conv_b, ev_w_out=ev_w_out, od_w_in=od_w_in, od_c_ln_g=od_c_ln_g, od_c_ln_b=od_c_ln_b, od_w_s=od_w_s, od_b_s=od_b_s, od_w_out=od_w_out, xa_w_q=xa_w_q, xa_w_k=xa_w_k, xa_w_v=xa_w_v, xa_w_o=xa_w_o, ffn_w_gate=ffn_w_gate, ffn_w_up=ffn_w_up, ffn_w_down=ffn_w_down)
    mom = dict(g_mix=m_g_mix, g_xattn=m_g_xattn, g_mem=m_g_mem, g_ffn=m_g_ffn, g_final=m_g_final, ev_w_in=m_ev_w_in, ev_a_conv_w=m_ev_a_conv_w, ev_a_conv_b=m_ev_a_conv_b, ev_a_ln_g=m_ev_a_ln_g, ev_a_ln_b=m_ev_a_ln_b, ev_b_conv_w=m_ev_b_conv_w, ev_b_conv_b=m_ev_b_conv_b, ev_w_out=m_ev_w_out, od_w_in=m_od_w_in, od_c_ln_g=m_od_c_ln_g, od_c_ln_b=m_od_c_ln_b, od_w_s=m_od_w_s, od_b_s=m_od_b_s, od_w_out=m_od_w_out, xa_w_q=m_xa_w_q, xa_w_k=m_xa_w_k, xa_w_v=m_xa_w_v, xa_w_o=m_xa_w_o, ffn_w_gate=m_ffn_w_gate, ffn_w_up=m_ffn_w_up, ffn_w_down=m_ffn_w_down)
    vel = dict(g_mix=v_g_mix, g_xattn=v_g_xattn, g_mem=v_g_mem, g_ffn=v_g_ffn, g_final=v_g_final, ev_w_in=v_ev_w_in, ev_a_conv_w=v_ev_a_conv_w, ev_a_conv_b=v_ev_a_conv_b, ev_a_ln_g=v_ev_a_ln_g, ev_a_ln_b=v_ev_a_ln_b, ev_b_conv_w=v_ev_b_conv_w, ev_b_conv_b=v_ev_b_conv_b, ev_w_out=v_ev_w_out, od_w_in=v_od_w_in, od_c_ln_g=v_od_c_ln_g, od_c_ln_b=v_od_c_ln_b, od_w_s=v_od_w_s, od_b_s=v_od_b_s, od_w_out=v_od_w_out, xa_w_q=v_xa_w_q, xa_w_k=v_xa_w_k, xa_w_v=v_xa_w_v, xa_w_o=v_xa_w_o, ffn_w_gate=v_ffn_w_gate, ffn_w_up=v_ffn_w_up, ffn_w_down=v_ffn_w_down)
    D = x.shape[-1]
    dev = 4 * lax.axis_index("x") + 2 * lax.axis_index("y") + lax.axis_index("c")

    def comm_layout(n, a):
        return jnp.transpose(a) if _shard_axis(n) == 2 else a

    shards = {(n, i): comm_layout(n, local[n][i]).astype(BF16) for n in BIG for i in range(local[n].shape[0])}
    small_sizes = [local[n].size for n in SMALL_SHARDED]
    small_block = _pad_rows(jnp.concatenate([local[n].reshape(-1) for n in SMALL_SHARDED]), 128, 8)
    small_all = _all_gather(small_block, "ag_small")
    comm = _Exchanges(shards, jnp.reshape(dev, (1,)).astype(jnp.int32), small_all)
    small_all = small_all.reshape(NDEV, -1)

    W = {n: local[n] for n in REPLICATED}
    o0 = 0
    for n, sz in zip(SMALL_SHARDED, small_sizes):
        blocks = small_all[:, o0:o0 + sz].reshape((NDEV,) + local[n].shape)
        W[n] = _full_from_blocks(blocks, _shard_axis(n))
        o0 += sz

    loss_row, grad_x, grads = _local_step(x[0], mem[0], loss_target[0], W, comm)

    reduced = comm.reduced(grad_x)
    gsh = {n: jnp.stack([comm_layout(n, reduced[n, i]) for i in range(local[n].shape[0])]) for n in BIG}

    rest = REPLICATED + SMALL_SHARDED
    rest_full_shapes = [grads[n].shape for n in rest]
    g_rest = _pad_rows(jnp.concatenate([grads[n].astype(F32).reshape(-1) for n in rest]), D, 8)
    g_rest = _sum_slots(_all_gather(g_rest, "ag_small_grads"), "sum_small_grads").reshape(-1)
    o0 = 0
    for n, shp in zip(rest, rest_full_shapes):
        sz = 1
        for s in shp:
            sz *= s
        full = g_rest[o0:o0 + sz].reshape(shp)
        o0 += sz
        if n in SMALL_SHARDED:
            full = lax.dynamic_index_in_dim(_blocks_from_full(full, _shard_axis(n)), dev, 0, keepdims=False)
        gsh[n] = full.reshape(local[n].shape)

    delta, new_m, new_v = {}, {}, {}
    for n in WEIGHTS:
        delta[n], new_m[n], new_v[n] = _adamw(local[n], gsh[n], mom[n], vel[n], f"adamw_{n}")

    loss = lax.psum(loss_row[0, 0], ("x", "y", "c"))
    return (loss, grad_x[None], *[gsh[n] for n in WEIGHTS], *[delta[n] for n in WEIGHTS],
            *[new_m[n] for n in WEIGHTS], *[new_v[n] for n in WEIGHTS])
```

```python
import jax
import jax.numpy as jnp
from jax import lax
from jax.experimental import pallas as pl
from jax.experimental.pallas import tpu as pltpu

F32, BF16 = jnp.float32, jnp.bfloat16
NDEV = 8
RMS_EPS = 1e-6
LN_EPS = 1e-5
CHUNK = 128
C_GROUPS = 8
XA_HEADS = 4
ADAM_LR, ADAM_B1, ADAM_B2, ADAM_EPS, ADAM_WD, ADAM_STEP = 0.001, 0.9, 0.999, 1e-08, 0.01, 10
HALO = 16
ROW_CHUNK = 32
V7X_VMEM_LIMIT = 56 * 1024 * 1024
MESH = pl.DeviceIdType.MESH

TS_ROW = 512
TS_MM = 1024
TN_MM = 1408
TS_FFN = 512
MM_ROW_CHUNK = 256
TS_CONV = 512
TS_SGU = 512
TS_ATTN = 512


def _cp(*sem):
    return pltpu.CompilerParams(dimension_semantics=sem, vmem_limit_bytes=V7X_VMEM_LIMIT)


def _pick(n, pref, align):
    for t in range(min(n, pref), 0, -1):
        if n % t == 0 and (t % align == 0 or t == n):
            return t
    return n


def _sigmoid(x):
    return 0.5 * jnp.tanh(0.5 * x) + 0.5


def _dot(a, b):
    return jnp.dot(a, b, preferred_element_type=F32)


def _dot_nt(a, b):
    return lax.dot_general(a, b, (((1,), (1,)), ((), ())), preferred_element_type=F32)


def _dot_tn(a, b):
    return lax.dot_general(a, b, (((0,), (0,)), ((), ())), preferred_element_type=F32)


_ANY = pl.BlockSpec(memory_space=pl.ANY)
_RESIDENT = pl.Buffered(1)


def _after(after):
    return ([], []) if after is None else ([_ANY], [after])


def _rms_fwd(h, g, name, after=None):
    S, D = h.shape
    ts = _pick(S, TS_MM, 16)
    after_specs, after_ops = _after(after)

    def body(h_ref, g_ref, *rest):
        o_ref = rest[-1]
        x = h_ref[...]
        r = lax.rsqrt(jnp.mean(x * x, axis=-1, keepdims=True) + RMS_EPS)
        o_ref[...] = ((x * r) * g_ref[...]).astype(o_ref.dtype)

    return pl.pallas_call(
        body, grid=(S // ts,),
        in_specs=[pl.BlockSpec((ts, D), lambda i: (i, 0)), pl.BlockSpec((1, D), lambda i: (0, 0))] + after_specs,
        out_specs=pl.BlockSpec((ts, D), lambda i: (i, 0)),
        out_shape=jax.ShapeDtypeStruct((S, D), BF16), compiler_params=_cp("parallel"), name=name)(h, g, *after_ops)


def _rms_bwd(dn, h, g, dres, name):
    S, D = h.shape
    ts = _pick(S, TS_ROW, 8)
    has_res = dres is not None

    def body(*refs):
        if has_res:
            dn_ref, h_ref, g_ref, dres_ref, dh_ref, dg_ref = refs
        else:
            dn_ref, h_ref, g_ref, dh_ref, dg_ref = refs
        x = h_ref[...]
        dn_ = dn_ref[...].astype(F32)
        r = lax.rsqrt(jnp.mean(x * x, axis=-1, keepdims=True) + RMS_EPS)
        xr = x * r

        @pl.when(pl.program_id(0) == 0)
        def _():
            dg_ref[...] = jnp.zeros_like(dg_ref)

        dg_ref[...] += jnp.sum(dn_ * xr, axis=0, keepdims=True)
        u = dn_ * g_ref[...]
        dh = r * u - xr * (r * jnp.mean(u * xr, axis=-1, keepdims=True))
        if has_res:
            dh = dh + dres_ref[...]
        dh_ref[...] = dh

    tile = pl.BlockSpec((ts, D), lambda i: (i, 0))
    vec = pl.BlockSpec((1, D), lambda i: (0, 0))
    ins = [dn, h, g] + ([dres] if has_res else [])
    return pl.pallas_call(
        body, grid=(S // ts,),
        in_specs=[tile, tile, vec] + ([tile] if has_res else []),
        out_specs=[tile, vec],
        out_shape=[jax.ShapeDtypeStruct((S, D), F32), jax.ShapeDtypeStruct((1, D), F32)],
        compiler_params=_cp("arbitrary"), name=name)(*ins)


def _loss_bwd(h, g, target, name):
    S, D = h.shape
    ts = _pick(S, TS_ROW, 8)

    def body(h_ref, g_ref, t_ref, loss_ref, dh_ref, dg_ref):
        x = h_ref[...]
        r = lax.rsqrt(jnp.mean(x * x, axis=-1, keepdims=True) + RMS_EPS)
        xr = x * r
        gg = g_ref[...]
        e = xr * gg - t_ref[...]

        @pl.when(pl.program_id(0) == 0)
        def _():
            dg_ref[...] = jnp.zeros_like(dg_ref)
            loss_ref[...] = jnp.zeros_like(loss_ref)

        tile_loss = jnp.sum(jnp.sum(e * e, axis=0, keepdims=True), axis=1, keepdims=True) * (0.5 / D)
        loss_ref[...] += jnp.broadcast_to(tile_loss, loss_ref.shape)
        dy = e * (1.0 / D)
        dg_ref[...] += jnp.sum(dy * xr, axis=0, keepdims=True)
        u = dy * gg
        dh_ref[...] = r * u - xr * (r * jnp.mean(u * xr, axis=-1, keepdims=True))

    tile = pl.BlockSpec((ts, D), lambda i: (i, 0))
    vec = pl.BlockSpec((1, D), lambda i: (0, 0))
    return pl.pallas_call(
        body, grid=(S // ts,),
        in_specs=[tile, vec, tile],
        out_specs=[pl.BlockSpec((1, 128), lambda i: (0, 0)), tile, vec],
        out_shape=[jax.ShapeDtypeStruct((1, 128), F32), jax.ShapeDtypeStruct((S, D), F32),
                   jax.ShapeDtypeStruct((1, D), F32)],
        compiler_params=_cp("arbitrary"), name=name)(h, g, target)


def _mm(pairs, name, out_dtype=F32, res=None, rms_g=None, rms_bwd=None, tm=None, tn=None, after=None):
    M = pairs[0][0].shape[0]
    N = pairs[0][1].shape[1 if pairs[0][2] == "nn" else 0]
    whole_rows = rms_g is not None or rms_bwd is not None
    tm = _pick(M, tm or TS_MM, 16)
    tn = N if whole_rows else _pick(N, tn or TN_MM, 128)
    npair = len(pairs)
    modes = [p[2] for p in pairs]
    after_specs, after_ops = _after(after)

    rc = MM_ROW_CHUNK if whole_rows and tm % MM_ROW_CHUNK == 0 else tm

    def body(*refs):
        rest = refs[2 * npair + len(after_ops):]
        res_ref = None
        if res is not None:
            res_ref, rest = rest[0], rest[1:]
        if rms_bwd is not None:
            dg_ref = rest[4]

            @pl.when(pl.program_id(0) == 0)
            def _():
                dg_ref[...] = jnp.zeros_like(dg_ref)

        for r0 in range(0, tm, rc):
            rows = pl.ds(r0, rc)
            acc = None
            for p in range(npair):
                a_ = refs[2 * p][rows, :].astype(BF16)
                d = _dot(a_, refs[2 * p + 1][...]) if modes[p] == "nn" else _dot_nt(a_, refs[2 * p + 1][...])
                acc = d if acc is None else acc + d
            if res_ref is not None:
                acc = acc + res_ref[rows, :]
            if rms_bwd is not None:
                h_ref, g_ref, dres_ref, dh_ref, _ = rest
                x = h_ref[rows, :]
                r = lax.rsqrt(jnp.mean(x * x, axis=-1, keepdims=True) + RMS_EPS)
                xr = x * r
                dg_ref[...] += jnp.sum(acc * xr, axis=0, keepdims=True)
                u = acc * g_ref[...]
                dh_ref[rows, :] = r * u - xr * (r * jnp.mean(u * xr, axis=-1, keepdims=True)) + dres_ref[rows, :]
            elif rms_g is not None:
                g_ref, o_ref, n_ref = rest
                o_ref[rows, :] = acc
                r = lax.rsqrt(jnp.mean(acc * acc, axis=-1, keepdims=True) + RMS_EPS)
                n_ref[rows, :] = ((acc * r) * g_ref[...]).astype(BF16)
            else:
                rest[0][rows, :] = acc.astype(rest[0].dtype)

    in_specs, ins = [], []
    for a, w, mode in pairs:
        K = a.shape[1]
        in_specs.append(pl.BlockSpec((tm, K), lambda i, j: (i, 0)))
        once = _RESIDENT if tn == N else None
        in_specs.append(pl.BlockSpec((K, tn), lambda i, j: (0, j), pipeline_mode=once) if mode == "nn"
                        else pl.BlockSpec((tn, K), lambda i, j: (j, 0), pipeline_mode=once))
        ins += [a, w]
    in_specs += after_specs
    ins += after_ops
    tile = pl.BlockSpec((tm, tn), lambda i, j: (i, j))
    vec = pl.BlockSpec((1, tn), lambda i, j: (0, j))
    if res is not None:
        in_specs.append(tile)
        ins.append(res)
    sem = ("parallel", "parallel")
    if rms_bwd is not None:
        in_specs += [tile, vec, tile]
        ins += list(rms_bwd)
        out_specs = [tile, vec]
        out_shape = [jax.ShapeDtypeStruct((M, N), F32), jax.ShapeDtypeStruct((1, N), F32)]
        sem = ("arbitrary", "arbitrary")
    elif rms_g is not None:
        in_specs.append(vec)
        ins.append(rms_g)
        out_specs = [tile, tile]
        out_shape = [jax.ShapeDtypeStruct((M, N), F32), jax.ShapeDtypeStruct((M, N), BF16)]
    else:
        out_specs = tile
        out_shape = jax.ShapeDtypeStruct((M, N), out_dtype)
    return pl.pallas_call(
        body, grid=(M // tm, N // tn), in_specs=in_specs, out_specs=out_specs, out_shape=out_shape,
        compiler_params=_cp(*sem), name=name)(*ins)


def _mm_tn(a, b, name, ts=None, tn=None):
    S, K = a.shape
    N = b.shape[1]
    ts = _pick(S, ts or TS_MM, 16)
    tn = _pick(N, tn or TN_MM, 128)
    nsteps = S // ts

    def body(a_ref, b_ref, o_ref, acc_ref):
        s = pl.program_id(1)

        @pl.when(s == 0)
        def _():
            acc_ref[...] = jnp.zeros_like(acc_ref)

        acc_ref[...] += _dot_tn(a_ref[...].astype(BF16), b_ref[...].astype(BF16))

        @pl.when(s == nsteps - 1)
        def _():
            o_ref[...] = acc_ref[...].astype(o_ref.dtype)

    return pl.pallas_call(
        body, grid=(N // tn, nsteps),
        in_specs=[pl.BlockSpec((ts, K), lambda j, s: (s, 0)), pl.BlockSpec((ts, tn), lambda j, s: (s, j))],
        out_specs=pl.BlockSpec((K, tn), lambda j, s: (0, j)), out_shape=jax.ShapeDtypeStruct((K, N), BF16),
        scratch_shapes=[pltpu.VMEM((K, tn), F32)],
        compiler_params=_cp("parallel", "arbitrary"), name=name)(a, b)


def _col_chunk(n):
    return 256 if n % 256 == 0 else 128


def _ffn_up(n, wgt, wut, name, after=None):
    S, D = n.shape
    F = wgt.shape[0]
    tm = _pick(S, TS_FFN, 16)
    ce = _col_chunk(F)
    after_specs, after_ops = _after(after)

    def body(n_ref, wg_ref, wu_ref, *rest):
        a_ref, b_ref, hid_ref = rest[-3:]
        x = n_ref[...]
        for c0 in range(0, F, ce):
            a = _dot_nt(x, wg_ref[c0:c0 + ce, :])
            b = _dot_nt(x, wu_ref[c0:c0 + ce, :])
            a_ref[:, c0:c0 + ce] = a.astype(BF16)
            b_ref[:, c0:c0 + ce] = b.astype(BF16)
            hid_ref[:, c0:c0 + ce] = (a * _sigmoid(a) * b).astype(BF16)

    wspec = pl.BlockSpec((F, D), lambda i: (0, 0), pipeline_mode=_RESIDENT)
    ospec = pl.BlockSpec((tm, F), lambda i: (i, 0))
    osh = jax.ShapeDtypeStruct((S, F), BF16)
    return pl.pallas_call(
        body, grid=(S // tm,),
        in_specs=[pl.BlockSpec((tm, D), lambda i: (i, 0)), wspec, wspec] + after_specs,
        out_specs=[ospec, ospec, ospec], out_shape=[osh, osh, osh],
        compiler_params=_cp("parallel"), name=name)(n, wgt, wut, *after_ops)


def _ffn_dhid(dh, wd, a, b, name):
    S, D = dh.shape
    F = wd.shape[0]
    tm = _pick(S, TS_FFN, 16)
    ce = _col_chunk(F)

    def body(dh_ref, wd_ref, a_ref, b_ref, da_ref, db_ref):
        x = dh_ref[...].astype(BF16)
        for c0 in range(0, F, ce):
            g = _dot_nt(x, wd_ref[c0:c0 + ce, :])
            a_ = a_ref[:, c0:c0 + ce].astype(F32)
            b_ = b_ref[:, c0:c0 + ce].astype(F32)
            sg = _sigmoid(a_)
            da_ref[:, c0:c0 + ce] = (g * b_ * (sg * (1.0 + a_ * (1.0 - sg)))).astype(BF16)
            db_ref[:, c0:c0 + ce] = (g * (a_ * sg)).astype(BF16)

    tile = pl.BlockSpec((tm, F), lambda i: (i, 0))
    osh = jax.ShapeDtypeStruct((S, F), BF16)
    return pl.pallas_call(
        body, grid=(S // tm,),
        in_specs=[pl.BlockSpec((tm, D), lambda i: (i, 0)),
                  pl.BlockSpec((F, D), lambda i: (0, 0), pipeline_mode=_RESIDENT), tile, tile],
        out_specs=[tile, tile], out_shape=[osh, osh],
        compiler_params=_cp("parallel"), name=name)(dh, wd, a, b)


def _softmax_rows(s):
    m = jnp.max(s, axis=-1, keepdims=True)
    p = jnp.exp(s - m)
    return p / jnp.sum(p, axis=-1, keepdims=True)


def _attn_fwd(q, k, v, name):
    S, D = q.shape
    M = k.shape[0]
    hd = D // XA_HEADS
    scale = hd ** -0.5
    ts = _pick(S, TS_ATTN, 16)

    def body(q_ref, k_ref, v_ref, o_ref):
        for h in range(XA_HEADS):
            sl = slice(h * hd, (h + 1) * hd)
            p = _softmax_rows(_dot_nt(q_ref[:, sl], k_ref[:, sl]) * scale)
            o_ref[:, sl] = _dot(p.astype(BF16), v_ref[:, sl]).astype(BF16)

    tile = pl.BlockSpec((ts, D), lambda i: (i, 0))
    memspec = pl.BlockSpec((M, D), lambda i: (0, 0))
    return pl.pallas_call(
        body, grid=(S // ts,), in_specs=[tile, memspec, memspec], out_specs=tile,
        out_shape=jax.ShapeDtypeStruct((S, D), BF16), compiler_params=_cp("parallel"), name=name)(q, k, v)


def _attn_bwd(q, k, v, do, name):
    S, D = q.shape
    M = k.shape[0]
    hd = D // XA_HEADS
    scale = hd ** -0.5
    ts = _pick(S, TS_ATTN, 16)

    def body(q_ref, k_ref, v_ref, do_ref, dq_ref, dk_ref, dv_ref):
        @pl.when(pl.program_id(0) == 0)
        def _():
            dk_ref[...] = jnp.zeros_like(dk_ref)
            dv_ref[...] = jnp.zeros_like(dv_ref)

        for h in range(XA_HEADS):
            sl = slice(h * hd, (h + 1) * hd)
            qh, kh, vh, doh = q_ref[:, sl], k_ref[:, sl], v_ref[:, sl], do_ref[:, sl]
            p = _softmax_rows(_dot_nt(qh, kh) * scale)
            dp = _dot_nt(doh, vh)
            dv_ref[:, sl] += _dot_tn(p.astype(BF16), doh)
            delta = jnp.sum(dp * p, axis=-1, keepdims=True)
            ds = (p * (dp - delta) * scale).astype(BF16)
            dq_ref[:, sl] = _dot(ds, kh).astype(BF16)
            dk_ref[:, sl] += _dot_tn(ds, qh)

    tile = pl.BlockSpec((ts, D), lambda i: (i, 0))
    memspec = pl.BlockSpec((M, D), lambda i: (0, 0))
    return pl.pallas_call(
        body, grid=(S // ts,), in_specs=[tile, memspec, memspec, tile], out_specs=[tile, memspec, memspec],
        out_shape=[jax.ShapeDtypeStruct((S, D), BF16), jax.ShapeDtypeStruct((M, D), F32),
                   jax.ShapeDtypeStruct((M, D), F32)],
        compiler_params=_cp("arbitrary"), name=name)(q, k, v, do)


def _halo_specs(ts, width, col):
    per = ts // HALO

    def prev(i):
        return (jnp.maximum(i * per - 1, 0), col)

    def nxt(i, n_tiles):
        return (jnp.minimum((i + 1) * per, n_tiles * per - 1), col)

    return prev, nxt


def _fill_ext(ext_ref, prev_val, main_val, next_val, first, last, ts):
    ext_ref[pl.ds(0, HALO), :] = jnp.where(first, 0.0, prev_val)
    ext_ref[pl.ds(HALO, ts), :] = main_val
    ext_ref[pl.ds(HALO + ts, HALO), :] = jnp.where(last, 0.0, next_val)


def _conv_fwd(z, wa, ba, lng, lnb, wb, bb, name, after=None):
    S = z.shape[0]
    C = z.shape[1] // 5
    KA, KB = wa.shape[0], wb.shape[0]
    pa, pb = KA // 2, KB // 2
    assert pa <= HALO and pb <= HALO
    ts = _pick(S, TS_CONV, ROW_CHUNK)
    nt = S // ts
    rc = ROW_CHUNK
    prev, nxt = _halo_specs(ts, 5 * C, 0)
    after_specs, after_ops = _after(after)

    def body(*refs):
        compute(*refs[:9], *refs[9 + len(after_ops):])

    def compute(z_ref, zp_ref, zn_ref, wa_ref, ba_ref, lng_ref, lnb_ref, wb_ref, bb_ref, ab_ref, ca_ref,
                ga_ext, tb_ext, win_a, win_b):
        i = pl.program_id(0)
        first, last = i == 0, i == nt - 1

        def glu(r):
            return r[:, 0:C] * _sigmoid(r[:, C:2 * C])

        def gcb(r):
            return r[:, 4 * C:5 * C] * r[:, 2 * C:3 * C]

        _fill_ext(ga_ext, glu(zp_ref), glu(z_ref), glu(zn_ref), first, last, ts)
        _fill_ext(tb_ext, gcb(zp_ref), gcb(z_ref), gcb(zn_ref), first, last, ts)

        def chunk(c, carry):
            r0 = pl.multiple_of(c * rc, rc)
            win_a[...] = ga_ext[pl.ds(r0, rc + 2 * HALO), :]
            win_b[...] = tb_ext[pl.ds(r0, rc + 2 * HALO), :]
            acc = jnp.zeros((rc, C), F32)
            for k in range(KA):
                acc = acc + wa_ref[k:k + 1, :] * win_a[pl.ds(HALO - pa + k, rc), :]
            ca = acc + ba_ref[...]
            ca_ref[pl.ds(r0, rc), :] = ca
            mu = jnp.mean(ca, axis=-1, keepdims=True)
            xc = ca - mu
            var = jnp.mean(xc * xc, axis=-1, keepdims=True)
            ln = xc * lax.rsqrt(var + LN_EPS) * lng_ref[...] + lnb_ref[...]
            ab_ref[pl.ds(r0, rc), 0:C] = (ln * _sigmoid(ln)).astype(BF16)
            cb = jnp.zeros((rc, C), F32) + bb_ref[...]
            for k in range(KB):
                cb = cb + wb_ref[k:k + 1, :] * win_b[pl.ds(HALO - pb + k, rc), :]
            ab_ref[pl.ds(r0, rc), C:2 * C] = (z_ref[pl.ds(r0, rc), 3 * C:4 * C] * cb).astype(BF16)
            return carry

        lax.fori_loop(0, ts // rc, chunk, 0)

    zspec = pl.BlockSpec((ts, 5 * C), lambda i: (i, 0))
    zprev = pl.BlockSpec((HALO, 5 * C), prev)
    znext = pl.BlockSpec((HALO, 5 * C), lambda i: nxt(i, nt))

    def full(a):
        return pl.BlockSpec(a.shape, lambda i: (0, 0))

    return pl.pallas_call(
        body, grid=(nt,),
        in_specs=[zspec, zprev, znext, full(wa), full(ba), full(lng), full(lnb), full(wb), full(bb)] + after_specs,
        out_specs=[pl.BlockSpec((ts, 2 * C), lambda i: (i, 0)), pl.BlockSpec((ts, C), lambda i: (i, 0))],
        out_shape=[jax.ShapeDtypeStruct((S, 2 * C), BF16), jax.ShapeDtypeStruct((S, C), F32)],
        scratch_shapes=[pltpu.VMEM((ts + 2 * HALO, C), F32), pltpu.VMEM((ts + 2 * HALO, C), F32),
                        pltpu.VMEM((rc + 2 * HALO, C), F32), pltpu.VMEM((rc + 2 * HALO, C), F32)],
        compiler_params=_cp("parallel"), name=name)(z, z, z, wa, ba, lng, lnb, wb, bb, *after_ops)


def _conv_bwd_ln(dab, ca, lng, lnb, name):
    S, C = ca.shape
    ts = _pick(S, TS_ROW, 8)

    def body(da_ref, ca_ref, lng_ref, lnb_ref, dca_ref, dg_ref, db_ref, dbias_ref):
        @pl.when(pl.program_id(0) == 0)
        def _():
            dg_ref[...] = jnp.zeros_like(dg_ref)
            db_ref[...] = jnp.zeros_like(db_ref)
            dbias_ref[...] = jnp.zeros_like(dbias_ref)

        ca_ = ca_ref[...]
        mu = jnp.mean(ca_, axis=-1, keepdims=True)
        xc = ca_ - mu
        rstd = lax.rsqrt(jnp.mean(xc * xc, axis=-1, keepdims=True) + LN_EPS)
        xh = xc * rstd
        ln = xh * lng_ref[...] + lnb_ref[...]
        sg = _sigmoid(ln)
        dln = da_ref[...].astype(F32) * (sg * (1.0 + ln * (1.0 - sg)))
        dg_ref[...] += jnp.sum(dln * xh, axis=0, keepdims=True)
        db_ref[...] += jnp.sum(dln, axis=0, keepdims=True)
        dxh = dln * lng_ref[...]
        dca = rstd * (dxh - jnp.mean(dxh, axis=-1, keepdims=True) - xh * jnp.mean(dxh * xh, axis=-1, keepdims=True))
        dca_ref[...] = dca
        dbias_ref[...] += jnp.sum(dca, axis=0, keepdims=True)

    tile = pl.BlockSpec((ts, C), lambda i: (i, 0))
    vec = pl.BlockSpec((1, C), lambda i: (0, 0))
    vsh = jax.ShapeDtypeStruct((1, C), F32)
    return pl.pallas_call(
        body, grid=(S // ts,), in_specs=[tile, tile, vec, vec], out_specs=[tile, vec, vec, vec],
        out_shape=[jax.ShapeDtypeStruct((S, C), F32), vsh, vsh, vsh],
        compiler_params=_cp("arbitrary"), name=name)(dab, ca, lng, lnb)


def _conv_bwd(z, dca, dab, wa, wb, bb, name):
    S = z.shape[0]
    C = z.shape[1] // 5
    KA, KB = wa.shape[0], wb.shape[0]
    pa, pb = KA // 2, KB // 2
    ts = _pick(S, TS_CONV, ROW_CHUNK)
    nt = S // ts
    rc = ROW_CHUNK
    prev0, nxt0 = _halo_specs(ts, C, 0)
    prev1, nxt1 = _halo_specs(ts, C, 1)

    def body(z_ref, zp_ref, zn_ref, dca_ref, dcap_ref, dcan_ref, db_ref, dbp_ref, dbn_ref, wa_ref, wb_ref, bb_ref,
             dz_ref, dwa_ref, dwb_ref, dbb_ref,
             ga_ext, dca_ext, tb_ext, dcb_ext, win_ga, win_dca, win_tb, win_dcb, acc_a, acc_b, acc_bias):
        i = pl.program_id(0)
        first, last = i == 0, i == nt - 1

        @pl.when(first)
        def _():
            acc_a[...] = jnp.zeros_like(acc_a)
            acc_b[...] = jnp.zeros_like(acc_b)
            acc_bias[...] = jnp.zeros_like(acc_bias)

        def glu(r):
            return r[:, 0:C] * _sigmoid(r[:, C:2 * C])

        def gcb(r):
            return r[:, 4 * C:5 * C] * r[:, 2 * C:3 * C]

        def dcb(d, r):
            return d[...].astype(F32) * r[:, 3 * C:4 * C]

        _fill_ext(ga_ext, glu(zp_ref), glu(z_ref), glu(zn_ref), first, last, ts)
        _fill_ext(tb_ext, gcb(zp_ref), gcb(z_ref), gcb(zn_ref), first, last, ts)
        _fill_ext(dca_ext, dcap_ref[...], dca_ref[...], dcan_ref[...], first, last, ts)
        _fill_ext(dcb_ext, dcb(dbp_ref, zp_ref), dcb(db_ref, z_ref), dcb(dbn_ref, zn_ref), first, last, ts)

        def fold(x):
            return jnp.sum(x.reshape(rc // 8, 8, C), axis=0)

        def chunk(c, carry):
            r0 = pl.multiple_of(c * rc, rc)
            win_ga[...] = ga_ext[pl.ds(r0, rc + 2 * HALO), :]
            win_dca[...] = dca_ext[pl.ds(r0, rc + 2 * HALO), :]
            win_tb[...] = tb_ext[pl.ds(r0, rc + 2 * HALO), :]
            win_dcb[...] = dcb_ext[pl.ds(r0, rc + 2 * HALO), :]
            dca_c = win_dca[pl.ds(HALO, rc), :]
            dglu = jnp.zeros((rc, C), F32)
            for k in range(KA):
                dglu = dglu + wa_ref[k:k + 1, :] * win_dca[pl.ds(HALO + pa - k, rc), :]
                acc_a[k] += fold(dca_c * win_ga[pl.ds(HALO - pa + k, rc), :])
            val = z_ref[pl.ds(r0, rc), 0:C]
            sg = _sigmoid(z_ref[pl.ds(r0, rc), C:2 * C])
            dz_ref[pl.ds(r0, rc), 0:C] = (dglu * sg).astype(BF16)
            dz_ref[pl.ds(r0, rc), C:2 * C] = (dglu * val * sg * (1.0 - sg)).astype(BF16)
            dcb_c = win_dcb[pl.ds(HALO, rc), :]
            cb = jnp.zeros((rc, C), F32) + bb_ref[...]
            dt = jnp.zeros((rc, C), F32)
            for k in range(KB):
                tb_k = win_tb[pl.ds(HALO - pb + k, rc), :]
                cb = cb + wb_ref[k:k + 1, :] * tb_k
                dt = dt + wb_ref[k:k + 1, :] * win_dcb[pl.ds(HALO + pb - k, rc), :]
                acc_b[k] += fold(dcb_c * tb_k)
            acc_bias[...] += fold(dcb_c)
            db_c = db_ref[pl.ds(r0, rc), :].astype(F32)
            dz_ref[pl.ds(r0, rc), 2 * C:3 * C] = (dt * z_ref[pl.ds(r0, rc), 4 * C:5 * C]).astype(BF16)
            dz_ref[pl.ds(r0, rc), 3 * C:4 * C] = (db_c * cb).astype(BF16)
            dz_ref[pl.ds(r0, rc), 4 * C:5 * C] = (dt * z_ref[pl.ds(r0, rc), 2 * C:3 * C]).astype(BF16)
            return carry

        lax.fori_loop(0, ts // rc, chunk, 0)

        @pl.when(last)
        def _():
            dwa_ref[...] = jnp.sum(acc_a[...], axis=1)
            dwb_ref[...] = jnp.sum(acc_b[...], axis=1)
            dbb_ref[...] = jnp.sum(acc_bias[...], axis=0, keepdims=True)

    zspec = pl.BlockSpec((ts, 5 * C), lambda i: (i, 0))
    zprev = pl.BlockSpec((HALO, 5 * C), prev0)
    znext = pl.BlockSpec((HALO, 5 * C), lambda i: nxt0(i, nt))
    dspec = pl.BlockSpec((ts, C), lambda i: (i, 0))
    dprev = pl.BlockSpec((HALO, C), prev0)
    dnext = pl.BlockSpec((HALO, C), lambda i: nxt0(i, nt))
    bspec = pl.BlockSpec((ts, C), lambda i: (i, 1))
    bprev = pl.BlockSpec((HALO, C), prev1)
    bnext = pl.BlockSpec((HALO, C), lambda i: nxt1(i, nt))

    def full(shape):
        return pl.BlockSpec(shape, lambda i: (0,) * len(shape))

    ext = pltpu.VMEM((ts + 2 * HALO, C), F32)
    win = pltpu.VMEM((rc + 2 * HALO, C), F32)
    return pl.pallas_call(
        body, grid=(nt,),
        in_specs=[zspec, zprev, znext, dspec, dprev, dnext, bspec, bprev, bnext,
                  full(wa.shape), full(wb.shape), full(bb.shape)],
        out_specs=[pl.BlockSpec((ts, 5 * C), lambda i: (i, 0)), full((KA, C)), full((KB, C)), full((1, C))],
        out_shape=[jax.ShapeDtypeStruct((S, 5 * C), BF16), jax.ShapeDtypeStruct((KA, C), F32),
                   jax.ShapeDtypeStruct((KB, C), F32), jax.ShapeDtypeStruct((1, C), F32)],
        scratch_shapes=[ext, ext, ext, ext, win, win, win, win,
                        pltpu.VMEM((KA, 8, C), F32), pltpu.VMEM((KB, 8, C), F32), pltpu.VMEM((8, C), F32)],
        compiler_params=_cp("arbitrary"), name=name)(z, z, z, dca, dca, dca, dab, dab, dab, wa, wb, bb)


_GELU_C = 0.7978845608028654
_GELU_A = 0.044715


def _gelu(x):
    return 0.5 * x * (1.0 + jnp.tanh(_GELU_C * (x + _GELU_A * (x * x * x))))


def _gelu_grad(x):
    t = jnp.tanh(_GELU_C * (x + _GELU_A * (x * x * x)))
    return 0.5 * (1.0 + t) + 0.5 * x * (1.0 - t * t) * (_GELU_C * (1.0 + 3.0 * _GELU_A * x * x))


def _sgu_fwd(zp, lng, lnb, ws, bsb, name):
    S = zp.shape[0]
    D = zp.shape[1] // 2
    G = ws.shape[0]
    gd = D // G
    ts = _pick(S, TS_SGU, CHUNK)
    ncs = ts // CHUNK

    def body(zp_ref, lng_ref, lnb_ref, ws_ref, bsb_ref, y_ref, vb_ref):
        v = _gelu(zp_ref[:, D:2 * D])
        mu = jnp.mean(v, axis=-1, keepdims=True)
        xc = v - mu
        rstd = lax.rsqrt(jnp.mean(xc * xc, axis=-1, keepdims=True) + LN_EPS)
        vb_ref[...] = (xc * rstd * lng_ref[...] + lnb_ref[...]).astype(BF16)
        for c in range(ncs):
            rows = slice(c * CHUNK, (c + 1) * CHUNK)
            for g in range(G):
                cols = slice(g * gd, (g + 1) * gd)
                sv = _dot(ws_ref[g], vb_ref[rows, cols]) + bsb_ref[:, cols]
                y_ref[rows, cols] = (_gelu(zp_ref[rows, cols]) * sv).astype(BF16)

    def full(a):
        return pl.BlockSpec(a.shape, lambda i: (0,) * a.ndim)

    return pl.pallas_call(
        body, grid=(S // ts,),
        in_specs=[pl.BlockSpec((ts, 2 * D), lambda i: (i, 0)), full(lng), full(lnb), full(ws), full(bsb)],
        out_specs=pl.BlockSpec((ts, D), lambda i: (i, 0)), out_shape=jax.ShapeDtypeStruct((S, D), BF16),
        scratch_shapes=[pltpu.VMEM((ts, D), BF16)],
        compiler_params=_cp("parallel"), name=name)(zp, lng, lnb, ws, bsb)


def _sgu_bwd(dy, zp, lng, lnb, ws, wst, bsb, name):
    S = zp.shape[0]
    D = zp.shape[1] // 2
    G = ws.shape[0]
    gd = D // G
    ts = _pick(S, TS_SGU, CHUNK)
    ncs = ts // CHUNK

    def body(dy_ref, zp_ref, lng_ref, lnb_ref, ws_ref, wst_ref, bsb_ref,
             dzp_ref, dws_ref, dbs_ref, dg_ref, db_ref, vb_ref, dvln_ref, acc_bs):
        i = pl.program_id(0)

        @pl.when(i == 0)
        def _():
            dws_ref[...] = jnp.zeros_like(dws_ref)
            acc_bs[...] = jnp.zeros_like(acc_bs)
            dg_ref[...] = jnp.zeros_like(dg_ref)
            db_ref[...] = jnp.zeros_like(db_ref)

        zv = zp_ref[:, D:2 * D]
        v = _gelu(zv)
        mu = jnp.mean(v, axis=-1, keepdims=True)
        xc = v - mu
        rstd = lax.rsqrt(jnp.mean(xc * xc, axis=-1, keepdims=True) + LN_EPS)
        xh = xc * rstd
        vb_ref[...] = (xh * lng_ref[...] + lnb_ref[...]).astype(BF16)
        for c in range(ncs):
            rows = slice(c * CHUNK, (c + 1) * CHUNK)
            for g in range(G):
                cols = slice(g * gd, (g + 1) * gd)
                zu = zp_ref[rows, cols]
                u = _gelu(zu)
                dy_ = dy_ref[rows, cols].astype(F32)
                sv = _dot(ws_ref[g], vb_ref[rows, cols]) + bsb_ref[:, cols]
                dzp_ref[rows, cols] = (dy_ * sv * _gelu_grad(zu)).astype(BF16)
                dsv = dy_ * u
                acc_bs[:, cols] += dsv
                dsvb = dsv.astype(BF16)
                dws_ref[g] += _dot_nt(dsvb, vb_ref[rows, cols])
                dvln_ref[rows, cols] = _dot(wst_ref[g], dsvb)
        dvln = dvln_ref[...]
        dg_ref[...] += jnp.sum(dvln * xh, axis=0, keepdims=True)
        db_ref[...] += jnp.sum(dvln, axis=0, keepdims=True)
        dxh = dvln * lng_ref[...]
        dv = rstd * (dxh - jnp.mean(dxh, axis=-1, keepdims=True) - xh * jnp.mean(dxh * xh, axis=-1, keepdims=True))
        dzp_ref[:, D:2 * D] = (dv * _gelu_grad(zv)).astype(BF16)

        @pl.when(i == pl.num_programs(0) - 1)
        def _():
            dbs_ref[...] = acc_bs[...]

    def full(shape):
        return pl.BlockSpec(shape, lambda i: (0,) * len(shape))

    return pl.pallas_call(
        body, grid=(S // ts,),
        in_specs=[pl.BlockSpec((ts, D), lambda i: (i, 0)), pl.BlockSpec((ts, 2 * D), lambda i: (i, 0)),
                  full(lng.shape), full(lnb.shape), full(ws.shape), full(wst.shape), full(bsb.shape)],
        out_specs=[pl.BlockSpec((ts, 2 * D), lambda i: (i, 0)), full(ws.shape), full(bsb.shape),
                   full((1, D)), full((1, D))],
        out_shape=[jax.ShapeDtypeStruct((S, 2 * D), BF16), jax.ShapeDtypeStruct(ws.shape, F32),
                   jax.ShapeDtypeStruct(bsb.shape, F32), jax.ShapeDtypeStruct((1, D), F32),
                   jax.ShapeDtypeStruct((1, D), F32)],
        scratch_shapes=[pltpu.VMEM((ts, D), BF16), pltpu.VMEM((ts, D), F32),
                        pltpu.VMEM(bsb.shape, F32)],
        compiler_params=_cp("arbitrary"), name=name)(dy, zp, lng, lnb, ws, wst, bsb)


def _group_sum(x, groups, name):
    P, D = x.shape
    gd = D // groups

    def body(x_ref, o_ref):
        for g in range(groups):
            o_ref[:, g:g + 1] = jnp.sum(x_ref[:, g * gd:(g + 1) * gd], axis=1, keepdims=True)

    return pl.pallas_call(body, out_shape=jax.ShapeDtypeStruct((P, groups), F32), name=name)(x)


def _adamw(w, g, m, v, name):
    shape = w.shape
    C = shape[-1]
    R = w.size // C
    tr = _pick(R, 1024, 8)
    bc1 = 1.0 - ADAM_B1 ** ADAM_STEP
    bc2 = 1.0 - ADAM_B2 ** ADAM_STEP

    def body(w_ref, g_ref, m_ref, v_ref, d_ref, nm_ref, nv_ref):
        g_ = g_ref[...]
        nm = ADAM_B1 * m_ref[...] + (1.0 - ADAM_B1) * g_
        nv = ADAM_B2 * v_ref[...] + (1.0 - ADAM_B2) * (g_ * g_)
        nm_ref[...] = nm
        nv_ref[...] = nv
        d_ref[...] = -ADAM_LR * ((nm / bc1) / (jnp.sqrt(nv / bc2) + ADAM_EPS) + ADAM_WD * w_ref[...])

    tile = pl.BlockSpec((tr, C), lambda i: (i, 0))
    sh = jax.ShapeDtypeStruct((R, C), F32)
    outs = pl.pallas_call(
        body, grid=(R // tr,), in_specs=[tile] * 4, out_specs=[tile] * 3, out_shape=[sh] * 3,
        compiler_params=_cp("parallel"), name=name)(*(a.reshape(R, C) for a in (w, g, m, v)))
    return tuple(o.reshape(shape) for o in outs)


_HBM = pl.BlockSpec(memory_space=pltpu.HBM)


def _remote(src, dst, send_sem, recv_sem, to):
    return pltpu.make_async_remote_copy(src_ref=src, dst_ref=dst, send_sem=send_sem, recv_sem=recv_sem,
                                        device_id=to, device_id_type=MESH)


def _all_gather(block, name):
    R, C = block.shape

    def body(x_ref, out_ref, send_sems, recv_sems, local_sem):
        x, y, c = lax.axis_index("x"), lax.axis_index("y"), lax.axis_index("c")
        me, sibling = (x, y, c), (x, y, 1 - c)
        chips = [(1 - x, y), (x, 1 - y), (1 - x, 1 - y)]

        def slot(px, py, pc):
            return out_ref.at[4 * px + 2 * py + pc]

        def copy(k, blk, to, src=None):
            return _remote(slot(*blk) if src is None else src, slot(*blk), send_sems.at[k], recv_sems.at[k], to)

        mine = pltpu.make_async_copy(x_ref, slot(*me), local_sem)
        mine.start()
        first = [copy(0, me, sibling, src=x_ref)]
        first += [copy(1 + j, me, (*chip, c), src=x_ref) for j, chip in enumerate(chips)]
        for cp in first:
            cp.start()
        passed = [copy(4 + j, (*chip, c), sibling) for j, chip in enumerate(chips)]
        for j, chip in enumerate(chips):
            copy(1 + j, (*chip, c), me).wait_recv()
            passed[j].start()
        copy(0, sibling, me).wait_recv()
        for j, chip in enumerate(chips):
            copy(4 + j, (*chip, 1 - c), me).wait_recv()
        for cp in first + passed:
            cp.wait_send()
        mine.wait()

    return pl.pallas_call(
        body, out_shape=jax.ShapeDtypeStruct((NDEV, R, C), block.dtype), in_specs=[_HBM], out_specs=_HBM,
        scratch_shapes=[pltpu.SemaphoreType.DMA((7,)), pltpu.SemaphoreType.DMA((7,)), pltpu.SemaphoreType.DMA],
        name=name)(block)


def _all_gather_weights(pack, rows, name, after=None):
    C = pack.shape[1]
    nw = len(rows)
    starts = [sum(rows[:w]) for w in range(nw)]
    after_specs, after_ops = _after(after)

    def body(pack_ref, *rest):
        rest = rest[len(after_ops):]
        outs = rest[:nw]
        send_sems, recv_sems, local_sem = rest[nw:]
        x, y, c = lax.axis_index("x"), lax.axis_index("y"), lax.axis_index("c")
        me, sibling = (x, y, c), (x, y, 1 - c)
        chips = [(1 - x, y), (x, 1 - y), (1 - x, 1 - y)]

        def block(w, px, py, pc):
            return outs[w].at[pl.ds((4 * px + 2 * py + pc) * rows[w], rows[w])]

        def mine(w):
            return pack_ref.at[pl.ds(starts[w], rows[w])]

        def all_of(k):
            return _remote(pack_ref, pack_ref, send_sems.at[k], recv_sems.at[k], me)

        for w in range(nw):
            pltpu.make_async_copy(mine(w), block(w, *me), local_sem).start()
        for k, to in enumerate([sibling] + [(*chip, c) for chip in chips]):
            for w in range(nw):
                _remote(mine(w), block(w, *me), send_sems.at[k], recv_sems.at[k], to).start()
        for j, chip in enumerate(chips):
            all_of(1 + j).wait_recv()
            for w in range(nw):
                _remote(block(w, *chip, c), block(w, *chip, c), send_sems.at[4 + j], recv_sems.at[4 + j], sibling).start()
        all_of(0).wait_recv()
        for j in range(3):
            all_of(4 + j).wait_recv()
        for k in range(7):
            all_of(k).wait_send()
        pltpu.make_async_copy(pack_ref, pack_ref, local_sem).wait()

    return pl.pallas_call(
        body, out_shape=[jax.ShapeDtypeStruct((NDEV * r, C), pack.dtype) for r in rows],
        in_specs=[_HBM] + after_specs, out_specs=[_HBM] * nw,
        scratch_shapes=[pltpu.SemaphoreType.DMA((7,)), pltpu.SemaphoreType.DMA((7,)), pltpu.SemaphoreType.DMA],
        name=name)(pack, *after_ops)


_SEM = pl.BlockSpec(memory_space=pltpu.SEMAPHORE)
_DATAFLOW = pltpu.SideEffectType.DATAFLOW_SIDE_EFFECTING


def _split_start(srcs, lands, plan, n, after, name):
    nbuf = len(srcs) + len(lands)
    after_specs, after_ops = _after(after)

    def body(*refs):
        src_refs, land_refs = refs[:len(srcs)], refs[len(srcs):nbuf]
        send_sems, recv_sems = refs[nbuf + len(after_ops)], refs[nbuf + len(after_ops) + 1]
        for k, (src, dst, to) in enumerate(plan(src_refs, land_refs)):
            _remote(src, dst, send_sems.at[k], recv_sems.at[k], to).start()
        refs[-1][...] = jnp.zeros_like(refs[-1])

    bufs = [pltpu.with_memory_space_constraint(a, pltpu.HBM) for a in list(srcs) + list(lands)]
    outs = pl.pallas_call(
        body, name=name,
        out_shape=(pltpu.SemaphoreType.DMA((n,)), pltpu.SemaphoreType.DMA((n,)),
                   *[pltpu.HBM(a.shape, a.dtype) for a in bufs], jax.ShapeDtypeStruct((8, 128), F32)),
        in_specs=[_HBM] * nbuf + after_specs,
        out_specs=(_SEM, _SEM, *[_HBM] * nbuf, pl.BlockSpec(memory_space=pltpu.VMEM)),
        input_output_aliases={i: 2 + i for i in range(nbuf)},
        compiler_params=pltpu.CompilerParams(has_side_effects=_DATAFLOW))(*bufs, *after_ops)
    return outs[0], outs[1], list(outs[2:2 + len(srcs)]), list(outs[2 + len(srcs):2 + nbuf]), outs[-1]


def _split_wait(send_sems, recv_sems, srcs, lands, plan, after, name):
    nbuf = len(srcs) + len(lands)
    after_specs, after_ops = _after(after)

    def body(*refs):
        src_refs, land_refs = refs[:len(srcs)], refs[len(srcs):nbuf]
        send_sems_ref, recv_sems_ref = refs[nbuf], refs[nbuf + 1]
        for k, (src, dst, to) in enumerate(plan(src_refs, land_refs)):
            copy = _remote(src, dst, send_sems_ref.at[k], recv_sems_ref.at[k], to)
            copy.wait_send()
            copy.wait_recv()

    outs = pl.pallas_call(
        body, name=name, out_shape=tuple(pltpu.HBM(a.shape, a.dtype) for a in list(srcs) + list(lands)),
        in_specs=[_HBM] * nbuf + [_SEM, _SEM] + after_specs, out_specs=tuple([_HBM] * nbuf),
        input_output_aliases={i: i for i in range(nbuf)},
        compiler_params=pltpu.CompilerParams(has_side_effects=_DATAFLOW))(*srcs, *lands, send_sems, recv_sems, *after_ops)
    return list(outs[:len(srcs)]), list(outs[len(srcs):])


def _peers(x, y, c):
    return [(mask, (1 - x if mask & 4 else x, 1 - y if mask & 2 else y, 1 - c if mask & 1 else c))
            for mask in range(1, NDEV)]


def _gather_plan(rows):
    starts = [sum(rows[:w]) for w in range(len(rows))]

    def plan(src_refs, land_refs):
        x, y, c = lax.axis_index("x"), lax.axis_index("y"), lax.axis_index("c")
        copies = []
        for w, r in enumerate(rows):
            mine = src_refs[0].at[pl.ds(starts[w], r)]
            dst = land_refs[w].at[pl.ds((4 * x + 2 * y + c) * r, r)]
            copies += [(mine, dst, peer) for _, peer in _peers(x, y, c)]
        return copies

    return plan, (NDEV - 1) * len(rows)


def _place_own(shards, fulls, dev_idx, name):
    nw = len(shards)

    def body(i_ref, *refs):
        for w in range(nw):
            refs[2 * nw + w][...] = refs[w][...]

    grid_spec = pltpu.PrefetchScalarGridSpec(
        num_scalar_prefetch=1, grid=(1,),
        in_specs=[pl.BlockSpec(s.shape, lambda t, i_ref: (0, 0)) for s in shards] + [_ANY] * nw,
        out_specs=[pl.BlockSpec(s.shape, lambda t, i_ref: (i_ref[0], 0)) for s in shards])
    outs = pl.pallas_call(
        body, grid_spec=grid_spec, out_shape=[jax.ShapeDtypeStruct(f.shape, f.dtype) for f in fulls],
        input_output_aliases={1 + nw + w: w for w in range(nw)}, name=name)(dev_idx, *shards, *fulls)
    return list(outs)


def _scatter_plan(rows):
    def plan(src_refs, land_refs):
        x, y, c = lax.axis_index("x"), lax.axis_index("y"), lax.axis_index("c")
        copies = []
        for w, r in enumerate(rows):
            for mask, (px, py, pc) in _peers(x, y, c):
                src = src_refs[w].at[pl.ds((4 * px + 2 * py + pc) * r, r)]
                copies.append((src, land_refs[w].at[mask - 1], (px, py, pc)))
        return copies

    return plan, (NDEV - 1) * len(rows)


def _scatter_sum(g, got, dev_idx, name):
    n1, r, C = got.shape
    tr = _pick(r, 512, 16)
    per = r // tr

    def body(i_ref, g_ref, got_ref, o_ref):
        acc = g_ref[...].astype(F32)
        for k in range(n1):
            acc = acc + got_ref[k].astype(F32)
        o_ref[...] = acc

    grid_spec = pltpu.PrefetchScalarGridSpec(
        num_scalar_prefetch=1, grid=(per,),
        in_specs=[pl.BlockSpec((tr, C), lambda t, i_ref: (i_ref[0] * per + t, 0)),
                  pl.BlockSpec((n1, tr, C), lambda t, i_ref: (0, t, 0))],
        out_specs=pl.BlockSpec((tr, C), lambda t, i_ref: (t, 0)))
    return pl.pallas_call(
        body, grid_spec=grid_spec, out_shape=jax.ShapeDtypeStruct((r, C), F32),
        compiler_params=_cp("parallel"), name=name)(dev_idx, g, got)


def _sum_slots(a, name):
    n, R, C = a.shape

    def body(a_ref, o_ref):
        acc = a_ref[0]
        for k in range(1, n):
            acc = acc + a_ref[k]
        o_ref[...] = acc

    return pl.pallas_call(body, out_shape=jax.ShapeDtypeStruct((R, C), F32), name=name)(a)


def _shard_axis(name):
    return {"ev_w_in": 2, "ev_a_conv_w": 2, "ev_b_conv_w": 2, "ev_w_out": 1, "od_w_in": 2, "od_c_ln_g": 1,
            "od_c_ln_b": 1, "od_w_out": 1, "xa_w_q": 1, "xa_w_k": 1, "xa_w_v": 1, "xa_w_o": 1,
            "ffn_w_gate": 2, "ffn_w_up": 2, "ffn_w_down": 1}[name]


BIG = ["ev_w_in", "ev_w_out", "od_w_in", "od_w_out", "xa_w_q", "xa_w_k", "xa_w_v", "xa_w_o",
       "ffn_w_gate", "ffn_w_up", "ffn_w_down"]
SMALL_SHARDED = ["ev_a_conv_w", "ev_b_conv_w", "od_c_ln_g", "od_c_ln_b"]
REPLICATED = ["g_mix", "g_xattn", "g_mem", "g_ffn", "g_final", "ev_a_conv_b", "ev_a_ln_g", "ev_a_ln_b",
              "ev_b_conv_b", "od_w_s", "od_b_s"]
WEIGHTS = ["g_mix", "g_xattn", "g_mem", "g_ffn", "g_final", "ev_w_in", "ev_a_conv_w", "ev_a_conv_b", "ev_a_ln_g",
           "ev_a_ln_b", "ev_b_conv_w", "ev_b_conv_b", "ev_w_out", "od_w_in", "od_c_ln_g", "od_c_ln_b", "od_w_s",
           "od_b_s", "od_w_out", "xa_w_q", "xa_w_k", "xa_w_v", "xa_w_o", "ffn_w_gate", "ffn_w_up", "ffn_w_down"]


def _full_from_blocks(blocks, axis):
    shard = blocks.shape[1:]
    full = jnp.moveaxis(blocks, 0, axis)
    return full.reshape(shard[:axis] + (NDEV * shard[axis],) + shard[axis + 1:])


def _blocks_from_full(full, axis):
    shp = full.shape
    split = full.reshape(shp[:axis] + (NDEV, shp[axis] // NDEV) + shp[axis + 1:])
    return jnp.moveaxis(split, axis, 0)


def _pad_rows(flat, width, row_align):
    per = width * row_align
    n = -(-flat.shape[0] // per) * per
    return jnp.pad(flat, (0, n - flat.shape[0])).reshape(n // width, width)


def _row(v):
    return v.reshape(1, -1)


def _xattn_fwd(h, nq, mem, g_m, wq, wk, wv, wo, g_next, tag, after):
    mem_n = _rms_fwd(mem, _row(g_m), f"xa_mem_rms_{tag}")
    q = _mm([(nq, wq, "nn")], f"xa_q_{tag}", out_dtype=BF16, after=after)
    k = _mm([(mem_n, wk, "nn")], f"xa_k_{tag}", out_dtype=BF16)
    v = _mm([(mem_n, wv, "nn")], f"xa_v_{tag}", out_dtype=BF16)
    o = _attn_fwd(q, k, v, f"xa_attn_{tag}")
    h_new, n_next = _mm([(o, wo, "nn")], f"xa_o_{tag}", res=h, rms_g=_row(g_next))
    return h_new, n_next, (h, nq, mem_n, q, k, v, o)


def _xattn_bwd(dh_new, saved, mem, g_x, g_m, wq, wk, wv, wo, tag, push):
    h, nq, mem_n, q, k, v, o = saved
    do = _mm([(dh_new, wo, "nt")], f"xa_do_{tag}", out_dtype=BF16)
    d_wo = _mm_tn(o, dh_new, f"xa_dwo_{tag}")
    dq, dk, dv = _attn_bwd(q, k, v, do, f"xa_attn_bwd_{tag}")
    d_wq = _mm_tn(nq, dq, f"xa_dwq_{tag}")
    d_wk = _mm_tn(mem_n, dk, f"xa_dwk_{tag}")
    d_wv = _mm_tn(mem_n, dv, f"xa_dwv_{tag}")
    token = push([d_wq, d_wk, d_wv, d_wo])
    dmem_n = _mm([(dk, wk, "nt"), (dv, wv, "nt")], f"xa_dmem_{tag}", after=token)
    _, d_gm = _rms_bwd(dmem_n, mem, _row(g_m), None, f"xa_mem_rms_bwd_{tag}")
    dh, d_gx = _mm([(dq, wq, "nt")], f"xa_dnq_{tag}", rms_bwd=(h, _row(g_x), dh_new), tm=512, after=token)
    return dh, dict(g_xattn=d_gx, g_mem=d_gm)


def _ffn_fwd(h, n, wgt, wut, wd, g_next, tag, after):
    a, b, hid = _ffn_up(n, wgt, wut, f"ffn_up_{tag}", after=after)
    if g_next is None:
        h_new, n_next = _mm([(hid, wd, "nn")], f"ffn_down_{tag}", res=h, tm=512, tn=1024), None
    else:
        h_new, n_next = _mm([(hid, wd, "nn")], f"ffn_down_{tag}", res=h, rms_g=_row(g_next), tm=512)
    return h_new, n_next, (h, n, a, b, hid)


def _ffn_bwd(dh_new, saved, g_f, wgt, wut, wd, tag, push):
    h, n, a, b, hid = saved
    da, db = _ffn_dhid(dh_new, wd, a, b, f"ffn_dhid_{tag}")
    d_wd = _mm_tn(hid, dh_new, f"ffn_dwd_{tag}", tn=512)
    d_wgt = _mm_tn(da, n, f"ffn_dwg_{tag}", tn=512)
    d_wut = _mm_tn(db, n, f"ffn_dwu_{tag}", tn=512)
    token = push([d_wgt, d_wut, d_wd])
    dh, d_gf = _mm([(da, wgt, "nn"), (db, wut, "nn")], f"ffn_dn_{tag}", rms_bwd=(h, _row(g_f), dh_new), tm=512,
                   after=token)
    return dh, dict(g_ffn=d_gf)


_XA = ["xa_w_q", "xa_w_k", "xa_w_v", "xa_w_o"]
_FFN = ["ffn_w_gate", "ffn_w_up", "ffn_w_down"]
GATHERS = {
    "ev_in": [("ev_w_in", 0)],
    "xa0": [("ev_w_out", 0)] + [(n, 0) for n in _XA],
    "ffn0": [(n, 0) for n in _FFN],
    "od": [("od_w_in", 0), ("od_w_out", 0)],
    "xa1": [(n, 1) for n in _XA],
    "ffn1": [(n, 1) for n in _FFN],
}
SCATTERS = {
    "ffn1": [(n, 1) for n in _FFN],
    "xa1": [(n, 1) for n in _XA],
    "od": [("od_w_in", 0), ("od_w_out", 0)],
    "ffn0": [(n, 0) for n in _FFN],
    "xa0": [(n, 0) for n in _XA],
    "ev": [("ev_w_in", 0), ("ev_w_out", 0)],
}


def _local_step(x, mem, loss_target, W, comm):
    grads = {}

    h0 = x
    (ev_w_in_t,), token = comm.weights("ev_in", None)
    n0 = _rms_fwd(h0, _row(W["g_mix"][0]), "ev_rms", after=token)
    z = _mm([(n0, ev_w_in_t, "nt")], "ev_in", tn=1280)
    token = comm.prefetch(["ffn0"], z)
    ab, ca = _conv_fwd(z, W["ev_a_conv_w"][0], W["ev_a_conv_b"], W["ev_a_ln_g"], W["ev_a_ln_b"],
                       W["ev_b_conv_w"][0], W["ev_b_conv_b"], "ev_conv", after=token)
    (ev_w_out, *xa_w0), _ = comm.weights("xa0", ab)
    h1, nq0 = _mm([(ab, ev_w_out, "nn")], "ev_out", res=h0, rms_g=_row(W["g_xattn"][0]))
    token = comm.prefetch(["od", "xa1"], nq0)
    h2, nf0, xa0 = _xattn_fwd(h1, nq0, mem, W["g_mem"][0], *xa_w0, W["g_ffn"][0], "l0", token)
    ffn_w0, _ = comm.weights("ffn0", nf0)
    token = comm.prefetch(["ffn1"], nf0)
    h3, n3, ff0 = _ffn_fwd(h2, nf0, *ffn_w0, W["g_mix"][1], "l0", token)

    (od_w_in_t, od_w_out), _ = comm.weights("od", n3)
    zp = _mm([(n3, od_w_in_t, "nt")], "od_in", tn=1024)
    D = x.shape[1]
    ws = W["od_w_s"][0].astype(BF16)
    wst = jnp.swapaxes(ws, 1, 2)
    bsb = jnp.repeat(jnp.transpose(W["od_b_s"][0]), D // C_GROUPS, axis=1)
    y_sgu = _sgu_fwd(zp, W["od_c_ln_g"], W["od_c_ln_b"], ws, bsb, "od_sgu")
    h4, nq1 = _mm([(y_sgu, od_w_out, "nn")], "od_out", res=h3, rms_g=_row(W["g_xattn"][1]))
    xa_w1, _ = comm.weights("xa1", nq1)
    h5, nf1, xa1 = _xattn_fwd(h4, nq1, mem, W["g_mem"][1], *xa_w1, W["g_ffn"][1], "l1", None)
    ffn_w1, _ = comm.weights("ffn1", nf1)
    h6, _, ff1 = _ffn_fwd(h5, nf1, *ffn_w1, None, "l1", None)

    loss_row, dh6, d_gfinal = _loss_bwd(h6, _row(W["g_final"]), loss_target, "loss")
    grads["g_final"] = d_gfinal.reshape(-1)

    dh5, g_ff1 = _ffn_bwd(dh6, ff1, W["g_ffn"][1], *ffn_w1, "l1", lambda dws: comm.grads("ffn1", dws))
    dh4, g_xa1 = _xattn_bwd(dh5, xa1, mem, W["g_xattn"][1], W["g_mem"][1], *xa_w1, "l1",
                            lambda dws: comm.grads("xa1", dws))
    dy_sgu = _mm([(dh4, od_w_out, "nt")], "od_dy", tn=1024)
    d_od_out = _mm_tn(y_sgu, dh4, "od_dwout", tn=1024)
    dzp, d_ws, d_bsb, d_clng, d_clnb = _sgu_bwd(dy_sgu, zp, W["od_c_ln_g"], W["od_c_ln_b"], ws, wst, bsb, "od_sgu_bwd")
    grads["od_w_s"] = d_ws[None]
    grads["od_b_s"] = jnp.transpose(_group_sum(d_bsb, C_GROUPS, "od_dbs"))[None]
    grads["od_c_ln_g"], grads["od_c_ln_b"] = d_clng, d_clnb
    token = comm.grads("od", [_mm_tn(dzp, n3, "od_dwin", tn=512), d_od_out])
    dh3, d_gmix1 = _mm([(dzp, od_w_in_t, "nn")], "od_dn", rms_bwd=(h3, _row(W["g_mix"][1]), dh4), tm=512, after=token)

    dh2, g_ff0 = _ffn_bwd(dh3, ff0, W["g_ffn"][0], *ffn_w0, "l0", lambda dws: comm.grads("ffn0", dws))
    dh1, g_xa0 = _xattn_bwd(dh2, xa0, mem, W["g_xattn"][0], W["g_mem"][0], *xa_w0, "l0",
                            lambda dws: comm.grads("xa0", dws))
    dab = _mm([(dh1, ev_w_out, "nt")], "ev_dab", tn=1024)
    d_ev_out = _mm_tn(ab, dh1, "ev_dwout", tn=1024)
    dca, d_lng, d_lnb, d_ba = _conv_bwd_ln(dab, ca, W["ev_a_ln_g"], W["ev_a_ln_b"], "ev_conv_bwd_ln")
    dz, d_wa, d_wb, d_bb = _conv_bwd(z, dca, dab, W["ev_a_conv_w"][0], W["ev_b_conv_w"][0], W["ev_b_conv_b"],
                                     "ev_conv_bwd")
    grads.update(ev_a_ln_g=d_lng, ev_a_ln_b=d_lnb, ev_a_conv_b=d_ba, ev_b_conv_b=d_bb,
                 ev_a_conv_w=d_wa[None], ev_b_conv_w=d_wb[None])
    token = comm.grads("ev", [_mm_tn(dz, n0, "ev_dwin", tn=512), d_ev_out])
    grad_x, d_gmix0 = _mm([(dz, ev_w_in_t, "nn")], "ev_dn", rms_bwd=(h0, _row(W["g_mix"][0]), dh1), tm=512, after=token)

    grads["g_mix"] = jnp.concatenate([d_gmix0, d_gmix1], axis=0)
    for key in ("g_xattn", "g_mem"):
        grads[key] = jnp.concatenate([g_xa0[key], g_xa1[key]], axis=0)
    grads["g_ffn"] = jnp.concatenate([g_ff0["g_ffn"], g_ff1["g_ffn"]], axis=0)
    return loss_row, grad_x, grads


class _Exchanges:
    def __init__(self, shards, dev_idx, after):
        self.shards, self.dev_idx = shards, dev_idx
        self.gathering, self.scattering = {}, {}
        self.first = _all_gather_weights(self._pack(GATHERS["ev_in"]), self._rows(GATHERS["ev_in"]), "ag_ev_in",
                                         after=after)
        self.first_token = self.prefetch(["xa0"], self.first[0])

    def _rows(self, entries):
        return [self.shards[e].shape[0] for e in entries]

    def _pack(self, entries):
        return jnp.concatenate([self.shards[e] for e in entries], axis=0)

    def prefetch(self, gathers, after):
        for name in gathers:
            rows = self._rows(GATHERS[name])
            pack = self._pack(GATHERS[name])
            lands = [lax.empty((NDEV * r, pack.shape[1]), pack.dtype) for r in rows]
            plan, n = _gather_plan(rows)
            send, recv, srcs, lands, after = _split_start([pack], lands, plan, n, after, f"ag_{name}_start")
            self.gathering[name] = (send, recv, srcs, lands, plan, rows)
        return after

    def weights(self, name, after):
        if name == "ev_in":
            return self.first, self.first_token
        send, recv, srcs, lands, plan, rows = self.gathering.pop(name)
        _, lands = _split_wait(send, recv, srcs, lands, plan, after, f"ag_{name}_wait")
        return _place_own([self.shards[e] for e in GATHERS[name]], lands, self.dev_idx, f"ag_{name}_own"), None

    def grads(self, name, dws):
        rows = self._rows(SCATTERS[name])
        lands = [lax.empty((NDEV - 1, r, d.shape[1]), d.dtype) for r, d in zip(rows, dws)]
        plan, n = _scatter_plan(rows)
        send, recv, srcs, lands, token = _split_start(dws, lands, plan, n, None, f"rs_{name}_start")
        self.scattering[name] = (send, recv, srcs, lands, plan)
        return token

    def reduced(self, after):
        out = {}
        for name, (send, recv, srcs, lands, plan) in self.scattering.items():
            srcs, lands = _split_wait(send, recv, srcs, lands, plan, after, f"rs_{name}_wait")
            for (w, i), g, got in zip(SCATTERS[name], srcs, lands):
                out[w, i] = _scatter_sum(g, got, self.dev_idx, f"rs_sum_{w}_{i}")
        return out


def kernel(x, mem, g_mix, g_xattn, g_mem, g_ffn, g_final, ev_w_in, ev_a_conv_w, ev_a_conv_b, ev_a_ln_g, ev_a_ln_b, ev_b_conv_w, ev_b_conv_b, ev_w_out, od_w_in, od_c_ln_g, od_c_ln_b, od_w_s, od_b_s, od_w_out, xa_w_q, xa_w_k, xa_w_v, xa_w_o, ffn_w_gate, ffn_w_up, ffn_w_down, loss_target, m_g_mix, m_g_xattn, m_g_mem, m_g_ffn, m_g_final, m_ev_w_in, m_ev_a_conv_w, m_ev_a_conv_b, m_ev_a_ln_g, m_ev_a_ln_b, m_ev_b_conv_w, m_ev_b_conv_b, m_ev_w_out, m_od_w_in, m_od_c_ln_g, m_od_c_ln_b, m_od_w_s, m_od_b_s, m_od_w_out, m_xa_w_q, m_xa_w_k, m_xa_w_v, m_xa_w_o, m_ffn_w_gate, m_ffn_w_up, m_ffn_w_down, v_g_mix, v_g_xattn, v_g_mem, v_g_ffn, v_g_final, v_ev_w_in, v_ev_a_conv_w, v_ev_a_conv_b, v_ev_a_ln_g, v_ev_a_ln_b, v_ev_b_conv_w, v_ev_b_conv_b, v_ev_w_out, v_od_w_in, v_od_c_ln_g, v_od_c_ln_b, v_od_w_s, v_od_b_s, v_od_w_out, v_xa_w_q, v_xa_w_k, v_xa_w_v, v_xa_w_o, v_ffn_w_gate, v_ffn_w_up, v_ffn_w_down):
    local = dict(g_mix=g_mix, g_xattn=g_xattn, g_mem=g_mem, g_ffn=g_ffn, g_final=g_final, ev_w_in=ev_w_in, ev_a_conv_w=ev_a_conv_w, ev_a_conv_b=ev_a_conv_b, ev_a_ln_g=ev_a_ln_g, ev_a_ln_b=ev_a_ln_b, ev_b_conv_w=ev_b_conv_w, ev_b_conv_b=ev_b_conv_b, ev_w_out=ev_w_out, od_w_in=od_w_in, od_c_ln_g=od_c_ln_g, od_c_ln_b=od_c_ln_b, od_w_s=od_w_s, od_b_s=od_b_s, od_w_out=od_w_out, xa_w_q=xa_w_q, xa_w_k=xa_w_k, xa_w_v=xa_w_v, xa_w_o=xa_w_o, ffn_w_gate=ffn_w_gate, ffn_w_up=ffn_w_up, ffn_w_down=ffn_w_down)
    mom = dict(g_mix=m_g_mix, g_xattn=m_g_xattn, g_mem=m_g_mem, g_ffn=m_g_ffn, g_final=m_g_final, ev_w_in=m_ev_w_in, ev_a_conv_w=m_ev_a_conv_w, ev_a_conv_b=m_ev_a_conv_b, ev_a_ln_g=m_ev_a_ln_g, ev_a_ln_b=m_ev_a_ln_b, ev_b_conv_w=m_ev_b_conv_w, ev_b_conv_b=m_ev_b_conv_b, ev_w_out=m_ev_w_out, od_w_in=m_od_w_in, od_c_ln_g=m_od_c_ln_g, od_c_ln_b=m_od_c_ln_b, od_w_s=m_od_w_s, od_b_s=m_od_b_s, od_w_out=m_od_w_out, xa_w_q=m_xa_w_q, xa_w_k=m_xa_w_k, xa_w_v=m_xa_w_v, xa_w_o=m_xa_w_o, ffn_w_gate=m_ffn_w_gate, ffn_w_up=m_ffn_w_up, ffn_w_down=m_ffn_w_down)
    vel = dict(g_mix=v_g_mix, g_xattn=v_g_xattn, g_mem=v_g_mem, g_ffn=v_g_ffn, g_final=v_g_final, ev_w_in=v_ev_w_in, ev_a_conv_w=v_ev_a_conv_w, ev_a_conv_b=v_ev_a_conv_b, ev_a_ln_g=v_ev_a_ln_g, ev_a_ln_b=v_ev_a_ln_b, ev_b_conv_w=v_ev_b_conv_w, ev_b_conv_b=v_ev_b_conv_b, ev_w_out=v_ev_w_out, od_w_in=v_od_w_in, od_c_ln_g=v_od_c_ln_g, od_c_ln_b=v_od_c_ln_b, od_w_s=v_od_w_s, od_b_s=v_od_b_s, od_w_out=v_od_w_out, xa_w_q=v_xa_w_q, xa_w_k=v_xa_w_k, xa_w_v=v_xa_w_v, xa_w_o=v_xa_w_o, ffn_w_gate=v_ffn_w_gate, ffn_w_up=v_ffn_w_up, ffn_w_down=v_ffn_w_down)
    D = x.shape[-1]
    dev = 4 * lax.axis_index("x") + 2 * lax.axis_index("y") + lax.axis_index("c")

    def comm_layout(n, a):
        return jnp.transpose(a) if _shard_axis(n) == 2 else a

    shards = {(n, i): comm_layout(n, local[n][i]).astype(BF16) for n in BIG for i in range(local[n].shape[0])}
    small_sizes = [local[n].size for n in SMALL_SHARDED]
    small_block = _pad_rows(jnp.concatenate([local[n].reshape(-1) for n in SMALL_SHARDED]), 128, 8)
    small_all = _all_gather(small_block, "ag_small")
    comm = _Exchanges(shards, jnp.reshape(dev, (1,)).astype(jnp.int32), small_all)
    small_all = small_all.reshape(NDEV, -1)

    W = {n: local[n] for n in REPLICATED}
    o0 = 0
    for n, sz in zip(SMALL_SHARDED, small_sizes):
        blocks = small_all[:, o0:o0 + sz].reshape((NDEV,) + local[n].shape)
        W[n] = _full_from_blocks(blocks, _shard_axis(n))
        o0 += sz

    loss_row, grad_x, grads = _local_step(x[0], mem[0], loss_target[0], W, comm)

    reduced = comm.reduced(grad_x)
    gsh = {n: jnp.stack([comm_layout(n, reduced[n, i]) for i in range(local[n].shape[0])]) for n in BIG}

    rest = REPLICATED + SMALL_SHARDED
    rest_full_shapes = [grads[n].shape for n in rest]
    g_rest = _pad_rows(jnp.concatenate([grads[n].astype(F32).reshape(-1) for n in rest]), D, 8)
    g_rest = _sum_slots(_all_gather(g_rest, "ag_small_grads"), "sum_small_grads").reshape(-1)
    o0 = 0
    for n, shp in zip(rest, rest_full_shapes):
        sz = 1
        for s in shp:
            sz *= s
        full = g_rest[o0:o0 + sz].reshape(shp)
        o0 += sz
        if n in SMALL_SHARDED:
            full = lax.dynamic_index_in_dim(_blocks_from_full(full, _shard_axis(n)), dev, 0, keepdims=False)
        gsh[n] = full.reshape(local[n].shape)

    delta, new_m, new_v = {}, {}, {}
    for n in WEIGHTS:
        delta[n], new_m[n], new_v[n] = _adamw(local[n], gsh[n], mom[n], vel[n], f"adamw_{n}")

    loss = lax.psum(loss_row[0, 0], ("x", "y", "c"))
    return (loss, grad_x[None], *[gsh[n] for n in WEIGHTS], *[delta[n] for n in WEIGHTS],
            *[new_m[n] for n in WEIGHTS], *[new_v[n] for n in WEIGHTS])
```

```python
import jax
import jax.numpy as jnp
from jax import lax
from jax.experimental import pallas as pl
from jax.experimental.pallas import tpu as pltpu

F32, BF16 = jnp.float32, jnp.bfloat16
NDEV = 8
RMS_EPS = 1e-6
LN_EPS = 1e-5
CHUNK = 128
C_GROUPS = 8
XA_HEADS = 4
ADAM_LR, ADAM_B1, ADAM_B2, ADAM_EPS, ADAM_WD, ADAM_STEP = 0.001, 0.9, 0.999, 1e-08, 0.01, 10
HALO = 16
ROW_CHUNK = 32
V7X_VMEM_LIMIT = 56 * 1024 * 1024
MESH = pl.DeviceIdType.MESH

TS_ROW = 512
TS_MM = 1024
TN_MM = 1408
TS_FFN = 512
MM_ROW_CHUNK = 256
TS_CONV = 512
TS_SGU = 512
TS_ATTN = 512


def _cp(*sem):
    return pltpu.CompilerParams(dimension_semantics=sem, vmem_limit_bytes=V7X_VMEM_LIMIT)


def _pick(n, pref, align):
    for t in range(min(n, pref), 0, -1):
        if n % t == 0 and (t % align == 0 or t == n):
            return t
    return n


def _sigmoid(x):
    return 0.5 * jnp.tanh(0.5 * x) + 0.5


def _dot(a, b):
    return jnp.dot(a, b, preferred_element_type=F32)


def _dot_nt(a, b):
    return lax.dot_general(a, b, (((1,), (1,)), ((), ())), preferred_element_type=F32)


def _dot_tn(a, b):
    return lax.dot_general(a, b, (((0,), (0,)), ((), ())), preferred_element_type=F32)


_ANY = pl.BlockSpec(memory_space=pl.ANY)
_RESIDENT = pl.Buffered(1)


def _after(after):
    return ([], []) if after is None else ([_ANY], [after])


def _rms_fwd(h, g, name, after=None):
    S, D = h.shape
    ts = _pick(S, TS_MM, 16)
    after_specs, after_ops = _after(after)

    def body(h_ref, g_ref, *rest):
        o_ref = rest[-1]
        x = h_ref[...]
        r = lax.rsqrt(jnp.mean(x * x, axis=-1, keepdims=True) + RMS_EPS)
        o_ref[...] = ((x * r) * g_ref[...]).astype(o_ref.dtype)

    return pl.pallas_call(
        body, grid=(S // ts,),
        in_specs=[pl.BlockSpec((ts, D), lambda i: (i, 0)), pl.BlockSpec((1, D), lambda i: (0, 0))] + after_specs,
        out_specs=pl.BlockSpec((ts, D), lambda i: (i, 0)),
        out_shape=jax.ShapeDtypeStruct((S, D), BF16), compiler_params=_cp("parallel"), name=name)(h, g, *after_ops)


def _rms_bwd(dn, h, g, dres, name):
    S, D = h.shape
    ts = _pick(S, TS_ROW, 8)
    has_res = dres is not None

    def body(*refs):
        if has_res:
            dn_ref, h_ref, g_ref, dres_ref, dh_ref, dg_ref = refs
        else:
            dn_ref, h_ref, g_ref, dh_ref, dg_ref = refs
        x = h_ref[...]
        dn_ = dn_ref[...].astype(F32)
        r = lax.rsqrt(jnp.mean(x * x, axis=-1, keepdims=True) + RMS_EPS)
        xr = x * r

        @pl.when(pl.program_id(0) == 0)
        def _():
            dg_ref[...] = jnp.zeros_like(dg_ref)

        dg_ref[...] += jnp.sum(dn_ * xr, axis=0, keepdims=True)
        u = dn_ * g_ref[...]
        dh = r * u - xr * (r * jnp.mean(u * xr, axis=-1, keepdims=True))
        if has_res:
            dh = dh + dres_ref[...]
        dh_ref[...] = dh

    tile = pl.BlockSpec((ts, D), lambda i: (i, 0))
    vec = pl.BlockSpec((1, D), lambda i: (0, 0))
    ins = [dn, h, g] + ([dres] if has_res else [])
    return pl.pallas_call(
        body, grid=(S // ts,),
        in_specs=[tile, tile, vec] + ([tile] if has_res else []),
        out_specs=[tile, vec],
        out_shape=[jax.ShapeDtypeStruct((S, D), F32), jax.ShapeDtypeStruct((1, D), F32)],
        compiler_params=_cp("arbitrary"), name=name)(*ins)


def _loss_bwd(h, g, target, name):
    S, D = h.shape
    ts = _pick(S, TS_ROW, 8)

    def body(h_ref, g_ref, t_ref, loss_ref, dh_ref, dg_ref):
        x = h_ref[...]
        r = lax.rsqrt(jnp.mean(x * x, axis=-1, keepdims=True) + RMS_EPS)
        xr = x * r
        gg = g_ref[...]
        e = xr * gg - t_ref[...]

        @pl.when(pl.program_id(0) == 0)
        def _():
            dg_ref[...] = jnp.zeros_like(dg_ref)
            loss_ref[...] = jnp.zeros_like(loss_ref)

        tile_loss = jnp.sum(jnp.sum(e * e, axis=0, keepdims=True), axis=1, keepdims=True) * (0.5 / D)
        loss_ref[...] += jnp.broadcast_to(tile_loss, loss_ref.shape)
        dy = e * (1.0 / D)
        dg_ref[...] += jnp.sum(dy * xr, axis=0, keepdims=True)
        u = dy * gg
        dh_ref[...] = r * u - xr * (r * jnp.mean(u * xr, axis=-1, keepdims=True))

    tile = pl.BlockSpec((ts, D), lambda i: (i, 0))
    vec = pl.BlockSpec((1, D), lambda i: (0, 0))
    return pl.pallas_call(
        body, grid=(S // ts,),
        in_specs=[tile, vec, tile],
        out_specs=[pl.BlockSpec((1, 128), lambda i: (0, 0)), tile, vec],
        out_shape=[jax.ShapeDtypeStruct((1, 128), F32), jax.ShapeDtypeStruct((S, D), F32),
                   jax.ShapeDtypeStruct((1, D), F32)],
        compiler_params=_cp("arbitrary"), name=name)(h, g, target)


def _mm(pairs, name, out_dtype=F32, res=None, rms_g=None, rms_bwd=None, tm=None, tn=None, after=None):
    M = pairs[0][0].shape[0]
    N = pairs[0][1].shape[1 if pairs[0][2] == "nn" else 0]
    whole_rows = rms_g is not None or rms_bwd is not None
    tm = _pick(M, tm or TS_MM, 16)
    tn = N if whole_rows else _pick(N, tn or TN_MM, 128)
    npair = len(pairs)
    modes = [p[2] for p in pairs]
    after_specs, after_ops = _after(after)

    rc = MM_ROW_CHUNK if whole_rows and tm % MM_ROW_CHUNK == 0 else tm

    def body(*refs):
        rest = refs[2 * npair + len(after_ops):]
        res_ref = None
        if res is not None:
            res_ref, rest = rest[0], rest[1:]
        if rms_bwd is not None:
            dg_ref = rest[4]

            @pl.when(pl.program_id(0) == 0)
            def _():
                dg_ref[...] = jnp.zeros_like(dg_ref)

        for r0 in range(0, tm, rc):
            rows = pl.ds(r0, rc)
            acc = None
            for p in range(npair):
                a_ = refs[2 * p][rows, :].astype(BF16)
                d = _dot(a_, refs[2 * p + 1][...]) if modes[p] == "nn" else _dot_nt(a_, refs[2 * p + 1][...])
                acc = d if acc is None else acc + d
            if res_ref is not None:
                acc = acc + res_ref[rows, :]
            if rms_bwd is not None:
                h_ref, g_ref, dres_ref, dh_ref, _ = rest
                x = h_ref[rows, :]
                r = lax.rsqrt(jnp.mean(x * x, axis=-1, keepdims=True) + RMS_EPS)
                xr = x * r
                dg_ref[...] += jnp.sum(acc * xr, axis=0, keepdims=True)
                u = acc * g_ref[...]
                dh_ref[rows, :] = r * u - xr * (r * jnp.mean(u * xr, axis=-1, keepdims=True)) + dres_ref[rows, :]
            elif rms_g is not None:
                g_ref, o_ref, n_ref = rest
                o_ref[rows, :] = acc
                r = lax.rsqrt(jnp.mean(acc * acc, axis=-1, keepdims=True) + RMS_EPS)
                n_ref[rows, :] = ((acc * r) * g_ref[...]).astype(BF16)
            else:
                rest[0][rows, :] = acc.astype(rest[0].dtype)

    in_specs, ins = [], []
    for a, w, mode in pairs:
        K = a.shape[1]
        in_specs.append(pl.BlockSpec((tm, K), lambda i, j: (i, 0)))
        once = _RESIDENT if tn == N else None
        in_specs.append(pl.BlockSpec((K, tn), lambda i, j: (0, j), pipeline_mode=once) if mode == "nn"
                        else pl.BlockSpec((tn, K), lambda i, j: (j, 0), pipeline_mode=once))
        ins += [a, w]
    in_specs += after_specs
    ins += after_ops
    tile = pl.BlockSpec((tm, tn), lambda i, j: (i, j))
    vec = pl.BlockSpec((1, tn), lambda i, j: (0, j))
    if res is not None:
        in_specs.append(tile)
        ins.append(res)
    sem = ("parallel", "parallel")
    if rms_bwd is not None:
        in_specs += [tile, vec, tile]
        ins += list(rms_bwd)
        out_specs = [tile, vec]
        out_shape = [jax.ShapeDtypeStruct((M, N), F32), jax.ShapeDtypeStruct((1, N), F32)]
        sem = ("arbitrary", "arbitrary")
    elif rms_g is not None:
        in_specs.append(vec)
        ins.append(rms_g)
        out_specs = [tile, tile]
        out_shape = [jax.ShapeDtypeStruct((M, N), F32), jax.ShapeDtypeStruct((M, N), BF16)]
    else:
        out_specs = tile
        out_shape = jax.ShapeDtypeStruct((M, N), out_dtype)
    return pl.pallas_call(
        body, grid=(M // tm, N // tn), in_specs=in_specs, out_specs=out_specs, out_shape=out_shape,
        compiler_params=_cp(*sem), name=name)(*ins)


def _mm_tn(a, b, name, ts=None, tn=None):
    S, K = a.shape
    N = b.shape[1]
    ts = _pick(S, ts or TS_MM, 16)
    tn = _pick(N, tn or TN_MM, 128)
    nsteps = S // ts

    def body(a_ref, b_ref, o_ref, acc_ref):
        s = pl.program_id(1)

        @pl.when(s == 0)
        def _():
            acc_ref[...] = jnp.zeros_like(acc_ref)

        acc_ref[...] += _dot_tn(a_ref[...].astype(BF16), b_ref[...].astype(BF16))

        @pl.when(s == nsteps - 1)
        def _():
            o_ref[...] = acc_ref[...].astype(o_ref.dtype)

    return pl.pallas_call(
        body, grid=(N // tn, nsteps),
        in_specs=[pl.BlockSpec((ts, K), lambda j, s: (s, 0)), pl.BlockSpec((ts, tn), lambda j, s: (s, j))],
        out_specs=pl.BlockSpec((K, tn), lambda j, s: (0, j)), out_shape=jax.ShapeDtypeStruct((K, N), BF16),
        scratch_shapes=[pltpu.VMEM((K, tn), F32)],
        compiler_params=_cp("parallel", "arbitrary"), name=name)(a, b)


def _col_chunk(n):
    return 256 if n % 256 == 0 else 128


def _ffn_up(n, wgt, wut, name, after=None):
    S, D = n.shape
    F = wgt.shape[0]
    tm = _pick(S, TS_FFN, 16)
    ce = _col_chunk(F)
    after_specs, after_ops = _after(after)

    def body(n_ref, wg_ref, wu_ref, *rest):
        a_ref, b_ref, hid_ref = rest[-3:]
        x = n_ref[...]
        for c0 in range(0, F, ce):
            a = _dot_nt(x, wg_ref[c0:c0 + ce, :])
            b = _dot_nt(x, wu_ref[c0:c0 + ce, :])
            a_ref[:, c0:c0 + ce] = a.astype(BF16)
            b_ref[:, c0:c0 + ce] = b.astype(BF16)
            hid_ref[:, c0:c0 + ce] = (a * _sigmoid(a) * b).astype(BF16)

    wspec = pl.BlockSpec((F, D), lambda i: (0, 0), pipeline_mode=_RESIDENT)
    ospec = pl.BlockSpec((tm, F), lambda i: (i, 0))
    osh = jax.ShapeDtypeStruct((S, F), BF16)
    return pl.pallas_call(
        body, grid=(S // tm,),
        in_specs=[pl.BlockSpec((tm, D), lambda i: (i, 0)), wspec, wspec] + after_specs,
        out_specs=[ospec, ospec, ospec], out_shape=[osh, osh, osh],
        compiler_params=_cp("parallel"), name=name)(n, wgt, wut, *after_ops)


def _ffn_dhid(dh, wd, a, b, name):
    S, D = dh.shape
    F = wd.shape[0]
    tm = _pick(S, TS_FFN, 16)
    ce = _col_chunk(F)

    def body(dh_ref, wd_ref, a_ref, b_ref, da_ref, db_ref):
        x = dh_ref[...].astype(BF16)
        for c0 in range(0, F, ce):
            g = _dot_nt(x, wd_ref[c0:c0 + ce, :])
            a_ = a_ref[:, c0:c0 + ce].astype(F32)
            b_ = b_ref[:, c0:c0 + ce].astype(F32)
            sg = _sigmoid(a_)
            da_ref[:, c0:c0 + ce] = (g * b_ * (sg * (1.0 + a_ * (1.0 - sg)))).astype(BF16)
            db_ref[:, c0:c0 + ce] = (g * (a_ * sg)).astype(BF16)

    tile = pl.BlockSpec((tm, F), lambda i: (i, 0))
    osh = jax.ShapeDtypeStruct((S, F), BF16)
    return pl.pallas_call(
        body, grid=(S // tm,),
        in_specs=[pl.BlockSpec((tm, D), lambda i: (i, 0)),
                  pl.BlockSpec((F, D), lambda i: (0, 0), pipeline_mode=_RESIDENT), tile, tile],
        out_specs=[tile, tile], out_shape=[osh, osh],
        compiler_params=_cp("parallel"), name=name)(dh, wd, a, b)


def _softmax_rows(s):
    m = jnp.max(s, axis=-1, keepdims=True)
    p = jnp.exp(s - m)
    return p / jnp.sum(p, axis=-1, keepdims=True)


def _attn_fwd(q, k, v, name):
    S, D = q.shape
    M = k.shape[0]
    hd = D // XA_HEADS
    scale = hd ** -0.5
    ts = _pick(S, TS_ATTN, 16)

    def body(q_ref, k_ref, v_ref, o_ref):
        for h in range(XA_HEADS):
            sl = slice(h * hd, (h + 1) * hd)
            p = _softmax_rows(_dot_nt(q_ref[:, sl], k_ref[:, sl]) * scale)
            o_ref[:, sl] = _dot(p.astype(BF16), v_ref[:, sl]).astype(BF16)

    tile = pl.BlockSpec((ts, D), lambda i: (i, 0))
    memspec = pl.BlockSpec((M, D), lambda i: (0, 0))
    return pl.pallas_call(
        body, grid=(S // ts,), in_specs=[tile, memspec, memspec], out_specs=tile,
        out_shape=jax.ShapeDtypeStruct((S, D), BF16), compiler_params=_cp("parallel"), name=name)(q, k, v)


def _attn_bwd(q, k, v, do, name):
    S, D = q.shape
    M = k.shape[0]
    hd = D // XA_HEADS
    scale = hd ** -0.5
    ts = _pick(S, TS_ATTN, 16)

    def body(q_ref, k_ref, v_ref, do_ref, dq_ref, dk_ref, dv_ref):
        @pl.when(pl.program_id(0) == 0)
        def _():
            dk_ref[...] = jnp.zeros_like(dk_ref)
            dv_ref[...] = jnp.zeros_like(dv_ref)

        for h in range(XA_HEADS):
            sl = slice(h * hd, (h + 1) * hd)
            qh, kh, vh, doh = q_ref[:, sl], k_ref[:, sl], v_ref[:, sl], do_ref[:, sl]
            p = _softmax_rows(_dot_nt(qh, kh) * scale)
            dp = _dot_nt(doh, vh)
            dv_ref[:, sl] += _dot_tn(p.astype(BF16), doh)
            delta = jnp.sum(dp * p, axis=-1, keepdims=True)
            ds = (p * (dp - delta) * scale).astype(BF16)
            dq_ref[:, sl] = _dot(ds, kh).astype(BF16)
            dk_ref[:, sl] += _dot_tn(ds, qh)

    tile = pl.BlockSpec((ts, D), lambda i: (i, 0))
    memspec = pl.BlockSpec((M, D), lambda i: (0, 0))
    return pl.pallas_call(
        body, grid=(S // ts,), in_specs=[tile, memspec, memspec, tile], out_specs=[tile, memspec, memspec],
        out_shape=[jax.ShapeDtypeStruct((S, D), BF16), jax.ShapeDtypeStruct((M, D), F32),
                   jax.ShapeDtypeStruct((M, D), F32)],
        compiler_params=_cp("arbitrary"), name=name)(q, k, v, do)


def _halo_specs(ts, width, col):
    per = ts // HALO

    def prev(i):
        return (jnp.maximum(i * per - 1, 0), col)

    def nxt(i, n_tiles):
        return (jnp.minimum((i + 1) * per, n_tiles * per - 1), col)

    return prev, nxt


def _fill_ext(ext_ref, prev_val, main_val, next_val, first, last, ts):
    ext_ref[pl.ds(0, HALO), :] = jnp.where(first, 0.0, prev_val)
    ext_ref[pl.ds(HALO, ts), :] = main_val
    ext_ref[pl.ds(HALO + ts, HALO), :] = jnp.where(last, 0.0, next_val)


SUBLANES = 8


def _fill_shifted(sh_ref, ts):
    n = ts + 2 * HALO - SUBLANES
    for s in range(1, SUBLANES):
        sh_ref[s, pl.ds(0, n), :] = sh_ref[0, pl.ds(s, n), :]


def _tap(sh_ref, r0, offset, rc):
    q, s = divmod(offset, SUBLANES)
    return sh_ref[s, pl.ds(pl.multiple_of(r0 + SUBLANES * q, SUBLANES), rc), :]


def _conv_fwd(z, wa, ba, lng, lnb, wb, bb, name, after=None):
    S = z.shape[0]
    C = z.shape[1] // 5
    KA, KB = wa.shape[0], wb.shape[0]
    pa, pb = KA // 2, KB // 2
    assert pa <= HALO and pb <= HALO
    ts = _pick(S, TS_CONV, ROW_CHUNK)
    nt = S // ts
    rc = ROW_CHUNK
    prev, nxt = _halo_specs(ts, 5 * C, 0)
    after_specs, after_ops = _after(after)

    def body(*refs):
        compute(*refs[:9], *refs[9 + len(after_ops):])

    def compute(z_ref, zp_ref, zn_ref, wa_ref, ba_ref, lng_ref, lnb_ref, wb_ref, bb_ref, ab_ref, ca_ref,
                ga_sh, tb_ext, win_b):
        i = pl.program_id(0)
        first, last = i == 0, i == nt - 1

        def glu(r):
            return r[:, 0:C] * _sigmoid(r[:, C:2 * C])

        def gcb(r):
            return r[:, 4 * C:5 * C] * r[:, 2 * C:3 * C]

        _fill_ext(ga_sh.at[0], glu(zp_ref), glu(z_ref), glu(zn_ref), first, last, ts)
        _fill_shifted(ga_sh, ts)
        _fill_ext(tb_ext, gcb(zp_ref), gcb(z_ref), gcb(zn_ref), first, last, ts)

        def chunk(c, carry):
            r0 = pl.multiple_of(c * rc, rc)
            win_b[...] = tb_ext[pl.ds(r0, rc + 2 * HALO), :]
            acc = jnp.zeros((rc, C), F32)
            for k in range(KA):
                acc = acc + wa_ref[k:k + 1, :] * _tap(ga_sh, r0, HALO - pa + k, rc)
            ca = acc + ba_ref[...]
            ca_ref[pl.ds(r0, rc), :] = ca
            mu = jnp.mean(ca, axis=-1, keepdims=True)
            xc = ca - mu
            var = jnp.mean(xc * xc, axis=-1, keepdims=True)
            ln = xc * lax.rsqrt(var + LN_EPS) * lng_ref[...] + lnb_ref[...]
            ab_ref[pl.ds(r0, rc), 0:C] = (ln * _sigmoid(ln)).astype(BF16)
            cb = jnp.zeros((rc, C), F32) + bb_ref[...]
            for k in range(KB):
                cb = cb + wb_ref[k:k + 1, :] * win_b[pl.ds(HALO - pb + k, rc), :]
            ab_ref[pl.ds(r0, rc), C:2 * C] = (z_ref[pl.ds(r0, rc), 3 * C:4 * C] * cb).astype(BF16)
            return carry

        lax.fori_loop(0, ts // rc, chunk, 0)

    zspec = pl.BlockSpec((ts, 5 * C), lambda i: (i, 0))
    zprev = pl.BlockSpec((HALO, 5 * C), prev)
    znext = pl.BlockSpec((HALO, 5 * C), lambda i: nxt(i, nt))

    def full(a):
        return pl.BlockSpec(a.shape, lambda i: (0, 0))

    return pl.pallas_call(
        body, grid=(nt,),
        in_specs=[zspec, zprev, znext, full(wa), full(ba), full(lng), full(lnb), full(wb), full(bb)] + after_specs,
        out_specs=[pl.BlockSpec((ts, 2 * C), lambda i: (i, 0)), pl.BlockSpec((ts, C), lambda i: (i, 0))],
        out_shape=[jax.ShapeDtypeStruct((S, 2 * C), BF16), jax.ShapeDtypeStruct((S, C), F32)],
        scratch_shapes=[pltpu.VMEM((SUBLANES, ts + 2 * HALO, C), F32), pltpu.VMEM((ts + 2 * HALO, C), F32),
                        pltpu.VMEM((rc + 2 * HALO, C), F32)],
        compiler_params=_cp("parallel"), name=name)(z, z, z, wa, ba, lng, lnb, wb, bb, *after_ops)


def _conv_bwd_ln(dab, ca, lng, lnb, name):
    S, C = ca.shape
    ts = _pick(S, TS_ROW, 8)

    def body(da_ref, ca_ref, lng_ref, lnb_ref, dca_ref, dg_ref, db_ref, dbias_ref):
        @pl.when(pl.program_id(0) == 0)
        def _():
            dg_ref[...] = jnp.zeros_like(dg_ref)
            db_ref[...] = jnp.zeros_like(db_ref)
            dbias_ref[...] = jnp.zeros_like(dbias_ref)

        ca_ = ca_ref[...]
        mu = jnp.mean(ca_, axis=-1, keepdims=True)
        xc = ca_ - mu
        rstd = lax.rsqrt(jnp.mean(xc * xc, axis=-1, keepdims=True) + LN_EPS)
        xh = xc * rstd
        ln = xh * lng_ref[...] + lnb_ref[...]
        sg = _sigmoid(ln)
        dln = da_ref[...].astype(F32) * (sg * (1.0 + ln * (1.0 - sg)))
        dg_ref[...] += jnp.sum(dln * xh, axis=0, keepdims=True)
        db_ref[...] += jnp.sum(dln, axis=0, keepdims=True)
        dxh = dln * lng_ref[...]
        dca = rstd * (dxh - jnp.mean(dxh, axis=-1, keepdims=True) - xh * jnp.mean(dxh * xh, axis=-1, keepdims=True))
        dca_ref[...] = dca
        dbias_ref[...] += jnp.sum(dca, axis=0, keepdims=True)

    tile = pl.BlockSpec((ts, C), lambda i: (i, 0))
    vec = pl.BlockSpec((1, C), lambda i: (0, 0))
    vsh = jax.ShapeDtypeStruct((1, C), F32)
    return pl.pallas_call(
        body, grid=(S // ts,), in_specs=[tile, tile, vec, vec], out_specs=[tile, vec, vec, vec],
        out_shape=[jax.ShapeDtypeStruct((S, C), F32), vsh, vsh, vsh],
        compiler_params=_cp("arbitrary"), name=name)(dab, ca, lng, lnb)


def _conv_bwd(z, dca, dab, wa, wb, bb, name):
    S = z.shape[0]
    C = z.shape[1] // 5
    KA, KB = wa.shape[0], wb.shape[0]
    pa, pb = KA // 2, KB // 2
    ts = _pick(S, TS_CONV, ROW_CHUNK)
    nt = S // ts
    rc = ROW_CHUNK
    prev0, nxt0 = _halo_specs(ts, C, 0)
    prev1, nxt1 = _halo_specs(ts, C, 1)

    def body(z_ref, zp_ref, zn_ref, dca_ref, dcap_ref, dcan_ref, db_ref, dbp_ref, dbn_ref, wa_ref, wb_ref, bb_ref,
             dz_ref, dwa_ref, dwb_ref, dbb_ref,
             ga_sh, dca_sh, tb_ext, dcb_ext, win_tb, win_dcb, acc_a, acc_b, acc_bias):
        i = pl.program_id(0)
        first, last = i == 0, i == nt - 1

        @pl.when(first)
        def _():
            acc_a[...] = jnp.zeros_like(acc_a)
            acc_b[...] = jnp.zeros_like(acc_b)
            acc_bias[...] = jnp.zeros_like(acc_bias)

        def glu(r):
            return r[:, 0:C] * _sigmoid(r[:, C:2 * C])

        def gcb(r):
            return r[:, 4 * C:5 * C] * r[:, 2 * C:3 * C]

        def dcb(d, r):
            return d[...].astype(F32) * r[:, 3 * C:4 * C]

        _fill_ext(ga_sh.at[0], glu(zp_ref), glu(z_ref), glu(zn_ref), first, last, ts)
        _fill_shifted(ga_sh, ts)
        _fill_ext(dca_sh.at[0], dcap_ref[...], dca_ref[...], dcan_ref[...], first, last, ts)
        _fill_shifted(dca_sh, ts)
        _fill_ext(tb_ext, gcb(zp_ref), gcb(z_ref), gcb(zn_ref), first, last, ts)
        _fill_ext(dcb_ext, dcb(dbp_ref, zp_ref), dcb(db_ref, z_ref), dcb(dbn_ref, zn_ref), first, last, ts)

        def fold(x):
            return jnp.sum(x.reshape(rc // 8, 8, C), axis=0)

        def chunk(c, carry):
            r0 = pl.multiple_of(c * rc, rc)
            win_tb[...] = tb_ext[pl.ds(r0, rc + 2 * HALO), :]
            win_dcb[...] = dcb_ext[pl.ds(r0, rc + 2 * HALO), :]
            dca_c = _tap(dca_sh, r0, HALO, rc)
            dglu = jnp.zeros((rc, C), F32)
            for k in range(KA):
                dglu = dglu + wa_ref[k:k + 1, :] * _tap(dca_sh, r0, HALO + pa - k, rc)
                acc_a[k] += fold(dca_c * _tap(ga_sh, r0, HALO - pa + k, rc))
            val = z_ref[pl.ds(r0, rc), 0:C]
            sg = _sigmoid(z_ref[pl.ds(r0, rc), C:2 * C])
            dz_ref[pl.ds(r0, rc), 0:C] = (dglu * sg).astype(BF16)
            dz_ref[pl.ds(r0, rc), C:2 * C] = (dglu * val * sg * (1.0 - sg)).astype(BF16)
            dcb_c = win_dcb[pl.ds(HALO, rc), :]
            cb = jnp.zeros((rc, C), F32) + bb_ref[...]
            dt = jnp.zeros((rc, C), F32)
            for k in range(KB):
                tb_k = win_tb[pl.ds(HALO - pb + k, rc), :]
                cb = cb + wb_ref[k:k + 1, :] * tb_k
                dt = dt + wb_ref[k:k + 1, :] * win_dcb[pl.ds(HALO + pb - k, rc), :]
                acc_b[k] += fold(dcb_c * tb_k)
            acc_bias[...] += fold(dcb_c)
            db_c = db_ref[pl.ds(r0, rc), :].astype(F32)
            dz_ref[pl.ds(r0, rc), 2 * C:3 * C] = (dt * z_ref[pl.ds(r0, rc), 4 * C:5 * C]).astype(BF16)
            dz_ref[pl.ds(r0, rc), 3 * C:4 * C] = (db_c * cb).astype(BF16)
            dz_ref[pl.ds(r0, rc), 4 * C:5 * C] = (dt * z_ref[pl.ds(r0, rc), 2 * C:3 * C]).astype(BF16)
            return carry

        lax.fori_loop(0, ts // rc, chunk, 0)

        @pl.when(last)
        def _():
            dwa_ref[...] = jnp.sum(acc_a[...], axis=1)
            dwb_ref[...] = jnp.sum(acc_b[...], axis=1)
            dbb_ref[...] = jnp.sum(acc_bias[...], axis=0, keepdims=True)

    zspec = pl.BlockSpec((ts, 5 * C), lambda i: (i, 0))
    zprev = pl.BlockSpec((HALO, 5 * C), prev0)
    znext = pl.BlockSpec((HALO, 5 * C), lambda i: nxt0(i, nt))
    dspec = pl.BlockSpec((ts, C), lambda i: (i, 0))
    dprev = pl.BlockSpec((HALO, C), prev0)
    dnext = pl.BlockSpec((HALO, C), lambda i: nxt0(i, nt))
    bspec = pl.BlockSpec((ts, C), lambda i: (i, 1))
    bprev = pl.BlockSpec((HALO, C), prev1)
    bnext = pl.BlockSpec((HALO, C), lambda i: nxt1(i, nt))

    def full(shape):
        return pl.BlockSpec(shape, lambda i: (0,) * len(shape))

    ext = pltpu.VMEM((ts + 2 * HALO, C), F32)
    shifted = pltpu.VMEM((SUBLANES, ts + 2 * HALO, C), F32)
    win = pltpu.VMEM((rc + 2 * HALO, C), F32)
    return pl.pallas_call(
        body, grid=(nt,),
        in_specs=[zspec, zprev, znext, dspec, dprev, dnext, bspec, bprev, bnext,
                  full(wa.shape), full(wb.shape), full(bb.shape)],
        out_specs=[pl.BlockSpec((ts, 5 * C), lambda i: (i, 0)), full((KA, C)), full((KB, C)), full((1, C))],
        out_shape=[jax.ShapeDtypeStruct((S, 5 * C), BF16), jax.ShapeDtypeStruct((KA, C), F32),
                   jax.ShapeDtypeStruct((KB, C), F32), jax.ShapeDtypeStruct((1, C), F32)],
        scratch_shapes=[shifted, shifted, ext, ext, win, win,
                        pltpu.VMEM((KA, 8, C), F32), pltpu.VMEM((KB, 8, C), F32), pltpu.VMEM((8, C), F32)],
        compiler_params=_cp("arbitrary"), name=name)(z, z, z, dca, dca, dca, dab, dab, dab, wa, wb, bb)


_GELU_C = 0.7978845608028654
_GELU_A = 0.044715


def _gelu(x):
    return 0.5 * x * (1.0 + jnp.tanh(_GELU_C * (x + _GELU_A * (x * x * x))))


def _gelu_and_grad(x):
    t = jnp.tanh(_GELU_C * (x + _GELU_A * (x * x * x)))
    hx = 0.5 * x
    return hx * (1.0 + t), 0.5 * (1.0 + t) + hx * (1.0 - t * t) * (_GELU_C * (1.0 + 3.0 * _GELU_A * x * x))


def _sgu_fwd(zp, lng, lnb, ws, bsb, name):
    S = zp.shape[0]
    D = zp.shape[1] // 2
    G = ws.shape[0]
    gd = D // G
    ts = _pick(S, TS_SGU, CHUNK)
    ncs = ts // CHUNK

    def body(zp_ref, lng_ref, lnb_ref, ws_ref, bsb_ref, y_ref, vb_ref):
        v = _gelu(zp_ref[:, D:2 * D])
        mu = jnp.mean(v, axis=-1, keepdims=True)
        xc = v - mu
        rstd = lax.rsqrt(jnp.mean(xc * xc, axis=-1, keepdims=True) + LN_EPS)
        vb_ref[...] = (xc * rstd * lng_ref[...] + lnb_ref[...]).astype(BF16)
        for c in range(ncs):
            rows = slice(c * CHUNK, (c + 1) * CHUNK)
            for g in range(G):
                cols = slice(g * gd, (g + 1) * gd)
                sv = _dot(ws_ref[g], vb_ref[rows, cols]) + bsb_ref[:, cols]
                y_ref[rows, cols] = (_gelu(zp_ref[rows, cols]) * sv).astype(BF16)

    def full(a):
        return pl.BlockSpec(a.shape, lambda i: (0,) * a.ndim)

    return pl.pallas_call(
        body, grid=(S // ts,),
        in_specs=[pl.BlockSpec((ts, 2 * D), lambda i: (i, 0)), full(lng), full(lnb), full(ws), full(bsb)],
        out_specs=pl.BlockSpec((ts, D), lambda i: (i, 0)), out_shape=jax.ShapeDtypeStruct((S, D), BF16),
        scratch_shapes=[pltpu.VMEM((ts, D), BF16)],
        compiler_params=_cp("parallel"), name=name)(zp, lng, lnb, ws, bsb)


def _sgu_bwd(dy, zp, lng, lnb, ws, wst, bsb, name):
    S = zp.shape[0]
    D = zp.shape[1] // 2
    G = ws.shape[0]
    gd = D // G
    ts = _pick(S, TS_SGU, CHUNK)
    ncs = ts // CHUNK

    def body(dy_ref, zp_ref, lng_ref, lnb_ref, ws_ref, wst_ref, bsb_ref,
             dzp_ref, dws_ref, dbs_ref, dg_ref, db_ref, vb_ref, dvln_ref, acc_bs):
        i = pl.program_id(0)

        @pl.when(i == 0)
        def _():
            dws_ref[...] = jnp.zeros_like(dws_ref)
            acc_bs[...] = jnp.zeros_like(acc_bs)
            dg_ref[...] = jnp.zeros_like(dg_ref)
            db_ref[...] = jnp.zeros_like(db_ref)

        v, dv_dz = _gelu_and_grad(zp_ref[:, D:2 * D])
        mu = jnp.mean(v, axis=-1, keepdims=True)
        xc = v - mu
        rstd = lax.rsqrt(jnp.mean(xc * xc, axis=-1, keepdims=True) + LN_EPS)
        xh = xc * rstd
        vb_ref[...] = (xh * lng_ref[...] + lnb_ref[...]).astype(BF16)
        for c in range(ncs):
            rows = slice(c * CHUNK, (c + 1) * CHUNK)
            for g in range(G):
                cols = slice(g * gd, (g + 1) * gd)
                u, du_dz = _gelu_and_grad(zp_ref[rows, cols])
                dy_ = dy_ref[rows, cols].astype(F32)
                sv = _dot(ws_ref[g], vb_ref[rows, cols]) + bsb_ref[:, cols]
                dzp_ref[rows, cols] = (dy_ * sv * du_dz).astype(BF16)
                dsv = dy_ * u
                acc_bs[:, cols] += dsv
                dsvb = dsv.astype(BF16)
                dws_ref[g] += _dot_nt(dsvb, vb_ref[rows, cols])
                dvln_ref[rows, cols] = _dot(wst_ref[g], dsvb)
        dvln = dvln_ref[...]
        dg_ref[...] += jnp.sum(dvln * xh, axis=0, keepdims=True)
        db_ref[...] += jnp.sum(dvln, axis=0, keepdims=True)
        dxh = dvln * lng_ref[...]
        dv = rstd * (dxh - jnp.mean(dxh, axis=-1, keepdims=True) - xh * jnp.mean(dxh * xh, axis=-1, keepdims=True))
        dzp_ref[:, D:2 * D] = (dv * dv_dz).astype(BF16)

        @pl.when(i == pl.num_programs(0) - 1)
        def _():
            dbs_ref[...] = acc_bs[...]

    def full(shape):
        return pl.BlockSpec(shape, lambda i: (0,) * len(shape))

    return pl.pallas_call(
        body, grid=(S // ts,),
        in_specs=[pl.BlockSpec((ts, D), lambda i: (i, 0)), pl.BlockSpec((ts, 2 * D), lambda i: (i, 0)),
                  full(lng.shape), full(lnb.shape), full(ws.shape), full(wst.shape), full(bsb.shape)],
        out_specs=[pl.BlockSpec((ts, 2 * D), lambda i: (i, 0)), full(ws.shape), full(bsb.shape),
                   full((1, D)), full((1, D))],
        out_shape=[jax.ShapeDtypeStruct((S, 2 * D), BF16), jax.ShapeDtypeStruct(ws.shape, F32),
                   jax.ShapeDtypeStruct(bsb.shape, F32), jax.ShapeDtypeStruct((1, D), F32),
                   jax.ShapeDtypeStruct((1, D), F32)],
        scratch_shapes=[pltpu.VMEM((ts, D), BF16), pltpu.VMEM((ts, D), F32),
                        pltpu.VMEM(bsb.shape, F32)],
        compiler_params=_cp("arbitrary"), name=name)(dy, zp, lng, lnb, ws, wst, bsb)


def _group_sum(x, groups, name):
    P, D = x.shape
    gd = D // groups

    def body(x_ref, o_ref):
        for g in range(groups):
            o_ref[:, g:g + 1] = jnp.sum(x_ref[:, g * gd:(g + 1) * gd], axis=1, keepdims=True)

    return pl.pallas_call(body, out_shape=jax.ShapeDtypeStruct((P, groups), F32), name=name)(x)


def _adamw(w, g, m, v, name):
    shape = w.shape
    C = shape[-1]
    R = w.size // C
    tr = _pick(R, 1024, 8)
    bc1 = 1.0 - ADAM_B1 ** ADAM_STEP
    bc2 = 1.0 - ADAM_B2 ** ADAM_STEP

    def body(w_ref, g_ref, m_ref, v_ref, d_ref, nm_ref, nv_ref):
        g_ = g_ref[...]
        nm = ADAM_B1 * m_ref[...] + (1.0 - ADAM_B1) * g_
        nv = ADAM_B2 * v_ref[...] + (1.0 - ADAM_B2) * (g_ * g_)
        nm_ref[...] = nm
        nv_ref[...] = nv
        d_ref[...] = -ADAM_LR * ((nm / bc1) / (jnp.sqrt(nv / bc2) + ADAM_EPS) + ADAM_WD * w_ref[...])

    tile = pl.BlockSpec((tr, C), lambda i: (i, 0))
    sh = jax.ShapeDtypeStruct((R, C), F32)
    outs = pl.pallas_call(
        body, grid=(R // tr,), in_specs=[tile] * 4, out_specs=[tile] * 3, out_shape=[sh] * 3,
        compiler_params=_cp("parallel"), name=name)(*(a.reshape(R, C) for a in (w, g, m, v)))
    return tuple(o.reshape(shape) for o in outs)


_HBM = pl.BlockSpec(memory_space=pltpu.HBM)


def _remote(src, dst, send_sem, recv_sem, to):
    return pltpu.make_async_remote_copy(src_ref=src, dst_ref=dst, send_sem=send_sem, recv_sem=recv_sem,
                                        device_id=to, device_id_type=MESH)


def _all_gather(block, name):
    R, C = block.shape

    def body(x_ref, out_ref, send_sems, recv_sems, local_sem):
        x, y, c = lax.axis_index("x"), lax.axis_index("y"), lax.axis_index("c")
        me, sibling = (x, y, c), (x, y, 1 - c)
        chips = [(1 - x, y), (x, 1 - y), (1 - x, 1 - y)]

        def slot(px, py, pc):
            return out_ref.at[4 * px + 2 * py + pc]

        def copy(k, blk, to, src=None):
            return _remote(slot(*blk) if src is None else src, slot(*blk), send_sems.at[k], recv_sems.at[k], to)

        mine = pltpu.make_async_copy(x_ref, slot(*me), local_sem)
        mine.start()
        first = [copy(0, me, sibling, src=x_ref)]
        first += [copy(1 + j, me, (*chip, c), src=x_ref) for j, chip in enumerate(chips)]
        for cp in first:
            cp.start()
        passed = [copy(4 + j, (*chip, c), sibling) for j, chip in enumerate(chips)]
        for j, chip in enumerate(chips):
            copy(1 + j, (*chip, c), me).wait_recv()
            passed[j].start()
        copy(0, sibling, me).wait_recv()
        for j, chip in enumerate(chips):
            copy(4 + j, (*chip, 1 - c), me).wait_recv()
        for cp in first + passed:
            cp.wait_send()
        mine.wait()

    return pl.pallas_call(
        body, out_shape=jax.ShapeDtypeStruct((NDEV, R, C), block.dtype), in_specs=[_HBM], out_specs=_HBM,
        scratch_shapes=[pltpu.SemaphoreType.DMA((7,)), pltpu.SemaphoreType.DMA((7,)), pltpu.SemaphoreType.DMA],
        name=name)(block)


def _all_gather_weights(pack, rows, name, after=None):
    C = pack.shape[1]
    nw = len(rows)
    starts = [sum(rows[:w]) for w in range(nw)]
    after_specs, after_ops = _after(after)

    def body(pack_ref, *rest):
        rest = rest[len(after_ops):]
        outs = rest[:nw]
        send_sems, recv_sems, local_sem = rest[nw:]
        x, y, c = lax.axis_index("x"), lax.axis_index("y"), lax.axis_index("c")
        me, sibling = (x, y, c), (x, y, 1 - c)
        chips = [(1 - x, y), (x, 1 - y), (1 - x, 1 - y)]

        def block(w, px, py, pc):
            return outs[w].at[pl.ds((4 * px + 2 * py + pc) * rows[w], rows[w])]

        def mine(w):
            return pack_ref.at[pl.ds(starts[w], rows[w])]

        def all_of(k):
            return _remote(pack_ref, pack_ref, send_sems.at[k], recv_sems.at[k], me)

        for w in range(nw):
            pltpu.make_async_copy(mine(w), block(w, *me), local_sem).start()
        for k, to in enumerate([sibling] + [(*chip, c) for chip in chips]):
            for w in range(nw):
                _remote(mine(w), block(w, *me), send_sems.at[k], recv_sems.at[k], to).start()
        for j, chip in enumerate(chips):
            all_of(1 + j).wait_recv()
            for w in range(nw):
                _remote(block(w, *chip, c), block(w, *chip, c), send_sems.at[4 + j], recv_sems.at[4 + j], sibling).start()
        all_of(0).wait_recv()
        for j in range(3):
            all_of(4 + j).wait_recv()
        for k in range(7):
            all_of(k).wait_send()
        pltpu.make_async_copy(pack_ref, pack_ref, local_sem).wait()

    return pl.pallas_call(
        body, out_shape=[jax.ShapeDtypeStruct((NDEV * r, C), pack.dtype) for r in rows],
        in_specs=[_HBM] + after_specs, out_specs=[_HBM] * nw,
        scratch_shapes=[pltpu.SemaphoreType.DMA((7,)), pltpu.SemaphoreType.DMA((7,)), pltpu.SemaphoreType.DMA],
        name=name)(pack, *after_ops)


_SEM = pl.BlockSpec(memory_space=pltpu.SEMAPHORE)
_DATAFLOW = pltpu.SideEffectType.DATAFLOW_SIDE_EFFECTING


def _split_start(srcs, lands, plan, n, after, name):
    nbuf = len(srcs) + len(lands)
    after_specs, after_ops = _after(after)

    def body(*refs):
        src_refs, land_refs = refs[:len(srcs)], refs[len(srcs):nbuf]
        send_sems, recv_sems = refs[nbuf + len(after_ops)], refs[nbuf + len(after_ops) + 1]
        for k, (src, dst, to) in enumerate(plan(src_refs, land_refs)):
            _remote(src, dst, send_sems.at[k], recv_sems.at[k], to).start()
        refs[-1][...] = jnp.zeros_like(refs[-1])

    bufs = [pltpu.with_memory_space_constraint(a, pltpu.HBM) for a in list(srcs) + list(lands)]
    outs = pl.pallas_call(
        body, name=name,
        out_shape=(pltpu.SemaphoreType.DMA((n,)), pltpu.SemaphoreType.DMA((n,)),
                   *[pltpu.HBM(a.shape, a.dtype) for a in bufs], jax.ShapeDtypeStruct((8, 128), F32)),
        in_specs=[_HBM] * nbuf + after_specs,
        out_specs=(_SEM, _SEM, *[_HBM] * nbuf, pl.BlockSpec(memory_space=pltpu.VMEM)),
        input_output_aliases={i: 2 + i for i in range(nbuf)},
        compiler_params=pltpu.CompilerParams(has_side_effects=_DATAFLOW))(*bufs, *after_ops)
    return outs[0], outs[1], list(outs[2:2 + len(srcs)]), list(outs[2 + len(srcs):2 + nbuf]), outs[-1]


def _split_wait(send_sems, recv_sems, srcs, lands, plan, after, name):
    nbuf = len(srcs) + len(lands)
    after_specs, after_ops = _after(after)

    def body(*refs):
        src_refs, land_refs = refs[:len(srcs)], refs[len(srcs):nbuf]
        send_sems_ref, recv_sems_ref = refs[nbuf], refs[nbuf + 1]
        for k, (src, dst, to) in enumerate(plan(src_refs, land_refs)):
            copy = _remote(src, dst, send_sems_ref.at[k], recv_sems_ref.at[k], to)
            copy.wait_send()
            copy.wait_recv()

    outs = pl.pallas_call(
        body, name=name, out_shape=tuple(pltpu.HBM(a.shape, a.dtype) for a in list(srcs) + list(lands)),
        in_specs=[_HBM] * nbuf + [_SEM, _SEM] + after_specs, out_specs=tuple([_HBM] * nbuf),
        input_output_aliases={i: i for i in range(nbuf)},
        compiler_params=pltpu.CompilerParams(has_side_effects=_DATAFLOW))(*srcs, *lands, send_sems, recv_sems, *after_ops)
    return list(outs[:len(srcs)]), list(outs[len(srcs):])


def _peers(x, y, c):
    return [(mask, (1 - x if mask & 4 else x, 1 - y if mask & 2 else y, 1 - c if mask & 1 else c))
            for mask in range(1, NDEV)]


def _gather_plan(rows):
    starts = [sum(rows[:w]) for w in range(len(rows))]

    def plan(src_refs, land_refs):
        x, y, c = lax.axis_index("x"), lax.axis_index("y"), lax.axis_index("c")
        copies = []
        for w, r in enumerate(rows):
            mine = src_refs[0].at[pl.ds(starts[w], r)]
            dst = land_refs[w].at[pl.ds((4 * x + 2 * y + c) * r, r)]
            copies += [(mine, dst, peer) for _, peer in _peers(x, y, c)]
        return copies

    return plan, (NDEV - 1) * len(rows)


def _place_own(shards, fulls, dev_idx, name):
    nw = len(shards)

    def body(i_ref, *refs):
        for w in range(nw):
            refs[2 * nw + w][...] = refs[w][...]

    grid_spec = pltpu.PrefetchScalarGridSpec(
        num_scalar_prefetch=1, grid=(1,),
        in_specs=[pl.BlockSpec(s.shape, lambda t, i_ref: (0, 0)) for s in shards] + [_ANY] * nw,
        out_specs=[pl.BlockSpec(s.shape, lambda t, i_ref: (i_ref[0], 0)) for s in shards])
    outs = pl.pallas_call(
        body, grid_spec=grid_spec, out_shape=[jax.ShapeDtypeStruct(f.shape, f.dtype) for f in fulls],
        input_output_aliases={1 + nw + w: w for w in range(nw)}, name=name)(dev_idx, *shards, *fulls)
    return list(outs)


def _scatter_plan(rows):
    def plan(src_refs, land_refs):
        x, y, c = lax.axis_index("x"), lax.axis_index("y"), lax.axis_index("c")
        copies = []
        for w, r in enumerate(rows):
            for mask, (px, py, pc) in _peers(x, y, c):
                src = src_refs[w].at[pl.ds((4 * px + 2 * py + pc) * r, r)]
                copies.append((src, land_refs[w].at[mask - 1], (px, py, pc)))
        return copies

    return plan, (NDEV - 1) * len(rows)


def _scatter_sum(g, got, dev_idx, name):
    n1, r, C = got.shape
    tr = _pick(r, 512, 16)
    per = r // tr

    def body(i_ref, g_ref, got_ref, o_ref):
        acc = g_ref[...].astype(F32)
        for k in range(n1):
            acc = acc + got_ref[k].astype(F32)
        o_ref[...] = acc

    grid_spec = pltpu.PrefetchScalarGridSpec(
        num_scalar_prefetch=1, grid=(per,),
        in_specs=[pl.BlockSpec((tr, C), lambda t, i_ref: (i_ref[0] * per + t, 0)),
                  pl.BlockSpec((n1, tr, C), lambda t, i_ref: (0, t, 0))],
        out_specs=pl.BlockSpec((tr, C), lambda t, i_ref: (t, 0)))
    return pl.pallas_call(
        body, grid_spec=grid_spec, out_shape=jax.ShapeDtypeStruct((r, C), F32),
        compiler_params=_cp("parallel"), name=name)(dev_idx, g, got)


def _sum_slots(a, name):
    n, R, C = a.shape

    def body(a_ref, o_ref):
        acc = a_ref[0]
        for k in range(1, n):
            acc = acc + a_ref[k]
        o_ref[...] = acc

    return pl.pallas_call(body, out_shape=jax.ShapeDtypeStruct((R, C), F32), name=name)(a)


def _shard_axis(name):
    return {"ev_w_in": 2, "ev_a_conv_w": 2, "ev_b_conv_w": 2, "ev_w_out": 1, "od_w_in": 2, "od_c_ln_g": 1,
            "od_c_ln_b": 1, "od_w_out": 1, "xa_w_q": 1, "xa_w_k": 1, "xa_w_v": 1, "xa_w_o": 1,
            "ffn_w_gate": 2, "ffn_w_up": 2, "ffn_w_down": 1}[name]


BIG = ["ev_w_in", "ev_w_out", "od_w_in", "od_w_out", "xa_w_q", "xa_w_k", "xa_w_v", "xa_w_o",
       "ffn_w_gate", "ffn_w_up", "ffn_w_down"]
SMALL_SHARDED = ["ev_a_conv_w", "ev_b_conv_w", "od_c_ln_g", "od_c_ln_b"]
REPLICATED = ["g_mix", "g_xattn", "g_mem", "g_ffn", "g_final", "ev_a_conv_b", "ev_a_ln_g", "ev_a_ln_b",
              "ev_b_conv_b", "od_w_s", "od_b_s"]
WEIGHTS = ["g_mix", "g_xattn", "g_mem", "g_ffn", "g_final", "ev_w_in", "ev_a_conv_w", "ev_a_conv_b", "ev_a_ln_g",
           "ev_a_ln_b", "ev_b_conv_w", "ev_b_conv_b", "ev_w_out", "od_w_in", "od_c_ln_g", "od_c_ln_b", "od_w_s",
           "od_b_s", "od_w_out", "xa_w_q", "xa_w_k", "xa_w_v", "xa_w_o", "ffn_w_gate", "ffn_w_up", "ffn_w_down"]


def _full_from_blocks(blocks, axis):
    shard = blocks.shape[1:]
    full = jnp.moveaxis(blocks, 0, axis)
    return full.reshape(shard[:axis] + (NDEV * shard[axis],) + shard[axis + 1:])


def _blocks_from_full(full, axis):
    shp = full.shape
    split = full.reshape(shp[:axis] + (NDEV, shp[axis] // NDEV) + shp[axis + 1:])
    return jnp.moveaxis(split, axis, 0)


def _pad_rows(flat, width, row_align):
    per = width * row_align
    n = -(-flat.shape[0] // per) * per
    return jnp.pad(flat, (0, n - flat.shape[0])).reshape(n // width, width)


def _row(v):
    return v.reshape(1, -1)


def _xattn_fwd(h, nq, mem, g_m, wq, wk, wv, wo, g_next, tag, after):
    mem_n = _rms_fwd(mem, _row(g_m), f"xa_mem_rms_{tag}")
    q = _mm([(nq, wq, "nn")], f"xa_q_{tag}", out_dtype=BF16, after=after)
    k = _mm([(mem_n, wk, "nn")], f"xa_k_{tag}", out_dtype=BF16)
    v = _mm([(mem_n, wv, "nn")], f"xa_v_{tag}", out_dtype=BF16)
    o = _attn_fwd(q, k, v, f"xa_attn_{tag}")
    h_new, n_next = _mm([(o, wo, "nn")], f"xa_o_{tag}", res=h, rms_g=_row(g_next))
    return h_new, n_next, (h, nq, mem_n, q, k, v, o)


def _xattn_bwd(dh_new, saved, mem, g_x, g_m, wq, wk, wv, wo, tag, push):
    h, nq, mem_n, q, k, v, o = saved
    do = _mm([(dh_new, wo, "nt")], f"xa_do_{tag}", out_dtype=BF16)
    d_wo = _mm_tn(o, dh_new, f"xa_dwo_{tag}")
    dq, dk, dv = _attn_bwd(q, k, v, do, f"xa_attn_bwd_{tag}")
    d_wq = _mm_tn(nq, dq, f"xa_dwq_{tag}")
    d_wk = _mm_tn(mem_n, dk, f"xa_dwk_{tag}")
    d_wv = _mm_tn(mem_n, dv, f"xa_dwv_{tag}")
    token = push([d_wq, d_wk, d_wv, d_wo])
    dmem_n = _mm([(dk, wk, "nt"), (dv, wv, "nt")], f"xa_dmem_{tag}", after=token)
    _, d_gm = _rms_bwd(dmem_n, mem, _row(g_m), None, f"xa_mem_rms_bwd_{tag}")
    dh, d_gx = _mm([(dq, wq, "nt")], f"xa_dnq_{tag}", rms_bwd=(h, _row(g_x), dh_new), tm=512, after=token)
    return dh, dict(g_xattn=d_gx, g_mem=d_gm)


def _ffn_fwd(h, n, wgt, wut, wd, g_next, tag, after):
    a, b, hid = _ffn_up(n, wgt, wut, f"ffn_up_{tag}", after=after)
    if g_next is None:
        h_new, n_next = _mm([(hid, wd, "nn")], f"ffn_down_{tag}", res=h, tm=512, tn=1024), None
    else:
        h_new, n_next = _mm([(hid, wd, "nn")], f"ffn_down_{tag}", res=h, rms_g=_row(g_next), tm=512)
    return h_new, n_next, (h, n, a, b, hid)


def _ffn_bwd(dh_new, saved, g_f, wgt, wut, wd, tag, push):
    h, n, a, b, hid = saved
    da, db = _ffn_dhid(dh_new, wd, a, b, f"ffn_dhid_{tag}")
    d_wd = _mm_tn(hid, dh_new, f"ffn_dwd_{tag}", tn=512)
    d_wgt = _mm_tn(da, n, f"ffn_dwg_{tag}", tn=512)
    d_wut = _mm_tn(db, n, f"ffn_dwu_{tag}", tn=512)
    token = push([d_wgt, d_wut, d_wd])
    dh, d_gf = _mm([(da, wgt, "nn"), (db, wut, "nn")], f"ffn_dn_{tag}", rms_bwd=(h, _row(g_f), dh_new), tm=512,
                   after=token)
    return dh, dict(g_ffn=d_gf)


_XA = ["xa_w_q", "xa_w_k", "xa_w_v", "xa_w_o"]
_FFN = ["ffn_w_gate", "ffn_w_up", "ffn_w_down"]
GATHERS = {
    "ev_in": [("ev_w_in", 0)],
    "xa0": [("ev_w_out", 0)] + [(n, 0) for n in _XA],
    "ffn0": [(n, 0) for n in _FFN],
    "od": [("od_w_in", 0), ("od_w_out", 0)],
    "xa1": [(n, 1) for n in _XA],
    "ffn1": [(n, 1) for n in _FFN],
}
SCATTERS = {
    "ffn1": [(n, 1) for n in _FFN],
    "xa1": [(n, 1) for n in _XA],
    "od": [("od_w_in", 0), ("od_w_out", 0)],
    "ffn0": [(n, 0) for n in _FFN],
    "xa0": [(n, 0) for n in _XA],
    "ev": [("ev_w_in", 0), ("ev_w_out", 0)],
}


def _local_step(x, mem, loss_target, W, comm):
    grads = {}

    h0 = x
    (ev_w_in_t,), token = comm.weights("ev_in", None)
    n0 = _rms_fwd(h0, _row(W["g_mix"][0]), "ev_rms", after=token)
    z = _mm([(n0, ev_w_in_t, "nt")], "ev_in", tn=1280)
    token = comm.prefetch(["ffn0"], z)
    ab, ca = _conv_fwd(z, W["ev_a_conv_w"][0], W["ev_a_conv_b"], W["ev_a_ln_g"], W["ev_a_ln_b"],
                       W["ev_b_conv_w"][0], W["ev_b_conv_b"], "ev_conv", after=token)
    (ev_w_out, *xa_w0), _ = comm.weights("xa0", ab)
    h1, nq0 = _mm([(ab, ev_w_out, "nn")], "ev_out", res=h0, rms_g=_row(W["g_xattn"][0]))
    token = comm.prefetch(["od", "xa1"], nq0)
    h2, nf0, xa0 = _xattn_fwd(h1, nq0, mem, W["g_mem"][0], *xa_w0, W["g_ffn"][0], "l0", token)
    ffn_w0, _ = comm.weights("ffn0", nf0)
    token = comm.prefetch(["ffn1"], nf0)
    h3, n3, ff0 = _ffn_fwd(h2, nf0, *ffn_w0, W["g_mix"][1], "l0", token)

    (od_w_in_t, od_w_out), _ = comm.weights("od", n3)
    zp = _mm([(n3, od_w_in_t, "nt")], "od_in", tn=1024)
    D = x.shape[1]
    ws = W["od_w_s"][0].astype(BF16)
    wst = jnp.swapaxes(ws, 1, 2)
    bsb = jnp.repeat(jnp.transpose(W["od_b_s"][0]), D // C_GROUPS, axis=1)
    y_sgu = _sgu_fwd(zp, W["od_c_ln_g"], W["od_c_ln_b"], ws, bsb, "od_sgu")
    h4, nq1 = _mm([(y_sgu, od_w_out, "nn")], "od_out", res=h3, rms_g=_row(W["g_xattn"][1]))
    xa_w1, _ = comm.weights("xa1", nq1)
    h5, nf1, xa1 = _xattn_fwd(h4, nq1, mem, W["g_mem"][1], *xa_w1, W["g_ffn"][1], "l1", None)
    ffn_w1, _ = comm.weights("ffn1", nf1)
    h6, _, ff1 = _ffn_fwd(h5, nf1, *ffn_w1, None, "l1", None)

    loss_row, dh6, d_gfinal = _loss_bwd(h6, _row(W["g_final"]), loss_target, "loss")
    grads["g_final"] = d_gfinal.reshape(-1)

    dh5, g_ff1 = _ffn_bwd(dh6, ff1, W["g_ffn"][1], *ffn_w1, "l1", lambda dws: comm.grads("ffn1", dws))
    dh4, g_xa1 = _xattn_bwd(dh5, xa1, mem, W["g_xattn"][1], W["g_mem"][1], *xa_w1, "l1",
                            lambda dws: comm.grads("xa1", dws))
    dy_sgu = _mm([(dh4, od_w_out, "nt")], "od_dy", tn=1024)
    d_od_out = _mm_tn(y_sgu, dh4, "od_dwout", tn=1024)
    dzp, d_ws, d_bsb, d_clng, d_clnb = _sgu_bwd(dy_sgu, zp, W["od_c_ln_g"], W["od_c_ln_b"], ws, wst, bsb, "od_sgu_bwd")
    grads["od_w_s"] = d_ws[None]
    grads["od_b_s"] = jnp.transpose(_group_sum(d_bsb, C_GROUPS, "od_dbs"))[None]
    grads["od_c_ln_g"], grads["od_c_ln_b"] = d_clng, d_clnb
    token = comm.grads("od", [_mm_tn(dzp, n3, "od_dwin", tn=512), d_od_out])
    dh3, d_gmix1 = _mm([(dzp, od_w_in_t, "nn")], "od_dn", rms_bwd=(h3, _row(W["g_mix"][1]), dh4), tm=512, after=token)

    dh2, g_ff0 = _ffn_bwd(dh3, ff0, W["g_ffn"][0], *ffn_w0, "l0", lambda dws: comm.grads("ffn0", dws))
    dh1, g_xa0 = _xattn_bwd(dh2, xa0, mem, W["g_xattn"][0], W["g_mem"][0], *xa_w0, "l0",
                            lambda dws: comm.grads("xa0", dws))
    dab = _mm([(dh1, ev_w_out, "nt")], "ev_dab", tn=1024)
    d_ev_out = _mm_tn(ab, dh1, "ev_dwout", tn=1024)
    dca, d_lng, d_lnb, d_ba = _conv_bwd_ln(dab, ca, W["ev_a_ln_g"], W["ev_a_ln_b"], "ev_conv_bwd_ln")
    dz, d_wa, d_wb, d_bb = _conv_bwd(z, dca, dab, W["ev_a_conv_w"][0], W["ev_b_conv_w"][0], W["ev_b_conv_b"],
                                     "ev_conv_bwd")
    grads.update(ev_a_ln_g=d_lng, ev_a_ln_b=d_lnb, ev_a_conv_b=d_ba, ev_b_conv_b=d_bb,
                 ev_a_conv_w=d_wa[None], ev_b_conv_w=d_wb[None])
    token = comm.grads("ev", [_mm_tn(dz, n0, "ev_dwin", tn=512), d_ev_out])
    grad_x, d_gmix0 = _mm([(dz, ev_w_in_t, "nn")], "ev_dn", rms_bwd=(h0, _row(W["g_mix"][0]), dh1), tm=512, after=token)

    grads["g_mix"] = jnp.concatenate([d_gmix0, d_gmix1], axis=0)
    for key in ("g_xattn", "g_mem"):
        grads[key] = jnp.concatenate([g_xa0[key], g_xa1[key]], axis=0)
    grads["g_ffn"] = jnp.concatenate([g_ff0["g_ffn"], g_ff1["g_ffn"]], axis=0)
    return loss_row, grad_x, grads


class _Exchanges:
    def __init__(self, shards, dev_idx, after):
        self.shards, self.dev_idx = shards, dev_idx
        self.gathering, self.scattering = {}, {}
        self.first = _all_gather_weights(self._pack(GATHERS["ev_in"]), self._rows(GATHERS["ev_in"]), "ag_ev_in",
                                         after=after)
        self.first_token = self.prefetch(["xa0"], self.first[0])

    def _rows(self, entries):
        return [self.shards[e].shape[0] for e in entries]

    def _pack(self, entries):
        return jnp.concatenate([self.shards[e] for e in entries], axis=0)

    def prefetch(self, gathers, after):
        for name in gathers:
            rows = self._rows(GATHERS[name])
            pack = self._pack(GATHERS[name])
            lands = [lax.empty((NDEV * r, pack.shape[1]), pack.dtype) for r in rows]
            plan, n = _gather_plan(rows)
            send, recv, srcs, lands, after = _split_start([pack], lands, plan, n, after, f"ag_{name}_start")
            self.gathering[name] = (send, recv, srcs, lands, plan, rows)
        return after

    def weights(self, name, after):
        if name == "ev_in":
            return self.first, self.first_token
        send, recv, srcs, lands, plan, rows = self.gathering.pop(name)
        _, lands = _split_wait(send, recv, srcs, lands, plan, after, f"ag_{name}_wait")
        return _place_own([self.shards[e] for e in GATHERS[name]], lands, self.dev_idx, f"ag_{name}_own"), None

    def grads(self, name, dws):
        rows = self._rows(SCATTERS[name])
        lands = [lax.empty((NDEV - 1, r, d.shape[1]), d.dtype) for r, d in zip(rows, dws)]
        plan, n = _scatter_plan(rows)
        send, recv, srcs, lands, token = _split_start(dws, lands, plan, n, None, f"rs_{name}_start")
        self.scattering[name] = (send, recv, srcs, lands, plan)
        return token

    def reduced(self, after):
        out = {}
        for name, (send, recv, srcs, lands, plan) in self.scattering.items():
            srcs, lands = _split_wait(send, recv, srcs, lands, plan, after, f"rs_{name}_wait")
            for (w, i), g, got in zip(SCATTERS[name], srcs, lands):
                out[w, i] = _scatter_sum(g, got, self.dev_idx, f"rs_sum_{w}_{i}")
        return out


def kernel(x, mem, g_mix, g_xattn, g_mem, g_ffn, g_final, ev_w_in, ev_a_conv_w, ev_a_conv_b, ev_a_ln_g, ev_a_ln_b, ev_b_conv_w, ev_b_conv_b, ev_w_out, od_w_in, od_c_ln_g, od_c_ln_b, od_w_s, od_b_s, od_w_out, xa_w_q, xa_w_k, xa_w_v, xa_w_o, ffn_w_gate, ffn_w_up, ffn_w_down, loss_target, m_g_mix, m_g_xattn, m_g_mem, m_g_ffn, m_g_final, m_ev_w_in, m_ev_a_conv_w, m_ev_a_conv_b, m_ev_a_ln_g, m_ev_a_ln_b, m_ev_b_conv_w, m_ev_b_conv_b, m_ev_w_out, m_od_w_in, m_od_c_ln_g, m_od_c_ln_b, m_od_w_s, m_od_b_s, m_od_w_out, m_xa_w_q, m_xa_w_k, m_xa_w_v, m_xa_w_o, m_ffn_w_gate, m_ffn_w_up, m_ffn_w_down, v_g_mix, v_g_xattn, v_g_mem, v_g_ffn, v_g_final, v_ev_w_in, v_ev_a_conv_w, v_ev_a_conv_b, v_ev_a_ln_g, v_ev_a_ln_b, v_ev_b_conv_w, v_ev_b_conv_b, v_ev_w_out, v_od_w_in, v_od_c_ln_g, v_od_c_ln_b, v_od_w_s, v_od_b_s, v_od_w_out, v_xa_w_q, v_xa_w_k, v_xa_w_v, v_xa_w_o, v_ffn_w_gate, v_ffn_w_up, v_ffn_w_down):
    local = dict(g_mix=g_mix, g_xattn=g_xattn, g_mem=g_mem, g_ffn=g_ffn, g_final=g_final, ev_w_in=ev_w_in, ev_a_conv_w=ev_a_conv_w, ev_a_conv_b=ev_a_conv_b, ev_a_ln_g=ev_a_ln_g, ev_a_ln_b=ev_a_ln_b, ev_b_conv_w=ev_b_conv_w, ev_b_conv_b=ev_b_conv_b, ev_w_out=ev_w_out, od_w_in=od_w_in, od_c_ln_g=od_c_ln_g, od_c_ln_b=od_c_ln_b, od_w_s=od_w_s, od_b_s=od_b_s, od_w_out=od_w_out, xa_w_q=xa_w_q, xa_w_k=xa_w_k, xa_w_v=xa_w_v, xa_w_o=xa_w_o, ffn_w_gate=ffn_w_gate, ffn_w_up=ffn_w_up, ffn_w_down=ffn_w_down)
    mom = dict(g_mix=m_g_mix, g_xattn=m_g_xattn, g_mem=m_g_mem, g_ffn=m_g_ffn, g_final=m_g_final, ev_w_in=m_ev_w_in, ev_a_conv_w=m_ev_a_conv_w, ev_a_conv_b=m_ev_a_conv_b, ev_a_ln_g=m_ev_a_ln_g, ev_a_ln_b=m_ev_a_ln_b, ev_b_conv_w=m_ev_b_conv_w, ev_b_conv_b=m_ev_b_conv_b, ev_w_out=m_ev_w_out, od_w_in=m_od_w_in, od_c_ln_g=m_od_c_ln_g, od_c_ln_b=m_od_c_ln_b, od_w_s=m_od_w_s, od_b_s=m_od_b_s, od_w_out=m_od_w_out, xa_w_q=m_xa_w_q, xa_w_k=m_xa_w_k, xa_w_v=m_xa_w_v, xa_w_o=m_xa_w_o, ffn_w_gate=m_ffn_w_gate, ffn_w_up=m_ffn_w_up, ffn_w_down=m_ffn_w_down)
    vel = dict(g_mix=v_g_mix, g_xattn=v_g_xattn, g_mem=v_g_mem, g_ffn=v_g_ffn, g_final=v_g_final, ev_w_in=v_ev_w_in, ev_a_conv_w=v_ev_a_conv_w, ev_a_conv_b=v_ev_a_conv_b, ev_a_ln_g=v_ev_a_ln_g, ev_a_ln_b=v_ev_a_ln_b, ev_b_conv_w=v_ev_b_conv_w, ev_b_conv_b=v_ev_b_conv_b, ev_w_out=v_ev_w_out, od_w_in=v_od_w_in, od_c_ln_g=v_od_c_ln_g, od_c_ln_b=v_od_c_ln_b, od_w_s=v_od_w_s, od_b_s=v_od_b_s, od_w_out=v_od_w_out, xa_w_q=v_xa_w_q, xa_w_k=v_xa_w_k, xa_w_v=v_xa_w_v, xa_w_o=v_xa_w_o, ffn_w_gate=v_ffn_w_gate, ffn_w_up=v_ffn_w_up, ffn_w_down=v_ffn_w_down)
    D = x.shape[-1]
    dev = 4 * lax.axis_index("x") + 2 * lax.axis_index("y") + lax.axis_index("c")

    def comm_layout(n, a):
        return jnp.transpose(a) if _shard_axis(n) == 2 else a

    shards = {(n, i): comm_layout(n, local[n][i]).astype(BF16) for n in BIG for i in range(local[n].shape[0])}
    small_sizes = [local[n].size for n in SMALL_SHARDED]
    small_block = _pad_rows(jnp.concatenate([local[n].reshape(-1) for n in SMALL_SHARDED]), 128, 8)
    small_all = _all_gather(small_block, "ag_small")
    comm = _Exchanges(shards, jnp.reshape(dev, (1,)).astype(jnp.int32), small_all)
    small_all = small_all.reshape(NDEV, -1)

    W = {n: local[n] for n in REPLICATED}
    o0 = 0
    for n, sz in zip(SMALL_SHARDED, small_sizes):
        blocks = small_all[:, o0:o0 + sz].reshape((NDEV,) + local[n].shape)
        W[n] = _full_from_blocks(blocks, _shard_axis(n))
        o0 += sz

    loss_row, grad_x, grads = _local_step(x[0], mem[0], loss_target[0], W, comm)

    reduced = comm.reduced(grad_x)
    gsh = {n: jnp.stack([comm_layout(n, reduced[n, i]) for i in range(local[n].shape[0])]) for n in BIG}

    rest = REPLICATED + SMALL_SHARDED
    rest_full_shapes = [grads[n].shape for n in rest]
    g_rest = _pad_rows(jnp.concatenate([grads[n].astype(F32).reshape(-1) for n in rest]), D, 8)
    g_rest = _sum_slots(_all_gather(g_rest, "ag_small_grads"), "sum_small_grads").reshape(-1)
    o0 = 0
    for n, shp in zip(rest, rest_full_shapes):
        sz = 1
        for s in shp:
            sz *= s
        full = g_rest[o0:o0 + sz].reshape(shp)
        o0 += sz
        if n in SMALL_SHARDED:
            full = lax.dynamic_index_in_dim(_blocks_from_full(full, _shard_axis(n)), dev, 0, keepdims=False)
        gsh[n] = full.reshape(local[n].shape)

    delta, new_m, new_v = {}, {}, {}
    for n in WEIGHTS:
        delta[n], new_m[n], new_v[n] = _adamw(local[n], gsh[n], mom[n], vel[n], f"adamw_{n}")

    loss = lax.psum(loss_row[0, 0], ("x", "y", "c"))
    return (loss, grad_x[None], *[gsh[n] for n in WEIGHTS], *[delta[n] for n in WEIGHTS],
            *[new_m[n] for n in WEIGHTS], *[new_v[n] for n in WEIGHTS])
```

```python
import jax
import jax.numpy as jnp
from jax import lax
from jax.experimental import pallas as pl
from jax.experimental.pallas import tpu as pltpu

F32, BF16 = jnp.float32, jnp.bfloat16
NDEV = 8
RMS_EPS = 1e-6
LN_EPS = 1e-5
CHUNK = 128
C_GROUPS = 8
XA_HEADS = 4
ADAM_LR, ADAM_B1, ADAM_B2, ADAM_EPS, ADAM_WD, ADAM_STEP = 0.001, 0.9, 0.999, 1e-08, 0.01, 10
HALO = 16
ROW_CHUNK = 32
V7X_VMEM_LIMIT = 56 * 1024 * 1024
MESH = pl.DeviceIdType.MESH

TS_ROW = 512
TS_MM = 1024
TN_MM = 1408
TS_FFN = 512
MM_ROW_CHUNK = 256
TS_CONV = 512
TS_SGU = 512
TS_ATTN = 512


def _cp(*sem):
    return pltpu.CompilerParams(dimension_semantics=sem, vmem_limit_bytes=V7X_VMEM_LIMIT)


def _pick(n, pref, align):
    for t in range(min(n, pref), 0, -1):
        if n % t == 0 and (t % align == 0 or t == n):
            return t
    return n


def _sigmoid(x):
    return 0.5 * jnp.tanh(0.5 * x) + 0.5


def _dot(a, b):
    return jnp.dot(a, b, preferred_element_type=F32)


def _dot_nt(a, b):
    return lax.dot_general(a, b, (((1,), (1,)), ((), ())), preferred_element_type=F32)


def _dot_tn(a, b):
    return lax.dot_general(a, b, (((0,), (0,)), ((), ())), preferred_element_type=F32)


_ANY = pl.BlockSpec(memory_space=pl.ANY)
_RESIDENT = pl.Buffered(1)


def _after(after):
    return ([], []) if after is None else ([_ANY], [after])


def _rms_fwd(h, g, name, after=None):
    S, D = h.shape
    ts = _pick(S, TS_MM, 16)
    after_specs, after_ops = _after(after)

    def body(h_ref, g_ref, *rest):
        o_ref = rest[-1]
        x = h_ref[...]
        r = lax.rsqrt(jnp.mean(x * x, axis=-1, keepdims=True) + RMS_EPS)
        o_ref[...] = ((x * r) * g_ref[...]).astype(o_ref.dtype)

    return pl.pallas_call(
        body, grid=(S // ts,),
        in_specs=[pl.BlockSpec((ts, D), lambda i: (i, 0)), pl.BlockSpec((1, D), lambda i: (0, 0))] + after_specs,
        out_specs=pl.BlockSpec((ts, D), lambda i: (i, 0)),
        out_shape=jax.ShapeDtypeStruct((S, D), BF16), compiler_params=_cp("parallel"), name=name)(h, g, *after_ops)


def _rms_bwd(dn, h, g, dres, name):
    S, D = h.shape
    ts = _pick(S, TS_ROW, 8)
    has_res = dres is not None

    def body(*refs):
        if has_res:
            dn_ref, h_ref, g_ref, dres_ref, dh_ref, dg_ref = refs
        else:
            dn_ref, h_ref, g_ref, dh_ref, dg_ref = refs
        x = h_ref[...]
        dn_ = dn_ref[...].astype(F32)
        r = lax.rsqrt(jnp.mean(x * x, axis=-1, keepdims=True) + RMS_EPS)
        xr = x * r

        @pl.when(pl.program_id(0) == 0)
        def _():
            dg_ref[...] = jnp.zeros_like(dg_ref)

        dg_ref[...] += jnp.sum(dn_ * xr, axis=0, keepdims=True)
        u = dn_ * g_ref[...]
        dh = r * u - xr * (r * jnp.mean(u * xr, axis=-1, keepdims=True))
        if has_res:
            dh = dh + dres_ref[...]
        dh_ref[...] = dh

    tile = pl.BlockSpec((ts, D), lambda i: (i, 0))
    vec = pl.BlockSpec((1, D), lambda i: (0, 0))
    ins = [dn, h, g] + ([dres] if has_res else [])
    return pl.pallas_call(
        body, grid=(S // ts,),
        in_specs=[tile, tile, vec] + ([tile] if has_res else []),
        out_specs=[tile, vec],
        out_shape=[jax.ShapeDtypeStruct((S, D), F32), jax.ShapeDtypeStruct((1, D), F32)],
        compiler_params=_cp("arbitrary"), name=name)(*ins)


def _loss_bwd(h, g, target, name):
    S, D = h.shape
    ts = _pick(S, TS_ROW, 8)

    def body(h_ref, g_ref, t_ref, loss_ref, dh_ref, dg_ref):
        x = h_ref[...]
        r = lax.rsqrt(jnp.mean(x * x, axis=-1, keepdims=True) + RMS_EPS)
        xr = x * r
        gg = g_ref[...]
        e = xr * gg - t_ref[...]

        @pl.when(pl.program_id(0) == 0)
        def _():
            dg_ref[...] = jnp.zeros_like(dg_ref)
            loss_ref[...] = jnp.zeros_like(loss_ref)

        tile_loss = jnp.sum(jnp.sum(e * e, axis=0, keepdims=True), axis=1, keepdims=True) * (0.5 / D)
        loss_ref[...] += jnp.broadcast_to(tile_loss, loss_ref.shape)
        dy = e * (1.0 / D)
        dg_ref[...] += jnp.sum(dy * xr, axis=0, keepdims=True)
        u = dy * gg
        dh_ref[...] = r * u - xr * (r * jnp.mean(u * xr, axis=-1, keepdims=True))

    tile = pl.BlockSpec((ts, D), lambda i: (i, 0))
    vec = pl.BlockSpec((1, D), lambda i: (0, 0))
    return pl.pallas_call(
        body, grid=(S // ts,),
        in_specs=[tile, vec, tile],
        out_specs=[pl.BlockSpec((1, 128), lambda i: (0, 0)), tile, vec],
        out_shape=[jax.ShapeDtypeStruct((1, 128), F32), jax.ShapeDtypeStruct((S, D), F32),
                   jax.ShapeDtypeStruct((1, D), F32)],
        compiler_params=_cp("arbitrary"), name=name)(h, g, target)


def _mm(pairs, name, out_dtype=F32, res=None, rms_g=None, rms_bwd=None, tm=None, tn=None, after=None):
    M = pairs[0][0].shape[0]
    N = pairs[0][1].shape[1 if pairs[0][2] == "nn" else 0]
    whole_rows = rms_g is not None or rms_bwd is not None
    tm = _pick(M, tm or TS_MM, 16)
    tn = N if whole_rows else _pick(N, tn or TN_MM, 128)
    npair = len(pairs)
    modes = [p[2] for p in pairs]
    after_specs, after_ops = _after(after)

    rc = MM_ROW_CHUNK if whole_rows and tm % MM_ROW_CHUNK == 0 else tm

    def body(*refs):
        rest = refs[2 * npair + len(after_ops):]
        res_ref = None
        if res is not None:
            res_ref, rest = rest[0], rest[1:]
        if rms_bwd is not None:
            dg_ref = rest[4]

            @pl.when(pl.program_id(0) == 0)
            def _():
                dg_ref[...] = jnp.zeros_like(dg_ref)

        for r0 in range(0, tm, rc):
            rows = pl.ds(r0, rc)
            acc = None
            for p in range(npair):
                a_ = refs[2 * p][rows, :].astype(BF16)
                d = _dot(a_, refs[2 * p + 1][...]) if modes[p] == "nn" else _dot_nt(a_, refs[2 * p + 1][...])
                acc = d if acc is None else acc + d
            if res_ref is not None:
                acc = acc + res_ref[rows, :]
            if rms_bwd is not None:
                h_ref, g_ref, dres_ref, dh_ref, _ = rest
                x = h_ref[rows, :]
                r = lax.rsqrt(jnp.mean(x * x, axis=-1, keepdims=True) + RMS_EPS)
                xr = x * r
                dg_ref[...] += jnp.sum(acc * xr, axis=0, keepdims=True)
                u = acc * g_ref[...]
                dh_ref[rows, :] = r * u - xr * (r * jnp.mean(u * xr, axis=-1, keepdims=True)) + dres_ref[rows, :]
            elif rms_g is not None:
                g_ref, o_ref, n_ref = rest
                o_ref[rows, :] = acc
                r = lax.rsqrt(jnp.mean(acc * acc, axis=-1, keepdims=True) + RMS_EPS)
                n_ref[rows, :] = ((acc * r) * g_ref[...]).astype(BF16)
            else:
                rest[0][rows, :] = acc.astype(rest[0].dtype)

    in_specs, ins = [], []
    for a, w, mode in pairs:
        K = a.shape[1]
        in_specs.append(pl.BlockSpec((tm, K), lambda i, j: (i, 0)))
        once = _RESIDENT if tn == N else None
        in_specs.append(pl.BlockSpec((K, tn), lambda i, j: (0, j), pipeline_mode=once) if mode == "nn"
                        else pl.BlockSpec((tn, K), lambda i, j: (j, 0), pipeline_mode=once))
        ins += [a, w]
    in_specs += after_specs
    ins += after_ops
    tile = pl.BlockSpec((tm, tn), lambda i, j: (i, j))
    vec = pl.BlockSpec((1, tn), lambda i, j: (0, j))
    if res is not None:
        in_specs.append(tile)
        ins.append(res)
    sem = ("parallel", "parallel")
    if rms_bwd is not None:
        in_specs += [tile, vec, tile]
        ins += list(rms_bwd)
        out_specs = [tile, vec]
        out_shape = [jax.ShapeDtypeStruct((M, N), F32), jax.ShapeDtypeStruct((1, N), F32)]
        sem = ("arbitrary", "arbitrary")
    elif rms_g is not None:
        in_specs.append(vec)
        ins.append(rms_g)
        out_specs = [tile, tile]
        out_shape = [jax.ShapeDtypeStruct((M, N), F32), jax.ShapeDtypeStruct((M, N), BF16)]
    else:
        out_specs = tile
        out_shape = jax.ShapeDtypeStruct((M, N), out_dtype)
    return pl.pallas_call(
        body, grid=(M // tm, N // tn), in_specs=in_specs, out_specs=out_specs, out_shape=out_shape,
        compiler_params=_cp(*sem), name=name)(*ins)


def _mm_tn(a, b, name, ts=None, tn=None):
    S, K = a.shape
    N = b.shape[1]
    ts = _pick(S, ts or TS_MM, 16)
    tn = _pick(N, tn or TN_MM, 128)
    nsteps = S // ts

    def body(a_ref, b_ref, o_ref, acc_ref):
        s = pl.program_id(1)

        @pl.when(s == 0)
        def _():
            acc_ref[...] = jnp.zeros_like(acc_ref)

        acc_ref[...] += _dot_tn(a_ref[...].astype(BF16), b_ref[...].astype(BF16))

        @pl.when(s == nsteps - 1)
        def _():
            o_ref[...] = acc_ref[...].astype(o_ref.dtype)

    return pl.pallas_call(
        body, grid=(N // tn, nsteps),
        in_specs=[pl.BlockSpec((ts, K), lambda j, s: (s, 0)), pl.BlockSpec((ts, tn), lambda j, s: (s, j))],
        out_specs=pl.BlockSpec((K, tn), lambda j, s: (0, j)), out_shape=jax.ShapeDtypeStruct((K, N), BF16),
        scratch_shapes=[pltpu.VMEM((K, tn), F32)],
        compiler_params=_cp("parallel", "arbitrary"), name=name)(a, b)


def _col_chunk(n):
    return 256 if n % 256 == 0 else 128


def _ffn_up(n, wgt, wut, name, after=None):
    S, D = n.shape
    F = wgt.shape[0]
    tm = _pick(S, TS_FFN, 16)
    ce = _col_chunk(F)
    after_specs, after_ops = _after(after)

    def body(n_ref, wg_ref, wu_ref, *rest):
        a_ref, b_ref, hid_ref = rest[-3:]
        x = n_ref[...]
        for c0 in range(0, F, ce):
            a = _dot_nt(x, wg_ref[c0:c0 + ce, :])
            b = _dot_nt(x, wu_ref[c0:c0 + ce, :])
            a_ref[:, c0:c0 + ce] = a.astype(BF16)
            b_ref[:, c0:c0 + ce] = b.astype(BF16)
            hid_ref[:, c0:c0 + ce] = (a * _sigmoid(a) * b).astype(BF16)

    wspec = pl.BlockSpec((F, D), lambda i: (0, 0), pipeline_mode=_RESIDENT)
    ospec = pl.BlockSpec((tm, F), lambda i: (i, 0))
    osh = jax.ShapeDtypeStruct((S, F), BF16)
    return pl.pallas_call(
        body, grid=(S // tm,),
        in_specs=[pl.BlockSpec((tm, D), lambda i: (i, 0)), wspec, wspec] + after_specs,
        out_specs=[ospec, ospec, ospec], out_shape=[osh, osh, osh],
        compiler_params=_cp("parallel"), name=name)(n, wgt, wut, *after_ops)


def _ffn_dhid(dh, wd, a, b, name):
    S, D = dh.shape
    F = wd.shape[0]
    tm = _pick(S, TS_FFN, 16)
    ce = _col_chunk(F)

    def body(dh_ref, wd_ref, a_ref, b_ref, da_ref, db_ref):
        x = dh_ref[...].astype(BF16)
        for c0 in range(0, F, ce):
            g = _dot_nt(x, wd_ref[c0:c0 + ce, :])
            a_ = a_ref[:, c0:c0 + ce].astype(F32)
            b_ = b_ref[:, c0:c0 + ce].astype(F32)
            sg = _sigmoid(a_)
            da_ref[:, c0:c0 + ce] = (g * b_ * (sg * (1.0 + a_ * (1.0 - sg)))).astype(BF16)
            db_ref[:, c0:c0 + ce] = (g * (a_ * sg)).astype(BF16)

    tile = pl.BlockSpec((tm, F), lambda i: (i, 0))
    osh = jax.ShapeDtypeStruct((S, F), BF16)
    return pl.pallas_call(
        body, grid=(S // tm,),
        in_specs=[pl.BlockSpec((tm, D), lambda i: (i, 0)),
                  pl.BlockSpec((F, D), lambda i: (0, 0), pipeline_mode=_RESIDENT), tile, tile],
        out_specs=[tile, tile], out_shape=[osh, osh],
        compiler_params=_cp("parallel"), name=name)(dh, wd, a, b)


def _softmax_rows(s):
    m = jnp.max(s, axis=-1, keepdims=True)
    p = jnp.exp(s - m)
    return p / jnp.sum(p, axis=-1, keepdims=True)


def _attn_fwd(q, k, v, name):
    S, D = q.shape
    M = k.shape[0]
    hd = D // XA_HEADS
    scale = hd ** -0.5
    ts = _pick(S, TS_ATTN, 16)

    def body(q_ref, k_ref, v_ref, o_ref):
        for h in range(XA_HEADS):
            sl = slice(h * hd, (h + 1) * hd)
            p = _softmax_rows(_dot_nt(q_ref[:, sl], k_ref[:, sl]) * scale)
            o_ref[:, sl] = _dot(p.astype(BF16), v_ref[:, sl]).astype(BF16)

    tile = pl.BlockSpec((ts, D), lambda i: (i, 0))
    memspec = pl.BlockSpec((M, D), lambda i: (0, 0))
    return pl.pallas_call(
        body, grid=(S // ts,), in_specs=[tile, memspec, memspec], out_specs=tile,
        out_shape=jax.ShapeDtypeStruct((S, D), BF16), compiler_params=_cp("parallel"), name=name)(q, k, v)


def _attn_bwd(q, k, v, do, name):
    S, D = q.shape
    M = k.shape[0]
    hd = D // XA_HEADS
    scale = hd ** -0.5
    ts = _pick(S, TS_ATTN, 16)

    def body(q_ref, k_ref, v_ref, do_ref, dq_ref, dk_ref, dv_ref):
        @pl.when(pl.program_id(0) == 0)
        def _():
            dk_ref[...] = jnp.zeros_like(dk_ref)
            dv_ref[...] = jnp.zeros_like(dv_ref)

        for h in range(XA_HEADS):
            sl = slice(h * hd, (h + 1) * hd)
            qh, kh, vh, doh = q_ref[:, sl], k_ref[:, sl], v_ref[:, sl], do_ref[:, sl]
            p = _softmax_rows(_dot_nt(qh, kh) * scale)
            dp = _dot_nt(doh, vh)
            dv_ref[:, sl] += _dot_tn(p.astype(BF16), doh)
            delta = jnp.sum(dp * p, axis=-1, keepdims=True)
            ds = (p * (dp - delta) * scale).astype(BF16)
            dq_ref[:, sl] = _dot(ds, kh).astype(BF16)
            dk_ref[:, sl] += _dot_tn(ds, qh)

    tile = pl.BlockSpec((ts, D), lambda i: (i, 0))
    memspec = pl.BlockSpec((M, D), lambda i: (0, 0))
    return pl.pallas_call(
        body, grid=(S // ts,), in_specs=[tile, memspec, memspec, tile], out_specs=[tile, memspec, memspec],
        out_shape=[jax.ShapeDtypeStruct((S, D), BF16), jax.ShapeDtypeStruct((M, D), F32),
                   jax.ShapeDtypeStruct((M, D), F32)],
        compiler_params=_cp("arbitrary"), name=name)(q, k, v, do)


def _halo_specs(ts, width, col):
    per = ts // HALO

    def prev(i):
        return (jnp.maximum(i * per - 1, 0), col)

    def nxt(i, n_tiles):
        return (jnp.minimum((i + 1) * per, n_tiles * per - 1), col)

    return prev, nxt


def _fill_ext(ext_ref, prev_val, main_val, next_val, first, last, ts):
    ext_ref[pl.ds(0, HALO), :] = jnp.where(first, 0.0, prev_val)
    ext_ref[pl.ds(HALO, ts), :] = main_val
    ext_ref[pl.ds(HALO + ts, HALO), :] = jnp.where(last, 0.0, next_val)


SUBLANES = 8


def _fill_shifted(sh_ref, ts):
    n = ts + 2 * HALO - SUBLANES
    for s in range(1, SUBLANES):
        sh_ref[s, pl.ds(0, n), :] = sh_ref[0, pl.ds(s, n), :]


def _tap(sh_ref, r0, offset, rc):
    q, s = divmod(offset, SUBLANES)
    return sh_ref[s, pl.ds(pl.multiple_of(r0 + SUBLANES * q, SUBLANES), rc), :]


def _conv_fwd(z, wa, ba, lng, lnb, wb, bb, name, after=None):
    S = z.shape[0]
    C = z.shape[1] // 5
    KA, KB = wa.shape[0], wb.shape[0]
    pa, pb = KA // 2, KB // 2
    assert pa <= HALO and pb <= HALO
    ts = _pick(S, TS_CONV, ROW_CHUNK)
    nt = S // ts
    rc = ROW_CHUNK
    prev, nxt = _halo_specs(ts, 5 * C, 0)
    after_specs, after_ops = _after(after)

    def body(*refs):
        compute(*refs[:9], *refs[9 + len(after_ops):])

    def compute(z_ref, zp_ref, zn_ref, wa_ref, ba_ref, lng_ref, lnb_ref, wb_ref, bb_ref, ab_ref, ca_ref,
                ga_sh, tb_ext, win_b):
        i = pl.program_id(0)
        first, last = i == 0, i == nt - 1

        def glu(r):
            return r[:, 0:C] * _sigmoid(r[:, C:2 * C])

        def gcb(r):
            return r[:, 4 * C:5 * C] * r[:, 2 * C:3 * C]

        _fill_ext(ga_sh.at[0], glu(zp_ref), glu(z_ref), glu(zn_ref), first, last, ts)
        _fill_shifted(ga_sh, ts)
        _fill_ext(tb_ext, gcb(zp_ref), gcb(z_ref), gcb(zn_ref), first, last, ts)

        def chunk(c, carry):
            r0 = pl.multiple_of(c * rc, rc)
            win_b[...] = tb_ext[pl.ds(r0, rc + 2 * HALO), :]
            acc = jnp.zeros((rc, C), F32)
            for k in range(KA):
                acc = acc + wa_ref[k:k + 1, :] * _tap(ga_sh, r0, HALO - pa + k, rc)
            ca = acc + ba_ref[...]
            ca_ref[pl.ds(r0, rc), :] = ca
            mu = jnp.mean(ca, axis=-1, keepdims=True)
            xc = ca - mu
            var = jnp.mean(xc * xc, axis=-1, keepdims=True)
            ln = xc * lax.rsqrt(var + LN_EPS) * lng_ref[...] + lnb_ref[...]
            ab_ref[pl.ds(r0, rc), 0:C] = (ln * _sigmoid(ln)).astype(BF16)
            cb = jnp.zeros((rc, C), F32) + bb_ref[...]
            for k in range(KB):
                cb = cb + wb_ref[k:k + 1, :] * win_b[pl.ds(HALO - pb + k, rc), :]
            ab_ref[pl.ds(r0, rc), C:2 * C] = (z_ref[pl.ds(r0, rc), 3 * C:4 * C] * cb).astype(BF16)
            return carry

        lax.fori_loop(0, ts // rc, chunk, 0)

    zspec = pl.BlockSpec((ts, 5 * C), lambda i: (i, 0))
    zprev = pl.BlockSpec((HALO, 5 * C), prev)
    znext = pl.BlockSpec((HALO, 5 * C), lambda i: nxt(i, nt))

    def full(a):
        return pl.BlockSpec(a.shape, lambda i: (0, 0))

    return pl.pallas_call(
        body, grid=(nt,),
        in_specs=[zspec, zprev, znext, full(wa), full(ba), full(lng), full(lnb), full(wb), full(bb)] + after_specs,
        out_specs=[pl.BlockSpec((ts, 2 * C), lambda i: (i, 0)), pl.BlockSpec((ts, C), lambda i: (i, 0))],
        out_shape=[jax.ShapeDtypeStruct((S, 2 * C), BF16), jax.ShapeDtypeStruct((S, C), F32)],
        scratch_shapes=[pltpu.VMEM((SUBLANES, ts + 2 * HALO, C), F32), pltpu.VMEM((ts + 2 * HALO, C), F32),
                        pltpu.VMEM((rc + 2 * HALO, C), F32)],
        compiler_params=_cp("parallel"), name=name)(z, z, z, wa, ba, lng, lnb, wb, bb, *after_ops)


def _conv_bwd_ln(dab, ca, lng, lnb, name):
    S, C = ca.shape
    ts = _pick(S, TS_ROW, 8)

    def body(da_ref, ca_ref, lng_ref, lnb_ref, dca_ref, dg_ref, db_ref, dbias_ref):
        @pl.when(pl.program_id(0) == 0)
        def _():
            dg_ref[...] = jnp.zeros_like(dg_ref)
            db_ref[...] = jnp.zeros_like(db_ref)
            dbias_ref[...] = jnp.zeros_like(dbias_ref)

        ca_ = ca_ref[...]
        mu = jnp.mean(ca_, axis=-1, keepdims=True)
        xc = ca_ - mu
        rstd = lax.rsqrt(jnp.mean(xc * xc, axis=-1, keepdims=True) + LN_EPS)
        xh = xc * rstd
        ln = xh * lng_ref[...] + lnb_ref[...]
        sg = _sigmoid(ln)
        dln = da_ref[...].astype(F32) * (sg * (1.0 + ln * (1.0 - sg)))
        dg_ref[...] += jnp.sum(dln * xh, axis=0, keepdims=True)
        db_ref[...] += jnp.sum(dln, axis=0, keepdims=True)
        dxh = dln * lng_ref[...]
        dca = rstd * (dxh - jnp.mean(dxh, axis=-1, keepdims=True) - xh * jnp.mean(dxh * xh, axis=-1, keepdims=True))
        dca_ref[...] = dca
        dbias_ref[...] += jnp.sum(dca, axis=0, keepdims=True)

    tile = pl.BlockSpec((ts, C), lambda i: (i, 0))
    vec = pl.BlockSpec((1, C), lambda i: (0, 0))
    vsh = jax.ShapeDtypeStruct((1, C), F32)
    return pl.pallas_call(
        body, grid=(S // ts,), in_specs=[tile, tile, vec, vec], out_specs=[tile, vec, vec, vec],
        out_shape=[jax.ShapeDtypeStruct((S, C), F32), vsh, vsh, vsh],
        compiler_params=_cp("arbitrary"), name=name)(dab, ca, lng, lnb)


def _conv_bwd(z, dca, dab, wa, wb, bb, name):
    S = z.shape[0]
    C = z.shape[1] // 5
    KA, KB = wa.shape[0], wb.shape[0]
    pa, pb = KA // 2, KB // 2
    ts = _pick(S, TS_CONV, ROW_CHUNK)
    nt = S // ts
    rc = ROW_CHUNK
    prev0, nxt0 = _halo_specs(ts, C, 0)
    prev1, nxt1 = _halo_specs(ts, C, 1)

    def body(z_ref, zp_ref, zn_ref, dca_ref, dcap_ref, dcan_ref, db_ref, dbp_ref, dbn_ref, wa_ref, wb_ref, bb_ref,
             dz_ref, dwa_ref, dwb_ref, dbb_ref,
             ga_sh, dca_sh, tb_ext, dcb_ext, win_tb, win_dcb, acc_a, acc_b, acc_bias):
        i = pl.program_id(0)
        first, last = i == 0, i == nt - 1

        @pl.when(first)
        def _():
            acc_a[...] = jnp.zeros_like(acc_a)
            acc_b[...] = jnp.zeros_like(acc_b)
            acc_bias[...] = jnp.zeros_like(acc_bias)

        def glu(r):
            return r[:, 0:C] * _sigmoid(r[:, C:2 * C])

        def gcb(r):
            return r[:, 4 * C:5 * C] * r[:, 2 * C:3 * C]

        def dcb(d, r):
            return d[...].astype(F32) * r[:, 3 * C:4 * C]

        _fill_ext(ga_sh.at[0], glu(zp_ref), glu(z_ref), glu(zn_ref), first, last, ts)
        _fill_shifted(ga_sh, ts)
        _fill_ext(dca_sh.at[0], dcap_ref[...], dca_ref[...], dcan_ref[...], first, last, ts)
        _fill_shifted(dca_sh, ts)
        _fill_ext(tb_ext, gcb(zp_ref), gcb(z_ref), gcb(zn_ref), first, last, ts)
        _fill_ext(dcb_ext, dcb(dbp_ref, zp_ref), dcb(db_ref, z_ref), dcb(dbn_ref, zn_ref), first, last, ts)

        def fold(x):
            return jnp.sum(x.reshape(rc // 8, 8, C), axis=0)

        def chunk(c, carry):
            r0 = pl.multiple_of(c * rc, rc)
            win_tb[...] = tb_ext[pl.ds(r0, rc + 2 * HALO), :]
            win_dcb[...] = dcb_ext[pl.ds(r0, rc + 2 * HALO), :]
            dca_c = _tap(dca_sh, r0, HALO, rc)
            dglu = jnp.zeros((rc, C), F32)
            for k in range(KA):
                dglu = dglu + wa_ref[k:k + 1, :] * _tap(dca_sh, r0, HALO + pa - k, rc)
                acc_a[k] += fold(dca_c * _tap(ga_sh, r0, HALO - pa + k, rc))
            val = z_ref[pl.ds(r0, rc), 0:C]
            sg = _sigmoid(z_ref[pl.ds(r0, rc), C:2 * C])
            dz_ref[pl.ds(r0, rc), 0:C] = (dglu * sg).astype(BF16)
            dz_ref[pl.ds(r0, rc), C:2 * C] = (dglu * val * sg * (1.0 - sg)).astype(BF16)
            dcb_c = win_dcb[pl.ds(HALO, rc), :]
            cb = jnp.zeros((rc, C), F32) + bb_ref[...]
            dt = jnp.zeros((rc, C), F32)
            for k in range(KB):
                tb_k = win_tb[pl.ds(HALO - pb + k, rc), :]
                cb = cb + wb_ref[k:k + 1, :] * tb_k
                dt = dt + wb_ref[k:k + 1, :] * win_dcb[pl.ds(HALO + pb - k, rc), :]
                acc_b[k] += fold(dcb_c * tb_k)
            acc_bias[...] += fold(dcb_c)
            db_c = db_ref[pl.ds(r0, rc), :].astype(F32)
            dz_ref[pl.ds(r0, rc), 2 * C:3 * C] = (dt * z_ref[pl.ds(r0, rc), 4 * C:5 * C]).astype(BF16)
            dz_ref[pl.ds(r0, rc), 3 * C:4 * C] = (db_c * cb).astype(BF16)
            dz_ref[pl.ds(r0, rc), 4 * C:5 * C] = (dt * z_ref[pl.ds(r0, rc), 2 * C:3 * C]).astype(BF16)
            return carry

        lax.fori_loop(0, ts // rc, chunk, 0)

        @pl.when(last)
        def _():
            dwa_ref[...] = jnp.sum(acc_a[...], axis=1)
            dwb_ref[...] = jnp.sum(acc_b[...], axis=1)
            dbb_ref[...] = jnp.sum(acc_bias[...], axis=0, keepdims=True)

    zspec = pl.BlockSpec((ts, 5 * C), lambda i: (i, 0))
    zprev = pl.BlockSpec((HALO, 5 * C), prev0)
    znext = pl.BlockSpec((HALO, 5 * C), lambda i: nxt0(i, nt))
    dspec = pl.BlockSpec((ts, C), lambda i: (i, 0))
    dprev = pl.BlockSpec((HALO, C), prev0)
    dnext = pl.BlockSpec((HALO, C), lambda i: nxt0(i, nt))
    bspec = pl.BlockSpec((ts, C), lambda i: (i, 1))
    bprev = pl.BlockSpec((HALO, C), prev1)
    bnext = pl.BlockSpec((HALO, C), lambda i: nxt1(i, nt))

    def full(shape):
        return pl.BlockSpec(shape, lambda i: (0,) * len(shape))

    ext = pltpu.VMEM((ts + 2 * HALO, C), F32)
    shifted = pltpu.VMEM((SUBLANES, ts + 2 * HALO, C), F32)
    win = pltpu.VMEM((rc + 2 * HALO, C), F32)
    return pl.pallas_call(
        body, grid=(nt,),
        in_specs=[zspec, zprev, znext, dspec, dprev, dnext, bspec, bprev, bnext,
                  full(wa.shape), full(wb.shape), full(bb.shape)],
        out_specs=[pl.BlockSpec((ts, 5 * C), lambda i: (i, 0)), full((KA, C)), full((KB, C)), full((1, C))],
        out_shape=[jax.ShapeDtypeStruct((S, 5 * C), BF16), jax.ShapeDtypeStruct((KA, C), F32),
                   jax.ShapeDtypeStruct((KB, C), F32), jax.ShapeDtypeStruct((1, C), F32)],
        scratch_shapes=[shifted, shifted, ext, ext, win, win,
                        pltpu.VMEM((KA, 8, C), F32), pltpu.VMEM((KB, 8, C), F32), pltpu.VMEM((8, C), F32)],
        compiler_params=_cp("arbitrary"), name=name)(z, z, z, dca, dca, dca, dab, dab, dab, wa, wb, bb)


_GELU_C = 0.7978845608028654
_GELU_A = 0.044715


def _gelu(x):
    return 0.5 * x * (1.0 + jnp.tanh(_GELU_C * (x + _GELU_A * (x * x * x))))


def _gelu_and_grad(x):
    t = jnp.tanh(_GELU_C * (x + _GELU_A * (x * x * x)))
    hx = 0.5 * x
    return hx * (1.0 + t), 0.5 * (1.0 + t) + hx * (1.0 - t * t) * (_GELU_C * (1.0 + 3.0 * _GELU_A * x * x))


def _sgu_fwd(zp, lng, lnb, ws, bsb, name):
    S = zp.shape[0]
    D = zp.shape[1] // 2
    G = ws.shape[0]
    gd = D // G
    ts = _pick(S, TS_SGU, CHUNK)
    ncs = ts // CHUNK

    def body(zp_ref, lng_ref, lnb_ref, ws_ref, bsb_ref, y_ref, vb_ref):
        v = _gelu(zp_ref[:, D:2 * D])
        mu = jnp.mean(v, axis=-1, keepdims=True)
        xc = v - mu
        rstd = lax.rsqrt(jnp.mean(xc * xc, axis=-1, keepdims=True) + LN_EPS)
        vb_ref[...] = (xc * rstd * lng_ref[...] + lnb_ref[...]).astype(BF16)
        for c in range(ncs):
            rows = slice(c * CHUNK, (c + 1) * CHUNK)
            for g in range(G):
                cols = slice(g * gd, (g + 1) * gd)
                sv = _dot(ws_ref[g], vb_ref[rows, cols]) + bsb_ref[:, cols]
                y_ref[rows, cols] = (_gelu(zp_ref[rows, cols]) * sv).astype(BF16)

    def full(a):
        return pl.BlockSpec(a.shape, lambda i: (0,) * a.ndim)

    return pl.pallas_call(
        body, grid=(S // ts,),
        in_specs=[pl.BlockSpec((ts, 2 * D), lambda i: (i, 0)), full(lng), full(lnb), full(ws), full(bsb)],
        out_specs=pl.BlockSpec((ts, D), lambda i: (i, 0)), out_shape=jax.ShapeDtypeStruct((S, D), BF16),
        scratch_shapes=[pltpu.VMEM((ts, D), BF16)],
        compiler_params=_cp("parallel"), name=name)(zp, lng, lnb, ws, bsb)


def _sgu_bwd(dy, zp, lng, lnb, ws, wst, bsb, name):
    S = zp.shape[0]
    D = zp.shape[1] // 2
    G = ws.shape[0]
    gd = D // G
    ts = _pick(S, TS_SGU, CHUNK)
    ncs = ts // CHUNK

    def body(dy_ref, zp_ref, lng_ref, lnb_ref, ws_ref, wst_ref, bsb_ref,
             dzp_ref, dws_ref, dbs_ref, dg_ref, db_ref, vb_ref, dvln_ref, acc_bs):
        i = pl.program_id(0)

        @pl.when(i == 0)
        def _():
            dws_ref[...] = jnp.zeros_like(dws_ref)
            acc_bs[...] = jnp.zeros_like(acc_bs)
            dg_ref[...] = jnp.zeros_like(dg_ref)
            db_ref[...] = jnp.zeros_like(db_ref)

        v, dv_dz = _gelu_and_grad(zp_ref[:, D:2 * D])
        mu = jnp.mean(v, axis=-1, keepdims=True)
        xc = v - mu
        rstd = lax.rsqrt(jnp.mean(xc * xc, axis=-1, keepdims=True) + LN_EPS)
        xh = xc * rstd
        vb_ref[...] = (xh * lng_ref[...] + lnb_ref[...]).astype(BF16)
        for c in range(ncs):
            rows = slice(c * CHUNK, (c + 1) * CHUNK)
            for g in range(G):
                cols = slice(g * gd, (g + 1) * gd)
                u, du_dz = _gelu_and_grad(zp_ref[rows, cols])
                dy_ = dy_ref[rows, cols].astype(F32)
                sv = _dot(ws_ref[g], vb_ref[rows, cols]) + bsb_ref[:, cols]
                dzp_ref[rows, cols] = (dy_ * sv * du_dz).astype(BF16)
                dsv = dy_ * u
                acc_bs[:, cols] += dsv
                dsvb = dsv.astype(BF16)
                dws_ref[g] += _dot_nt(dsvb, vb_ref[rows, cols])
                dvln_ref[rows, cols] = _dot(wst_ref[g], dsvb)
        dvln = dvln_ref[...]
        dg_ref[...] += jnp.sum(dvln * xh, axis=0, keepdims=True)
        db_ref[...] += jnp.sum(dvln, axis=0, keepdims=True)
        dxh = dvln * lng_ref[...]
        dv = rstd * (dxh - jnp.mean(dxh, axis=-1, keepdims=True) - xh * jnp.mean(dxh * xh, axis=-1, keepdims=True))
        dzp_ref[:, D:2 * D] = (dv * dv_dz).astype(BF16)

        @pl.when(i == pl.num_programs(0) - 1)
        def _():
            dbs_ref[...] = acc_bs[...]

    def full(shape):
        return pl.BlockSpec(shape, lambda i: (0,) * len(shape))

    return pl.pallas_call(
        body, grid=(S // ts,),
        in_specs=[pl.BlockSpec((ts, D), lambda i: (i, 0)), pl.BlockSpec((ts, 2 * D), lambda i: (i, 0)),
                  full(lng.shape), full(lnb.shape), full(ws.shape), full(wst.shape), full(bsb.shape)],
        out_specs=[pl.BlockSpec((ts, 2 * D), lambda i: (i, 0)), full(ws.shape), full(bsb.shape),
                   full((1, D)), full((1, D))],
        out_shape=[jax.ShapeDtypeStruct((S, 2 * D), BF16), jax.ShapeDtypeStruct(ws.shape, F32),
                   jax.ShapeDtypeStruct(bsb.shape, F32), jax.ShapeDtypeStruct((1, D), F32),
                   jax.ShapeDtypeStruct((1, D), F32)],
        scratch_shapes=[pltpu.VMEM((ts, D), BF16), pltpu.VMEM((ts, D), F32),
                        pltpu.VMEM(bsb.shape, F32)],
        compiler_params=_cp("arbitrary"), name=name)(dy, zp, lng, lnb, ws, wst, bsb)


def _group_sum(x, groups, name):
    P, D = x.shape
    gd = D // groups

    def body(x_ref, o_ref):
        for g in range(groups):
            o_ref[:, g:g + 1] = jnp.sum(x_ref[:, g * gd:(g + 1) * gd], axis=1, keepdims=True)

    return pl.pallas_call(body, out_shape=jax.ShapeDtypeStruct((P, groups), F32), name=name)(x)


def _adamw(w, g, m, v, name):
    shape = w.shape
    C = shape[-1]
    R = w.size // C
    tr = _pick(R, 1024, 8)

    def body(w_ref, g_ref, m_ref, v_ref, d_ref, nm_ref, nv_ref):
        d_ref[...], nm_ref[...], nv_ref[...] = _adamw_math(w_ref[...], g_ref[...], m_ref[...], v_ref[...])

    tile = pl.BlockSpec((tr, C), lambda i: (i, 0))
    sh = jax.ShapeDtypeStruct((R, C), F32)
    outs = pl.pallas_call(
        body, grid=(R // tr,), in_specs=[tile] * 4, out_specs=[tile] * 3, out_shape=[sh] * 3,
        compiler_params=_cp("parallel"), name=name)(*(a.reshape(R, C) for a in (w, g, m, v)))
    return tuple(o.reshape(shape) for o in outs)


_HBM = pl.BlockSpec(memory_space=pltpu.HBM)


def _remote(src, dst, send_sem, recv_sem, to):
    return pltpu.make_async_remote_copy(src_ref=src, dst_ref=dst, send_sem=send_sem, recv_sem=recv_sem,
                                        device_id=to, device_id_type=MESH)


def _all_gather(block, name):
    R, C = block.shape

    def body(x_ref, out_ref, send_sems, recv_sems, local_sem):
        x, y, c = lax.axis_index("x"), lax.axis_index("y"), lax.axis_index("c")
        me, sibling = (x, y, c), (x, y, 1 - c)
        chips = [(1 - x, y), (x, 1 - y), (1 - x, 1 - y)]

        def slot(px, py, pc):
            return out_ref.at[4 * px + 2 * py + pc]

        def copy(k, blk, to, src=None):
            return _remote(slot(*blk) if src is None else src, slot(*blk), send_sems.at[k], recv_sems.at[k], to)

        mine = pltpu.make_async_copy(x_ref, slot(*me), local_sem)
        mine.start()
        first = [copy(0, me, sibling, src=x_ref)]
        first += [copy(1 + j, me, (*chip, c), src=x_ref) for j, chip in enumerate(chips)]
        for cp in first:
            cp.start()
        passed = [copy(4 + j, (*chip, c), sibling) for j, chip in enumerate(chips)]
        for j, chip in enumerate(chips):
            copy(1 + j, (*chip, c), me).wait_recv()
            passed[j].start()
        copy(0, sibling, me).wait_recv()
        for j, chip in enumerate(chips):
            copy(4 + j, (*chip, 1 - c), me).wait_recv()
        for cp in first + passed:
            cp.wait_send()
        mine.wait()

    return pl.pallas_call(
        body, out_shape=jax.ShapeDtypeStruct((NDEV, R, C), block.dtype), in_specs=[_HBM], out_specs=_HBM,
        scratch_shapes=[pltpu.SemaphoreType.DMA((7,)), pltpu.SemaphoreType.DMA((7,)), pltpu.SemaphoreType.DMA],
        name=name)(block)


def _all_gather_weights(pack, rows, name, after=None):
    C = pack.shape[1]
    nw = len(rows)
    starts = [sum(rows[:w]) for w in range(nw)]
    after_specs, after_ops = _after(after)

    def body(pack_ref, *rest):
        rest = rest[len(after_ops):]
        outs = rest[:nw]
        send_sems, recv_sems, local_sem = rest[nw:]
        x, y, c = lax.axis_index("x"), lax.axis_index("y"), lax.axis_index("c")
        me, sibling = (x, y, c), (x, y, 1 - c)
        chips = [(1 - x, y), (x, 1 - y), (1 - x, 1 - y)]

        def block(w, px, py, pc):
            return outs[w].at[pl.ds((4 * px + 2 * py + pc) * rows[w], rows[w])]

        def mine(w):
            return pack_ref.at[pl.ds(starts[w], rows[w])]

        def all_of(k):
            return _remote(pack_ref, pack_ref, send_sems.at[k], recv_sems.at[k], me)

        for w in range(nw):
            pltpu.make_async_copy(mine(w), block(w, *me), local_sem).start()
        for k, to in enumerate([sibling] + [(*chip, c) for chip in chips]):
            for w in range(nw):
                _remote(mine(w), block(w, *me), send_sems.at[k], recv_sems.at[k], to).start()
        for j, chip in enumerate(chips):
            all_of(1 + j).wait_recv()
            for w in range(nw):
                _remote(block(w, *chip, c), block(w, *chip, c), send_sems.at[4 + j], recv_sems.at[4 + j], sibling).start()
        all_of(0).wait_recv()
        for j in range(3):
            all_of(4 + j).wait_recv()
        for k in range(7):
            all_of(k).wait_send()
        pltpu.make_async_copy(pack_ref, pack_ref, local_sem).wait()

    return pl.pallas_call(
        body, out_shape=[jax.ShapeDtypeStruct((NDEV * r, C), pack.dtype) for r in rows],
        in_specs=[_HBM] + after_specs, out_specs=[_HBM] * nw,
        scratch_shapes=[pltpu.SemaphoreType.DMA((7,)), pltpu.SemaphoreType.DMA((7,)), pltpu.SemaphoreType.DMA],
        name=name)(pack, *after_ops)


_SEM = pl.BlockSpec(memory_space=pltpu.SEMAPHORE)
_DATAFLOW = pltpu.SideEffectType.DATAFLOW_SIDE_EFFECTING


def _split_start(srcs, lands, plan, n, after, name):
    nbuf = len(srcs) + len(lands)
    after_specs, after_ops = _after(after)

    def body(*refs):
        src_refs, land_refs = refs[:len(srcs)], refs[len(srcs):nbuf]
        send_sems, recv_sems = refs[nbuf + len(after_ops)], refs[nbuf + len(after_ops) + 1]
        for k, (src, dst, to) in enumerate(plan(src_refs, land_refs)):
            _remote(src, dst, send_sems.at[k], recv_sems.at[k], to).start()
        refs[-1][...] = jnp.zeros_like(refs[-1])

    bufs = [pltpu.with_memory_space_constraint(a, pltpu.HBM) for a in list(srcs) + list(lands)]
    outs = pl.pallas_call(
        body, name=name,
        out_shape=(pltpu.SemaphoreType.DMA((n,)), pltpu.SemaphoreType.DMA((n,)),
                   *[pltpu.HBM(a.shape, a.dtype) for a in bufs], jax.ShapeDtypeStruct((8, 128), F32)),
        in_specs=[_HBM] * nbuf + after_specs,
        out_specs=(_SEM, _SEM, *[_HBM] * nbuf, pl.BlockSpec(memory_space=pltpu.VMEM)),
        input_output_aliases={i: 2 + i for i in range(nbuf)},
        compiler_params=pltpu.CompilerParams(has_side_effects=_DATAFLOW))(*bufs, *after_ops)
    return outs[0], outs[1], list(outs[2:2 + len(srcs)]), list(outs[2 + len(srcs):2 + nbuf]), outs[-1]


def _split_wait(send_sems, recv_sems, srcs, lands, plan, after, name):
    nbuf = len(srcs) + len(lands)
    after_specs, after_ops = _after(after)

    def body(*refs):
        src_refs, land_refs = refs[:len(srcs)], refs[len(srcs):nbuf]
        send_sems_ref, recv_sems_ref = refs[nbuf], refs[nbuf + 1]
        for k, (src, dst, to) in enumerate(plan(src_refs, land_refs)):
            copy = _remote(src, dst, send_sems_ref.at[k], recv_sems_ref.at[k], to)
            copy.wait_send()
            copy.wait_recv()

    outs = pl.pallas_call(
        body, name=name, out_shape=tuple(pltpu.HBM(a.shape, a.dtype) for a in list(srcs) + list(lands)),
        in_specs=[_HBM] * nbuf + [_SEM, _SEM] + after_specs, out_specs=tuple([_HBM] * nbuf),
        input_output_aliases={i: i for i in range(nbuf)},
        compiler_params=pltpu.CompilerParams(has_side_effects=_DATAFLOW))(*srcs, *lands, send_sems, recv_sems, *after_ops)
    return list(outs[:len(srcs)]), list(outs[len(srcs):])


def _peers(x, y, c):
    return [(mask, (1 - x if mask & 4 else x, 1 - y if mask & 2 else y, 1 - c if mask & 1 else c))
            for mask in range(1, NDEV)]


def _gather_plan(rows):
    starts = [sum(rows[:w]) for w in range(len(rows))]

    def plan(src_refs, land_refs):
        x, y, c = lax.axis_index("x"), lax.axis_index("y"), lax.axis_index("c")
        copies = []
        for w, r in enumerate(rows):
            mine = src_refs[0].at[pl.ds(starts[w], r)]
            dst = land_refs[w].at[pl.ds((4 * x + 2 * y + c) * r, r)]
            copies += [(mine, dst, peer) for _, peer in _peers(x, y, c)]
        return copies

    return plan, (NDEV - 1) * len(rows)


def _place_own(shards, fulls, dev_idx, name):
    nw = len(shards)

    def body(i_ref, *refs):
        for w in range(nw):
            refs[2 * nw + w][...] = refs[w][...]

    grid_spec = pltpu.PrefetchScalarGridSpec(
        num_scalar_prefetch=1, grid=(1,),
        in_specs=[pl.BlockSpec(s.shape, lambda t, i_ref: (0, 0)) for s in shards] + [_ANY] * nw,
        out_specs=[pl.BlockSpec(s.shape, lambda t, i_ref: (i_ref[0], 0)) for s in shards])
    outs = pl.pallas_call(
        body, grid_spec=grid_spec, out_shape=[jax.ShapeDtypeStruct(f.shape, f.dtype) for f in fulls],
        input_output_aliases={1 + nw + w: w for w in range(nw)}, name=name)(dev_idx, *shards, *fulls)
    return list(outs)


def _scatter_plan(rows):
    def plan(src_refs, land_refs):
        x, y, c = lax.axis_index("x"), lax.axis_index("y"), lax.axis_index("c")
        copies = []
        for w, r in enumerate(rows):
            for mask, (px, py, pc) in _peers(x, y, c):
                src = src_refs[w].at[pl.ds((4 * px + 2 * py + pc) * r, r)]
                copies.append((src, land_refs[w].at[mask - 1], (px, py, pc)))
        return copies

    return plan, (NDEV - 1) * len(rows)


def _adamw_math(w, g, m, v):
    nm = ADAM_B1 * m + (1.0 - ADAM_B1) * g
    nv = ADAM_B2 * v + (1.0 - ADAM_B2) * (g * g)
    bc1 = 1.0 - ADAM_B1 ** ADAM_STEP
    bc2 = 1.0 - ADAM_B2 ** ADAM_STEP
    return -ADAM_LR * ((nm / bc1) / (jnp.sqrt(nv / bc2) + ADAM_EPS) + ADAM_WD * w), nm, nv


def _finish_weight(gs, gots, dev_idx, w, m, v, transposed, name, after=None):
    L = len(gs)
    n1, r, C = gots[0].shape
    block = (None,) + w.shape[1:]

    after_specs, after_ops = _after(after)

    def body(i_ref, *refs):
        ins, (w_ref, m_ref, v_ref), (g_out, d_out, m_out, v_out) = refs[:2 * L], refs[2 * L:2 * L + 3], refs[-4:]
        for layer in range(L):
            @pl.when(pl.program_id(0) == layer)
            def _():
                g_ref, got_ref = ins[2 * layer], ins[2 * layer + 1]
                acc = g_ref[...].astype(F32)
                for k in range(n1):
                    acc = acc + got_ref[k].astype(F32)
                if transposed:
                    acc = acc.T
                g_out[...] = acc
                d_out[...], m_out[...], v_out[...] = _adamw_math(w_ref[...], acc, m_ref[...], v_ref[...])

    in_specs, ins = [], []
    for g, got in zip(gs, gots):
        in_specs += [pl.BlockSpec((r, C), lambda t, i_ref: (i_ref[0], 0), pipeline_mode=_RESIDENT),
                     pl.BlockSpec((n1, r, C), lambda t, i_ref: (0, 0, 0), pipeline_mode=_RESIDENT)]
        ins += [g, got]
    per_layer = pl.BlockSpec(block, lambda t, i_ref: (t, 0, 0))
    grid_spec = pltpu.PrefetchScalarGridSpec(
        num_scalar_prefetch=1, grid=(L,), in_specs=in_specs + [per_layer] * 3 + after_specs,
        out_specs=[per_layer] * 4)
    return pl.pallas_call(
        body, grid_spec=grid_spec, out_shape=[jax.ShapeDtypeStruct(w.shape, F32)] * 4,
        compiler_params=_cp("arbitrary"), name=name)(dev_idx, *ins, w, m, v, *after_ops)


def _sum_slots(a, name):
    n, R, C = a.shape

    def body(a_ref, o_ref):
        acc = a_ref[0]
        for k in range(1, n):
            acc = acc + a_ref[k]
        o_ref[...] = acc

    return pl.pallas_call(body, out_shape=jax.ShapeDtypeStruct((R, C), F32), name=name)(a)


def _shard_axis(name):
    return {"ev_w_in": 2, "ev_a_conv_w": 2, "ev_b_conv_w": 2, "ev_w_out": 1, "od_w_in": 2, "od_c_ln_g": 1,
            "od_c_ln_b": 1, "od_w_out": 1, "xa_w_q": 1, "xa_w_k": 1, "xa_w_v": 1, "xa_w_o": 1,
            "ffn_w_gate": 2, "ffn_w_up": 2, "ffn_w_down": 1}[name]


BIG = ["ev_w_in", "ev_w_out", "od_w_in", "od_w_out", "xa_w_q", "xa_w_k", "xa_w_v", "xa_w_o",
       "ffn_w_gate", "ffn_w_up", "ffn_w_down"]
SMALL_SHARDED = ["ev_a_conv_w", "ev_b_conv_w", "od_c_ln_g", "od_c_ln_b"]
REPLICATED = ["g_mix", "g_xattn", "g_mem", "g_ffn", "g_final", "ev_a_conv_b", "ev_a_ln_g", "ev_a_ln_b",
              "ev_b_conv_b", "od_w_s", "od_b_s"]
WEIGHTS = ["g_mix", "g_xattn", "g_mem", "g_ffn", "g_final", "ev_w_in", "ev_a_conv_w", "ev_a_conv_b", "ev_a_ln_g",
           "ev_a_ln_b", "ev_b_conv_w", "ev_b_conv_b", "ev_w_out", "od_w_in", "od_c_ln_g", "od_c_ln_b", "od_w_s",
           "od_b_s", "od_w_out", "xa_w_q", "xa_w_k", "xa_w_v", "xa_w_o", "ffn_w_gate", "ffn_w_up", "ffn_w_down"]


def _full_from_blocks(blocks, axis):
    shard = blocks.shape[1:]
    full = jnp.moveaxis(blocks, 0, axis)
    return full.reshape(shard[:axis] + (NDEV * shard[axis],) + shard[axis + 1:])


def _blocks_from_full(full, axis):
    shp = full.shape
    split = full.reshape(shp[:axis] + (NDEV, shp[axis] // NDEV) + shp[axis + 1:])
    return jnp.moveaxis(split, axis, 0)


def _pad_rows(flat, width, row_align):
    per = width * row_align
    n = -(-flat.shape[0] // per) * per
    return jnp.pad(flat, (0, n - flat.shape[0])).reshape(n // width, width)


def _row(v):
    return v.reshape(1, -1)


def _xattn_fwd(h, nq, mem, g_m, wq, wk, wv, wo, g_next, tag, after):
    mem_n = _rms_fwd(mem, _row(g_m), f"xa_mem_rms_{tag}")
    q = _mm([(nq, wq, "nn")], f"xa_q_{tag}", out_dtype=BF16, after=after)
    k = _mm([(mem_n, wk, "nn")], f"xa_k_{tag}", out_dtype=BF16)
    v = _mm([(mem_n, wv, "nn")], f"xa_v_{tag}", out_dtype=BF16)
    o = _attn_fwd(q, k, v, f"xa_attn_{tag}")
    h_new, n_next = _mm([(o, wo, "nn")], f"xa_o_{tag}", res=h, rms_g=_row(g_next))
    return h_new, n_next, (h, nq, mem_n, q, k, v, o)


def _xattn_bwd(dh_new, saved, mem, g_x, g_m, wq, wk, wv, wo, tag, push):
    h, nq, mem_n, q, k, v, o = saved
    do = _mm([(dh_new, wo, "nt")], f"xa_do_{tag}", out_dtype=BF16)
    d_wo = _mm_tn(o, dh_new, f"xa_dwo_{tag}")
    dq, dk, dv = _attn_bwd(q, k, v, do, f"xa_attn_bwd_{tag}")
    d_wq = _mm_tn(nq, dq, f"xa_dwq_{tag}")
    d_wk = _mm_tn(mem_n, dk, f"xa_dwk_{tag}")
    d_wv = _mm_tn(mem_n, dv, f"xa_dwv_{tag}")
    token = push([d_wq, d_wk, d_wv, d_wo])
    dmem_n = _mm([(dk, wk, "nt"), (dv, wv, "nt")], f"xa_dmem_{tag}", after=token)
    _, d_gm = _rms_bwd(dmem_n, mem, _row(g_m), None, f"xa_mem_rms_bwd_{tag}")
    dh, d_gx = _mm([(dq, wq, "nt")], f"xa_dnq_{tag}", rms_bwd=(h, _row(g_x), dh_new), tm=512, after=token)
    return dh, dict(g_xattn=d_gx, g_mem=d_gm)


def _ffn_fwd(h, n, wgt, wut, wd, g_next, tag, after):
    a, b, hid = _ffn_up(n, wgt, wut, f"ffn_up_{tag}", after=after)
    if g_next is None:
        h_new, n_next = _mm([(hid, wd, "nn")], f"ffn_down_{tag}", res=h, tm=512, tn=1024), None
    else:
        h_new, n_next = _mm([(hid, wd, "nn")], f"ffn_down_{tag}", res=h, rms_g=_row(g_next), tm=512)
    return h_new, n_next, (h, n, a, b, hid)


def _ffn_bwd(dh_new, saved, g_f, wgt, wut, wd, tag, push):
    h, n, a, b, hid = saved
    da, db = _ffn_dhid(dh_new, wd, a, b, f"ffn_dhid_{tag}")
    d_wd = _mm_tn(hid, dh_new, f"ffn_dwd_{tag}", tn=512)
    d_wgt = _mm_tn(da, n, f"ffn_dwg_{tag}", tn=512)
    d_wut = _mm_tn(db, n, f"ffn_dwu_{tag}", tn=512)
    token = push([d_wgt, d_wut, d_wd])
    dh, d_gf = _mm([(da, wgt, "nn"), (db, wut, "nn")], f"ffn_dn_{tag}", rms_bwd=(h, _row(g_f), dh_new), tm=512,
                   after=token)
    return dh, dict(g_ffn=d_gf)


_XA = ["xa_w_q", "xa_w_k", "xa_w_v", "xa_w_o"]
_FFN = ["ffn_w_gate", "ffn_w_up", "ffn_w_down"]
GATHERS = {
    "ev_in": [("ev_w_in", 0)],
    "xa0": [("ev_w_out", 0)] + [(n, 0) for n in _XA],
    "ffn0": [(n, 0) for n in _FFN],
    "od": [("od_w_in", 0), ("od_w_out", 0)],
    "xa1": [(n, 1) for n in _XA],
    "ffn1": [(n, 1) for n in _FFN],
}
SCATTERS = {
    "ffn1": [(n, 1) for n in _FFN],
    "xa1": [(n, 1) for n in _XA],
    "od": [("od_w_in", 0), ("od_w_out", 0)],
    "ffn0": [(n, 0) for n in _FFN],
    "xa0": [(n, 0) for n in _XA],
    "ev": [("ev_w_in", 0), ("ev_w_out", 0)],
}


def _local_step(x, mem, loss_target, W, comm):
    grads = {}

    h0 = x
    (ev_w_in_t,), token = comm.weights("ev_in", None)
    n0 = _rms_fwd(h0, _row(W["g_mix"][0]), "ev_rms", after=token)
    z = _mm([(n0, ev_w_in_t, "nt")], "ev_in", tn=1280)
    token = comm.prefetch(["ffn0"], z)
    ab, ca = _conv_fwd(z, W["ev_a_conv_w"][0], W["ev_a_conv_b"], W["ev_a_ln_g"], W["ev_a_ln_b"],
                       W["ev_b_conv_w"][0], W["ev_b_conv_b"], "ev_conv", after=token)
    (ev_w_out, *xa_w0), _ = comm.weights("xa0", ab)
    h1, nq0 = _mm([(ab, ev_w_out, "nn")], "ev_out", res=h0, rms_g=_row(W["g_xattn"][0]))
    token = comm.prefetch(["od", "xa1"], nq0)
    h2, nf0, xa0 = _xattn_fwd(h1, nq0, mem, W["g_mem"][0], *xa_w0, W["g_ffn"][0], "l0", token)
    ffn_w0, _ = comm.weights("ffn0", nf0)
    token = comm.prefetch(["ffn1"], nf0)
    h3, n3, ff0 = _ffn_fwd(h2, nf0, *ffn_w0, W["g_mix"][1], "l0", token)

    (od_w_in_t, od_w_out), _ = comm.weights("od", n3)
    zp = _mm([(n3, od_w_in_t, "nt")], "od_in", tn=1024)
    D = x.shape[1]
    ws = W["od_w_s"][0].astype(BF16)
    wst = jnp.swapaxes(ws, 1, 2)
    bsb = jnp.repeat(jnp.transpose(W["od_b_s"][0]), D // C_GROUPS, axis=1)
    y_sgu = _sgu_fwd(zp, W["od_c_ln_g"], W["od_c_ln_b"], ws, bsb, "od_sgu")
    h4, nq1 = _mm([(y_sgu, od_w_out, "nn")], "od_out", res=h3, rms_g=_row(W["g_xattn"][1]))
    xa_w1, _ = comm.weights("xa1", nq1)
    h5, nf1, xa1 = _xattn_fwd(h4, nq1, mem, W["g_mem"][1], *xa_w1, W["g_ffn"][1], "l1", None)
    ffn_w1, _ = comm.weights("ffn1", nf1)
    h6, _, ff1 = _ffn_fwd(h5, nf1, *ffn_w1, None, "l1", None)

    loss_row, dh6, d_gfinal = _loss_bwd(h6, _row(W["g_final"]), loss_target, "loss")
    grads["g_final"] = d_gfinal.reshape(-1)

    dh5, g_ff1 = _ffn_bwd(dh6, ff1, W["g_ffn"][1], *ffn_w1, "l1", lambda dws: comm.grads("ffn1", dws))
    dh4, g_xa1 = _xattn_bwd(dh5, xa1, mem, W["g_xattn"][1], W["g_mem"][1], *xa_w1, "l1",
                            lambda dws: comm.grads("xa1", dws))
    dy_sgu = _mm([(dh4, od_w_out, "nt")], "od_dy", tn=1024)
    d_od_out = _mm_tn(y_sgu, dh4, "od_dwout", tn=1024)
    dzp, d_ws, d_bsb, d_clng, d_clnb = _sgu_bwd(dy_sgu, zp, W["od_c_ln_g"], W["od_c_ln_b"], ws, wst, bsb, "od_sgu_bwd")
    grads["od_w_s"] = d_ws[None]
    grads["od_b_s"] = jnp.transpose(_group_sum(d_bsb, C_GROUPS, "od_dbs"))[None]
    grads["od_c_ln_g"], grads["od_c_ln_b"] = d_clng, d_clnb
    token = comm.grads("od", [_mm_tn(dzp, n3, "od_dwin", tn=512), d_od_out])
    dh3, d_gmix1 = _mm([(dzp, od_w_in_t, "nn")], "od_dn", rms_bwd=(h3, _row(W["g_mix"][1]), dh4), tm=512, after=token)

    dh2, g_ff0 = _ffn_bwd(dh3, ff0, W["g_ffn"][0], *ffn_w0, "l0", lambda dws: comm.grads("ffn0", dws))
    dh1, g_xa0 = _xattn_bwd(dh2, xa0, mem, W["g_xattn"][0], W["g_mem"][0], *xa_w0, "l0",
                            lambda dws: comm.grads("xa0", dws))
    dab = _mm([(dh1, ev_w_out, "nt")], "ev_dab", tn=1024)
    d_ev_out = _mm_tn(ab, dh1, "ev_dwout", tn=1024)
    dca, d_lng, d_lnb, d_ba = _conv_bwd_ln(dab, ca, W["ev_a_ln_g"], W["ev_a_ln_b"], "ev_conv_bwd_ln")
    dz, d_wa, d_wb, d_bb = _conv_bwd(z, dca, dab, W["ev_a_conv_w"][0], W["ev_b_conv_w"][0], W["ev_b_conv_b"],
                                     "ev_conv_bwd")
    grads.update(ev_a_ln_g=d_lng, ev_a_ln_b=d_lnb, ev_a_conv_b=d_ba, ev_b_conv_b=d_bb,
                 ev_a_conv_w=d_wa[None], ev_b_conv_w=d_wb[None])
    token = comm.grads("ev", [_mm_tn(dz, n0, "ev_dwin", tn=512), d_ev_out])
    grad_x, d_gmix0 = _mm([(dz, ev_w_in_t, "nn")], "ev_dn", rms_bwd=(h0, _row(W["g_mix"][0]), dh1), tm=512, after=token)

    grads["g_mix"] = jnp.concatenate([d_gmix0, d_gmix1], axis=0)
    for key in ("g_xattn", "g_mem"):
        grads[key] = jnp.concatenate([g_xa0[key], g_xa1[key]], axis=0)
    grads["g_ffn"] = jnp.concatenate([g_ff0["g_ffn"], g_ff1["g_ffn"]], axis=0)
    return loss_row, grad_x, grads


class _Exchanges:
    def __init__(self, shards, dev_idx, after):
        self.shards, self.dev_idx = shards, dev_idx
        self.gathering, self.scattering = {}, {}
        self.first = _all_gather_weights(self._pack(GATHERS["ev_in"]), self._rows(GATHERS["ev_in"]), "ag_ev_in",
                                         after=after)
        self.first_token = self.prefetch(["xa0"], self.first[0])

    def _rows(self, entries):
        return [self.shards[e].shape[0] for e in entries]

    def _pack(self, entries):
        return jnp.concatenate([self.shards[e] for e in entries], axis=0)

    def prefetch(self, gathers, after):
        for name in gathers:
            rows = self._rows(GATHERS[name])
            pack = self._pack(GATHERS[name])
            lands = [lax.empty((NDEV * r, pack.shape[1]), pack.dtype) for r in rows]
            plan, n = _gather_plan(rows)
            send, recv, srcs, lands, after = _split_start([pack], lands, plan, n, after, f"ag_{name}_start")
            self.gathering[name] = (send, recv, srcs, lands, plan, rows)
        return after

    def weights(self, name, after):
        if name == "ev_in":
            return self.first, self.first_token
        send, recv, srcs, lands, plan, rows = self.gathering.pop(name)
        _, lands = _split_wait(send, recv, srcs, lands, plan, after, f"ag_{name}_wait")
        return _place_own([self.shards[e] for e in GATHERS[name]], lands, self.dev_idx, f"ag_{name}_own"), None

    def grads(self, name, dws):
        rows = self._rows(SCATTERS[name])
        lands = [lax.empty((NDEV - 1, r, d.shape[1]), d.dtype) for r, d in zip(rows, dws)]
        plan, n = _scatter_plan(rows)
        send, recv, srcs, lands, token = _split_start(dws, lands, plan, n, None, f"rs_{name}_start")
        self.scattering[name] = (send, recv, srcs, lands, plan)
        return token

    def received(self, after):
        out = {}
        for name, (send, recv, srcs, lands, plan) in self.scattering.items():
            srcs, lands = _split_wait(send, recv, srcs, lands, plan, after, f"rs_{name}_wait")
            for entry, g, got in zip(SCATTERS[name], srcs, lands):
                out[entry] = (g, got)
        return out


def kernel(x, mem, g_mix, g_xattn, g_mem, g_ffn, g_final, ev_w_in, ev_a_conv_w, ev_a_conv_b, ev_a_ln_g, ev_a_ln_b, ev_b_conv_w, ev_b_conv_b, ev_w_out, od_w_in, od_c_ln_g, od_c_ln_b, od_w_s, od_b_s, od_w_out, xa_w_q, xa_w_k, xa_w_v, xa_w_o, ffn_w_gate, ffn_w_up, ffn_w_down, loss_target, m_g_mix, m_g_xattn, m_g_mem, m_g_ffn, m_g_final, m_ev_w_in, m_ev_a_conv_w, m_ev_a_conv_b, m_ev_a_ln_g, m_ev_a_ln_b, m_ev_b_conv_w, m_ev_b_conv_b, m_ev_w_out, m_od_w_in, m_od_c_ln_g, m_od_c_ln_b, m_od_w_s, m_od_b_s, m_od_w_out, m_xa_w_q, m_xa_w_k, m_xa_w_v, m_xa_w_o, m_ffn_w_gate, m_ffn_w_up, m_ffn_w_down, v_g_mix, v_g_xattn, v_g_mem, v_g_ffn, v_g_final, v_ev_w_in, v_ev_a_conv_w, v_ev_a_conv_b, v_ev_a_ln_g, v_ev_a_ln_b, v_ev_b_conv_w, v_ev_b_conv_b, v_ev_w_out, v_od_w_in, v_od_c_ln_g, v_od_c_ln_b, v_od_w_s, v_od_b_s, v_od_w_out, v_xa_w_q, v_xa_w_k, v_xa_w_v, v_xa_w_o, v_ffn_w_gate, v_ffn_w_up, v_ffn_w_down):
    local = dict(g_mix=g_mix, g_xattn=g_xattn, g_mem=g_mem, g_ffn=g_ffn, g_final=g_final, ev_w_in=ev_w_in, ev_a_conv_w=ev_a_conv_w, ev_a_conv_b=ev_a_conv_b, ev_a_ln_g=ev_a_ln_g, ev_a_ln_b=ev_a_ln_b, ev_b_conv_w=ev_b_conv_w, ev_b_conv_b=ev_b_conv_b, ev_w_out=ev_w_out, od_w_in=od_w_in, od_c_ln_g=od_c_ln_g, od_c_ln_b=od_c_ln_b, od_w_s=od_w_s, od_b_s=od_b_s, od_w_out=od_w_out, xa_w_q=xa_w_q, xa_w_k=xa_w_k, xa_w_v=xa_w_v, xa_w_o=xa_w_o, ffn_w_gate=ffn_w_gate, ffn_w_up=ffn_w_up, ffn_w_down=ffn_w_down)
    mom = dict(g_mix=m_g_mix, g_xattn=m_g_xattn, g_mem=m_g_mem, g_ffn=m_g_ffn, g_final=m_g_final, ev_w_in=m_ev_w_in, ev_a_conv_w=m_ev_a_conv_w, ev_a_conv_b=m_ev_a_conv_b, ev_a_ln_g=m_ev_a_ln_g, ev_a_ln_b=m_ev_a_ln_b, ev_b_conv_w=m_ev_b_conv_w, ev_b_conv_b=m_ev_b_conv_b, ev_w_out=m_ev_w_out, od_w_in=m_od_w_in, od_c_ln_g=m_od_c_ln_g, od_c_ln_b=m_od_c_ln_b, od_w_s=m_od_w_s, od_b_s=m_od_b_s, od_w_out=m_od_w_out, xa_w_q=m_xa_w_q, xa_w_k=m_xa_w_k, xa_w_v=m_xa_w_v, xa_w_o=m_xa_w_o, ffn_w_gate=m_ffn_w_gate, ffn_w_up=m_ffn_w_up, ffn_w_down=m_ffn_w_down)
    vel = dict(g_mix=v_g_mix, g_xattn=v_g_xattn, g_mem=v_g_mem, g_ffn=v_g_ffn, g_final=v_g_final, ev_w_in=v_ev_w_in, ev_a_conv_w=v_ev_a_conv_w, ev_a_conv_b=v_ev_a_conv_b, ev_a_ln_g=v_ev_a_ln_g, ev_a_ln_b=v_ev_a_ln_b, ev_b_conv_w=v_ev_b_conv_w, ev_b_conv_b=v_ev_b_conv_b, ev_w_out=v_ev_w_out, od_w_in=v_od_w_in, od_c_ln_g=v_od_c_ln_g, od_c_ln_b=v_od_c_ln_b, od_w_s=v_od_w_s, od_b_s=v_od_b_s, od_w_out=v_od_w_out, xa_w_q=v_xa_w_q, xa_w_k=v_xa_w_k, xa_w_v=v_xa_w_v, xa_w_o=v_xa_w_o, ffn_w_gate=v_ffn_w_gate, ffn_w_up=v_ffn_w_up, ffn_w_down=v_ffn_w_down)
    D = x.shape[-1]
    dev = 4 * lax.axis_index("x") + 2 * lax.axis_index("y") + lax.axis_index("c")

    def comm_layout(n, a):
        return jnp.transpose(a) if _shard_axis(n) == 2 else a

    shards = {(n, i): comm_layout(n, local[n][i]).astype(BF16) for n in BIG for i in range(local[n].shape[0])}
    small_sizes = [local[n].size for n in SMALL_SHARDED]
    small_block = _pad_rows(jnp.concatenate([local[n].reshape(-1) for n in SMALL_SHARDED]), 128, 8)
    small_all = _all_gather(small_block, "ag_small")
    comm = _Exchanges(shards, jnp.reshape(dev, (1,)).astype(jnp.int32), small_all)
    small_all = small_all.reshape(NDEV, -1)

    W = {n: local[n] for n in REPLICATED}
    o0 = 0
    for n, sz in zip(SMALL_SHARDED, small_sizes):
        blocks = small_all[:, o0:o0 + sz].reshape((NDEV,) + local[n].shape)
        W[n] = _full_from_blocks(blocks, _shard_axis(n))
        o0 += sz

    loss_row, grad_x, grads = _local_step(x[0], mem[0], loss_target[0], W, comm)

    received = comm.received(grad_x)
    rest = REPLICATED + SMALL_SHARDED
    rest_full_shapes = [grads[n].shape for n in rest]
    g_rest = _pad_rows(jnp.concatenate([grads[n].astype(F32).reshape(-1) for n in rest]), D, 8)
    small_rows = g_rest.shape[0]
    small_plan, small_n = _gather_plan([small_rows])
    small_send, small_recv, small_srcs, small_lands, token = _split_start(
        [g_rest], [lax.empty((NDEV * small_rows, D), F32)], small_plan, small_n, received["ev_w_in", 0][1],
        "ag_small_grads_start")

    gsh, delta, new_m, new_v = {}, {}, {}, {}
    for n in BIG:
        parts = [received[n, i] for i in range(local[n].shape[0])]
        gsh[n], delta[n], new_m[n], new_v[n] = _finish_weight(
            [p[0] for p in parts], [p[1] for p in parts], comm.dev_idx, local[n], mom[n], vel[n],
            _shard_axis(n) == 2, f"finish_{n}", after=token)

    _, small_lands = _split_wait(small_send, small_recv, small_srcs, small_lands, small_plan, delta[BIG[-1]],
                                 "ag_small_grads_wait")
    partials = _place_own([g_rest], small_lands, comm.dev_idx, "ag_small_grads_own")[0]
    g_rest = _sum_slots(partials.reshape(NDEV, small_rows, D), "sum_small_grads").reshape(-1)
    o0 = 0
    for n, shp in zip(rest, rest_full_shapes):
        sz = 1
        for s in shp:
            sz *= s
        full = g_rest[o0:o0 + sz].reshape(shp)
        o0 += sz
        if n in SMALL_SHARDED:
            full = lax.dynamic_index_in_dim(_blocks_from_full(full, _shard_axis(n)), dev, 0, keepdims=False)
        gsh[n] = full.reshape(local[n].shape)

    for n in rest:
        delta[n], new_m[n], new_v[n] = _adamw(local[n], gsh[n], mom[n], vel[n], f"adamw_{n}")

    loss = lax.psum(loss_row[0, 0], ("x", "y", "c"))
    return (loss, grad_x[None], *[gsh[n] for n in WEIGHTS], *[delta[n] for n in WEIGHTS],
            *[new_m[n] for n in WEIGHTS], *[new_v[n] for n in WEIGHTS])
```

```python
import jax
import jax.numpy as jnp
from jax import lax
from jax.experimental import pallas as pl
from jax.experimental.pallas import tpu as pltpu

F32, BF16 = jnp.float32, jnp.bfloat16
NDEV = 8
RMS_EPS = 1e-6
LN_EPS = 1e-5
CHUNK = 128
C_GROUPS = 8
XA_HEADS = 4
ADAM_LR, ADAM_B1, ADAM_B2, ADAM_EPS, ADAM_WD, ADAM_STEP = 0.001, 0.9, 0.999, 1e-08, 0.01, 10
HALO = 16
ROW_CHUNK = 32
V7X_VMEM_LIMIT = 56 * 1024 * 1024
MESH = pl.DeviceIdType.MESH

TS_ROW = 512
TS_MM = 1024
TN_MM = 1408
TS_FFN = 512
MM_ROW_CHUNK = 256
TS_CONV = 512
TS_SGU = 512
TS_ATTN = 512


def _cp(*sem):
    return pltpu.CompilerParams(dimension_semantics=sem, vmem_limit_bytes=V7X_VMEM_LIMIT)


def _pick(n, pref, align):
    for t in range(min(n, pref), 0, -1):
        if n % t == 0 and (t % align == 0 or t == n):
            return t
    return n


def _sigmoid(x):
    return 0.5 * jnp.tanh(0.5 * x) + 0.5


def _dot(a, b):
    return jnp.dot(a, b, preferred_element_type=F32)


def _dot_nt(a, b):
    return lax.dot_general(a, b, (((1,), (1,)), ((), ())), preferred_element_type=F32)


def _dot_tn(a, b):
    return lax.dot_general(a, b, (((0,), (0,)), ((), ())), preferred_element_type=F32)


_ANY = pl.BlockSpec(memory_space=pl.ANY)
_RESIDENT = pl.Buffered(1)


def _after(after):
    return ([], []) if after is None else ([_ANY], [after])


def _rms_fwd(h, g, name, after=None):
    S, D = h.shape
    ts = _pick(S, TS_MM, 16)
    after_specs, after_ops = _after(after)

    def body(h_ref, g_ref, *rest):
        o_ref = rest[-1]
        x = h_ref[...]
        r = lax.rsqrt(jnp.mean(x * x, axis=-1, keepdims=True) + RMS_EPS)
        o_ref[...] = ((x * r) * g_ref[...]).astype(o_ref.dtype)

    return pl.pallas_call(
        body, grid=(S // ts,),
        in_specs=[pl.BlockSpec((ts, D), lambda i: (i, 0)), pl.BlockSpec((1, D), lambda i: (0, 0))] + after_specs,
        out_specs=pl.BlockSpec((ts, D), lambda i: (i, 0)),
        out_shape=jax.ShapeDtypeStruct((S, D), BF16), compiler_params=_cp("parallel"), name=name)(h, g, *after_ops)


def _rms_bwd(dn, h, g, dres, name):
    S, D = h.shape
    ts = _pick(S, TS_ROW, 8)
    has_res = dres is not None

    def body(*refs):
        if has_res:
            dn_ref, h_ref, g_ref, dres_ref, dh_ref, dg_ref = refs
        else:
            dn_ref, h_ref, g_ref, dh_ref, dg_ref = refs
        x = h_ref[...]
        dn_ = dn_ref[...].astype(F32)
        r = lax.rsqrt(jnp.mean(x * x, axis=-1, keepdims=True) + RMS_EPS)
        xr = x * r

        @pl.when(pl.program_id(0) == 0)
        def _():
            dg_ref[...] = jnp.zeros_like(dg_ref)

        dg_ref[...] += jnp.sum(dn_ * xr, axis=0, keepdims=True)
        u = dn_ * g_ref[...]
        dh = r * u - xr * (r * jnp.mean(u * xr, axis=-1, keepdims=True))
        if has_res:
            dh = dh + dres_ref[...]
        dh_ref[...] = dh

    tile = pl.BlockSpec((ts, D), lambda i: (i, 0))
    vec = pl.BlockSpec((1, D), lambda i: (0, 0))
    ins = [dn, h, g] + ([dres] if has_res else [])
    return pl.pallas_call(
        body, grid=(S // ts,),
        in_specs=[tile, tile, vec] + ([tile] if has_res else []),
        out_specs=[tile, vec],
        out_shape=[jax.ShapeDtypeStruct((S, D), F32), jax.ShapeDtypeStruct((1, D), F32)],
        compiler_params=_cp("arbitrary"), name=name)(*ins)


def _loss_bwd(h, g, target, name):
    S, D = h.shape
    ts = _pick(S, TS_ROW, 8)

    def body(h_ref, g_ref, t_ref, loss_ref, dh_ref, dg_ref):
        x = h_ref[...]
        r = lax.rsqrt(jnp.mean(x * x, axis=-1, keepdims=True) + RMS_EPS)
        xr = x * r
        gg = g_ref[...]
        e = xr * gg - t_ref[...]

        @pl.when(pl.program_id(0) == 0)
        def _():
            dg_ref[...] = jnp.zeros_like(dg_ref)
            loss_ref[...] = jnp.zeros_like(loss_ref)

        tile_loss = jnp.sum(jnp.sum(e * e, axis=0, keepdims=True), axis=1, keepdims=True) * (0.5 / D)
        loss_ref[...] += jnp.broadcast_to(tile_loss, loss_ref.shape)
        dy = e * (1.0 / D)
        dg_ref[...] += jnp.sum(dy * xr, axis=0, keepdims=True)
        u = dy * gg
        dh_ref[...] = r * u - xr * (r * jnp.mean(u * xr, axis=-1, keepdims=True))

    tile = pl.BlockSpec((ts, D), lambda i: (i, 0))
    vec = pl.BlockSpec((1, D), lambda i: (0, 0))
    return pl.pallas_call(
        body, grid=(S // ts,),
        in_specs=[tile, vec, tile],
        out_specs=[pl.BlockSpec((1, 128), lambda i: (0, 0)), tile, vec],
        out_shape=[jax.ShapeDtypeStruct((1, 128), F32), jax.ShapeDtypeStruct((S, D), F32),
                   jax.ShapeDtypeStruct((1, D), F32)],
        compiler_params=_cp("arbitrary"), name=name)(h, g, target)


def _mm(pairs, name, out_dtype=F32, res=None, rms_g=None, rms_bwd=None, tm=None, tn=None, after=None):
    M = pairs[0][0].shape[0]
    N = pairs[0][1].shape[1 if pairs[0][2] == "nn" else 0]
    whole_rows = rms_g is not None or rms_bwd is not None
    tm = _pick(M, tm or TS_MM, 16)
    tn = N if whole_rows else _pick(N, tn or TN_MM, 128)
    npair = len(pairs)
    modes = [p[2] for p in pairs]
    after_specs, after_ops = _after(after)

    rc = MM_ROW_CHUNK if whole_rows and tm % MM_ROW_CHUNK == 0 else tm

    def body(*refs):
        rest = refs[2 * npair + len(after_ops):]
        res_ref = None
        if res is not None:
            res_ref, rest = rest[0], rest[1:]
        if rms_bwd is not None:
            dg_ref = rest[4]

            @pl.when(pl.program_id(0) == 0)
            def _():
                dg_ref[...] = jnp.zeros_like(dg_ref)

        for r0 in range(0, tm, rc):
            rows = pl.ds(r0, rc)
            acc = None
            for p in range(npair):
                a_ = refs[2 * p][rows, :].astype(BF16)
                d = _dot(a_, refs[2 * p + 1][...]) if modes[p] == "nn" else _dot_nt(a_, refs[2 * p + 1][...])
                acc = d if acc is None else acc + d
            if res_ref is not None:
                acc = acc + res_ref[rows, :]
            if rms_bwd is not None:
                h_ref, g_ref, dres_ref, dh_ref, _ = rest
                x = h_ref[rows, :]
                r = lax.rsqrt(jnp.mean(x * x, axis=-1, keepdims=True) + RMS_EPS)
                xr = x * r
                dg_ref[...] += jnp.sum(acc * xr, axis=0, keepdims=True)
                u = acc * g_ref[...]
                dh_ref[rows, :] = r * u - xr * (r * jnp.mean(u * xr, axis=-1, keepdims=True)) + dres_ref[rows, :]
            elif rms_g is not None:
                g_ref, o_ref, n_ref = rest
                o_ref[rows, :] = acc
                r = lax.rsqrt(jnp.mean(acc * acc, axis=-1, keepdims=True) + RMS_EPS)
                n_ref[rows, :] = ((acc * r) * g_ref[...]).astype(BF16)
            else:
                rest[0][rows, :] = acc.astype(rest[0].dtype)

    in_specs, ins = [], []
    for a, w, mode in pairs:
        K = a.shape[1]
        in_specs.append(pl.BlockSpec((tm, K), lambda i, j: (i, 0)))
        once = _RESIDENT if tn == N else None
        in_specs.append(pl.BlockSpec((K, tn), lambda i, j: (0, j), pipeline_mode=once) if mode == "nn"
                        else pl.BlockSpec((tn, K), lambda i, j: (j, 0), pipeline_mode=once))
        ins += [a, w]
    in_specs += after_specs
    ins += after_ops
    tile = pl.BlockSpec((tm, tn), lambda i, j: (i, j))
    vec = pl.BlockSpec((1, tn), lambda i, j: (0, j))
    if res is not None:
        in_specs.append(tile)
        ins.append(res)
    sem = ("parallel", "parallel")
    if rms_bwd is not None:
        in_specs += [tile, vec, tile]
        ins += list(rms_bwd)
        out_specs = [tile, vec]
        out_shape = [jax.ShapeDtypeStruct((M, N), F32), jax.ShapeDtypeStruct((1, N), F32)]
        sem = ("arbitrary", "arbitrary")
    elif rms_g is not None:
        in_specs.append(vec)
        ins.append(rms_g)
        out_specs = [tile, tile]
        out_shape = [jax.ShapeDtypeStruct((M, N), F32), jax.ShapeDtypeStruct((M, N), BF16)]
    else:
        out_specs = tile
        out_shape = jax.ShapeDtypeStruct((M, N), out_dtype)
    return pl.pallas_call(
        body, grid=(M // tm, N // tn), in_specs=in_specs, out_specs=out_specs, out_shape=out_shape,
        compiler_params=_cp(*sem), name=name)(*ins)


def _mm_tn(a, b, name, ts=None, tn=None):
    S, K = a.shape
    N = b.shape[1]
    ts = _pick(S, ts or TS_MM, 16)
    tn = _pick(N, tn or TN_MM, 128)
    nsteps = S // ts

    def body(a_ref, b_ref, o_ref, acc_ref):
        s = pl.program_id(1)

        @pl.when(s == 0)
        def _():
            acc_ref[...] = jnp.zeros_like(acc_ref)

        acc_ref[...] += _dot_tn(a_ref[...].astype(BF16), b_ref[...].astype(BF16))

        @pl.when(s == nsteps - 1)
        def _():
            o_ref[...] = acc_ref[...].astype(o_ref.dtype)

    return pl.pallas_call(
        body, grid=(N // tn, nsteps),
        in_specs=[pl.BlockSpec((ts, K), lambda j, s: (s, 0)), pl.BlockSpec((ts, tn), lambda j, s: (s, j))],
        out_specs=pl.BlockSpec((K, tn), lambda j, s: (0, j)), out_shape=jax.ShapeDtypeStruct((K, N), BF16),
        scratch_shapes=[pltpu.VMEM((K, tn), F32)],
        compiler_params=_cp("parallel", "arbitrary"), name=name)(a, b)


def _col_chunk(n):
    return 256 if n % 256 == 0 else 128


def _ffn_up(n, wgt, wut, name, after=None):
    S, D = n.shape
    F = wgt.shape[0]
    tm = _pick(S, TS_FFN, 16)
    ce = _col_chunk(F)
    after_specs, after_ops = _after(after)

    def body(n_ref, wg_ref, wu_ref, *rest):
        a_ref, b_ref, hid_ref = rest[-3:]
        x = n_ref[...]
        for c0 in range(0, F, ce):
            a = _dot_nt(x, wg_ref[c0:c0 + ce, :])
            b = _dot_nt(x, wu_ref[c0:c0 + ce, :])
            a_ref[:, c0:c0 + ce] = a.astype(BF16)
            b_ref[:, c0:c0 + ce] = b.astype(BF16)
            hid_ref[:, c0:c0 + ce] = (a * _sigmoid(a) * b).astype(BF16)

    wspec = pl.BlockSpec((F, D), lambda i: (0, 0), pipeline_mode=_RESIDENT)
    ospec = pl.BlockSpec((tm, F), lambda i: (i, 0))
    osh = jax.ShapeDtypeStruct((S, F), BF16)
    return pl.pallas_call(
        body, grid=(S // tm,),
        in_specs=[pl.BlockSpec((tm, D), lambda i: (i, 0)), wspec, wspec] + after_specs,
        out_specs=[ospec, ospec, ospec], out_shape=[osh, osh, osh],
        compiler_params=_cp("parallel"), name=name)(n, wgt, wut, *after_ops)


def _ffn_dhid(dh, wd, a, b, name):
    S, D = dh.shape
    F = wd.shape[0]
    tm = _pick(S, TS_FFN, 16)
    ce = _col_chunk(F)

    def body(dh_ref, wd_ref, a_ref, b_ref, da_ref, db_ref):
        x = dh_ref[...].astype(BF16)
        for c0 in range(0, F, ce):
            g = _dot_nt(x, wd_ref[c0:c0 + ce, :])
            a_ = a_ref[:, c0:c0 + ce].astype(F32)
            b_ = b_ref[:, c0:c0 + ce].astype(F32)
            sg = _sigmoid(a_)
            da_ref[:, c0:c0 + ce] = (g * b_ * (sg * (1.0 + a_ * (1.0 - sg)))).astype(BF16)
            db_ref[:, c0:c0 + ce] = (g * (a_ * sg)).astype(BF16)

    tile = pl.BlockSpec((tm, F), lambda i: (i, 0))
    osh = jax.ShapeDtypeStruct((S, F), BF16)
    return pl.pallas_call(
        body, grid=(S // tm,),
        in_specs=[pl.BlockSpec((tm, D), lambda i: (i, 0)),
                  pl.BlockSpec((F, D), lambda i: (0, 0), pipeline_mode=_RESIDENT), tile, tile],
        out_specs=[tile, tile], out_shape=[osh, osh],
        compiler_params=_cp("parallel"), name=name)(dh, wd, a, b)


def _softmax_rows(s):
    m = jnp.max(s, axis=-1, keepdims=True)
    p = jnp.exp(s - m)
    return p / jnp.sum(p, axis=-1, keepdims=True)


def _attn_fwd(q, k, v, name):
    S, D = q.shape
    M = k.shape[0]
    hd = D // XA_HEADS
    scale = hd ** -0.5
    ts = _pick(S, TS_ATTN, 16)

    def body(q_ref, k_ref, v_ref, o_ref):
        for h in range(XA_HEADS):
            sl = slice(h * hd, (h + 1) * hd)
            p = _softmax_rows(_dot_nt(q_ref[:, sl], k_ref[:, sl]) * scale)
            o_ref[:, sl] = _dot(p.astype(BF16), v_ref[:, sl]).astype(BF16)

    tile = pl.BlockSpec((ts, D), lambda i: (i, 0))
    memspec = pl.BlockSpec((M, D), lambda i: (0, 0))
    return pl.pallas_call(
        body, grid=(S // ts,), in_specs=[tile, memspec, memspec], out_specs=tile,
        out_shape=jax.ShapeDtypeStruct((S, D), BF16), compiler_params=_cp("parallel"), name=name)(q, k, v)


def _attn_bwd(q, k, v, do, name):
    S, D = q.shape
    M = k.shape[0]
    hd = D // XA_HEADS
    scale = hd ** -0.5
    ts = _pick(S, TS_ATTN, 16)

    def body(q_ref, k_ref, v_ref, do_ref, dq_ref, dk_ref, dv_ref):
        @pl.when(pl.program_id(0) == 0)
        def _():
            dk_ref[...] = jnp.zeros_like(dk_ref)
            dv_ref[...] = jnp.zeros_like(dv_ref)

        for h in range(XA_HEADS):
            sl = slice(h * hd, (h + 1) * hd)
            qh, kh, vh, doh = q_ref[:, sl], k_ref[:, sl], v_ref[:, sl], do_ref[:, sl]
            p = _softmax_rows(_dot_nt(qh, kh) * scale)
            dp = _dot_nt(doh, vh)
            dv_ref[:, sl] += _dot_tn(p.astype(BF16), doh)
            delta = jnp.sum(dp * p, axis=-1, keepdims=True)
            ds = (p * (dp - delta) * scale).astype(BF16)
            dq_ref[:, sl] = _dot(ds, kh).astype(BF16)
            dk_ref[:, sl] += _dot_tn(ds, qh)

    tile = pl.BlockSpec((ts, D), lambda i: (i, 0))
    memspec = pl.BlockSpec((M, D), lambda i: (0, 0))
    return pl.pallas_call(
        body, grid=(S // ts,), in_specs=[tile, memspec, memspec, tile], out_specs=[tile, memspec, memspec],
        out_shape=[jax.ShapeDtypeStruct((S, D), BF16), jax.ShapeDtypeStruct((M, D), F32),
                   jax.ShapeDtypeStruct((M, D), F32)],
        compiler_params=_cp("arbitrary"), name=name)(q, k, v, do)


def _halo_specs(ts, width, col):
    per = ts // HALO

    def prev(i):
        return (jnp.maximum(i * per - 1, 0), col)

    def nxt(i, n_tiles):
        return (jnp.minimum((i + 1) * per, n_tiles * per - 1), col)

    return prev, nxt


def _fill_ext(ext_ref, prev_val, main_val, next_val, first, last, ts):
    ext_ref[pl.ds(0, HALO), :] = jnp.where(first, 0.0, prev_val)
    ext_ref[pl.ds(HALO, ts), :] = main_val
    ext_ref[pl.ds(HALO + ts, HALO), :] = jnp.where(last, 0.0, next_val)


SUBLANES = 8


def _fill_shifted(sh_ref, ts):
    n = ts + 2 * HALO - SUBLANES
    for s in range(1, SUBLANES):
        sh_ref[s, pl.ds(0, n), :] = sh_ref[0, pl.ds(s, n), :]


def _tap(sh_ref, r0, offset, rc):
    q, s = divmod(offset, SUBLANES)
    return sh_ref[s, pl.ds(pl.multiple_of(r0 + SUBLANES * q, SUBLANES), rc), :]


def _conv_fwd(z, wa, ba, lng, lnb, wb, bb, name, after=None):
    S = z.shape[0]
    C = z.shape[1] // 5
    KA, KB = wa.shape[0], wb.shape[0]
    pa, pb = KA // 2, KB // 2
    assert pa <= HALO and pb <= HALO
    ts = _pick(S, TS_CONV, ROW_CHUNK)
    nt = S // ts
    rc = ROW_CHUNK
    prev, nxt = _halo_specs(ts, 5 * C, 0)
    after_specs, after_ops = _after(after)

    def body(*refs):
        compute(*refs[:9], *refs[9 + len(after_ops):])

    def compute(z_ref, zp_ref, zn_ref, wa_ref, ba_ref, lng_ref, lnb_ref, wb_ref, bb_ref, ab_ref, ca_ref,
                ga_sh, tb_ext, win_b):
        i = pl.program_id(0)
        first, last = i == 0, i == nt - 1

        def glu(r):
            return r[:, 0:C] * _sigmoid(r[:, C:2 * C])

        def gcb(r):
            return r[:, 4 * C:5 * C] * r[:, 2 * C:3 * C]

        _fill_ext(ga_sh.at[0], glu(zp_ref), glu(z_ref), glu(zn_ref), first, last, ts)
        _fill_shifted(ga_sh, ts)
        _fill_ext(tb_ext, gcb(zp_ref), gcb(z_ref), gcb(zn_ref), first, last, ts)

        def chunk(c, carry):
            r0 = pl.multiple_of(c * rc, rc)
            win_b[...] = tb_ext[pl.ds(r0, rc + 2 * HALO), :]
            acc = jnp.zeros((rc, C), F32)
            for k in range(KA):
                acc = acc + wa_ref[k:k + 1, :] * _tap(ga_sh, r0, HALO - pa + k, rc)
            ca = acc + ba_ref[...]
            ca_ref[pl.ds(r0, rc), :] = ca
            mu = jnp.mean(ca, axis=-1, keepdims=True)
            xc = ca - mu
            var = jnp.mean(xc * xc, axis=-1, keepdims=True)
            ln = xc * lax.rsqrt(var + LN_EPS) * lng_ref[...] + lnb_ref[...]
            ab_ref[pl.ds(r0, rc), 0:C] = (ln * _sigmoid(ln)).astype(BF16)
            cb = jnp.zeros((rc, C), F32) + bb_ref[...]
            for k in range(KB):
                cb = cb + wb_ref[k:k + 1, :] * win_b[pl.ds(HALO - pb + k, rc), :]
            ab_ref[pl.ds(r0, rc), C:2 * C] = (z_ref[pl.ds(r0, rc), 3 * C:4 * C] * cb).astype(BF16)
            return carry

        lax.fori_loop(0, ts // rc, chunk, 0)

    zspec = pl.BlockSpec((ts, 5 * C), lambda i: (i, 0))
    zprev = pl.BlockSpec((HALO, 5 * C), prev)
    znext = pl.BlockSpec((HALO, 5 * C), lambda i: nxt(i, nt))

    def full(a):
        return pl.BlockSpec(a.shape, lambda i: (0, 0))

    return pl.pallas_call(
        body, grid=(nt,),
        in_specs=[zspec, zprev, znext, full(wa), full(ba), full(lng), full(lnb), full(wb), full(bb)] + after_specs,
        out_specs=[pl.BlockSpec((ts, 2 * C), lambda i: (i, 0)), pl.BlockSpec((ts, C), lambda i: (i, 0))],
        out_shape=[jax.ShapeDtypeStruct((S, 2 * C), BF16), jax.ShapeDtypeStruct((S, C), F32)],
        scratch_shapes=[pltpu.VMEM((SUBLANES, ts + 2 * HALO, C), F32), pltpu.VMEM((ts + 2 * HALO, C), F32),
                        pltpu.VMEM((rc + 2 * HALO, C), F32)],
        compiler_params=_cp("parallel"), name=name)(z, z, z, wa, ba, lng, lnb, wb, bb, *after_ops)


def _conv_bwd_ln(dab, ca, lng, lnb, name):
    S, C = ca.shape
    ts = _pick(S, TS_ROW, 8)

    def body(da_ref, ca_ref, lng_ref, lnb_ref, dca_ref, dg_ref, db_ref, dbias_ref):
        @pl.when(pl.program_id(0) == 0)
        def _():
            dg_ref[...] = jnp.zeros_like(dg_ref)
            db_ref[...] = jnp.zeros_like(db_ref)
            dbias_ref[...] = jnp.zeros_like(dbias_ref)

        ca_ = ca_ref[...]
        mu = jnp.mean(ca_, axis=-1, keepdims=True)
        xc = ca_ - mu
        rstd = lax.rsqrt(jnp.mean(xc * xc, axis=-1, keepdims=True) + LN_EPS)
        xh = xc * rstd
        ln = xh * lng_ref[...] + lnb_ref[...]
        sg = _sigmoid(ln)
        dln = da_ref[...].astype(F32) * (sg * (1.0 + ln * (1.0 - sg)))
        dg_ref[...] += jnp.sum(dln * xh, axis=0, keepdims=True)
        db_ref[...] += jnp.sum(dln, axis=0, keepdims=True)
        dxh = dln * lng_ref[...]
        dca = rstd * (dxh - jnp.mean(dxh, axis=-1, keepdims=True) - xh * jnp.mean(dxh * xh, axis=-1, keepdims=True))
        dca_ref[...] = dca
        dbias_ref[...] += jnp.sum(dca, axis=0, keepdims=True)

    tile = pl.BlockSpec((ts, C), lambda i: (i, 0))
    vec = pl.BlockSpec((1, C), lambda i: (0, 0))
    vsh = jax.ShapeDtypeStruct((1, C), F32)
    return pl.pallas_call(
        body, grid=(S // ts,), in_specs=[tile, tile, vec, vec], out_specs=[tile, vec, vec, vec],
        out_shape=[jax.ShapeDtypeStruct((S, C), F32), vsh, vsh, vsh],
        compiler_params=_cp("arbitrary"), name=name)(dab, ca, lng, lnb)


def _conv_bwd(z, dca, dab, wa, wb, bb, name):
    S = z.shape[0]
    C = z.shape[1] // 5
    KA, KB = wa.shape[0], wb.shape[0]
    pa, pb = KA // 2, KB // 2
    ts = _pick(S, TS_CONV, ROW_CHUNK)
    nt = S // ts
    rc = ROW_CHUNK
    prev0, nxt0 = _halo_specs(ts, C, 0)
    prev1, nxt1 = _halo_specs(ts, C, 1)

    def body(z_ref, zp_ref, zn_ref, dca_ref, dcap_ref, dcan_ref, db_ref, dbp_ref, dbn_ref, wa_ref, wb_ref, bb_ref,
             dz_ref, dwa_ref, dwb_ref, dbb_ref,
             ga_sh, dca_sh, tb_ext, dcb_ext, win_tb, win_dcb, acc_a, acc_b, acc_bias):
        i = pl.program_id(0)
        first, last = i == 0, i == nt - 1

        @pl.when(first)
        def _():
            acc_a[...] = jnp.zeros_like(acc_a)
            acc_b[...] = jnp.zeros_like(acc_b)
            acc_bias[...] = jnp.zeros_like(acc_bias)

        def glu(r):
            return r[:, 0:C] * _sigmoid(r[:, C:2 * C])

        def gcb(r):
            return r[:, 4 * C:5 * C] * r[:, 2 * C:3 * C]

        def dcb(d, r):
            return d[...].astype(F32) * r[:, 3 * C:4 * C]

        _fill_ext(ga_sh.at[0], glu(zp_ref), glu(z_ref), glu(zn_ref), first, last, ts)
        _fill_shifted(ga_sh, ts)
        _fill_ext(dca_sh.at[0], dcap_ref[...], dca_ref[...], dcan_ref[...], first, last, ts)
        _fill_shifted(dca_sh, ts)
        _fill_ext(tb_ext, gcb(zp_ref), gcb(z_ref), gcb(zn_ref), first, last, ts)
        _fill_ext(dcb_ext, dcb(dbp_ref, zp_ref), dcb(db_ref, z_ref), dcb(dbn_ref, zn_ref), first, last, ts)

        def fold(x):
            return jnp.sum(x.reshape(rc // 8, 8, C), axis=0)

        def chunk(c, carry):
            r0 = pl.multiple_of(c * rc, rc)
            win_tb[...] = tb_ext[pl.ds(r0, rc + 2 * HALO), :]
            win_dcb[...] = dcb_ext[pl.ds(r0, rc + 2 * HALO), :]
            dca_c = _tap(dca_sh, r0, HALO, rc)
            dglu = jnp.zeros((rc, C), F32)
            for k in range(KA):
                dglu = dglu + wa_ref[k:k + 1, :] * _tap(dca_sh, r0, HALO + pa - k, rc)
                acc_a[k] += fold(dca_c * _tap(ga_sh, r0, HALO - pa + k, rc))
            val = z_ref[pl.ds(r0, rc), 0:C]
            sg = _sigmoid(z_ref[pl.ds(r0, rc), C:2 * C])
            dz_ref[pl.ds(r0, rc), 0:C] = (dglu * sg).astype(BF16)
            dz_ref[pl.ds(r0, rc), C:2 * C] = (dglu * val * sg * (1.0 - sg)).astype(BF16)
            dcb_c = win_dcb[pl.ds(HALO, rc), :]
            cb = jnp.zeros((rc, C), F32) + bb_ref[...]
            dt = jnp.zeros((rc, C), F32)
            for k in range(KB):
                tb_k = win_tb[pl.ds(HALO - pb + k, rc), :]
                cb = cb + wb_ref[k:k + 1, :] * tb_k
                dt = dt + wb_ref[k:k + 1, :] * win_dcb[pl.ds(HALO + pb - k, rc), :]
                acc_b[k] += fold(dcb_c * tb_k)
            acc_bias[...] += fold(dcb_c)
            db_c = db_ref[pl.ds(r0, rc), :].astype(F32)
            dz_ref[pl.ds(r0, rc), 2 * C:3 * C] = (dt * z_ref[pl.ds(r0, rc), 4 * C:5 * C]).astype(BF16)
            dz_ref[pl.ds(r0, rc), 3 * C:4 * C] = (db_c * cb).astype(BF16)
            dz_ref[pl.ds(r0, rc), 4 * C:5 * C] = (dt * z_ref[pl.ds(r0, rc), 2 * C:3 * C]).astype(BF16)
            return carry

        lax.fori_loop(0, ts // rc, chunk, 0)

        @pl.when(last)
        def _():
            dwa_ref[...] = jnp.sum(acc_a[...], axis=1)
            dwb_ref[...] = jnp.sum(acc_b[...], axis=1)
            dbb_ref[...] = jnp.sum(acc_bias[...], axis=0, keepdims=True)

    zspec = pl.BlockSpec((ts, 5 * C), lambda i: (i, 0))
    zprev = pl.BlockSpec((HALO, 5 * C), prev0)
    znext = pl.BlockSpec((HALO, 5 * C), lambda i: nxt0(i, nt))
    dspec = pl.BlockSpec((ts, C), lambda i: (i, 0))
    dprev = pl.BlockSpec((HALO, C), prev0)
    dnext = pl.BlockSpec((HALO, C), lambda i: nxt0(i, nt))
    bspec = pl.BlockSpec((ts, C), lambda i: (i, 1))
    bprev = pl.BlockSpec((HALO, C), prev1)
    bnext = pl.BlockSpec((HALO, C), lambda i: nxt1(i, nt))

    def full(shape):
        return pl.BlockSpec(shape, lambda i: (0,) * len(shape))

    ext = pltpu.VMEM((ts + 2 * HALO, C), F32)
    shifted = pltpu.VMEM((SUBLANES, ts + 2 * HALO, C), F32)
    win = pltpu.VMEM((rc + 2 * HALO, C), F32)
    return pl.pallas_call(
        body, grid=(nt,),
        in_specs=[zspec, zprev, znext, dspec, dprev, dnext, bspec, bprev, bnext,
                  full(wa.shape), full(wb.shape), full(bb.shape)],
        out_specs=[pl.BlockSpec((ts, 5 * C), lambda i: (i, 0)), full((KA, C)), full((KB, C)), full((1, C))],
        out_shape=[jax.ShapeDtypeStruct((S, 5 * C), BF16), jax.ShapeDtypeStruct((KA, C), F32),
                   jax.ShapeDtypeStruct((KB, C), F32), jax.ShapeDtypeStruct((1, C), F32)],
        scratch_shapes=[shifted, shifted, ext, ext, win, win,
                        pltpu.VMEM((KA, 8, C), F32), pltpu.VMEM((KB, 8, C), F32), pltpu.VMEM((8, C), F32)],
        compiler_params=_cp("arbitrary"), name=name)(z, z, z, dca, dca, dca, dab, dab, dab, wa, wb, bb)


_GELU_C = 0.7978845608028654
_GELU_A = 0.044715


def _gelu(x):
    return 0.5 * x * (1.0 + jnp.tanh(_GELU_C * (x + _GELU_A * (x * x * x))))


def _gelu_and_grad(x):
    t = jnp.tanh(_GELU_C * (x + _GELU_A * (x * x * x)))
    hx = 0.5 * x
    return hx * (1.0 + t), 0.5 * (1.0 + t) + hx * (1.0 - t * t) * (_GELU_C * (1.0 + 3.0 * _GELU_A * x * x))


def _sgu_fwd(zp, lng, lnb, ws, bsb, name):
    S = zp.shape[0]
    D = zp.shape[1] // 2
    G = ws.shape[0]
    gd = D // G
    ts = _pick(S, TS_SGU, CHUNK)
    ncs = ts // CHUNK

    def body(zp_ref, lng_ref, lnb_ref, ws_ref, bsb_ref, y_ref, vb_ref):
        v = _gelu(zp_ref[:, D:2 * D])
        mu = jnp.mean(v, axis=-1, keepdims=True)
        xc = v - mu
        rstd = lax.rsqrt(jnp.mean(xc * xc, axis=-1, keepdims=True) + LN_EPS)
        vb_ref[...] = (xc * rstd * lng_ref[...] + lnb_ref[...]).astype(BF16)
        for c in range(ncs):
            rows = slice(c * CHUNK, (c + 1) * CHUNK)
            for g in range(G):
                cols = slice(g * gd, (g + 1) * gd)
                sv = _dot(ws_ref[g], vb_ref[rows, cols]) + bsb_ref[:, cols]
                y_ref[rows, cols] = (_gelu(zp_ref[rows, cols]) * sv).astype(BF16)

    def full(a):
        return pl.BlockSpec(a.shape, lambda i: (0,) * a.ndim)

    return pl.pallas_call(
        body, grid=(S // ts,),
        in_specs=[pl.BlockSpec((ts, 2 * D), lambda i: (i, 0)), full(lng), full(lnb), full(ws), full(bsb)],
        out_specs=pl.BlockSpec((ts, D), lambda i: (i, 0)), out_shape=jax.ShapeDtypeStruct((S, D), BF16),
        scratch_shapes=[pltpu.VMEM((ts, D), BF16)],
        compiler_params=_cp("parallel"), name=name)(zp, lng, lnb, ws, bsb)


def _sgu_bwd(dy, zp, lng, lnb, ws, wst, bsb, name):
    S = zp.shape[0]
    D = zp.shape[1] // 2
    G = ws.shape[0]
    gd = D // G
    ts = _pick(S, TS_SGU, CHUNK)
    ncs = ts // CHUNK

    def body(dy_ref, zp_ref, lng_ref, lnb_ref, ws_ref, wst_ref, bsb_ref,
             dzp_ref, dws_ref, dbs_ref, dg_ref, db_ref, vb_ref, dvln_ref, acc_bs):
        i = pl.program_id(0)

        @pl.when(i == 0)
        def _():
            dws_ref[...] = jnp.zeros_like(dws_ref)
            acc_bs[...] = jnp.zeros_like(acc_bs)
            dg_ref[...] = jnp.zeros_like(dg_ref)
            db_ref[...] = jnp.zeros_like(db_ref)

        v, dv_dz = _gelu_and_grad(zp_ref[:, D:2 * D])
        mu = jnp.mean(v, axis=-1, keepdims=True)
        xc = v - mu
        rstd = lax.rsqrt(jnp.mean(xc * xc, axis=-1, keepdims=True) + LN_EPS)
        xh = xc * rstd
        vb_ref[...] = (xh * lng_ref[...] + lnb_ref[...]).astype(BF16)
        for c in range(ncs):
            rows = slice(c * CHUNK, (c + 1) * CHUNK)
            for g in range(G):
                cols = slice(g * gd, (g + 1) * gd)
                u, du_dz = _gelu_and_grad(zp_ref[rows, cols])
                dy_ = dy_ref[rows, cols].astype(F32)
                sv = _dot(ws_ref[g], vb_ref[rows, cols]) + bsb_ref[:, cols]
                dzp_ref[rows, cols] = (dy_ * sv * du_dz).astype(BF16)
                dsv = dy_ * u
                acc_bs[:, cols] += dsv
                dsvb = dsv.astype(BF16)
                dws_ref[g] += _dot_nt(dsvb, vb_ref[rows, cols])
                dvln_ref[rows, cols] = _dot(wst_ref[g], dsvb)
        dvln = dvln_ref[...]
        dg_ref[...] += jnp.sum(dvln * xh, axis=0, keepdims=True)
        db_ref[...] += jnp.sum(dvln, axis=0, keepdims=True)
        dxh = dvln * lng_ref[...]
        dv = rstd * (dxh - jnp.mean(dxh, axis=-1, keepdims=True) - xh * jnp.mean(dxh * xh, axis=-1, keepdims=True))
        dzp_ref[:, D:2 * D] = (dv * dv_dz).astype(BF16)

        @pl.when(i == pl.num_programs(0) - 1)
        def _():
            dbs_ref[...] = acc_bs[...]

    def full(shape):
        return pl.BlockSpec(shape, lambda i: (0,) * len(shape))

    return pl.pallas_call(
        body, grid=(S // ts,),
        in_specs=[pl.BlockSpec((ts, D), lambda i: (i, 0)), pl.BlockSpec((ts, 2 * D), lambda i: (i, 0)),
                  full(lng.shape), full(lnb.shape), full(ws.shape), full(wst.shape), full(bsb.shape)],
        out_specs=[pl.BlockSpec((ts, 2 * D), lambda i: (i, 0)), full(ws.shape), full(bsb.shape),
                   full((1, D)), full((1, D))],
        out_shape=[jax.ShapeDtypeStruct((S, 2 * D), BF16), jax.ShapeDtypeStruct(ws.shape, F32),
                   jax.ShapeDtypeStruct(bsb.shape, F32), jax.ShapeDtypeStruct((1, D), F32),
                   jax.ShapeDtypeStruct((1, D), F32)],
        scratch_shapes=[pltpu.VMEM((ts, D), BF16), pltpu.VMEM((ts, D), F32),
                        pltpu.VMEM(bsb.shape, F32)],
        compiler_params=_cp("arbitrary"), name=name)(dy, zp, lng, lnb, ws, wst, bsb)


def _group_sum(x, groups, name):
    P, D = x.shape
    gd = D // groups

    def body(x_ref, o_ref):
        for g in range(groups):
            o_ref[:, g:g + 1] = jnp.sum(x_ref[:, g * gd:(g + 1) * gd], axis=1, keepdims=True)

    return pl.pallas_call(body, out_shape=jax.ShapeDtypeStruct((P, groups), F32), name=name)(x)


def _adamw(w, g, m, v, name):
    shape = w.shape
    C = shape[-1]
    R = w.size // C
    tr = _pick(R, 1024, 8)

    def body(w_ref, g_ref, m_ref, v_ref, d_ref, nm_ref, nv_ref):
        d_ref[...], nm_ref[...], nv_ref[...] = _adamw_math(w_ref[...], g_ref[...], m_ref[...], v_ref[...])

    tile = pl.BlockSpec((tr, C), lambda i: (i, 0))
    sh = jax.ShapeDtypeStruct((R, C), F32)
    outs = pl.pallas_call(
        body, grid=(R // tr,), in_specs=[tile] * 4, out_specs=[tile] * 3, out_shape=[sh] * 3,
        compiler_params=_cp("parallel"), name=name)(*(a.reshape(R, C) for a in (w, g, m, v)))
    return tuple(o.reshape(shape) for o in outs)


_HBM = pl.BlockSpec(memory_space=pltpu.HBM)


def _remote(src, dst, send_sem, recv_sem, to):
    return pltpu.make_async_remote_copy(src_ref=src, dst_ref=dst, send_sem=send_sem, recv_sem=recv_sem,
                                        device_id=to, device_id_type=MESH)


def _all_gather(block, name):
    R, C = block.shape

    def body(x_ref, out_ref, send_sems, recv_sems, local_sem):
        x, y, c = lax.axis_index("x"), lax.axis_index("y"), lax.axis_index("c")
        me, sibling = (x, y, c), (x, y, 1 - c)
        chips = [(1 - x, y), (x, 1 - y), (1 - x, 1 - y)]

        def slot(px, py, pc):
            return out_ref.at[4 * px + 2 * py + pc]

        def copy(k, blk, to, src=None):
            return _remote(slot(*blk) if src is None else src, slot(*blk), send_sems.at[k], recv_sems.at[k], to)

        mine = pltpu.make_async_copy(x_ref, slot(*me), local_sem)
        mine.start()
        first = [copy(0, me, sibling, src=x_ref)]
        first += [copy(1 + j, me, (*chip, c), src=x_ref) for j, chip in enumerate(chips)]
        for cp in first:
            cp.start()
        passed = [copy(4 + j, (*chip, c), sibling) for j, chip in enumerate(chips)]
        for j, chip in enumerate(chips):
            copy(1 + j, (*chip, c), me).wait_recv()
            passed[j].start()
        copy(0, sibling, me).wait_recv()
        for j, chip in enumerate(chips):
            copy(4 + j, (*chip, 1 - c), me).wait_recv()
        for cp in first + passed:
            cp.wait_send()
        mine.wait()

    return pl.pallas_call(
        body, out_shape=jax.ShapeDtypeStruct((NDEV, R, C), block.dtype), in_specs=[_HBM], out_specs=_HBM,
        scratch_shapes=[pltpu.SemaphoreType.DMA((7,)), pltpu.SemaphoreType.DMA((7,)), pltpu.SemaphoreType.DMA],
        name=name)(block)


def _all_gather_weights(pack, rows, name, after=None):
    C = pack.shape[1]
    nw = len(rows)
    starts = [sum(rows[:w]) for w in range(nw)]
    after_specs, after_ops = _after(after)

    def body(pack_ref, *rest):
        rest = rest[len(after_ops):]
        outs = rest[:nw]
        send_sems, recv_sems, local_sem = rest[nw:]
        x, y, c = lax.axis_index("x"), lax.axis_index("y"), lax.axis_index("c")
        me, sibling = (x, y, c), (x, y, 1 - c)
        chips = [(1 - x, y), (x, 1 - y), (1 - x, 1 - y)]

        def block(w, px, py, pc):
            return outs[w].at[pl.ds((4 * px + 2 * py + pc) * rows[w], rows[w])]

        def mine(w):
            return pack_ref.at[pl.ds(starts[w], rows[w])]

        def all_of(k):
            return _remote(pack_ref, pack_ref, send_sems.at[k], recv_sems.at[k], me)

        for w in range(nw):
            pltpu.make_async_copy(mine(w), block(w, *me), local_sem).start()
        for k, to in enumerate([sibling] + [(*chip, c) for chip in chips]):
            for w in range(nw):
                _remote(mine(w), block(w, *me), send_sems.at[k], recv_sems.at[k], to).start()
        for j, chip in enumerate(chips):
            all_of(1 + j).wait_recv()
            for w in range(nw):
                _remote(block(w, *chip, c), block(w, *chip, c), send_sems.at[4 + j], recv_sems.at[4 + j], sibling).start()
        all_of(0).wait_recv()
        for j in range(3):
            all_of(4 + j).wait_recv()
        for k in range(7):
            all_of(k).wait_send()
        pltpu.make_async_copy(pack_ref, pack_ref, local_sem).wait()

    return pl.pallas_call(
        body, out_shape=[jax.ShapeDtypeStruct((NDEV * r, C), pack.dtype) for r in rows],
        in_specs=[_HBM] + after_specs, out_specs=[_HBM] * nw,
        scratch_shapes=[pltpu.SemaphoreType.DMA((7,)), pltpu.SemaphoreType.DMA((7,)), pltpu.SemaphoreType.DMA],
        name=name)(pack, *after_ops)


_SEM = pl.BlockSpec(memory_space=pltpu.SEMAPHORE)
_DATAFLOW = pltpu.SideEffectType.DATAFLOW_SIDE_EFFECTING


def _split_start(srcs, lands, plan, n, after, name):
    nbuf = len(srcs) + len(lands)
    after_specs, after_ops = _after(after)

    def body(*refs):
        src_refs, land_refs = refs[:len(srcs)], refs[len(srcs):nbuf]
        send_sems, recv_sems = refs[nbuf + len(after_ops)], refs[nbuf + len(after_ops) + 1]
        for k, (src, dst, to) in enumerate(plan(src_refs, land_refs)):
            _remote(src, dst, send_sems.at[k], recv_sems.at[k], to).start()
        refs[-1][...] = jnp.zeros_like(refs[-1])

    bufs = [pltpu.with_memory_space_constraint(a, pltpu.HBM) for a in list(srcs) + list(lands)]
    outs = pl.pallas_call(
        body, name=name,
        out_shape=(pltpu.SemaphoreType.DMA((n,)), pltpu.SemaphoreType.DMA((n,)),
                   *[pltpu.HBM(a.shape, a.dtype) for a in bufs], jax.ShapeDtypeStruct((8, 128), F32)),
        in_specs=[_HBM] * nbuf + after_specs,
        out_specs=(_SEM, _SEM, *[_HBM] * nbuf, pl.BlockSpec(memory_space=pltpu.VMEM)),
        input_output_aliases={i: 2 + i for i in range(nbuf)},
        compiler_params=pltpu.CompilerParams(has_side_effects=_DATAFLOW))(*bufs, *after_ops)
    return outs[0], outs[1], list(outs[2:2 + len(srcs)]), list(outs[2 + len(srcs):2 + nbuf]), outs[-1]


def _split_wait(send_sems, recv_sems, srcs, lands, plan, after, name):
    nbuf = len(srcs) + len(lands)
    after_specs, after_ops = _after(after)

    def body(*refs):
        src_refs, land_refs = refs[:len(srcs)], refs[len(srcs):nbuf]
        send_sems_ref, recv_sems_ref = refs[nbuf], refs[nbuf + 1]
        for k, (src, dst, to) in enumerate(plan(src_refs, land_refs)):
            copy = _remote(src, dst, send_sems_ref.at[k], recv_sems_ref.at[k], to)
            copy.wait_send()
            copy.wait_recv()

    outs = pl.pallas_call(
        body, name=name, out_shape=tuple(pltpu.HBM(a.shape, a.dtype) for a in list(srcs) + list(lands)),
        in_specs=[_HBM] * nbuf + [_SEM, _SEM] + after_specs, out_specs=tuple([_HBM] * nbuf),
        input_output_aliases={i: i for i in range(nbuf)},
        compiler_params=pltpu.CompilerParams(has_side_effects=_DATAFLOW))(*srcs, *lands, send_sems, recv_sems, *after_ops)
    return list(outs[:len(srcs)]), list(outs[len(srcs):])


def _peers(x, y, c):
    return [(mask, (1 - x if mask & 4 else x, 1 - y if mask & 2 else y, 1 - c if mask & 1 else c))
            for mask in range(1, NDEV)]


def _gather_plan(rows):
    starts = [sum(rows[:w]) for w in range(len(rows))]

    def plan(src_refs, land_refs):
        x, y, c = lax.axis_index("x"), lax.axis_index("y"), lax.axis_index("c")
        copies = []
        for w, r in enumerate(rows):
            mine = src_refs[0].at[pl.ds(starts[w], r)]
            dst = land_refs[w].at[pl.ds((4 * x + 2 * y + c) * r, r)]
            copies += [(mine, dst, peer) for _, peer in _peers(x, y, c)]
        return copies

    return plan, (NDEV - 1) * len(rows)


def _place_own(shards, fulls, dev_idx, name):
    nw = len(shards)

    def body(i_ref, *refs):
        for w in range(nw):
            refs[2 * nw + w][...] = refs[w][...]

    grid_spec = pltpu.PrefetchScalarGridSpec(
        num_scalar_prefetch=1, grid=(1,),
        in_specs=[pl.BlockSpec(s.shape, lambda t, i_ref: (0, 0)) for s in shards] + [_ANY] * nw,
        out_specs=[pl.BlockSpec(s.shape, lambda t, i_ref: (i_ref[0], 0)) for s in shards])
    outs = pl.pallas_call(
        body, grid_spec=grid_spec, out_shape=[jax.ShapeDtypeStruct(f.shape, f.dtype) for f in fulls],
        input_output_aliases={1 + nw + w: w for w in range(nw)}, name=name)(dev_idx, *shards, *fulls)
    return list(outs)


def _scatter_plan(rows):
    def plan(src_refs, land_refs):
        x, y, c = lax.axis_index("x"), lax.axis_index("y"), lax.axis_index("c")
        copies = []
        for w, r in enumerate(rows):
            for mask, (px, py, pc) in _peers(x, y, c):
                src = src_refs[w].at[pl.ds((4 * px + 2 * py + pc) * r, r)]
                copies.append((src, land_refs[w].at[mask - 1], (px, py, pc)))
        return copies

    return plan, (NDEV - 1) * len(rows)


def _adamw_math(w, g, m, v):
    nm = ADAM_B1 * m + (1.0 - ADAM_B1) * g
    nv = ADAM_B2 * v + (1.0 - ADAM_B2) * (g * g)
    bc1 = 1.0 - ADAM_B1 ** ADAM_STEP
    bc2 = 1.0 - ADAM_B2 ** ADAM_STEP
    return -ADAM_LR * ((nm / bc1) / (jnp.sqrt(nv / bc2) + ADAM_EPS) + ADAM_WD * w), nm, nv


def _finish_weight(gs, gots, dev_idx, w, m, v, name, after=None):
    L = len(gs)
    n1, r, C = gots[0].shape
    block = (None,) + w.shape[1:]

    after_specs, after_ops = _after(after)

    def body(i_ref, *refs):
        ins, (w_ref, m_ref, v_ref), (g_out, d_out, m_out, v_out) = refs[:2 * L], refs[2 * L:2 * L + 3], refs[-4:]
        for layer in range(L):
            @pl.when(pl.program_id(0) == layer)
            def _():
                g_ref, got_ref = ins[2 * layer], ins[2 * layer + 1]
                acc = g_ref[...].astype(F32)
                for k in range(n1):
                    acc = acc + got_ref[k].astype(F32)
                g_out[...] = acc
                d_out[...], m_out[...], v_out[...] = _adamw_math(w_ref[...], acc, m_ref[...], v_ref[...])

    in_specs, ins = [], []
    for g, got in zip(gs, gots):
        in_specs += [pl.BlockSpec((r, C), lambda t, i_ref: (i_ref[0], 0), pipeline_mode=_RESIDENT),
                     pl.BlockSpec((n1, r, C), lambda t, i_ref: (0, 0, 0), pipeline_mode=_RESIDENT)]
        ins += [g, got]
    per_layer = pl.BlockSpec(block, lambda t, i_ref: (t, 0, 0))
    grid_spec = pltpu.PrefetchScalarGridSpec(
        num_scalar_prefetch=1, grid=(L,), in_specs=in_specs + [per_layer] * 3 + after_specs,
        out_specs=[per_layer] * 4)
    return pl.pallas_call(
        body, grid_spec=grid_spec, out_shape=[jax.ShapeDtypeStruct(w.shape, F32)] * 4,
        compiler_params=_cp("arbitrary"), name=name)(dev_idx, *ins, w, m, v, *after_ops)


def _sum_slots(a, name):
    n, R, C = a.shape

    def body(a_ref, o_ref):
        acc = a_ref[0]
        for k in range(1, n):
            acc = acc + a_ref[k]
        o_ref[...] = acc

    return pl.pallas_call(body, out_shape=jax.ShapeDtypeStruct((R, C), F32), name=name)(a)


def _shard_axis(name):
    return {"ev_w_in": 2, "ev_a_conv_w": 2, "ev_b_conv_w": 2, "ev_w_out": 1, "od_w_in": 2, "od_c_ln_g": 1,
            "od_c_ln_b": 1, "od_w_out": 1, "xa_w_q": 1, "xa_w_k": 1, "xa_w_v": 1, "xa_w_o": 1,
            "ffn_w_gate": 2, "ffn_w_up": 2, "ffn_w_down": 1}[name]


BIG = ["ev_w_in", "ev_w_out", "od_w_in", "od_w_out", "xa_w_q", "xa_w_k", "xa_w_v", "xa_w_o",
       "ffn_w_gate", "ffn_w_up", "ffn_w_down"]
SMALL_SHARDED = ["ev_a_conv_w", "ev_b_conv_w", "od_c_ln_g", "od_c_ln_b"]
REPLICATED = ["g_mix", "g_xattn", "g_mem", "g_ffn", "g_final", "ev_a_conv_b", "ev_a_ln_g", "ev_a_ln_b",
              "ev_b_conv_b", "od_w_s", "od_b_s"]
WEIGHTS = ["g_mix", "g_xattn", "g_mem", "g_ffn", "g_final", "ev_w_in", "ev_a_conv_w", "ev_a_conv_b", "ev_a_ln_g",
           "ev_a_ln_b", "ev_b_conv_w", "ev_b_conv_b", "ev_w_out", "od_w_in", "od_c_ln_g", "od_c_ln_b", "od_w_s",
           "od_b_s", "od_w_out", "xa_w_q", "xa_w_k", "xa_w_v", "xa_w_o", "ffn_w_gate", "ffn_w_up", "ffn_w_down"]


def _full_from_blocks(blocks, axis):
    shard = blocks.shape[1:]
    full = jnp.moveaxis(blocks, 0, axis)
    return full.reshape(shard[:axis] + (NDEV * shard[axis],) + shard[axis + 1:])


def _blocks_from_full(full, axis):
    shp = full.shape
    split = full.reshape(shp[:axis] + (NDEV, shp[axis] // NDEV) + shp[axis + 1:])
    return jnp.moveaxis(split, axis, 0)


def _pad_rows(flat, width, row_align):
    per = width * row_align
    n = -(-flat.shape[0] // per) * per
    return jnp.pad(flat, (0, n - flat.shape[0])).reshape(n // width, width)


def _row(v):
    return v.reshape(1, -1)


def _xattn_fwd(h, nq, mem, g_m, wq, wk, wv, wo, g_next, tag, after):
    mem_n = _rms_fwd(mem, _row(g_m), f"xa_mem_rms_{tag}")
    q = _mm([(nq, wq, "nn")], f"xa_q_{tag}", out_dtype=BF16, after=after)
    k = _mm([(mem_n, wk, "nn")], f"xa_k_{tag}", out_dtype=BF16)
    v = _mm([(mem_n, wv, "nn")], f"xa_v_{tag}", out_dtype=BF16)
    o = _attn_fwd(q, k, v, f"xa_attn_{tag}")
    h_new, n_next = _mm([(o, wo, "nn")], f"xa_o_{tag}", res=h, rms_g=_row(g_next))
    return h_new, n_next, (h, nq, mem_n, q, k, v, o)


def _xattn_bwd(dh_new, saved, mem, g_x, g_m, wq, wk, wv, wo, tag, push):
    h, nq, mem_n, q, k, v, o = saved
    do = _mm([(dh_new, wo, "nt")], f"xa_do_{tag}", out_dtype=BF16)
    d_wo = _mm_tn(o, dh_new, f"xa_dwo_{tag}")
    dq, dk, dv = _attn_bwd(q, k, v, do, f"xa_attn_bwd_{tag}")
    d_wq = _mm_tn(nq, dq, f"xa_dwq_{tag}")
    d_wk = _mm_tn(mem_n, dk, f"xa_dwk_{tag}")
    d_wv = _mm_tn(mem_n, dv, f"xa_dwv_{tag}")
    token = push([d_wq, d_wk, d_wv, d_wo])
    dmem_n = _mm([(dk, wk, "nt"), (dv, wv, "nt")], f"xa_dmem_{tag}", after=token)
    _, d_gm = _rms_bwd(dmem_n, mem, _row(g_m), None, f"xa_mem_rms_bwd_{tag}")
    dh, d_gx = _mm([(dq, wq, "nt")], f"xa_dnq_{tag}", rms_bwd=(h, _row(g_x), dh_new), tm=512, after=token)
    return dh, dict(g_xattn=d_gx, g_mem=d_gm)


def _ffn_fwd(h, n, wgt, wut, wd, g_next, tag, after):
    a, b, hid = _ffn_up(n, wgt, wut, f"ffn_up_{tag}", after=after)
    if g_next is None:
        h_new, n_next = _mm([(hid, wd, "nn")], f"ffn_down_{tag}", res=h, tm=512, tn=1024), None
    else:
        h_new, n_next = _mm([(hid, wd, "nn")], f"ffn_down_{tag}", res=h, rms_g=_row(g_next), tm=512)
    return h_new, n_next, (h, n, a, b, hid)


def _ffn_bwd(dh_new, saved, g_f, wgt, wut, wd, tag, push):
    h, n, a, b, hid = saved
    da, db = _ffn_dhid(dh_new, wd, a, b, f"ffn_dhid_{tag}")
    d_wd = _mm_tn(hid, dh_new, f"ffn_dwd_{tag}", tn=512)
    d_wgt = _mm_tn(da, n, f"ffn_dwg_{tag}", tn=512)
    d_wut = _mm_tn(db, n, f"ffn_dwu_{tag}", tn=512)
    token = push([d_wgt, d_wut, d_wd])
    dh, d_gf = _mm([(da, wgt, "nn"), (db, wut, "nn")], f"ffn_dn_{tag}", rms_bwd=(h, _row(g_f), dh_new), tm=512,
                   after=token)
    return dh, dict(g_ffn=d_gf)


_XA = ["xa_w_q", "xa_w_k", "xa_w_v", "xa_w_o"]
_FFN = ["ffn_w_gate", "ffn_w_up", "ffn_w_down"]
GATHERS = {
    "ev_in": [("ev_w_in", 0)],
    "xa0": [("ev_w_out", 0)] + [(n, 0) for n in _XA],
    "ffn0": [(n, 0) for n in _FFN],
    "od": [("od_w_in", 0), ("od_w_out", 0)],
    "xa1": [(n, 1) for n in _XA],
    "ffn1": [(n, 1) for n in _FFN],
}
SCATTERS = {
    "ffn1": [(n, 1) for n in _FFN],
    "xa1": [(n, 1) for n in _XA],
    "od": [("od_w_in", 0), ("od_w_out", 0)],
    "ffn0": [(n, 0) for n in _FFN],
    "xa0": [(n, 0) for n in _XA],
    "ev_out": [("ev_w_out", 0)],
    "ev_in": [("ev_w_in", 0)],
}


def _local_step(x, mem, loss_target, W, comm):
    grads = {}

    h0 = x
    (ev_w_in_t,), token = comm.weights("ev_in", None)
    n0 = _rms_fwd(h0, _row(W["g_mix"][0]), "ev_rms", after=token)
    z = _mm([(n0, ev_w_in_t, "nt")], "ev_in", tn=1280)
    token = comm.prefetch(["ffn0"], z)
    ab, ca = _conv_fwd(z, W["ev_a_conv_w"][0], W["ev_a_conv_b"], W["ev_a_ln_g"], W["ev_a_ln_b"],
                       W["ev_b_conv_w"][0], W["ev_b_conv_b"], "ev_conv", after=token)
    (ev_w_out, *xa_w0), _ = comm.weights("xa0", ab)
    h1, nq0 = _mm([(ab, ev_w_out, "nn")], "ev_out", res=h0, rms_g=_row(W["g_xattn"][0]))
    token = comm.prefetch(["od", "xa1"], nq0)
    h2, nf0, xa0 = _xattn_fwd(h1, nq0, mem, W["g_mem"][0], *xa_w0, W["g_ffn"][0], "l0", token)
    ffn_w0, _ = comm.weights("ffn0", nf0)
    token = comm.prefetch(["ffn1"], nf0)
    h3, n3, ff0 = _ffn_fwd(h2, nf0, *ffn_w0, W["g_mix"][1], "l0", token)

    (od_w_in_t, od_w_out), _ = comm.weights("od", n3)
    zp = _mm([(n3, od_w_in_t, "nt")], "od_in", tn=1024)
    D = x.shape[1]
    ws = W["od_w_s"][0].astype(BF16)
    wst = jnp.swapaxes(ws, 1, 2)
    bsb = jnp.repeat(jnp.transpose(W["od_b_s"][0]), D // C_GROUPS, axis=1)
    y_sgu = _sgu_fwd(zp, W["od_c_ln_g"], W["od_c_ln_b"], ws, bsb, "od_sgu")
    h4, nq1 = _mm([(y_sgu, od_w_out, "nn")], "od_out", res=h3, rms_g=_row(W["g_xattn"][1]))
    xa_w1, _ = comm.weights("xa1", nq1)
    h5, nf1, xa1 = _xattn_fwd(h4, nq1, mem, W["g_mem"][1], *xa_w1, W["g_ffn"][1], "l1", None)
    ffn_w1, _ = comm.weights("ffn1", nf1)
    h6, _, ff1 = _ffn_fwd(h5, nf1, *ffn_w1, None, "l1", None)

    loss_row, dh6, d_gfinal = _loss_bwd(h6, _row(W["g_final"]), loss_target, "loss")
    grads["g_final"] = d_gfinal.reshape(-1)

    dh5, g_ff1 = _ffn_bwd(dh6, ff1, W["g_ffn"][1], *ffn_w1, "l1", lambda dws: comm.grads("ffn1", dws))
    dh4, g_xa1 = _xattn_bwd(dh5, xa1, mem, W["g_xattn"][1], W["g_mem"][1], *xa_w1, "l1",
                            lambda dws: comm.grads("xa1", dws))
    dy_sgu = _mm([(dh4, od_w_out, "nt")], "od_dy", tn=1024)
    d_od_out = _mm_tn(y_sgu, dh4, "od_dwout", tn=1024)
    dzp, d_ws, d_bsb, d_clng, d_clnb = _sgu_bwd(dy_sgu, zp, W["od_c_ln_g"], W["od_c_ln_b"], ws, wst, bsb, "od_sgu_bwd")
    grads["od_w_s"] = d_ws[None]
    grads["od_b_s"] = jnp.transpose(_group_sum(d_bsb, C_GROUPS, "od_dbs"))[None]
    grads["od_c_ln_g"], grads["od_c_ln_b"] = d_clng, d_clnb
    token = comm.grads("od", [_mm_tn(dzp, n3, "od_dwin", tn=512), d_od_out])
    dh3, d_gmix1 = _mm([(dzp, od_w_in_t, "nn")], "od_dn", rms_bwd=(h3, _row(W["g_mix"][1]), dh4), tm=512, after=token)

    dh2, g_ff0 = _ffn_bwd(dh3, ff0, W["g_ffn"][0], *ffn_w0, "l0", lambda dws: comm.grads("ffn0", dws))
    dh1, g_xa0 = _xattn_bwd(dh2, xa0, mem, W["g_xattn"][0], W["g_mem"][0], *xa_w0, "l0",
                            lambda dws: comm.grads("xa0", dws))
    token = comm.grads("ev_out", [_mm_tn(ab, dh1, "ev_dwout", tn=1024)])
    dab = _mm([(dh1, ev_w_out, "nt")], "ev_dab", tn=1024, after=token)
    dca, d_lng, d_lnb, d_ba = _conv_bwd_ln(dab, ca, W["ev_a_ln_g"], W["ev_a_ln_b"], "ev_conv_bwd_ln")
    dz, d_wa, d_wb, d_bb = _conv_bwd(z, dca, dab, W["ev_a_conv_w"][0], W["ev_b_conv_w"][0], W["ev_b_conv_b"],
                                     "ev_conv_bwd")
    grads.update(ev_a_ln_g=d_lng, ev_a_ln_b=d_lnb, ev_a_conv_b=d_ba, ev_b_conv_b=d_bb,
                 ev_a_conv_w=d_wa[None], ev_b_conv_w=d_wb[None])
    token = comm.grads("ev_in", [_mm_tn(dz, n0, "ev_dwin", tn=512)])
    grad_x, d_gmix0 = _mm([(dz, ev_w_in_t, "nn")], "ev_dn", rms_bwd=(h0, _row(W["g_mix"][0]), dh1), tm=512, after=token)

    grads["g_mix"] = jnp.concatenate([d_gmix0, d_gmix1], axis=0)
    for key in ("g_xattn", "g_mem"):
        grads[key] = jnp.concatenate([g_xa0[key], g_xa1[key]], axis=0)
    grads["g_ffn"] = jnp.concatenate([g_ff0["g_ffn"], g_ff1["g_ffn"]], axis=0)
    return loss_row, grad_x, grads


class _Exchanges:
    def __init__(self, shards, dev_idx, after):
        self.shards, self.dev_idx = shards, dev_idx
        self.gathering, self.scattering = {}, {}
        self.first = _all_gather_weights(self._pack(GATHERS["ev_in"]), self._rows(GATHERS["ev_in"]), "ag_ev_in",
                                         after=after)
        self.first_token = self.prefetch(["xa0"], self.first[0])

    def _rows(self, entries):
        return [self.shards[e].shape[0] for e in entries]

    def _pack(self, entries):
        return jnp.concatenate([self.shards[e] for e in entries], axis=0)

    def prefetch(self, gathers, after):
        for name in gathers:
            rows = self._rows(GATHERS[name])
            pack = self._pack(GATHERS[name])
            lands = [lax.empty((NDEV * r, pack.shape[1]), pack.dtype) for r in rows]
            plan, n = _gather_plan(rows)
            send, recv, srcs, lands, after = _split_start([pack], lands, plan, n, after, f"ag_{name}_start")
            self.gathering[name] = (send, recv, srcs, lands, plan, rows)
        return after

    def weights(self, name, after):
        if name == "ev_in":
            return self.first, self.first_token
        send, recv, srcs, lands, plan, rows = self.gathering.pop(name)
        _, lands = _split_wait(send, recv, srcs, lands, plan, after, f"ag_{name}_wait")
        return _place_own([self.shards[e] for e in GATHERS[name]], lands, self.dev_idx, f"ag_{name}_own"), None

    def grads(self, name, dws):
        rows = self._rows(SCATTERS[name])
        lands = [lax.empty((NDEV - 1, r, d.shape[1]), d.dtype) for r, d in zip(rows, dws)]
        plan, n = _scatter_plan(rows)
        send, recv, srcs, lands, token = _split_start(dws, lands, plan, n, None, f"rs_{name}_start")
        self.scattering[name] = (send, recv, srcs, lands, plan)
        return token

    def received(self, after):
        out = {}
        for name, (send, recv, srcs, lands, plan) in self.scattering.items():
            srcs, lands = _split_wait(send, recv, srcs, lands, plan, after, f"rs_{name}_wait")
            for entry, g, got in zip(SCATTERS[name], srcs, lands):
                out[entry] = (g, got)
        return out


def kernel(x, mem, g_mix, g_xattn, g_mem, g_ffn, g_final, ev_w_in, ev_a_conv_w, ev_a_conv_b, ev_a_ln_g, ev_a_ln_b, ev_b_conv_w, ev_b_conv_b, ev_w_out, od_w_in, od_c_ln_g, od_c_ln_b, od_w_s, od_b_s, od_w_out, xa_w_q, xa_w_k, xa_w_v, xa_w_o, ffn_w_gate, ffn_w_up, ffn_w_down, loss_target, m_g_mix, m_g_xattn, m_g_mem, m_g_ffn, m_g_final, m_ev_w_in, m_ev_a_conv_w, m_ev_a_conv_b, m_ev_a_ln_g, m_ev_a_ln_b, m_ev_b_conv_w, m_ev_b_conv_b, m_ev_w_out, m_od_w_in, m_od_c_ln_g, m_od_c_ln_b, m_od_w_s, m_od_b_s, m_od_w_out, m_xa_w_q, m_xa_w_k, m_xa_w_v, m_xa_w_o, m_ffn_w_gate, m_ffn_w_up, m_ffn_w_down, v_g_mix, v_g_xattn, v_g_mem, v_g_ffn, v_g_final, v_ev_w_in, v_ev_a_conv_w, v_ev_a_conv_b, v_ev_a_ln_g, v_ev_a_ln_b, v_ev_b_conv_w, v_ev_b_conv_b, v_ev_w_out, v_od_w_in, v_od_c_ln_g, v_od_c_ln_b, v_od_w_s, v_od_b_s, v_od_w_out, v_xa_w_q, v_xa_w_k, v_xa_w_v, v_xa_w_o, v_ffn_w_gate, v_ffn_w_up, v_ffn_w_down):
    local = dict(g_mix=g_mix, g_xattn=g_xattn, g_mem=g_mem, g_ffn=g_ffn, g_final=g_final, ev_w_in=ev_w_in, ev_a_conv_w=ev_a_conv_w, ev_a_conv_b=ev_a_conv_b, ev_a_ln_g=ev_a_ln_g, ev_a_ln_b=ev_a_ln_b, ev_b_conv_w=ev_b_conv_w, ev_b_conv_b=ev_b_conv_b, ev_w_out=ev_w_out, od_w_in=od_w_in, od_c_ln_g=od_c_ln_g, od_c_ln_b=od_c_ln_b, od_w_s=od_w_s, od_b_s=od_b_s, od_w_out=od_w_out, xa_w_q=xa_w_q, xa_w_k=xa_w_k, xa_w_v=xa_w_v, xa_w_o=xa_w_o, ffn_w_gate=ffn_w_gate, ffn_w_up=ffn_w_up, ffn_w_down=ffn_w_down)
    mom = dict(g_mix=m_g_mix, g_xattn=m_g_xattn, g_mem=m_g_mem, g_ffn=m_g_ffn, g_final=m_g_final, ev_w_in=m_ev_w_in, ev_a_conv_w=m_ev_a_conv_w, ev_a_conv_b=m_ev_a_conv_b, ev_a_ln_g=m_ev_a_ln_g, ev_a_ln_b=m_ev_a_ln_b, ev_b_conv_w=m_ev_b_conv_w, ev_b_conv_b=m_ev_b_conv_b, ev_w_out=m_ev_w_out, od_w_in=m_od_w_in, od_c_ln_g=m_od_c_ln_g, od_c_ln_b=m_od_c_ln_b, od_w_s=m_od_w_s, od_b_s=m_od_b_s, od_w_out=m_od_w_out, xa_w_q=m_xa_w_q, xa_w_k=m_xa_w_k, xa_w_v=m_xa_w_v, xa_w_o=m_xa_w_o, ffn_w_gate=m_ffn_w_gate, ffn_w_up=m_ffn_w_up, ffn_w_down=m_ffn_w_down)
    vel = dict(g_mix=v_g_mix, g_xattn=v_g_xattn, g_mem=v_g_mem, g_ffn=v_g_ffn, g_final=v_g_final, ev_w_in=v_ev_w_in, ev_a_conv_w=v_ev_a_conv_w, ev_a_conv_b=v_ev_a_conv_b, ev_a_ln_g=v_ev_a_ln_g, ev_a_ln_b=v_ev_a_ln_b, ev_b_conv_w=v_ev_b_conv_w, ev_b_conv_b=v_ev_b_conv_b, ev_w_out=v_ev_w_out, od_w_in=v_od_w_in, od_c_ln_g=v_od_c_ln_g, od_c_ln_b=v_od_c_ln_b, od_w_s=v_od_w_s, od_b_s=v_od_b_s, od_w_out=v_od_w_out, xa_w_q=v_xa_w_q, xa_w_k=v_xa_w_k, xa_w_v=v_xa_w_v, xa_w_o=v_xa_w_o, ffn_w_gate=v_ffn_w_gate, ffn_w_up=v_ffn_w_up, ffn_w_down=v_ffn_w_down)
    D = x.shape[-1]
    dev = 4 * lax.axis_index("x") + 2 * lax.axis_index("y") + lax.axis_index("c")

    def comm_layout(n, a):
        return jnp.transpose(a) if _shard_axis(n) == 2 else a

    shards = {(n, i): comm_layout(n, local[n][i]).astype(BF16) for n in BIG for i in range(local[n].shape[0])}
    small_sizes = [local[n].size for n in SMALL_SHARDED]
    small_block = _pad_rows(jnp.concatenate([local[n].reshape(-1) for n in SMALL_SHARDED]), 128, 8)
    small_all = _all_gather(small_block, "ag_small")
    comm = _Exchanges(shards, jnp.reshape(dev, (1,)).astype(jnp.int32), small_all)
    small_all = small_all.reshape(NDEV, -1)

    W = {n: local[n] for n in REPLICATED}
    o0 = 0
    for n, sz in zip(SMALL_SHARDED, small_sizes):
        blocks = small_all[:, o0:o0 + sz].reshape((NDEV,) + local[n].shape)
        W[n] = _full_from_blocks(blocks, _shard_axis(n))
        o0 += sz

    loss_row, grad_x, grads = _local_step(x[0], mem[0], loss_target[0], W, comm)

    received = comm.received(grad_x)
    rest = REPLICATED + SMALL_SHARDED
    rest_full_shapes = [grads[n].shape for n in rest]
    g_rest = _pad_rows(jnp.concatenate([grads[n].astype(F32).reshape(-1) for n in rest]), D, 8)
    small_rows = g_rest.shape[0]
    small_plan, small_n = _gather_plan([small_rows])
    small_send, small_recv, small_srcs, small_lands, token = _split_start(
        [g_rest], [lax.empty((NDEV * small_rows, D), F32)], small_plan, small_n, received["ev_w_in", 0][1],
        "ag_small_grads_start")

    gsh, delta, new_m, new_v = {}, {}, {}, {}
    def stacked_layout(n, a):
        return jnp.swapaxes(a, 1, 2) if _shard_axis(n) == 2 else a

    for n in BIG:
        parts = [received[n, i] for i in range(local[n].shape[0])]
        outs = _finish_weight([p[0] for p in parts], [p[1] for p in parts], comm.dev_idx,
                              *(stacked_layout(n, a) for a in (local[n], mom[n], vel[n])), f"finish_{n}", after=token)
        gsh[n], delta[n], new_m[n], new_v[n] = (stacked_layout(n, o) for o in outs)

    _, small_lands = _split_wait(small_send, small_recv, small_srcs, small_lands, small_plan, delta[BIG[-1]],
                                 "ag_small_grads_wait")
    partials = _place_own([g_rest], small_lands, comm.dev_idx, "ag_small_grads_own")[0]
    g_rest = _sum_slots(partials.reshape(NDEV, small_rows, D), "sum_small_grads").reshape(-1)
    o0 = 0
    for n, shp in zip(rest, rest_full_shapes):
        sz = 1
        for s in shp:
            sz *= s
        full = g_rest[o0:o0 + sz].reshape(shp)
        o0 += sz
        if n in SMALL_SHARDED:
            full = lax.dynamic_index_in_dim(_blocks_from_full(full, _shard_axis(n)), dev, 0, keepdims=False)
        gsh[n] = full.reshape(local[n].shape)

    for n in rest:
        delta[n], new_m[n], new_v[n] = _adamw(local[n], gsh[n], mom[n], vel[n], f"adamw_{n}")

    loss = lax.psum(loss_row[0, 0], ("x", "y", "c"))
    return (loss, grad_x[None], *[gsh[n] for n in WEIGHTS], *[delta[n] for n in WEIGHTS],
            *[new_m[n] for n in WEIGHTS], *[new_v[n] for n in WEIGHTS])
```

```python
import jax
import jax.numpy as jnp
from jax import lax
from jax.experimental import pallas as pl
from jax.experimental.pallas import tpu as pltpu

F32, BF16 = jnp.float32, jnp.bfloat16
NDEV = 8
RMS_EPS = 1e-6
LN_EPS = 1e-5
CHUNK = 128
C_GROUPS = 8
XA_HEADS = 4
ADAM_LR, ADAM_B1, ADAM_B2, ADAM_EPS, ADAM_WD, ADAM_STEP = 0.001, 0.9, 0.999, 1e-08, 0.01, 10
HALO = 16
ROW_CHUNK = 32
V7X_VMEM_LIMIT = 56 * 1024 * 1024
MESH = pl.DeviceIdType.MESH

TS_ROW = 512
TS_MM = 1024
TN_MM = 1408
TS_FFN = 512
MM_ROW_CHUNK = 256
TS_CONV = 512
TS_SGU = 512
TS_ATTN = 512


def _cp(*sem):
    return pltpu.CompilerParams(dimension_semantics=sem, vmem_limit_bytes=V7X_VMEM_LIMIT)


def _pick(n, pref, align):
    for t in range(min(n, pref), 0, -1):
        if n % t == 0 and (t % align == 0 or t == n):
            return t
    return n


def _sigmoid(x):
    return 0.5 * jnp.tanh(0.5 * x) + 0.5


def _dot(a, b):
    return jnp.dot(a, b, preferred_element_type=F32)


def _dot_nt(a, b):
    return lax.dot_general(a, b, (((1,), (1,)), ((), ())), preferred_element_type=F32)


def _dot_tn(a, b):
    return lax.dot_general(a, b, (((0,), (0,)), ((), ())), preferred_element_type=F32)


_ANY = pl.BlockSpec(memory_space=pl.ANY)
_RESIDENT = pl.Buffered(1)


def _after(after):
    return ([], []) if after is None else ([_ANY], [after])


def _rms_fwd(h, g, name, after=None):
    S, D = h.shape
    ts = _pick(S, TS_MM, 16)
    after_specs, after_ops = _after(after)

    def body(h_ref, g_ref, *rest):
        o_ref = rest[-1]
        x = h_ref[...]
        r = lax.rsqrt(jnp.mean(x * x, axis=-1, keepdims=True) + RMS_EPS)
        o_ref[...] = ((x * r) * g_ref[...]).astype(o_ref.dtype)

    return pl.pallas_call(
        body, grid=(S // ts,),
        in_specs=[pl.BlockSpec((ts, D), lambda i: (i, 0)), pl.BlockSpec((1, D), lambda i: (0, 0))] + after_specs,
        out_specs=pl.BlockSpec((ts, D), lambda i: (i, 0)),
        out_shape=jax.ShapeDtypeStruct((S, D), BF16), compiler_params=_cp("parallel"), name=name)(h, g, *after_ops)


def _rms_bwd(dn, h, g, dres, name):
    S, D = h.shape
    ts = _pick(S, TS_ROW, 8)
    has_res = dres is not None

    def body(*refs):
        if has_res:
            dn_ref, h_ref, g_ref, dres_ref, dh_ref, dg_ref = refs
        else:
            dn_ref, h_ref, g_ref, dh_ref, dg_ref = refs
        x = h_ref[...]
        dn_ = dn_ref[...].astype(F32)
        r = lax.rsqrt(jnp.mean(x * x, axis=-1, keepdims=True) + RMS_EPS)
        xr = x * r

        @pl.when(pl.program_id(0) == 0)
        def _():
            dg_ref[...] = jnp.zeros_like(dg_ref)

        dg_ref[...] += jnp.sum(dn_ * xr, axis=0, keepdims=True)
        u = dn_ * g_ref[...]
        dh = r * u - xr * (r * jnp.mean(u * xr, axis=-1, keepdims=True))
        if has_res:
            dh = dh + dres_ref[...]
        dh_ref[...] = dh

    tile = pl.BlockSpec((ts, D), lambda i: (i, 0))
    vec = pl.BlockSpec((1, D), lambda i: (0, 0))
    ins = [dn, h, g] + ([dres] if has_res else [])
    return pl.pallas_call(
        body, grid=(S // ts,),
        in_specs=[tile, tile, vec] + ([tile] if has_res else []),
        out_specs=[tile, vec],
        out_shape=[jax.ShapeDtypeStruct((S, D), F32), jax.ShapeDtypeStruct((1, D), F32)],
        compiler_params=_cp("arbitrary"), name=name)(*ins)


def _loss_bwd(h, g, target, name):
    S, D = h.shape
    ts = _pick(S, TS_ROW, 8)

    def body(h_ref, g_ref, t_ref, loss_ref, dh_ref, dg_ref):
        x = h_ref[...]
        r = lax.rsqrt(jnp.mean(x * x, axis=-1, keepdims=True) + RMS_EPS)
        xr = x * r
        gg = g_ref[...]
        e = xr * gg - t_ref[...]

        @pl.when(pl.program_id(0) == 0)
        def _():
            dg_ref[...] = jnp.zeros_like(dg_ref)
            loss_ref[...] = jnp.zeros_like(loss_ref)

        tile_loss = jnp.sum(jnp.sum(e * e, axis=0, keepdims=True), axis=1, keepdims=True) * (0.5 / D)
        loss_ref[...] += jnp.broadcast_to(tile_loss, loss_ref.shape)
        dy = e * (1.0 / D)
        dg_ref[...] += jnp.sum(dy * xr, axis=0, keepdims=True)
        u = dy * gg
        dh_ref[...] = r * u - xr * (r * jnp.mean(u * xr, axis=-1, keepdims=True))

    tile = pl.BlockSpec((ts, D), lambda i: (i, 0))
    vec = pl.BlockSpec((1, D), lambda i: (0, 0))
    return pl.pallas_call(
        body, grid=(S // ts,),
        in_specs=[tile, vec, tile],
        out_specs=[pl.BlockSpec((1, 128), lambda i: (0, 0)), tile, vec],
        out_shape=[jax.ShapeDtypeStruct((1, 128), F32), jax.ShapeDtypeStruct((S, D), F32),
                   jax.ShapeDtypeStruct((1, D), F32)],
        compiler_params=_cp("arbitrary"), name=name)(h, g, target)


def _mm(pairs, name, out_dtype=F32, res=None, rms_g=None, rms_bwd=None, tm=None, tn=None, after=None):
    M = pairs[0][0].shape[0]
    N = pairs[0][1].shape[1 if pairs[0][2] == "nn" else 0]
    whole_rows = rms_g is not None or rms_bwd is not None
    tm = _pick(M, tm or TS_MM, 16)
    tn = N if whole_rows else _pick(N, tn or TN_MM, 128)
    npair = len(pairs)
    modes = [p[2] for p in pairs]
    after_specs, after_ops = _after(after)

    rc = MM_ROW_CHUNK if whole_rows and tm % MM_ROW_CHUNK == 0 else tm

    def body(*refs):
        rest = refs[2 * npair + len(after_ops):]
        res_ref = None
        if res is not None:
            res_ref, rest = rest[0], rest[1:]
        if rms_bwd is not None:
            dg_ref = rest[4]

            @pl.when(pl.program_id(0) == 0)
            def _():
                dg_ref[...] = jnp.zeros_like(dg_ref)

        for r0 in range(0, tm, rc):
            rows = pl.ds(r0, rc)
            acc = None
            for p in range(npair):
                a_ = refs[2 * p][rows, :].astype(BF16)
                d = _dot(a_, refs[2 * p + 1][...]) if modes[p] == "nn" else _dot_nt(a_, refs[2 * p + 1][...])
                acc = d if acc is None else acc + d
            if res_ref is not None:
                acc = acc + res_ref[rows, :]
            if rms_bwd is not None:
                h_ref, g_ref, dres_ref, dh_ref, _ = rest
                x = h_ref[rows, :]
                r = lax.rsqrt(jnp.mean(x * x, axis=-1, keepdims=True) + RMS_EPS)
                xr = x * r
                dg_ref[...] += jnp.sum(acc * xr, axis=0, keepdims=True)
                u = acc * g_ref[...]
                dh_ref[rows, :] = r * u - xr * (r * jnp.mean(u * xr, axis=-1, keepdims=True)) + dres_ref[rows, :]
            elif rms_g is not None:
                g_ref, o_ref, n_ref = rest
                o_ref[rows, :] = acc
                r = lax.rsqrt(jnp.mean(acc * acc, axis=-1, keepdims=True) + RMS_EPS)
                n_ref[rows, :] = ((acc * r) * g_ref[...]).astype(BF16)
            else:
                rest[0][rows, :] = acc.astype(rest[0].dtype)

    in_specs, ins = [], []
    for a, w, mode in pairs:
        K = a.shape[1]
        in_specs.append(pl.BlockSpec((tm, K), lambda i, j: (i, 0)))
        once = _RESIDENT if tn == N else None
        in_specs.append(pl.BlockSpec((K, tn), lambda i, j: (0, j), pipeline_mode=once) if mode == "nn"
                        else pl.BlockSpec((tn, K), lambda i, j: (j, 0), pipeline_mode=once))
        ins += [a, w]
    in_specs += after_specs
    ins += after_ops
    tile = pl.BlockSpec((tm, tn), lambda i, j: (i, j))
    vec = pl.BlockSpec((1, tn), lambda i, j: (0, j))
    if res is not None:
        in_specs.append(tile)
        ins.append(res)
    sem = ("parallel", "parallel")
    if rms_bwd is not None:
        in_specs += [tile, vec, tile]
        ins += list(rms_bwd)
        out_specs = [tile, vec]
        out_shape = [jax.ShapeDtypeStruct((M, N), F32), jax.ShapeDtypeStruct((1, N), F32)]
        sem = ("arbitrary", "arbitrary")
    elif rms_g is not None:
        in_specs.append(vec)
        ins.append(rms_g)
        out_specs = [tile, tile]
        out_shape = [jax.ShapeDtypeStruct((M, N), F32), jax.ShapeDtypeStruct((M, N), BF16)]
    else:
        out_specs = tile
        out_shape = jax.ShapeDtypeStruct((M, N), out_dtype)
    return pl.pallas_call(
        body, grid=(M // tm, N // tn), in_specs=in_specs, out_specs=out_specs, out_shape=out_shape,
        compiler_params=_cp(*sem), name=name)(*ins)


def _mm_tn(a, b, name, ts=None, tn=None):
    S, K = a.shape
    N = b.shape[1]
    ts = _pick(S, ts or TS_MM, 16)
    tn = _pick(N, tn or TN_MM, 128)
    nsteps = S // ts

    def body(a_ref, b_ref, o_ref, acc_ref):
        s = pl.program_id(1)

        @pl.when(s == 0)
        def _():
            acc_ref[...] = jnp.zeros_like(acc_ref)

        acc_ref[...] += _dot_tn(a_ref[...].astype(BF16), b_ref[...].astype(BF16))

        @pl.when(s == nsteps - 1)
        def _():
            o_ref[...] = acc_ref[...].astype(o_ref.dtype)

    return pl.pallas_call(
        body, grid=(N // tn, nsteps),
        in_specs=[pl.BlockSpec((ts, K), lambda j, s: (s, 0)), pl.BlockSpec((ts, tn), lambda j, s: (s, j))],
        out_specs=pl.BlockSpec((K, tn), lambda j, s: (0, j)), out_shape=jax.ShapeDtypeStruct((K, N), BF16),
        scratch_shapes=[pltpu.VMEM((K, tn), F32)],
        compiler_params=_cp("parallel", "arbitrary"), name=name)(a, b)


def _col_chunk(n):
    return 256 if n % 256 == 0 else 128


def _ffn_up(n, wgt, wut, name, after=None):
    S, D = n.shape
    F = wgt.shape[0]
    tm = _pick(S, TS_FFN, 16)
    ce = _col_chunk(F)
    after_specs, after_ops = _after(after)

    def body(n_ref, wg_ref, wu_ref, *rest):
        a_ref, b_ref, hid_ref = rest[-3:]
        x = n_ref[...]
        for c0 in range(0, F, ce):
            a = _dot_nt(x, wg_ref[c0:c0 + ce, :])
            b = _dot_nt(x, wu_ref[c0:c0 + ce, :])
            a_ref[:, c0:c0 + ce] = a.astype(BF16)
            b_ref[:, c0:c0 + ce] = b.astype(BF16)
            hid_ref[:, c0:c0 + ce] = (a * _sigmoid(a) * b).astype(BF16)

    wspec = pl.BlockSpec((F, D), lambda i: (0, 0), pipeline_mode=_RESIDENT)
    ospec = pl.BlockSpec((tm, F), lambda i: (i, 0))
    osh = jax.ShapeDtypeStruct((S, F), BF16)
    return pl.pallas_call(
        body, grid=(S // tm,),
        in_specs=[pl.BlockSpec((tm, D), lambda i: (i, 0)), wspec, wspec] + after_specs,
        out_specs=[ospec, ospec, ospec], out_shape=[osh, osh, osh],
        compiler_params=_cp("parallel"), name=name)(n, wgt, wut, *after_ops)


def _ffn_dhid(dh, wd, a, b, name):
    S, D = dh.shape
    F = wd.shape[0]
    tm = _pick(S, TS_FFN, 16)
    ce = _col_chunk(F)

    def body(dh_ref, wd_ref, a_ref, b_ref, da_ref, db_ref):
        x = dh_ref[...].astype(BF16)
        for c0 in range(0, F, ce):
            g = _dot_nt(x, wd_ref[c0:c0 + ce, :]).astype(BF16)
            a_ = a_ref[:, c0:c0 + ce]
            sg = _sigmoid(a_)
            silu = a_ * sg
            da_ref[:, c0:c0 + ce] = (g * b_ref[:, c0:c0 + ce]) * (sg + silu * (1.0 - sg))
            db_ref[:, c0:c0 + ce] = g * silu

    tile = pl.BlockSpec((tm, F), lambda i: (i, 0))
    osh = jax.ShapeDtypeStruct((S, F), BF16)
    return pl.pallas_call(
        body, grid=(S // tm,),
        in_specs=[pl.BlockSpec((tm, D), lambda i: (i, 0)),
                  pl.BlockSpec((F, D), lambda i: (0, 0), pipeline_mode=_RESIDENT), tile, tile],
        out_specs=[tile, tile], out_shape=[osh, osh],
        compiler_params=_cp("parallel"), name=name)(dh, wd, a, b)


def _softmax_rows(s):
    m = jnp.max(s, axis=-1, keepdims=True)
    p = jnp.exp(s - m)
    return p / jnp.sum(p, axis=-1, keepdims=True)


def _attn_fwd(q, k, v, name):
    S, D = q.shape
    M = k.shape[0]
    hd = D // XA_HEADS
    scale = hd ** -0.5
    ts = _pick(S, TS_ATTN, 16)

    def body(q_ref, k_ref, v_ref, o_ref):
        for h in range(XA_HEADS):
            sl = slice(h * hd, (h + 1) * hd)
            p = _softmax_rows(_dot_nt(q_ref[:, sl], k_ref[:, sl]) * scale)
            o_ref[:, sl] = _dot(p.astype(BF16), v_ref[:, sl]).astype(BF16)

    tile = pl.BlockSpec((ts, D), lambda i: (i, 0))
    memspec = pl.BlockSpec((M, D), lambda i: (0, 0))
    return pl.pallas_call(
        body, grid=(S // ts,), in_specs=[tile, memspec, memspec], out_specs=tile,
        out_shape=jax.ShapeDtypeStruct((S, D), BF16), compiler_params=_cp("parallel"), name=name)(q, k, v)


def _attn_bwd(q, k, v, do, name):
    S, D = q.shape
    M = k.shape[0]
    hd = D // XA_HEADS
    scale = hd ** -0.5
    ts = _pick(S, TS_ATTN, 16)

    def body(q_ref, k_ref, v_ref, do_ref, dq_ref, dk_ref, dv_ref):
        @pl.when(pl.program_id(0) == 0)
        def _():
            dk_ref[...] = jnp.zeros_like(dk_ref)
            dv_ref[...] = jnp.zeros_like(dv_ref)

        for h in range(XA_HEADS):
            sl = slice(h * hd, (h + 1) * hd)
            qh, kh, vh, doh = q_ref[:, sl], k_ref[:, sl], v_ref[:, sl], do_ref[:, sl]
            p = _softmax_rows(_dot_nt(qh, kh) * scale)
            dp = _dot_nt(doh, vh)
            dv_ref[:, sl] += _dot_tn(p.astype(BF16), doh)
            delta = jnp.sum(dp * p, axis=-1, keepdims=True)
            ds = (p * (dp - delta) * scale).astype(BF16)
            dq_ref[:, sl] = _dot(ds, kh).astype(BF16)
            dk_ref[:, sl] += _dot_tn(ds, qh)

    tile = pl.BlockSpec((ts, D), lambda i: (i, 0))
    memspec = pl.BlockSpec((M, D), lambda i: (0, 0))
    return pl.pallas_call(
        body, grid=(S // ts,), in_specs=[tile, memspec, memspec, tile], out_specs=[tile, memspec, memspec],
        out_shape=[jax.ShapeDtypeStruct((S, D), BF16), jax.ShapeDtypeStruct((M, D), F32),
                   jax.ShapeDtypeStruct((M, D), F32)],
        compiler_params=_cp("arbitrary"), name=name)(q, k, v, do)


def _halo_specs(ts, width, col):
    per = ts // HALO

    def prev(i):
        return (jnp.maximum(i * per - 1, 0), col)

    def nxt(i, n_tiles):
        return (jnp.minimum((i + 1) * per, n_tiles * per - 1), col)

    return prev, nxt


def _fill_ext(ext_ref, prev_val, main_val, next_val, first, last, ts):
    ext_ref[pl.ds(0, HALO), :] = jnp.where(first, 0.0, prev_val)
    ext_ref[pl.ds(HALO, ts), :] = main_val
    ext_ref[pl.ds(HALO + ts, HALO), :] = jnp.where(last, 0.0, next_val)


SUBLANES = 8


def _fill_shifted(sh_ref, ts):
    n = ts + 2 * HALO - SUBLANES
    for s in range(1, SUBLANES):
        sh_ref[s, pl.ds(0, n), :] = sh_ref[0, pl.ds(s, n), :]


def _tap(sh_ref, r0, offset, rc):
    q, s = divmod(offset, SUBLANES)
    return sh_ref[s, pl.ds(pl.multiple_of(r0 + SUBLANES * q, SUBLANES), rc), :]


def _conv_fwd(z, wa, ba, lng, lnb, wb, bb, name, after=None):
    S = z.shape[0]
    C = z.shape[1] // 5
    KA, KB = wa.shape[0], wb.shape[0]
    pa, pb = KA // 2, KB // 2
    assert pa <= HALO and pb <= HALO
    ts = _pick(S, TS_CONV, ROW_CHUNK)
    nt = S // ts
    rc = ROW_CHUNK
    prev, nxt = _halo_specs(ts, 5 * C, 0)
    after_specs, after_ops = _after(after)

    def body(*refs):
        compute(*refs[:9], *refs[9 + len(after_ops):])

    def compute(z_ref, zp_ref, zn_ref, wa_ref, ba_ref, lng_ref, lnb_ref, wb_ref, bb_ref, ab_ref, ca_ref,
                ga_sh, tb_ext, win_b):
        i = pl.program_id(0)
        first, last = i == 0, i == nt - 1

        def glu(r):
            return r[:, 0:C] * _sigmoid(r[:, C:2 * C])

        def gcb(r):
            return r[:, 4 * C:5 * C] * r[:, 2 * C:3 * C]

        _fill_ext(ga_sh.at[0], glu(zp_ref), glu(z_ref), glu(zn_ref), first, last, ts)
        _fill_shifted(ga_sh, ts)
        _fill_ext(tb_ext, gcb(zp_ref), gcb(z_ref), gcb(zn_ref), first, last, ts)

        def chunk(c, carry):
            r0 = pl.multiple_of(c * rc, rc)
            win_b[...] = tb_ext[pl.ds(r0, rc + 2 * HALO), :]
            acc = jnp.zeros((rc, C), F32)
            for k in range(KA):
                acc = acc + wa_ref[k:k + 1, :] * _tap(ga_sh, r0, HALO - pa + k, rc)
            ca = acc + ba_ref[...]
            ca_ref[pl.ds(r0, rc), :] = ca
            mu = jnp.mean(ca, axis=-1, keepdims=True)
            xc = ca - mu
            var = jnp.mean(xc * xc, axis=-1, keepdims=True)
            ln = xc * lax.rsqrt(var + LN_EPS) * lng_ref[...] + lnb_ref[...]
            ab_ref[pl.ds(r0, rc), 0:C] = (ln * _sigmoid(ln)).astype(BF16)
            cb = jnp.zeros((rc, C), F32) + bb_ref[...]
            for k in range(KB):
                cb = cb + wb_ref[k:k + 1, :] * win_b[pl.ds(HALO - pb + k, rc), :]
            ab_ref[pl.ds(r0, rc), C:2 * C] = (z_ref[pl.ds(r0, rc), 3 * C:4 * C] * cb).astype(BF16)
            return carry

        lax.fori_loop(0, ts // rc, chunk, 0)

    zspec = pl.BlockSpec((ts, 5 * C), lambda i: (i, 0))
    zprev = pl.BlockSpec((HALO, 5 * C), prev)
    znext = pl.BlockSpec((HALO, 5 * C), lambda i: nxt(i, nt))

    def full(a):
        return pl.BlockSpec(a.shape, lambda i: (0, 0))

    return pl.pallas_call(
        body, grid=(nt,),
        in_specs=[zspec, zprev, znext, full(wa), full(ba), full(lng), full(lnb), full(wb), full(bb)] + after_specs,
        out_specs=[pl.BlockSpec((ts, 2 * C), lambda i: (i, 0)), pl.BlockSpec((ts, C), lambda i: (i, 0))],
        out_shape=[jax.ShapeDtypeStruct((S, 2 * C), BF16), jax.ShapeDtypeStruct((S, C), F32)],
        scratch_shapes=[pltpu.VMEM((SUBLANES, ts + 2 * HALO, C), F32), pltpu.VMEM((ts + 2 * HALO, C), F32),
                        pltpu.VMEM((rc + 2 * HALO, C), F32)],
        compiler_params=_cp("parallel"), name=name)(z, z, z, wa, ba, lng, lnb, wb, bb, *after_ops)


def _conv_bwd_ln(dab, ca, lng, lnb, name):
    S, C = ca.shape
    ts = _pick(S, TS_ROW, 8)

    def body(da_ref, ca_ref, lng_ref, lnb_ref, dca_ref, dg_ref, db_ref, dbias_ref):
        @pl.when(pl.program_id(0) == 0)
        def _():
            dg_ref[...] = jnp.zeros_like(dg_ref)
            db_ref[...] = jnp.zeros_like(db_ref)
            dbias_ref[...] = jnp.zeros_like(dbias_ref)

        ca_ = ca_ref[...]
        mu = jnp.mean(ca_, axis=-1, keepdims=True)
        xc = ca_ - mu
        rstd = lax.rsqrt(jnp.mean(xc * xc, axis=-1, keepdims=True) + LN_EPS)
        xh = xc * rstd
        ln = xh * lng_ref[...] + lnb_ref[...]
        sg = _sigmoid(ln)
        dln = da_ref[...].astype(F32) * (sg * (1.0 + ln * (1.0 - sg)))
        dg_ref[...] += jnp.sum(dln * xh, axis=0, keepdims=True)
        db_ref[...] += jnp.sum(dln, axis=0, keepdims=True)
        dxh = dln * lng_ref[...]
        dca = rstd * (dxh - jnp.mean(dxh, axis=-1, keepdims=True) - xh * jnp.mean(dxh * xh, axis=-1, keepdims=True))
        dca_ref[...] = dca
        dbias_ref[...] += jnp.sum(dca, axis=0, keepdims=True)

    tile = pl.BlockSpec((ts, C), lambda i: (i, 0))
    vec = pl.BlockSpec((1, C), lambda i: (0, 0))
    vsh = jax.ShapeDtypeStruct((1, C), F32)
    return pl.pallas_call(
        body, grid=(S // ts,), in_specs=[tile, tile, vec, vec], out_specs=[tile, vec, vec, vec],
        out_shape=[jax.ShapeDtypeStruct((S, C), F32), vsh, vsh, vsh],
        compiler_params=_cp("arbitrary"), name=name)(dab, ca, lng, lnb)


def _conv_bwd(z, dca, dab, wa, wb, bb, name):
    S = z.shape[0]
    C = z.shape[1] // 5
    KA, KB = wa.shape[0], wb.shape[0]
    pa, pb = KA // 2, KB // 2
    ts = _pick(S, TS_CONV, ROW_CHUNK)
    nt = S // ts
    rc = ROW_CHUNK
    prev0, nxt0 = _halo_specs(ts, C, 0)
    prev1, nxt1 = _halo_specs(ts, C, 1)

    def body(z_ref, zp_ref, zn_ref, dca_ref, dcap_ref, dcan_ref, db_ref, dbp_ref, dbn_ref, wa_ref, wb_ref, bb_ref,
             dz_ref, dwa_ref, dwb_ref, dbb_ref,
             ga_sh, dca_sh, tb_ext, dcb_ext, win_tb, win_dcb, acc_a, acc_b, acc_bias):
        i = pl.program_id(0)
        first, last = i == 0, i == nt - 1

        @pl.when(first)
        def _():
            acc_a[...] = jnp.zeros_like(acc_a)
            acc_b[...] = jnp.zeros_like(acc_b)
            acc_bias[...] = jnp.zeros_like(acc_bias)

        def glu(r):
            return r[:, 0:C] * _sigmoid(r[:, C:2 * C])

        def gcb(r):
            return r[:, 4 * C:5 * C] * r[:, 2 * C:3 * C]

        def dcb(d, r):
            return d[...].astype(F32) * r[:, 3 * C:4 * C]

        _fill_ext(ga_sh.at[0], glu(zp_ref), glu(z_ref), glu(zn_ref), first, last, ts)
        _fill_shifted(ga_sh, ts)
        _fill_ext(dca_sh.at[0], dcap_ref[...], dca_ref[...], dcan_ref[...], first, last, ts)
        _fill_shifted(dca_sh, ts)
        _fill_ext(tb_ext, gcb(zp_ref), gcb(z_ref), gcb(zn_ref), first, last, ts)
        _fill_ext(dcb_ext, dcb(dbp_ref, zp_ref), dcb(db_ref, z_ref), dcb(dbn_ref, zn_ref), first, last, ts)

        def fold(x):
            return jnp.sum(x.reshape(rc // 8, 8, C), axis=0)

        def chunk(c, carry):
            r0 = pl.multiple_of(c * rc, rc)
            win_tb[...] = tb_ext[pl.ds(r0, rc + 2 * HALO), :]
            win_dcb[...] = dcb_ext[pl.ds(r0, rc + 2 * HALO), :]
            dca_c = _tap(dca_sh, r0, HALO, rc)
            dglu = jnp.zeros((rc, C), F32)
            for k in range(KA):
                dglu = dglu + wa_ref[k:k + 1, :] * _tap(dca_sh, r0, HALO + pa - k, rc)
                acc_a[k] += fold(dca_c * _tap(ga_sh, r0, HALO - pa + k, rc))
            val = z_ref[pl.ds(r0, rc), 0:C]
            sg = _sigmoid(z_ref[pl.ds(r0, rc), C:2 * C])
            dz_ref[pl.ds(r0, rc), 0:C] = (dglu * sg).astype(BF16)
            dz_ref[pl.ds(r0, rc), C:2 * C] = (dglu * val * sg * (1.0 - sg)).astype(BF16)
            dcb_c = win_dcb[pl.ds(HALO, rc), :]
            cb = jnp.zeros((rc, C), F32) + bb_ref[...]
            dt = jnp.zeros((rc, C), F32)
            for k in range(KB):
                tb_k = win_tb[pl.ds(HALO - pb + k, rc), :]
                cb = cb + wb_ref[k:k + 1, :] * tb_k
                dt = dt + wb_ref[k:k + 1, :] * win_dcb[pl.ds(HALO + pb - k, rc), :]
                acc_b[k] += fold(dcb_c * tb_k)
            acc_bias[...] += fold(dcb_c)
            db_c = db_ref[pl.ds(r0, rc), :].astype(F32)
            dz_ref[pl.ds(r0, rc), 2 * C:3 * C] = (dt * z_ref[pl.ds(r0, rc), 4 * C:5 * C]).astype(BF16)
            dz_ref[pl.ds(r0, rc), 3 * C:4 * C] = (db_c * cb).astype(BF16)
            dz_ref[pl.ds(r0, rc), 4 * C:5 * C] = (dt * z_ref[pl.ds(r0, rc), 2 * C:3 * C]).astype(BF16)
            return carry

        lax.fori_loop(0, ts // rc, chunk, 0)

        @pl.when(last)
        def _():
            dwa_ref[...] = jnp.sum(acc_a[...], axis=1)
            dwb_ref[...] = jnp.sum(acc_b[...], axis=1)
            dbb_ref[...] = jnp.sum(acc_bias[...], axis=0, keepdims=True)

    zspec = pl.BlockSpec((ts, 5 * C), lambda i: (i, 0))
    zprev = pl.BlockSpec((HALO, 5 * C), prev0)
    znext = pl.BlockSpec((HALO, 5 * C), lambda i: nxt0(i, nt))
    dspec = pl.BlockSpec((ts, C), lambda i: (i, 0))
    dprev = pl.BlockSpec((HALO, C), prev0)
    dnext = pl.BlockSpec((HALO, C), lambda i: nxt0(i, nt))
    bspec = pl.BlockSpec((ts, C), lambda i: (i, 1))
    bprev = pl.BlockSpec((HALO, C), prev1)
    bnext = pl.BlockSpec((HALO, C), lambda i: nxt1(i, nt))

    def full(shape):
        return pl.BlockSpec(shape, lambda i: (0,) * len(shape))

    ext = pltpu.VMEM((ts + 2 * HALO, C), F32)
    shifted = pltpu.VMEM((SUBLANES, ts + 2 * HALO, C), F32)
    win = pltpu.VMEM((rc + 2 * HALO, C), F32)
    return pl.pallas_call(
        body, grid=(nt,),
        in_specs=[zspec, zprev, znext, dspec, dprev, dnext, bspec, bprev, bnext,
                  full(wa.shape), full(wb.shape), full(bb.shape)],
        out_specs=[pl.BlockSpec((ts, 5 * C), lambda i: (i, 0)), full((KA, C)), full((KB, C)), full((1, C))],
        out_shape=[jax.ShapeDtypeStruct((S, 5 * C), BF16), jax.ShapeDtypeStruct((KA, C), F32),
                   jax.ShapeDtypeStruct((KB, C), F32), jax.ShapeDtypeStruct((1, C), F32)],
        scratch_shapes=[shifted, shifted, ext, ext, win, win,
                        pltpu.VMEM((KA, 8, C), F32), pltpu.VMEM((KB, 8, C), F32), pltpu.VMEM((8, C), F32)],
        compiler_params=_cp("arbitrary"), name=name)(z, z, z, dca, dca, dca, dab, dab, dab, wa, wb, bb)


_GELU_C = 0.7978845608028654
_GELU_A = 0.044715


def _gelu(x):
    return 0.5 * x * (1.0 + jnp.tanh(_GELU_C * (x + _GELU_A * (x * x * x))))


def _gelu_and_grad(x):
    t = jnp.tanh(_GELU_C * (x + _GELU_A * (x * x * x)))
    hx = 0.5 * x
    return hx * (1.0 + t), 0.5 * (1.0 + t) + hx * (1.0 - t * t) * (_GELU_C * (1.0 + 3.0 * _GELU_A * x * x))


def _sgu_fwd(zp, lng, lnb, ws, bsb, name):
    S = zp.shape[0]
    D = zp.shape[1] // 2
    G = ws.shape[0]
    gd = D // G
    ts = _pick(S, TS_SGU, CHUNK)
    ncs = ts // CHUNK

    def body(zp_ref, lng_ref, lnb_ref, ws_ref, bsb_ref, y_ref, vb_ref):
        v = _gelu(zp_ref[:, D:2 * D])
        mu = jnp.mean(v, axis=-1, keepdims=True)
        xc = v - mu
        rstd = lax.rsqrt(jnp.mean(xc * xc, axis=-1, keepdims=True) + LN_EPS)
        vb_ref[...] = (xc * rstd * lng_ref[...] + lnb_ref[...]).astype(BF16)
        for c in range(ncs):
            rows = slice(c * CHUNK, (c + 1) * CHUNK)
            for g in range(G):
                cols = slice(g * gd, (g + 1) * gd)
                sv = _dot(ws_ref[g], vb_ref[rows, cols]) + bsb_ref[:, cols]
                y_ref[rows, cols] = (_gelu(zp_ref[rows, cols]) * sv).astype(BF16)

    def full(a):
        return pl.BlockSpec(a.shape, lambda i: (0,) * a.ndim)

    return pl.pallas_call(
        body, grid=(S // ts,),
        in_specs=[pl.BlockSpec((ts, 2 * D), lambda i: (i, 0)), full(lng), full(lnb), full(ws), full(bsb)],
        out_specs=pl.BlockSpec((ts, D), lambda i: (i, 0)), out_shape=jax.ShapeDtypeStruct((S, D), BF16),
        scratch_shapes=[pltpu.VMEM((ts, D), BF16)],
        compiler_params=_cp("parallel"), name=name)(zp, lng, lnb, ws, bsb)


def _sgu_bwd(dy, zp, lng, lnb, ws, wst, bsb, name):
    S = zp.shape[0]
    D = zp.shape[1] // 2
    G = ws.shape[0]
    gd = D // G
    ts = _pick(S, TS_SGU, CHUNK)
    ncs = ts // CHUNK

    def body(dy_ref, zp_ref, lng_ref, lnb_ref, ws_ref, wst_ref, bsb_ref,
             dzp_ref, dws_ref, dbs_ref, dg_ref, db_ref, vb_ref, dvln_ref, acc_bs):
        i = pl.program_id(0)

        @pl.when(i == 0)
        def _():
            dws_ref[...] = jnp.zeros_like(dws_ref)
            acc_bs[...] = jnp.zeros_like(acc_bs)
            dg_ref[...] = jnp.zeros_like(dg_ref)
            db_ref[...] = jnp.zeros_like(db_ref)

        v, dv_dz = _gelu_and_grad(zp_ref[:, D:2 * D])
        mu = jnp.mean(v, axis=-1, keepdims=True)
        xc = v - mu
        rstd = lax.rsqrt(jnp.mean(xc * xc, axis=-1, keepdims=True) + LN_EPS)
        xh = xc * rstd
        vb_ref[...] = (xh * lng_ref[...] + lnb_ref[...]).astype(BF16)
        for c in range(ncs):
            rows = slice(c * CHUNK, (c + 1) * CHUNK)
            for g in range(G):
                cols = slice(g * gd, (g + 1) * gd)
                u, du_dz = _gelu_and_grad(zp_ref[rows, cols])
                dy_ = dy_ref[rows, cols].astype(F32)
                sv = _dot(ws_ref[g], vb_ref[rows, cols]) + bsb_ref[:, cols]
                dzp_ref[rows, cols] = (dy_ * sv * du_dz).astype(BF16)
                dsv = dy_ * u
                acc_bs[:, cols] += dsv
                dsvb = dsv.astype(BF16)
                dws_ref[g] += _dot_nt(dsvb, vb_ref[rows, cols])
                dvln_ref[rows, cols] = _dot(wst_ref[g], dsvb)
        dvln = dvln_ref[...]
        dg_ref[...] += jnp.sum(dvln * xh, axis=0, keepdims=True)
        db_ref[...] += jnp.sum(dvln, axis=0, keepdims=True)
        dxh = dvln * lng_ref[...]
        dv = rstd * (dxh - jnp.mean(dxh, axis=-1, keepdims=True) - xh * jnp.mean(dxh * xh, axis=-1, keepdims=True))
        dzp_ref[:, D:2 * D] = (dv * dv_dz).astype(BF16)

        @pl.when(i == pl.num_programs(0) - 1)
        def _():
            dbs_ref[...] = acc_bs[...]

    def full(shape):
        return pl.BlockSpec(shape, lambda i: (0,) * len(shape))

    return pl.pallas_call(
        body, grid=(S // ts,),
        in_specs=[pl.BlockSpec((ts, D), lambda i: (i, 0)), pl.BlockSpec((ts, 2 * D), lambda i: (i, 0)),
                  full(lng.shape), full(lnb.shape), full(ws.shape), full(wst.shape), full(bsb.shape)],
        out_specs=[pl.BlockSpec((ts, 2 * D), lambda i: (i, 0)), full(ws.shape), full(bsb.shape),
                   full((1, D)), full((1, D))],
        out_shape=[jax.ShapeDtypeStruct((S, 2 * D), BF16), jax.ShapeDtypeStruct(ws.shape, F32),
                   jax.ShapeDtypeStruct(bsb.shape, F32), jax.ShapeDtypeStruct((1, D), F32),
                   jax.ShapeDtypeStruct((1, D), F32)],
        scratch_shapes=[pltpu.VMEM((ts, D), BF16), pltpu.VMEM((ts, D), F32),
                        pltpu.VMEM(bsb.shape, F32)],
        compiler_params=_cp("arbitrary"), name=name)(dy, zp, lng, lnb, ws, wst, bsb)


def _group_sum(x, groups, name):
    P, D = x.shape
    gd = D // groups

    def body(x_ref, o_ref):
        for g in range(groups):
            o_ref[:, g:g + 1] = jnp.sum(x_ref[:, g * gd:(g + 1) * gd], axis=1, keepdims=True)

    return pl.pallas_call(body, out_shape=jax.ShapeDtypeStruct((P, groups), F32), name=name)(x)


def _adamw(w, g, m, v, name):
    shape = w.shape
    C = shape[-1]
    R = w.size // C
    tr = _pick(R, 1024, 8)

    def body(w_ref, g_ref, m_ref, v_ref, d_ref, nm_ref, nv_ref):
        d_ref[...], nm_ref[...], nv_ref[...] = _adamw_math(w_ref[...], g_ref[...], m_ref[...], v_ref[...])

    tile = pl.BlockSpec((tr, C), lambda i: (i, 0))
    sh = jax.ShapeDtypeStruct((R, C), F32)
    outs = pl.pallas_call(
        body, grid=(R // tr,), in_specs=[tile] * 4, out_specs=[tile] * 3, out_shape=[sh] * 3,
        compiler_params=_cp("parallel"), name=name)(*(a.reshape(R, C) for a in (w, g, m, v)))
    return tuple(o.reshape(shape) for o in outs)


_HBM = pl.BlockSpec(memory_space=pltpu.HBM)


def _remote(src, dst, send_sem, recv_sem, to):
    return pltpu.make_async_remote_copy(src_ref=src, dst_ref=dst, send_sem=send_sem, recv_sem=recv_sem,
                                        device_id=to, device_id_type=MESH)


def _all_gather(block, name):
    R, C = block.shape

    def body(x_ref, out_ref, send_sems, recv_sems, local_sem):
        x, y, c = lax.axis_index("x"), lax.axis_index("y"), lax.axis_index("c")
        me, sibling = (x, y, c), (x, y, 1 - c)
        chips = [(1 - x, y), (x, 1 - y), (1 - x, 1 - y)]

        def slot(px, py, pc):
            return out_ref.at[4 * px + 2 * py + pc]

        def copy(k, blk, to, src=None):
            return _remote(slot(*blk) if src is None else src, slot(*blk), send_sems.at[k], recv_sems.at[k], to)

        mine = pltpu.make_async_copy(x_ref, slot(*me), local_sem)
        mine.start()
        first = [copy(0, me, sibling, src=x_ref)]
        first += [copy(1 + j, me, (*chip, c), src=x_ref) for j, chip in enumerate(chips)]
        for cp in first:
            cp.start()
        passed = [copy(4 + j, (*chip, c), sibling) for j, chip in enumerate(chips)]
        for j, chip in enumerate(chips):
            copy(1 + j, (*chip, c), me).wait_recv()
            passed[j].start()
        copy(0, sibling, me).wait_recv()
        for j, chip in enumerate(chips):
            copy(4 + j, (*chip, 1 - c), me).wait_recv()
        for cp in first + passed:
            cp.wait_send()
        mine.wait()

    return pl.pallas_call(
        body, out_shape=jax.ShapeDtypeStruct((NDEV, R, C), block.dtype), in_specs=[_HBM], out_specs=_HBM,
        scratch_shapes=[pltpu.SemaphoreType.DMA((7,)), pltpu.SemaphoreType.DMA((7,)), pltpu.SemaphoreType.DMA],
        name=name)(block)


def _all_gather_weights(pack, rows, name, after=None):
    C = pack.shape[1]
    nw = len(rows)
    starts = [sum(rows[:w]) for w in range(nw)]
    after_specs, after_ops = _after(after)

    def body(pack_ref, *rest):
        rest = rest[len(after_ops):]
        outs = rest[:nw]
        send_sems, recv_sems, local_sem = rest[nw:]
        x, y, c = lax.axis_index("x"), lax.axis_index("y"), lax.axis_index("c")
        me, sibling = (x, y, c), (x, y, 1 - c)
        chips = [(1 - x, y), (x, 1 - y), (1 - x, 1 - y)]

        def block(w, px, py, pc):
            return outs[w].at[pl.ds((4 * px + 2 * py + pc) * rows[w], rows[w])]

        def mine(w):
            return pack_ref.at[pl.ds(starts[w], rows[w])]

        def all_of(k):
            return _remote(pack_ref, pack_ref, send_sems.at[k], recv_sems.at[k], me)

        for w in range(nw):
            pltpu.make_async_copy(mine(w), block(w, *me), local_sem).start()
        for k, to in enumerate([sibling] + [(*chip, c) for chip in chips]):
            for w in range(nw):
                _remote(mine(w), block(w, *me), send_sems.at[k], recv_sems.at[k], to).start()
        for j, chip in enumerate(chips):
            all_of(1 + j).wait_recv()
            for w in range(nw):
                _remote(block(w, *chip, c), block(w, *chip, c), send_sems.at[4 + j], recv_sems.at[4 + j], sibling).start()
        all_of(0).wait_recv()
        for j in range(3):
            all_of(4 + j).wait_recv()
        for k in range(7):
            all_of(k).wait_send()
        pltpu.make_async_copy(pack_ref, pack_ref, local_sem).wait()

    return pl.pallas_call(
        body, out_shape=[jax.ShapeDtypeStruct((NDEV * r, C), pack.dtype) for r in rows],
        in_specs=[_HBM] + after_specs, out_specs=[_HBM] * nw,
        scratch_shapes=[pltpu.SemaphoreType.DMA((7,)), pltpu.SemaphoreType.DMA((7,)), pltpu.SemaphoreType.DMA],
        name=name)(pack, *after_ops)


_SEM = pl.BlockSpec(memory_space=pltpu.SEMAPHORE)
_DATAFLOW = pltpu.SideEffectType.DATAFLOW_SIDE_EFFECTING


def _split_start(srcs, lands, plan, n, after, name):
    nbuf = len(srcs) + len(lands)
    after_specs, after_ops = _after(after)

    def body(*refs):
        src_refs, land_refs = refs[:len(srcs)], refs[len(srcs):nbuf]
        send_sems, recv_sems = refs[nbuf + len(after_ops)], refs[nbuf + len(after_ops) + 1]
        for k, (src, dst, to) in enumerate(plan(src_refs, land_refs)):
            _remote(src, dst, send_sems.at[k], recv_sems.at[k], to).start()
        refs[-1][...] = jnp.zeros_like(refs[-1])

    bufs = [pltpu.with_memory_space_constraint(a, pltpu.HBM) for a in list(srcs) + list(lands)]
    outs = pl.pallas_call(
        body, name=name,
        out_shape=(pltpu.SemaphoreType.DMA((n,)), pltpu.SemaphoreType.DMA((n,)),
                   *[pltpu.HBM(a.shape, a.dtype) for a in bufs], jax.ShapeDtypeStruct((8, 128), F32)),
        in_specs=[_HBM] * nbuf + after_specs,
        out_specs=(_SEM, _SEM, *[_HBM] * nbuf, pl.BlockSpec(memory_space=pltpu.VMEM)),
        input_output_aliases={i: 2 + i for i in range(nbuf)},
        compiler_params=pltpu.CompilerParams(has_side_effects=_DATAFLOW))(*bufs, *after_ops)
    return outs[0], outs[1], list(outs[2:2 + len(srcs)]), list(outs[2 + len(srcs):2 + nbuf]), outs[-1]


def _split_wait(send_sems, recv_sems, srcs, lands, plan, after, name):
    nbuf = len(srcs) + len(lands)
    after_specs, after_ops = _after(after)

    def body(*refs):
        src_refs, land_refs = refs[:len(srcs)], refs[len(srcs):nbuf]
        send_sems_ref, recv_sems_ref = refs[nbuf], refs[nbuf + 1]
        for k, (src, dst, to) in enumerate(plan(src_refs, land_refs)):
            copy = _remote(src, dst, send_sems_ref.at[k], recv_sems_ref.at[k], to)
            copy.wait_send()
            copy.wait_recv()

    outs = pl.pallas_call(
        body, name=name, out_shape=tuple(pltpu.HBM(a.shape, a.dtype) for a in list(srcs) + list(lands)),
        in_specs=[_HBM] * nbuf + [_SEM, _SEM] + after_specs, out_specs=tuple([_HBM] * nbuf),
        input_output_aliases={i: i for i in range(nbuf)},
        compiler_params=pltpu.CompilerParams(has_side_effects=_DATAFLOW))(*srcs, *lands, send_sems, recv_sems, *after_ops)
    return list(outs[:len(srcs)]), list(outs[len(srcs):])


def _peers(x, y, c):
    return [(mask, (1 - x if mask & 4 else x, 1 - y if mask & 2 else y, 1 - c if mask & 1 else c))
            for mask in range(1, NDEV)]


def _gather_plan(rows):
    starts = [sum(rows[:w]) for w in range(len(rows))]

    def plan(src_refs, land_refs):
        x, y, c = lax.axis_index("x"), lax.axis_index("y"), lax.axis_index("c")
        copies = []
        for w, r in enumerate(rows):
            mine = src_refs[0].at[pl.ds(starts[w], r)]
            dst = land_refs[w].at[pl.ds((4 * x + 2 * y + c) * r, r)]
            copies += [(mine, dst, peer) for _, peer in _peers(x, y, c)]
        return copies

    return plan, (NDEV - 1) * len(rows)


def _place_own(shards, fulls, dev_idx, name):
    nw = len(shards)

    def body(i_ref, *refs):
        for w in range(nw):
            refs[2 * nw + w][...] = refs[w][...]

    grid_spec = pltpu.PrefetchScalarGridSpec(
        num_scalar_prefetch=1, grid=(1,),
        in_specs=[pl.BlockSpec(s.shape, lambda t, i_ref: (0, 0)) for s in shards] + [_ANY] * nw,
        out_specs=[pl.BlockSpec(s.shape, lambda t, i_ref: (i_ref[0], 0)) for s in shards])
    outs = pl.pallas_call(
        body, grid_spec=grid_spec, out_shape=[jax.ShapeDtypeStruct(f.shape, f.dtype) for f in fulls],
        input_output_aliases={1 + nw + w: w for w in range(nw)}, name=name)(dev_idx, *shards, *fulls)
    return list(outs)


def _scatter_plan(rows):
    def plan(src_refs, land_refs):
        x, y, c = lax.axis_index("x"), lax.axis_index("y"), lax.axis_index("c")
        copies = []
        for w, r in enumerate(rows):
            for mask, (px, py, pc) in _peers(x, y, c):
                src = src_refs[w].at[pl.ds((4 * px + 2 * py + pc) * r, r)]
                copies.append((src, land_refs[w].at[mask - 1], (px, py, pc)))
        return copies

    return plan, (NDEV - 1) * len(rows)


def _adamw_math(w, g, m, v):
    nm = ADAM_B1 * m + (1.0 - ADAM_B1) * g
    nv = ADAM_B2 * v + (1.0 - ADAM_B2) * (g * g)
    bc1 = 1.0 - ADAM_B1 ** ADAM_STEP
    bc2 = 1.0 - ADAM_B2 ** ADAM_STEP
    return -ADAM_LR * ((nm / bc1) / (jnp.sqrt(nv / bc2) + ADAM_EPS) + ADAM_WD * w), nm, nv


def _finish_weight(gs, gots, dev_idx, w, m, v, name, after=None):
    L = len(gs)
    n1, r, C = gots[0].shape
    block = (None,) + w.shape[1:]

    after_specs, after_ops = _after(after)

    def body(i_ref, *refs):
        ins, (w_ref, m_ref, v_ref), (g_out, d_out, m_out, v_out) = refs[:2 * L], refs[2 * L:2 * L + 3], refs[-4:]
        for layer in range(L):
            @pl.when(pl.program_id(0) == layer)
            def _():
                g_ref, got_ref = ins[2 * layer], ins[2 * layer + 1]
                acc = g_ref[...].astype(F32)
                for k in range(n1):
                    acc = acc + got_ref[k].astype(F32)
                g_out[...] = acc
                d_out[...], m_out[...], v_out[...] = _adamw_math(w_ref[...], acc, m_ref[...], v_ref[...])

    in_specs, ins = [], []
    for g, got in zip(gs, gots):
        in_specs += [pl.BlockSpec((r, C), lambda t, i_ref: (i_ref[0], 0), pipeline_mode=_RESIDENT),
                     pl.BlockSpec((n1, r, C), lambda t, i_ref: (0, 0, 0), pipeline_mode=_RESIDENT)]
        ins += [g, got]
    per_layer = pl.BlockSpec(block, lambda t, i_ref: (t, 0, 0))
    grid_spec = pltpu.PrefetchScalarGridSpec(
        num_scalar_prefetch=1, grid=(L,), in_specs=in_specs + [per_layer] * 3 + after_specs,
        out_specs=[per_layer] * 4)
    return pl.pallas_call(
        body, grid_spec=grid_spec, out_shape=[jax.ShapeDtypeStruct(w.shape, F32)] * 4,
        compiler_params=_cp("arbitrary"), name=name)(dev_idx, *ins, w, m, v, *after_ops)


def _sum_slots(a, name):
    n, R, C = a.shape

    def body(a_ref, o_ref):
        acc = a_ref[0]
        for k in range(1, n):
            acc = acc + a_ref[k]
        o_ref[...] = acc

    return pl.pallas_call(body, out_shape=jax.ShapeDtypeStruct((R, C), F32), name=name)(a)


def _shard_axis(name):
    return {"ev_w_in": 2, "ev_a_conv_w": 2, "ev_b_conv_w": 2, "ev_w_out": 1, "od_w_in": 2, "od_c_ln_g": 1,
            "od_c_ln_b": 1, "od_w_out": 1, "xa_w_q": 1, "xa_w_k": 1, "xa_w_v": 1, "xa_w_o": 1,
            "ffn_w_gate": 2, "ffn_w_up": 2, "ffn_w_down": 1}[name]


BIG = ["ev_w_in", "ev_w_out", "od_w_in", "od_w_out", "xa_w_q", "xa_w_k", "xa_w_v", "xa_w_o",
       "ffn_w_gate", "ffn_w_up", "ffn_w_down"]
SMALL_SHARDED = ["ev_a_conv_w", "ev_b_conv_w", "od_c_ln_g", "od_c_ln_b"]
REPLICATED = ["g_mix", "g_xattn", "g_mem", "g_ffn", "g_final", "ev_a_conv_b", "ev_a_ln_g", "ev_a_ln_b",
              "ev_b_conv_b", "od_w_s", "od_b_s"]
WEIGHTS = ["g_mix", "g_xattn", "g_mem", "g_ffn", "g_final", "ev_w_in", "ev_a_conv_w", "ev_a_conv_b", "ev_a_ln_g",
           "ev_a_ln_b", "ev_b_conv_w", "ev_b_conv_b", "ev_w_out", "od_w_in", "od_c_ln_g", "od_c_ln_b", "od_w_s",
           "od_b_s", "od_w_out", "xa_w_q", "xa_w_k", "xa_w_v", "xa_w_o", "ffn_w_gate", "ffn_w_up", "ffn_w_down"]


def _full_from_blocks(blocks, axis):
    shard = blocks.shape[1:]
    full = jnp.moveaxis(blocks, 0, axis)
    return full.reshape(shard[:axis] + (NDEV * shard[axis],) + shard[axis + 1:])


def _blocks_from_full(full, axis):
    shp = full.shape
    split = full.reshape(shp[:axis] + (NDEV, shp[axis] // NDEV) + shp[axis + 1:])
    return jnp.moveaxis(split, axis, 0)


def _pad_rows(flat, width, row_align):
    per = width * row_align
    n = -(-flat.shape[0] // per) * per
    return jnp.pad(flat, (0, n - flat.shape[0])).reshape(n // width, width)


def _row(v):
    return v.reshape(1, -1)


def _xattn_fwd(h, nq, mem, g_m, wq, wk, wv, wo, g_next, tag, after):
    mem_n = _rms_fwd(mem, _row(g_m), f"xa_mem_rms_{tag}")
    q = _mm([(nq, wq, "nn")], f"xa_q_{tag}", out_dtype=BF16, after=after)
    k = _mm([(mem_n, wk, "nn")], f"xa_k_{tag}", out_dtype=BF16)
    v = _mm([(mem_n, wv, "nn")], f"xa_v_{tag}", out_dtype=BF16)
    o = _attn_fwd(q, k, v, f"xa_attn_{tag}")
    h_new, n_next = _mm([(o, wo, "nn")], f"xa_o_{tag}", res=h, rms_g=_row(g_next))
    return h_new, n_next, (h, nq, mem_n, q, k, v, o)


def _xattn_bwd(dh_new, saved, mem, g_x, g_m, wq, wk, wv, wo, tag, push):
    h, nq, mem_n, q, k, v, o = saved
    do = _mm([(dh_new, wo, "nt")], f"xa_do_{tag}", out_dtype=BF16)
    d_wo = _mm_tn(o, dh_new, f"xa_dwo_{tag}")
    dq, dk, dv = _attn_bwd(q, k, v, do, f"xa_attn_bwd_{tag}")
    d_wq = _mm_tn(nq, dq, f"xa_dwq_{tag}")
    d_wk = _mm_tn(mem_n, dk, f"xa_dwk_{tag}")
    d_wv = _mm_tn(mem_n, dv, f"xa_dwv_{tag}")
    token = push([d_wq, d_wk, d_wv, d_wo])
    dmem_n = _mm([(dk, wk, "nt"), (dv, wv, "nt")], f"xa_dmem_{tag}", after=token)
    _, d_gm = _rms_bwd(dmem_n, mem, _row(g_m), None, f"xa_mem_rms_bwd_{tag}")
    dh, d_gx = _mm([(dq, wq, "nt")], f"xa_dnq_{tag}", rms_bwd=(h, _row(g_x), dh_new), tm=512, after=token)
    return dh, dict(g_xattn=d_gx, g_mem=d_gm)


def _ffn_fwd(h, n, wgt, wut, wd, g_next, tag, after):
    a, b, hid = _ffn_up(n, wgt, wut, f"ffn_up_{tag}", after=after)
    if g_next is None:
        h_new, n_next = _mm([(hid, wd, "nn")], f"ffn_down_{tag}", res=h, tm=512, tn=1024), None
    else:
        h_new, n_next = _mm([(hid, wd, "nn")], f"ffn_down_{tag}", res=h, rms_g=_row(g_next), tm=512)
    return h_new, n_next, (h, n, a, b, hid)


def _ffn_bwd(dh_new, saved, g_f, wgt, wut, wd, tag, push):
    h, n, a, b, hid = saved
    da, db = _ffn_dhid(dh_new, wd, a, b, f"ffn_dhid_{tag}")
    d_wd = _mm_tn(hid, dh_new, f"ffn_dwd_{tag}", tn=512)
    d_wgt = _mm_tn(da, n, f"ffn_dwg_{tag}", tn=512)
    d_wut = _mm_tn(db, n, f"ffn_dwu_{tag}", tn=512)
    token = push([d_wgt, d_wut, d_wd])
    dh, d_gf = _mm([(da, wgt, "nn"), (db, wut, "nn")], f"ffn_dn_{tag}", rms_bwd=(h, _row(g_f), dh_new), tm=512,
                   after=token)
    return dh, dict(g_ffn=d_gf)


_XA = ["xa_w_q", "xa_w_k", "xa_w_v", "xa_w_o"]
_FFN = ["ffn_w_gate", "ffn_w_up", "ffn_w_down"]
GATHERS = {
    "ev_in": [("ev_w_in", 0)],
    "xa0": [("ev_w_out", 0)] + [(n, 0) for n in _XA],
    "ffn0": [(n, 0) for n in _FFN],
    "od": [("od_w_in", 0), ("od_w_out", 0)],
    "xa1": [(n, 1) for n in _XA],
    "ffn1": [(n, 1) for n in _FFN],
}
SCATTERS = {
    "ffn1": [(n, 1) for n in _FFN],
    "xa1": [(n, 1) for n in _XA],
    "od": [("od_w_in", 0), ("od_w_out", 0)],
    "ffn0": [(n, 0) for n in _FFN],
    "xa0": [(n, 0) for n in _XA],
    "ev_out": [("ev_w_out", 0)],
    "ev_in": [("ev_w_in", 0)],
}


def _local_step(x, mem, loss_target, W, comm):
    grads = {}

    h0 = x
    (ev_w_in_t,), token = comm.weights("ev_in", None)
    n0 = _rms_fwd(h0, _row(W["g_mix"][0]), "ev_rms", after=token)
    z = _mm([(n0, ev_w_in_t, "nt")], "ev_in", tn=1280)
    token = comm.prefetch(["ffn0"], z)
    ab, ca = _conv_fwd(z, W["ev_a_conv_w"][0], W["ev_a_conv_b"], W["ev_a_ln_g"], W["ev_a_ln_b"],
                       W["ev_b_conv_w"][0], W["ev_b_conv_b"], "ev_conv", after=token)
    (ev_w_out, *xa_w0), _ = comm.weights("xa0", ab)
    h1, nq0 = _mm([(ab, ev_w_out, "nn")], "ev_out", res=h0, rms_g=_row(W["g_xattn"][0]))
    token = comm.prefetch(["od", "xa1"], nq0)
    h2, nf0, xa0 = _xattn_fwd(h1, nq0, mem, W["g_mem"][0], *xa_w0, W["g_ffn"][0], "l0", token)
    ffn_w0, _ = comm.weights("ffn0", nf0)
    token = comm.prefetch(["ffn1"], nf0)
    h3, n3, ff0 = _ffn_fwd(h2, nf0, *ffn_w0, W["g_mix"][1], "l0", token)

    (od_w_in_t, od_w_out), _ = comm.weights("od", n3)
    zp = _mm([(n3, od_w_in_t, "nt")], "od_in", tn=1024)
    D = x.shape[1]
    ws = W["od_w_s"][0].astype(BF16)
    wst = jnp.swapaxes(ws, 1, 2)
    bsb = jnp.repeat(jnp.transpose(W["od_b_s"][0]), D // C_GROUPS, axis=1)
    y_sgu = _sgu_fwd(zp, W["od_c_ln_g"], W["od_c_ln_b"], ws, bsb, "od_sgu")
    h4, nq1 = _mm([(y_sgu, od_w_out, "nn")], "od_out", res=h3, rms_g=_row(W["g_xattn"][1]))
    xa_w1, _ = comm.weights("xa1", nq1)
    h5, nf1, xa1 = _xattn_fwd(h4, nq1, mem, W["g_mem"][1], *xa_w1, W["g_ffn"][1], "l1", None)
    ffn_w1, _ = comm.weights("ffn1", nf1)
    h6, _, ff1 = _ffn_fwd(h5, nf1, *ffn_w1, None, "l1", None)

    loss_row, dh6, d_gfinal = _loss_bwd(h6, _row(W["g_final"]), loss_target, "loss")
    grads["g_final"] = d_gfinal.reshape(-1)

    dh5, g_ff1 = _ffn_bwd(dh6, ff1, W["g_ffn"][1], *ffn_w1, "l1", lambda dws: comm.grads("ffn1", dws))
    dh4, g_xa1 = _xattn_bwd(dh5, xa1, mem, W["g_xattn"][1], W["g_mem"][1], *xa_w1, "l1",
                            lambda dws: comm.grads("xa1", dws))
    dy_sgu = _mm([(dh4, od_w_out, "nt")], "od_dy", tn=1024)
    d_od_out = _mm_tn(y_sgu, dh4, "od_dwout", tn=1024)
    dzp, d_ws, d_bsb, d_clng, d_clnb = _sgu_bwd(dy_sgu, zp, W["od_c_ln_g"], W["od_c_ln_b"], ws, wst, bsb, "od_sgu_bwd")
    grads["od_w_s"] = d_ws[None]
    grads["od_b_s"] = jnp.transpose(_group_sum(d_bsb, C_GROUPS, "od_dbs"))[None]
    grads["od_c_ln_g"], grads["od_c_ln_b"] = d_clng, d_clnb
    token = comm.grads("od", [_mm_tn(dzp, n3, "od_dwin", tn=512), d_od_out])
    dh3, d_gmix1 = _mm([(dzp, od_w_in_t, "nn")], "od_dn", rms_bwd=(h3, _row(W["g_mix"][1]), dh4), tm=512, after=token)

    dh2, g_ff0 = _ffn_bwd(dh3, ff0, W["g_ffn"][0], *ffn_w0, "l0", lambda dws: comm.grads("ffn0", dws))
    dh1, g_xa0 = _xattn_bwd(dh2, xa0, mem, W["g_xattn"][0], W["g_mem"][0], *xa_w0, "l0",
                            lambda dws: comm.grads("xa0", dws))
    token = comm.grads("ev_out", [_mm_tn(ab, dh1, "ev_dwout", tn=1024)])
    dab = _mm([(dh1, ev_w_out, "nt")], "ev_dab", tn=1024, after=token)
    dca, d_lng, d_lnb, d_ba = _conv_bwd_ln(dab, ca, W["ev_a_ln_g"], W["ev_a_ln_b"], "ev_conv_bwd_ln")
    dz, d_wa, d_wb, d_bb = _conv_bwd(z, dca, dab, W["ev_a_conv_w"][0], W["ev_b_conv_w"][0], W["ev_b_conv_b"],
                                     "ev_conv_bwd")
    grads.update(ev_a_ln_g=d_lng, ev_a_ln_b=d_lnb, ev_a_conv_b=d_ba, ev_b_conv_b=d_bb,
                 ev_a_conv_w=d_wa[None], ev_b_conv_w=d_wb[None])
    token = comm.grads("ev_in", [_mm_tn(dz, n0, "ev_dwin", tn=512)])
    grad_x, d_gmix0 = _mm([(dz, ev_w_in_t, "nn")], "ev_dn", rms_bwd=(h0, _row(W["g_mix"][0]), dh1), tm=512, after=token)

    grads["g_mix"] = jnp.concatenate([d_gmix0, d_gmix1], axis=0)
    for key in ("g_xattn", "g_mem"):
        grads[key] = jnp.concatenate([g_xa0[key], g_xa1[key]], axis=0)
    grads["g_ffn"] = jnp.concatenate([g_ff0["g_ffn"], g_ff1["g_ffn"]], axis=0)
    return loss_row, grad_x, grads


class _Exchanges:
    def __init__(self, shards, dev_idx, after):
        self.shards, self.dev_idx = shards, dev_idx
        self.gathering, self.scattering = {}, {}
        self.first = _all_gather_weights(self._pack(GATHERS["ev_in"]), self._rows(GATHERS["ev_in"]), "ag_ev_in",
                                         after=after)
        self.first_token = self.prefetch(["xa0"], self.first[0])

    def _rows(self, entries):
        return [self.shards[e].shape[0] for e in entries]

    def _pack(self, entries):
        return jnp.concatenate([self.shards[e] for e in entries], axis=0)

    def prefetch(self, gathers, after):
        for name in gathers:
            rows = self._rows(GATHERS[name])
            pack = self._pack(GATHERS[name])
            lands = [lax.empty((NDEV * r, pack.shape[1]), pack.dtype) for r in rows]
            plan, n = _gather_plan(rows)
            send, recv, srcs, lands, after = _split_start([pack], lands, plan, n, after, f"ag_{name}_start")
            self.gathering[name] = (send, recv, srcs, lands, plan, rows)
        return after

    def weights(self, name, after):
        if name == "ev_in":
            return self.first, self.first_token
        send, recv, srcs, lands, plan, rows = self.gathering.pop(name)
        _, lands = _split_wait(send, recv, srcs, lands, plan, after, f"ag_{name}_wait")
        return _place_own([self.shards[e] for e in GATHERS[name]], lands, self.dev_idx, f"ag_{name}_own"), None

    def grads(self, name, dws):
        rows = self._rows(SCATTERS[name])
        lands = [lax.empty((NDEV - 1, r, d.shape[1]), d.dtype) for r, d in zip(rows, dws)]
        plan, n = _scatter_plan(rows)
        send, recv, srcs, lands, token = _split_start(dws, lands, plan, n, None, f"rs_{name}_start")
        self.scattering[name] = (send, recv, srcs, lands, plan)
        return token

    def received(self, after):
        out = {}
        for name, (send, recv, srcs, lands, plan) in self.scattering.items():
            srcs, lands = _split_wait(send, recv, srcs, lands, plan, after, f"rs_{name}_wait")
            for entry, g, got in zip(SCATTERS[name], srcs, lands):
                out[entry] = (g, got)
        return out


def kernel(x, mem, g_mix, g_xattn, g_mem, g_ffn, g_final, ev_w_in, ev_a_conv_w, ev_a_conv_b, ev_a_ln_g, ev_a_ln_b, ev_b_conv_w, ev_b_conv_b, ev_w_out, od_w_in, od_c_ln_g, od_c_ln_b, od_w_s, od_b_s, od_w_out, xa_w_q, xa_w_k, xa_w_v, xa_w_o, ffn_w_gate, ffn_w_up, ffn_w_down, loss_target, m_g_mix, m_g_xattn, m_g_mem, m_g_ffn, m_g_final, m_ev_w_in, m_ev_a_conv_w, m_ev_a_conv_b, m_ev_a_ln_g, m_ev_a_ln_b, m_ev_b_conv_w, m_ev_b_conv_b, m_ev_w_out, m_od_w_in, m_od_c_ln_g, m_od_c_ln_b, m_od_w_s, m_od_b_s, m_od_w_out, m_xa_w_q, m_xa_w_k, m_xa_w_v, m_xa_w_o, m_ffn_w_gate, m_ffn_w_up, m_ffn_w_down, v_g_mix, v_g_xattn, v_g_mem, v_g_ffn, v_g_final, v_ev_w_in, v_ev_a_conv_w, v_ev_a_conv_b, v_ev_a_ln_g, v_ev_a_ln_b, v_ev_b_conv_w, v_ev_b_conv_b, v_ev_w_out, v_od_w_in, v_od_c_ln_g, v_od_c_ln_b, v_od_w_s, v_od_b_s, v_od_w_out, v_xa_w_q, v_xa_w_k, v_xa_w_v, v_xa_w_o, v_ffn_w_gate, v_ffn_w_up, v_ffn_w_down):
    local = dict(g_mix=g_mix, g_xattn=g_xattn, g_mem=g_mem, g_ffn=g_ffn, g_final=g_final, ev_w_in=ev_w_in, ev_a_conv_w=ev_a_conv_w, ev_a_conv_b=ev_a_conv_b, ev_a_ln_g=ev_a_ln_g, ev_a_ln_b=ev_a_ln_b, ev_b_conv_w=ev_b_conv_w, ev_b_conv_b=ev_b_conv_b, ev_w_out=ev_w_out, od_w_in=od_w_in, od_c_ln_g=od_c_ln_g, od_c_ln_b=od_c_ln_b, od_w_s=od_w_s, od_b_s=od_b_s, od_w_out=od_w_out, xa_w_q=xa_w_q, xa_w_k=xa_w_k, xa_w_v=xa_w_v, xa_w_o=xa_w_o, ffn_w_gate=ffn_w_gate, ffn_w_up=ffn_w_up, ffn_w_down=ffn_w_down)
    mom = dict(g_mix=m_g_mix, g_xattn=m_g_xattn, g_mem=m_g_mem, g_ffn=m_g_ffn, g_final=m_g_final, ev_w_in=m_ev_w_in, ev_a_conv_w=m_ev_a_conv_w, ev_a_conv_b=m_ev_a_conv_b, ev_a_ln_g=m_ev_a_ln_g, ev_a_ln_b=m_ev_a_ln_b, ev_b_conv_w=m_ev_b_conv_w, ev_b_conv_b=m_ev_b_conv_b, ev_w_out=m_ev_w_out, od_w_in=m_od_w_in, od_c_ln_g=m_od_c_ln_g, od_c_ln_b=m_od_c_ln_b, od_w_s=m_od_w_s, od_b_s=m_od_b_s, od_w_out=m_od_w_out, xa_w_q=m_xa_w_q, xa_w_k=m_xa_w_k, xa_w_v=m_xa_w_v, xa_w_o=m_xa_w_o, ffn_w_gate=m_ffn_w_gate, ffn_w_up=m_ffn_w_up, ffn_w_down=m_ffn_w_down)
    vel = dict(g_mix=v_g_mix, g_xattn=v_g_xattn, g_mem=v_g_mem, g_ffn=v_g_ffn, g_final=v_g_final, ev_w_in=v_ev_w_in, ev_a_conv_w=v_ev_a_conv_w, ev_a_conv_b=v_ev_a_conv_b, ev_a_ln_g=v_ev_a_ln_g, ev_a_ln_b=v_ev_a_ln_b, ev_b_conv_w=v_ev_b_conv_w, ev_b_conv_b=v_ev_b_conv_b, ev_w_out=v_ev_w_out, od_w_in=v_od_w_in, od_c_ln_g=v_od_c_ln_g, od_c_ln_b=v_od_c_ln_b, od_w_s=v_od_w_s, od_b_s=v_od_b_s, od_w_out=v_od_w_out, xa_w_q=v_xa_w_q, xa_w_k=v_xa_w_k, xa_w_v=v_xa_w_v, xa_w_o=v_xa_w_o, ffn_w_gate=v_ffn_w_gate, ffn_w_up=v_ffn_w_up, ffn_w_down=v_ffn_w_down)
    D = x.shape[-1]
    dev = 4 * lax.axis_index("x") + 2 * lax.axis_index("y") + lax.axis_index("c")

    def comm_layout(n, a):
        return jnp.transpose(a) if _shard_axis(n) == 2 else a

    shards = {(n, i): comm_layout(n, local[n][i]).astype(BF16) for n in BIG for i in range(local[n].shape[0])}
    small_sizes = [local[n].size for n in SMALL_SHARDED]
    small_block = _pad_rows(jnp.concatenate([local[n].reshape(-1) for n in SMALL_SHARDED]), 128, 8)
    small_all = _all_gather(small_block, "ag_small")
    comm = _Exchanges(shards, jnp.reshape(dev, (1,)).astype(jnp.int32), small_all)
    small_all = small_all.reshape(NDEV, -1)

    W = {n: local[n] for n in REPLICATED}
    o0 = 0
    for n, sz in zip(SMALL_SHARDED, small_sizes):
        blocks = small_all[:, o0:o0 + sz].reshape((NDEV,) + local[n].shape)
        W[n] = _full_from_blocks(blocks, _shard_axis(n))
        o0 += sz

    loss_row, grad_x, grads = _local_step(x[0], mem[0], loss_target[0], W, comm)

    received = comm.received(grad_x)
    rest = REPLICATED + SMALL_SHARDED
    rest_full_shapes = [grads[n].shape for n in rest]
    g_rest = _pad_rows(jnp.concatenate([grads[n].astype(F32).reshape(-1) for n in rest]), D, 8)
    small_rows = g_rest.shape[0]
    small_plan, small_n = _gather_plan([small_rows])
    small_send, small_recv, small_srcs, small_lands, token = _split_start(
        [g_rest], [lax.empty((NDEV * small_rows, D), F32)], small_plan, small_n, received["ev_w_in", 0][1],
        "ag_small_grads_start")

    gsh, delta, new_m, new_v = {}, {}, {}, {}
    def stacked_layout(n, a):
        return jnp.swapaxes(a, 1, 2) if _shard_axis(n) == 2 else a

    for n in BIG:
        parts = [received[n, i] for i in range(local[n].shape[0])]
        outs = _finish_weight([p[0] for p in parts], [p[1] for p in parts], comm.dev_idx,
                              *(stacked_layout(n, a) for a in (local[n], mom[n], vel[n])), f"finish_{n}", after=token)
        gsh[n], delta[n], new_m[n], new_v[n] = (stacked_layout(n, o) for o in outs)

    _, small_lands = _split_wait(small_send, small_recv, small_srcs, small_lands, small_plan, delta[BIG[-1]],
                                 "ag_small_grads_wait")
    partials = _place_own([g_rest], small_lands, comm.dev_idx, "ag_small_grads_own")[0]
    g_rest = _sum_slots(partials.reshape(NDEV, small_rows, D), "sum_small_grads").reshape(-1)
    o0 = 0
    for n, shp in zip(rest, rest_full_shapes):
        sz = 1
        for s in shp:
            sz *= s
        full = g_rest[o0:o0 + sz].reshape(shp)
        o0 += sz
        if n in SMALL_SHARDED:
            full = lax.dynamic_index_in_dim(_blocks_from_full(full, _shard_axis(n)), dev, 0, keepdims=False)
        gsh[n] = full.reshape(local[n].shape)

    for n in rest:
        delta[n], new_m[n], new_v[n] = _adamw(local[n], gsh[n], mom[n], vel[n], f"adamw_{n}")

    loss = lax.psum(loss_row[0, 0], ("x", "y", "c"))
    return (loss, grad_x[None], *[gsh[n] for n in WEIGHTS], *[delta[n] for n in WEIGHTS],
            *[new_m[n] for n in WEIGHTS], *[new_v[n] for n in WEIGHTS])
```

```python
import jax
import jax.numpy as jnp
from jax import lax
from jax.experimental import pallas as pl
from jax.experimental.pallas import tpu as pltpu

F32, BF16 = jnp.float32, jnp.bfloat16
NDEV = 8
RMS_EPS = 1e-6
LN_EPS = 1e-5
CHUNK = 128
C_GROUPS = 8
XA_HEADS = 4
ADAM_LR, ADAM_B1, ADAM_B2, ADAM_EPS, ADAM_WD, ADAM_STEP = 0.001, 0.9, 0.999, 1e-08, 0.01, 10
HALO = 16
ROW_CHUNK = 32
V7X_VMEM_LIMIT = 56 * 1024 * 1024
MESH = pl.DeviceIdType.MESH

TS_ROW = 512
TS_MM = 2048
TN_MM = 1408
TS_FFN = 512
MM_ROW_CHUNK = 256
TS_CONV = 512
TS_SGU = 512
TS_ATTN = 2048


def _cp(*sem):
    return pltpu.CompilerParams(dimension_semantics=sem, vmem_limit_bytes=V7X_VMEM_LIMIT)


def _pick(n, pref, align):
    for t in range(min(n, pref), 0, -1):
        if n % t == 0 and (t % align == 0 or t == n):
            return t
    return n


def _sigmoid(x):
    return 0.5 * jnp.tanh(0.5 * x) + 0.5


def _dot(a, b):
    return jnp.dot(a, b, preferred_element_type=F32)


def _dot_nt(a, b):
    return lax.dot_general(a, b, (((1,), (1,)), ((), ())), preferred_element_type=F32)


def _dot_tn(a, b):
    return lax.dot_general(a, b, (((0,), (0,)), ((), ())), preferred_element_type=F32)


_ANY = pl.BlockSpec(memory_space=pl.ANY)
_RESIDENT = pl.Buffered(1)


def _after(after):
    return ([], []) if after is None else ([_ANY], [after])


def _rms_fwd(h, g, name, after=None):
    S, D = h.shape
    ts = _pick(S, TS_MM, 16)
    after_specs, after_ops = _after(after)

    def body(h_ref, g_ref, *rest):
        o_ref = rest[-1]
        x = h_ref[...]
        r = lax.rsqrt(jnp.mean(x * x, axis=-1, keepdims=True) + RMS_EPS)
        o_ref[...] = ((x * r) * g_ref[...]).astype(o_ref.dtype)

    return pl.pallas_call(
        body, grid=(S // ts,),
        in_specs=[pl.BlockSpec((ts, D), lambda i: (i, 0)), pl.BlockSpec((1, D), lambda i: (0, 0))] + after_specs,
        out_specs=pl.BlockSpec((ts, D), lambda i: (i, 0)),
        out_shape=jax.ShapeDtypeStruct((S, D), BF16), compiler_params=_cp("parallel"), name=name)(h, g, *after_ops)


def _rms_bwd(dn, h, g, dres, name):
    S, D = h.shape
    ts = _pick(S, TS_ROW, 8)
    has_res = dres is not None

    def body(*refs):
        if has_res:
            dn_ref, h_ref, g_ref, dres_ref, dh_ref, dg_ref = refs
        else:
            dn_ref, h_ref, g_ref, dh_ref, dg_ref = refs
        x = h_ref[...]
        dn_ = dn_ref[...].astype(F32)
        r = lax.rsqrt(jnp.mean(x * x, axis=-1, keepdims=True) + RMS_EPS)
        xr = x * r

        @pl.when(pl.program_id(0) == 0)
        def _():
            dg_ref[...] = jnp.zeros_like(dg_ref)

        dg_ref[...] += jnp.sum(dn_ * xr, axis=0, keepdims=True)
        u = dn_ * g_ref[...]
        dh = r * u - xr * (r * jnp.mean(u * xr, axis=-1, keepdims=True))
        if has_res:
            dh = dh + dres_ref[...]
        dh_ref[...] = dh

    tile = pl.BlockSpec((ts, D), lambda i: (i, 0))
    vec = pl.BlockSpec((1, D), lambda i: (0, 0))
    ins = [dn, h, g] + ([dres] if has_res else [])
    return pl.pallas_call(
        body, grid=(S // ts,),
        in_specs=[tile, tile, vec] + ([tile] if has_res else []),
        out_specs=[tile, vec],
        out_shape=[jax.ShapeDtypeStruct((S, D), F32), jax.ShapeDtypeStruct((1, D), F32)],
        compiler_params=_cp("arbitrary"), name=name)(*ins)


def _loss_bwd(h, g, target, name):
    S, D = h.shape
    ts = _pick(S, TS_ROW, 8)

    def body(h_ref, g_ref, t_ref, loss_ref, dh_ref, dg_ref):
        x = h_ref[...]
        r = lax.rsqrt(jnp.mean(x * x, axis=-1, keepdims=True) + RMS_EPS)
        xr = x * r
        gg = g_ref[...]
        e = xr * gg - t_ref[...]

        @pl.when(pl.program_id(0) == 0)
        def _():
            dg_ref[...] = jnp.zeros_like(dg_ref)
            loss_ref[...] = jnp.zeros_like(loss_ref)

        tile_loss = jnp.sum(jnp.sum(e * e, axis=0, keepdims=True), axis=1, keepdims=True) * (0.5 / D)
        loss_ref[...] += jnp.broadcast_to(tile_loss, loss_ref.shape)
        dy = e * (1.0 / D)
        dg_ref[...] += jnp.sum(dy * xr, axis=0, keepdims=True)
        u = dy * gg
        dh_ref[...] = r * u - xr * (r * jnp.mean(u * xr, axis=-1, keepdims=True))

    tile = pl.BlockSpec((ts, D), lambda i: (i, 0))
    vec = pl.BlockSpec((1, D), lambda i: (0, 0))
    return pl.pallas_call(
        body, grid=(S // ts,),
        in_specs=[tile, vec, tile],
        out_specs=[pl.BlockSpec((1, 128), lambda i: (0, 0)), tile, vec],
        out_shape=[jax.ShapeDtypeStruct((1, 128), F32), jax.ShapeDtypeStruct((S, D), F32),
                   jax.ShapeDtypeStruct((1, D), F32)],
        compiler_params=_cp("arbitrary"), name=name)(h, g, target)


def _mm(pairs, name, out_dtype=F32, res=None, rms_g=None, rms_bwd=None, tm=None, tn=None, after=None):
    M = pairs[0][0].shape[0]
    N = pairs[0][1].shape[1 if pairs[0][2] == "nn" else 0]
    whole_rows = rms_g is not None or rms_bwd is not None
    tm = _pick(M, tm or TS_MM, 16)
    tn = N if whole_rows else _pick(N, tn or TN_MM, 128)
    npair = len(pairs)
    modes = [p[2] for p in pairs]
    after_specs, after_ops = _after(after)

    rc = MM_ROW_CHUNK if whole_rows and tm % MM_ROW_CHUNK == 0 else tm

    def body(*refs):
        rest = refs[2 * npair + len(after_ops):]
        res_ref = None
        if res is not None:
            res_ref, rest = rest[0], rest[1:]
        if rms_bwd is not None:
            dg_ref = rest[4]

            @pl.when(pl.program_id(0) == 0)
            def _():
                dg_ref[...] = jnp.zeros_like(dg_ref)

        for r0 in range(0, tm, rc):
            rows = pl.ds(r0, rc)
            acc = None
            for p in range(npair):
                a_ = refs[2 * p][rows, :].astype(BF16)
                d = _dot(a_, refs[2 * p + 1][...]) if modes[p] == "nn" else _dot_nt(a_, refs[2 * p + 1][...])
                acc = d if acc is None else acc + d
            if res_ref is not None:
                acc = acc + res_ref[rows, :]
            if rms_bwd is not None:
                h_ref, g_ref, dres_ref, dh_ref, _ = rest
                x = h_ref[rows, :]
                r = lax.rsqrt(jnp.mean(x * x, axis=-1, keepdims=True) + RMS_EPS)
                xr = x * r
                dg_ref[...] += jnp.sum(acc * xr, axis=0, keepdims=True)
                u = acc * g_ref[...]
                dh_ref[rows, :] = r * u - xr * (r * jnp.mean(u * xr, axis=-1, keepdims=True)) + dres_ref[rows, :]
            elif rms_g is not None:
                g_ref, o_ref, n_ref = rest
                o_ref[rows, :] = acc
                r = lax.rsqrt(jnp.mean(acc * acc, axis=-1, keepdims=True) + RMS_EPS)
                n_ref[rows, :] = ((acc * r) * g_ref[...]).astype(BF16)
            else:
                rest[0][rows, :] = acc.astype(rest[0].dtype)

    in_specs, ins = [], []
    for a, w, mode in pairs:
        K = a.shape[1]
        in_specs.append(pl.BlockSpec((tm, K), lambda i, j: (i, 0)))
        once = _RESIDENT if tn == N else None
        in_specs.append(pl.BlockSpec((K, tn), lambda i, j: (0, j), pipeline_mode=once) if mode == "nn"
                        else pl.BlockSpec((tn, K), lambda i, j: (j, 0), pipeline_mode=once))
        ins += [a, w]
    in_specs += after_specs
    ins += after_ops
    tile = pl.BlockSpec((tm, tn), lambda i, j: (i, j))
    vec = pl.BlockSpec((1, tn), lambda i, j: (0, j))
    if res is not None:
        in_specs.append(tile)
        ins.append(res)
    sem = ("parallel", "parallel")
    if rms_bwd is not None:
        in_specs += [tile, vec, tile]
        ins += list(rms_bwd)
        out_specs = [tile, vec]
        out_shape = [jax.ShapeDtypeStruct((M, N), F32), jax.ShapeDtypeStruct((1, N), F32)]
        sem = ("arbitrary", "arbitrary")
    elif rms_g is not None:
        in_specs.append(vec)
        ins.append(rms_g)
        out_specs = [tile, tile]
        out_shape = [jax.ShapeDtypeStruct((M, N), F32), jax.ShapeDtypeStruct((M, N), BF16)]
    else:
        out_specs = tile
        out_shape = jax.ShapeDtypeStruct((M, N), out_dtype)
    return pl.pallas_call(
        body, grid=(M // tm, N // tn), in_specs=in_specs, out_specs=out_specs, out_shape=out_shape,
        compiler_params=_cp(*sem), name=name)(*ins)


def _mm_tn(a, b, name, ts=None, tn=None):
    S, K = a.shape
    N = b.shape[1]
    ts = _pick(S, ts or TS_MM, 16)
    tn = _pick(N, tn or TN_MM, 128)
    nsteps = S // ts

    def body(a_ref, b_ref, o_ref, acc_ref):
        s = pl.program_id(1)

        @pl.when(s == 0)
        def _():
            acc_ref[...] = jnp.zeros_like(acc_ref)

        acc_ref[...] += _dot_tn(a_ref[...].astype(BF16), b_ref[...].astype(BF16))

        @pl.when(s == nsteps - 1)
        def _():
            o_ref[...] = acc_ref[...].astype(o_ref.dtype)

    return pl.pallas_call(
        body, grid=(N // tn, nsteps),
        in_specs=[pl.BlockSpec((ts, K), lambda j, s: (s, 0)), pl.BlockSpec((ts, tn), lambda j, s: (s, j))],
        out_specs=pl.BlockSpec((K, tn), lambda j, s: (0, j)), out_shape=jax.ShapeDtypeStruct((K, N), BF16),
        scratch_shapes=[pltpu.VMEM((K, tn), F32)],
        compiler_params=_cp("parallel", "arbitrary"), name=name)(a, b)


def _col_chunk(n):
    return 256 if n % 256 == 0 else 128


def _ffn_up(n, wgt, wut, name, after=None):
    S, D = n.shape
    F = wgt.shape[0]
    tm = _pick(S, TS_FFN, 16)
    ce = _col_chunk(F)
    after_specs, after_ops = _after(after)

    def body(n_ref, wg_ref, wu_ref, *rest):
        a_ref, b_ref, hid_ref = rest[-3:]
        x = n_ref[...]
        for c0 in range(0, F, ce):
            a = _dot_nt(x, wg_ref[c0:c0 + ce, :])
            b = _dot_nt(x, wu_ref[c0:c0 + ce, :])
            a_ref[:, c0:c0 + ce] = a.astype(BF16)
            b_ref[:, c0:c0 + ce] = b.astype(BF16)
            hid_ref[:, c0:c0 + ce] = (a * _sigmoid(a) * b).astype(BF16)

    wspec = pl.BlockSpec((F, D), lambda i: (0, 0), pipeline_mode=_RESIDENT)
    ospec = pl.BlockSpec((tm, F), lambda i: (i, 0))
    osh = jax.ShapeDtypeStruct((S, F), BF16)
    return pl.pallas_call(
        body, grid=(S // tm,),
        in_specs=[pl.BlockSpec((tm, D), lambda i: (i, 0)), wspec, wspec] + after_specs,
        out_specs=[ospec, ospec, ospec], out_shape=[osh, osh, osh],
        compiler_params=_cp("parallel"), name=name)(n, wgt, wut, *after_ops)


def _ffn_dhid(dh, wd, a, b, name):
    S, D = dh.shape
    F = wd.shape[0]
    tm = _pick(S, TS_FFN, 16)
    ce = _col_chunk(F)

    def body(dh_ref, wd_ref, a_ref, b_ref, da_ref, db_ref):
        x = dh_ref[...].astype(BF16)
        for c0 in range(0, F, ce):
            g = _dot_nt(x, wd_ref[c0:c0 + ce, :]).astype(BF16)
            a_ = a_ref[:, c0:c0 + ce]
            sg = _sigmoid(a_)
            silu = a_ * sg
            da_ref[:, c0:c0 + ce] = (g * b_ref[:, c0:c0 + ce]) * (sg + silu * (1.0 - sg))
            db_ref[:, c0:c0 + ce] = g * silu

    tile = pl.BlockSpec((tm, F), lambda i: (i, 0))
    osh = jax.ShapeDtypeStruct((S, F), BF16)
    return pl.pallas_call(
        body, grid=(S // tm,),
        in_specs=[pl.BlockSpec((tm, D), lambda i: (i, 0)),
                  pl.BlockSpec((F, D), lambda i: (0, 0), pipeline_mode=_RESIDENT), tile, tile],
        out_specs=[tile, tile], out_shape=[osh, osh],
        compiler_params=_cp("parallel"), name=name)(dh, wd, a, b)


def _softmax_rows(s):
    m = jnp.max(s, axis=-1, keepdims=True)
    p = jnp.exp(s - m)
    return p / jnp.sum(p, axis=-1, keepdims=True)


def _attn_fwd(q, k, v, name):
    S, D = q.shape
    M = k.shape[0]
    hd = D // XA_HEADS
    scale = hd ** -0.5
    ts = _pick(S, TS_ATTN, 16)

    def body(q_ref, k_ref, v_ref, o_ref):
        for h in range(XA_HEADS):
            sl = slice(h * hd, (h + 1) * hd)
            p = _softmax_rows(_dot_nt(q_ref[:, sl], k_ref[:, sl]) * scale)
            o_ref[:, sl] = _dot(p.astype(BF16), v_ref[:, sl]).astype(BF16)

    tile = pl.BlockSpec((ts, D), lambda i: (i, 0))
    memspec = pl.BlockSpec((M, D), lambda i: (0, 0))
    return pl.pallas_call(
        body, grid=(S // ts,), in_specs=[tile, memspec, memspec], out_specs=tile,
        out_shape=jax.ShapeDtypeStruct((S, D), BF16), compiler_params=_cp("parallel"), name=name)(q, k, v)


def _attn_bwd(q, k, v, do, name):
    S, D = q.shape
    M = k.shape[0]
    hd = D // XA_HEADS
    scale = hd ** -0.5
    ts = _pick(S, TS_ATTN, 16)

    def body(q_ref, k_ref, v_ref, do_ref, dq_ref, dk_ref, dv_ref):
        @pl.when(pl.program_id(0) == 0)
        def _():
            dk_ref[...] = jnp.zeros_like(dk_ref)
            dv_ref[...] = jnp.zeros_like(dv_ref)

        for h in range(XA_HEADS):
            sl = slice(h * hd, (h + 1) * hd)
            qh, kh, vh, doh = q_ref[:, sl], k_ref[:, sl], v_ref[:, sl], do_ref[:, sl]
            p = _softmax_rows(_dot_nt(qh, kh) * scale)
            dp = _dot_nt(doh, vh)
            dv_ref[:, sl] += _dot_tn(p.astype(BF16), doh)
            delta = jnp.sum(dp * p, axis=-1, keepdims=True)
            ds = (p * (dp - delta) * scale).astype(BF16)
            dq_ref[:, sl] = _dot(ds, kh).astype(BF16)
            dk_ref[:, sl] += _dot_tn(ds, qh)

    tile = pl.BlockSpec((ts, D), lambda i: (i, 0))
    memspec = pl.BlockSpec((M, D), lambda i: (0, 0))
    return pl.pallas_call(
        body, grid=(S // ts,), in_specs=[tile, memspec, memspec, tile], out_specs=[tile, memspec, memspec],
        out_shape=[jax.ShapeDtypeStruct((S, D), BF16), jax.ShapeDtypeStruct((M, D), F32),
                   jax.ShapeDtypeStruct((M, D), F32)],
        compiler_params=_cp("arbitrary"), name=name)(q, k, v, do)


def _halo_specs(ts, width, col):
    per = ts // HALO

    def prev(i):
        return (jnp.maximum(i * per - 1, 0), col)

    def nxt(i, n_tiles):
        return (jnp.minimum((i + 1) * per, n_tiles * per - 1), col)

    return prev, nxt


def _fill_ext(ext_ref, prev_val, main_val, next_val, first, last, ts):
    ext_ref[pl.ds(0, HALO), :] = jnp.where(first, 0.0, prev_val)
    ext_ref[pl.ds(HALO, ts), :] = main_val
    ext_ref[pl.ds(HALO + ts, HALO), :] = jnp.where(last, 0.0, next_val)


SUBLANES = 8


def _fill_shifted(sh_ref, ts):
    n = ts + 2 * HALO - SUBLANES
    for s in range(1, SUBLANES):
        sh_ref[s, pl.ds(0, n), :] = sh_ref[0, pl.ds(s, n), :]


def _tap(sh_ref, r0, offset, rc):
    q, s = divmod(offset, SUBLANES)
    return sh_ref[s, pl.ds(pl.multiple_of(r0 + SUBLANES * q, SUBLANES), rc), :]


def _conv_fwd(z, wa, ba, lng, lnb, wb, bb, name, after=None):
    S = z.shape[0]
    C = z.shape[1] // 5
    KA, KB = wa.shape[0], wb.shape[0]
    pa, pb = KA // 2, KB // 2
    assert pa <= HALO and pb <= HALO
    ts = _pick(S, TS_CONV, ROW_CHUNK)
    nt = S // ts
    rc = ROW_CHUNK
    prev, nxt = _halo_specs(ts, 5 * C, 0)
    after_specs, after_ops = _after(after)

    def body(*refs):
        compute(*refs[:9], *refs[9 + len(after_ops):])

    def compute(z_ref, zp_ref, zn_ref, wa_ref, ba_ref, lng_ref, lnb_ref, wb_ref, bb_ref, ab_ref, ca_ref,
                ga_sh, tb_ext, win_b):
        i = pl.program_id(0)
        first, last = i == 0, i == nt - 1

        def glu(r):
            return r[:, 0:C] * _sigmoid(r[:, C:2 * C])

        def gcb(r):
            return r[:, 4 * C:5 * C] * r[:, 2 * C:3 * C]

        _fill_ext(ga_sh.at[0], glu(zp_ref), glu(z_ref), glu(zn_ref), first, last, ts)
        _fill_shifted(ga_sh, ts)
        _fill_ext(tb_ext, gcb(zp_ref), gcb(z_ref), gcb(zn_ref), first, last, ts)

        def chunk(c, carry):
            r0 = pl.multiple_of(c * rc, rc)
            win_b[...] = tb_ext[pl.ds(r0, rc + 2 * HALO), :]
            acc = jnp.zeros((rc, C), F32)
            for k in range(KA):
                acc = acc + wa_ref[k:k + 1, :] * _tap(ga_sh, r0, HALO - pa + k, rc)
            ca = acc + ba_ref[...]
            ca_ref[pl.ds(r0, rc), :] = ca
            mu = jnp.mean(ca, axis=-1, keepdims=True)
            xc = ca - mu
            var = jnp.mean(xc * xc, axis=-1, keepdims=True)
            ln = xc * lax.rsqrt(var + LN_EPS) * lng_ref[...] + lnb_ref[...]
            ab_ref[pl.ds(r0, rc), 0:C] = (ln * _sigmoid(ln)).astype(BF16)
            cb = jnp.zeros((rc, C), F32) + bb_ref[...]
            for k in range(KB):
                cb = cb + wb_ref[k:k + 1, :] * win_b[pl.ds(HALO - pb + k, rc), :]
            ab_ref[pl.ds(r0, rc), C:2 * C] = (z_ref[pl.ds(r0, rc), 3 * C:4 * C] * cb).astype(BF16)
            return carry

        lax.fori_loop(0, ts // rc, chunk, 0)

    zspec = pl.BlockSpec((ts, 5 * C), lambda i: (i, 0))
    zprev = pl.BlockSpec((HALO, 5 * C), prev)
    znext = pl.BlockSpec((HALO, 5 * C), lambda i: nxt(i, nt))

    def full(a):
        return pl.BlockSpec(a.shape, lambda i: (0, 0))

    return pl.pallas_call(
        body, grid=(nt,),
        in_specs=[zspec, zprev, znext, full(wa), full(ba), full(lng), full(lnb), full(wb), full(bb)] + after_specs,
        out_specs=[pl.BlockSpec((ts, 2 * C), lambda i: (i, 0)), pl.BlockSpec((ts, C), lambda i: (i, 0))],
        out_shape=[jax.ShapeDtypeStruct((S, 2 * C), BF16), jax.ShapeDtypeStruct((S, C), F32)],
        scratch_shapes=[pltpu.VMEM((SUBLANES, ts + 2 * HALO, C), F32), pltpu.VMEM((ts + 2 * HALO, C), F32),
                        pltpu.VMEM((rc + 2 * HALO, C), F32)],
        compiler_params=_cp("parallel"), name=name)(z, z, z, wa, ba, lng, lnb, wb, bb, *after_ops)


def _conv_bwd_ln(dab, ca, lng, lnb, name):
    S, C = ca.shape
    ts = _pick(S, TS_ROW, 8)

    def body(da_ref, ca_ref, lng_ref, lnb_ref, dca_ref, dg_ref, db_ref, dbias_ref):
        @pl.when(pl.program_id(0) == 0)
        def _():
            dg_ref[...] = jnp.zeros_like(dg_ref)
            db_ref[...] = jnp.zeros_like(db_ref)
            dbias_ref[...] = jnp.zeros_like(dbias_ref)

        ca_ = ca_ref[...]
        mu = jnp.mean(ca_, axis=-1, keepdims=True)
        xc = ca_ - mu
        rstd = lax.rsqrt(jnp.mean(xc * xc, axis=-1, keepdims=True) + LN_EPS)
        xh = xc * rstd
        ln = xh * lng_ref[...] + lnb_ref[...]
        sg = _sigmoid(ln)
        dln = da_ref[...].astype(F32) * (sg * (1.0 + ln * (1.0 - sg)))
        dg_ref[...] += jnp.sum(dln * xh, axis=0, keepdims=True)
        db_ref[...] += jnp.sum(dln, axis=0, keepdims=True)
        dxh = dln * lng_ref[...]
        dca = rstd * (dxh - jnp.mean(dxh, axis=-1, keepdims=True) - xh * jnp.mean(dxh * xh, axis=-1, keepdims=True))
        dca_ref[...] = dca
        dbias_ref[...] += jnp.sum(dca, axis=0, keepdims=True)

    tile = pl.BlockSpec((ts, C), lambda i: (i, 0))
    vec = pl.BlockSpec((1, C), lambda i: (0, 0))
    vsh = jax.ShapeDtypeStruct((1, C), F32)
    return pl.pallas_call(
        body, grid=(S // ts,), in_specs=[tile, tile, vec, vec], out_specs=[tile, vec, vec, vec],
        out_shape=[jax.ShapeDtypeStruct((S, C), F32), vsh, vsh, vsh],
        compiler_params=_cp("arbitrary"), name=name)(dab, ca, lng, lnb)


def _conv_bwd(z, dca, dab, wa, wb, bb, name):
    S = z.shape[0]
    C = z.shape[1] // 5
    KA, KB = wa.shape[0], wb.shape[0]
    pa, pb = KA // 2, KB // 2
    ts = _pick(S, TS_CONV, ROW_CHUNK)
    nt = S // ts
    rc = ROW_CHUNK
    prev0, nxt0 = _halo_specs(ts, C, 0)
    prev1, nxt1 = _halo_specs(ts, C, 1)

    def body(z_ref, zp_ref, zn_ref, dca_ref, dcap_ref, dcan_ref, db_ref, dbp_ref, dbn_ref, wa_ref, wb_ref, bb_ref,
             dz_ref, dwa_ref, dwb_ref, dbb_ref,
             ga_sh, dca_sh, tb_ext, dcb_ext, win_tb, win_dcb, acc_a, acc_b, acc_bias):
        i = pl.program_id(0)
        first, last = i == 0, i == nt - 1

        @pl.when(first)
        def _():
            acc_a[...] = jnp.zeros_like(acc_a)
            acc_b[...] = jnp.zeros_like(acc_b)
            acc_bias[...] = jnp.zeros_like(acc_bias)

        def glu(r):
            return r[:, 0:C] * _sigmoid(r[:, C:2 * C])

        def gcb(r):
            return r[:, 4 * C:5 * C] * r[:, 2 * C:3 * C]

        def dcb(d, r):
            return d[...].astype(F32) * r[:, 3 * C:4 * C]

        _fill_ext(ga_sh.at[0], glu(zp_ref), glu(z_ref), glu(zn_ref), first, last, ts)
        _fill_shifted(ga_sh, ts)
        _fill_ext(dca_sh.at[0], dcap_ref[...], dca_ref[...], dcan_ref[...], first, last, ts)
        _fill_shifted(dca_sh, ts)
        _fill_ext(tb_ext, gcb(zp_ref), gcb(z_ref), gcb(zn_ref), first, last, ts)
        _fill_ext(dcb_ext, dcb(dbp_ref, zp_ref), dcb(db_ref, z_ref), dcb(dbn_ref, zn_ref), first, last, ts)

        def fold(x):
            return jnp.sum(x.reshape(rc // 8, 8, C), axis=0)

        def chunk(c, carry):
            r0 = pl.multiple_of(c * rc, rc)
            win_tb[...] = tb_ext[pl.ds(r0, rc + 2 * HALO), :]
            win_dcb[...] = dcb_ext[pl.ds(r0, rc + 2 * HALO), :]
            dca_c = _tap(dca_sh, r0, HALO, rc)
            dglu = jnp.zeros((rc, C), F32)
            for k in range(KA):
                dglu = dglu + wa_ref[k:k + 1, :] * _tap(dca_sh, r0, HALO + pa - k, rc)
                acc_a[k] += fold(dca_c * _tap(ga_sh, r0, HALO - pa + k, rc))
            val = z_ref[pl.ds(r0, rc), 0:C]
            sg = _sigmoid(z_ref[pl.ds(r0, rc), C:2 * C])
            dz_ref[pl.ds(r0, rc), 0:C] = (dglu * sg).astype(BF16)
            dz_ref[pl.ds(r0, rc), C:2 * C] = (dglu * val * sg * (1.0 - sg)).astype(BF16)
            dcb_c = win_dcb[pl.ds(HALO, rc), :]
            cb = jnp.zeros((rc, C), F32) + bb_ref[...]
            dt = jnp.zeros((rc, C), F32)
            for k in range(KB):
                tb_k = win_tb[pl.ds(HALO - pb + k, rc), :]
                cb = cb + wb_ref[k:k + 1, :] * tb_k
                dt = dt + wb_ref[k:k + 1, :] * win_dcb[pl.ds(HALO + pb - k, rc), :]
                acc_b[k] += fold(dcb_c * tb_k)
            acc_bias[...] += fold(dcb_c)
            db_c = db_ref[pl.ds(r0, rc), :].astype(F32)
            dz_ref[pl.ds(r0, rc), 2 * C:3 * C] = (dt * z_ref[pl.ds(r0, rc), 4 * C:5 * C]).astype(BF16)
            dz_ref[pl.ds(r0, rc), 3 * C:4 * C] = (db_c * cb).astype(BF16)
            dz_ref[pl.ds(r0, rc), 4 * C:5 * C] = (dt * z_ref[pl.ds(r0, rc), 2 * C:3 * C]).astype(BF16)
            return carry

        lax.fori_loop(0, ts // rc, chunk, 0)

        @pl.when(last)
        def _():
            dwa_ref[...] = jnp.sum(acc_a[...], axis=1)
            dwb_ref[...] = jnp.sum(acc_b[...], axis=1)
            dbb_ref[...] = jnp.sum(acc_bias[...], axis=0, keepdims=True)

    zspec = pl.BlockSpec((ts, 5 * C), lambda i: (i, 0))
    zprev = pl.BlockSpec((HALO, 5 * C), prev0)
    znext = pl.BlockSpec((HALO, 5 * C), lambda i: nxt0(i, nt))
    dspec = pl.BlockSpec((ts, C), lambda i: (i, 0))
    dprev = pl.BlockSpec((HALO, C), prev0)
    dnext = pl.BlockSpec((HALO, C), lambda i: nxt0(i, nt))
    bspec = pl.BlockSpec((ts, C), lambda i: (i, 1))
    bprev = pl.BlockSpec((HALO, C), prev1)
    bnext = pl.BlockSpec((HALO, C), lambda i: nxt1(i, nt))

    def full(shape):
        return pl.BlockSpec(shape, lambda i: (0,) * len(shape))

    ext = pltpu.VMEM((ts + 2 * HALO, C), F32)
    shifted = pltpu.VMEM((SUBLANES, ts + 2 * HALO, C), F32)
    win = pltpu.VMEM((rc + 2 * HALO, C), F32)
    return pl.pallas_call(
        body, grid=(nt,),
        in_specs=[zspec, zprev, znext, dspec, dprev, dnext, bspec, bprev, bnext,
                  full(wa.shape), full(wb.shape), full(bb.shape)],
        out_specs=[pl.BlockSpec((ts, 5 * C), lambda i: (i, 0)), full((KA, C)), full((KB, C)), full((1, C))],
        out_shape=[jax.ShapeDtypeStruct((S, 5 * C), BF16), jax.ShapeDtypeStruct((KA, C), F32),
                   jax.ShapeDtypeStruct((KB, C), F32), jax.ShapeDtypeStruct((1, C), F32)],
        scratch_shapes=[shifted, shifted, ext, ext, win, win,
                        pltpu.VMEM((KA, 8, C), F32), pltpu.VMEM((KB, 8, C), F32), pltpu.VMEM((8, C), F32)],
        compiler_params=_cp("arbitrary"), name=name)(z, z, z, dca, dca, dca, dab, dab, dab, wa, wb, bb)


_GELU_C = 0.7978845608028654
_GELU_A = 0.044715


def _gelu(x):
    return 0.5 * x * (1.0 + jnp.tanh(_GELU_C * (x + _GELU_A * (x * x * x))))


def _gelu_and_grad(x):
    t = jnp.tanh(_GELU_C * (x + _GELU_A * (x * x * x)))
    hx = 0.5 * x
    return hx * (1.0 + t), 0.5 * (1.0 + t) + hx * (1.0 - t * t) * (_GELU_C * (1.0 + 3.0 * _GELU_A * x * x))


def _sgu_fwd(zp, lng, lnb, ws, bsb, name):
    S = zp.shape[0]
    D = zp.shape[1] // 2
    G = ws.shape[0]
    gd = D // G
    ts = _pick(S, TS_SGU, CHUNK)
    ncs = ts // CHUNK

    def body(zp_ref, lng_ref, lnb_ref, ws_ref, bsb_ref, y_ref, vb_ref):
        v = _gelu(zp_ref[:, D:2 * D])
        mu = jnp.mean(v, axis=-1, keepdims=True)
        xc = v - mu
        rstd = lax.rsqrt(jnp.mean(xc * xc, axis=-1, keepdims=True) + LN_EPS)
        vb_ref[...] = (xc * rstd * lng_ref[...] + lnb_ref[...]).astype(BF16)
        for c in range(ncs):
            rows = slice(c * CHUNK, (c + 1) * CHUNK)
            for g in range(G):
                cols = slice(g * gd, (g + 1) * gd)
                sv = _dot(ws_ref[g], vb_ref[rows, cols]) + bsb_ref[:, cols]
                y_ref[rows, cols] = (_gelu(zp_ref[rows, cols]) * sv).astype(BF16)

    def full(a):
        return pl.BlockSpec(a.shape, lambda i: (0,) * a.ndim)

    return pl.pallas_call(
        body, grid=(S // ts,),
        in_specs=[pl.BlockSpec((ts, 2 * D), lambda i: (i, 0)), full(lng), full(lnb), full(ws), full(bsb)],
        out_specs=pl.BlockSpec((ts, D), lambda i: (i, 0)), out_shape=jax.ShapeDtypeStruct((S, D), BF16),
        scratch_shapes=[pltpu.VMEM((ts, D), BF16)],
        compiler_params=_cp("parallel"), name=name)(zp, lng, lnb, ws, bsb)


def _sgu_bwd(dy, zp, lng, lnb, ws, wst, bsb, name):
    S = zp.shape[0]
    D = zp.shape[1] // 2
    G = ws.shape[0]
    gd = D // G
    ts = _pick(S, TS_SGU, CHUNK)
    ncs = ts // CHUNK

    def body(dy_ref, zp_ref, lng_ref, lnb_ref, ws_ref, wst_ref, bsb_ref,
             dzp_ref, dws_ref, dbs_ref, dg_ref, db_ref, vb_ref, dvln_ref, acc_bs):
        i = pl.program_id(0)

        @pl.when(i == 0)
        def _():
            dws_ref[...] = jnp.zeros_like(dws_ref)
            acc_bs[...] = jnp.zeros_like(acc_bs)
            dg_ref[...] = jnp.zeros_like(dg_ref)
            db_ref[...] = jnp.zeros_like(db_ref)

        v, dv_dz = _gelu_and_grad(zp_ref[:, D:2 * D])
        mu = jnp.mean(v, axis=-1, keepdims=True)
        xc = v - mu
        rstd = lax.rsqrt(jnp.mean(xc * xc, axis=-1, keepdims=True) + LN_EPS)
        xh = xc * rstd
        vb_ref[...] = (xh * lng_ref[...] + lnb_ref[...]).astype(BF16)
        for c in range(ncs):
            rows = slice(c * CHUNK, (c + 1) * CHUNK)
            for g in range(G):
                cols = slice(g * gd, (g + 1) * gd)
                u, du_dz = _gelu_and_grad(zp_ref[rows, cols])
                dy_ = dy_ref[rows, cols].astype(F32)
                sv = _dot(ws_ref[g], vb_ref[rows, cols]) + bsb_ref[:, cols]
                dzp_ref[rows, cols] = (dy_ * sv * du_dz).astype(BF16)
                dsv = dy_ * u
                acc_bs[:, cols] += dsv
                dsvb = dsv.astype(BF16)
                dws_ref[g] += _dot_nt(dsvb, vb_ref[rows, cols])
                dvln_ref[rows, cols] = _dot(wst_ref[g], dsvb)
        dvln = dvln_ref[...]
        dg_ref[...] += jnp.sum(dvln * xh, axis=0, keepdims=True)
        db_ref[...] += jnp.sum(dvln, axis=0, keepdims=True)
        dxh = dvln * lng_ref[...]
        dv = rstd * (dxh - jnp.mean(dxh, axis=-1, keepdims=True) - xh * jnp.mean(dxh * xh, axis=-1, keepdims=True))
        dzp_ref[:, D:2 * D] = (dv * dv_dz).astype(BF16)

        @pl.when(i == pl.num_programs(0) - 1)
        def _():
            dbs_ref[...] = acc_bs[...]

    def full(shape):
        return pl.BlockSpec(shape, lambda i: (0,) * len(shape))

    return pl.pallas_call(
        body, grid=(S // ts,),
        in_specs=[pl.BlockSpec((ts, D), lambda i: (i, 0)), pl.BlockSpec((ts, 2 * D), lambda i: (i, 0)),
                  full(lng.shape), full(lnb.shape), full(ws.shape), full(wst.shape), full(bsb.shape)],
        out_specs=[pl.BlockSpec((ts, 2 * D), lambda i: (i, 0)), full(ws.shape), full(bsb.shape),
                   full((1, D)), full((1, D))],
        out_shape=[jax.ShapeDtypeStruct((S, 2 * D), BF16), jax.ShapeDtypeStruct(ws.shape, F32),
                   jax.ShapeDtypeStruct(bsb.shape, F32), jax.ShapeDtypeStruct((1, D), F32),
                   jax.ShapeDtypeStruct((1, D), F32)],
        scratch_shapes=[pltpu.VMEM((ts, D), BF16), pltpu.VMEM((ts, D), F32),
                        pltpu.VMEM(bsb.shape, F32)],
        compiler_params=_cp("arbitrary"), name=name)(dy, zp, lng, lnb, ws, wst, bsb)


def _group_sum(x, groups, name):
    P, D = x.shape
    gd = D // groups

    def body(x_ref, o_ref):
        for g in range(groups):
            o_ref[:, g:g + 1] = jnp.sum(x_ref[:, g * gd:(g + 1) * gd], axis=1, keepdims=True)

    return pl.pallas_call(body, out_shape=jax.ShapeDtypeStruct((P, groups), F32), name=name)(x)


def _adamw(w, g, m, v, name):
    shape = w.shape
    C = shape[-1]
    R = w.size // C
    tr = _pick(R, 1024, 8)

    def body(w_ref, g_ref, m_ref, v_ref, d_ref, nm_ref, nv_ref):
        d_ref[...], nm_ref[...], nv_ref[...] = _adamw_math(w_ref[...], g_ref[...], m_ref[...], v_ref[...])

    tile = pl.BlockSpec((tr, C), lambda i: (i, 0))
    sh = jax.ShapeDtypeStruct((R, C), F32)
    outs = pl.pallas_call(
        body, grid=(R // tr,), in_specs=[tile] * 4, out_specs=[tile] * 3, out_shape=[sh] * 3,
        compiler_params=_cp("parallel"), name=name)(*(a.reshape(R, C) for a in (w, g, m, v)))
    return tuple(o.reshape(shape) for o in outs)


_HBM = pl.BlockSpec(memory_space=pltpu.HBM)


def _remote(src, dst, send_sem, recv_sem, to):
    return pltpu.make_async_remote_copy(src_ref=src, dst_ref=dst, send_sem=send_sem, recv_sem=recv_sem,
                                        device_id=to, device_id_type=MESH)


def _all_gather(block, name):
    R, C = block.shape

    def body(x_ref, out_ref, send_sems, recv_sems, local_sem):
        x, y, c = lax.axis_index("x"), lax.axis_index("y"), lax.axis_index("c")
        me, sibling = (x, y, c), (x, y, 1 - c)
        chips = [(1 - x, y), (x, 1 - y), (1 - x, 1 - y)]

        def slot(px, py, pc):
            return out_ref.at[4 * px + 2 * py + pc]

        def copy(k, blk, to, src=None):
            return _remote(slot(*blk) if src is None else src, slot(*blk), send_sems.at[k], recv_sems.at[k], to)

        mine = pltpu.make_async_copy(x_ref, slot(*me), local_sem)
        mine.start()
        first = [copy(0, me, sibling, src=x_ref)]
        first += [copy(1 + j, me, (*chip, c), src=x_ref) for j, chip in enumerate(chips)]
        for cp in first:
            cp.start()
        passed = [copy(4 + j, (*chip, c), sibling) for j, chip in enumerate(chips)]
        for j, chip in enumerate(chips):
            copy(1 + j, (*chip, c), me).wait_recv()
            passed[j].start()
        copy(0, sibling, me).wait_recv()
        for j, chip in enumerate(chips):
            copy(4 + j, (*chip, 1 - c), me).wait_recv()
        for cp in first + passed:
            cp.wait_send()
        mine.wait()

    return pl.pallas_call(
        body, out_shape=jax.ShapeDtypeStruct((NDEV, R, C), block.dtype), in_specs=[_HBM], out_specs=_HBM,
        scratch_shapes=[pltpu.SemaphoreType.DMA((7,)), pltpu.SemaphoreType.DMA((7,)), pltpu.SemaphoreType.DMA],
        name=name)(block)


def _all_gather_weights(pack, rows, name, after=None):
    C = pack.shape[1]
    nw = len(rows)
    starts = [sum(rows[:w]) for w in range(nw)]
    after_specs, after_ops = _after(after)

    def body(pack_ref, *rest):
        rest = rest[len(after_ops):]
        outs = rest[:nw]
        send_sems, recv_sems, local_sem = rest[nw:]
        x, y, c = lax.axis_index("x"), lax.axis_index("y"), lax.axis_index("c")
        me, sibling = (x, y, c), (x, y, 1 - c)
        chips = [(1 - x, y), (x, 1 - y), (1 - x, 1 - y)]

        def block(w, px, py, pc):
            return outs[w].at[pl.ds((4 * px + 2 * py + pc) * rows[w], rows[w])]

        def mine(w):
            return pack_ref.at[pl.ds(starts[w], rows[w])]

        def all_of(k):
            return _remote(pack_ref, pack_ref, send_sems.at[k], recv_sems.at[k], me)

        for w in range(nw):
            pltpu.make_async_copy(mine(w), block(w, *me), local_sem).start()
        for k, to in enumerate([sibling] + [(*chip, c) for chip in chips]):
            for w in range(nw):
                _remote(mine(w), block(w, *me), send_sems.at[k], recv_sems.at[k], to).start()
        for j, chip in enumerate(chips):
            all_of(1 + j).wait_recv()
            for w in range(nw):
                _remote(block(w, *chip, c), block(w, *chip, c), send_sems.at[4 + j], recv_sems.at[4 + j], sibling).start()
        all_of(0).wait_recv()
        for j in range(3):
            all_of(4 + j).wait_recv()
        for k in range(7):
            all_of(k).wait_send()
        pltpu.make_async_copy(pack_ref, pack_ref, local_sem).wait()

    return pl.pallas_call(
        body, out_shape=[jax.ShapeDtypeStruct((NDEV * r, C), pack.dtype) for r in rows],
        in_specs=[_HBM] + after_specs, out_specs=[_HBM] * nw,
        scratch_shapes=[pltpu.SemaphoreType.DMA((7,)), pltpu.SemaphoreType.DMA((7,)), pltpu.SemaphoreType.DMA],
        name=name)(pack, *after_ops)


_SEM = pl.BlockSpec(memory_space=pltpu.SEMAPHORE)
_DATAFLOW = pltpu.SideEffectType.DATAFLOW_SIDE_EFFECTING


def _split_start(srcs, lands, plan, n, after, name):
    nbuf = len(srcs) + len(lands)
    after_specs, after_ops = _after(after)

    def body(*refs):
        src_refs, land_refs = refs[:len(srcs)], refs[len(srcs):nbuf]
        send_sems, recv_sems = refs[nbuf + len(after_ops)], refs[nbuf + len(after_ops) + 1]
        for k, (src, dst, to) in enumerate(plan(src_refs, land_refs)):
            _remote(src, dst, send_sems.at[k], recv_sems.at[k], to).start()
        refs[-1][...] = jnp.zeros_like(refs[-1])

    bufs = [pltpu.with_memory_space_constraint(a, pltpu.HBM) for a in list(srcs) + list(lands)]
    outs = pl.pallas_call(
        body, name=name,
        out_shape=(pltpu.SemaphoreType.DMA((n,)), pltpu.SemaphoreType.DMA((n,)),
                   *[pltpu.HBM(a.shape, a.dtype) for a in bufs], jax.ShapeDtypeStruct((8, 128), F32)),
        in_specs=[_HBM] * nbuf + after_specs,
        out_specs=(_SEM, _SEM, *[_HBM] * nbuf, pl.BlockSpec(memory_space=pltpu.VMEM)),
        input_output_aliases={i: 2 + i for i in range(nbuf)},
        compiler_params=pltpu.CompilerParams(has_side_effects=_DATAFLOW))(*bufs, *after_ops)
    return outs[0], outs[1], list(outs[2:2 + len(srcs)]), list(outs[2 + len(srcs):2 + nbuf]), outs[-1]


def _split_wait(send_sems, recv_sems, srcs, lands, plan, after, name):
    nbuf = len(srcs) + len(lands)
    after_specs, after_ops = _after(after)

    def body(*refs):
        src_refs, land_refs = refs[:len(srcs)], refs[len(srcs):nbuf]
        send_sems_ref, recv_sems_ref = refs[nbuf], refs[nbuf + 1]
        for k, (src, dst, to) in enumerate(plan(src_refs, land_refs)):
            copy = _remote(src, dst, send_sems_ref.at[k], recv_sems_ref.at[k], to)
            copy.wait_send()
            copy.wait_recv()

    outs = pl.pallas_call(
        body, name=name, out_shape=tuple(pltpu.HBM(a.shape, a.dtype) for a in list(srcs) + list(lands)),
        in_specs=[_HBM] * nbuf + [_SEM, _SEM] + after_specs, out_specs=tuple([_HBM] * nbuf),
        input_output_aliases={i: i for i in range(nbuf)},
        compiler_params=pltpu.CompilerParams(has_side_effects=_DATAFLOW))(*srcs, *lands, send_sems, recv_sems, *after_ops)
    return list(outs[:len(srcs)]), list(outs[len(srcs):])


def _peers(x, y, c):
    return [(mask, (1 - x if mask & 4 else x, 1 - y if mask & 2 else y, 1 - c if mask & 1 else c))
            for mask in range(1, NDEV)]


def _gather_plan(rows):
    starts = [sum(rows[:w]) for w in range(len(rows))]

    def plan(src_refs, land_refs):
        x, y, c = lax.axis_index("x"), lax.axis_index("y"), lax.axis_index("c")
        copies = []
        for w, r in enumerate(rows):
            mine = src_refs[0].at[pl.ds(starts[w], r)]
            dst = land_refs[w].at[pl.ds((4 * x + 2 * y + c) * r, r)]
            copies += [(mine, dst, peer) for _, peer in _peers(x, y, c)]
        return copies

    return plan, (NDEV - 1) * len(rows)


def _place_own(shards, fulls, dev_idx, name):
    nw = len(shards)

    def body(i_ref, *refs):
        for w in range(nw):
            refs[2 * nw + w][...] = refs[w][...]

    grid_spec = pltpu.PrefetchScalarGridSpec(
        num_scalar_prefetch=1, grid=(1,),
        in_specs=[pl.BlockSpec(s.shape, lambda t, i_ref: (0, 0)) for s in shards] + [_ANY] * nw,
        out_specs=[pl.BlockSpec(s.shape, lambda t, i_ref: (i_ref[0], 0)) for s in shards])
    outs = pl.pallas_call(
        body, grid_spec=grid_spec, out_shape=[jax.ShapeDtypeStruct(f.shape, f.dtype) for f in fulls],
        input_output_aliases={1 + nw + w: w for w in range(nw)}, name=name)(dev_idx, *shards, *fulls)
    return list(outs)


def _scatter_plan(rows):
    def plan(src_refs, land_refs):
        x, y, c = lax.axis_index("x"), lax.axis_index("y"), lax.axis_index("c")
        copies = []
        for w, r in enumerate(rows):
            for mask, (px, py, pc) in _peers(x, y, c):
                src = src_refs[w].at[pl.ds((4 * px + 2 * py + pc) * r, r)]
                copies.append((src, land_refs[w].at[mask - 1], (px, py, pc)))
        return copies

    return plan, (NDEV - 1) * len(rows)


def _adamw_math(w, g, m, v):
    nm = ADAM_B1 * m + (1.0 - ADAM_B1) * g
    nv = ADAM_B2 * v + (1.0 - ADAM_B2) * (g * g)
    bc1 = 1.0 - ADAM_B1 ** ADAM_STEP
    bc2 = 1.0 - ADAM_B2 ** ADAM_STEP
    return -ADAM_LR * ((nm / bc1) / (jnp.sqrt(nv / bc2) + ADAM_EPS) + ADAM_WD * w), nm, nv


def _finish_weight(gs, gots, dev_idx, w, m, v, name, after=None):
    L = len(gs)
    n1, r, C = gots[0].shape
    block = (None,) + w.shape[1:]

    after_specs, after_ops = _after(after)

    def body(i_ref, *refs):
        ins, (w_ref, m_ref, v_ref), (g_out, d_out, m_out, v_out) = refs[:2 * L], refs[2 * L:2 * L + 3], refs[-4:]
        for layer in range(L):
            @pl.when(pl.program_id(0) == layer)
            def _():
                g_ref, got_ref = ins[2 * layer], ins[2 * layer + 1]
                acc = g_ref[...].astype(F32)
                for k in range(n1):
                    acc = acc + got_ref[k].astype(F32)
                g_out[...] = acc
                d_out[...], m_out[...], v_out[...] = _adamw_math(w_ref[...], acc, m_ref[...], v_ref[...])

    in_specs, ins = [], []
    for g, got in zip(gs, gots):
        in_specs += [pl.BlockSpec((r, C), lambda t, i_ref: (i_ref[0], 0), pipeline_mode=_RESIDENT),
                     pl.BlockSpec((n1, r, C), lambda t, i_ref: (0, 0, 0), pipeline_mode=_RESIDENT)]
        ins += [g, got]
    per_layer = pl.BlockSpec(block, lambda t, i_ref: (t, 0, 0))
    grid_spec = pltpu.PrefetchScalarGridSpec(
        num_scalar_prefetch=1, grid=(L,), in_specs=in_specs + [per_layer] * 3 + after_specs,
        out_specs=[per_layer] * 4)
    return pl.pallas_call(
        body, grid_spec=grid_spec, out_shape=[jax.ShapeDtypeStruct(w.shape, F32)] * 4,
        compiler_params=_cp("arbitrary"), name=name)(dev_idx, *ins, w, m, v, *after_ops)


def _sum_slots(a, name):
    n, R, C = a.shape

    def body(a_ref, o_ref):
        acc = a_ref[0]
        for k in range(1, n):
            acc = acc + a_ref[k]
        o_ref[...] = acc

    return pl.pallas_call(body, out_shape=jax.ShapeDtypeStruct((R, C), F32), name=name)(a)


def _shard_axis(name):
    return {"ev_w_in": 2, "ev_a_conv_w": 2, "ev_b_conv_w": 2, "ev_w_out": 1, "od_w_in": 2, "od_c_ln_g": 1,
            "od_c_ln_b": 1, "od_w_out": 1, "xa_w_q": 1, "xa_w_k": 1, "xa_w_v": 1, "xa_w_o": 1,
            "ffn_w_gate": 2, "ffn_w_up": 2, "ffn_w_down": 1}[name]


BIG = ["ev_w_in", "ev_w_out", "od_w_in", "od_w_out", "xa_w_q", "xa_w_k", "xa_w_v", "xa_w_o",
       "ffn_w_gate", "ffn_w_up", "ffn_w_down"]
SMALL_SHARDED = ["ev_a_conv_w", "ev_b_conv_w", "od_c_ln_g", "od_c_ln_b"]
REPLICATED = ["g_mix", "g_xattn", "g_mem", "g_ffn", "g_final", "ev_a_conv_b", "ev_a_ln_g", "ev_a_ln_b",
              "ev_b_conv_b", "od_w_s", "od_b_s"]
WEIGHTS = ["g_mix", "g_xattn", "g_mem", "g_ffn", "g_final", "ev_w_in", "ev_a_conv_w", "ev_a_conv_b", "ev_a_ln_g",
           "ev_a_ln_b", "ev_b_conv_w", "ev_b_conv_b", "ev_w_out", "od_w_in", "od_c_ln_g", "od_c_ln_b", "od_w_s",
           "od_b_s", "od_w_out", "xa_w_q", "xa_w_k", "xa_w_v", "xa_w_o", "ffn_w_gate", "ffn_w_up", "ffn_w_down"]


def _full_from_blocks(blocks, axis):
    shard = blocks.shape[1:]
    full = jnp.moveaxis(blocks, 0, axis)
    return full.reshape(shard[:axis] + (NDEV * shard[axis],) + shard[axis + 1:])


def _blocks_from_full(full, axis):
    shp = full.shape
    split = full.reshape(shp[:axis] + (NDEV, shp[axis] // NDEV) + shp[axis + 1:])
    return jnp.moveaxis(split, axis, 0)


def _pad_rows(flat, width, row_align):
    per = width * row_align
    n = -(-flat.shape[0] // per) * per
    return jnp.pad(flat, (0, n - flat.shape[0])).reshape(n // width, width)


def _row(v):
    return v.reshape(1, -1)


def _xattn_fwd(h, nq, mem, g_m, wq, wk, wv, wo, g_next, tag, after):
    mem_n = _rms_fwd(mem, _row(g_m), f"xa_mem_rms_{tag}")
    q = _mm([(nq, wq, "nn")], f"xa_q_{tag}", out_dtype=BF16, after=after)
    k = _mm([(mem_n, wk, "nn")], f"xa_k_{tag}", out_dtype=BF16)
    v = _mm([(mem_n, wv, "nn")], f"xa_v_{tag}", out_dtype=BF16)
    o = _attn_fwd(q, k, v, f"xa_attn_{tag}")
    h_new, n_next = _mm([(o, wo, "nn")], f"xa_o_{tag}", res=h, rms_g=_row(g_next))
    return h_new, n_next, (h, nq, mem_n, q, k, v, o)


def _xattn_bwd(dh_new, saved, mem, g_x, g_m, wq, wk, wv, wo, tag, push):
    h, nq, mem_n, q, k, v, o = saved
    do = _mm([(dh_new, wo, "nt")], f"xa_do_{tag}", out_dtype=BF16)
    d_wo = _mm_tn(o, dh_new, f"xa_dwo_{tag}")
    dq, dk, dv = _attn_bwd(q, k, v, do, f"xa_attn_bwd_{tag}")
    d_wq = _mm_tn(nq, dq, f"xa_dwq_{tag}")
    d_wk = _mm_tn(mem_n, dk, f"xa_dwk_{tag}")
    d_wv = _mm_tn(mem_n, dv, f"xa_dwv_{tag}")
    token = push([d_wq, d_wk, d_wv, d_wo])
    dmem_n = _mm([(dk, wk, "nt"), (dv, wv, "nt")], f"xa_dmem_{tag}", after=token)
    _, d_gm = _rms_bwd(dmem_n, mem, _row(g_m), None, f"xa_mem_rms_bwd_{tag}")
    dh, d_gx = _mm([(dq, wq, "nt")], f"xa_dnq_{tag}", rms_bwd=(h, _row(g_x), dh_new), tm=1024, after=token)
    return dh, dict(g_xattn=d_gx, g_mem=d_gm)


def _ffn_fwd(h, n, wgt, wut, wd, g_next, tag, after):
    a, b, hid = _ffn_up(n, wgt, wut, f"ffn_up_{tag}", after=after)
    if g_next is None:
        h_new, n_next = _mm([(hid, wd, "nn")], f"ffn_down_{tag}", res=h, tm=1024, tn=1024), None
    else:
        h_new, n_next = _mm([(hid, wd, "nn")], f"ffn_down_{tag}", res=h, rms_g=_row(g_next), tm=1024)
    return h_new, n_next, (h, n, a, b, hid)


def _ffn_bwd(dh_new, saved, g_f, wgt, wut, wd, tag, push):
    h, n, a, b, hid = saved
    da, db = _ffn_dhid(dh_new, wd, a, b, f"ffn_dhid_{tag}")
    d_wd = _mm_tn(hid, dh_new, f"ffn_dwd_{tag}", tn=512)
    d_wgt = _mm_tn(da, n, f"ffn_dwg_{tag}", tn=512)
    d_wut = _mm_tn(db, n, f"ffn_dwu_{tag}", tn=512)
    token = push([d_wgt, d_wut, d_wd])
    dh, d_gf = _mm([(da, wgt, "nn"), (db, wut, "nn")], f"ffn_dn_{tag}", rms_bwd=(h, _row(g_f), dh_new), tm=512,
                   after=token)
    return dh, dict(g_ffn=d_gf)


_XA = ["xa_w_q", "xa_w_k", "xa_w_v", "xa_w_o"]
_FFN = ["ffn_w_gate", "ffn_w_up", "ffn_w_down"]
GATHERS = {
    "ev_in": [("ev_w_in", 0)],
    "xa0": [("ev_w_out", 0)] + [(n, 0) for n in _XA],
    "ffn0": [(n, 0) for n in _FFN],
    "od": [("od_w_in", 0), ("od_w_out", 0)],
    "xa1": [(n, 1) for n in _XA],
    "ffn1": [(n, 1) for n in _FFN],
}
SCATTERS = {
    "ffn1": [(n, 1) for n in _FFN],
    "xa1": [(n, 1) for n in _XA],
    "od": [("od_w_in", 0), ("od_w_out", 0)],
    "ffn0": [(n, 0) for n in _FFN],
    "xa0": [(n, 0) for n in _XA],
    "ev_out": [("ev_w_out", 0)],
    "ev_in": [("ev_w_in", 0)],
}


def _local_step(x, mem, loss_target, W, comm):
    grads = {}

    h0 = x
    (ev_w_in_t,), token = comm.weights("ev_in", None)
    n0 = _rms_fwd(h0, _row(W["g_mix"][0]), "ev_rms", after=token)
    z = _mm([(n0, ev_w_in_t, "nt")], "ev_in", tn=1280)
    token = comm.prefetch(["ffn0"], z)
    ab, ca = _conv_fwd(z, W["ev_a_conv_w"][0], W["ev_a_conv_b"], W["ev_a_ln_g"], W["ev_a_ln_b"],
                       W["ev_b_conv_w"][0], W["ev_b_conv_b"], "ev_conv", after=token)
    (ev_w_out, *xa_w0), _ = comm.weights("xa0", ab)
    h1, nq0 = _mm([(ab, ev_w_out, "nn")], "ev_out", res=h0, rms_g=_row(W["g_xattn"][0]))
    token = comm.prefetch(["od", "xa1"], nq0)
    h2, nf0, xa0 = _xattn_fwd(h1, nq0, mem, W["g_mem"][0], *xa_w0, W["g_ffn"][0], "l0", token)
    ffn_w0, _ = comm.weights("ffn0", nf0)
    token = comm.prefetch(["ffn1"], nf0)
    h3, n3, ff0 = _ffn_fwd(h2, nf0, *ffn_w0, W["g_mix"][1], "l0", token)

    (od_w_in_t, od_w_out), _ = comm.weights("od", n3)
    zp = _mm([(n3, od_w_in_t, "nt")], "od_in", tn=1024)
    D = x.shape[1]
    ws = W["od_w_s"][0].astype(BF16)
    wst = jnp.swapaxes(ws, 1, 2)
    bsb = jnp.repeat(jnp.transpose(W["od_b_s"][0]), D // C_GROUPS, axis=1)
    y_sgu = _sgu_fwd(zp, W["od_c_ln_g"], W["od_c_ln_b"], ws, bsb, "od_sgu")
    h4, nq1 = _mm([(y_sgu, od_w_out, "nn")], "od_out", res=h3, rms_g=_row(W["g_xattn"][1]))
    xa_w1, _ = comm.weights("xa1", nq1)
    h5, nf1, xa1 = _xattn_fwd(h4, nq1, mem, W["g_mem"][1], *xa_w1, W["g_ffn"][1], "l1", None)
    ffn_w1, _ = comm.weights("ffn1", nf1)
    h6, _, ff1 = _ffn_fwd(h5, nf1, *ffn_w1, None, "l1", None)

    loss_row, dh6, d_gfinal = _loss_bwd(h6, _row(W["g_final"]), loss_target, "loss")
    grads["g_final"] = d_gfinal.reshape(-1)

    dh5, g_ff1 = _ffn_bwd(dh6, ff1, W["g_ffn"][1], *ffn_w1, "l1", lambda dws: comm.grads("ffn1", dws))
    dh4, g_xa1 = _xattn_bwd(dh5, xa1, mem, W["g_xattn"][1], W["g_mem"][1], *xa_w1, "l1",
                            lambda dws: comm.grads("xa1", dws))
    dy_sgu = _mm([(dh4, od_w_out, "nt")], "od_dy", tn=1024)
    d_od_out = _mm_tn(y_sgu, dh4, "od_dwout", tn=1024)
    dzp, d_ws, d_bsb, d_clng, d_clnb = _sgu_bwd(dy_sgu, zp, W["od_c_ln_g"], W["od_c_ln_b"], ws, wst, bsb, "od_sgu_bwd")
    grads["od_w_s"] = d_ws[None]
    grads["od_b_s"] = jnp.transpose(_group_sum(d_bsb, C_GROUPS, "od_dbs"))[None]
    grads["od_c_ln_g"], grads["od_c_ln_b"] = d_clng, d_clnb
    token = comm.grads("od", [_mm_tn(dzp, n3, "od_dwin", tn=512), d_od_out])
    dh3, d_gmix1 = _mm([(dzp, od_w_in_t, "nn")], "od_dn", rms_bwd=(h3, _row(W["g_mix"][1]), dh4), tm=1024, after=token)

    dh2, g_ff0 = _ffn_bwd(dh3, ff0, W["g_ffn"][0], *ffn_w0, "l0", lambda dws: comm.grads("ffn0", dws))
    dh1, g_xa0 = _xattn_bwd(dh2, xa0, mem, W["g_xattn"][0], W["g_mem"][0], *xa_w0, "l0",
                            lambda dws: comm.grads("xa0", dws))
    token = comm.grads("ev_out", [_mm_tn(ab, dh1, "ev_dwout", tn=1024)])
    dab = _mm([(dh1, ev_w_out, "nt")], "ev_dab", tn=1024, after=token)
    dca, d_lng, d_lnb, d_ba = _conv_bwd_ln(dab, ca, W["ev_a_ln_g"], W["ev_a_ln_b"], "ev_conv_bwd_ln")
    dz, d_wa, d_wb, d_bb = _conv_bwd(z, dca, dab, W["ev_a_conv_w"][0], W["ev_b_conv_w"][0], W["ev_b_conv_b"],
                                     "ev_conv_bwd")
    grads.update(ev_a_ln_g=d_lng, ev_a_ln_b=d_lnb, ev_a_conv_b=d_ba, ev_b_conv_b=d_bb,
                 ev_a_conv_w=d_wa[None], ev_b_conv_w=d_wb[None])
    token = comm.grads("ev_in", [_mm_tn(dz, n0, "ev_dwin", tn=512)])
    grad_x, d_gmix0 = _mm([(dz, ev_w_in_t, "nn")], "ev_dn", rms_bwd=(h0, _row(W["g_mix"][0]), dh1), tm=1024, after=token)

    grads["g_mix"] = jnp.concatenate([d_gmix0, d_gmix1], axis=0)
    for key in ("g_xattn", "g_mem"):
        grads[key] = jnp.concatenate([g_xa0[key], g_xa1[key]], axis=0)
    grads["g_ffn"] = jnp.concatenate([g_ff0["g_ffn"], g_ff1["g_ffn"]], axis=0)
    return loss_row, grad_x, grads


class _Exchanges:
    def __init__(self, shards, dev_idx, after):
        self.shards, self.dev_idx = shards, dev_idx
        self.gathering, self.scattering = {}, {}
        self.first = _all_gather_weights(self._pack(GATHERS["ev_in"]), self._rows(GATHERS["ev_in"]), "ag_ev_in",
                                         after=after)
        self.first_token = self.prefetch(["xa0"], self.first[0])

    def _rows(self, entries):
        return [self.shards[e].shape[0] for e in entries]

    def _pack(self, entries):
        return jnp.concatenate([self.shards[e] for e in entries], axis=0)

    def prefetch(self, gathers, after):
        for name in gathers:
            rows = self._rows(GATHERS[name])
            pack = self._pack(GATHERS[name])
            lands = [lax.empty((NDEV * r, pack.shape[1]), pack.dtype) for r in rows]
            plan, n = _gather_plan(rows)
            send, recv, srcs, lands, after = _split_start([pack], lands, plan, n, after, f"ag_{name}_start")
            self.gathering[name] = (send, recv, srcs, lands, plan, rows)
        return after

    def weights(self, name, after):
        if name == "ev_in":
            return self.first, self.first_token
        send, recv, srcs, lands, plan, rows = self.gathering.pop(name)
        _, lands = _split_wait(send, recv, srcs, lands, plan, after, f"ag_{name}_wait")
        return _place_own([self.shards[e] for e in GATHERS[name]], lands, self.dev_idx, f"ag_{name}_own"), None

    def grads(self, name, dws):
        rows = self._rows(SCATTERS[name])
        lands = [lax.empty((NDEV - 1, r, d.shape[1]), d.dtype) for r, d in zip(rows, dws)]
        plan, n = _scatter_plan(rows)
        send, recv, srcs, lands, token = _split_start(dws, lands, plan, n, None, f"rs_{name}_start")
        self.scattering[name] = (send, recv, srcs, lands, plan)
        return token

    def received(self, after):
        out = {}
        for name, (send, recv, srcs, lands, plan) in self.scattering.items():
            srcs, lands = _split_wait(send, recv, srcs, lands, plan, after, f"rs_{name}_wait")
            for entry, g, got in zip(SCATTERS[name], srcs, lands):
                out[entry] = (g, got)
        return out


def kernel(x, mem, g_mix, g_xattn, g_mem, g_ffn, g_final, ev_w_in, ev_a_conv_w, ev_a_conv_b, ev_a_ln_g, ev_a_ln_b, ev_b_conv_w, ev_b_conv_b, ev_w_out, od_w_in, od_c_ln_g, od_c_ln_b, od_w_s, od_b_s, od_w_out, xa_w_q, xa_w_k, xa_w_v, xa_w_o, ffn_w_gate, ffn_w_up, ffn_w_down, loss_target, m_g_mix, m_g_xattn, m_g_mem, m_g_ffn, m_g_final, m_ev_w_in, m_ev_a_conv_w, m_ev_a_conv_b, m_ev_a_ln_g, m_ev_a_ln_b, m_ev_b_conv_w, m_ev_b_conv_b, m_ev_w_out, m_od_w_in, m_od_c_ln_g, m_od_c_ln_b, m_od_w_s, m_od_b_s, m_od_w_out, m_xa_w_q, m_xa_w_k, m_xa_w_v, m_xa_w_o, m_ffn_w_gate, m_ffn_w_up, m_ffn_w_down, v_g_mix, v_g_xattn, v_g_mem, v_g_ffn, v_g_final, v_ev_w_in, v_ev_a_conv_w, v_ev_a_conv_b, v_ev_a_ln_g, v_ev_a_ln_b, v_ev_b_conv_w, v_ev_b_conv_b, v_ev_w_out, v_od_w_in, v_od_c_ln_g, v_od_c_ln_b, v_od_w_s, v_od_b_s, v_od_w_out, v_xa_w_q, v_xa_w_k, v_xa_w_v, v_xa_w_o, v_ffn_w_gate, v_ffn_w_up, v_ffn_w_down):
    local = dict(g_mix=g_mix, g_xattn=g_xattn, g_mem=g_mem, g_ffn=g_ffn, g_final=g_final, ev_w_in=ev_w_in, ev_a_conv_w=ev_a_conv_w, ev_a_conv_b=ev_a_conv_b, ev_a_ln_g=ev_a_ln_g, ev_a_ln_b=ev_a_ln_b, ev_b_conv_w=ev_b_conv_w, ev_b_conv_b=ev_b_conv_b, ev_w_out=ev_w_out, od_w_in=od_w_in, od_c_ln_g=od_c_ln_g, od_c_ln_b=od_c_ln_b, od_w_s=od_w_s, od_b_s=od_b_s, od_w_out=od_w_out, xa_w_q=xa_w_q, xa_w_k=xa_w_k, xa_w_v=xa_w_v, xa_w_o=xa_w_o, ffn_w_gate=ffn_w_gate, ffn_w_up=ffn_w_up, ffn_w_down=ffn_w_down)
    mom = dict(g_mix=m_g_mix, g_xattn=m_g_xattn, g_mem=m_g_mem, g_ffn=m_g_ffn, g_final=m_g_final, ev_w_in=m_ev_w_in, ev_a_conv_w=m_ev_a_conv_w, ev_a_conv_b=m_ev_a_conv_b, ev_a_ln_g=m_ev_a_ln_g, ev_a_ln_b=m_ev_a_ln_b, ev_b_conv_w=m_ev_b_conv_w, ev_b_conv_b=m_ev_b_conv_b, ev_w_out=m_ev_w_out, od_w_in=m_od_w_in, od_c_ln_g=m_od_c_ln_g, od_c_ln_b=m_od_c_ln_b, od_w_s=m_od_w_s, od_b_s=m_od_b_s, od_w_out=m_od_w_out, xa_w_q=m_xa_w_q, xa_w_k=m_xa_w_k, xa_w_v=m_xa_w_v, xa_w_o=m_xa_w_o, ffn_w_gate=m_ffn_w_gate, ffn_w_up=m_ffn_w_up, ffn_w_down=m_ffn_w_down)
    vel = dict(g_mix=v_g_mix, g_xattn=v_g_xattn, g_mem=v_g_mem, g_ffn=v_g_ffn, g_final=v_g_final, ev_w_in=v_ev_w_in, ev_a_conv_w=v_ev_a_conv_w, ev_a_conv_b=v_ev_a_conv_b, ev_a_ln_g=v_ev_a_ln_g, ev_a_ln_b=v_ev_a_ln_b, ev_b_conv_w=v_ev_b_conv_w, ev_b_conv_b=v_ev_b_conv_b, ev_w_out=v_ev_w_out, od_w_in=v_od_w_in, od_c_ln_g=v_od_c_ln_g, od_c_ln_b=v_od_c_ln_b, od_w_s=v_od_w_s, od_b_s=v_od_b_s, od_w_out=v_od_w_out, xa_w_q=v_xa_w_q, xa_w_k=v_xa_w_k, xa_w_v=v_xa_w_v, xa_w_o=v_xa_w_o, ffn_w_gate=v_ffn_w_gate, ffn_w_up=v_ffn_w_up, ffn_w_down=v_ffn_w_down)
    D = x.shape[-1]
    dev = 4 * lax.axis_index("x") + 2 * lax.axis_index("y") + lax.axis_index("c")

    def comm_layout(n, a):
        return jnp.transpose(a) if _shard_axis(n) == 2 else a

    shards = {(n, i): comm_layout(n, local[n][i]).astype(BF16) for n in BIG for i in range(local[n].shape[0])}
    small_sizes = [local[n].size for n in SMALL_SHARDED]
    small_block = _pad_rows(jnp.concatenate([local[n].reshape(-1) for n in SMALL_SHARDED]), 128, 8)
    small_all = _all_gather(small_block, "ag_small")
    comm = _Exchanges(shards, jnp.reshape(dev, (1,)).astype(jnp.int32), small_all)
    small_all = small_all.reshape(NDEV, -1)

    W = {n: local[n] for n in REPLICATED}
    o0 = 0
    for n, sz in zip(SMALL_SHARDED, small_sizes):
        blocks = small_all[:, o0:o0 + sz].reshape((NDEV,) + local[n].shape)
        W[n] = _full_from_blocks(blocks, _shard_axis(n))
        o0 += sz

    loss_row, grad_x, grads = _local_step(x[0], mem[0], loss_target[0], W, comm)

    received = comm.received(grad_x)
    rest = REPLICATED + SMALL_SHARDED
    rest_full_shapes = [grads[n].shape for n in rest]
    g_rest = _pad_rows(jnp.concatenate([grads[n].astype(F32).reshape(-1) for n in rest]), D, 8)
    small_rows = g_rest.shape[0]
    small_plan, small_n = _gather_plan([small_rows])
    small_send, small_recv, small_srcs, small_lands, token = _split_start(
        [g_rest], [lax.empty((NDEV * small_rows, D), F32)], small_plan, small_n, received["ev_w_in", 0][1],
        "ag_small_grads_start")

    gsh, delta, new_m, new_v = {}, {}, {}, {}
    def stacked_layout(n, a):
        return jnp.swapaxes(a, 1, 2) if _shard_axis(n) == 2 else a

    for n in BIG:
        parts = [received[n, i] for i in range(local[n].shape[0])]
        outs = _finish_weight([p[0] for p in parts], [p[1] for p in parts], comm.dev_idx,
                              *(stacked_layout(n, a) for a in (local[n], mom[n], vel[n])), f"finish_{n}", after=token)
        gsh[n], delta[n], new_m[n], new_v[n] = (stacked_layout(n, o) for o in outs)

    _, small_lands = _split_wait(small_send, small_recv, small_srcs, small_lands, small_plan, delta[BIG[-1]],
                                 "ag_small_grads_wait")
    partials = _place_own([g_rest], small_lands, comm.dev_idx, "ag_small_grads_own")[0]
    g_rest = _sum_slots(partials.reshape(NDEV, small_rows, D), "sum_small_grads").reshape(-1)
    o0 = 0
    for n, shp in zip(rest, rest_full_shapes):
        sz = 1
        for s in shp:
            sz *= s
        full = g_rest[o0:o0 + sz].reshape(shp)
        o0 += sz
        if n in SMALL_SHARDED:
            full = lax.dynamic_index_in_dim(_blocks_from_full(full, _shard_axis(n)), dev, 0, keepdims=False)
        gsh[n] = full.reshape(local[n].shape)

    for n in rest:
        delta[n], new_m[n], new_v[n] = _adamw(local[n], gsh[n], mom[n], vel[n], f"adamw_{n}")

    loss = lax.psum(loss_row[0, 0], ("x", "y", "c"))
    return (loss, grad_x[None], *[gsh[n] for n in WEIGHTS], *[delta[n] for n in WEIGHTS],
            *[new_m[n] for n in WEIGHTS], *[new_v[n] for n in WEIGHTS])
```

```python
import jax
import jax.numpy as jnp
from jax import lax
from jax.experimental import pallas as pl
from jax.experimental.pallas import tpu as pltpu

F32, BF16 = jnp.float32, jnp.bfloat16
NDEV = 8
RMS_EPS = 1e-6
LN_EPS = 1e-5
CHUNK = 128
C_GROUPS = 8
XA_HEADS = 4
ADAM_LR, ADAM_B1, ADAM_B2, ADAM_EPS, ADAM_WD, ADAM_STEP = 0.001, 0.9, 0.999, 1e-08, 0.01, 10
HALO = 16
ROW_CHUNK = 32
V7X_VMEM_LIMIT = 56 * 1024 * 1024
MESH = pl.DeviceIdType.MESH

TS_ROW = 512
TS_MM = 2048
TN_MM = 1408
TS_FFN = 512
MM_ROW_CHUNK = 256
TS_CONV = 512
TS_SGU = 512
TS_ATTN = 2048


def _cp(*sem):
    return pltpu.CompilerParams(dimension_semantics=sem, vmem_limit_bytes=V7X_VMEM_LIMIT)


def _pick(n, pref, align):
    for t in range(min(n, pref), 0, -1):
        if n % t == 0 and (t % align == 0 or t == n):
            return t
    return n


def _sigmoid(x):
    return 0.5 * jnp.tanh(0.5 * x) + 0.5


def _dot(a, b):
    return jnp.dot(a, b, preferred_element_type=F32)


def _dot_nt(a, b):
    return lax.dot_general(a, b, (((1,), (1,)), ((), ())), preferred_element_type=F32)


def _dot_tn(a, b):
    return lax.dot_general(a, b, (((0,), (0,)), ((), ())), preferred_element_type=F32)


_ANY = pl.BlockSpec(memory_space=pl.ANY)
_RESIDENT = pl.Buffered(1)


def _after(after):
    return ([], []) if after is None else ([_ANY], [after])


def _rms_fwd(h, g, name, after=None):
    S, D = h.shape
    ts = _pick(S, TS_MM, 16)
    after_specs, after_ops = _after(after)

    def body(h_ref, g_ref, *rest):
        o_ref = rest[-1]
        x = h_ref[...]
        r = lax.rsqrt(jnp.mean(x * x, axis=-1, keepdims=True) + RMS_EPS)
        o_ref[...] = ((x * r) * g_ref[...]).astype(o_ref.dtype)

    return pl.pallas_call(
        body, grid=(S // ts,),
        in_specs=[pl.BlockSpec((ts, D), lambda i: (i, 0)), pl.BlockSpec((1, D), lambda i: (0, 0))] + after_specs,
        out_specs=pl.BlockSpec((ts, D), lambda i: (i, 0)),
        out_shape=jax.ShapeDtypeStruct((S, D), BF16), compiler_params=_cp("parallel"), name=name)(h, g, *after_ops)


def _rms_bwd(dn, h, g, dres, name):
    S, D = h.shape
    ts = _pick(S, TS_ROW, 8)
    has_res = dres is not None

    def body(*refs):
        if has_res:
            dn_ref, h_ref, g_ref, dres_ref, dh_ref, dg_ref = refs
        else:
            dn_ref, h_ref, g_ref, dh_ref, dg_ref = refs
        x = h_ref[...]
        dn_ = dn_ref[...].astype(F32)
        r = lax.rsqrt(jnp.mean(x * x, axis=-1, keepdims=True) + RMS_EPS)
        xr = x * r

        @pl.when(pl.program_id(0) == 0)
        def _():
            dg_ref[...] = jnp.zeros_like(dg_ref)

        dg_ref[...] += jnp.sum(dn_ * xr, axis=0, keepdims=True)
        u = dn_ * g_ref[...]
        dh = r * u - xr * (r * jnp.mean(u * xr, axis=-1, keepdims=True))
        if has_res:
            dh = dh + dres_ref[...]
        dh_ref[...] = dh

    tile = pl.BlockSpec((ts, D), lambda i: (i, 0))
    vec = pl.BlockSpec((1, D), lambda i: (0, 0))
    ins = [dn, h, g] + ([dres] if has_res else [])
    return pl.pallas_call(
        body, grid=(S // ts,),
        in_specs=[tile, tile, vec] + ([tile] if has_res else []),
        out_specs=[tile, vec],
        out_shape=[jax.ShapeDtypeStruct((S, D), F32), jax.ShapeDtypeStruct((1, D), F32)],
        compiler_params=_cp("arbitrary"), name=name)(*ins)


def _loss_bwd(h, g, target, name):
    S, D = h.shape
    ts = _pick(S, TS_ROW, 8)

    def body(h_ref, g_ref, t_ref, loss_ref, dh_ref, dg_ref):
        x = h_ref[...]
        r = lax.rsqrt(jnp.mean(x * x, axis=-1, keepdims=True) + RMS_EPS)
        xr = x * r
        gg = g_ref[...]
        e = xr * gg - t_ref[...]

        @pl.when(pl.program_id(0) == 0)
        def _():
            dg_ref[...] = jnp.zeros_like(dg_ref)
            loss_ref[...] = jnp.zeros_like(loss_ref)

        tile_loss = jnp.sum(jnp.sum(e * e, axis=0, keepdims=True), axis=1, keepdims=True) * (0.5 / D)
        loss_ref[...] += jnp.broadcast_to(tile_loss, loss_ref.shape)
        dy = e * (1.0 / D)
        dg_ref[...] += jnp.sum(dy * xr, axis=0, keepdims=True)
        u = dy * gg
        dh_ref[...] = r * u - xr * (r * jnp.mean(u * xr, axis=-1, keepdims=True))

    tile = pl.BlockSpec((ts, D), lambda i: (i, 0))
    vec = pl.BlockSpec((1, D), lambda i: (0, 0))
    return pl.pallas_call(
        body, grid=(S // ts,),
        in_specs=[tile, vec, tile],
        out_specs=[pl.BlockSpec((1, 128), lambda i: (0, 0)), tile, vec],
        out_shape=[jax.ShapeDtypeStruct((1, 128), F32), jax.ShapeDtypeStruct((S, D), F32),
                   jax.ShapeDtypeStruct((1, D), F32)],
        compiler_params=_cp("arbitrary"), name=name)(h, g, target)


def _mm(pairs, name, out_dtype=F32, res=None, rms_g=None, rms_bwd=None, tm=None, tn=None, after=None):
    M = pairs[0][0].shape[0]
    N = pairs[0][1].shape[1 if pairs[0][2] == "nn" else 0]
    whole_rows = rms_g is not None or rms_bwd is not None
    tm = _pick(M, tm or TS_MM, 16)
    tn = N if whole_rows else _pick(N, tn or TN_MM, 128)
    npair = len(pairs)
    modes = [p[2] for p in pairs]
    after_specs, after_ops = _after(after)

    rc = MM_ROW_CHUNK if whole_rows and tm % MM_ROW_CHUNK == 0 else tm

    def body(*refs):
        rest = refs[2 * npair + len(after_ops):]
        res_ref = None
        if res is not None:
            res_ref, rest = rest[0], rest[1:]
        if rms_bwd is not None:
            dg_ref = rest[4]

            @pl.when(pl.program_id(0) == 0)
            def _():
                dg_ref[...] = jnp.zeros_like(dg_ref)

        for r0 in range(0, tm, rc):
            rows = pl.ds(r0, rc)
            acc = None
            for p in range(npair):
                a_ = refs[2 * p][rows, :].astype(BF16)
                d = _dot(a_, refs[2 * p + 1][...]) if modes[p] == "nn" else _dot_nt(a_, refs[2 * p + 1][...])
                acc = d if acc is None else acc + d
            if res_ref is not None:
                acc = acc + res_ref[rows, :]
            if rms_bwd is not None:
                h_ref, g_ref, dres_ref, dh_ref, _ = rest
                x = h_ref[rows, :]
                r = lax.rsqrt(jnp.mean(x * x, axis=-1, keepdims=True) + RMS_EPS)
                xr = x * r
                dg_ref[...] += jnp.sum(acc * xr, axis=0, keepdims=True)
                u = acc * g_ref[...]
                dh_ref[rows, :] = r * u - xr * (r * jnp.mean(u * xr, axis=-1, keepdims=True)) + dres_ref[rows, :]
            elif rms_g is not None:
                g_ref, o_ref, n_ref = rest
                o_ref[rows, :] = acc
                r = lax.rsqrt(jnp.mean(acc * acc, axis=-1, keepdims=True) + RMS_EPS)
                n_ref[rows, :] = ((acc * r) * g_ref[...]).astype(BF16)
            else:
                rest[0][rows, :] = acc.astype(rest[0].dtype)

    in_specs, ins = [], []
    for a, w, mode in pairs:
        K = a.shape[1]
        in_specs.append(pl.BlockSpec((tm, K), lambda i, j: (i, 0)))
        once = _RESIDENT if tn == N else None
        in_specs.append(pl.BlockSpec((K, tn), lambda i, j: (0, j), pipeline_mode=once) if mode == "nn"
                        else pl.BlockSpec((tn, K), lambda i, j: (j, 0), pipeline_mode=once))
        ins += [a, w]
    in_specs += after_specs
    ins += after_ops
    tile = pl.BlockSpec((tm, tn), lambda i, j: (i, j))
    vec = pl.BlockSpec((1, tn), lambda i, j: (0, j))
    if res is not None:
        in_specs.append(tile)
        ins.append(res)
    sem = ("parallel", "parallel")
    if rms_bwd is not None:
        in_specs += [tile, vec, tile]
        ins += list(rms_bwd)
        out_specs = [tile, vec]
        out_shape = [jax.ShapeDtypeStruct((M, N), F32), jax.ShapeDtypeStruct((1, N), F32)]
        sem = ("arbitrary", "arbitrary")
    elif rms_g is not None:
        in_specs.append(vec)
        ins.append(rms_g)
        out_specs = [tile, tile]
        out_shape = [jax.ShapeDtypeStruct((M, N), F32), jax.ShapeDtypeStruct((M, N), BF16)]
    else:
        out_specs = tile
        out_shape = jax.ShapeDtypeStruct((M, N), out_dtype)
    return pl.pallas_call(
        body, grid=(M // tm, N // tn), in_specs=in_specs, out_specs=out_specs, out_shape=out_shape,
        compiler_params=_cp(*sem), name=name)(*ins)


def _mm_tn(a, b, name, ts=None, tn=None):
    S, K = a.shape
    N = b.shape[1]
    ts = _pick(S, ts or TS_MM, 16)
    tn = _pick(N, tn or TN_MM, 128)
    nsteps = S // ts

    def body(a_ref, b_ref, o_ref, acc_ref):
        s = pl.program_id(1)

        @pl.when(s == 0)
        def _():
            acc_ref[...] = jnp.zeros_like(acc_ref)

        acc_ref[...] += _dot_tn(a_ref[...].astype(BF16), b_ref[...].astype(BF16))

        @pl.when(s == nsteps - 1)
        def _():
            o_ref[...] = acc_ref[...].astype(o_ref.dtype)

    return pl.pallas_call(
        body, grid=(N // tn, nsteps),
        in_specs=[pl.BlockSpec((ts, K), lambda j, s: (s, 0)), pl.BlockSpec((ts, tn), lambda j, s: (s, j))],
        out_specs=pl.BlockSpec((K, tn), lambda j, s: (0, j)), out_shape=jax.ShapeDtypeStruct((K, N), BF16),
        scratch_shapes=[pltpu.VMEM((K, tn), F32)],
        compiler_params=_cp("parallel", "arbitrary"), name=name)(a, b)


def _col_chunk(n):
    return 256 if n % 256 == 0 else 128


def _ffn_up(n, wgt, wut, name, after=None):
    S, D = n.shape
    F = wgt.shape[0]
    tm = _pick(S, TS_FFN, 16)
    ce = _col_chunk(F)
    after_specs, after_ops = _after(after)

    def body(n_ref, wg_ref, wu_ref, *rest):
        a_ref, b_ref, hid_ref = rest[-3:]
        x = n_ref[...]
        for c0 in range(0, F, ce):
            a = _dot_nt(x, wg_ref[c0:c0 + ce, :])
            b = _dot_nt(x, wu_ref[c0:c0 + ce, :])
            a_ref[:, c0:c0 + ce] = a.astype(BF16)
            b_ref[:, c0:c0 + ce] = b.astype(BF16)
            hid_ref[:, c0:c0 + ce] = (a * _sigmoid(a) * b).astype(BF16)

    wspec = pl.BlockSpec((F, D), lambda i: (0, 0), pipeline_mode=_RESIDENT)
    ospec = pl.BlockSpec((tm, F), lambda i: (i, 0))
    osh = jax.ShapeDtypeStruct((S, F), BF16)
    return pl.pallas_call(
        body, grid=(S // tm,),
        in_specs=[pl.BlockSpec((tm, D), lambda i: (i, 0)), wspec, wspec] + after_specs,
        out_specs=[ospec, ospec, ospec], out_shape=[osh, osh, osh],
        compiler_params=_cp("parallel"), name=name)(n, wgt, wut, *after_ops)


def _ffn_dhid(dh, wd, a, b, name):
    S, D = dh.shape
    F = wd.shape[0]
    tm = _pick(S, TS_FFN, 16)
    ce = _col_chunk(F)

    def body(dh_ref, wd_ref, a_ref, b_ref, da_ref, db_ref):
        x = dh_ref[...].astype(BF16)
        for c0 in range(0, F, ce):
            g = _dot_nt(x, wd_ref[c0:c0 + ce, :]).astype(BF16)
            a_ = a_ref[:, c0:c0 + ce]
            sg = _sigmoid(a_)
            silu = a_ * sg
            da_ref[:, c0:c0 + ce] = (g * b_ref[:, c0:c0 + ce]) * (sg + silu * (1.0 - sg))
            db_ref[:, c0:c0 + ce] = g * silu

    tile = pl.BlockSpec((tm, F), lambda i: (i, 0))
    osh = jax.ShapeDtypeStruct((S, F), BF16)
    return pl.pallas_call(
        body, grid=(S // tm,),
        in_specs=[pl.BlockSpec((tm, D), lambda i: (i, 0)),
                  pl.BlockSpec((F, D), lambda i: (0, 0), pipeline_mode=_RESIDENT), tile, tile],
        out_specs=[tile, tile], out_shape=[osh, osh],
        compiler_params=_cp("parallel"), name=name)(dh, wd, a, b)


def _softmax_rows(s):
    m = jnp.max(s, axis=-1, keepdims=True)
    p = jnp.exp(s - m)
    return p / jnp.sum(p, axis=-1, keepdims=True)


def _attn_fwd(q, k, v, name):
    S, D = q.shape
    M = k.shape[0]
    hd = D // XA_HEADS
    scale = hd ** -0.5
    ts = _pick(S, TS_ATTN, 16)

    def body(q_ref, k_ref, v_ref, o_ref):
        for h in range(XA_HEADS):
            sl = slice(h * hd, (h + 1) * hd)
            p = _softmax_rows(_dot_nt(q_ref[:, sl], k_ref[:, sl]) * scale)
            o_ref[:, sl] = _dot(p.astype(BF16), v_ref[:, sl]).astype(BF16)

    tile = pl.BlockSpec((ts, D), lambda i: (i, 0))
    memspec = pl.BlockSpec((M, D), lambda i: (0, 0))
    return pl.pallas_call(
        body, grid=(S // ts,), in_specs=[tile, memspec, memspec], out_specs=tile,
        out_shape=jax.ShapeDtypeStruct((S, D), BF16), compiler_params=_cp("parallel"), name=name)(q, k, v)


def _attn_bwd(q, k, v, do, name):
    S, D = q.shape
    M = k.shape[0]
    hd = D // XA_HEADS
    scale = hd ** -0.5
    ts = _pick(S, TS_ATTN, 16)

    def body(q_ref, k_ref, v_ref, do_ref, dq_ref, dk_ref, dv_ref):
        @pl.when(pl.program_id(0) == 0)
        def _():
            dk_ref[...] = jnp.zeros_like(dk_ref)
            dv_ref[...] = jnp.zeros_like(dv_ref)

        for h in range(XA_HEADS):
            sl = slice(h * hd, (h + 1) * hd)
            qh, kh, vh, doh = q_ref[:, sl], k_ref[:, sl], v_ref[:, sl], do_ref[:, sl]
            p = _softmax_rows(_dot_nt(qh, kh) * scale)
            dp = _dot_nt(doh, vh)
            dv_ref[:, sl] += _dot_tn(p.astype(BF16), doh)
            delta = jnp.sum(dp * p, axis=-1, keepdims=True)
            ds = (p * (dp - delta) * scale).astype(BF16)
            dq_ref[:, sl] = _dot(ds, kh).astype(BF16)
            dk_ref[:, sl] += _dot_tn(ds, qh)

    tile = pl.BlockSpec((ts, D), lambda i: (i, 0))
    memspec = pl.BlockSpec((M, D), lambda i: (0, 0))
    return pl.pallas_call(
        body, grid=(S // ts,), in_specs=[tile, memspec, memspec, tile], out_specs=[tile, memspec, memspec],
        out_shape=[jax.ShapeDtypeStruct((S, D), BF16), jax.ShapeDtypeStruct((M, D), F32),
                   jax.ShapeDtypeStruct((M, D), F32)],
        compiler_params=_cp("arbitrary"), name=name)(q, k, v, do)


def _halo_specs(ts, width, col):
    per = ts // HALO

    def prev(i):
        return (jnp.maximum(i * per - 1, 0), col)

    def nxt(i, n_tiles):
        return (jnp.minimum((i + 1) * per, n_tiles * per - 1), col)

    return prev, nxt


def _fill_ext(ext_ref, prev_val, main_val, next_val, first, last, ts):
    ext_ref[pl.ds(0, HALO), :] = jnp.where(first, 0.0, prev_val)
    ext_ref[pl.ds(HALO, ts), :] = main_val
    ext_ref[pl.ds(HALO + ts, HALO), :] = jnp.where(last, 0.0, next_val)


SUBLANES = 8


def _fill_shifted(sh_ref, ts):
    n = ts + 2 * HALO - SUBLANES
    for s in range(1, SUBLANES):
        sh_ref[s, pl.ds(0, n), :] = sh_ref[0, pl.ds(s, n), :]


def _tap(sh_ref, r0, offset, rc):
    q, s = divmod(offset, SUBLANES)
    return sh_ref[s, pl.ds(pl.multiple_of(r0 + SUBLANES * q, SUBLANES), rc), :]


def _conv_fwd(z, wa, ba, lng, lnb, wb, bb, name, after=None):
    S = z.shape[0]
    C = z.shape[1] // 5
    KA, KB = wa.shape[0], wb.shape[0]
    pa, pb = KA // 2, KB // 2
    assert pa <= HALO and pb <= HALO
    ts = _pick(S, TS_CONV, ROW_CHUNK)
    nt = S // ts
    rc = ROW_CHUNK
    prev, nxt = _halo_specs(ts, 5 * C, 0)
    after_specs, after_ops = _after(after)

    def body(*refs):
        compute(*refs[:9], *refs[9 + len(after_ops):])

    def compute(z_ref, zp_ref, zn_ref, wa_ref, ba_ref, lng_ref, lnb_ref, wb_ref, bb_ref, ab_ref, ca_ref,
                ga_sh, tb_ext, win_b):
        i = pl.program_id(0)
        first, last = i == 0, i == nt - 1

        def glu(r):
            return r[:, 0:C] * _sigmoid(r[:, C:2 * C])

        def gcb(r):
            return r[:, 4 * C:5 * C] * r[:, 2 * C:3 * C]

        _fill_ext(ga_sh.at[0], glu(zp_ref), glu(z_ref), glu(zn_ref), first, last, ts)
        _fill_shifted(ga_sh, ts)
        _fill_ext(tb_ext, gcb(zp_ref), gcb(z_ref), gcb(zn_ref), first, last, ts)

        def chunk(c, carry):
            r0 = pl.multiple_of(c * rc, rc)
            win_b[...] = tb_ext[pl.ds(r0, rc + 2 * HALO), :]
            acc = jnp.zeros((rc, C), F32)
            for k in range(KA):
                acc = acc + wa_ref[k:k + 1, :] * _tap(ga_sh, r0, HALO - pa + k, rc)
            ca = acc + ba_ref[...]
            ca_ref[pl.ds(r0, rc), :] = ca
            mu = jnp.mean(ca, axis=-1, keepdims=True)
            xc = ca - mu
            var = jnp.mean(xc * xc, axis=-1, keepdims=True)
            ln = xc * lax.rsqrt(var + LN_EPS) * lng_ref[...] + lnb_ref[...]
            ab_ref[pl.ds(r0, rc), 0:C] = (ln * _sigmoid(ln)).astype(BF16)
            cb = jnp.zeros((rc, C), F32) + bb_ref[...]
            for k in range(KB):
                cb = cb + wb_ref[k:k + 1, :] * win_b[pl.ds(HALO - pb + k, rc), :]
            ab_ref[pl.ds(r0, rc), C:2 * C] = (z_ref[pl.ds(r0, rc), 3 * C:4 * C] * cb).astype(BF16)
            return carry

        lax.fori_loop(0, ts // rc, chunk, 0)

    zspec = pl.BlockSpec((ts, 5 * C), lambda i: (i, 0))
    zprev = pl.BlockSpec((HALO, 5 * C), prev)
    znext = pl.BlockSpec((HALO, 5 * C), lambda i: nxt(i, nt))

    def full(a):
        return pl.BlockSpec(a.shape, lambda i: (0, 0))

    return pl.pallas_call(
        body, grid=(nt,),
        in_specs=[zspec, zprev, znext, full(wa), full(ba), full(lng), full(lnb), full(wb), full(bb)] + after_specs,
        out_specs=[pl.BlockSpec((ts, 2 * C), lambda i: (i, 0)), pl.BlockSpec((ts, C), lambda i: (i, 0))],
        out_shape=[jax.ShapeDtypeStruct((S, 2 * C), BF16), jax.ShapeDtypeStruct((S, C), F32)],
        scratch_shapes=[pltpu.VMEM((SUBLANES, ts + 2 * HALO, C), F32), pltpu.VMEM((ts + 2 * HALO, C), F32),
                        pltpu.VMEM((rc + 2 * HALO, C), F32)],
        compiler_params=_cp("parallel"), name=name)(z, z, z, wa, ba, lng, lnb, wb, bb, *after_ops)


def _conv_bwd_ln(dab, ca, lng, lnb, name):
    S, C = ca.shape
    ts = _pick(S, TS_ROW, 8)

    def body(da_ref, ca_ref, lng_ref, lnb_ref, dca_ref, dg_ref, db_ref, dbias_ref):
        @pl.when(pl.program_id(0) == 0)
        def _():
            dg_ref[...] = jnp.zeros_like(dg_ref)
            db_ref[...] = jnp.zeros_like(db_ref)
            dbias_ref[...] = jnp.zeros_like(dbias_ref)

        ca_ = ca_ref[...]
        mu = jnp.mean(ca_, axis=-1, keepdims=True)
        xc = ca_ - mu
        rstd = lax.rsqrt(jnp.mean(xc * xc, axis=-1, keepdims=True) + LN_EPS)
        xh = xc * rstd
        ln = xh * lng_ref[...] + lnb_ref[...]
        sg = _sigmoid(ln)
        dln = da_ref[...].astype(F32) * (sg * (1.0 + ln * (1.0 - sg)))
        dg_ref[...] += jnp.sum(dln * xh, axis=0, keepdims=True)
        db_ref[...] += jnp.sum(dln, axis=0, keepdims=True)
        dxh = dln * lng_ref[...]
        dca = rstd * (dxh - jnp.mean(dxh, axis=-1, keepdims=True) - xh * jnp.mean(dxh * xh, axis=-1, keepdims=True))
        dca_ref[...] = dca
        dbias_ref[...] += jnp.sum(dca, axis=0, keepdims=True)

    tile = pl.BlockSpec((ts, C), lambda i: (i, 0))
    vec = pl.BlockSpec((1, C), lambda i: (0, 0))
    vsh = jax.ShapeDtypeStruct((1, C), F32)
    return pl.pallas_call(
        body, grid=(S // ts,), in_specs=[tile, tile, vec, vec], out_specs=[tile, vec, vec, vec],
        out_shape=[jax.ShapeDtypeStruct((S, C), F32), vsh, vsh, vsh],
        compiler_params=_cp("arbitrary"), name=name)(dab, ca, lng, lnb)


def _conv_bwd(z, dca, dab, wa, wb, bb, name):
    S = z.shape[0]
    C = z.shape[1] // 5
    KA, KB = wa.shape[0], wb.shape[0]
    pa, pb = KA // 2, KB // 2
    ts = _pick(S, TS_CONV, ROW_CHUNK)
    nt = S // ts
    rc = ROW_CHUNK
    prev0, nxt0 = _halo_specs(ts, C, 0)
    prev1, nxt1 = _halo_specs(ts, C, 1)

    def body(z_ref, zp_ref, zn_ref, dca_ref, dcap_ref, dcan_ref, db_ref, dbp_ref, dbn_ref, wa_ref, wb_ref, bb_ref,
             dz_ref, dwa_ref, dwb_ref, dbb_ref,
             ga_sh, dca_sh, tb_ext, dcb_ext, win_tb, win_dcb, acc_a, acc_b, acc_bias):
        i = pl.program_id(0)
        first, last = i == 0, i == nt - 1

        @pl.when(first)
        def _():
            acc_a[...] = jnp.zeros_like(acc_a)
            acc_b[...] = jnp.zeros_like(acc_b)
            acc_bias[...] = jnp.zeros_like(acc_bias)

        def glu(r):
            return r[:, 0:C] * _sigmoid(r[:, C:2 * C])

        def gcb(r):
            return r[:, 4 * C:5 * C] * r[:, 2 * C:3 * C]

        def dcb(d, r):
            return d[...].astype(F32) * r[:, 3 * C:4 * C]

        _fill_ext(ga_sh.at[0], glu(zp_ref), glu(z_ref), glu(zn_ref), first, last, ts)
        _fill_shifted(ga_sh, ts)
        _fill_ext(dca_sh.at[0], dcap_ref[...], dca_ref[...], dcan_ref[...], first, last, ts)
        _fill_shifted(dca_sh, ts)
        _fill_ext(tb_ext, gcb(zp_ref), gcb(z_ref), gcb(zn_ref), first, last, ts)
        _fill_ext(dcb_ext, dcb(dbp_ref, zp_ref), dcb(db_ref, z_ref), dcb(dbn_ref, zn_ref), first, last, ts)

        def fold(x):
            return jnp.sum(x.reshape(rc // 8, 8, C), axis=0)

        def chunk(c, carry):
            r0 = pl.multiple_of(c * rc, rc)
            win_tb[...] = tb_ext[pl.ds(r0, rc + 2 * HALO), :]
            win_dcb[...] = dcb_ext[pl.ds(r0, rc + 2 * HALO), :]
            dca_c = _tap(dca_sh, r0, HALO, rc)
            dglu = jnp.zeros((rc, C), F32)
            for k in range(KA):
                dglu = dglu + wa_ref[k:k + 1, :] * _tap(dca_sh, r0, HALO + pa - k, rc)
                acc_a[k] += fold(dca_c * _tap(ga_sh, r0, HALO - pa + k, rc))
            val = z_ref[pl.ds(r0, rc), 0:C]
            sg = _sigmoid(z_ref[pl.ds(r0, rc), C:2 * C])
            dz_ref[pl.ds(r0, rc), 0:C] = (dglu * sg).astype(BF16)
            dz_ref[pl.ds(r0, rc), C:2 * C] = (dglu * val * sg * (1.0 - sg)).astype(BF16)
            dcb_c = win_dcb[pl.ds(HALO, rc), :]
            cb = jnp.zeros((rc, C), F32) + bb_ref[...]
            dt = jnp.zeros((rc, C), F32)
            for k in range(KB):
                tb_k = win_tb[pl.ds(HALO - pb + k, rc), :]
                cb = cb + wb_ref[k:k + 1, :] * tb_k
                dt = dt + wb_ref[k:k + 1, :] * win_dcb[pl.ds(HALO + pb - k, rc), :]
                acc_b[k] += fold(dcb_c * tb_k)
            acc_bias[...] += fold(dcb_c)
            db_c = db_ref[pl.ds(r0, rc), :].astype(F32)
            dz_ref[pl.ds(r0, rc), 2 * C:3 * C] = (dt * z_ref[pl.ds(r0, rc), 4 * C:5 * C]).astype(BF16)
            dz_ref[pl.ds(r0, rc), 3 * C:4 * C] = (db_c * cb).astype(BF16)
            dz_ref[pl.ds(r0, rc), 4 * C:5 * C] = (dt * z_ref[pl.ds(r0, rc), 2 * C:3 * C]).astype(BF16)
            return carry

        lax.fori_loop(0, ts // rc, chunk, 0)

        @pl.when(last)
        def _():
            dwa_ref[...] = jnp.sum(acc_a[...], axis=1)
            dwb_ref[...] = jnp.sum(acc_b[...], axis=1)
            dbb_ref[...] = jnp.sum(acc_bias[...], axis=0, keepdims=True)

    zspec = pl.BlockSpec((ts, 5 * C), lambda i: (i, 0))
    zprev = pl.BlockSpec((HALO, 5 * C), prev0)
    znext = pl.BlockSpec((HALO, 5 * C), lambda i: nxt0(i, nt))
    dspec = pl.BlockSpec((ts, C), lambda i: (i, 0))
    dprev = pl.BlockSpec((HALO, C), prev0)
    dnext = pl.BlockSpec((HALO, C), lambda i: nxt0(i, nt))
    bspec = pl.BlockSpec((ts, C), lambda i: (i, 1))
    bprev = pl.BlockSpec((HALO, C), prev1)
    bnext = pl.BlockSpec((HALO, C), lambda i: nxt1(i, nt))

    def full(shape):
        return pl.BlockSpec(shape, lambda i: (0,) * len(shape))

    ext = pltpu.VMEM((ts + 2 * HALO, C), F32)
    shifted = pltpu.VMEM((SUBLANES, ts + 2 * HALO, C), F32)
    win = pltpu.VMEM((rc + 2 * HALO, C), F32)
    return pl.pallas_call(
        body, grid=(nt,),
        in_specs=[zspec, zprev, znext, dspec, dprev, dnext, bspec, bprev, bnext,
                  full(wa.shape), full(wb.shape), full(bb.shape)],
        out_specs=[pl.BlockSpec((ts, 5 * C), lambda i: (i, 0)), full((KA, C)), full((KB, C)), full((1, C))],
        out_shape=[jax.ShapeDtypeStruct((S, 5 * C), BF16), jax.ShapeDtypeStruct((KA, C), F32),
                   jax.ShapeDtypeStruct((KB, C), F32), jax.ShapeDtypeStruct((1, C), F32)],
        scratch_shapes=[shifted, shifted, ext, ext, win, win,
                        pltpu.VMEM((KA, 8, C), F32), pltpu.VMEM((KB, 8, C), F32), pltpu.VMEM((8, C), F32)],
        compiler_params=_cp("arbitrary"), name=name)(z, z, z, dca, dca, dca, dab, dab, dab, wa, wb, bb)


_GELU_C = 0.7978845608028654
_GELU_A = 0.044715


def _gelu(x):
    return 0.5 * x * (1.0 + jnp.tanh(_GELU_C * (x + _GELU_A * (x * x * x))))


def _gelu_and_grad(x):
    t = jnp.tanh(_GELU_C * (x + _GELU_A * (x * x * x)))
    hx = 0.5 * x
    return hx * (1.0 + t), 0.5 * (1.0 + t) + hx * (1.0 - t * t) * (_GELU_C * (1.0 + 3.0 * _GELU_A * x * x))


def _sgu_fwd(zp, lng, lnb, ws, bsb, name):
    S = zp.shape[0]
    D = zp.shape[1] // 2
    G = ws.shape[0]
    gd = D // G
    ts = _pick(S, TS_SGU, CHUNK)
    ncs = ts // CHUNK

    def body(zp_ref, lng_ref, lnb_ref, ws_ref, bsb_ref, y_ref, vb_ref):
        v = _gelu(zp_ref[:, D:2 * D])
        mu = jnp.mean(v, axis=-1, keepdims=True)
        xc = v - mu
        rstd = lax.rsqrt(jnp.mean(xc * xc, axis=-1, keepdims=True) + LN_EPS)
        vb_ref[...] = (xc * rstd * lng_ref[...] + lnb_ref[...]).astype(BF16)
        for c in range(ncs):
            rows = slice(c * CHUNK, (c + 1) * CHUNK)
            for g in range(G):
                cols = slice(g * gd, (g + 1) * gd)
                sv = _dot(ws_ref[g], vb_ref[rows, cols]) + bsb_ref[:, cols]
                y_ref[rows, cols] = (_gelu(zp_ref[rows, cols]) * sv).astype(BF16)

    def full(a):
        return pl.BlockSpec(a.shape, lambda i: (0,) * a.ndim)

    return pl.pallas_call(
        body, grid=(S // ts,),
        in_specs=[pl.BlockSpec((ts, 2 * D), lambda i: (i, 0)), full(lng), full(lnb), full(ws), full(bsb)],
        out_specs=pl.BlockSpec((ts, D), lambda i: (i, 0)), out_shape=jax.ShapeDtypeStruct((S, D), BF16),
        scratch_shapes=[pltpu.VMEM((ts, D), BF16)],
        compiler_params=_cp("parallel"), name=name)(zp, lng, lnb, ws, bsb)


def _sgu_bwd(dy, zp, lng, lnb, ws, wst, bsb, name):
    S = zp.shape[0]
    D = zp.shape[1] // 2
    G = ws.shape[0]
    gd = D // G
    ts = _pick(S, TS_SGU, CHUNK)
    ncs = ts // CHUNK

    def body(dy_ref, zp_ref, lng_ref, lnb_ref, ws_ref, wst_ref, bsb_ref,
             dzp_ref, dws_ref, dbs_ref, dg_ref, db_ref, vb_ref, dvln_ref, acc_bs):
        i = pl.program_id(0)

        @pl.when(i == 0)
        def _():
            dws_ref[...] = jnp.zeros_like(dws_ref)
            acc_bs[...] = jnp.zeros_like(acc_bs)
            dg_ref[...] = jnp.zeros_like(dg_ref)
            db_ref[...] = jnp.zeros_like(db_ref)

        v, dv_dz = _gelu_and_grad(zp_ref[:, D:2 * D])
        mu = jnp.mean(v, axis=-1, keepdims=True)
        xc = v - mu
        rstd = lax.rsqrt(jnp.mean(xc * xc, axis=-1, keepdims=True) + LN_EPS)
        xh = xc * rstd
        vb_ref[...] = (xh * lng_ref[...] + lnb_ref[...]).astype(BF16)
        for c in range(ncs):
            rows = slice(c * CHUNK, (c + 1) * CHUNK)
            for g in range(G):
                cols = slice(g * gd, (g + 1) * gd)
                u, du_dz = _gelu_and_grad(zp_ref[rows, cols])
                dy_ = dy_ref[rows, cols].astype(F32)
                sv = _dot(ws_ref[g], vb_ref[rows, cols]) + bsb_ref[:, cols]
                dzp_ref[rows, cols] = (dy_ * sv * du_dz).astype(BF16)
                dsv = dy_ * u
                acc_bs[:, cols] += dsv
                dsvb = dsv.astype(BF16)
                dws_ref[g] += _dot_nt(dsvb, vb_ref[rows, cols])
                dvln_ref[rows, cols] = _dot(wst_ref[g], dsvb)
        dvln = dvln_ref[...]
        dg_ref[...] += jnp.sum(dvln * xh, axis=0, keepdims=True)
        db_ref[...] += jnp.sum(dvln, axis=0, keepdims=True)
        dxh = dvln * lng_ref[...]
        dv = rstd * (dxh - jnp.mean(dxh, axis=-1, keepdims=True) - xh * jnp.mean(dxh * xh, axis=-1, keepdims=True))
        dzp_ref[:, D:2 * D] = (dv * dv_dz).astype(BF16)

        @pl.when(i == pl.num_programs(0) - 1)
        def _():
            dbs_ref[...] = acc_bs[...]

    def full(shape):
        return pl.BlockSpec(shape, lambda i: (0,) * len(shape))

    return pl.pallas_call(
        body, grid=(S // ts,),
        in_specs=[pl.BlockSpec((ts, D), lambda i: (i, 0)), pl.BlockSpec((ts, 2 * D), lambda i: (i, 0)),
                  full(lng.shape), full(lnb.shape), full(ws.shape), full(wst.shape), full(bsb.shape)],
        out_specs=[pl.BlockSpec((ts, 2 * D), lambda i: (i, 0)), full(ws.shape), full(bsb.shape),
                   full((1, D)), full((1, D))],
        out_shape=[jax.ShapeDtypeStruct((S, 2 * D), BF16), jax.ShapeDtypeStruct(ws.shape, F32),
                   jax.ShapeDtypeStruct(bsb.shape, F32), jax.ShapeDtypeStruct((1, D), F32),
                   jax.ShapeDtypeStruct((1, D), F32)],
        scratch_shapes=[pltpu.VMEM((ts, D), BF16), pltpu.VMEM((ts, D), F32),
                        pltpu.VMEM(bsb.shape, F32)],
        compiler_params=_cp("arbitrary"), name=name)(dy, zp, lng, lnb, ws, wst, bsb)


def _group_sum(x, groups, name):
    P, D = x.shape
    gd = D // groups

    def body(x_ref, o_ref):
        for g in range(groups):
            o_ref[:, g:g + 1] = jnp.sum(x_ref[:, g * gd:(g + 1) * gd], axis=1, keepdims=True)

    return pl.pallas_call(body, out_shape=jax.ShapeDtypeStruct((P, groups), F32), name=name)(x)


def _adamw_small(ws, gs, ms, vs, name):
    n = len(ws)
    shapes = [w.shape for w in ws]
    flat = [(w.size // w.shape[-1], w.shape[-1]) for w in ws]

    def body(*refs):
        for k in range(n):
            w_ref, g_ref, m_ref, v_ref = (refs[j * n + k] for j in range(4))
            d_ref, nm_ref, nv_ref = (refs[(4 + j) * n + k] for j in range(3))
            d_ref[...], nm_ref[...], nv_ref[...] = _adamw_math(w_ref[...], g_ref[...], m_ref[...], v_ref[...])

    outs = pl.pallas_call(
        body, out_shape=[jax.ShapeDtypeStruct(f, F32) for f in flat] * 3, name=name)(
            *(a.reshape(f) for group in (ws, gs, ms, vs) for a, f in zip(group, flat)))
    return [tuple(outs[j * n + k].reshape(shapes[k]) for j in range(3)) for k in range(n)]


_HBM = pl.BlockSpec(memory_space=pltpu.HBM)


def _remote(src, dst, send_sem, recv_sem, to):
    return pltpu.make_async_remote_copy(src_ref=src, dst_ref=dst, send_sem=send_sem, recv_sem=recv_sem,
                                        device_id=to, device_id_type=MESH)


def _all_gather(block, name):
    R, C = block.shape

    def body(x_ref, out_ref, send_sems, recv_sems, local_sem):
        x, y, c = lax.axis_index("x"), lax.axis_index("y"), lax.axis_index("c")
        me, sibling = (x, y, c), (x, y, 1 - c)
        chips = [(1 - x, y), (x, 1 - y), (1 - x, 1 - y)]

        def slot(px, py, pc):
            return out_ref.at[4 * px + 2 * py + pc]

        def copy(k, blk, to, src=None):
            return _remote(slot(*blk) if src is None else src, slot(*blk), send_sems.at[k], recv_sems.at[k], to)

        mine = pltpu.make_async_copy(x_ref, slot(*me), local_sem)
        mine.start()
        first = [copy(0, me, sibling, src=x_ref)]
        first += [copy(1 + j, me, (*chip, c), src=x_ref) for j, chip in enumerate(chips)]
        for cp in first:
            cp.start()
        passed = [copy(4 + j, (*chip, c), sibling) for j, chip in enumerate(chips)]
        for j, chip in enumerate(chips):
            copy(1 + j, (*chip, c), me).wait_recv()
            passed[j].start()
        copy(0, sibling, me).wait_recv()
        for j, chip in enumerate(chips):
            copy(4 + j, (*chip, 1 - c), me).wait_recv()
        for cp in first + passed:
            cp.wait_send()
        mine.wait()

    return pl.pallas_call(
        body, out_shape=jax.ShapeDtypeStruct((NDEV, R, C), block.dtype), in_specs=[_HBM], out_specs=_HBM,
        scratch_shapes=[pltpu.SemaphoreType.DMA((7,)), pltpu.SemaphoreType.DMA((7,)), pltpu.SemaphoreType.DMA],
        name=name)(block)


def _all_gather_weights(pack, rows, name, after=None):
    C = pack.shape[1]
    nw = len(rows)
    starts = [sum(rows[:w]) for w in range(nw)]
    after_specs, after_ops = _after(after)

    def body(pack_ref, *rest):
        rest = rest[len(after_ops):]
        outs = rest[:nw]
        send_sems, recv_sems, local_sem = rest[nw:]
        x, y, c = lax.axis_index("x"), lax.axis_index("y"), lax.axis_index("c")
        me, sibling = (x, y, c), (x, y, 1 - c)
        chips = [(1 - x, y), (x, 1 - y), (1 - x, 1 - y)]

        def block(w, px, py, pc):
            return outs[w].at[pl.ds((4 * px + 2 * py + pc) * rows[w], rows[w])]

        def mine(w):
            return pack_ref.at[pl.ds(starts[w], rows[w])]

        def all_of(k):
            return _remote(pack_ref, pack_ref, send_sems.at[k], recv_sems.at[k], me)

        for w in range(nw):
            pltpu.make_async_copy(mine(w), block(w, *me), local_sem).start()
        for k, to in enumerate([sibling] + [(*chip, c) for chip in chips]):
            for w in range(nw):
                _remote(mine(w), block(w, *me), send_sems.at[k], recv_sems.at[k], to).start()
        for j, chip in enumerate(chips):
            all_of(1 + j).wait_recv()
            for w in range(nw):
                _remote(block(w, *chip, c), block(w, *chip, c), send_sems.at[4 + j], recv_sems.at[4 + j], sibling).start()
        all_of(0).wait_recv()
        for j in range(3):
            all_of(4 + j).wait_recv()
        for k in range(7):
            all_of(k).wait_send()
        pltpu.make_async_copy(pack_ref, pack_ref, local_sem).wait()

    return pl.pallas_call(
        body, out_shape=[jax.ShapeDtypeStruct((NDEV * r, C), pack.dtype) for r in rows],
        in_specs=[_HBM] + after_specs, out_specs=[_HBM] * nw,
        scratch_shapes=[pltpu.SemaphoreType.DMA((7,)), pltpu.SemaphoreType.DMA((7,)), pltpu.SemaphoreType.DMA],
        name=name)(pack, *after_ops)


_SEM = pl.BlockSpec(memory_space=pltpu.SEMAPHORE)
_DATAFLOW = pltpu.SideEffectType.DATAFLOW_SIDE_EFFECTING


def _split_start(srcs, lands, plan, n, after, name):
    nbuf = len(srcs) + len(lands)
    after_specs, after_ops = _after(after)

    def body(*refs):
        src_refs, land_refs = refs[:len(srcs)], refs[len(srcs):nbuf]
        send_sems, recv_sems = refs[nbuf + len(after_ops)], refs[nbuf + len(after_ops) + 1]
        for k, (src, dst, to) in enumerate(plan(src_refs, land_refs)):
            _remote(src, dst, send_sems.at[k], recv_sems.at[k], to).start()
        refs[-1][...] = jnp.zeros_like(refs[-1])

    bufs = [pltpu.with_memory_space_constraint(a, pltpu.HBM) for a in list(srcs) + list(lands)]
    outs = pl.pallas_call(
        body, name=name,
        out_shape=(pltpu.SemaphoreType.DMA((n,)), pltpu.SemaphoreType.DMA((n,)),
                   *[pltpu.HBM(a.shape, a.dtype) for a in bufs], jax.ShapeDtypeStruct((8, 128), F32)),
        in_specs=[_HBM] * nbuf + after_specs,
        out_specs=(_SEM, _SEM, *[_HBM] * nbuf, pl.BlockSpec(memory_space=pltpu.VMEM)),
        input_output_aliases={i: 2 + i for i in range(nbuf)},
        compiler_params=pltpu.CompilerParams(has_side_effects=_DATAFLOW))(*bufs, *after_ops)
    return outs[0], outs[1], list(outs[2:2 + len(srcs)]), list(outs[2 + len(srcs):2 + nbuf]), outs[-1]


def _split_wait(send_sems, recv_sems, srcs, lands, plan, after, name):
    nbuf = len(srcs) + len(lands)
    after_specs, after_ops = _after(after)

    def body(*refs):
        src_refs, land_refs = refs[:len(srcs)], refs[len(srcs):nbuf]
        send_sems_ref, recv_sems_ref = refs[nbuf], refs[nbuf + 1]
        for k, (src, dst, to) in enumerate(plan(src_refs, land_refs)):
            copy = _remote(src, dst, send_sems_ref.at[k], recv_sems_ref.at[k], to)
            copy.wait_send()
            copy.wait_recv()

    outs = pl.pallas_call(
        body, name=name, out_shape=tuple(pltpu.HBM(a.shape, a.dtype) for a in list(srcs) + list(lands)),
        in_specs=[_HBM] * nbuf + [_SEM, _SEM] + after_specs, out_specs=tuple([_HBM] * nbuf),
        input_output_aliases={i: i for i in range(nbuf)},
        compiler_params=pltpu.CompilerParams(has_side_effects=_DATAFLOW))(*srcs, *lands, send_sems, recv_sems, *after_ops)
    return list(outs[:len(srcs)]), list(outs[len(srcs):])


def _peers(x, y, c):
    return [(mask, (1 - x if mask & 4 else x, 1 - y if mask & 2 else y, 1 - c if mask & 1 else c))
            for mask in range(1, NDEV)]


def _gather_plan(rows):
    starts = [sum(rows[:w]) for w in range(len(rows))]

    def plan(src_refs, land_refs):
        x, y, c = lax.axis_index("x"), lax.axis_index("y"), lax.axis_index("c")
        copies = []
        for w, r in enumerate(rows):
            mine = src_refs[0].at[pl.ds(starts[w], r)]
            dst = land_refs[w].at[pl.ds((4 * x + 2 * y + c) * r, r)]
            copies += [(mine, dst, peer) for _, peer in _peers(x, y, c)]
        return copies

    return plan, (NDEV - 1) * len(rows)


def _place_own(shards, fulls, dev_idx, name):
    nw = len(shards)

    def body(i_ref, *refs):
        for w in range(nw):
            refs[2 * nw + w][...] = refs[w][...]

    grid_spec = pltpu.PrefetchScalarGridSpec(
        num_scalar_prefetch=1, grid=(1,),
        in_specs=[pl.BlockSpec(s.shape, lambda t, i_ref: (0, 0)) for s in shards] + [_ANY] * nw,
        out_specs=[pl.BlockSpec(s.shape, lambda t, i_ref: (i_ref[0], 0)) for s in shards])
    outs = pl.pallas_call(
        body, grid_spec=grid_spec, out_shape=[jax.ShapeDtypeStruct(f.shape, f.dtype) for f in fulls],
        input_output_aliases={1 + nw + w: w for w in range(nw)}, name=name)(dev_idx, *shards, *fulls)
    return list(outs)


def _scatter_plan(rows):
    def plan(src_refs, land_refs):
        x, y, c = lax.axis_index("x"), lax.axis_index("y"), lax.axis_index("c")
        copies = []
        for w, r in enumerate(rows):
            for mask, (px, py, pc) in _peers(x, y, c):
                src = src_refs[w].at[pl.ds((4 * px + 2 * py + pc) * r, r)]
                copies.append((src, land_refs[w].at[mask - 1], (px, py, pc)))
        return copies

    return plan, (NDEV - 1) * len(rows)


def _adamw_math(w, g, m, v):
    nm = ADAM_B1 * m + (1.0 - ADAM_B1) * g
    nv = ADAM_B2 * v + (1.0 - ADAM_B2) * (g * g)
    bc1 = 1.0 - ADAM_B1 ** ADAM_STEP
    bc2 = 1.0 - ADAM_B2 ** ADAM_STEP
    return -ADAM_LR * ((nm / bc1) / (jnp.sqrt(nv / bc2) + ADAM_EPS) + ADAM_WD * w), nm, nv


def _finish_weight(gs, gots, dev_idx, w, m, v, name, after=None):
    L = len(gs)
    n1, r, C = gots[0].shape
    block = (None,) + w.shape[1:]

    after_specs, after_ops = _after(after)

    def body(i_ref, *refs):
        ins, (w_ref, m_ref, v_ref), (g_out, d_out, m_out, v_out) = refs[:2 * L], refs[2 * L:2 * L + 3], refs[-4:]
        for layer in range(L):
            @pl.when(pl.program_id(0) == layer)
            def _():
                g_ref, got_ref = ins[2 * layer], ins[2 * layer + 1]
                acc = g_ref[...].astype(F32)
                for k in range(n1):
                    acc = acc + got_ref[k].astype(F32)
                g_out[...] = acc
                d_out[...], m_out[...], v_out[...] = _adamw_math(w_ref[...], acc, m_ref[...], v_ref[...])

    in_specs, ins = [], []
    for g, got in zip(gs, gots):
        in_specs += [pl.BlockSpec((r, C), lambda t, i_ref: (i_ref[0], 0), pipeline_mode=_RESIDENT),
                     pl.BlockSpec((n1, r, C), lambda t, i_ref: (0, 0, 0), pipeline_mode=_RESIDENT)]
        ins += [g, got]
    per_layer = pl.BlockSpec(block, lambda t, i_ref: (t, 0, 0))
    grid_spec = pltpu.PrefetchScalarGridSpec(
        num_scalar_prefetch=1, grid=(L,), in_specs=in_specs + [per_layer] * 3 + after_specs,
        out_specs=[per_layer] * 4)
    return pl.pallas_call(
        body, grid_spec=grid_spec, out_shape=[jax.ShapeDtypeStruct(w.shape, F32)] * 4,
        compiler_params=_cp("arbitrary"), name=name)(dev_idx, *ins, w, m, v, *after_ops)


def _sum_slots(a, name):
    n, R, C = a.shape

    def body(a_ref, o_ref):
        acc = a_ref[0]
        for k in range(1, n):
            acc = acc + a_ref[k]
        o_ref[...] = acc

    return pl.pallas_call(body, out_shape=jax.ShapeDtypeStruct((R, C), F32), name=name)(a)


def _shard_axis(name):
    return {"ev_w_in": 2, "ev_a_conv_w": 2, "ev_b_conv_w": 2, "ev_w_out": 1, "od_w_in": 2, "od_c_ln_g": 1,
            "od_c_ln_b": 1, "od_w_out": 1, "xa_w_q": 1, "xa_w_k": 1, "xa_w_v": 1, "xa_w_o": 1,
            "ffn_w_gate": 2, "ffn_w_up": 2, "ffn_w_down": 1}[name]


BIG = ["ev_w_in", "ev_w_out", "od_w_in", "od_w_out", "xa_w_q", "xa_w_k", "xa_w_v", "xa_w_o",
       "ffn_w_gate", "ffn_w_up", "ffn_w_down"]
SMALL_SHARDED = ["ev_a_conv_w", "ev_b_conv_w", "od_c_ln_g", "od_c_ln_b"]
REPLICATED = ["g_mix", "g_xattn", "g_mem", "g_ffn", "g_final", "ev_a_conv_b", "ev_a_ln_g", "ev_a_ln_b",
              "ev_b_conv_b", "od_w_s", "od_b_s"]
WEIGHTS = ["g_mix", "g_xattn", "g_mem", "g_ffn", "g_final", "ev_w_in", "ev_a_conv_w", "ev_a_conv_b", "ev_a_ln_g",
           "ev_a_ln_b", "ev_b_conv_w", "ev_b_conv_b", "ev_w_out", "od_w_in", "od_c_ln_g", "od_c_ln_b", "od_w_s",
           "od_b_s", "od_w_out", "xa_w_q", "xa_w_k", "xa_w_v", "xa_w_o", "ffn_w_gate", "ffn_w_up", "ffn_w_down"]


def _full_from_blocks(blocks, axis):
    shard = blocks.shape[1:]
    full = jnp.moveaxis(blocks, 0, axis)
    return full.reshape(shard[:axis] + (NDEV * shard[axis],) + shard[axis + 1:])


def _blocks_from_full(full, axis):
    shp = full.shape
    split = full.reshape(shp[:axis] + (NDEV, shp[axis] // NDEV) + shp[axis + 1:])
    return jnp.moveaxis(split, axis, 0)


def _pad_rows(flat, width, row_align):
    per = width * row_align
    n = -(-flat.shape[0] // per) * per
    return jnp.pad(flat, (0, n - flat.shape[0])).reshape(n // width, width)


def _row(v):
    return v.reshape(1, -1)


def _xattn_fwd(h, nq, mem, g_m, wq, wk, wv, wo, g_next, tag, after):
    mem_n = _rms_fwd(mem, _row(g_m), f"xa_mem_rms_{tag}")
    q = _mm([(nq, wq, "nn")], f"xa_q_{tag}", out_dtype=BF16, after=after)
    k = _mm([(mem_n, wk, "nn")], f"xa_k_{tag}", out_dtype=BF16)
    v = _mm([(mem_n, wv, "nn")], f"xa_v_{tag}", out_dtype=BF16)
    o = _attn_fwd(q, k, v, f"xa_attn_{tag}")
    h_new, n_next = _mm([(o, wo, "nn")], f"xa_o_{tag}", res=h, rms_g=_row(g_next))
    return h_new, n_next, (h, nq, mem_n, q, k, v, o)


def _xattn_bwd(dh_new, saved, mem, g_x, g_m, wq, wk, wv, wo, tag, push):
    h, nq, mem_n, q, k, v, o = saved
    do = _mm([(dh_new, wo, "nt")], f"xa_do_{tag}", out_dtype=BF16)
    d_wo = _mm_tn(o, dh_new, f"xa_dwo_{tag}")
    dq, dk, dv = _attn_bwd(q, k, v, do, f"xa_attn_bwd_{tag}")
    d_wq = _mm_tn(nq, dq, f"xa_dwq_{tag}")
    d_wk = _mm_tn(mem_n, dk, f"xa_dwk_{tag}")
    d_wv = _mm_tn(mem_n, dv, f"xa_dwv_{tag}")
    token = push([d_wq, d_wk, d_wv, d_wo])
    dmem_n = _mm([(dk, wk, "nt"), (dv, wv, "nt")], f"xa_dmem_{tag}", after=token)
    _, d_gm = _rms_bwd(dmem_n, mem, _row(g_m), None, f"xa_mem_rms_bwd_{tag}")
    dh, d_gx = _mm([(dq, wq, "nt")], f"xa_dnq_{tag}", rms_bwd=(h, _row(g_x), dh_new), tm=1024, after=token)
    return dh, dict(g_xattn=d_gx, g_mem=d_gm)


def _ffn_fwd(h, n, wgt, wut, wd, g_next, tag, after):
    a, b, hid = _ffn_up(n, wgt, wut, f"ffn_up_{tag}", after=after)
    if g_next is None:
        h_new, n_next = _mm([(hid, wd, "nn")], f"ffn_down_{tag}", res=h, tm=1024, tn=1024), None
    else:
        h_new, n_next = _mm([(hid, wd, "nn")], f"ffn_down_{tag}", res=h, rms_g=_row(g_next), tm=1024)
    return h_new, n_next, (h, n, a, b, hid)


def _ffn_bwd(dh_new, saved, g_f, wgt, wut, wd, tag, push):
    h, n, a, b, hid = saved
    da, db = _ffn_dhid(dh_new, wd, a, b, f"ffn_dhid_{tag}")
    d_wd = _mm_tn(hid, dh_new, f"ffn_dwd_{tag}", ts=1024, tn=1024)
    d_wgt = _mm_tn(da, n, f"ffn_dwg_{tag}", ts=1024, tn=1024)
    d_wut = _mm_tn(db, n, f"ffn_dwu_{tag}", ts=1024, tn=1024)
    token = push([d_wgt, d_wut, d_wd])
    dh, d_gf = _mm([(da, wgt, "nn"), (db, wut, "nn")], f"ffn_dn_{tag}", rms_bwd=(h, _row(g_f), dh_new), tm=512,
                   after=token)
    return dh, dict(g_ffn=d_gf)


_XA = ["xa_w_q", "xa_w_k", "xa_w_v", "xa_w_o"]
_FFN = ["ffn_w_gate", "ffn_w_up", "ffn_w_down"]
GATHERS = {
    "ev_in": [("ev_w_in", 0)],
    "xa0": [("ev_w_out", 0)] + [(n, 0) for n in _XA],
    "ffn0": [(n, 0) for n in _FFN],
    "od": [("od_w_in", 0), ("od_w_out", 0)],
    "xa1": [(n, 1) for n in _XA],
    "ffn1": [(n, 1) for n in _FFN],
}
SCATTERS = {
    "ffn1": [(n, 1) for n in _FFN],
    "xa1": [(n, 1) for n in _XA],
    "od": [("od_w_in", 0), ("od_w_out", 0)],
    "ffn0": [(n, 0) for n in _FFN],
    "xa0": [(n, 0) for n in _XA],
    "ev_out": [("ev_w_out", 0)],
    "ev_in": [("ev_w_in", 0)],
}


def _local_step(x, mem, loss_target, W, comm):
    grads = {}

    h0 = x
    (ev_w_in_t,), token = comm.weights("ev_in", None)
    n0 = _rms_fwd(h0, _row(W["g_mix"][0]), "ev_rms", after=token)
    z = _mm([(n0, ev_w_in_t, "nt")], "ev_in", tn=1280)
    token = comm.prefetch(["ffn0"], z)
    ab, ca = _conv_fwd(z, W["ev_a_conv_w"][0], W["ev_a_conv_b"], W["ev_a_ln_g"], W["ev_a_ln_b"],
                       W["ev_b_conv_w"][0], W["ev_b_conv_b"], "ev_conv", after=token)
    (ev_w_out, *xa_w0), _ = comm.weights("xa0", ab)
    h1, nq0 = _mm([(ab, ev_w_out, "nn")], "ev_out", res=h0, rms_g=_row(W["g_xattn"][0]))
    token = comm.prefetch(["od", "xa1"], nq0)
    h2, nf0, xa0 = _xattn_fwd(h1, nq0, mem, W["g_mem"][0], *xa_w0, W["g_ffn"][0], "l0", token)
    ffn_w0, _ = comm.weights("ffn0", nf0)
    token = comm.prefetch(["ffn1"], nf0)
    h3, n3, ff0 = _ffn_fwd(h2, nf0, *ffn_w0, W["g_mix"][1], "l0", token)

    (od_w_in_t, od_w_out), _ = comm.weights("od", n3)
    zp = _mm([(n3, od_w_in_t, "nt")], "od_in", tn=1024)
    D = x.shape[1]
    ws = W["od_w_s"][0].astype(BF16)
    wst = jnp.swapaxes(ws, 1, 2)
    bsb = jnp.repeat(jnp.transpose(W["od_b_s"][0]), D // C_GROUPS, axis=1)
    y_sgu = _sgu_fwd(zp, W["od_c_ln_g"], W["od_c_ln_b"], ws, bsb, "od_sgu")
    h4, nq1 = _mm([(y_sgu, od_w_out, "nn")], "od_out", res=h3, rms_g=_row(W["g_xattn"][1]))
    xa_w1, _ = comm.weights("xa1", nq1)
    h5, nf1, xa1 = _xattn_fwd(h4, nq1, mem, W["g_mem"][1], *xa_w1, W["g_ffn"][1], "l1", None)
    ffn_w1, _ = comm.weights("ffn1", nf1)
    h6, _, ff1 = _ffn_fwd(h5, nf1, *ffn_w1, None, "l1", None)

    loss_row, dh6, d_gfinal = _loss_bwd(h6, _row(W["g_final"]), loss_target, "loss")
    grads["g_final"] = d_gfinal.reshape(-1)

    dh5, g_ff1 = _ffn_bwd(dh6, ff1, W["g_ffn"][1], *ffn_w1, "l1", lambda dws: comm.grads("ffn1", dws))
    dh4, g_xa1 = _xattn_bwd(dh5, xa1, mem, W["g_xattn"][1], W["g_mem"][1], *xa_w1, "l1",
                            lambda dws: comm.grads("xa1", dws))
    dy_sgu = _mm([(dh4, od_w_out, "nt")], "od_dy", tn=1024)
    d_od_out = _mm_tn(y_sgu, dh4, "od_dwout", tn=1024)
    dzp, d_ws, d_bsb, d_clng, d_clnb = _sgu_bwd(dy_sgu, zp, W["od_c_ln_g"], W["od_c_ln_b"], ws, wst, bsb, "od_sgu_bwd")
    grads["od_w_s"] = d_ws[None]
    grads["od_b_s"] = jnp.transpose(_group_sum(d_bsb, C_GROUPS, "od_dbs"))[None]
    grads["od_c_ln_g"], grads["od_c_ln_b"] = d_clng, d_clnb
    token = comm.grads("od", [_mm_tn(dzp, n3, "od_dwin", ts=1024, tn=1024), d_od_out])
    dh3, d_gmix1 = _mm([(dzp, od_w_in_t, "nn")], "od_dn", rms_bwd=(h3, _row(W["g_mix"][1]), dh4), tm=1024, after=token)

    dh2, g_ff0 = _ffn_bwd(dh3, ff0, W["g_ffn"][0], *ffn_w0, "l0", lambda dws: comm.grads("ffn0", dws))
    dh1, g_xa0 = _xattn_bwd(dh2, xa0, mem, W["g_xattn"][0], W["g_mem"][0], *xa_w0, "l0",
                            lambda dws: comm.grads("xa0", dws))
    token = comm.grads("ev_out", [_mm_tn(ab, dh1, "ev_dwout", tn=1024)])
    dab = _mm([(dh1, ev_w_out, "nt")], "ev_dab", tn=1024, after=token)
    dca, d_lng, d_lnb, d_ba = _conv_bwd_ln(dab, ca, W["ev_a_ln_g"], W["ev_a_ln_b"], "ev_conv_bwd_ln")
    dz, d_wa, d_wb, d_bb = _conv_bwd(z, dca, dab, W["ev_a_conv_w"][0], W["ev_b_conv_w"][0], W["ev_b_conv_b"],
                                     "ev_conv_bwd")
    grads.update(ev_a_ln_g=d_lng, ev_a_ln_b=d_lnb, ev_a_conv_b=d_ba, ev_b_conv_b=d_bb,
                 ev_a_conv_w=d_wa[None], ev_b_conv_w=d_wb[None])
    token = comm.grads("ev_in", [_mm_tn(dz, n0, "ev_dwin", ts=1024, tn=1024)])
    grad_x, d_gmix0 = _mm([(dz, ev_w_in_t, "nn")], "ev_dn", rms_bwd=(h0, _row(W["g_mix"][0]), dh1), tm=1024, after=token)

    grads["g_mix"] = jnp.concatenate([d_gmix0, d_gmix1], axis=0)
    for key in ("g_xattn", "g_mem"):
        grads[key] = jnp.concatenate([g_xa0[key], g_xa1[key]], axis=0)
    grads["g_ffn"] = jnp.concatenate([g_ff0["g_ffn"], g_ff1["g_ffn"]], axis=0)
    return loss_row, grad_x, grads


class _Exchanges:
    def __init__(self, shards, dev_idx, after):
        self.shards, self.dev_idx = shards, dev_idx
        self.gathering, self.scattering = {}, {}
        self.first = _all_gather_weights(self._pack(GATHERS["ev_in"]), self._rows(GATHERS["ev_in"]), "ag_ev_in",
                                         after=after)
        self.first_token = self.prefetch(["xa0"], self.first[0])

    def _rows(self, entries):
        return [self.shards[e].shape[0] for e in entries]

    def _pack(self, entries):
        return jnp.concatenate([self.shards[e] for e in entries], axis=0)

    def prefetch(self, gathers, after):
        for name in gathers:
            rows = self._rows(GATHERS[name])
            pack = self._pack(GATHERS[name])
            lands = [lax.empty((NDEV * r, pack.shape[1]), pack.dtype) for r in rows]
            plan, n = _gather_plan(rows)
            send, recv, srcs, lands, after = _split_start([pack], lands, plan, n, after, f"ag_{name}_start")
            self.gathering[name] = (send, recv, srcs, lands, plan, rows)
        return after

    def weights(self, name, after):
        if name == "ev_in":
            return self.first, self.first_token
        send, recv, srcs, lands, plan, rows = self.gathering.pop(name)
        _, lands = _split_wait(send, recv, srcs, lands, plan, after, f"ag_{name}_wait")
        return _place_own([self.shards[e] for e in GATHERS[name]], lands, self.dev_idx, f"ag_{name}_own"), None

    def grads(self, name, dws):
        rows = self._rows(SCATTERS[name])
        lands = [lax.empty((NDEV - 1, r, d.shape[1]), d.dtype) for r, d in zip(rows, dws)]
        plan, n = _scatter_plan(rows)
        send, recv, srcs, lands, token = _split_start(dws, lands, plan, n, None, f"rs_{name}_start")
        self.scattering[name] = (send, recv, srcs, lands, plan)
        return token

    def received(self, after):
        out = {}
        for name, (send, recv, srcs, lands, plan) in self.scattering.items():
            srcs, lands = _split_wait(send, recv, srcs, lands, plan, after, f"rs_{name}_wait")
            for entry, g, got in zip(SCATTERS[name], srcs, lands):
                out[entry] = (g, got)
        return out


def kernel(x, mem, g_mix, g_xattn, g_mem, g_ffn, g_final, ev_w_in, ev_a_conv_w, ev_a_conv_b, ev_a_ln_g, ev_a_ln_b, ev_b_conv_w, ev_b_conv_b, ev_w_out, od_w_in, od_c_ln_g, od_c_ln_b, od_w_s, od_b_s, od_w_out, xa_w_q, xa_w_k, xa_w_v, xa_w_o, ffn_w_gate, ffn_w_up, ffn_w_down, loss_target, m_g_mix, m_g_xattn, m_g_mem, m_g_ffn, m_g_final, m_ev_w_in, m_ev_a_conv_w, m_ev_a_conv_b, m_ev_a_ln_g, m_ev_a_ln_b, m_ev_b_conv_w, m_ev_b_conv_b, m_ev_w_out, m_od_w_in, m_od_c_ln_g, m_od_c_ln_b, m_od_w_s, m_od_b_s, m_od_w_out, m_xa_w_q, m_xa_w_k, m_xa_w_v, m_xa_w_o, m_ffn_w_gate, m_ffn_w_up, m_ffn_w_down, v_g_mix, v_g_xattn, v_g_mem, v_g_ffn, v_g_final, v_ev_w_in, v_ev_a_conv_w, v_ev_a_conv_b, v_ev_a_ln_g, v_ev_a_ln_b, v_ev_b_conv_w, v_ev_b_conv_b, v_ev_w_out, v_od_w_in, v_od_c_ln_g, v_od_c_ln_b, v_od_w_s, v_od_b_s, v_od_w_out, v_xa_w_q, v_xa_w_k, v_xa_w_v, v_xa_w_o, v_ffn_w_gate, v_ffn_w_up, v_ffn_w_down):
    local = dict(g_mix=g_mix, g_xattn=g_xattn, g_mem=g_mem, g_ffn=g_ffn, g_final=g_final, ev_w_in=ev_w_in, ev_a_conv_w=ev_a_conv_w, ev_a_conv_b=ev_a_conv_b, ev_a_ln_g=ev_a_ln_g, ev_a_ln_b=ev_a_ln_b, ev_b_conv_w=ev_b_conv_w, ev_b_conv_b=ev_b_conv_b, ev_w_out=ev_w_out, od_w_in=od_w_in, od_c_ln_g=od_c_ln_g, od_c_ln_b=od_c_ln_b, od_w_s=od_w_s, od_b_s=od_b_s, od_w_out=od_w_out, xa_w_q=xa_w_q, xa_w_k=xa_w_k, xa_w_v=xa_w_v, xa_w_o=xa_w_o, ffn_w_gate=ffn_w_gate, ffn_w_up=ffn_w_up, ffn_w_down=ffn_w_down)
    mom = dict(g_mix=m_g_mix, g_xattn=m_g_xattn, g_mem=m_g_mem, g_ffn=m_g_ffn, g_final=m_g_final, ev_w_in=m_ev_w_in, ev_a_conv_w=m_ev_a_conv_w, ev_a_conv_b=m_ev_a_conv_b, ev_a_ln_g=m_ev_a_ln_g, ev_a_ln_b=m_ev_a_ln_b, ev_b_conv_w=m_ev_b_conv_w, ev_b_conv_b=m_ev_b_conv_b, ev_w_out=m_ev_w_out, od_w_in=m_od_w_in, od_c_ln_g=m_od_c_ln_g, od_c_ln_b=m_od_c_ln_b, od_w_s=m_od_w_s, od_b_s=m_od_b_s, od_w_out=m_od_w_out, xa_w_q=m_xa_w_q, xa_w_k=m_xa_w_k, xa_w_v=m_xa_w_v, xa_w_o=m_xa_w_o, ffn_w_gate=m_ffn_w_gate, ffn_w_up=m_ffn_w_up, ffn_w_down=m_ffn_w_down)
    vel = dict(g_mix=v_g_mix, g_xattn=v_g_xattn, g_mem=v_g_mem, g_ffn=v_g_ffn, g_final=v_g_final, ev_w_in=v_ev_w_in, ev_a_conv_w=v_ev_a_conv_w, ev_a_conv_b=v_ev_a_conv_b, ev_a_ln_g=v_ev_a_ln_g, ev_a_ln_b=v_ev_a_ln_b, ev_b_conv_w=v_ev_b_conv_w, ev_b_conv_b=v_ev_b_conv_b, ev_w_out=v_ev_w_out, od_w_in=v_od_w_in, od_c_ln_g=v_od_c_ln_g, od_c_ln_b=v_od_c_ln_b, od_w_s=v_od_w_s, od_b_s=v_od_b_s, od_w_out=v_od_w_out, xa_w_q=v_xa_w_q, xa_w_k=v_xa_w_k, xa_w_v=v_xa_w_v, xa_w_o=v_xa_w_o, ffn_w_gate=v_ffn_w_gate, ffn_w_up=v_ffn_w_up, ffn_w_down=v_ffn_w_down)
    D = x.shape[-1]
    dev = 4 * lax.axis_index("x") + 2 * lax.axis_index("y") + lax.axis_index("c")

    def comm_layout(n, a):
        return jnp.transpose(a) if _shard_axis(n) == 2 else a

    shards = {(n, i): comm_layout(n, local[n][i]).astype(BF16) for n in BIG for i in range(local[n].shape[0])}
    small_sizes = [local[n].size for n in SMALL_SHARDED]
    small_block = _pad_rows(jnp.concatenate([local[n].reshape(-1) for n in SMALL_SHARDED]), 128, 8)
    small_all = _all_gather(small_block, "ag_small")
    comm = _Exchanges(shards, jnp.reshape(dev, (1,)).astype(jnp.int32), small_all)
    small_all = small_all.reshape(NDEV, -1)

    W = {n: local[n] for n in REPLICATED}
    o0 = 0
    for n, sz in zip(SMALL_SHARDED, small_sizes):
        blocks = small_all[:, o0:o0 + sz].reshape((NDEV,) + local[n].shape)
        W[n] = _full_from_blocks(blocks, _shard_axis(n))
        o0 += sz

    loss_row, grad_x, grads = _local_step(x[0], mem[0], loss_target[0], W, comm)

    received = comm.received(grad_x)
    rest = REPLICATED + SMALL_SHARDED
    rest_full_shapes = [grads[n].shape for n in rest]
    g_rest = _pad_rows(jnp.concatenate([grads[n].astype(F32).reshape(-1) for n in rest]), D, 8)
    small_rows = g_rest.shape[0]
    small_plan, small_n = _gather_plan([small_rows])
    small_send, small_recv, small_srcs, small_lands, token = _split_start(
        [g_rest], [lax.empty((NDEV * small_rows, D), F32)], small_plan, small_n, received["ev_w_in", 0][1],
        "ag_small_grads_start")

    gsh, delta, new_m, new_v = {}, {}, {}, {}
    def stacked_layout(n, a):
        return jnp.swapaxes(a, 1, 2) if _shard_axis(n) == 2 else a

    for n in BIG:
        parts = [received[n, i] for i in range(local[n].shape[0])]
        outs = _finish_weight([p[0] for p in parts], [p[1] for p in parts], comm.dev_idx,
                              *(stacked_layout(n, a) for a in (local[n], mom[n], vel[n])), f"finish_{n}", after=token)
        gsh[n], delta[n], new_m[n], new_v[n] = (stacked_layout(n, o) for o in outs)

    _, small_lands = _split_wait(small_send, small_recv, small_srcs, small_lands, small_plan, delta[BIG[-1]],
                                 "ag_small_grads_wait")
    partials = _place_own([g_rest], small_lands, comm.dev_idx, "ag_small_grads_own")[0]
    g_rest = _sum_slots(partials.reshape(NDEV, small_rows, D), "sum_small_grads").reshape(-1)
    o0 = 0
    for n, shp in zip(rest, rest_full_shapes):
        sz = 1
        for s in shp:
            sz *= s
        full = g_rest[o0:o0 + sz].reshape(shp)
        o0 += sz
        if n in SMALL_SHARDED:
            full = lax.dynamic_index_in_dim(_blocks_from_full(full, _shard_axis(n)), dev, 0, keepdims=False)
        gsh[n] = full.reshape(local[n].shape)

    small = _adamw_small([local[n] for n in rest], [gsh[n] for n in rest], [mom[n] for n in rest],
                         [vel[n] for n in rest], "adamw_small")
    for n, (d, nm, nv) in zip(rest, small):
        delta[n], new_m[n], new_v[n] = d, nm, nv

    loss = lax.psum(loss_row[0, 0], ("x", "y", "c"))
    return (loss, grad_x[None], *[gsh[n] for n in WEIGHTS], *[delta[n] for n in WEIGHTS],
            *[new_m[n] for n in WEIGHTS], *[new_v[n] for n in WEIGHTS])
```

```python
import jax
import jax.numpy as jnp
from jax import lax
from jax.experimental import pallas as pl
from jax.experimental.pallas import tpu as pltpu

F32, BF16 = jnp.float32, jnp.bfloat16
NDEV = 8
RMS_EPS = 1e-6
LN_EPS = 1e-5
CHUNK = 128
C_GROUPS = 8
XA_HEADS = 4
ADAM_LR, ADAM_B1, ADAM_B2, ADAM_EPS, ADAM_WD, ADAM_STEP = 0.001, 0.9, 0.999, 1e-08, 0.01, 10
HALO = 16
ROW_CHUNK = 32
V7X_VMEM_LIMIT = 56 * 1024 * 1024
MESH = pl.DeviceIdType.MESH

TS_ROW = 512
TS_MM = 2048
TN_MM = 1408
TS_FFN = 512
MM_ROW_CHUNK = 256
TS_CONV = 512
TS_SGU = 512
TS_ATTN = 2048


def _cp(*sem):
    return pltpu.CompilerParams(dimension_semantics=sem, vmem_limit_bytes=V7X_VMEM_LIMIT)


def _pick(n, pref, align):
    for t in range(min(n, pref), 0, -1):
        if n % t == 0 and (t % align == 0 or t == n):
            return t
    return n


def _sigmoid(x):
    return 0.5 * jnp.tanh(0.5 * x) + 0.5


def _dot(a, b):
    return jnp.dot(a, b, preferred_element_type=F32)


def _dot_nt(a, b):
    return lax.dot_general(a, b, (((1,), (1,)), ((), ())), preferred_element_type=F32)


def _dot_tn(a, b):
    return lax.dot_general(a, b, (((0,), (0,)), ((), ())), preferred_element_type=F32)


_ANY = pl.BlockSpec(memory_space=pl.ANY)
_RESIDENT = pl.Buffered(1)


def _after(after):
    if after is None:
        return [], []
    ops = list(after) if isinstance(after, (list, tuple)) else [after]
    return [_ANY] * len(ops), ops


def _rms_fwd(h, g, name, after=None):
    S, D = h.shape
    ts = _pick(S, TS_MM, 16)
    after_specs, after_ops = _after(after)

    def body(h_ref, g_ref, *rest):
        o_ref = rest[-1]
        x = h_ref[...]
        r = lax.rsqrt(jnp.mean(x * x, axis=-1, keepdims=True) + RMS_EPS)
        o_ref[...] = ((x * r) * g_ref[...]).astype(o_ref.dtype)

    return pl.pallas_call(
        body, grid=(S // ts,),
        in_specs=[pl.BlockSpec((ts, D), lambda i: (i, 0)), pl.BlockSpec((1, D), lambda i: (0, 0))] + after_specs,
        out_specs=pl.BlockSpec((ts, D), lambda i: (i, 0)),
        out_shape=jax.ShapeDtypeStruct((S, D), BF16), compiler_params=_cp("parallel"), name=name)(h, g, *after_ops)


def _rms_bwd(dn, h, g, dres, name):
    S, D = h.shape
    ts = _pick(S, TS_ROW, 8)
    has_res = dres is not None

    def body(*refs):
        if has_res:
            dn_ref, h_ref, g_ref, dres_ref, dh_ref, dg_ref = refs
        else:
            dn_ref, h_ref, g_ref, dh_ref, dg_ref = refs
        x = h_ref[...]
        dn_ = dn_ref[...].astype(F32)
        r = lax.rsqrt(jnp.mean(x * x, axis=-1, keepdims=True) + RMS_EPS)
        xr = x * r

        @pl.when(pl.program_id(0) == 0)
        def _():
            dg_ref[...] = jnp.zeros_like(dg_ref)

        dg_ref[...] += jnp.sum(dn_ * xr, axis=0, keepdims=True)
        u = dn_ * g_ref[...]
        dh = r * u - xr * (r * jnp.mean(u * xr, axis=-1, keepdims=True))
        if has_res:
            dh = dh + dres_ref[...]
        dh_ref[...] = dh

    tile = pl.BlockSpec((ts, D), lambda i: (i, 0))
    vec = pl.BlockSpec((1, D), lambda i: (0, 0))
    ins = [dn, h, g] + ([dres] if has_res else [])
    return pl.pallas_call(
        body, grid=(S // ts,),
        in_specs=[tile, tile, vec] + ([tile] if has_res else []),
        out_specs=[tile, vec],
        out_shape=[jax.ShapeDtypeStruct((S, D), F32), jax.ShapeDtypeStruct((1, D), F32)],
        compiler_params=_cp("arbitrary"), name=name)(*ins)


def _loss_bwd(h, g, target, name):
    S, D = h.shape
    ts = _pick(S, TS_ROW, 8)

    def body(h_ref, g_ref, t_ref, loss_ref, dh_ref, dg_ref):
        x = h_ref[...]
        r = lax.rsqrt(jnp.mean(x * x, axis=-1, keepdims=True) + RMS_EPS)
        xr = x * r
        gg = g_ref[...]
        e = xr * gg - t_ref[...]

        @pl.when(pl.program_id(0) == 0)
        def _():
            dg_ref[...] = jnp.zeros_like(dg_ref)
            loss_ref[...] = jnp.zeros_like(loss_ref)

        tile_loss = jnp.sum(jnp.sum(e * e, axis=0, keepdims=True), axis=1, keepdims=True) * (0.5 / D)
        loss_ref[...] += jnp.broadcast_to(tile_loss, loss_ref.shape)
        dy = e * (1.0 / D)
        dg_ref[...] += jnp.sum(dy * xr, axis=0, keepdims=True)
        u = dy * gg
        dh_ref[...] = r * u - xr * (r * jnp.mean(u * xr, axis=-1, keepdims=True))

    tile = pl.BlockSpec((ts, D), lambda i: (i, 0))
    vec = pl.BlockSpec((1, D), lambda i: (0, 0))
    return pl.pallas_call(
        body, grid=(S // ts,),
        in_specs=[tile, vec, tile],
        out_specs=[pl.BlockSpec((1, 128), lambda i: (0, 0)), tile, vec],
        out_shape=[jax.ShapeDtypeStruct((1, 128), F32), jax.ShapeDtypeStruct((S, D), F32),
                   jax.ShapeDtypeStruct((1, D), F32)],
        compiler_params=_cp("arbitrary"), name=name)(h, g, target)


def _mm(pairs, name, out_dtype=F32, res=None, rms_g=None, rms_bwd=None, tm=None, tn=None, after=None):
    M = pairs[0][0].shape[0]
    N = pairs[0][1].shape[1 if pairs[0][2] == "nn" else 0]
    whole_rows = rms_g is not None or rms_bwd is not None
    tm = _pick(M, tm or TS_MM, 16)
    tn = N if whole_rows else _pick(N, tn or TN_MM, 128)
    npair = len(pairs)
    modes = [p[2] for p in pairs]
    after_specs, after_ops = _after(after)

    rc = MM_ROW_CHUNK if whole_rows and tm % MM_ROW_CHUNK == 0 else tm

    def body(*refs):
        rest = refs[2 * npair + len(after_ops):]
        res_ref = None
        if res is not None:
            res_ref, rest = rest[0], rest[1:]
        if rms_bwd is not None:
            dg_ref = rest[4]

            @pl.when(pl.program_id(0) == 0)
            def _():
                dg_ref[...] = jnp.zeros_like(dg_ref)

        for r0 in range(0, tm, rc):
            rows = pl.ds(r0, rc)
            acc = None
            for p in range(npair):
                a_ = refs[2 * p][rows, :].astype(BF16)
                d = _dot(a_, refs[2 * p + 1][...]) if modes[p] == "nn" else _dot_nt(a_, refs[2 * p + 1][...])
                acc = d if acc is None else acc + d
            if res_ref is not None:
                acc = acc + res_ref[rows, :]
            if rms_bwd is not None:
                h_ref, g_ref, dres_ref, dh_ref, _ = rest
                x = h_ref[rows, :]
                r = lax.rsqrt(jnp.mean(x * x, axis=-1, keepdims=True) + RMS_EPS)
                xr = x * r
                dg_ref[...] += jnp.sum(acc * xr, axis=0, keepdims=True)
                u = acc * g_ref[...]
                dh_ref[rows, :] = r * u - xr * (r * jnp.mean(u * xr, axis=-1, keepdims=True)) + dres_ref[rows, :]
            elif rms_g is not None:
                g_ref, o_ref, n_ref = rest
                o_ref[rows, :] = acc
                r = lax.rsqrt(jnp.mean(acc * acc, axis=-1, keepdims=True) + RMS_EPS)
                n_ref[rows, :] = ((acc * r) * g_ref[...]).astype(BF16)
            else:
                rest[0][rows, :] = acc.astype(rest[0].dtype)

    in_specs, ins = [], []
    for a, w, mode in pairs:
        K = a.shape[1]
        in_specs.append(pl.BlockSpec((tm, K), lambda i, j: (i, 0)))
        once = _RESIDENT if tn == N else None
        in_specs.append(pl.BlockSpec((K, tn), lambda i, j: (0, j), pipeline_mode=once) if mode == "nn"
                        else pl.BlockSpec((tn, K), lambda i, j: (j, 0), pipeline_mode=once))
        ins += [a, w]
    in_specs += after_specs
    ins += after_ops
    tile = pl.BlockSpec((tm, tn), lambda i, j: (i, j))
    vec = pl.BlockSpec((1, tn), lambda i, j: (0, j))
    if res is not None:
        in_specs.append(tile)
        ins.append(res)
    sem = ("parallel", "parallel")
    if rms_bwd is not None:
        in_specs += [tile, vec, tile]
        ins += list(rms_bwd)
        out_specs = [tile, vec]
        out_shape = [jax.ShapeDtypeStruct((M, N), F32), jax.ShapeDtypeStruct((1, N), F32)]
        sem = ("arbitrary", "arbitrary")
    elif rms_g is not None:
        in_specs.append(vec)
        ins.append(rms_g)
        out_specs = [tile, tile]
        out_shape = [jax.ShapeDtypeStruct((M, N), F32), jax.ShapeDtypeStruct((M, N), BF16)]
    else:
        out_specs = tile
        out_shape = jax.ShapeDtypeStruct((M, N), out_dtype)
    return pl.pallas_call(
        body, grid=(M // tm, N // tn), in_specs=in_specs, out_specs=out_specs, out_shape=out_shape,
        compiler_params=_cp(*sem), name=name)(*ins)


def _mm_tn(a, b, name, ts=None, tn=None):
    S, K = a.shape
    N = b.shape[1]
    ts = _pick(S, ts or TS_MM, 16)
    tn = _pick(N, tn or TN_MM, 128)
    nsteps = S // ts

    def body(a_ref, b_ref, o_ref, acc_ref):
        s = pl.program_id(1)

        @pl.when(s == 0)
        def _():
            acc_ref[...] = jnp.zeros_like(acc_ref)

        acc_ref[...] += _dot_tn(a_ref[...].astype(BF16), b_ref[...].astype(BF16))

        @pl.when(s == nsteps - 1)
        def _():
            o_ref[...] = acc_ref[...].astype(o_ref.dtype)

    return pl.pallas_call(
        body, grid=(N // tn, nsteps),
        in_specs=[pl.BlockSpec((ts, K), lambda j, s: (s, 0)), pl.BlockSpec((ts, tn), lambda j, s: (s, j))],
        out_specs=pl.BlockSpec((K, tn), lambda j, s: (0, j)), out_shape=jax.ShapeDtypeStruct((K, N), BF16),
        scratch_shapes=[pltpu.VMEM((K, tn), F32)],
        compiler_params=_cp("parallel", "arbitrary"), name=name)(a, b)


def _col_chunk(n):
    return 256 if n % 256 == 0 else 128


def _ffn_up(n, wgt, wut, name, after=None):
    S, D = n.shape
    F = wgt.shape[0]
    tm = _pick(S, TS_FFN, 16)
    ce = _col_chunk(F)
    after_specs, after_ops = _after(after)

    def body(n_ref, wg_ref, wu_ref, *rest):
        a_ref, b_ref, hid_ref = rest[-3:]
        x = n_ref[...]
        for c0 in range(0, F, ce):
            a = _dot_nt(x, wg_ref[c0:c0 + ce, :])
            b = _dot_nt(x, wu_ref[c0:c0 + ce, :])
            a_ref[:, c0:c0 + ce] = a.astype(BF16)
            b_ref[:, c0:c0 + ce] = b.astype(BF16)
            hid_ref[:, c0:c0 + ce] = (a * _sigmoid(a) * b).astype(BF16)

    wspec = pl.BlockSpec((F, D), lambda i: (0, 0), pipeline_mode=_RESIDENT)
    ospec = pl.BlockSpec((tm, F), lambda i: (i, 0))
    osh = jax.ShapeDtypeStruct((S, F), BF16)
    return pl.pallas_call(
        body, grid=(S // tm,),
        in_specs=[pl.BlockSpec((tm, D), lambda i: (i, 0)), wspec, wspec] + after_specs,
        out_specs=[ospec, ospec, ospec], out_shape=[osh, osh, osh],
        compiler_params=_cp("parallel"), name=name)(n, wgt, wut, *after_ops)


def _ffn_dhid(dh, wd, a, b, name):
    S, D = dh.shape
    F = wd.shape[0]
    tm = _pick(S, TS_FFN, 16)
    ce = _col_chunk(F)

    def body(dh_ref, wd_ref, a_ref, b_ref, da_ref, db_ref):
        x = dh_ref[...].astype(BF16)
        for c0 in range(0, F, ce):
            g = _dot_nt(x, wd_ref[c0:c0 + ce, :]).astype(BF16)
            a_ = a_ref[:, c0:c0 + ce]
            sg = _sigmoid(a_)
            silu = a_ * sg
            da_ref[:, c0:c0 + ce] = (g * b_ref[:, c0:c0 + ce]) * (sg + silu * (1.0 - sg))
            db_ref[:, c0:c0 + ce] = g * silu

    tile = pl.BlockSpec((tm, F), lambda i: (i, 0))
    osh = jax.ShapeDtypeStruct((S, F), BF16)
    return pl.pallas_call(
        body, grid=(S // tm,),
        in_specs=[pl.BlockSpec((tm, D), lambda i: (i, 0)),
                  pl.BlockSpec((F, D), lambda i: (0, 0), pipeline_mode=_RESIDENT), tile, tile],
        out_specs=[tile, tile], out_shape=[osh, osh],
        compiler_params=_cp("parallel"), name=name)(dh, wd, a, b)


def _softmax_rows(s):
    m = jnp.max(s, axis=-1, keepdims=True)
    p = jnp.exp(s - m)
    return p / jnp.sum(p, axis=-1, keepdims=True)


def _attn_fwd(q, k, v, name):
    S, D = q.shape
    M = k.shape[0]
    hd = D // XA_HEADS
    scale = hd ** -0.5
    ts = _pick(S, TS_ATTN, 16)

    def body(q_ref, k_ref, v_ref, o_ref):
        for h in range(XA_HEADS):
            sl = slice(h * hd, (h + 1) * hd)
            p = _softmax_rows(_dot_nt(q_ref[:, sl], k_ref[:, sl]) * scale)
            o_ref[:, sl] = _dot(p.astype(BF16), v_ref[:, sl]).astype(BF16)

    tile = pl.BlockSpec((ts, D), lambda i: (i, 0))
    memspec = pl.BlockSpec((M, D), lambda i: (0, 0))
    return pl.pallas_call(
        body, grid=(S // ts,), in_specs=[tile, memspec, memspec], out_specs=tile,
        out_shape=jax.ShapeDtypeStruct((S, D), BF16), compiler_params=_cp("parallel"), name=name)(q, k, v)


def _attn_bwd(q, k, v, do, name):
    S, D = q.shape
    M = k.shape[0]
    hd = D // XA_HEADS
    scale = hd ** -0.5
    ts = _pick(S, TS_ATTN, 16)

    def body(q_ref, k_ref, v_ref, do_ref, dq_ref, dk_ref, dv_ref):
        @pl.when(pl.program_id(0) == 0)
        def _():
            dk_ref[...] = jnp.zeros_like(dk_ref)
            dv_ref[...] = jnp.zeros_like(dv_ref)

        for h in range(XA_HEADS):
            sl = slice(h * hd, (h + 1) * hd)
            qh, kh, vh, doh = q_ref[:, sl], k_ref[:, sl], v_ref[:, sl], do_ref[:, sl]
            p = _softmax_rows(_dot_nt(qh, kh) * scale)
            dp = _dot_nt(doh, vh)
            dv_ref[:, sl] += _dot_tn(p.astype(BF16), doh)
            delta = jnp.sum(dp * p, axis=-1, keepdims=True)
            ds = (p * (dp - delta) * scale).astype(BF16)
            dq_ref[:, sl] = _dot(ds, kh).astype(BF16)
            dk_ref[:, sl] += _dot_tn(ds, qh)

    tile = pl.BlockSpec((ts, D), lambda i: (i, 0))
    memspec = pl.BlockSpec((M, D), lambda i: (0, 0))
    return pl.pallas_call(
        body, grid=(S // ts,), in_specs=[tile, memspec, memspec, tile], out_specs=[tile, memspec, memspec],
        out_shape=[jax.ShapeDtypeStruct((S, D), BF16), jax.ShapeDtypeStruct((M, D), F32),
                   jax.ShapeDtypeStruct((M, D), F32)],
        compiler_params=_cp("arbitrary"), name=name)(q, k, v, do)


def _halo_specs(ts, width, col):
    per = ts // HALO

    def prev(i):
        return (jnp.maximum(i * per - 1, 0), col)

    def nxt(i, n_tiles):
        return (jnp.minimum((i + 1) * per, n_tiles * per - 1), col)

    return prev, nxt


def _fill_ext(ext_ref, prev_val, main_val, next_val, first, last, ts):
    ext_ref[pl.ds(0, HALO), :] = jnp.where(first, 0.0, prev_val)
    ext_ref[pl.ds(HALO, ts), :] = main_val
    ext_ref[pl.ds(HALO + ts, HALO), :] = jnp.where(last, 0.0, next_val)


SUBLANES = 8


def _fill_shifted(sh_ref, ts):
    n = ts + 2 * HALO - SUBLANES
    for s in range(1, SUBLANES):
        sh_ref[s, pl.ds(0, n), :] = sh_ref[0, pl.ds(s, n), :]


def _tap(sh_ref, r0, offset, rc):
    q, s = divmod(offset, SUBLANES)
    return sh_ref[s, pl.ds(pl.multiple_of(r0 + SUBLANES * q, SUBLANES), rc), :]


def _conv_fwd(z, wa, ba, lng, lnb, wb, bb, name, after=None):
    S = z.shape[0]
    C = z.shape[1] // 5
    KA, KB = wa.shape[0], wb.shape[0]
    pa, pb = KA // 2, KB // 2
    assert pa <= HALO and pb <= HALO
    ts = _pick(S, TS_CONV, ROW_CHUNK)
    nt = S // ts
    rc = ROW_CHUNK
    prev, nxt = _halo_specs(ts, 5 * C, 0)
    after_specs, after_ops = _after(after)

    def body(*refs):
        compute(*refs[:9], *refs[9 + len(after_ops):])

    def compute(z_ref, zp_ref, zn_ref, wa_ref, ba_ref, lng_ref, lnb_ref, wb_ref, bb_ref, ab_ref, ca_ref,
                ga_sh, tb_ext, win_b):
        i = pl.program_id(0)
        first, last = i == 0, i == nt - 1

        def glu(r):
            return r[:, 0:C] * _sigmoid(r[:, C:2 * C])

        def gcb(r):
            return r[:, 4 * C:5 * C] * r[:, 2 * C:3 * C]

        _fill_ext(ga_sh.at[0], glu(zp_ref), glu(z_ref), glu(zn_ref), first, last, ts)
        _fill_shifted(ga_sh, ts)
        _fill_ext(tb_ext, gcb(zp_ref), gcb(z_ref), gcb(zn_ref), first, last, ts)

        def chunk(c, carry):
            r0 = pl.multiple_of(c * rc, rc)
            win_b[...] = tb_ext[pl.ds(r0, rc + 2 * HALO), :]
            acc = jnp.zeros((rc, C), F32)
            for k in range(KA):
                acc = acc + wa_ref[k:k + 1, :] * _tap(ga_sh, r0, HALO - pa + k, rc)
            ca = acc + ba_ref[...]
            ca_ref[pl.ds(r0, rc), :] = ca
            mu = jnp.mean(ca, axis=-1, keepdims=True)
            xc = ca - mu
            var = jnp.mean(xc * xc, axis=-1, keepdims=True)
            ln = xc * lax.rsqrt(var + LN_EPS) * lng_ref[...] + lnb_ref[...]
            ab_ref[pl.ds(r0, rc), 0:C] = (ln * _sigmoid(ln)).astype(BF16)
            cb = jnp.zeros((rc, C), F32) + bb_ref[...]
            for k in range(KB):
                cb = cb + wb_ref[k:k + 1, :] * win_b[pl.ds(HALO - pb + k, rc), :]
            ab_ref[pl.ds(r0, rc), C:2 * C] = (z_ref[pl.ds(r0, rc), 3 * C:4 * C] * cb).astype(BF16)
            return carry

        lax.fori_loop(0, ts // rc, chunk, 0)

    zspec = pl.BlockSpec((ts, 5 * C), lambda i: (i, 0))
    zprev = pl.BlockSpec((HALO, 5 * C), prev)
    znext = pl.BlockSpec((HALO, 5 * C), lambda i: nxt(i, nt))

    def full(a):
        return pl.BlockSpec(a.shape, lambda i: (0, 0))

    return pl.pallas_call(
        body, grid=(nt,),
        in_specs=[zspec, zprev, znext, full(wa), full(ba), full(lng), full(lnb), full(wb), full(bb)] + after_specs,
        out_specs=[pl.BlockSpec((ts, 2 * C), lambda i: (i, 0)), pl.BlockSpec((ts, C), lambda i: (i, 0))],
        out_shape=[jax.ShapeDtypeStruct((S, 2 * C), BF16), jax.ShapeDtypeStruct((S, C), F32)],
        scratch_shapes=[pltpu.VMEM((SUBLANES, ts + 2 * HALO, C), F32), pltpu.VMEM((ts + 2 * HALO, C), F32),
                        pltpu.VMEM((rc + 2 * HALO, C), F32)],
        compiler_params=_cp("parallel"), name=name)(z, z, z, wa, ba, lng, lnb, wb, bb, *after_ops)


def _conv_bwd_ln(dab, ca, lng, lnb, name):
    S, C = ca.shape
    ts = _pick(S, TS_ROW, 8)

    def body(da_ref, ca_ref, lng_ref, lnb_ref, dca_ref, dg_ref, db_ref, dbias_ref):
        @pl.when(pl.program_id(0) == 0)
        def _():
            dg_ref[...] = jnp.zeros_like(dg_ref)
            db_ref[...] = jnp.zeros_like(db_ref)
            dbias_ref[...] = jnp.zeros_like(dbias_ref)

        ca_ = ca_ref[...]
        mu = jnp.mean(ca_, axis=-1, keepdims=True)
        xc = ca_ - mu
        rstd = lax.rsqrt(jnp.mean(xc * xc, axis=-1, keepdims=True) + LN_EPS)
        xh = xc * rstd
        ln = xh * lng_ref[...] + lnb_ref[...]
        sg = _sigmoid(ln)
        dln = da_ref[...].astype(F32) * (sg * (1.0 + ln * (1.0 - sg)))
        dg_ref[...] += jnp.sum(dln * xh, axis=0, keepdims=True)
        db_ref[...] += jnp.sum(dln, axis=0, keepdims=True)
        dxh = dln * lng_ref[...]
        dca = rstd * (dxh - jnp.mean(dxh, axis=-1, keepdims=True) - xh * jnp.mean(dxh * xh, axis=-1, keepdims=True))
        dca_ref[...] = dca
        dbias_ref[...] += jnp.sum(dca, axis=0, keepdims=True)

    tile = pl.BlockSpec((ts, C), lambda i: (i, 0))
    vec = pl.BlockSpec((1, C), lambda i: (0, 0))
    vsh = jax.ShapeDtypeStruct((1, C), F32)
    return pl.pallas_call(
        body, grid=(S // ts,), in_specs=[tile, tile, vec, vec], out_specs=[tile, vec, vec, vec],
        out_shape=[jax.ShapeDtypeStruct((S, C), F32), vsh, vsh, vsh],
        compiler_params=_cp("arbitrary"), name=name)(dab, ca, lng, lnb)


def _conv_bwd(z, dca, dab, wa, wb, bb, name):
    S = z.shape[0]
    C = z.shape[1] // 5
    KA, KB = wa.shape[0], wb.shape[0]
    pa, pb = KA // 2, KB // 2
    ts = _pick(S, TS_CONV, ROW_CHUNK)
    nt = S // ts
    rc = ROW_CHUNK
    prev0, nxt0 = _halo_specs(ts, C, 0)
    prev1, nxt1 = _halo_specs(ts, C, 1)

    def body(z_ref, zp_ref, zn_ref, dca_ref, dcap_ref, dcan_ref, db_ref, dbp_ref, dbn_ref, wa_ref, wb_ref, bb_ref,
             dz_ref, dwa_ref, dwb_ref, dbb_ref,
             ga_sh, dca_sh, tb_ext, dcb_ext, win_tb, win_dcb, acc_a, acc_b, acc_bias):
        i = pl.program_id(0)
        first, last = i == 0, i == nt - 1

        @pl.when(first)
        def _():
            acc_a[...] = jnp.zeros_like(acc_a)
            acc_b[...] = jnp.zeros_like(acc_b)
            acc_bias[...] = jnp.zeros_like(acc_bias)

        def glu(r):
            return r[:, 0:C] * _sigmoid(r[:, C:2 * C])

        def gcb(r):
            return r[:, 4 * C:5 * C] * r[:, 2 * C:3 * C]

        def dcb(d, r):
            return d[...].astype(F32) * r[:, 3 * C:4 * C]

        _fill_ext(ga_sh.at[0], glu(zp_ref), glu(z_ref), glu(zn_ref), first, last, ts)
        _fill_shifted(ga_sh, ts)
        _fill_ext(dca_sh.at[0], dcap_ref[...], dca_ref[...], dcan_ref[...], first, last, ts)
        _fill_shifted(dca_sh, ts)
        _fill_ext(tb_ext, gcb(zp_ref), gcb(z_ref), gcb(zn_ref), first, last, ts)
        _fill_ext(dcb_ext, dcb(dbp_ref, zp_ref), dcb(db_ref, z_ref), dcb(dbn_ref, zn_ref), first, last, ts)

        def fold(x):
            return jnp.sum(x.reshape(rc // 8, 8, C), axis=0)

        def chunk(c, carry):
            r0 = pl.multiple_of(c * rc, rc)
            win_tb[...] = tb_ext[pl.ds(r0, rc + 2 * HALO), :]
            win_dcb[...] = dcb_ext[pl.ds(r0, rc + 2 * HALO), :]
            dca_c = _tap(dca_sh, r0, HALO, rc)
            dglu = jnp.zeros((rc, C), F32)
            for k in range(KA):
                dglu = dglu + wa_ref[k:k + 1, :] * _tap(dca_sh, r0, HALO + pa - k, rc)
                acc_a[k] += fold(dca_c * _tap(ga_sh, r0, HALO - pa + k, rc))
            val = z_ref[pl.ds(r0, rc), 0:C]
            sg = _sigmoid(z_ref[pl.ds(r0, rc), C:2 * C])
            dz_ref[pl.ds(r0, rc), 0:C] = (dglu * sg).astype(BF16)
            dz_ref[pl.ds(r0, rc), C:2 * C] = (dglu * val * sg * (1.0 - sg)).astype(BF16)
            dcb_c = win_dcb[pl.ds(HALO, rc), :]
            cb = jnp.zeros((rc, C), F32) + bb_ref[...]
            dt = jnp.zeros((rc, C), F32)
            for k in range(KB):
                tb_k = win_tb[pl.ds(HALO - pb + k, rc), :]
                cb = cb + wb_ref[k:k + 1, :] * tb_k
                dt = dt + wb_ref[k:k + 1, :] * win_dcb[pl.ds(HALO + pb - k, rc), :]
                acc_b[k] += fold(dcb_c * tb_k)
            acc_bias[...] += fold(dcb_c)
            db_c = db_ref[pl.ds(r0, rc), :].astype(F32)
            dz_ref[pl.ds(r0, rc), 2 * C:3 * C] = (dt * z_ref[pl.ds(r0, rc), 4 * C:5 * C]).astype(BF16)
            dz_ref[pl.ds(r0, rc), 3 * C:4 * C] = (db_c * cb).astype(BF16)
            dz_ref[pl.ds(r0, rc), 4 * C:5 * C] = (dt * z_ref[pl.ds(r0, rc), 2 * C:3 * C]).astype(BF16)
            return carry

        lax.fori_loop(0, ts // rc, chunk, 0)

        @pl.when(last)
        def _():
            dwa_ref[...] = jnp.sum(acc_a[...], axis=1)
            dwb_ref[...] = jnp.sum(acc_b[...], axis=1)
            dbb_ref[...] = jnp.sum(acc_bias[...], axis=0, keepdims=True)

    zspec = pl.BlockSpec((ts, 5 * C), lambda i: (i, 0))
    zprev = pl.BlockSpec((HALO, 5 * C), prev0)
    znext = pl.BlockSpec((HALO, 5 * C), lambda i: nxt0(i, nt))
    dspec = pl.BlockSpec((ts, C), lambda i: (i, 0))
    dprev = pl.BlockSpec((HALO, C), prev0)
    dnext = pl.BlockSpec((HALO, C), lambda i: nxt0(i, nt))
    bspec = pl.BlockSpec((ts, C), lambda i: (i, 1))
    bprev = pl.BlockSpec((HALO, C), prev1)
    bnext = pl.BlockSpec((HALO, C), lambda i: nxt1(i, nt))

    def full(shape):
        return pl.BlockSpec(shape, lambda i: (0,) * len(shape))

    ext = pltpu.VMEM((ts + 2 * HALO, C), F32)
    shifted = pltpu.VMEM((SUBLANES, ts + 2 * HALO, C), F32)
    win = pltpu.VMEM((rc + 2 * HALO, C), F32)
    return pl.pallas_call(
        body, grid=(nt,),
        in_specs=[zspec, zprev, znext, dspec, dprev, dnext, bspec, bprev, bnext,
                  full(wa.shape), full(wb.shape), full(bb.shape)],
        out_specs=[pl.BlockSpec((ts, 5 * C), lambda i: (i, 0)), full((KA, C)), full((KB, C)), full((1, C))],
        out_shape=[jax.ShapeDtypeStruct((S, 5 * C), BF16), jax.ShapeDtypeStruct((KA, C), F32),
                   jax.ShapeDtypeStruct((KB, C), F32), jax.ShapeDtypeStruct((1, C), F32)],
        scratch_shapes=[shifted, shifted, ext, ext, win, win,
                        pltpu.VMEM((KA, 8, C), F32), pltpu.VMEM((KB, 8, C), F32), pltpu.VMEM((8, C), F32)],
        compiler_params=_cp("arbitrary"), name=name)(z, z, z, dca, dca, dca, dab, dab, dab, wa, wb, bb)


_GELU_C = 0.7978845608028654
_GELU_A = 0.044715


def _gelu(x):
    return 0.5 * x * (1.0 + jnp.tanh(_GELU_C * (x + _GELU_A * (x * x * x))))


def _gelu_and_grad(x):
    t = jnp.tanh(_GELU_C * (x + _GELU_A * (x * x * x)))
    hx = 0.5 * x
    return hx * (1.0 + t), 0.5 * (1.0 + t) + hx * (1.0 - t * t) * (_GELU_C * (1.0 + 3.0 * _GELU_A * x * x))


def _sgu_fwd(zp, lng, lnb, ws, bsb, name):
    S = zp.shape[0]
    D = zp.shape[1] // 2
    G = ws.shape[0]
    gd = D // G
    ts = _pick(S, TS_SGU, CHUNK)
    ncs = ts // CHUNK

    def body(zp_ref, lng_ref, lnb_ref, ws_ref, bsb_ref, y_ref, vb_ref):
        v = _gelu(zp_ref[:, D:2 * D])
        mu = jnp.mean(v, axis=-1, keepdims=True)
        xc = v - mu
        rstd = lax.rsqrt(jnp.mean(xc * xc, axis=-1, keepdims=True) + LN_EPS)
        vb_ref[...] = (xc * rstd * lng_ref[...] + lnb_ref[...]).astype(BF16)
        for c in range(ncs):
            rows = slice(c * CHUNK, (c + 1) * CHUNK)
            for g in range(G):
                cols = slice(g * gd, (g + 1) * gd)
                sv = _dot(ws_ref[g], vb_ref[rows, cols]) + bsb_ref[:, cols]
                y_ref[rows, cols] = (_gelu(zp_ref[rows, cols]) * sv).astype(BF16)

    def full(a):
        return pl.BlockSpec(a.shape, lambda i: (0,) * a.ndim)

    return pl.pallas_call(
        body, grid=(S // ts,),
        in_specs=[pl.BlockSpec((ts, 2 * D), lambda i: (i, 0)), full(lng), full(lnb), full(ws), full(bsb)],
        out_specs=pl.BlockSpec((ts, D), lambda i: (i, 0)), out_shape=jax.ShapeDtypeStruct((S, D), BF16),
        scratch_shapes=[pltpu.VMEM((ts, D), BF16)],
        compiler_params=_cp("parallel"), name=name)(zp, lng, lnb, ws, bsb)


def _sgu_bwd(dy, zp, lng, lnb, ws, wst, bsb, name):
    S = zp.shape[0]
    D = zp.shape[1] // 2
    G = ws.shape[0]
    gd = D // G
    ts = _pick(S, TS_SGU, CHUNK)
    ncs = ts // CHUNK

    def body(dy_ref, zp_ref, lng_ref, lnb_ref, ws_ref, wst_ref, bsb_ref,
             dzp_ref, dws_ref, dbs_ref, dg_ref, db_ref, vb_ref, dvln_ref, acc_bs):
        i = pl.program_id(0)

        @pl.when(i == 0)
        def _():
            dws_ref[...] = jnp.zeros_like(dws_ref)
            acc_bs[...] = jnp.zeros_like(acc_bs)
            dg_ref[...] = jnp.zeros_like(dg_ref)
            db_ref[...] = jnp.zeros_like(db_ref)

        v, dv_dz = _gelu_and_grad(zp_ref[:, D:2 * D])
        mu = jnp.mean(v, axis=-1, keepdims=True)
        xc = v - mu
        rstd = lax.rsqrt(jnp.mean(xc * xc, axis=-1, keepdims=True) + LN_EPS)
        xh = xc * rstd
        vb_ref[...] = (xh * lng_ref[...] + lnb_ref[...]).astype(BF16)
        for c in range(ncs):
            rows = slice(c * CHUNK, (c + 1) * CHUNK)
            for g in range(G):
                cols = slice(g * gd, (g + 1) * gd)
                u, du_dz = _gelu_and_grad(zp_ref[rows, cols])
                dy_ = dy_ref[rows, cols].astype(F32)
                sv = _dot(ws_ref[g], vb_ref[rows, cols]) + bsb_ref[:, cols]
                dzp_ref[rows, cols] = (dy_ * sv * du_dz).astype(BF16)
                dsv = dy_ * u
                acc_bs[:, cols] += dsv
                dsvb = dsv.astype(BF16)
                dws_ref[g] += _dot_nt(dsvb, vb_ref[rows, cols])
                dvln_ref[rows, cols] = _dot(wst_ref[g], dsvb)
        dvln = dvln_ref[...]
        dg_ref[...] += jnp.sum(dvln * xh, axis=0, keepdims=True)
        db_ref[...] += jnp.sum(dvln, axis=0, keepdims=True)
        dxh = dvln * lng_ref[...]
        dv = rstd * (dxh - jnp.mean(dxh, axis=-1, keepdims=True) - xh * jnp.mean(dxh * xh, axis=-1, keepdims=True))
        dzp_ref[:, D:2 * D] = (dv * dv_dz).astype(BF16)

        @pl.when(i == pl.num_programs(0) - 1)
        def _():
            dbs_ref[...] = acc_bs[...]

    def full(shape):
        return pl.BlockSpec(shape, lambda i: (0,) * len(shape))

    return pl.pallas_call(
        body, grid=(S // ts,),
        in_specs=[pl.BlockSpec((ts, D), lambda i: (i, 0)), pl.BlockSpec((ts, 2 * D), lambda i: (i, 0)),
                  full(lng.shape), full(lnb.shape), full(ws.shape), full(wst.shape), full(bsb.shape)],
        out_specs=[pl.BlockSpec((ts, 2 * D), lambda i: (i, 0)), full(ws.shape), full(bsb.shape),
                   full((1, D)), full((1, D))],
        out_shape=[jax.ShapeDtypeStruct((S, 2 * D), BF16), jax.ShapeDtypeStruct(ws.shape, F32),
                   jax.ShapeDtypeStruct(bsb.shape, F32), jax.ShapeDtypeStruct((1, D), F32),
                   jax.ShapeDtypeStruct((1, D), F32)],
        scratch_shapes=[pltpu.VMEM((ts, D), BF16), pltpu.VMEM((ts, D), F32),
                        pltpu.VMEM(bsb.shape, F32)],
        compiler_params=_cp("arbitrary"), name=name)(dy, zp, lng, lnb, ws, wst, bsb)


def _group_sum(x, groups, name):
    P, D = x.shape
    gd = D // groups

    def body(x_ref, o_ref):
        for g in range(groups):
            o_ref[:, g:g + 1] = jnp.sum(x_ref[:, g * gd:(g + 1) * gd], axis=1, keepdims=True)

    return pl.pallas_call(body, out_shape=jax.ShapeDtypeStruct((P, groups), F32), name=name)(x)


def _adamw_small(ws, gs, ms, vs, name):
    n = len(ws)
    shapes = [w.shape for w in ws]
    flat = [(w.size // w.shape[-1], w.shape[-1]) for w in ws]

    def body(*refs):
        for k in range(n):
            w_ref, g_ref, m_ref, v_ref = (refs[j * n + k] for j in range(4))
            d_ref, nm_ref, nv_ref = (refs[(4 + j) * n + k] for j in range(3))
            d_ref[...], nm_ref[...], nv_ref[...] = _adamw_math(w_ref[...], g_ref[...], m_ref[...], v_ref[...])

    outs = pl.pallas_call(
        body, out_shape=[jax.ShapeDtypeStruct(f, F32) for f in flat] * 3, name=name)(
            *(a.reshape(f) for group in (ws, gs, ms, vs) for a, f in zip(group, flat)))
    return [tuple(outs[j * n + k].reshape(shapes[k]) for j in range(3)) for k in range(n)]


_HBM = pl.BlockSpec(memory_space=pltpu.HBM)


def _remote(src, dst, send_sem, recv_sem, to):
    return pltpu.make_async_remote_copy(src_ref=src, dst_ref=dst, send_sem=send_sem, recv_sem=recv_sem,
                                        device_id=to, device_id_type=MESH)


def _all_gather(block, name):
    R, C = block.shape

    def body(x_ref, out_ref, send_sems, recv_sems, local_sem):
        x, y, c = lax.axis_index("x"), lax.axis_index("y"), lax.axis_index("c")
        me, sibling = (x, y, c), (x, y, 1 - c)
        chips = [(1 - x, y), (x, 1 - y), (1 - x, 1 - y)]

        def slot(px, py, pc):
            return out_ref.at[4 * px + 2 * py + pc]

        def copy(k, blk, to, src=None):
            return _remote(slot(*blk) if src is None else src, slot(*blk), send_sems.at[k], recv_sems.at[k], to)

        mine = pltpu.make_async_copy(x_ref, slot(*me), local_sem)
        mine.start()
        first = [copy(0, me, sibling, src=x_ref)]
        first += [copy(1 + j, me, (*chip, c), src=x_ref) for j, chip in enumerate(chips)]
        for cp in first:
            cp.start()
        passed = [copy(4 + j, (*chip, c), sibling) for j, chip in enumerate(chips)]
        for j, chip in enumerate(chips):
            copy(1 + j, (*chip, c), me).wait_recv()
            passed[j].start()
        copy(0, sibling, me).wait_recv()
        for j, chip in enumerate(chips):
            copy(4 + j, (*chip, 1 - c), me).wait_recv()
        for cp in first + passed:
            cp.wait_send()
        mine.wait()

    return pl.pallas_call(
        body, out_shape=jax.ShapeDtypeStruct((NDEV, R, C), block.dtype), in_specs=[_HBM], out_specs=_HBM,
        scratch_shapes=[pltpu.SemaphoreType.DMA((7,)), pltpu.SemaphoreType.DMA((7,)), pltpu.SemaphoreType.DMA],
        name=name)(block)


def _all_gather_weights(pack, rows, name, after=None):
    C = pack.shape[1]
    nw = len(rows)
    starts = [sum(rows[:w]) for w in range(nw)]
    after_specs, after_ops = _after(after)

    def body(pack_ref, *rest):
        rest = rest[len(after_ops):]
        outs = rest[:nw]
        send_sems, recv_sems, local_sem = rest[nw:]
        x, y, c = lax.axis_index("x"), lax.axis_index("y"), lax.axis_index("c")
        me, sibling = (x, y, c), (x, y, 1 - c)
        chips = [(1 - x, y), (x, 1 - y), (1 - x, 1 - y)]

        def block(w, px, py, pc):
            return outs[w].at[pl.ds((4 * px + 2 * py + pc) * rows[w], rows[w])]

        def mine(w):
            return pack_ref.at[pl.ds(starts[w], rows[w])]

        def all_of(k):
            return _remote(pack_ref, pack_ref, send_sems.at[k], recv_sems.at[k], me)

        for w in range(nw):
            pltpu.make_async_copy(mine(w), block(w, *me), local_sem).start()
        for k, to in enumerate([sibling] + [(*chip, c) for chip in chips]):
            for w in range(nw):
                _remote(mine(w), block(w, *me), send_sems.at[k], recv_sems.at[k], to).start()
        for j, chip in enumerate(chips):
            all_of(1 + j).wait_recv()
            for w in range(nw):
                _remote(block(w, *chip, c), block(w, *chip, c), send_sems.at[4 + j], recv_sems.at[4 + j], sibling).start()
        all_of(0).wait_recv()
        for j in range(3):
            all_of(4 + j).wait_recv()
        for k in range(7):
            all_of(k).wait_send()
        pltpu.make_async_copy(pack_ref, pack_ref, local_sem).wait()

    return pl.pallas_call(
        body, out_shape=[jax.ShapeDtypeStruct((NDEV * r, C), pack.dtype) for r in rows],
        in_specs=[_HBM] + after_specs, out_specs=[_HBM] * nw,
        scratch_shapes=[pltpu.SemaphoreType.DMA((7,)), pltpu.SemaphoreType.DMA((7,)), pltpu.SemaphoreType.DMA],
        name=name)(pack, *after_ops)


_SEM = pl.BlockSpec(memory_space=pltpu.SEMAPHORE)
_DATAFLOW = pltpu.SideEffectType.DATAFLOW_SIDE_EFFECTING


def _split_start(srcs, lands, plan, n, after, name):
    nbuf = len(srcs) + len(lands)
    after_specs, after_ops = _after(after)

    def body(*refs):
        src_refs, land_refs = refs[:len(srcs)], refs[len(srcs):nbuf]
        send_sems, recv_sems = refs[nbuf + len(after_ops)], refs[nbuf + len(after_ops) + 1]
        for k, (src, dst, to) in enumerate(plan(src_refs, land_refs)):
            _remote(src, dst, send_sems.at[k], recv_sems.at[k], to).start()
        refs[-1][...] = jnp.zeros_like(refs[-1])

    bufs = [pltpu.with_memory_space_constraint(a, pltpu.HBM) for a in list(srcs) + list(lands)]
    outs = pl.pallas_call(
        body, name=name,
        out_shape=(pltpu.SemaphoreType.DMA((n,)), pltpu.SemaphoreType.DMA((n,)),
                   *[pltpu.HBM(a.shape, a.dtype) for a in bufs], jax.ShapeDtypeStruct((8, 128), F32)),
        in_specs=[_HBM] * nbuf + after_specs,
        out_specs=(_SEM, _SEM, *[_HBM] * nbuf, pl.BlockSpec(memory_space=pltpu.VMEM)),
        input_output_aliases={i: 2 + i for i in range(nbuf)},
        compiler_params=pltpu.CompilerParams(has_side_effects=_DATAFLOW))(*bufs, *after_ops)
    return outs[0], outs[1], list(outs[2:2 + len(srcs)]), list(outs[2 + len(srcs):2 + nbuf]), outs[-1]


def _split_wait(send_sems, recv_sems, srcs, lands, plan, after, name):
    nbuf = len(srcs) + len(lands)
    after_specs, after_ops = _after(after)

    def body(*refs):
        src_refs, land_refs = refs[:len(srcs)], refs[len(srcs):nbuf]
        send_sems_ref, recv_sems_ref = refs[nbuf], refs[nbuf + 1]
        for k, (src, dst, to) in enumerate(plan(src_refs, land_refs)):
            copy = _remote(src, dst, send_sems_ref.at[k], recv_sems_ref.at[k], to)
            copy.wait_send()
            copy.wait_recv()

    outs = pl.pallas_call(
        body, name=name, out_shape=tuple(pltpu.HBM(a.shape, a.dtype) for a in list(srcs) + list(lands)),
        in_specs=[_HBM] * nbuf + [_SEM, _SEM] + after_specs, out_specs=tuple([_HBM] * nbuf),
        input_output_aliases={i: i for i in range(nbuf)},
        compiler_params=pltpu.CompilerParams(has_side_effects=_DATAFLOW))(*srcs, *lands, send_sems, recv_sems, *after_ops)
    return list(outs[:len(srcs)]), list(outs[len(srcs):])


def _peers(x, y, c):
    return [(mask, (1 - x if mask & 4 else x, 1 - y if mask & 2 else y, 1 - c if mask & 1 else c))
            for mask in range(1, NDEV)]


def _gather_plan(rows):
    starts = [sum(rows[:w]) for w in range(len(rows))]

    def plan(src_refs, land_refs):
        x, y, c = lax.axis_index("x"), lax.axis_index("y"), lax.axis_index("c")
        copies = []
        for w, r in enumerate(rows):
            mine = src_refs[0].at[pl.ds(starts[w], r)]
            dst = land_refs[w].at[pl.ds((4 * x + 2 * y + c) * r, r)]
            copies += [(mine, dst, peer) for _, peer in _peers(x, y, c)]
        return copies

    return plan, (NDEV - 1) * len(rows)


def _place_own(shards, fulls, dev_idx, name):
    nw = len(shards)

    def body(i_ref, *refs):
        for w in range(nw):
            refs[2 * nw + w][...] = refs[w][...]

    grid_spec = pltpu.PrefetchScalarGridSpec(
        num_scalar_prefetch=1, grid=(1,),
        in_specs=[pl.BlockSpec(s.shape, lambda t, i_ref: (0, 0)) for s in shards] + [_ANY] * nw,
        out_specs=[pl.BlockSpec(s.shape, lambda t, i_ref: (i_ref[0], 0)) for s in shards])
    outs = pl.pallas_call(
        body, grid_spec=grid_spec, out_shape=[jax.ShapeDtypeStruct(f.shape, f.dtype) for f in fulls],
        input_output_aliases={1 + nw + w: w for w in range(nw)}, name=name)(dev_idx, *shards, *fulls)
    return list(outs)


def _scatter_plan(rows):
    def plan(src_refs, land_refs):
        x, y, c = lax.axis_index("x"), lax.axis_index("y"), lax.axis_index("c")
        copies = []
        for w, r in enumerate(rows):
            for mask, (px, py, pc) in _peers(x, y, c):
                src = src_refs[w].at[pl.ds((4 * px + 2 * py + pc) * r, r)]
                copies.append((src, land_refs[w].at[mask - 1], (px, py, pc)))
        return copies

    return plan, (NDEV - 1) * len(rows)


def _adamw_math(w, g, m, v):
    nm = ADAM_B1 * m + (1.0 - ADAM_B1) * g
    nv = ADAM_B2 * v + (1.0 - ADAM_B2) * (g * g)
    bc1 = 1.0 - ADAM_B1 ** ADAM_STEP
    bc2 = 1.0 - ADAM_B2 ** ADAM_STEP
    return -ADAM_LR * ((nm / bc1) / (jnp.sqrt(nv / bc2) + ADAM_EPS) + ADAM_WD * w), nm, nv


def _finish_weight(gs, gots, dev_idx, w, m, v, name, after=None):
    L = len(gs)
    n1, r, C = gots[0].shape
    block = (None,) + w.shape[1:]

    after_specs, after_ops = _after(after)

    def body(i_ref, *refs):
        ins, (w_ref, m_ref, v_ref), (g_out, d_out, m_out, v_out) = refs[:2 * L], refs[2 * L:2 * L + 3], refs[-4:]
        for layer in range(L):
            @pl.when(pl.program_id(0) == layer)
            def _():
                g_ref, got_ref = ins[2 * layer], ins[2 * layer + 1]
                acc = g_ref[...].astype(F32)
                for k in range(n1):
                    acc = acc + got_ref[k].astype(F32)
                g_out[...] = acc
                d_out[...], m_out[...], v_out[...] = _adamw_math(w_ref[...], acc, m_ref[...], v_ref[...])

    in_specs, ins = [], []
    for g, got in zip(gs, gots):
        in_specs += [pl.BlockSpec((r, C), lambda t, i_ref: (i_ref[0], 0), pipeline_mode=_RESIDENT),
                     pl.BlockSpec((n1, r, C), lambda t, i_ref: (0, 0, 0), pipeline_mode=_RESIDENT)]
        ins += [g, got]
    per_layer = pl.BlockSpec(block, lambda t, i_ref: (t, 0, 0))
    grid_spec = pltpu.PrefetchScalarGridSpec(
        num_scalar_prefetch=1, grid=(L,), in_specs=in_specs + [per_layer] * 3 + after_specs,
        out_specs=[per_layer] * 4)
    return pl.pallas_call(
        body, grid_spec=grid_spec, out_shape=[jax.ShapeDtypeStruct(w.shape, F32)] * 4,
        compiler_params=_cp("arbitrary"), name=name)(dev_idx, *ins, w, m, v, *after_ops)


def _sum_slots(a, name):
    n, R, C = a.shape

    def body(a_ref, o_ref):
        acc = a_ref[0]
        for k in range(1, n):
            acc = acc + a_ref[k]
        o_ref[...] = acc

    return pl.pallas_call(body, out_shape=jax.ShapeDtypeStruct((R, C), F32), name=name)(a)


def _shard_axis(name):
    return {"ev_w_in": 2, "ev_a_conv_w": 2, "ev_b_conv_w": 2, "ev_w_out": 1, "od_w_in": 2, "od_c_ln_g": 1,
            "od_c_ln_b": 1, "od_w_out": 1, "xa_w_q": 1, "xa_w_k": 1, "xa_w_v": 1, "xa_w_o": 1,
            "ffn_w_gate": 2, "ffn_w_up": 2, "ffn_w_down": 1}[name]


BIG = ["ev_w_in", "ev_w_out", "od_w_in", "od_w_out", "xa_w_q", "xa_w_k", "xa_w_v", "xa_w_o",
       "ffn_w_gate", "ffn_w_up", "ffn_w_down"]
SMALL_SHARDED = ["ev_a_conv_w", "ev_b_conv_w", "od_c_ln_g", "od_c_ln_b"]
REPLICATED = ["g_mix", "g_xattn", "g_mem", "g_ffn", "g_final", "ev_a_conv_b", "ev_a_ln_g", "ev_a_ln_b",
              "ev_b_conv_b", "od_w_s", "od_b_s"]
WEIGHTS = ["g_mix", "g_xattn", "g_mem", "g_ffn", "g_final", "ev_w_in", "ev_a_conv_w", "ev_a_conv_b", "ev_a_ln_g",
           "ev_a_ln_b", "ev_b_conv_w", "ev_b_conv_b", "ev_w_out", "od_w_in", "od_c_ln_g", "od_c_ln_b", "od_w_s",
           "od_b_s", "od_w_out", "xa_w_q", "xa_w_k", "xa_w_v", "xa_w_o", "ffn_w_gate", "ffn_w_up", "ffn_w_down"]


def _full_from_blocks(blocks, axis):
    shard = blocks.shape[1:]
    full = jnp.moveaxis(blocks, 0, axis)
    return full.reshape(shard[:axis] + (NDEV * shard[axis],) + shard[axis + 1:])


def _blocks_from_full(full, axis):
    shp = full.shape
    split = full.reshape(shp[:axis] + (NDEV, shp[axis] // NDEV) + shp[axis + 1:])
    return jnp.moveaxis(split, axis, 0)


def _pad_rows(flat, width, row_align):
    per = width * row_align
    n = -(-flat.shape[0] // per) * per
    return jnp.pad(flat, (0, n - flat.shape[0])).reshape(n // width, width)


def _row(v):
    return v.reshape(1, -1)


def _xattn_fwd(h, nq, mem, g_m, wq, wk, wv, wo, g_next, tag, after):
    mem_n = _rms_fwd(mem, _row(g_m), f"xa_mem_rms_{tag}")
    q = _mm([(nq, wq, "nn")], f"xa_q_{tag}", out_dtype=BF16, after=after)
    k = _mm([(mem_n, wk, "nn")], f"xa_k_{tag}", out_dtype=BF16)
    v = _mm([(mem_n, wv, "nn")], f"xa_v_{tag}", out_dtype=BF16)
    o = _attn_fwd(q, k, v, f"xa_attn_{tag}")
    h_new, n_next = _mm([(o, wo, "nn")], f"xa_o_{tag}", res=h, rms_g=_row(g_next))
    return h_new, n_next, (h, nq, mem_n, q, k, v, o)


def _xattn_bwd(dh_new, saved, mem, g_x, g_m, wq, wk, wv, wo, tag, push):
    h, nq, mem_n, q, k, v, o = saved
    do = _mm([(dh_new, wo, "nt")], f"xa_do_{tag}", out_dtype=BF16)
    d_wo = _mm_tn(o, dh_new, f"xa_dwo_{tag}")
    dq, dk, dv = _attn_bwd(q, k, v, do, f"xa_attn_bwd_{tag}")
    d_wq = _mm_tn(nq, dq, f"xa_dwq_{tag}")
    d_wk = _mm_tn(mem_n, dk, f"xa_dwk_{tag}")
    d_wv = _mm_tn(mem_n, dv, f"xa_dwv_{tag}")
    token = push([d_wq, d_wk, d_wv, d_wo])
    dmem_n = _mm([(dk, wk, "nt"), (dv, wv, "nt")], f"xa_dmem_{tag}", after=token)
    _, d_gm = _rms_bwd(dmem_n, mem, _row(g_m), None, f"xa_mem_rms_bwd_{tag}")
    dh, d_gx = _mm([(dq, wq, "nt")], f"xa_dnq_{tag}", rms_bwd=(h, _row(g_x), dh_new), tm=1024, after=token)
    return dh, dict(g_xattn=d_gx, g_mem=d_gm)


def _ffn_fwd(h, n, wgt, wut, wd, g_next, tag, after):
    a, b, hid = _ffn_up(n, wgt, wut, f"ffn_up_{tag}", after=after)
    if g_next is None:
        h_new, n_next = _mm([(hid, wd, "nn")], f"ffn_down_{tag}", res=h, tm=1024, tn=1024), None
    else:
        h_new, n_next = _mm([(hid, wd, "nn")], f"ffn_down_{tag}", res=h, rms_g=_row(g_next), tm=1024)
    return h_new, n_next, (h, n, a, b, hid)


def _ffn_bwd(dh_new, saved, g_f, wgt, wut, wd, tag, push):
    h, n, a, b, hid = saved
    da, db = _ffn_dhid(dh_new, wd, a, b, f"ffn_dhid_{tag}")
    d_wd = _mm_tn(hid, dh_new, f"ffn_dwd_{tag}", ts=1024, tn=1024)
    d_wgt = _mm_tn(da, n, f"ffn_dwg_{tag}", ts=1024, tn=1024)
    d_wut = _mm_tn(db, n, f"ffn_dwu_{tag}", ts=1024, tn=1024)
    token = push([d_wgt, d_wut, d_wd])
    dh, d_gf = _mm([(da, wgt, "nn"), (db, wut, "nn")], f"ffn_dn_{tag}", rms_bwd=(h, _row(g_f), dh_new), tm=512,
                   after=token)
    return dh, dict(g_ffn=d_gf)


_XA = ["xa_w_q", "xa_w_k", "xa_w_v", "xa_w_o"]
_FFN = ["ffn_w_gate", "ffn_w_up", "ffn_w_down"]
GATHERS = {
    "ev_in": [("ev_w_in", 0)],
    "xa0": [("ev_w_out", 0)] + [(n, 0) for n in _XA],
    "ffn0": [(n, 0) for n in _FFN],
    "od": [("od_w_in", 0), ("od_w_out", 0)],
    "xa1": [(n, 1) for n in _XA],
    "ffn1": [(n, 1) for n in _FFN],
}
SCATTERS = {
    "ffn1": [(n, 1) for n in _FFN],
    "xa1": [(n, 1) for n in _XA],
    "od": [("od_w_in", 0), ("od_w_out", 0)],
    "ffn0": [(n, 0) for n in _FFN],
    "xa0": [(n, 0) for n in _XA],
    "ev_out": [("ev_w_out", 0)],
    "ev_in": [("ev_w_in", 0)],
}


def _local_step(x, mem, loss_target, W, comm):
    grads = {}

    h0 = x
    (ev_w_in_t,), token = comm.weights("ev_in", None)
    n0 = _rms_fwd(h0, _row(W["g_mix"][0]), "ev_rms", after=token)
    z = _mm([(n0, ev_w_in_t, "nt")], "ev_in", tn=1280)
    token = comm.prefetch(["ffn0"], z)
    ab, ca = _conv_fwd(z, W["ev_a_conv_w"][0], W["ev_a_conv_b"], W["ev_a_ln_g"], W["ev_a_ln_b"],
                       W["ev_b_conv_w"][0], W["ev_b_conv_b"], "ev_conv", after=token)
    (ev_w_out, *xa_w0), _ = comm.weights("xa0", ab)
    h1, nq0 = _mm([(ab, ev_w_out, "nn")], "ev_out", res=h0, rms_g=_row(W["g_xattn"][0]))
    token = comm.prefetch(["od", "xa1"], nq0)
    h2, nf0, xa0 = _xattn_fwd(h1, nq0, mem, W["g_mem"][0], *xa_w0, W["g_ffn"][0], "l0", token)
    ffn_w0, _ = comm.weights("ffn0", nf0)
    token = comm.prefetch(["ffn1"], nf0)
    h3, n3, ff0 = _ffn_fwd(h2, nf0, *ffn_w0, W["g_mix"][1], "l0", token)

    (od_w_in_t, od_w_out), _ = comm.weights("od", n3)
    zp = _mm([(n3, od_w_in_t, "nt")], "od_in", tn=1024)
    D = x.shape[1]
    ws = W["od_w_s"][0].astype(BF16)
    wst = jnp.swapaxes(ws, 1, 2)
    bsb = jnp.repeat(jnp.transpose(W["od_b_s"][0]), D // C_GROUPS, axis=1)
    y_sgu = _sgu_fwd(zp, W["od_c_ln_g"], W["od_c_ln_b"], ws, bsb, "od_sgu")
    h4, nq1 = _mm([(y_sgu, od_w_out, "nn")], "od_out", res=h3, rms_g=_row(W["g_xattn"][1]))
    xa_w1, _ = comm.weights("xa1", nq1)
    h5, nf1, xa1 = _xattn_fwd(h4, nq1, mem, W["g_mem"][1], *xa_w1, W["g_ffn"][1], "l1", None)
    ffn_w1, _ = comm.weights("ffn1", nf1)
    h6, _, ff1 = _ffn_fwd(h5, nf1, *ffn_w1, None, "l1", None)

    loss_row, dh6, d_gfinal = _loss_bwd(h6, _row(W["g_final"]), loss_target, "loss")
    grads["g_final"] = d_gfinal.reshape(-1)

    dh5, g_ff1 = _ffn_bwd(dh6, ff1, W["g_ffn"][1], *ffn_w1, "l1", lambda dws: comm.grads("ffn1", dws))
    dh4, g_xa1 = _xattn_bwd(dh5, xa1, mem, W["g_xattn"][1], W["g_mem"][1], *xa_w1, "l1",
                            lambda dws: comm.grads("xa1", dws))
    dy_sgu = _mm([(dh4, od_w_out, "nt")], "od_dy", tn=1024)
    d_od_out = _mm_tn(y_sgu, dh4, "od_dwout", tn=1024)
    dzp, d_ws, d_bsb, d_clng, d_clnb = _sgu_bwd(dy_sgu, zp, W["od_c_ln_g"], W["od_c_ln_b"], ws, wst, bsb, "od_sgu_bwd")
    grads["od_w_s"] = d_ws[None]
    grads["od_b_s"] = jnp.transpose(_group_sum(d_bsb, C_GROUPS, "od_dbs"))[None]
    grads["od_c_ln_g"], grads["od_c_ln_b"] = d_clng, d_clnb
    token = comm.grads("od", [_mm_tn(dzp, n3, "od_dwin", ts=1024, tn=1024), d_od_out])
    dh3, d_gmix1 = _mm([(dzp, od_w_in_t, "nn")], "od_dn", rms_bwd=(h3, _row(W["g_mix"][1]), dh4), tm=1024, after=token)

    dh2, g_ff0 = _ffn_bwd(dh3, ff0, W["g_ffn"][0], *ffn_w0, "l0", lambda dws: comm.grads("ffn0", dws))
    dh1, g_xa0 = _xattn_bwd(dh2, xa0, mem, W["g_xattn"][0], W["g_mem"][0], *xa_w0, "l0",
                            lambda dws: comm.grads("xa0", dws))
    token = comm.grads("ev_out", [_mm_tn(ab, dh1, "ev_dwout", tn=1024)])
    dab = _mm([(dh1, ev_w_out, "nt")], "ev_dab", tn=1024, after=token)
    dca, d_lng, d_lnb, d_ba = _conv_bwd_ln(dab, ca, W["ev_a_ln_g"], W["ev_a_ln_b"], "ev_conv_bwd_ln")
    dz, d_wa, d_wb, d_bb = _conv_bwd(z, dca, dab, W["ev_a_conv_w"][0], W["ev_b_conv_w"][0], W["ev_b_conv_b"],
                                     "ev_conv_bwd")
    grads.update(ev_a_ln_g=d_lng, ev_a_ln_b=d_lnb, ev_a_conv_b=d_ba, ev_b_conv_b=d_bb,
                 ev_a_conv_w=d_wa[None], ev_b_conv_w=d_wb[None])
    token = comm.grads("ev_in", [_mm_tn(dz, n0, "ev_dwin", ts=1024, tn=1024)])
    grad_x, d_gmix0 = _mm([(dz, ev_w_in_t, "nn")], "ev_dn", rms_bwd=(h0, _row(W["g_mix"][0]), dh1), tm=1024, after=token)

    grads["g_mix"] = jnp.concatenate([d_gmix0, d_gmix1], axis=0)
    for key in ("g_xattn", "g_mem"):
        grads[key] = jnp.concatenate([g_xa0[key], g_xa1[key]], axis=0)
    grads["g_ffn"] = jnp.concatenate([g_ff0["g_ffn"], g_ff1["g_ffn"]], axis=0)
    return loss_row, grad_x, grads


class _Exchanges:
    def __init__(self, shards, dev_idx, after):
        self.shards, self.dev_idx = shards, dev_idx
        self.gathering, self.scattering = {}, {}
        self.first = _all_gather_weights(self._pack(GATHERS["ev_in"]), self._rows(GATHERS["ev_in"]), "ag_ev_in",
                                         after=after)
        self.first_token = self.prefetch(["xa0"], self.first[0])

    def _rows(self, entries):
        return [self.shards[e].shape[0] for e in entries]

    def _pack(self, entries):
        return jnp.concatenate([self.shards[e] for e in entries], axis=0)

    def prefetch(self, gathers, after):
        for name in gathers:
            rows = self._rows(GATHERS[name])
            pack = self._pack(GATHERS[name])
            lands = [lax.empty((NDEV * r, pack.shape[1]), pack.dtype) for r in rows]
            plan, n = _gather_plan(rows)
            send, recv, srcs, lands, after = _split_start([pack], lands, plan, n, after, f"ag_{name}_start")
            self.gathering[name] = (send, recv, srcs, lands, plan, rows)
        return after

    def weights(self, name, after):
        if name == "ev_in":
            return self.first, self.first_token
        send, recv, srcs, lands, plan, rows = self.gathering.pop(name)
        _, lands = _split_wait(send, recv, srcs, lands, plan, after, f"ag_{name}_wait")
        return _place_own([self.shards[e] for e in GATHERS[name]], lands, self.dev_idx, f"ag_{name}_own"), None

    def grads(self, name, dws):
        rows = self._rows(SCATTERS[name])
        lands = [lax.empty((NDEV - 1, r, d.shape[1]), d.dtype) for r, d in zip(rows, dws)]
        plan, n = _scatter_plan(rows)
        send, recv, srcs, lands, token = _split_start(dws, lands, plan, n, None, f"rs_{name}_start")
        self.scattering[name] = (send, recv, srcs, lands, plan)
        return token

    def received(self, after):
        out = {}
        for name, (send, recv, srcs, lands, plan) in self.scattering.items():
            srcs, lands = _split_wait(send, recv, srcs, lands, plan, after, f"rs_{name}_wait")
            for entry, g, got in zip(SCATTERS[name], srcs, lands):
                out[entry] = (g, got)
        return out


def kernel(x, mem, g_mix, g_xattn, g_mem, g_ffn, g_final, ev_w_in, ev_a_conv_w, ev_a_conv_b, ev_a_ln_g, ev_a_ln_b, ev_b_conv_w, ev_b_conv_b, ev_w_out, od_w_in, od_c_ln_g, od_c_ln_b, od_w_s, od_b_s, od_w_out, xa_w_q, xa_w_k, xa_w_v, xa_w_o, ffn_w_gate, ffn_w_up, ffn_w_down, loss_target, m_g_mix, m_g_xattn, m_g_mem, m_g_ffn, m_g_final, m_ev_w_in, m_ev_a_conv_w, m_ev_a_conv_b, m_ev_a_ln_g, m_ev_a_ln_b, m_ev_b_conv_w, m_ev_b_conv_b, m_ev_w_out, m_od_w_in, m_od_c_ln_g, m_od_c_ln_b, m_od_w_s, m_od_b_s, m_od_w_out, m_xa_w_q, m_xa_w_k, m_xa_w_v, m_xa_w_o, m_ffn_w_gate, m_ffn_w_up, m_ffn_w_down, v_g_mix, v_g_xattn, v_g_mem, v_g_ffn, v_g_final, v_ev_w_in, v_ev_a_conv_w, v_ev_a_conv_b, v_ev_a_ln_g, v_ev_a_ln_b, v_ev_b_conv_w, v_ev_b_conv_b, v_ev_w_out, v_od_w_in, v_od_c_ln_g, v_od_c_ln_b, v_od_w_s, v_od_b_s, v_od_w_out, v_xa_w_q, v_xa_w_k, v_xa_w_v, v_xa_w_o, v_ffn_w_gate, v_ffn_w_up, v_ffn_w_down):
    local = dict(g_mix=g_mix, g_xattn=g_xattn, g_mem=g_mem, g_ffn=g_ffn, g_final=g_final, ev_w_in=ev_w_in, ev_a_conv_w=ev_a_conv_w, ev_a_conv_b=ev_a_conv_b, ev_a_ln_g=ev_a_ln_g, ev_a_ln_b=ev_a_ln_b, ev_b_conv_w=ev_b_conv_w, ev_b_conv_b=ev_b_conv_b, ev_w_out=ev_w_out, od_w_in=od_w_in, od_c_ln_g=od_c_ln_g, od_c_ln_b=od_c_ln_b, od_w_s=od_w_s, od_b_s=od_b_s, od_w_out=od_w_out, xa_w_q=xa_w_q, xa_w_k=xa_w_k, xa_w_v=xa_w_v, xa_w_o=xa_w_o, ffn_w_gate=ffn_w_gate, ffn_w_up=ffn_w_up, ffn_w_down=ffn_w_down)
    mom = dict(g_mix=m_g_mix, g_xattn=m_g_xattn, g_mem=m_g_mem, g_ffn=m_g_ffn, g_final=m_g_final, ev_w_in=m_ev_w_in, ev_a_conv_w=m_ev_a_conv_w, ev_a_conv_b=m_ev_a_conv_b, ev_a_ln_g=m_ev_a_ln_g, ev_a_ln_b=m_ev_a_ln_b, ev_b_conv_w=m_ev_b_conv_w, ev_b_conv_b=m_ev_b_conv_b, ev_w_out=m_ev_w_out, od_w_in=m_od_w_in, od_c_ln_g=m_od_c_ln_g, od_c_ln_b=m_od_c_ln_b, od_w_s=m_od_w_s, od_b_s=m_od_b_s, od_w_out=m_od_w_out, xa_w_q=m_xa_w_q, xa_w_k=m_xa_w_k, xa_w_v=m_xa_w_v, xa_w_o=m_xa_w_o, ffn_w_gate=m_ffn_w_gate, ffn_w_up=m_ffn_w_up, ffn_w_down=m_ffn_w_down)
    vel = dict(g_mix=v_g_mix, g_xattn=v_g_xattn, g_mem=v_g_mem, g_ffn=v_g_ffn, g_final=v_g_final, ev_w_in=v_ev_w_in, ev_a_conv_w=v_ev_a_conv_w, ev_a_conv_b=v_ev_a_conv_b, ev_a_ln_g=v_ev_a_ln_g, ev_a_ln_b=v_ev_a_ln_b, ev_b_conv_w=v_ev_b_conv_w, ev_b_conv_b=v_ev_b_conv_b, ev_w_out=v_ev_w_out, od_w_in=v_od_w_in, od_c_ln_g=v_od_c_ln_g, od_c_ln_b=v_od_c_ln_b, od_w_s=v_od_w_s, od_b_s=v_od_b_s, od_w_out=v_od_w_out, xa_w_q=v_xa_w_q, xa_w_k=v_xa_w_k, xa_w_v=v_xa_w_v, xa_w_o=v_xa_w_o, ffn_w_gate=v_ffn_w_gate, ffn_w_up=v_ffn_w_up, ffn_w_down=v_ffn_w_down)
    D = x.shape[-1]
    dev = 4 * lax.axis_index("x") + 2 * lax.axis_index("y") + lax.axis_index("c")

    def comm_layout(n, a):
        return jnp.transpose(a) if _shard_axis(n) == 2 else a

    shards = {(n, i): comm_layout(n, local[n][i]).astype(BF16) for n in BIG for i in range(local[n].shape[0])}
    small_sizes = [local[n].size for n in SMALL_SHARDED]
    small_block = _pad_rows(jnp.concatenate([local[n].reshape(-1) for n in SMALL_SHARDED]), 128, 8)
    small_all = _all_gather(small_block, "ag_small")
    comm = _Exchanges(shards, jnp.reshape(dev, (1,)).astype(jnp.int32), small_all)
    small_all = small_all.reshape(NDEV, -1)

    W = {n: local[n] for n in REPLICATED}
    o0 = 0
    for n, sz in zip(SMALL_SHARDED, small_sizes):
        blocks = small_all[:, o0:o0 + sz].reshape((NDEV,) + local[n].shape)
        W[n] = _full_from_blocks(blocks, _shard_axis(n))
        o0 += sz

    loss_row, grad_x, grads = _local_step(x[0], mem[0], loss_target[0], W, comm)

    received = comm.received(grad_x)
    rest = REPLICATED + SMALL_SHARDED
    rest_full_shapes = [grads[n].shape for n in rest]
    g_rest = _pad_rows(jnp.concatenate([grads[n].astype(F32).reshape(-1) for n in rest]), D, 8)
    small_rows = g_rest.shape[0]
    small_plan, small_n = _gather_plan([small_rows])
    small_send, small_recv, small_srcs, small_lands, token = _split_start(
        [g_rest], [lax.empty((NDEV * small_rows, D), F32)], small_plan, small_n, received["ev_w_in", 0][1],
        "ag_small_grads_start")

    gsh, delta, new_m, new_v = {}, {}, {}, {}
    def stacked_layout(n, a):
        return jnp.swapaxes(a, 1, 2) if _shard_axis(n) == 2 else a

    for n in BIG:
        parts = [received[n, i] for i in range(local[n].shape[0])]
        outs = _finish_weight([p[0] for p in parts], [p[1] for p in parts], comm.dev_idx,
                              *(stacked_layout(n, a) for a in (local[n], mom[n], vel[n])), f"finish_{n}", after=token)
        gsh[n], delta[n], new_m[n], new_v[n] = (stacked_layout(n, o) for o in outs)

    _, small_lands = _split_wait(small_send, small_recv, small_srcs, small_lands, small_plan,
                                 [delta[n] for n in BIG], "ag_small_grads_wait")
    partials = _place_own([g_rest], small_lands, comm.dev_idx, "ag_small_grads_own")[0]
    g_rest = _sum_slots(partials.reshape(NDEV, small_rows, D), "sum_small_grads").reshape(-1)
    o0 = 0
    for n, shp in zip(rest, rest_full_shapes):
        sz = 1
        for s in shp:
            sz *= s
        full = g_rest[o0:o0 + sz].reshape(shp)
        o0 += sz
        if n in SMALL_SHARDED:
            full = lax.dynamic_index_in_dim(_blocks_from_full(full, _shard_axis(n)), dev, 0, keepdims=False)
        gsh[n] = full.reshape(local[n].shape)

    small = _adamw_small([local[n] for n in rest], [gsh[n] for n in rest], [mom[n] for n in rest],
                         [vel[n] for n in rest], "adamw_small")
    for n, (d, nm, nv) in zip(rest, small):
        delta[n], new_m[n], new_v[n] = d, nm, nv

    loss = lax.psum(loss_row[0, 0], ("x", "y", "c"))
    return (loss, grad_x[None], *[gsh[n] for n in WEIGHTS], *[delta[n] for n in WEIGHTS],
            *[new_m[n] for n in WEIGHTS], *[new_v[n] for n in WEIGHTS])
```

```python
import jax
import jax.numpy as jnp
from jax import lax
from jax.experimental import pallas as pl
from jax.experimental.pallas import tpu as pltpu

F32, BF16 = jnp.float32, jnp.bfloat16
NDEV = 8
RMS_EPS = 1e-6
LN_EPS = 1e-5
CHUNK = 128
C_GROUPS = 8
XA_HEADS = 4
ADAM_LR, ADAM_B1, ADAM_B2, ADAM_EPS, ADAM_WD, ADAM_STEP = 0.001, 0.9, 0.999, 1e-08, 0.01, 10
HALO = 16
ROW_CHUNK = 32
V7X_VMEM_LIMIT = 56 * 1024 * 1024
MESH = pl.DeviceIdType.MESH

TS_ROW = 512
TS_MM = 2048
TN_MM = 1408
TS_FFN = 512
MM_ROW_CHUNK = 256
TS_CONV = 512
TS_SGU = 512
TS_ATTN = 2048


def _cp(*sem):
    return pltpu.CompilerParams(dimension_semantics=sem, vmem_limit_bytes=V7X_VMEM_LIMIT)


def _pick(n, pref, align):
    for t in range(min(n, pref), 0, -1):
        if n % t == 0 and (t % align == 0 or t == n):
            return t
    return n


def _sigmoid(x):
    return 0.5 * jnp.tanh(0.5 * x) + 0.5


def _dot(a, b):
    return jnp.dot(a, b, preferred_element_type=F32)


def _dot_nt(a, b):
    return lax.dot_general(a, b, (((1,), (1,)), ((), ())), preferred_element_type=F32)


def _dot_tn(a, b):
    return lax.dot_general(a, b, (((0,), (0,)), ((), ())), preferred_element_type=F32)


_ANY = pl.BlockSpec(memory_space=pl.ANY)
_RESIDENT = pl.Buffered(1)


def _after(after):
    if after is None:
        return [], []
    ops = list(after) if isinstance(after, (list, tuple)) else [after]
    return [_ANY] * len(ops), ops


def _rms_fwd(h, g, name, after=None):
    S, D = h.shape
    ts = _pick(S, TS_MM, 16)
    after_specs, after_ops = _after(after)

    def body(h_ref, g_ref, *rest):
        o_ref = rest[-1]
        x = h_ref[...]
        r = lax.rsqrt(jnp.mean(x * x, axis=-1, keepdims=True) + RMS_EPS)
        o_ref[...] = ((x * r) * g_ref[...]).astype(o_ref.dtype)

    return pl.pallas_call(
        body, grid=(S // ts,),
        in_specs=[pl.BlockSpec((ts, D), lambda i: (i, 0)), pl.BlockSpec((1, D), lambda i: (0, 0))] + after_specs,
        out_specs=pl.BlockSpec((ts, D), lambda i: (i, 0)),
        out_shape=jax.ShapeDtypeStruct((S, D), BF16), compiler_params=_cp("parallel"), name=name)(h, g, *after_ops)


def _rms_bwd(dn, h, g, dres, name):
    S, D = h.shape
    ts = _pick(S, TS_ROW, 8)
    has_res = dres is not None

    def body(*refs):
        if has_res:
            dn_ref, h_ref, g_ref, dres_ref, dh_ref, dg_ref = refs
        else:
            dn_ref, h_ref, g_ref, dh_ref, dg_ref = refs
        x = h_ref[...]
        dn_ = dn_ref[...].astype(F32)
        r = lax.rsqrt(jnp.mean(x * x, axis=-1, keepdims=True) + RMS_EPS)
        xr = x * r

        @pl.when(pl.program_id(0) == 0)
        def _():
            dg_ref[...] = jnp.zeros_like(dg_ref)

        dg_ref[...] += jnp.sum(dn_ * xr, axis=0, keepdims=True)
        u = dn_ * g_ref[...]
        dh = r * u - xr * (r * jnp.mean(u * xr, axis=-1, keepdims=True))
        if has_res:
            dh = dh + dres_ref[...]
        dh_ref[...] = dh

    tile = pl.BlockSpec((ts, D), lambda i: (i, 0))
    vec = pl.BlockSpec((1, D), lambda i: (0, 0))
    ins = [dn, h, g] + ([dres] if has_res else [])
    return pl.pallas_call(
        body, grid=(S // ts,),
        in_specs=[tile, tile, vec] + ([tile] if has_res else []),
        out_specs=[tile, vec],
        out_shape=[jax.ShapeDtypeStruct((S, D), F32), jax.ShapeDtypeStruct((1, D), F32)],
        compiler_params=_cp("arbitrary"), name=name)(*ins)


def _mm(pairs, name, out_dtype=F32, res=None, rms_g=None, rms_bwd=None, loss=None, tm=None, tn=None, after=None):
    M = pairs[0][0].shape[0]
    N = pairs[0][1].shape[1 if pairs[0][2] == "nn" else 0]
    whole_rows = rms_g is not None or rms_bwd is not None or loss is not None
    tm = _pick(M, tm or TS_MM, 16)
    tn = N if whole_rows else _pick(N, tn or TN_MM, 128)
    npair = len(pairs)
    modes = [p[2] for p in pairs]
    after_specs, after_ops = _after(after)

    rc = MM_ROW_CHUNK if whole_rows and tm % MM_ROW_CHUNK == 0 else tm

    def body(*refs):
        rest = refs[2 * npair + len(after_ops):]
        res_ref = None
        if res is not None:
            res_ref, rest = rest[0], rest[1:]
        if rms_bwd is not None:
            dg_ref = rest[4]

            @pl.when(pl.program_id(0) == 0)
            def _():
                dg_ref[...] = jnp.zeros_like(dg_ref)

        if loss is not None:
            g_ref, t_ref, loss_ref, dh_ref, dg_ref = rest

            @pl.when(pl.program_id(0) == 0)
            def _():
                dg_ref[...] = jnp.zeros_like(dg_ref)
                loss_ref[...] = jnp.zeros_like(loss_ref)

        for r0 in range(0, tm, rc):
            rows = pl.ds(r0, rc)
            acc = None
            for p in range(npair):
                a_ = refs[2 * p][rows, :].astype(BF16)
                d = _dot(a_, refs[2 * p + 1][...]) if modes[p] == "nn" else _dot_nt(a_, refs[2 * p + 1][...])
                acc = d if acc is None else acc + d
            if res_ref is not None:
                acc = acc + res_ref[rows, :]
            if rms_bwd is not None:
                h_ref, g_ref, dres_ref, dh_ref, _ = rest
                x = h_ref[rows, :]
                r = lax.rsqrt(jnp.mean(x * x, axis=-1, keepdims=True) + RMS_EPS)
                xr = x * r
                dg_ref[...] += jnp.sum(acc * xr, axis=0, keepdims=True)
                u = acc * g_ref[...]
                dh_ref[rows, :] = r * u - xr * (r * jnp.mean(u * xr, axis=-1, keepdims=True)) + dres_ref[rows, :]
            elif loss is not None:
                r = lax.rsqrt(jnp.mean(acc * acc, axis=-1, keepdims=True) + RMS_EPS)
                xr = acc * r
                gg = g_ref[...]
                e = xr * gg - t_ref[rows, :]
                chunk_loss = jnp.sum(jnp.sum(e * e, axis=0, keepdims=True), axis=1, keepdims=True) * (0.5 / N)
                loss_ref[...] += jnp.broadcast_to(chunk_loss, loss_ref.shape)
                dy = e * (1.0 / N)
                dg_ref[...] += jnp.sum(dy * xr, axis=0, keepdims=True)
                u = dy * gg
                dh_ref[rows, :] = r * u - xr * (r * jnp.mean(u * xr, axis=-1, keepdims=True))
            elif rms_g is not None:
                g_ref, o_ref, n_ref = rest
                o_ref[rows, :] = acc
                r = lax.rsqrt(jnp.mean(acc * acc, axis=-1, keepdims=True) + RMS_EPS)
                n_ref[rows, :] = ((acc * r) * g_ref[...]).astype(BF16)
            else:
                rest[0][rows, :] = acc.astype(rest[0].dtype)

    in_specs, ins = [], []
    for a, w, mode in pairs:
        K = a.shape[1]
        in_specs.append(pl.BlockSpec((tm, K), lambda i, j: (i, 0)))
        once = _RESIDENT if tn == N else None
        in_specs.append(pl.BlockSpec((K, tn), lambda i, j: (0, j), pipeline_mode=once) if mode == "nn"
                        else pl.BlockSpec((tn, K), lambda i, j: (j, 0), pipeline_mode=once))
        ins += [a, w]
    in_specs += after_specs
    ins += after_ops
    tile = pl.BlockSpec((tm, tn), lambda i, j: (i, j))
    vec = pl.BlockSpec((1, tn), lambda i, j: (0, j))
    if res is not None:
        in_specs.append(tile)
        ins.append(res)
    sem = ("parallel", "parallel")
    if rms_bwd is not None:
        in_specs += [tile, vec, tile]
        ins += list(rms_bwd)
        out_specs = [tile, vec]
        out_shape = [jax.ShapeDtypeStruct((M, N), F32), jax.ShapeDtypeStruct((1, N), F32)]
        sem = ("arbitrary", "arbitrary")
    elif loss is not None:
        in_specs += [vec, tile]
        ins += list(loss)
        out_specs = [pl.BlockSpec((1, 128), lambda i, j: (0, 0)), tile, vec]
        out_shape = [jax.ShapeDtypeStruct((1, 128), F32), jax.ShapeDtypeStruct((M, N), F32),
                     jax.ShapeDtypeStruct((1, N), F32)]
        sem = ("arbitrary", "arbitrary")
    elif rms_g is not None:
        in_specs.append(vec)
        ins.append(rms_g)
        out_specs = [tile, tile]
        out_shape = [jax.ShapeDtypeStruct((M, N), F32), jax.ShapeDtypeStruct((M, N), BF16)]
    else:
        out_specs = tile
        out_shape = jax.ShapeDtypeStruct((M, N), out_dtype)
    return pl.pallas_call(
        body, grid=(M // tm, N // tn), in_specs=in_specs, out_specs=out_specs, out_shape=out_shape,
        compiler_params=_cp(*sem), name=name)(*ins)


def _mm_tn(a, b, name, ts=None, tn=None):
    S, K = a.shape
    N = b.shape[1]
    ts = _pick(S, ts or TS_MM, 16)
    tn = _pick(N, tn or TN_MM, 128)
    nsteps = S // ts

    def body(a_ref, b_ref, o_ref, acc_ref):
        s = pl.program_id(1)

        @pl.when(s == 0)
        def _():
            acc_ref[...] = jnp.zeros_like(acc_ref)

        acc_ref[...] += _dot_tn(a_ref[...].astype(BF16), b_ref[...].astype(BF16))

        @pl.when(s == nsteps - 1)
        def _():
            o_ref[...] = acc_ref[...].astype(o_ref.dtype)

    return pl.pallas_call(
        body, grid=(N // tn, nsteps),
        in_specs=[pl.BlockSpec((ts, K), lambda j, s: (s, 0)), pl.BlockSpec((ts, tn), lambda j, s: (s, j))],
        out_specs=pl.BlockSpec((K, tn), lambda j, s: (0, j)), out_shape=jax.ShapeDtypeStruct((K, N), BF16),
        scratch_shapes=[pltpu.VMEM((K, tn), F32)],
        compiler_params=_cp("parallel", "arbitrary"), name=name)(a, b)


def _col_chunk(n):
    return 256 if n % 256 == 0 else 128


def _ffn_up(n, wgt, wut, name, after=None):
    S, D = n.shape
    F = wgt.shape[0]
    tm = _pick(S, TS_FFN, 16)
    ce = _col_chunk(F)
    after_specs, after_ops = _after(after)

    def body(n_ref, wg_ref, wu_ref, *rest):
        a_ref, b_ref, hid_ref = rest[-3:]
        x = n_ref[...]
        for c0 in range(0, F, ce):
            a = _dot_nt(x, wg_ref[c0:c0 + ce, :])
            b = _dot_nt(x, wu_ref[c0:c0 + ce, :])
            a_ref[:, c0:c0 + ce] = a.astype(BF16)
            b_ref[:, c0:c0 + ce] = b.astype(BF16)
            hid_ref[:, c0:c0 + ce] = (a * _sigmoid(a) * b).astype(BF16)

    wspec = pl.BlockSpec((F, D), lambda i: (0, 0), pipeline_mode=_RESIDENT)
    ospec = pl.BlockSpec((tm, F), lambda i: (i, 0))
    osh = jax.ShapeDtypeStruct((S, F), BF16)
    return pl.pallas_call(
        body, grid=(S // tm,),
        in_specs=[pl.BlockSpec((tm, D), lambda i: (i, 0)), wspec, wspec] + after_specs,
        out_specs=[ospec, ospec, ospec], out_shape=[osh, osh, osh],
        compiler_params=_cp("parallel"), name=name)(n, wgt, wut, *after_ops)


def _ffn_dhid(dh, wd, a, b, name):
    S, D = dh.shape
    F = wd.shape[0]
    tm = _pick(S, TS_FFN, 16)
    ce = _col_chunk(F)

    def body(dh_ref, wd_ref, a_ref, b_ref, da_ref, db_ref):
        x = dh_ref[...].astype(BF16)
        for c0 in range(0, F, ce):
            g = _dot_nt(x, wd_ref[c0:c0 + ce, :]).astype(BF16)
            a_ = a_ref[:, c0:c0 + ce]
            sg = _sigmoid(a_)
            silu = a_ * sg
            da_ref[:, c0:c0 + ce] = (g * b_ref[:, c0:c0 + ce]) * (sg + silu * (1.0 - sg))
            db_ref[:, c0:c0 + ce] = g * silu

    tile = pl.BlockSpec((tm, F), lambda i: (i, 0))
    osh = jax.ShapeDtypeStruct((S, F), BF16)
    return pl.pallas_call(
        body, grid=(S // tm,),
        in_specs=[pl.BlockSpec((tm, D), lambda i: (i, 0)),
                  pl.BlockSpec((F, D), lambda i: (0, 0), pipeline_mode=_RESIDENT), tile, tile],
        out_specs=[tile, tile], out_shape=[osh, osh],
        compiler_params=_cp("parallel"), name=name)(dh, wd, a, b)


def _softmax_rows(s):
    m = jnp.max(s, axis=-1, keepdims=True)
    p = jnp.exp(s - m)
    return p / jnp.sum(p, axis=-1, keepdims=True)


def _attn_fwd(q, k, v, name):
    S, D = q.shape
    M = k.shape[0]
    hd = D // XA_HEADS
    scale = hd ** -0.5
    ts = _pick(S, TS_ATTN, 16)

    def body(q_ref, k_ref, v_ref, o_ref):
        for h in range(XA_HEADS):
            sl = slice(h * hd, (h + 1) * hd)
            p = _softmax_rows(_dot_nt(q_ref[:, sl], k_ref[:, sl]) * scale)
            o_ref[:, sl] = _dot(p.astype(BF16), v_ref[:, sl]).astype(BF16)

    tile = pl.BlockSpec((ts, D), lambda i: (i, 0))
    memspec = pl.BlockSpec((M, D), lambda i: (0, 0))
    return pl.pallas_call(
        body, grid=(S // ts,), in_specs=[tile, memspec, memspec], out_specs=tile,
        out_shape=jax.ShapeDtypeStruct((S, D), BF16), compiler_params=_cp("parallel"), name=name)(q, k, v)


def _attn_bwd(q, k, v, do, name):
    S, D = q.shape
    M = k.shape[0]
    hd = D // XA_HEADS
    scale = hd ** -0.5
    ts = _pick(S, TS_ATTN, 16)

    def body(q_ref, k_ref, v_ref, do_ref, dq_ref, dk_ref, dv_ref):
        @pl.when(pl.program_id(0) == 0)
        def _():
            dk_ref[...] = jnp.zeros_like(dk_ref)
            dv_ref[...] = jnp.zeros_like(dv_ref)

        for h in range(XA_HEADS):
            sl = slice(h * hd, (h + 1) * hd)
            qh, kh, vh, doh = q_ref[:, sl], k_ref[:, sl], v_ref[:, sl], do_ref[:, sl]
            p = _softmax_rows(_dot_nt(qh, kh) * scale)
            dp = _dot_nt(doh, vh)
            dv_ref[:, sl] += _dot_tn(p.astype(BF16), doh)
            delta = jnp.sum(dp * p, axis=-1, keepdims=True)
            ds = (p * (dp - delta) * scale).astype(BF16)
            dq_ref[:, sl] = _dot(ds, kh).astype(BF16)
            dk_ref[:, sl] += _dot_tn(ds, qh)

    tile = pl.BlockSpec((ts, D), lambda i: (i, 0))
    memspec = pl.BlockSpec((M, D), lambda i: (0, 0))
    return pl.pallas_call(
        body, grid=(S // ts,), in_specs=[tile, memspec, memspec, tile], out_specs=[tile, memspec, memspec],
        out_shape=[jax.ShapeDtypeStruct((S, D), BF16), jax.ShapeDtypeStruct((M, D), F32),
                   jax.ShapeDtypeStruct((M, D), F32)],
        compiler_params=_cp("arbitrary"), name=name)(q, k, v, do)


def _halo_specs(ts, width, col):
    per = ts // HALO

    def prev(i):
        return (jnp.maximum(i * per - 1, 0), col)

    def nxt(i, n_tiles):
        return (jnp.minimum((i + 1) * per, n_tiles * per - 1), col)

    return prev, nxt


def _fill_ext(ext_ref, prev_val, main_val, next_val, first, last, ts):
    ext_ref[pl.ds(0, HALO), :] = jnp.where(first, 0.0, prev_val)
    ext_ref[pl.ds(HALO, ts), :] = main_val
    ext_ref[pl.ds(HALO + ts, HALO), :] = jnp.where(last, 0.0, next_val)


SUBLANES = 8


def _fill_shifted(sh_ref, ts):
    n = ts + 2 * HALO - SUBLANES
    for s in range(1, SUBLANES):
        sh_ref[s, pl.ds(0, n), :] = sh_ref[0, pl.ds(s, n), :]


def _tap(sh_ref, r0, offset, rc):
    q, s = divmod(offset, SUBLANES)
    return sh_ref[s, pl.ds(pl.multiple_of(r0 + SUBLANES * q, SUBLANES), rc), :]


def _conv_fwd(z, wa, ba, lng, lnb, wb, bb, name, after=None):
    S = z.shape[0]
    C = z.shape[1] // 5
    KA, KB = wa.shape[0], wb.shape[0]
    pa, pb = KA // 2, KB // 2
    assert pa <= HALO and pb <= HALO
    ts = _pick(S, TS_CONV, ROW_CHUNK)
    nt = S // ts
    rc = ROW_CHUNK
    prev, nxt = _halo_specs(ts, 5 * C, 0)
    after_specs, after_ops = _after(after)

    def body(*refs):
        compute(*refs[:9], *refs[9 + len(after_ops):])

    def compute(z_ref, zp_ref, zn_ref, wa_ref, ba_ref, lng_ref, lnb_ref, wb_ref, bb_ref, ab_ref, ca_ref,
                ga_sh, tb_ext, win_b):
        i = pl.program_id(0)
        first, last = i == 0, i == nt - 1

        def glu(r):
            return r[:, 0:C] * _sigmoid(r[:, C:2 * C])

        def gcb(r):
            return r[:, 4 * C:5 * C] * r[:, 2 * C:3 * C]

        _fill_ext(ga_sh.at[0], glu(zp_ref), glu(z_ref), glu(zn_ref), first, last, ts)
        _fill_shifted(ga_sh, ts)
        _fill_ext(tb_ext, gcb(zp_ref), gcb(z_ref), gcb(zn_ref), first, last, ts)

        def chunk(c, carry):
            r0 = pl.multiple_of(c * rc, rc)
            win_b[...] = tb_ext[pl.ds(r0, rc + 2 * HALO), :]
            acc = jnp.zeros((rc, C), F32)
            for k in range(KA):
                acc = acc + wa_ref[k:k + 1, :] * _tap(ga_sh, r0, HALO - pa + k, rc)
            ca = acc + ba_ref[...]
            ca_ref[pl.ds(r0, rc), :] = ca
            mu = jnp.mean(ca, axis=-1, keepdims=True)
            xc = ca - mu
            var = jnp.mean(xc * xc, axis=-1, keepdims=True)
            ln = xc * lax.rsqrt(var + LN_EPS) * lng_ref[...] + lnb_ref[...]
            ab_ref[pl.ds(r0, rc), 0:C] = (ln * _sigmoid(ln)).astype(BF16)
            cb = jnp.zeros((rc, C), F32) + bb_ref[...]
            for k in range(KB):
                cb = cb + wb_ref[k:k + 1, :] * win_b[pl.ds(HALO - pb + k, rc), :]
            ab_ref[pl.ds(r0, rc), C:2 * C] = (z_ref[pl.ds(r0, rc), 3 * C:4 * C] * cb).astype(BF16)
            return carry

        lax.fori_loop(0, ts // rc, chunk, 0)

    zspec = pl.BlockSpec((ts, 5 * C), lambda i: (i, 0))
    zprev = pl.BlockSpec((HALO, 5 * C), prev)
    znext = pl.BlockSpec((HALO, 5 * C), lambda i: nxt(i, nt))

    def full(a):
        return pl.BlockSpec(a.shape, lambda i: (0, 0))

    return pl.pallas_call(
        body, grid=(nt,),
        in_specs=[zspec, zprev, znext, full(wa), full(ba), full(lng), full(lnb), full(wb), full(bb)] + after_specs,
        out_specs=[pl.BlockSpec((ts, 2 * C), lambda i: (i, 0)), pl.BlockSpec((ts, C), lambda i: (i, 0))],
        out_shape=[jax.ShapeDtypeStruct((S, 2 * C), BF16), jax.ShapeDtypeStruct((S, C), F32)],
        scratch_shapes=[pltpu.VMEM((SUBLANES, ts + 2 * HALO, C), F32), pltpu.VMEM((ts + 2 * HALO, C), F32),
                        pltpu.VMEM((rc + 2 * HALO, C), F32)],
        compiler_params=_cp("parallel"), name=name)(z, z, z, wa, ba, lng, lnb, wb, bb, *after_ops)


def _conv_bwd_ln(dab, ca, lng, lnb, name):
    S, C = ca.shape
    ts = _pick(S, TS_ROW, 8)

    def body(da_ref, ca_ref, lng_ref, lnb_ref, dca_ref, dg_ref, db_ref, dbias_ref):
        @pl.when(pl.program_id(0) == 0)
        def _():
            dg_ref[...] = jnp.zeros_like(dg_ref)
            db_ref[...] = jnp.zeros_like(db_ref)
            dbias_ref[...] = jnp.zeros_like(dbias_ref)

        ca_ = ca_ref[...]
        mu = jnp.mean(ca_, axis=-1, keepdims=True)
        xc = ca_ - mu
        rstd = lax.rsqrt(jnp.mean(xc * xc, axis=-1, keepdims=True) + LN_EPS)
        xh = xc * rstd
        ln = xh * lng_ref[...] + lnb_ref[...]
        sg = _sigmoid(ln)
        dln = da_ref[...].astype(F32) * (sg * (1.0 + ln * (1.0 - sg)))
        dg_ref[...] += jnp.sum(dln * xh, axis=0, keepdims=True)
        db_ref[...] += jnp.sum(dln, axis=0, keepdims=True)
        dxh = dln * lng_ref[...]
        dca = rstd * (dxh - jnp.mean(dxh, axis=-1, keepdims=True) - xh * jnp.mean(dxh * xh, axis=-1, keepdims=True))
        dca_ref[...] = dca
        dbias_ref[...] += jnp.sum(dca, axis=0, keepdims=True)

    tile = pl.BlockSpec((ts, C), lambda i: (i, 0))
    vec = pl.BlockSpec((1, C), lambda i: (0, 0))
    vsh = jax.ShapeDtypeStruct((1, C), F32)
    return pl.pallas_call(
        body, grid=(S // ts,), in_specs=[tile, tile, vec, vec], out_specs=[tile, vec, vec, vec],
        out_shape=[jax.ShapeDtypeStruct((S, C), F32), vsh, vsh, vsh],
        compiler_params=_cp("arbitrary"), name=name)(dab, ca, lng, lnb)


def _conv_bwd(z, dca, dab, wa, wb, bb, name):
    S = z.shape[0]
    C = z.shape[1] // 5
    KA, KB = wa.shape[0], wb.shape[0]
    pa, pb = KA // 2, KB // 2
    ts = _pick(S, TS_CONV, ROW_CHUNK)
    nt = S // ts
    rc = ROW_CHUNK
    prev0, nxt0 = _halo_specs(ts, C, 0)
    prev1, nxt1 = _halo_specs(ts, C, 1)

    def body(z_ref, zp_ref, zn_ref, dca_ref, dcap_ref, dcan_ref, db_ref, dbp_ref, dbn_ref, wa_ref, wb_ref, bb_ref,
             dz_ref, dwa_ref, dwb_ref, dbb_ref,
             ga_sh, dca_sh, tb_ext, dcb_ext, win_tb, win_dcb, acc_a, acc_b, acc_bias):
        i = pl.program_id(0)
        first, last = i == 0, i == nt - 1

        @pl.when(first)
        def _():
            acc_a[...] = jnp.zeros_like(acc_a)
            acc_b[...] = jnp.zeros_like(acc_b)
            acc_bias[...] = jnp.zeros_like(acc_bias)

        def glu(r):
            return r[:, 0:C] * _sigmoid(r[:, C:2 * C])

        def gcb(r):
            return r[:, 4 * C:5 * C] * r[:, 2 * C:3 * C]

        def dcb(d, r):
            return d[...].astype(F32) * r[:, 3 * C:4 * C]

        _fill_ext(ga_sh.at[0], glu(zp_ref), glu(z_ref), glu(zn_ref), first, last, ts)
        _fill_shifted(ga_sh, ts)
        _fill_ext(dca_sh.at[0], dcap_ref[...], dca_ref[...], dcan_ref[...], first, last, ts)
        _fill_shifted(dca_sh, ts)
        _fill_ext(tb_ext, gcb(zp_ref), gcb(z_ref), gcb(zn_ref), first, last, ts)
        _fill_ext(dcb_ext, dcb(dbp_ref, zp_ref), dcb(db_ref, z_ref), dcb(dbn_ref, zn_ref), first, last, ts)

        def fold(x):
            return jnp.sum(x.reshape(rc // 8, 8, C), axis=0)

        def chunk(c, carry):
            r0 = pl.multiple_of(c * rc, rc)
            win_tb[...] = tb_ext[pl.ds(r0, rc + 2 * HALO), :]
            win_dcb[...] = dcb_ext[pl.ds(r0, rc + 2 * HALO), :]
            dca_c = _tap(dca_sh, r0, HALO, rc)
            dglu = jnp.zeros((rc, C), F32)
            for k in range(KA):
                dglu = dglu + wa_ref[k:k + 1, :] * _tap(dca_sh, r0, HALO + pa - k, rc)
                acc_a[k] += fold(dca_c * _tap(ga_sh, r0, HALO - pa + k, rc))
            val = z_ref[pl.ds(r0, rc), 0:C]
            sg = _sigmoid(z_ref[pl.ds(r0, rc), C:2 * C])
            dz_ref[pl.ds(r0, rc), 0:C] = (dglu * sg).astype(BF16)
            dz_ref[pl.ds(r0, rc), C:2 * C] = (dglu * val * sg * (1.0 - sg)).astype(BF16)
            dcb_c = win_dcb[pl.ds(HALO, rc), :]
            cb = jnp.zeros((rc, C), F32) + bb_ref[...]
            dt = jnp.zeros((rc, C), F32)
            for k in range(KB):
                tb_k = win_tb[pl.ds(HALO - pb + k, rc), :]
                cb = cb + wb_ref[k:k + 1, :] * tb_k
                dt = dt + wb_ref[k:k + 1, :] * win_dcb[pl.ds(HALO + pb - k, rc), :]
                acc_b[k] += fold(dcb_c * tb_k)
            acc_bias[...] += fold(dcb_c)
            db_c = db_ref[pl.ds(r0, rc), :].astype(F32)
            dz_ref[pl.ds(r0, rc), 2 * C:3 * C] = (dt * z_ref[pl.ds(r0, rc), 4 * C:5 * C]).astype(BF16)
            dz_ref[pl.ds(r0, rc), 3 * C:4 * C] = (db_c * cb).astype(BF16)
            dz_ref[pl.ds(r0, rc), 4 * C:5 * C] = (dt * z_ref[pl.ds(r0, rc), 2 * C:3 * C]).astype(BF16)
            return carry

        lax.fori_loop(0, ts // rc, chunk, 0)

        @pl.when(last)
        def _():
            dwa_ref[...] = jnp.sum(acc_a[...], axis=1)
            dwb_ref[...] = jnp.sum(acc_b[...], axis=1)
            dbb_ref[...] = jnp.sum(acc_bias[...], axis=0, keepdims=True)

    zspec = pl.BlockSpec((ts, 5 * C), lambda i: (i, 0))
    zprev = pl.BlockSpec((HALO, 5 * C), prev0)
    znext = pl.BlockSpec((HALO, 5 * C), lambda i: nxt0(i, nt))
    dspec = pl.BlockSpec((ts, C), lambda i: (i, 0))
    dprev = pl.BlockSpec((HALO, C), prev0)
    dnext = pl.BlockSpec((HALO, C), lambda i: nxt0(i, nt))
    bspec = pl.BlockSpec((ts, C), lambda i: (i, 1))
    bprev = pl.BlockSpec((HALO, C), prev1)
    bnext = pl.BlockSpec((HALO, C), lambda i: nxt1(i, nt))

    def full(shape):
        return pl.BlockSpec(shape, lambda i: (0,) * len(shape))

    ext = pltpu.VMEM((ts + 2 * HALO, C), F32)
    shifted = pltpu.VMEM((SUBLANES, ts + 2 * HALO, C), F32)
    win = pltpu.VMEM((rc + 2 * HALO, C), F32)
    return pl.pallas_call(
        body, grid=(nt,),
        in_specs=[zspec, zprev, znext, dspec, dprev, dnext, bspec, bprev, bnext,
                  full(wa.shape), full(wb.shape), full(bb.shape)],
        out_specs=[pl.BlockSpec((ts, 5 * C), lambda i: (i, 0)), full((KA, C)), full((KB, C)), full((1, C))],
        out_shape=[jax.ShapeDtypeStruct((S, 5 * C), BF16), jax.ShapeDtypeStruct((KA, C), F32),
                   jax.ShapeDtypeStruct((KB, C), F32), jax.ShapeDtypeStruct((1, C), F32)],
        scratch_shapes=[shifted, shifted, ext, ext, win, win,
                        pltpu.VMEM((KA, 8, C), F32), pltpu.VMEM((KB, 8, C), F32), pltpu.VMEM((8, C), F32)],
        compiler_params=_cp("arbitrary"), name=name)(z, z, z, dca, dca, dca, dab, dab, dab, wa, wb, bb)


_GELU_C = 0.7978845608028654
_GELU_A = 0.044715


def _gelu(x):
    return 0.5 * x * (1.0 + jnp.tanh(_GELU_C * (x + _GELU_A * (x * x * x))))


def _gelu_and_grad(x):
    t = jnp.tanh(_GELU_C * (x + _GELU_A * (x * x * x)))
    hx = 0.5 * x
    return hx * (1.0 + t), 0.5 * (1.0 + t) + hx * (1.0 - t * t) * (_GELU_C * (1.0 + 3.0 * _GELU_A * x * x))


def _sgu_fwd(zp, lng, lnb, ws, bsb, name):
    S = zp.shape[0]
    D = zp.shape[1] // 2
    G = ws.shape[0]
    gd = D // G
    ts = _pick(S, TS_SGU, CHUNK)
    ncs = ts // CHUNK

    def body(zp_ref, lng_ref, lnb_ref, ws_ref, bsb_ref, y_ref, vb_ref):
        v = _gelu(zp_ref[:, D:2 * D])
        mu = jnp.mean(v, axis=-1, keepdims=True)
        xc = v - mu
        rstd = lax.rsqrt(jnp.mean(xc * xc, axis=-1, keepdims=True) + LN_EPS)
        vb_ref[...] = (xc * rstd * lng_ref[...] + lnb_ref[...]).astype(BF16)
        for c in range(ncs):
            rows = slice(c * CHUNK, (c + 1) * CHUNK)
            for g in range(G):
                cols = slice(g * gd, (g + 1) * gd)
                sv = _dot(ws_ref[g], vb_ref[rows, cols]) + bsb_ref[:, cols]
                y_ref[rows, cols] = (_gelu(zp_ref[rows, cols]) * sv).astype(BF16)

    def full(a):
        return pl.BlockSpec(a.shape, lambda i: (0,) * a.ndim)

    return pl.pallas_call(
        body, grid=(S // ts,),
        in_specs=[pl.BlockSpec((ts, 2 * D), lambda i: (i, 0)), full(lng), full(lnb), full(ws), full(bsb)],
        out_specs=pl.BlockSpec((ts, D), lambda i: (i, 0)), out_shape=jax.ShapeDtypeStruct((S, D), BF16),
        scratch_shapes=[pltpu.VMEM((ts, D), BF16)],
        compiler_params=_cp("parallel"), name=name)(zp, lng, lnb, ws, bsb)


def _sgu_bwd(dy, zp, lng, lnb, ws, wst, bsb, name):
    S = zp.shape[0]
    D = zp.shape[1] // 2
    G = ws.shape[0]
    gd = D // G
    ts = _pick(S, TS_SGU, CHUNK)
    ncs = ts // CHUNK

    def body(dy_ref, zp_ref, lng_ref, lnb_ref, ws_ref, wst_ref, bsb_ref,
             dzp_ref, dws_ref, dbs_ref, dg_ref, db_ref, vb_ref, dvln_ref, acc_bs):
        i = pl.program_id(0)

        @pl.when(i == 0)
        def _():
            dws_ref[...] = jnp.zeros_like(dws_ref)
            acc_bs[...] = jnp.zeros_like(acc_bs)
            dg_ref[...] = jnp.zeros_like(dg_ref)
            db_ref[...] = jnp.zeros_like(db_ref)

        v, dv_dz = _gelu_and_grad(zp_ref[:, D:2 * D])
        mu = jnp.mean(v, axis=-1, keepdims=True)
        xc = v - mu
        rstd = lax.rsqrt(jnp.mean(xc * xc, axis=-1, keepdims=True) + LN_EPS)
        xh = xc * rstd
        vb_ref[...] = (xh * lng_ref[...] + lnb_ref[...]).astype(BF16)
        for c in range(ncs):
            rows = slice(c * CHUNK, (c + 1) * CHUNK)
            for g in range(G):
                cols = slice(g * gd, (g + 1) * gd)
                u, du_dz = _gelu_and_grad(zp_ref[rows, cols])
                dy_ = dy_ref[rows, cols].astype(F32)
                sv = _dot(ws_ref[g], vb_ref[rows, cols]) + bsb_ref[:, cols]
                dzp_ref[rows, cols] = (dy_ * sv * du_dz).astype(BF16)
                dsv = dy_ * u
                acc_bs[:, cols] += dsv
                dsvb = dsv.astype(BF16)
                dws_ref[g] += _dot_nt(dsvb, vb_ref[rows, cols])
                dvln_ref[rows, cols] = _dot(wst_ref[g], dsvb)
        dvln = dvln_ref[...]
        dg_ref[...] += jnp.sum(dvln * xh, axis=0, keepdims=True)
        db_ref[...] += jnp.sum(dvln, axis=0, keepdims=True)
        dxh = dvln * lng_ref[...]
        dv = rstd * (dxh - jnp.mean(dxh, axis=-1, keepdims=True) - xh * jnp.mean(dxh * xh, axis=-1, keepdims=True))
        dzp_ref[:, D:2 * D] = (dv * dv_dz).astype(BF16)

        @pl.when(i == pl.num_programs(0) - 1)
        def _():
            dbs_ref[...] = acc_bs[...]

    def full(shape):
        return pl.BlockSpec(shape, lambda i: (0,) * len(shape))

    return pl.pallas_call(
        body, grid=(S // ts,),
        in_specs=[pl.BlockSpec((ts, D), lambda i: (i, 0)), pl.BlockSpec((ts, 2 * D), lambda i: (i, 0)),
                  full(lng.shape), full(lnb.shape), full(ws.shape), full(wst.shape), full(bsb.shape)],
        out_specs=[pl.BlockSpec((ts, 2 * D), lambda i: (i, 0)), full(ws.shape), full(bsb.shape),
                   full((1, D)), full((1, D))],
        out_shape=[jax.ShapeDtypeStruct((S, 2 * D), BF16), jax.ShapeDtypeStruct(ws.shape, F32),
                   jax.ShapeDtypeStruct(bsb.shape, F32), jax.ShapeDtypeStruct((1, D), F32),
                   jax.ShapeDtypeStruct((1, D), F32)],
        scratch_shapes=[pltpu.VMEM((ts, D), BF16), pltpu.VMEM((ts, D), F32),
                        pltpu.VMEM(bsb.shape, F32)],
        compiler_params=_cp("arbitrary"), name=name)(dy, zp, lng, lnb, ws, wst, bsb)


def _group_sum(x, groups, name):
    P, D = x.shape
    gd = D // groups

    def body(x_ref, o_ref):
        for g in range(groups):
            o_ref[:, g:g + 1] = jnp.sum(x_ref[:, g * gd:(g + 1) * gd], axis=1, keepdims=True)

    return pl.pallas_call(body, out_shape=jax.ShapeDtypeStruct((P, groups), F32), name=name)(x)


def _adamw_small(ws, gs, ms, vs, name):
    n = len(ws)
    shapes = [w.shape for w in ws]
    flat = [(w.size // w.shape[-1], w.shape[-1]) for w in ws]

    def body(*refs):
        for k in range(n):
            w_ref, g_ref, m_ref, v_ref = (refs[j * n + k] for j in range(4))
            d_ref, nm_ref, nv_ref = (refs[(4 + j) * n + k] for j in range(3))
            d_ref[...], nm_ref[...], nv_ref[...] = _adamw_math(w_ref[...], g_ref[...], m_ref[...], v_ref[...])

    outs = pl.pallas_call(
        body, out_shape=[jax.ShapeDtypeStruct(f, F32) for f in flat] * 3, name=name)(
            *(a.reshape(f) for group in (ws, gs, ms, vs) for a, f in zip(group, flat)))
    return [tuple(outs[j * n + k].reshape(shapes[k]) for j in range(3)) for k in range(n)]


_HBM = pl.BlockSpec(memory_space=pltpu.HBM)


def _remote(src, dst, send_sem, recv_sem, to):
    return pltpu.make_async_remote_copy(src_ref=src, dst_ref=dst, send_sem=send_sem, recv_sem=recv_sem,
                                        device_id=to, device_id_type=MESH)


def _all_gather(block, name):
    R, C = block.shape

    def body(x_ref, out_ref, send_sems, recv_sems, local_sem):
        x, y, c = lax.axis_index("x"), lax.axis_index("y"), lax.axis_index("c")
        me, sibling = (x, y, c), (x, y, 1 - c)
        chips = [(1 - x, y), (x, 1 - y), (1 - x, 1 - y)]

        def slot(px, py, pc):
            return out_ref.at[4 * px + 2 * py + pc]

        def copy(k, blk, to, src=None):
            return _remote(slot(*blk) if src is None else src, slot(*blk), send_sems.at[k], recv_sems.at[k], to)

        mine = pltpu.make_async_copy(x_ref, slot(*me), local_sem)
        mine.start()
        first = [copy(0, me, sibling, src=x_ref)]
        first += [copy(1 + j, me, (*chip, c), src=x_ref) for j, chip in enumerate(chips)]
        for cp in first:
            cp.start()
        passed = [copy(4 + j, (*chip, c), sibling) for j, chip in enumerate(chips)]
        for j, chip in enumerate(chips):
            copy(1 + j, (*chip, c), me).wait_recv()
            passed[j].start()
        copy(0, sibling, me).wait_recv()
        for j, chip in enumerate(chips):
            copy(4 + j, (*chip, 1 - c), me).wait_recv()
        for cp in first + passed:
            cp.wait_send()
        mine.wait()

    return pl.pallas_call(
        body, out_shape=jax.ShapeDtypeStruct((NDEV, R, C), block.dtype), in_specs=[_HBM], out_specs=_HBM,
        scratch_shapes=[pltpu.SemaphoreType.DMA((7,)), pltpu.SemaphoreType.DMA((7,)), pltpu.SemaphoreType.DMA],
        name=name)(block)


def _all_gather_weights(pack, rows, name, after=None):
    C = pack.shape[1]
    nw = len(rows)
    starts = [sum(rows[:w]) for w in range(nw)]
    after_specs, after_ops = _after(after)

    def body(pack_ref, *rest):
        rest = rest[len(after_ops):]
        outs = rest[:nw]
        send_sems, recv_sems, local_sem = rest[nw:]
        x, y, c = lax.axis_index("x"), lax.axis_index("y"), lax.axis_index("c")
        me, sibling = (x, y, c), (x, y, 1 - c)
        chips = [(1 - x, y), (x, 1 - y), (1 - x, 1 - y)]

        def block(w, px, py, pc):
            return outs[w].at[pl.ds((4 * px + 2 * py + pc) * rows[w], rows[w])]

        def mine(w):
            return pack_ref.at[pl.ds(starts[w], rows[w])]

        def all_of(k):
            return _remote(pack_ref, pack_ref, send_sems.at[k], recv_sems.at[k], me)

        for w in range(nw):
            pltpu.make_async_copy(mine(w), block(w, *me), local_sem).start()
        for k, to in enumerate([sibling] + [(*chip, c) for chip in chips]):
            for w in range(nw):
                _remote(mine(w), block(w, *me), send_sems.at[k], recv_sems.at[k], to).start()
        for j, chip in enumerate(chips):
            all_of(1 + j).wait_recv()
            for w in range(nw):
                _remote(block(w, *chip, c), block(w, *chip, c), send_sems.at[4 + j], recv_sems.at[4 + j], sibling).start()
        all_of(0).wait_recv()
        for j in range(3):
            all_of(4 + j).wait_recv()
        for k in range(7):
            all_of(k).wait_send()
        pltpu.make_async_copy(pack_ref, pack_ref, local_sem).wait()

    return pl.pallas_call(
        body, out_shape=[jax.ShapeDtypeStruct((NDEV * r, C), pack.dtype) for r in rows],
        in_specs=[_HBM] + after_specs, out_specs=[_HBM] * nw,
        scratch_shapes=[pltpu.SemaphoreType.DMA((7,)), pltpu.SemaphoreType.DMA((7,)), pltpu.SemaphoreType.DMA],
        name=name)(pack, *after_ops)


_SEM = pl.BlockSpec(memory_space=pltpu.SEMAPHORE)
_DATAFLOW = pltpu.SideEffectType.DATAFLOW_SIDE_EFFECTING


def _split_start(srcs, lands, plan, n, after, name):
    nbuf = len(srcs) + len(lands)
    after_specs, after_ops = _after(after)

    def body(*refs):
        src_refs, land_refs = refs[:len(srcs)], refs[len(srcs):nbuf]
        send_sems, recv_sems = refs[nbuf + len(after_ops)], refs[nbuf + len(after_ops) + 1]
        for k, (src, dst, to) in enumerate(plan(src_refs, land_refs)):
            _remote(src, dst, send_sems.at[k], recv_sems.at[k], to).start()
        refs[-1][...] = jnp.zeros_like(refs[-1])

    bufs = [pltpu.with_memory_space_constraint(a, pltpu.HBM) for a in list(srcs) + list(lands)]
    outs = pl.pallas_call(
        body, name=name,
        out_shape=(pltpu.SemaphoreType.DMA((n,)), pltpu.SemaphoreType.DMA((n,)),
                   *[pltpu.HBM(a.shape, a.dtype) for a in bufs], jax.ShapeDtypeStruct((8, 128), F32)),
        in_specs=[_HBM] * nbuf + after_specs,
        out_specs=(_SEM, _SEM, *[_HBM] * nbuf, pl.BlockSpec(memory_space=pltpu.VMEM)),
        input_output_aliases={i: 2 + i for i in range(nbuf)},
        compiler_params=pltpu.CompilerParams(has_side_effects=_DATAFLOW))(*bufs, *after_ops)
    return outs[0], outs[1], list(outs[2:2 + len(srcs)]), list(outs[2 + len(srcs):2 + nbuf]), outs[-1]


def _split_wait(send_sems, recv_sems, srcs, lands, plan, after, name):
    nbuf = len(srcs) + len(lands)
    after_specs, after_ops = _after(after)

    def body(*refs):
        src_refs, land_refs = refs[:len(srcs)], refs[len(srcs):nbuf]
        send_sems_ref, recv_sems_ref = refs[nbuf], refs[nbuf + 1]
        for k, (src, dst, to) in enumerate(plan(src_refs, land_refs)):
            copy = _remote(src, dst, send_sems_ref.at[k], recv_sems_ref.at[k], to)
            copy.wait_send()
            copy.wait_recv()

    outs = pl.pallas_call(
        body, name=name, out_shape=tuple(pltpu.HBM(a.shape, a.dtype) for a in list(srcs) + list(lands)),
        in_specs=[_HBM] * nbuf + [_SEM, _SEM] + after_specs, out_specs=tuple([_HBM] * nbuf),
        input_output_aliases={i: i for i in range(nbuf)},
        compiler_params=pltpu.CompilerParams(has_side_effects=_DATAFLOW))(*srcs, *lands, send_sems, recv_sems, *after_ops)
    return list(outs[:len(srcs)]), list(outs[len(srcs):])


def _peers(x, y, c):
    return [(mask, (1 - x if mask & 4 else x, 1 - y if mask & 2 else y, 1 - c if mask & 1 else c))
            for mask in range(1, NDEV)]


def _gather_plan(rows):
    starts = [sum(rows[:w]) for w in range(len(rows))]

    def plan(src_refs, land_refs):
        x, y, c = lax.axis_index("x"), lax.axis_index("y"), lax.axis_index("c")
        copies = []
        for w, r in enumerate(rows):
            mine = src_refs[0].at[pl.ds(starts[w], r)]
            dst = land_refs[w].at[pl.ds((4 * x + 2 * y + c) * r, r)]
            copies += [(mine, dst, peer) for _, peer in _peers(x, y, c)]
        return copies

    return plan, (NDEV - 1) * len(rows)


def _place_own(shards, fulls, dev_idx, name):
    nw = len(shards)

    def body(i_ref, *refs):
        for w in range(nw):
            refs[2 * nw + w][...] = refs[w][...]

    grid_spec = pltpu.PrefetchScalarGridSpec(
        num_scalar_prefetch=1, grid=(1,),
        in_specs=[pl.BlockSpec(s.shape, lambda t, i_ref: (0, 0)) for s in shards] + [_ANY] * nw,
        out_specs=[pl.BlockSpec(s.shape, lambda t, i_ref: (i_ref[0], 0)) for s in shards])
    outs = pl.pallas_call(
        body, grid_spec=grid_spec, out_shape=[jax.ShapeDtypeStruct(f.shape, f.dtype) for f in fulls],
        input_output_aliases={1 + nw + w: w for w in range(nw)}, name=name)(dev_idx, *shards, *fulls)
    return list(outs)


def _scatter_plan(rows):
    def plan(src_refs, land_refs):
        x, y, c = lax.axis_index("x"), lax.axis_index("y"), lax.axis_index("c")
        copies = []
        for w, r in enumerate(rows):
            for mask, (px, py, pc) in _peers(x, y, c):
                src = src_refs[w].at[pl.ds((4 * px + 2 * py + pc) * r, r)]
                copies.append((src, land_refs[w].at[mask - 1], (px, py, pc)))
        return copies

    return plan, (NDEV - 1) * len(rows)


def _adamw_math(w, g, m, v):
    nm = ADAM_B1 * m + (1.0 - ADAM_B1) * g
    nv = ADAM_B2 * v + (1.0 - ADAM_B2) * (g * g)
    bc1 = 1.0 - ADAM_B1 ** ADAM_STEP
    bc2 = 1.0 - ADAM_B2 ** ADAM_STEP
    return -ADAM_LR * ((nm / bc1) / (jnp.sqrt(nv / bc2) + ADAM_EPS) + ADAM_WD * w), nm, nv


def _finish_weight(gs, gots, dev_idx, w, m, v, name, after=None):
    L = len(gs)
    n1, r, C = gots[0].shape
    block = (None,) + w.shape[1:]

    after_specs, after_ops = _after(after)

    def body(i_ref, *refs):
        ins, (w_ref, m_ref, v_ref), (g_out, d_out, m_out, v_out) = refs[:2 * L], refs[2 * L:2 * L + 3], refs[-4:]
        for layer in range(L):
            @pl.when(pl.program_id(0) == layer)
            def _():
                g_ref, got_ref = ins[2 * layer], ins[2 * layer + 1]
                acc = g_ref[...].astype(F32)
                for k in range(n1):
                    acc = acc + got_ref[k].astype(F32)
                g_out[...] = acc
                d_out[...], m_out[...], v_out[...] = _adamw_math(w_ref[...], acc, m_ref[...], v_ref[...])

    in_specs, ins = [], []
    for g, got in zip(gs, gots):
        in_specs += [pl.BlockSpec((r, C), lambda t, i_ref: (i_ref[0], 0), pipeline_mode=_RESIDENT),
                     pl.BlockSpec((n1, r, C), lambda t, i_ref: (0, 0, 0), pipeline_mode=_RESIDENT)]
        ins += [g, got]
    per_layer = pl.BlockSpec(block, lambda t, i_ref: (t, 0, 0))
    grid_spec = pltpu.PrefetchScalarGridSpec(
        num_scalar_prefetch=1, grid=(L,), in_specs=in_specs + [per_layer] * 3 + after_specs,
        out_specs=[per_layer] * 4)
    return pl.pallas_call(
        body, grid_spec=grid_spec, out_shape=[jax.ShapeDtypeStruct(w.shape, F32)] * 4,
        compiler_params=_cp("arbitrary"), name=name)(dev_idx, *ins, w, m, v, *after_ops)


def _sum_slots(a, name):
    n, R, C = a.shape

    def body(a_ref, o_ref):
        acc = a_ref[0]
        for k in range(1, n):
            acc = acc + a_ref[k]
        o_ref[...] = acc

    return pl.pallas_call(body, out_shape=jax.ShapeDtypeStruct((R, C), F32), name=name)(a)


def _shard_axis(name):
    return {"ev_w_in": 2, "ev_a_conv_w": 2, "ev_b_conv_w": 2, "ev_w_out": 1, "od_w_in": 2, "od_c_ln_g": 1,
            "od_c_ln_b": 1, "od_w_out": 1, "xa_w_q": 1, "xa_w_k": 1, "xa_w_v": 1, "xa_w_o": 1,
            "ffn_w_gate": 2, "ffn_w_up": 2, "ffn_w_down": 1}[name]


BIG = ["ev_w_in", "ev_w_out", "od_w_in", "od_w_out", "xa_w_q", "xa_w_k", "xa_w_v", "xa_w_o",
       "ffn_w_gate", "ffn_w_up", "ffn_w_down"]
SMALL_SHARDED = ["ev_a_conv_w", "ev_b_conv_w", "od_c_ln_g", "od_c_ln_b"]
REPLICATED = ["g_mix", "g_xattn", "g_mem", "g_ffn", "g_final", "ev_a_conv_b", "ev_a_ln_g", "ev_a_ln_b",
              "ev_b_conv_b", "od_w_s", "od_b_s"]
WEIGHTS = ["g_mix", "g_xattn", "g_mem", "g_ffn", "g_final", "ev_w_in", "ev_a_conv_w", "ev_a_conv_b", "ev_a_ln_g",
           "ev_a_ln_b", "ev_b_conv_w", "ev_b_conv_b", "ev_w_out", "od_w_in", "od_c_ln_g", "od_c_ln_b", "od_w_s",
           "od_b_s", "od_w_out", "xa_w_q", "xa_w_k", "xa_w_v", "xa_w_o", "ffn_w_gate", "ffn_w_up", "ffn_w_down"]


def _full_from_blocks(blocks, axis):
    shard = blocks.shape[1:]
    full = jnp.moveaxis(blocks, 0, axis)
    return full.reshape(shard[:axis] + (NDEV * shard[axis],) + shard[axis + 1:])


def _blocks_from_full(full, axis):
    shp = full.shape
    split = full.reshape(shp[:axis] + (NDEV, shp[axis] // NDEV) + shp[axis + 1:])
    return jnp.moveaxis(split, axis, 0)


def _pad_rows(flat, width, row_align):
    per = width * row_align
    n = -(-flat.shape[0] // per) * per
    return jnp.pad(flat, (0, n - flat.shape[0])).reshape(n // width, width)


def _row(v):
    return v.reshape(1, -1)


def _xattn_fwd(h, nq, mem, g_m, wq, wk, wv, wo, g_next, tag, after):
    mem_n = _rms_fwd(mem, _row(g_m), f"xa_mem_rms_{tag}")
    q = _mm([(nq, wq, "nn")], f"xa_q_{tag}", out_dtype=BF16, after=after)
    k = _mm([(mem_n, wk, "nn")], f"xa_k_{tag}", out_dtype=BF16)
    v = _mm([(mem_n, wv, "nn")], f"xa_v_{tag}", out_dtype=BF16)
    o = _attn_fwd(q, k, v, f"xa_attn_{tag}")
    h_new, n_next = _mm([(o, wo, "nn")], f"xa_o_{tag}", res=h, rms_g=_row(g_next))
    return h_new, n_next, (h, nq, mem_n, q, k, v, o)


def _xattn_bwd(dh_new, saved, mem, g_x, g_m, wq, wk, wv, wo, tag, push):
    h, nq, mem_n, q, k, v, o = saved
    do = _mm([(dh_new, wo, "nt")], f"xa_do_{tag}", out_dtype=BF16)
    d_wo = _mm_tn(o, dh_new, f"xa_dwo_{tag}")
    dq, dk, dv = _attn_bwd(q, k, v, do, f"xa_attn_bwd_{tag}")
    d_wq = _mm_tn(nq, dq, f"xa_dwq_{tag}")
    d_wk = _mm_tn(mem_n, dk, f"xa_dwk_{tag}")
    d_wv = _mm_tn(mem_n, dv, f"xa_dwv_{tag}")
    token = push([d_wq, d_wk, d_wv, d_wo])
    dmem_n = _mm([(dk, wk, "nt"), (dv, wv, "nt")], f"xa_dmem_{tag}", after=token)
    _, d_gm = _rms_bwd(dmem_n, mem, _row(g_m), None, f"xa_mem_rms_bwd_{tag}")
    dh, d_gx = _mm([(dq, wq, "nt")], f"xa_dnq_{tag}", rms_bwd=(h, _row(g_x), dh_new), tm=1024, after=token)
    return dh, dict(g_xattn=d_gx, g_mem=d_gm)


def _ffn_fwd(h, n, wgt, wut, wd, g_next, tag, after, loss=None):
    a, b, hid = _ffn_up(n, wgt, wut, f"ffn_up_{tag}", after=after)
    saved = (h, n, a, b, hid)
    if loss is not None:
        return _mm([(hid, wd, "nn")], f"ffn_down_{tag}", res=h, loss=loss, tm=1024), saved
    h_new, n_next = _mm([(hid, wd, "nn")], f"ffn_down_{tag}", res=h, rms_g=_row(g_next), tm=1024)
    return h_new, n_next, saved


def _ffn_bwd(dh_new, saved, g_f, wgt, wut, wd, tag, push):
    h, n, a, b, hid = saved
    da, db = _ffn_dhid(dh_new, wd, a, b, f"ffn_dhid_{tag}")
    d_wd = _mm_tn(hid, dh_new, f"ffn_dwd_{tag}", ts=1024, tn=1024)
    d_wgt = _mm_tn(da, n, f"ffn_dwg_{tag}", ts=1024, tn=1024)
    d_wut = _mm_tn(db, n, f"ffn_dwu_{tag}", ts=1024, tn=1024)
    token = push([d_wgt, d_wut, d_wd])
    dh, d_gf = _mm([(da, wgt, "nn"), (db, wut, "nn")], f"ffn_dn_{tag}", rms_bwd=(h, _row(g_f), dh_new), tm=512,
                   after=token)
    return dh, dict(g_ffn=d_gf)


_XA = ["xa_w_q", "xa_w_k", "xa_w_v", "xa_w_o"]
_FFN = ["ffn_w_gate", "ffn_w_up", "ffn_w_down"]
GATHERS = {
    "ev_in": [("ev_w_in", 0)],
    "xa0": [("ev_w_out", 0)] + [(n, 0) for n in _XA],
    "ffn0": [(n, 0) for n in _FFN],
    "od": [("od_w_in", 0), ("od_w_out", 0)],
    "xa1": [(n, 1) for n in _XA],
    "ffn1": [(n, 1) for n in _FFN],
}
SCATTERS = {
    "ffn1": [(n, 1) for n in _FFN],
    "xa1": [(n, 1) for n in _XA],
    "od": [("od_w_in", 0), ("od_w_out", 0)],
    "ffn0": [(n, 0) for n in _FFN],
    "xa0": [(n, 0) for n in _XA],
    "ev_out": [("ev_w_out", 0)],
    "ev_in": [("ev_w_in", 0)],
}


def _local_step(x, mem, loss_target, W, comm):
    grads = {}

    h0 = x
    (ev_w_in_t,), token = comm.weights("ev_in", None)
    n0 = _rms_fwd(h0, _row(W["g_mix"][0]), "ev_rms", after=token)
    z = _mm([(n0, ev_w_in_t, "nt")], "ev_in", tn=1280)
    token = comm.prefetch(["ffn0"], z)
    ab, ca = _conv_fwd(z, W["ev_a_conv_w"][0], W["ev_a_conv_b"], W["ev_a_ln_g"], W["ev_a_ln_b"],
                       W["ev_b_conv_w"][0], W["ev_b_conv_b"], "ev_conv", after=token)
    (ev_w_out, *xa_w0), _ = comm.weights("xa0", ab)
    h1, nq0 = _mm([(ab, ev_w_out, "nn")], "ev_out", res=h0, rms_g=_row(W["g_xattn"][0]))
    token = comm.prefetch(["od", "xa1"], nq0)
    h2, nf0, xa0 = _xattn_fwd(h1, nq0, mem, W["g_mem"][0], *xa_w0, W["g_ffn"][0], "l0", token)
    ffn_w0, _ = comm.weights("ffn0", nf0)
    token = comm.prefetch(["ffn1"], nf0)
    h3, n3, ff0 = _ffn_fwd(h2, nf0, *ffn_w0, W["g_mix"][1], "l0", token)

    (od_w_in_t, od_w_out), _ = comm.weights("od", n3)
    zp = _mm([(n3, od_w_in_t, "nt")], "od_in", tn=1024)
    D = x.shape[1]
    ws = W["od_w_s"][0].astype(BF16)
    wst = jnp.swapaxes(ws, 1, 2)
    bsb = jnp.repeat(jnp.transpose(W["od_b_s"][0]), D // C_GROUPS, axis=1)
    y_sgu = _sgu_fwd(zp, W["od_c_ln_g"], W["od_c_ln_b"], ws, bsb, "od_sgu")
    h4, nq1 = _mm([(y_sgu, od_w_out, "nn")], "od_out", res=h3, rms_g=_row(W["g_xattn"][1]))
    xa_w1, _ = comm.weights("xa1", nq1)
    h5, nf1, xa1 = _xattn_fwd(h4, nq1, mem, W["g_mem"][1], *xa_w1, W["g_ffn"][1], "l1", None)
    ffn_w1, _ = comm.weights("ffn1", nf1)
    (loss_row, dh6, d_gfinal), ff1 = _ffn_fwd(h5, nf1, *ffn_w1, None, "l1", None,
                                              loss=(_row(W["g_final"]), loss_target))
    grads["g_final"] = d_gfinal.reshape(-1)


    dh5, g_ff1 = _ffn_bwd(dh6, ff1, W["g_ffn"][1], *ffn_w1, "l1", lambda dws: comm.grads("ffn1", dws))
    dh4, g_xa1 = _xattn_bwd(dh5, xa1, mem, W["g_xattn"][1], W["g_mem"][1], *xa_w1, "l1",
                            lambda dws: comm.grads("xa1", dws))
    dy_sgu = _mm([(dh4, od_w_out, "nt")], "od_dy", tn=1024)
    d_od_out = _mm_tn(y_sgu, dh4, "od_dwout", tn=1024)
    dzp, d_ws, d_bsb, d_clng, d_clnb = _sgu_bwd(dy_sgu, zp, W["od_c_ln_g"], W["od_c_ln_b"], ws, wst, bsb, "od_sgu_bwd")
    grads["od_w_s"] = d_ws[None]
    grads["od_b_s"] = jnp.transpose(_group_sum(d_bsb, C_GROUPS, "od_dbs"))[None]
    grads["od_c_ln_g"], grads["od_c_ln_b"] = d_clng, d_clnb
    token = comm.grads("od", [_mm_tn(dzp, n3, "od_dwin", ts=1024, tn=1024), d_od_out])
    dh3, d_gmix1 = _mm([(dzp, od_w_in_t, "nn")], "od_dn", rms_bwd=(h3, _row(W["g_mix"][1]), dh4), tm=1024, after=token)

    dh2, g_ff0 = _ffn_bwd(dh3, ff0, W["g_ffn"][0], *ffn_w0, "l0", lambda dws: comm.grads("ffn0", dws))
    dh1, g_xa0 = _xattn_bwd(dh2, xa0, mem, W["g_xattn"][0], W["g_mem"][0], *xa_w0, "l0",
                            lambda dws: comm.grads("xa0", dws))
    token = comm.grads("ev_out", [_mm_tn(ab, dh1, "ev_dwout", tn=1024)])
    dab = _mm([(dh1, ev_w_out, "nt")], "ev_dab", tn=1024, after=token)
    dca, d_lng, d_lnb, d_ba = _conv_bwd_ln(dab, ca, W["ev_a_ln_g"], W["ev_a_ln_b"], "ev_conv_bwd_ln")
    dz, d_wa, d_wb, d_bb = _conv_bwd(z, dca, dab, W["ev_a_conv_w"][0], W["ev_b_conv_w"][0], W["ev_b_conv_b"],
                                     "ev_conv_bwd")
    grads.update(ev_a_ln_g=d_lng, ev_a_ln_b=d_lnb, ev_a_conv_b=d_ba, ev_b_conv_b=d_bb,
                 ev_a_conv_w=d_wa[None], ev_b_conv_w=d_wb[None])
    token = comm.grads("ev_in", [_mm_tn(dz, n0, "ev_dwin", ts=1024, tn=1024)])
    grad_x, d_gmix0 = _mm([(dz, ev_w_in_t, "nn")], "ev_dn", rms_bwd=(h0, _row(W["g_mix"][0]), dh1), tm=1024, after=token)

    grads["g_mix"] = jnp.concatenate([d_gmix0, d_gmix1], axis=0)
    for key in ("g_xattn", "g_mem"):
        grads[key] = jnp.concatenate([g_xa0[key], g_xa1[key]], axis=0)
    grads["g_ffn"] = jnp.concatenate([g_ff0["g_ffn"], g_ff1["g_ffn"]], axis=0)
    return loss_row, grad_x, grads


class _Exchanges:
    def __init__(self, shards, dev_idx, after):
        self.shards, self.dev_idx = shards, dev_idx
        self.gathering, self.scattering = {}, {}
        self.first = _all_gather_weights(self._pack(GATHERS["ev_in"]), self._rows(GATHERS["ev_in"]), "ag_ev_in",
                                         after=after)
        self.first_token = self.prefetch(["xa0"], self.first[0])

    def _rows(self, entries):
        return [self.shards[e].shape[0] for e in entries]

    def _pack(self, entries):
        return jnp.concatenate([self.shards[e] for e in entries], axis=0)

    def prefetch(self, gathers, after):
        for name in gathers:
            rows = self._rows(GATHERS[name])
            pack = self._pack(GATHERS[name])
            lands = [lax.empty((NDEV * r, pack.shape[1]), pack.dtype) for r in rows]
            plan, n = _gather_plan(rows)
            send, recv, srcs, lands, after = _split_start([pack], lands, plan, n, after, f"ag_{name}_start")
            self.gathering[name] = (send, recv, srcs, lands, plan, rows)
        return after

    def weights(self, name, after):
        if name == "ev_in":
            return self.first, self.first_token
        send, recv, srcs, lands, plan, rows = self.gathering.pop(name)
        _, lands = _split_wait(send, recv, srcs, lands, plan, after, f"ag_{name}_wait")
        return _place_own([self.shards[e] for e in GATHERS[name]], lands, self.dev_idx, f"ag_{name}_own"), None

    def grads(self, name, dws):
        rows = self._rows(SCATTERS[name])
        lands = [lax.empty((NDEV - 1, r, d.shape[1]), d.dtype) for r, d in zip(rows, dws)]
        plan, n = _scatter_plan(rows)
        send, recv, srcs, lands, token = _split_start(dws, lands, plan, n, None, f"rs_{name}_start")
        self.scattering[name] = (send, recv, srcs, lands, plan)
        return token

    def received(self, after):
        out = {}
        for name, (send, recv, srcs, lands, plan) in self.scattering.items():
            srcs, lands = _split_wait(send, recv, srcs, lands, plan, after, f"rs_{name}_wait")
            for entry, g, got in zip(SCATTERS[name], srcs, lands):
                out[entry] = (g, got)
        return out


def kernel(x, mem, g_mix, g_xattn, g_mem, g_ffn, g_final, ev_w_in, ev_a_conv_w, ev_a_conv_b, ev_a_ln_g, ev_a_ln_b, ev_b_conv_w, ev_b_conv_b, ev_w_out, od_w_in, od_c_ln_g, od_c_ln_b, od_w_s, od_b_s, od_w_out, xa_w_q, xa_w_k, xa_w_v, xa_w_o, ffn_w_gate, ffn_w_up, ffn_w_down, loss_target, m_g_mix, m_g_xattn, m_g_mem, m_g_ffn, m_g_final, m_ev_w_in, m_ev_a_conv_w, m_ev_a_conv_b, m_ev_a_ln_g, m_ev_a_ln_b, m_ev_b_conv_w, m_ev_b_conv_b, m_ev_w_out, m_od_w_in, m_od_c_ln_g, m_od_c_ln_b, m_od_w_s, m_od_b_s, m_od_w_out, m_xa_w_q, m_xa_w_k, m_xa_w_v, m_xa_w_o, m_ffn_w_gate, m_ffn_w_up, m_ffn_w_down, v_g_mix, v_g_xattn, v_g_mem, v_g_ffn, v_g_final, v_ev_w_in, v_ev_a_conv_w, v_ev_a_conv_b, v_ev_a_ln_g, v_ev_a_ln_b, v_ev_b_conv_w, v_ev_b_conv_b, v_ev_w_out, v_od_w_in, v_od_c_ln_g, v_od_c_ln_b, v_od_w_s, v_od_b_s, v_od_w_out, v_xa_w_q, v_xa_w_k, v_xa_w_v, v_xa_w_o, v_ffn_w_gate, v_ffn_w_up, v_ffn_w_down):
    local = dict(g_mix=g_mix, g_xattn=g_xattn, g_mem=g_mem, g_ffn=g_ffn, g_final=g_final, ev_w_in=ev_w_in, ev_a_conv_w=ev_a_conv_w, ev_a_conv_b=ev_a_conv_b, ev_a_ln_g=ev_a_ln_g, ev_a_ln_b=ev_a_ln_b, ev_b_conv_w=ev_b_conv_w, ev_b_conv_b=ev_b_conv_b, ev_w_out=ev_w_out, od_w_in=od_w_in, od_c_ln_g=od_c_ln_g, od_c_ln_b=od_c_ln_b, od_w_s=od_w_s, od_b_s=od_b_s, od_w_out=od_w_out, xa_w_q=xa_w_q, xa_w_k=xa_w_k, xa_w_v=xa_w_v, xa_w_o=xa_w_o, ffn_w_gate=ffn_w_gate, ffn_w_up=ffn_w_up, ffn_w_down=ffn_w_down)
    mom = dict(g_mix=m_g_mix, g_xattn=m_g_xattn, g_mem=m_g_mem, g_ffn=m_g_ffn, g_final=m_g_final, ev_w_in=m_ev_w_in, ev_a_conv_w=m_ev_a_conv_w, ev_a_conv_b=m_ev_a_conv_b, ev_a_ln_g=m_ev_a_ln_g, ev_a_ln_b=m_ev_a_ln_b, ev_b_conv_w=m_ev_b_conv_w, ev_b_conv_b=m_ev_b_conv_b, ev_w_out=m_ev_w_out, od_w_in=m_od_w_in, od_c_ln_g=m_od_c_ln_g, od_c_ln_b=m_od_c_ln_b, od_w_s=m_od_w_s, od_b_s=m_od_b_s, od_w_out=m_od_w_out, xa_w_q=m_xa_w_q, xa_w_k=m_xa_w_k, xa_w_v=m_xa_w_v, xa_w_o=m_xa_w_o, ffn_w_gate=m_ffn_w_gate, ffn_w_up=m_ffn_w_up, ffn_w_down=m_ffn_w_down)
    vel = dict(g_mix=v_g_mix, g_xattn=v_g_xattn, g_mem=v_g_mem, g_ffn=v_g_ffn, g_final=v_g_final, ev_w_in=v_ev_w_in, ev_a_conv_w=v_ev_a_conv_w, ev_a_conv_b=v_ev_a_conv_b, ev_a_ln_g=v_ev_a_ln_g, ev_a_ln_b=v_ev_a_ln_b, ev_b_conv_w=v_ev_b_conv_w, ev_b_conv_b=v_ev_b_conv_b, ev_w_out=v_ev_w_out, od_w_in=v_od_w_in, od_c_ln_g=v_od_c_ln_g, od_c_ln_b=v_od_c_ln_b, od_w_s=v_od_w_s, od_b_s=v_od_b_s, od_w_out=v_od_w_out, xa_w_q=v_xa_w_q, xa_w_k=v_xa_w_k, xa_w_v=v_xa_w_v, xa_w_o=v_xa_w_o, ffn_w_gate=v_ffn_w_gate, ffn_w_up=v_ffn_w_up, ffn_w_down=v_ffn_w_down)
    D = x.shape[-1]
    dev = 4 * lax.axis_index("x") + 2 * lax.axis_index("y") + lax.axis_index("c")

    def comm_layout(n, a):
        return jnp.transpose(a) if _shard_axis(n) == 2 else a

    shards = {(n, i): comm_layout(n, local[n][i]).astype(BF16) for n in BIG for i in range(local[n].shape[0])}
    small_sizes = [local[n].size for n in SMALL_SHARDED]
    small_block = _pad_rows(jnp.concatenate([local[n].reshape(-1) for n in SMALL_SHARDED]), 128, 8)
    small_all = _all_gather(small_block, "ag_small")
    comm = _Exchanges(shards, jnp.reshape(dev, (1,)).astype(jnp.int32), small_all)
    small_all = small_all.reshape(NDEV, -1)

    W = {n: local[n] for n in REPLICATED}
    o0 = 0
    for n, sz in zip(SMALL_SHARDED, small_sizes):
        blocks = small_all[:, o0:o0 + sz].reshape((NDEV,) + local[n].shape)
        W[n] = _full_from_blocks(blocks, _shard_axis(n))
        o0 += sz

    loss_row, grad_x, grads = _local_step(x[0], mem[0], loss_target[0], W, comm)

    received = comm.received(grad_x)
    rest = REPLICATED + SMALL_SHARDED
    rest_full_shapes = [grads[n].shape for n in rest]
    g_rest = _pad_rows(jnp.concatenate([grads[n].astype(F32).reshape(-1) for n in rest]), D, 8)
    small_rows = g_rest.shape[0]
    small_plan, small_n = _gather_plan([small_rows])
    small_send, small_recv, small_srcs, small_lands, token = _split_start(
        [g_rest], [lax.empty((NDEV * small_rows, D), F32)], small_plan, small_n, received["ev_w_in", 0][1],
        "ag_small_grads_start")

    gsh, delta, new_m, new_v = {}, {}, {}, {}
    def stacked_layout(n, a):
        return jnp.swapaxes(a, 1, 2) if _shard_axis(n) == 2 else a

    for n in BIG:
        parts = [received[n, i] for i in range(local[n].shape[0])]
        outs = _finish_weight([p[0] for p in parts], [p[1] for p in parts], comm.dev_idx,
                              *(stacked_layout(n, a) for a in (local[n], mom[n], vel[n])), f"finish_{n}", after=token)
        gsh[n], delta[n], new_m[n], new_v[n] = (stacked_layout(n, o) for o in outs)

    _, small_lands = _split_wait(small_send, small_recv, small_srcs, small_lands, small_plan,
                                 [delta[n] for n in BIG], "ag_small_grads_wait")
    partials = _place_own([g_rest], small_lands, comm.dev_idx, "ag_small_grads_own")[0]
    g_rest = _sum_slots(partials.reshape(NDEV, small_rows, D), "sum_small_grads").reshape(-1)
    o0 = 0
    for n, shp in zip(rest, rest_full_shapes):
        sz = 1
        for s in shp:
            sz *= s
        full = g_rest[o0:o0 + sz].reshape(shp)
        o0 += sz
        if n in SMALL_SHARDED:
            full = lax.dynamic_index_in_dim(_blocks_from_full(full, _shard_axis(n)), dev, 0, keepdims=False)
        gsh[n] = full.reshape(local[n].shape)

    small = _adamw_small([local[n] for n in rest], [gsh[n] for n in rest], [mom[n] for n in rest],
                         [vel[n] for n in rest], "adamw_small")
    for n, (d, nm, nv) in zip(rest, small):
        delta[n], new_m[n], new_v[n] = d, nm, nv

    loss = lax.psum(loss_row[0, 0], ("x", "y", "c"))
    return (loss, grad_x[None], *[gsh[n] for n in WEIGHTS], *[delta[n] for n in WEIGHTS],
            *[new_m[n] for n in WEIGHTS], *[new_v[n] for n in WEIGHTS])
```

```python
import jax
import jax.numpy as jnp
from jax import lax
from jax.experimental import pallas as pl
from jax.experimental.pallas import tpu as pltpu

F32, BF16 = jnp.float32, jnp.bfloat16
NDEV = 8
RMS_EPS = 1e-6
LN_EPS = 1e-5
CHUNK = 128
C_GROUPS = 8
XA_HEADS = 4
ADAM_LR, ADAM_B1, ADAM_B2, ADAM_EPS, ADAM_WD, ADAM_STEP = 0.001, 0.9, 0.999, 1e-08, 0.01, 10
HALO = 16
ROW_CHUNK = 32
ROW_CHUNK_FWD = 64
V7X_VMEM_LIMIT = 56 * 1024 * 1024
MESH = pl.DeviceIdType.MESH

TS_ROW = 512
TS_MM = 2048
TN_MM = 1408
TS_FFN = 512
MM_ROW_CHUNK = 256
TS_CONV = 512
TS_SGU = 512
TS_ATTN = 2048


def _cp(*sem):
    return pltpu.CompilerParams(dimension_semantics=sem, vmem_limit_bytes=V7X_VMEM_LIMIT)


def _pick(n, pref, align):
    for t in range(min(n, pref), 0, -1):
        if n % t == 0 and (t % align == 0 or t == n):
            return t
    return n


def _sigmoid(x):
    return 0.5 * jnp.tanh(0.5 * x) + 0.5


def _dot(a, b):
    return jnp.dot(a, b, preferred_element_type=F32)


def _dot_nt(a, b):
    return lax.dot_general(a, b, (((1,), (1,)), ((), ())), preferred_element_type=F32)


def _dot_tn(a, b):
    return lax.dot_general(a, b, (((0,), (0,)), ((), ())), preferred_element_type=F32)


_ANY = pl.BlockSpec(memory_space=pl.ANY)
_RESIDENT = pl.Buffered(1)


def _after(after):
    if after is None:
        return [], []
    ops = list(after) if isinstance(after, (list, tuple)) else [after]
    return [_ANY] * len(ops), ops


def _rms_fwd(h, g, name, after=None):
    S, D = h.shape
    ts = _pick(S, TS_MM, 16)
    after_specs, after_ops = _after(after)

    def body(h_ref, g_ref, *rest):
        o_ref = rest[-1]
        x = h_ref[...]
        r = lax.rsqrt(jnp.mean(x * x, axis=-1, keepdims=True) + RMS_EPS)
        o_ref[...] = ((x * r) * g_ref[...]).astype(o_ref.dtype)

    return pl.pallas_call(
        body, grid=(S // ts,),
        in_specs=[pl.BlockSpec((ts, D), lambda i: (i, 0)), pl.BlockSpec((1, D), lambda i: (0, 0))] + after_specs,
        out_specs=pl.BlockSpec((ts, D), lambda i: (i, 0)),
        out_shape=jax.ShapeDtypeStruct((S, D), BF16), compiler_params=_cp("parallel"), name=name)(h, g, *after_ops)


def _rms_bwd(dn, h, g, dres, name):
    S, D = h.shape
    ts = _pick(S, TS_ROW, 8)
    has_res = dres is not None

    def body(*refs):
        if has_res:
            dn_ref, h_ref, g_ref, dres_ref, dh_ref, dg_ref = refs
        else:
            dn_ref, h_ref, g_ref, dh_ref, dg_ref = refs
        x = h_ref[...]
        dn_ = dn_ref[...].astype(F32)
        r = lax.rsqrt(jnp.mean(x * x, axis=-1, keepdims=True) + RMS_EPS)
        xr = x * r

        @pl.when(pl.program_id(0) == 0)
        def _():
            dg_ref[...] = jnp.zeros_like(dg_ref)

        dg_ref[...] += jnp.sum(dn_ * xr, axis=0, keepdims=True)
        u = dn_ * g_ref[...]
        dh = r * u - xr * (r * jnp.mean(u * xr, axis=-1, keepdims=True))
        if has_res:
            dh = dh + dres_ref[...]
        dh_ref[...] = dh

    tile = pl.BlockSpec((ts, D), lambda i: (i, 0))
    vec = pl.BlockSpec((1, D), lambda i: (0, 0))
    ins = [dn, h, g] + ([dres] if has_res else [])
    return pl.pallas_call(
        body, grid=(S // ts,),
        in_specs=[tile, tile, vec] + ([tile] if has_res else []),
        out_specs=[tile, vec],
        out_shape=[jax.ShapeDtypeStruct((S, D), F32), jax.ShapeDtypeStruct((1, D), F32)],
        compiler_params=_cp("arbitrary"), name=name)(*ins)


def _mm(pairs, name, out_dtype=F32, res=None, rms_g=None, rms_bwd=None, loss=None, tm=None, tn=None, after=None):
    M = pairs[0][0].shape[0]
    N = pairs[0][1].shape[1 if pairs[0][2] == "nn" else 0]
    whole_rows = rms_g is not None or rms_bwd is not None or loss is not None
    tm = _pick(M, tm or TS_MM, 16)
    tn = N if whole_rows else _pick(N, tn or TN_MM, 128)
    npair = len(pairs)
    modes = [p[2] for p in pairs]
    after_specs, after_ops = _after(after)

    rc = MM_ROW_CHUNK if whole_rows and tm % MM_ROW_CHUNK == 0 else tm

    def body(*refs):
        rest = refs[2 * npair + len(after_ops):]
        res_ref = None
        if res is not None:
            res_ref, rest = rest[0], rest[1:]
        if rms_bwd is not None:
            dg_ref = rest[4]

            @pl.when(pl.program_id(0) == 0)
            def _():
                dg_ref[...] = jnp.zeros_like(dg_ref)

        if loss is not None:
            g_ref, t_ref, loss_ref, dh_ref, dg_ref = rest

            @pl.when(pl.program_id(0) == 0)
            def _():
                dg_ref[...] = jnp.zeros_like(dg_ref)
                loss_ref[...] = jnp.zeros_like(loss_ref)

        for r0 in range(0, tm, rc):
            rows = pl.ds(r0, rc)
            acc = None
            for p in range(npair):
                a_ = refs[2 * p][rows, :].astype(BF16)
                d = _dot(a_, refs[2 * p + 1][...]) if modes[p] == "nn" else _dot_nt(a_, refs[2 * p + 1][...])
                acc = d if acc is None else acc + d
            if res_ref is not None:
                acc = acc + res_ref[rows, :]
            if rms_bwd is not None:
                h_ref, g_ref, dres_ref, dh_ref, _ = rest
                x = h_ref[rows, :]
                r = lax.rsqrt(jnp.mean(x * x, axis=-1, keepdims=True) + RMS_EPS)
                xr = x * r
                dg_ref[...] += jnp.sum(acc * xr, axis=0, keepdims=True)
                u = acc * g_ref[...]
                dh_ref[rows, :] = r * u - xr * (r * jnp.mean(u * xr, axis=-1, keepdims=True)) + dres_ref[rows, :]
            elif loss is not None:
                r = lax.rsqrt(jnp.mean(acc * acc, axis=-1, keepdims=True) + RMS_EPS)
                xr = acc * r
                gg = g_ref[...]
                e = xr * gg - t_ref[rows, :]
                chunk_loss = jnp.sum(jnp.sum(e * e, axis=0, keepdims=True), axis=1, keepdims=True) * (0.5 / N)
                loss_ref[...] += jnp.broadcast_to(chunk_loss, loss_ref.shape)
                dy = e * (1.0 / N)
                dg_ref[...] += jnp.sum(dy * xr, axis=0, keepdims=True)
                u = dy * gg
                dh_ref[rows, :] = r * u - xr * (r * jnp.mean(u * xr, axis=-1, keepdims=True))
            elif rms_g is not None:
                g_ref, o_ref, n_ref = rest
                o_ref[rows, :] = acc
                r = lax.rsqrt(jnp.mean(acc * acc, axis=-1, keepdims=True) + RMS_EPS)
                n_ref[rows, :] = ((acc * r) * g_ref[...]).astype(BF16)
            else:
                rest[0][rows, :] = acc.astype(rest[0].dtype)

    in_specs, ins = [], []
    for a, w, mode in pairs:
        K = a.shape[1]
        in_specs.append(pl.BlockSpec((tm, K), lambda i, j: (i, 0)))
        once = _RESIDENT if tn == N else None
        in_specs.append(pl.BlockSpec((K, tn), lambda i, j: (0, j), pipeline_mode=once) if mode == "nn"
                        else pl.BlockSpec((tn, K), lambda i, j: (j, 0), pipeline_mode=once))
        ins += [a, w]
    in_specs += after_specs
    ins += after_ops
    tile = pl.BlockSpec((tm, tn), lambda i, j: (i, j))
    vec = pl.BlockSpec((1, tn), lambda i, j: (0, j))
    if res is not None:
        in_specs.append(tile)
        ins.append(res)
    sem = ("parallel", "parallel")
    if rms_bwd is not None:
        in_specs += [tile, vec, tile]
        ins += list(rms_bwd)
        out_specs = [tile, vec]
        out_shape = [jax.ShapeDtypeStruct((M, N), F32), jax.ShapeDtypeStruct((1, N), F32)]
        sem = ("arbitrary", "arbitrary")
    elif loss is not None:
        in_specs += [vec, tile]
        ins += list(loss)
        out_specs = [pl.BlockSpec((1, 128), lambda i, j: (0, 0)), tile, vec]
        out_shape = [jax.ShapeDtypeStruct((1, 128), F32), jax.ShapeDtypeStruct((M, N), F32),
                     jax.ShapeDtypeStruct((1, N), F32)]
        sem = ("arbitrary", "arbitrary")
    elif rms_g is not None:
        in_specs.append(vec)
        ins.append(rms_g)
        out_specs = [tile, tile]
        out_shape = [jax.ShapeDtypeStruct((M, N), F32), jax.ShapeDtypeStruct((M, N), BF16)]
    else:
        out_specs = tile
        out_shape = jax.ShapeDtypeStruct((M, N), out_dtype)
    return pl.pallas_call(
        body, grid=(M // tm, N // tn), in_specs=in_specs, out_specs=out_specs, out_shape=out_shape,
        compiler_params=_cp(*sem), name=name)(*ins)


def _mm_tn(a, b, name, ts=None, tn=None):
    S, K = a.shape
    N = b.shape[1]
    ts = _pick(S, ts or TS_MM, 16)
    tn = _pick(N, tn or TN_MM, 128)
    nsteps = S // ts

    def body(a_ref, b_ref, o_ref, acc_ref):
        s = pl.program_id(1)

        @pl.when(s == 0)
        def _():
            acc_ref[...] = jnp.zeros_like(acc_ref)

        acc_ref[...] += _dot_tn(a_ref[...].astype(BF16), b_ref[...].astype(BF16))

        @pl.when(s == nsteps - 1)
        def _():
            o_ref[...] = acc_ref[...].astype(o_ref.dtype)

    return pl.pallas_call(
        body, grid=(N // tn, nsteps),
        in_specs=[pl.BlockSpec((ts, K), lambda j, s: (s, 0)), pl.BlockSpec((ts, tn), lambda j, s: (s, j))],
        out_specs=pl.BlockSpec((K, tn), lambda j, s: (0, j)), out_shape=jax.ShapeDtypeStruct((K, N), BF16),
        scratch_shapes=[pltpu.VMEM((K, tn), F32)],
        compiler_params=_cp("parallel", "arbitrary"), name=name)(a, b)


def _col_chunk(n):
    return 256 if n % 256 == 0 else 128


def _ffn_up(n, wgt, wut, name, after=None):
    S, D = n.shape
    F = wgt.shape[0]
    tm = _pick(S, TS_FFN, 16)
    ce = _col_chunk(F)
    after_specs, after_ops = _after(after)

    def body(n_ref, wg_ref, wu_ref, *rest):
        a_ref, b_ref, hid_ref = rest[-3:]
        x = n_ref[...]
        for c0 in range(0, F, ce):
            a = _dot_nt(x, wg_ref[c0:c0 + ce, :])
            b = _dot_nt(x, wu_ref[c0:c0 + ce, :])
            a_ref[:, c0:c0 + ce] = a.astype(BF16)
            b_ref[:, c0:c0 + ce] = b.astype(BF16)
            hid_ref[:, c0:c0 + ce] = (a * _sigmoid(a) * b).astype(BF16)

    wspec = pl.BlockSpec((F, D), lambda i: (0, 0), pipeline_mode=_RESIDENT)
    ospec = pl.BlockSpec((tm, F), lambda i: (i, 0))
    osh = jax.ShapeDtypeStruct((S, F), BF16)
    return pl.pallas_call(
        body, grid=(S // tm,),
        in_specs=[pl.BlockSpec((tm, D), lambda i: (i, 0)), wspec, wspec] + after_specs,
        out_specs=[ospec, ospec, ospec], out_shape=[osh, osh, osh],
        compiler_params=_cp("parallel"), name=name)(n, wgt, wut, *after_ops)


def _ffn_dhid(dh, wd, a, b, name):
    S, D = dh.shape
    F = wd.shape[0]
    tm = _pick(S, TS_FFN, 16)
    ce = _col_chunk(F)

    def body(dh_ref, wd_ref, a_ref, b_ref, da_ref, db_ref):
        x = dh_ref[...].astype(BF16)
        for c0 in range(0, F, ce):
            g = _dot_nt(x, wd_ref[c0:c0 + ce, :]).astype(BF16)
            a_ = a_ref[:, c0:c0 + ce]
            sg = _sigmoid(a_)
            silu = a_ * sg
            da_ref[:, c0:c0 + ce] = (g * b_ref[:, c0:c0 + ce]) * (sg + silu * (1.0 - sg))
            db_ref[:, c0:c0 + ce] = g * silu

    tile = pl.BlockSpec((tm, F), lambda i: (i, 0))
    osh = jax.ShapeDtypeStruct((S, F), BF16)
    return pl.pallas_call(
        body, grid=(S // tm,),
        in_specs=[pl.BlockSpec((tm, D), lambda i: (i, 0)),
                  pl.BlockSpec((F, D), lambda i: (0, 0), pipeline_mode=_RESIDENT), tile, tile],
        out_specs=[tile, tile], out_shape=[osh, osh],
        compiler_params=_cp("parallel"), name=name)(dh, wd, a, b)


def _softmax_rows(s):
    m = jnp.max(s, axis=-1, keepdims=True)
    p = jnp.exp(s - m)
    return p / jnp.sum(p, axis=-1, keepdims=True)


def _attn_fwd(q, k, v, name):
    S, D = q.shape
    M = k.shape[0]
    hd = D // XA_HEADS
    scale = hd ** -0.5
    ts = _pick(S, TS_ATTN, 16)

    def body(q_ref, k_ref, v_ref, o_ref):
        for h in range(XA_HEADS):
            sl = slice(h * hd, (h + 1) * hd)
            p = _softmax_rows(_dot_nt(q_ref[:, sl], k_ref[:, sl]) * scale)
            o_ref[:, sl] = _dot(p.astype(BF16), v_ref[:, sl]).astype(BF16)

    tile = pl.BlockSpec((ts, D), lambda i: (i, 0))
    memspec = pl.BlockSpec((M, D), lambda i: (0, 0))
    return pl.pallas_call(
        body, grid=(S // ts,), in_specs=[tile, memspec, memspec], out_specs=tile,
        out_shape=jax.ShapeDtypeStruct((S, D), BF16), compiler_params=_cp("parallel"), name=name)(q, k, v)


def _attn_bwd(q, k, v, do, name):
    S, D = q.shape
    M = k.shape[0]
    hd = D // XA_HEADS
    scale = hd ** -0.5
    ts = _pick(S, TS_ATTN, 16)

    def body(q_ref, k_ref, v_ref, do_ref, dq_ref, dk_ref, dv_ref):
        @pl.when(pl.program_id(0) == 0)
        def _():
            dk_ref[...] = jnp.zeros_like(dk_ref)
            dv_ref[...] = jnp.zeros_like(dv_ref)

        for h in range(XA_HEADS):
            sl = slice(h * hd, (h + 1) * hd)
            qh, kh, vh, doh = q_ref[:, sl], k_ref[:, sl], v_ref[:, sl], do_ref[:, sl]
            p = _softmax_rows(_dot_nt(qh, kh) * scale)
            dp = _dot_nt(doh, vh)
            dv_ref[:, sl] += _dot_tn(p.astype(BF16), doh)
            delta = jnp.sum(dp * p, axis=-1, keepdims=True)
            ds = (p * (dp - delta) * scale).astype(BF16)
            dq_ref[:, sl] = _dot(ds, kh).astype(BF16)
            dk_ref[:, sl] += _dot_tn(ds, qh)

    tile = pl.BlockSpec((ts, D), lambda i: (i, 0))
    memspec = pl.BlockSpec((M, D), lambda i: (0, 0))
    return pl.pallas_call(
        body, grid=(S // ts,), in_specs=[tile, memspec, memspec, tile], out_specs=[tile, memspec, memspec],
        out_shape=[jax.ShapeDtypeStruct((S, D), BF16), jax.ShapeDtypeStruct((M, D), F32),
                   jax.ShapeDtypeStruct((M, D), F32)],
        compiler_params=_cp("arbitrary"), name=name)(q, k, v, do)


def _halo_specs(ts, col):
    per = ts // HALO

    def prev(i):
        return (jnp.maximum(i * per - 1, 0), col)

    def nxt(i, n_tiles):
        return (jnp.minimum((i + 1) * per, n_tiles * per - 1), col)

    return prev, nxt


def _fill_ext(ext_ref, prev_val, main_val, next_val, first, last, ts):
    ext_ref[pl.ds(0, HALO), :] = jnp.where(first, 0.0, prev_val)
    ext_ref[pl.ds(HALO, ts), :] = main_val
    ext_ref[pl.ds(HALO + ts, HALO), :] = jnp.where(last, 0.0, next_val)


SUBLANES = 8


def _fill_shifted(sh_ref, ts):
    n = ts + 2 * HALO - SUBLANES
    for s in range(1, SUBLANES):
        sh_ref[s, pl.ds(0, n), :] = sh_ref[0, pl.ds(s, n), :]


def _tap(sh_ref, r0, offset, rc):
    q, s = divmod(offset, SUBLANES)
    return sh_ref[s, pl.ds(pl.multiple_of(r0 + SUBLANES * q, SUBLANES), rc), :]


def _conv_fwd(z, wa, ba, lng, lnb, wb, bb, name, after=None):
    S = z.shape[0]
    C = z.shape[1] // 5
    KA, KB = wa.shape[0], wb.shape[0]
    pa, pb = KA // 2, KB // 2
    assert pa <= HALO and pb <= HALO
    ts = _pick(S, TS_CONV, ROW_CHUNK_FWD)
    nt = S // ts
    rc = ROW_CHUNK_FWD
    prev, nxt = _halo_specs(ts, 0)
    after_specs, after_ops = _after(after)

    def body(*refs):
        compute(*refs[:9], *refs[9 + len(after_ops):])

    def compute(z_ref, zp_ref, zn_ref, wa_ref, ba_ref, lng_ref, lnb_ref, wb_ref, bb_ref, ab_ref, ca_ref,
                ga_sh, tb_ext, win_b):
        i = pl.program_id(0)
        first, last = i == 0, i == nt - 1

        def glu(r):
            return r[:, 0:C] * _sigmoid(r[:, C:2 * C])

        def gcb(r):
            return r[:, 4 * C:5 * C] * r[:, 2 * C:3 * C]

        _fill_ext(ga_sh.at[0], glu(zp_ref), glu(z_ref), glu(zn_ref), first, last, ts)
        _fill_shifted(ga_sh, ts)
        _fill_ext(tb_ext, gcb(zp_ref), gcb(z_ref), gcb(zn_ref), first, last, ts)

        def chunk(c, carry):
            r0 = pl.multiple_of(c * rc, rc)
            win_b[...] = tb_ext[pl.ds(r0, rc + 2 * HALO), :]
            acc = jnp.zeros((rc, C), F32)
            for k in range(KA):
                acc = acc + wa_ref[k:k + 1, :] * _tap(ga_sh, r0, HALO - pa + k, rc)
            ca = acc + ba_ref[...]
            ca_ref[pl.ds(r0, rc), :] = ca
            mu = jnp.mean(ca, axis=-1, keepdims=True)
            xc = ca - mu
            var = jnp.mean(xc * xc, axis=-1, keepdims=True)
            ln = xc * lax.rsqrt(var + LN_EPS) * lng_ref[...] + lnb_ref[...]
            ab_ref[pl.ds(r0, rc), 0:C] = (ln * _sigmoid(ln)).astype(BF16)
            cb = jnp.zeros((rc, C), F32) + bb_ref[...]
            for k in range(KB):
                cb = cb + wb_ref[k:k + 1, :] * win_b[pl.ds(HALO - pb + k, rc), :]
            ab_ref[pl.ds(r0, rc), C:2 * C] = (z_ref[pl.ds(r0, rc), 3 * C:4 * C] * cb).astype(BF16)
            return carry

        lax.fori_loop(0, ts // rc, chunk, 0)

    zspec = pl.BlockSpec((ts, 5 * C), lambda i: (i, 0))
    zprev = pl.BlockSpec((HALO, 5 * C), prev)
    znext = pl.BlockSpec((HALO, 5 * C), lambda i: nxt(i, nt))

    def full(a):
        return pl.BlockSpec(a.shape, lambda i: (0, 0))

    return pl.pallas_call(
        body, grid=(nt,),
        in_specs=[zspec, zprev, znext, full(wa), full(ba), full(lng), full(lnb), full(wb), full(bb)] + after_specs,
        out_specs=[pl.BlockSpec((ts, 2 * C), lambda i: (i, 0)), pl.BlockSpec((ts, C), lambda i: (i, 0))],
        out_shape=[jax.ShapeDtypeStruct((S, 2 * C), BF16), jax.ShapeDtypeStruct((S, C), F32)],
        scratch_shapes=[pltpu.VMEM((SUBLANES, ts + 2 * HALO, C), F32), pltpu.VMEM((ts + 2 * HALO, C), F32),
                        pltpu.VMEM((rc + 2 * HALO, C), F32)],
        compiler_params=_cp("parallel"), name=name)(z, z, z, wa, ba, lng, lnb, wb, bb, *after_ops)


def _conv_bwd_ln(dab, ca, lng, lnb, name):
    S, C = ca.shape
    ts = _pick(S, TS_ROW, 8)

    def body(da_ref, ca_ref, lng_ref, lnb_ref, dca_ref, dg_ref, db_ref, dbias_ref):
        @pl.when(pl.program_id(0) == 0)
        def _():
            dg_ref[...] = jnp.zeros_like(dg_ref)
            db_ref[...] = jnp.zeros_like(db_ref)
            dbias_ref[...] = jnp.zeros_like(dbias_ref)

        ca_ = ca_ref[...]
        mu = jnp.mean(ca_, axis=-1, keepdims=True)
        xc = ca_ - mu
        rstd = lax.rsqrt(jnp.mean(xc * xc, axis=-1, keepdims=True) + LN_EPS)
        xh = xc * rstd
        ln = xh * lng_ref[...] + lnb_ref[...]
        sg = _sigmoid(ln)
        dln = da_ref[...].astype(F32) * (sg * (1.0 + ln * (1.0 - sg)))
        dg_ref[...] += jnp.sum(dln * xh, axis=0, keepdims=True)
        db_ref[...] += jnp.sum(dln, axis=0, keepdims=True)
        dxh = dln * lng_ref[...]
        dca = rstd * (dxh - jnp.mean(dxh, axis=-1, keepdims=True) - xh * jnp.mean(dxh * xh, axis=-1, keepdims=True))
        dca_ref[...] = dca
        dbias_ref[...] += jnp.sum(dca, axis=0, keepdims=True)

    tile = pl.BlockSpec((ts, C), lambda i: (i, 0))
    vec = pl.BlockSpec((1, C), lambda i: (0, 0))
    vsh = jax.ShapeDtypeStruct((1, C), F32)
    return pl.pallas_call(
        body, grid=(S // ts,), in_specs=[tile, tile, vec, vec], out_specs=[tile, vec, vec, vec],
        out_shape=[jax.ShapeDtypeStruct((S, C), F32), vsh, vsh, vsh],
        compiler_params=_cp("arbitrary"), name=name)(dab, ca, lng, lnb)


def _conv_bwd(z, dca, dab, wa, wb, bb, name):
    S = z.shape[0]
    C = z.shape[1] // 5
    KA, KB = wa.shape[0], wb.shape[0]
    pa, pb = KA // 2, KB // 2
    ts = _pick(S, TS_CONV, ROW_CHUNK)
    nt = S // ts
    rc = ROW_CHUNK
    prev0, nxt0 = _halo_specs(ts, 0)
    prev1, nxt1 = _halo_specs(ts, 1)

    def body(z_ref, zp_ref, zn_ref, dca_ref, dcap_ref, dcan_ref, db_ref, dbp_ref, dbn_ref, wa_ref, wb_ref, bb_ref,
             dz_ref, dwa_ref, dwb_ref, dbb_ref,
             ga_sh, dca_sh, tb_ext, dcb_ext, win_tb, win_dcb, acc_a, acc_b, acc_bias):
        i = pl.program_id(0)
        first, last = i == 0, i == nt - 1

        @pl.when(first)
        def _():
            acc_a[...] = jnp.zeros_like(acc_a)
            acc_b[...] = jnp.zeros_like(acc_b)
            acc_bias[...] = jnp.zeros_like(acc_bias)

        def glu(r):
            return r[:, 0:C] * _sigmoid(r[:, C:2 * C])

        def gcb(r):
            return r[:, 4 * C:5 * C] * r[:, 2 * C:3 * C]

        def dcb(d, r):
            return d[...].astype(F32) * r[:, 3 * C:4 * C]

        _fill_ext(ga_sh.at[0], glu(zp_ref), glu(z_ref), glu(zn_ref), first, last, ts)
        _fill_shifted(ga_sh, ts)
        _fill_ext(dca_sh.at[0], dcap_ref[...], dca_ref[...], dcan_ref[...], first, last, ts)
        _fill_shifted(dca_sh, ts)
        _fill_ext(tb_ext, gcb(zp_ref), gcb(z_ref), gcb(zn_ref), first, last, ts)
        _fill_ext(dcb_ext, dcb(dbp_ref, zp_ref), dcb(db_ref, z_ref), dcb(dbn_ref, zn_ref), first, last, ts)

        def fold(x):
            return jnp.sum(x.reshape(rc // 8, 8, C), axis=0)

        def chunk(c, carry):
            r0 = pl.multiple_of(c * rc, rc)
            win_tb[...] = tb_ext[pl.ds(r0, rc + 2 * HALO), :]
            win_dcb[...] = dcb_ext[pl.ds(r0, rc + 2 * HALO), :]
            dca_c = _tap(dca_sh, r0, HALO, rc)
            dglu = jnp.zeros((rc, C), F32)
            for k in range(KA):
                dglu = dglu + wa_ref[k:k + 1, :] * _tap(dca_sh, r0, HALO + pa - k, rc)
                acc_a[k] += fold(dca_c * _tap(ga_sh, r0, HALO - pa + k, rc))
            val = z_ref[pl.ds(r0, rc), 0:C]
            sg = _sigmoid(z_ref[pl.ds(r0, rc), C:2 * C])
            dz_ref[pl.ds(r0, rc), 0:C] = (dglu * sg).astype(BF16)
            dz_ref[pl.ds(r0, rc), C:2 * C] = (dglu * val * sg * (1.0 - sg)).astype(BF16)
            dcb_c = win_dcb[pl.ds(HALO, rc), :]
            cb = jnp.zeros((rc, C), F32) + bb_ref[...]
            dt = jnp.zeros((rc, C), F32)
            for k in range(KB):
                tb_k = win_tb[pl.ds(HALO - pb + k, rc), :]
                cb = cb + wb_ref[k:k + 1, :] * tb_k
                dt = dt + wb_ref[k:k + 1, :] * win_dcb[pl.ds(HALO + pb - k, rc), :]
                acc_b[k] += fold(dcb_c * tb_k)
            acc_bias[...] += fold(dcb_c)
            db_c = db_ref[pl.ds(r0, rc), :].astype(F32)
            dz_ref[pl.ds(r0, rc), 2 * C:3 * C] = (dt * z_ref[pl.ds(r0, rc), 4 * C:5 * C]).astype(BF16)
            dz_ref[pl.ds(r0, rc), 3 * C:4 * C] = (db_c * cb).astype(BF16)
            dz_ref[pl.ds(r0, rc), 4 * C:5 * C] = (dt * z_ref[pl.ds(r0, rc), 2 * C:3 * C]).astype(BF16)
            return carry

        lax.fori_loop(0, ts // rc, chunk, 0)

        @pl.when(last)
        def _():
            dwa_ref[...] = jnp.sum(acc_a[...], axis=1)
            dwb_ref[...] = jnp.sum(acc_b[...], axis=1)
            dbb_ref[...] = jnp.sum(acc_bias[...], axis=0, keepdims=True)

    zspec = pl.BlockSpec((ts, 5 * C), lambda i: (i, 0))
    zprev = pl.BlockSpec((HALO, 5 * C), prev0)
    znext = pl.BlockSpec((HALO, 5 * C), lambda i: nxt0(i, nt))
    dspec = pl.BlockSpec((ts, C), lambda i: (i, 0))
    dprev = pl.BlockSpec((HALO, C), prev0)
    dnext = pl.BlockSpec((HALO, C), lambda i: nxt0(i, nt))
    bspec = pl.BlockSpec((ts, C), lambda i: (i, 1))
    bprev = pl.BlockSpec((HALO, C), prev1)
    bnext = pl.BlockSpec((HALO, C), lambda i: nxt1(i, nt))

    def full(shape):
        return pl.BlockSpec(shape, lambda i: (0,) * len(shape))

    ext = pltpu.VMEM((ts + 2 * HALO, C), F32)
    shifted = pltpu.VMEM((SUBLANES, ts + 2 * HALO, C), F32)
    win = pltpu.VMEM((rc + 2 * HALO, C), F32)
    return pl.pallas_call(
        body, grid=(nt,),
        in_specs=[zspec, zprev, znext, dspec, dprev, dnext, bspec, bprev, bnext,
                  full(wa.shape), full(wb.shape), full(bb.shape)],
        out_specs=[pl.BlockSpec((ts, 5 * C), lambda i: (i, 0)), full((KA, C)), full((KB, C)), full((1, C))],
        out_shape=[jax.ShapeDtypeStruct((S, 5 * C), BF16), jax.ShapeDtypeStruct((KA, C), F32),
                   jax.ShapeDtypeStruct((KB, C), F32), jax.ShapeDtypeStruct((1, C), F32)],
        scratch_shapes=[shifted, shifted, ext, ext, win, win,
                        pltpu.VMEM((KA, 8, C), F32), pltpu.VMEM((KB, 8, C), F32), pltpu.VMEM((8, C), F32)],
        compiler_params=_cp("arbitrary"), name=name)(z, z, z, dca, dca, dca, dab, dab, dab, wa, wb, bb)


_GELU_C = 0.7978845608028654
_GELU_A = 0.044715


def _gelu(x):
    return 0.5 * x * (1.0 + jnp.tanh(_GELU_C * (x + _GELU_A * (x * x * x))))


def _gelu_and_grad(x):
    t = jnp.tanh(_GELU_C * (x + _GELU_A * (x * x * x)))
    hx = 0.5 * x
    return hx * (1.0 + t), 0.5 * (1.0 + t) + hx * (1.0 - t * t) * (_GELU_C * (1.0 + 3.0 * _GELU_A * x * x))


def _sgu_fwd(zp, lng, lnb, ws, bsb, name):
    S = zp.shape[0]
    D = zp.shape[1] // 2
    G = ws.shape[0]
    gd = D // G
    ts = _pick(S, TS_SGU, CHUNK)
    ncs = ts // CHUNK

    def body(zp_ref, lng_ref, lnb_ref, ws_ref, bsb_ref, y_ref, vb_ref):
        v = _gelu(zp_ref[:, D:2 * D])
        mu = jnp.mean(v, axis=-1, keepdims=True)
        xc = v - mu
        rstd = lax.rsqrt(jnp.mean(xc * xc, axis=-1, keepdims=True) + LN_EPS)
        vb_ref[...] = (xc * rstd * lng_ref[...] + lnb_ref[...]).astype(BF16)
        for c in range(ncs):
            rows = slice(c * CHUNK, (c + 1) * CHUNK)
            for g in range(G):
                cols = slice(g * gd, (g + 1) * gd)
                sv = _dot(ws_ref[g], vb_ref[rows, cols]) + bsb_ref[:, cols]
                y_ref[rows, cols] = (_gelu(zp_ref[rows, cols]) * sv).astype(BF16)

    def full(a):
        return pl.BlockSpec(a.shape, lambda i: (0,) * a.ndim)

    return pl.pallas_call(
        body, grid=(S // ts,),
        in_specs=[pl.BlockSpec((ts, 2 * D), lambda i: (i, 0)), full(lng), full(lnb), full(ws), full(bsb)],
        out_specs=pl.BlockSpec((ts, D), lambda i: (i, 0)), out_shape=jax.ShapeDtypeStruct((S, D), BF16),
        scratch_shapes=[pltpu.VMEM((ts, D), BF16)],
        compiler_params=_cp("parallel"), name=name)(zp, lng, lnb, ws, bsb)


def _sgu_bwd(dy, zp, lng, lnb, ws, wst, bsb, name):
    S = zp.shape[0]
    D = zp.shape[1] // 2
    G = ws.shape[0]
    gd = D // G
    ts = _pick(S, TS_SGU, CHUNK)
    ncs = ts // CHUNK

    def body(dy_ref, zp_ref, lng_ref, lnb_ref, ws_ref, wst_ref, bsb_ref,
             dzp_ref, dws_ref, dbs_ref, dg_ref, db_ref, vb_ref, dvln_ref, acc_bs):
        i = pl.program_id(0)

        @pl.when(i == 0)
        def _():
            dws_ref[...] = jnp.zeros_like(dws_ref)
            acc_bs[...] = jnp.zeros_like(acc_bs)
            dg_ref[...] = jnp.zeros_like(dg_ref)
            db_ref[...] = jnp.zeros_like(db_ref)

        v, dv_dz = _gelu_and_grad(zp_ref[:, D:2 * D])
        mu = jnp.mean(v, axis=-1, keepdims=True)
        xc = v - mu
        rstd = lax.rsqrt(jnp.mean(xc * xc, axis=-1, keepdims=True) + LN_EPS)
        xh = xc * rstd
        vb_ref[...] = (xh * lng_ref[...] + lnb_ref[...]).astype(BF16)
        for c in range(ncs):
            rows = slice(c * CHUNK, (c + 1) * CHUNK)
            for g in range(G):
                cols = slice(g * gd, (g + 1) * gd)
                u, du_dz = _gelu_and_grad(zp_ref[rows, cols])
                dy_ = dy_ref[rows, cols].astype(F32)
                sv = _dot(ws_ref[g], vb_ref[rows, cols]) + bsb_ref[:, cols]
                dzp_ref[rows, cols] = (dy_ * sv * du_dz).astype(BF16)
                dsv = dy_ * u
                acc_bs[:, cols] += dsv
                dsvb = dsv.astype(BF16)
                dws_ref[g] += _dot_nt(dsvb, vb_ref[rows, cols])
                dvln_ref[rows, cols] = _dot(wst_ref[g], dsvb)
        dvln = dvln_ref[...]
        dg_ref[...] += jnp.sum(dvln * xh, axis=0, keepdims=True)
        db_ref[...] += jnp.sum(dvln, axis=0, keepdims=True)
        dxh = dvln * lng_ref[...]
        dv = rstd * (dxh - jnp.mean(dxh, axis=-1, keepdims=True) - xh * jnp.mean(dxh * xh, axis=-1, keepdims=True))
        dzp_ref[:, D:2 * D] = (dv * dv_dz).astype(BF16)

        @pl.when(i == pl.num_programs(0) - 1)
        def _():
            dbs_ref[...] = acc_bs[...]

    def full(shape):
        return pl.BlockSpec(shape, lambda i: (0,) * len(shape))

    return pl.pallas_call(
        body, grid=(S // ts,),
        in_specs=[pl.BlockSpec((ts, D), lambda i: (i, 0)), pl.BlockSpec((ts, 2 * D), lambda i: (i, 0)),
                  full(lng.shape), full(lnb.shape), full(ws.shape), full(wst.shape), full(bsb.shape)],
        out_specs=[pl.BlockSpec((ts, 2 * D), lambda i: (i, 0)), full(ws.shape), full(bsb.shape),
                   full((1, D)), full((1, D))],
        out_shape=[jax.ShapeDtypeStruct((S, 2 * D), BF16), jax.ShapeDtypeStruct(ws.shape, F32),
                   jax.ShapeDtypeStruct(bsb.shape, F32), jax.ShapeDtypeStruct((1, D), F32),
                   jax.ShapeDtypeStruct((1, D), F32)],
        scratch_shapes=[pltpu.VMEM((ts, D), BF16), pltpu.VMEM((ts, D), F32),
                        pltpu.VMEM(bsb.shape, F32)],
        compiler_params=_cp("arbitrary"), name=name)(dy, zp, lng, lnb, ws, wst, bsb)


def _group_sum(x, groups, name):
    P, D = x.shape
    gd = D // groups

    def body(x_ref, o_ref):
        for g in range(groups):
            o_ref[:, g:g + 1] = jnp.sum(x_ref[:, g * gd:(g + 1) * gd], axis=1, keepdims=True)

    return pl.pallas_call(body, out_shape=jax.ShapeDtypeStruct((P, groups), F32), name=name)(x)


def _adamw_small(ws, gs, ms, vs, name):
    n = len(ws)
    shapes = [w.shape for w in ws]
    flat = [(w.size // w.shape[-1], w.shape[-1]) for w in ws]

    def body(*refs):
        for k in range(n):
            w_ref, g_ref, m_ref, v_ref = (refs[j * n + k] for j in range(4))
            d_ref, nm_ref, nv_ref = (refs[(4 + j) * n + k] for j in range(3))
            d_ref[...], nm_ref[...], nv_ref[...] = _adamw_math(w_ref[...], g_ref[...], m_ref[...], v_ref[...])

    outs = pl.pallas_call(
        body, out_shape=[jax.ShapeDtypeStruct(f, F32) for f in flat] * 3, name=name)(
            *(a.reshape(f) for group in (ws, gs, ms, vs) for a, f in zip(group, flat)))
    return [tuple(outs[j * n + k].reshape(shapes[k]) for j in range(3)) for k in range(n)]


_HBM = pl.BlockSpec(memory_space=pltpu.HBM)


def _remote(src, dst, send_sem, recv_sem, to):
    return pltpu.make_async_remote_copy(src_ref=src, dst_ref=dst, send_sem=send_sem, recv_sem=recv_sem,
                                        device_id=to, device_id_type=MESH)


def _all_gather(block, name):
    R, C = block.shape

    def body(x_ref, out_ref, send_sems, recv_sems, local_sem):
        x, y, c = lax.axis_index("x"), lax.axis_index("y"), lax.axis_index("c")
        me, sibling = (x, y, c), (x, y, 1 - c)
        chips = [(1 - x, y), (x, 1 - y), (1 - x, 1 - y)]

        def slot(px, py, pc):
            return out_ref.at[4 * px + 2 * py + pc]

        def copy(k, blk, to, src=None):
            return _remote(slot(*blk) if src is None else src, slot(*blk), send_sems.at[k], recv_sems.at[k], to)

        mine = pltpu.make_async_copy(x_ref, slot(*me), local_sem)
        mine.start()
        first = [copy(0, me, sibling, src=x_ref)]
        first += [copy(1 + j, me, (*chip, c), src=x_ref) for j, chip in enumerate(chips)]
        for cp in first:
            cp.start()
        passed = [copy(4 + j, (*chip, c), sibling) for j, chip in enumerate(chips)]
        for j, chip in enumerate(chips):
            copy(1 + j, (*chip, c), me).wait_recv()
            passed[j].start()
        copy(0, sibling, me).wait_recv()
        for j, chip in enumerate(chips):
            copy(4 + j, (*chip, 1 - c), me).wait_recv()
        for cp in first + passed:
            cp.wait_send()
        mine.wait()

    return pl.pallas_call(
        body, out_shape=jax.ShapeDtypeStruct((NDEV, R, C), block.dtype), in_specs=[_HBM], out_specs=_HBM,
        scratch_shapes=[pltpu.SemaphoreType.DMA((7,)), pltpu.SemaphoreType.DMA((7,)), pltpu.SemaphoreType.DMA],
        name=name)(block)


def _all_gather_weights(pack, rows, name, after=None):
    C = pack.shape[1]
    nw = len(rows)
    starts = [sum(rows[:w]) for w in range(nw)]
    after_specs, after_ops = _after(after)

    def body(pack_ref, *rest):
        rest = rest[len(after_ops):]
        outs = rest[:nw]
        send_sems, recv_sems, local_sem = rest[nw:]
        x, y, c = lax.axis_index("x"), lax.axis_index("y"), lax.axis_index("c")
        me, sibling = (x, y, c), (x, y, 1 - c)
        chips = [(1 - x, y), (x, 1 - y), (1 - x, 1 - y)]

        def block(w, px, py, pc):
            return outs[w].at[pl.ds((4 * px + 2 * py + pc) * rows[w], rows[w])]

        def mine(w):
            return pack_ref.at[pl.ds(starts[w], rows[w])]

        def all_of(k):
            return _remote(pack_ref, pack_ref, send_sems.at[k], recv_sems.at[k], me)

        for w in range(nw):
            pltpu.make_async_copy(mine(w), block(w, *me), local_sem).start()
        for k, to in enumerate([sibling] + [(*chip, c) for chip in chips]):
            for w in range(nw):
                _remote(mine(w), block(w, *me), send_sems.at[k], recv_sems.at[k], to).start()
        for j, chip in enumerate(chips):
            all_of(1 + j).wait_recv()
            for w in range(nw):
                _remote(block(w, *chip, c), block(w, *chip, c), send_sems.at[4 + j], recv_sems.at[4 + j], sibling).start()
        all_of(0).wait_recv()
        for j in range(3):
            all_of(4 + j).wait_recv()
        for k in range(7):
            all_of(k).wait_send()
        pltpu.make_async_copy(pack_ref, pack_ref, local_sem).wait()

    return pl.pallas_call(
        body, out_shape=[jax.ShapeDtypeStruct((NDEV * r, C), pack.dtype) for r in rows],
        in_specs=[_HBM] + after_specs, out_specs=[_HBM] * nw,
        scratch_shapes=[pltpu.SemaphoreType.DMA((7,)), pltpu.SemaphoreType.DMA((7,)), pltpu.SemaphoreType.DMA],
        name=name)(pack, *after_ops)


_SEM = pl.BlockSpec(memory_space=pltpu.SEMAPHORE)
_DATAFLOW = pltpu.SideEffectType.DATAFLOW_SIDE_EFFECTING


def _split_start(srcs, lands, plan, n, after, name):
    nbuf = len(srcs) + len(lands)
    after_specs, after_ops = _after(after)

    def body(*refs):
        src_refs, land_refs = refs[:len(srcs)], refs[len(srcs):nbuf]
        send_sems, recv_sems = refs[nbuf + len(after_ops)], refs[nbuf + len(after_ops) + 1]
        for k, (src, dst, to) in enumerate(plan(src_refs, land_refs)):
            _remote(src, dst, send_sems.at[k], recv_sems.at[k], to).start()
        refs[-1][...] = jnp.zeros_like(refs[-1])

    bufs = [pltpu.with_memory_space_constraint(a, pltpu.HBM) for a in list(srcs) + list(lands)]
    outs = pl.pallas_call(
        body, name=name,
        out_shape=(pltpu.SemaphoreType.DMA((n,)), pltpu.SemaphoreType.DMA((n,)),
                   *[pltpu.HBM(a.shape, a.dtype) for a in bufs], jax.ShapeDtypeStruct((8, 128), F32)),
        in_specs=[_HBM] * nbuf + after_specs,
        out_specs=(_SEM, _SEM, *[_HBM] * nbuf, pl.BlockSpec(memory_space=pltpu.VMEM)),
        input_output_aliases={i: 2 + i for i in range(nbuf)},
        compiler_params=pltpu.CompilerParams(has_side_effects=_DATAFLOW))(*bufs, *after_ops)
    return outs[0], outs[1], list(outs[2:2 + len(srcs)]), list(outs[2 + len(srcs):2 + nbuf]), outs[-1]


def _split_wait(send_sems, recv_sems, srcs, lands, plan, after, name):
    nbuf = len(srcs) + len(lands)
    after_specs, after_ops = _after(after)

    def body(*refs):
        src_refs, land_refs = refs[:len(srcs)], refs[len(srcs):nbuf]
        send_sems_ref, recv_sems_ref = refs[nbuf], refs[nbuf + 1]
        for k, (src, dst, to) in enumerate(plan(src_refs, land_refs)):
            copy = _remote(src, dst, send_sems_ref.at[k], recv_sems_ref.at[k], to)
            copy.wait_send()
            copy.wait_recv()

    outs = pl.pallas_call(
        body, name=name, out_shape=tuple(pltpu.HBM(a.shape, a.dtype) for a in list(srcs) + list(lands)),
        in_specs=[_HBM] * nbuf + [_SEM, _SEM] + after_specs, out_specs=tuple([_HBM] * nbuf),
        input_output_aliases={i: i for i in range(nbuf)},
        compiler_params=pltpu.CompilerParams(has_side_effects=_DATAFLOW))(*srcs, *lands, send_sems, recv_sems, *after_ops)
    return list(outs[:len(srcs)]), list(outs[len(srcs):])


def _peers(x, y, c):
    return [(mask, (1 - x if mask & 4 else x, 1 - y if mask & 2 else y, 1 - c if mask & 1 else c))
            for mask in range(1, NDEV)]


def _gather_plan(rows):
    starts = [sum(rows[:w]) for w in range(len(rows))]

    def plan(src_refs, land_refs):
        x, y, c = lax.axis_index("x"), lax.axis_index("y"), lax.axis_index("c")
        copies = []
        for w, r in enumerate(rows):
            mine = src_refs[0].at[pl.ds(starts[w], r)]
            dst = land_refs[w].at[pl.ds((4 * x + 2 * y + c) * r, r)]
            copies += [(mine, dst, peer) for _, peer in _peers(x, y, c)]
        return copies

    return plan, (NDEV - 1) * len(rows)


def _place_own(shards, fulls, dev_idx, name):
    nw = len(shards)

    def body(i_ref, *refs):
        for w in range(nw):
            refs[2 * nw + w][...] = refs[w][...]

    grid_spec = pltpu.PrefetchScalarGridSpec(
        num_scalar_prefetch=1, grid=(1,),
        in_specs=[pl.BlockSpec(s.shape, lambda t, i_ref: (0, 0)) for s in shards] + [_ANY] * nw,
        out_specs=[pl.BlockSpec(s.shape, lambda t, i_ref: (i_ref[0], 0)) for s in shards])
    outs = pl.pallas_call(
        body, grid_spec=grid_spec, out_shape=[jax.ShapeDtypeStruct(f.shape, f.dtype) for f in fulls],
        input_output_aliases={1 + nw + w: w for w in range(nw)}, name=name)(dev_idx, *shards, *fulls)
    return list(outs)


def _scatter_plan(rows):
    def plan(src_refs, land_refs):
        x, y, c = lax.axis_index("x"), lax.axis_index("y"), lax.axis_index("c")
        copies = []
        for w, r in enumerate(rows):
            for mask, (px, py, pc) in _peers(x, y, c):
                src = src_refs[w].at[pl.ds((4 * px + 2 * py + pc) * r, r)]
                copies.append((src, land_refs[w].at[mask - 1], (px, py, pc)))
        return copies

    return plan, (NDEV - 1) * len(rows)


def _adamw_math(w, g, m, v):
    nm = ADAM_B1 * m + (1.0 - ADAM_B1) * g
    nv = ADAM_B2 * v + (1.0 - ADAM_B2) * (g * g)
    bc1 = 1.0 - ADAM_B1 ** ADAM_STEP
    bc2 = 1.0 - ADAM_B2 ** ADAM_STEP
    return -ADAM_LR * ((nm / bc1) / (jnp.sqrt(nv / bc2) + ADAM_EPS) + ADAM_WD * w), nm, nv


def _finish_weight(gs, gots, dev_idx, w, m, v, name, after=None):
    L = len(gs)
    n1, r, C = gots[0].shape
    block = (None,) + w.shape[1:]

    after_specs, after_ops = _after(after)

    def body(i_ref, *refs):
        ins, (w_ref, m_ref, v_ref), (g_out, d_out, m_out, v_out) = refs[:2 * L], refs[2 * L:2 * L + 3], refs[-4:]
        for layer in range(L):
            @pl.when(pl.program_id(0) == layer)
            def _():
                g_ref, got_ref = ins[2 * layer], ins[2 * layer + 1]
                acc = g_ref[...].astype(F32)
                for k in range(n1):
                    acc = acc + got_ref[k].astype(F32)
                g_out[...] = acc
                d_out[...], m_out[...], v_out[...] = _adamw_math(w_ref[...], acc, m_ref[...], v_ref[...])

    in_specs, ins = [], []
    for g, got in zip(gs, gots):
        in_specs += [pl.BlockSpec((r, C), lambda t, i_ref: (i_ref[0], 0), pipeline_mode=_RESIDENT),
                     pl.BlockSpec((n1, r, C), lambda t, i_ref: (0, 0, 0), pipeline_mode=_RESIDENT)]
        ins += [g, got]
    per_layer = pl.BlockSpec(block, lambda t, i_ref: (t, 0, 0))
    grid_spec = pltpu.PrefetchScalarGridSpec(
        num_scalar_prefetch=1, grid=(L,), in_specs=in_specs + [per_layer] * 3 + after_specs,
        out_specs=[per_layer] * 4)
    return pl.pallas_call(
        body, grid_spec=grid_spec, out_shape=[jax.ShapeDtypeStruct(w.shape, F32)] * 4,
        compiler_params=_cp("arbitrary"), name=name)(dev_idx, *ins, w, m, v, *after_ops)


def _sum_slots(a, name):
    n, R, C = a.shape

    def body(a_ref, o_ref):
        acc = a_ref[0]
        for k in range(1, n):
            acc = acc + a_ref[k]
        o_ref[...] = acc

    return pl.pallas_call(body, out_shape=jax.ShapeDtypeStruct((R, C), F32), name=name)(a)


def _shard_axis(name):
    return {"ev_w_in": 2, "ev_a_conv_w": 2, "ev_b_conv_w": 2, "ev_w_out": 1, "od_w_in": 2, "od_c_ln_g": 1,
            "od_c_ln_b": 1, "od_w_out": 1, "xa_w_q": 1, "xa_w_k": 1, "xa_w_v": 1, "xa_w_o": 1,
            "ffn_w_gate": 2, "ffn_w_up": 2, "ffn_w_down": 1}[name]


BIG = ["ev_w_in", "ev_w_out", "od_w_in", "od_w_out", "xa_w_q", "xa_w_k", "xa_w_v", "xa_w_o",
       "ffn_w_gate", "ffn_w_up", "ffn_w_down"]
SMALL_SHARDED = ["ev_a_conv_w", "ev_b_conv_w", "od_c_ln_g", "od_c_ln_b"]
REPLICATED = ["g_mix", "g_xattn", "g_mem", "g_ffn", "g_final", "ev_a_conv_b", "ev_a_ln_g", "ev_a_ln_b",
              "ev_b_conv_b", "od_w_s", "od_b_s"]
WEIGHTS = ["g_mix", "g_xattn", "g_mem", "g_ffn", "g_final", "ev_w_in", "ev_a_conv_w", "ev_a_conv_b", "ev_a_ln_g",
           "ev_a_ln_b", "ev_b_conv_w", "ev_b_conv_b", "ev_w_out", "od_w_in", "od_c_ln_g", "od_c_ln_b", "od_w_s",
           "od_b_s", "od_w_out", "xa_w_q", "xa_w_k", "xa_w_v", "xa_w_o", "ffn_w_gate", "ffn_w_up", "ffn_w_down"]


def _full_from_blocks(blocks, axis):
    shard = blocks.shape[1:]
    full = jnp.moveaxis(blocks, 0, axis)
    return full.reshape(shard[:axis] + (NDEV * shard[axis],) + shard[axis + 1:])


def _blocks_from_full(full, axis):
    shp = full.shape
    split = full.reshape(shp[:axis] + (NDEV, shp[axis] // NDEV) + shp[axis + 1:])
    return jnp.moveaxis(split, axis, 0)


def _pad_rows(flat, width, row_align):
    per = width * row_align
    n = -(-flat.shape[0] // per) * per
    return jnp.pad(flat, (0, n - flat.shape[0])).reshape(n // width, width)


def _row(v):
    return v.reshape(1, -1)


def _xattn_fwd(h, nq, mem, g_m, wq, wk, wv, wo, g_next, tag, after):
    mem_n = _rms_fwd(mem, _row(g_m), f"xa_mem_rms_{tag}")
    q = _mm([(nq, wq, "nn")], f"xa_q_{tag}", out_dtype=BF16, after=after)
    k = _mm([(mem_n, wk, "nn")], f"xa_k_{tag}", out_dtype=BF16)
    v = _mm([(mem_n, wv, "nn")], f"xa_v_{tag}", out_dtype=BF16)
    o = _attn_fwd(q, k, v, f"xa_attn_{tag}")
    h_new, n_next = _mm([(o, wo, "nn")], f"xa_o_{tag}", res=h, rms_g=_row(g_next))
    return h_new, n_next, (h, nq, mem_n, q, k, v, o)


def _xattn_bwd(dh_new, saved, mem, g_x, g_m, wq, wk, wv, wo, tag, push):
    h, nq, mem_n, q, k, v, o = saved
    do = _mm([(dh_new, wo, "nt")], f"xa_do_{tag}", out_dtype=BF16)
    d_wo = _mm_tn(o, dh_new, f"xa_dwo_{tag}")
    dq, dk, dv = _attn_bwd(q, k, v, do, f"xa_attn_bwd_{tag}")
    d_wq = _mm_tn(nq, dq, f"xa_dwq_{tag}")
    d_wk = _mm_tn(mem_n, dk, f"xa_dwk_{tag}")
    d_wv = _mm_tn(mem_n, dv, f"xa_dwv_{tag}")
    token = push([d_wq, d_wk, d_wv, d_wo])
    dmem_n = _mm([(dk, wk, "nt"), (dv, wv, "nt")], f"xa_dmem_{tag}", after=token)
    _, d_gm = _rms_bwd(dmem_n, mem, _row(g_m), None, f"xa_mem_rms_bwd_{tag}")
    dh, d_gx = _mm([(dq, wq, "nt")], f"xa_dnq_{tag}", rms_bwd=(h, _row(g_x), dh_new), tm=1024, after=token)
    return dh, dict(g_xattn=d_gx, g_mem=d_gm)


def _ffn_fwd(h, n, wgt, wut, wd, g_next, tag, after, loss=None):
    a, b, hid = _ffn_up(n, wgt, wut, f"ffn_up_{tag}", after=after)
    saved = (h, n, a, b, hid)
    if loss is not None:
        return _mm([(hid, wd, "nn")], f"ffn_down_{tag}", res=h, loss=loss, tm=1024), saved
    h_new, n_next = _mm([(hid, wd, "nn")], f"ffn_down_{tag}", res=h, rms_g=_row(g_next), tm=1024)
    return h_new, n_next, saved


def _ffn_bwd(dh_new, saved, g_f, wgt, wut, wd, tag, push):
    h, n, a, b, hid = saved
    da, db = _ffn_dhid(dh_new, wd, a, b, f"ffn_dhid_{tag}")
    d_wd = _mm_tn(hid, dh_new, f"ffn_dwd_{tag}", ts=1024, tn=1024)
    d_wgt = _mm_tn(da, n, f"ffn_dwg_{tag}", ts=1024, tn=1024)
    d_wut = _mm_tn(db, n, f"ffn_dwu_{tag}", ts=1024, tn=1024)
    token = push([d_wgt, d_wut, d_wd])
    dh, d_gf = _mm([(da, wgt, "nn"), (db, wut, "nn")], f"ffn_dn_{tag}", rms_bwd=(h, _row(g_f), dh_new), tm=512,
                   after=token)
    return dh, dict(g_ffn=d_gf)


_XA = ["xa_w_q", "xa_w_k", "xa_w_v", "xa_w_o"]
_FFN = ["ffn_w_gate", "ffn_w_up", "ffn_w_down"]
GATHERS = {
    "ev_in": [("ev_w_in", 0)],
    "xa0": [("ev_w_out", 0)] + [(n, 0) for n in _XA],
    "ffn0": [(n, 0) for n in _FFN],
    "od": [("od_w_in", 0), ("od_w_out", 0)],
    "xa1": [(n, 1) for n in _XA],
    "ffn1": [(n, 1) for n in _FFN],
}
SCATTERS = {
    "ffn1": [(n, 1) for n in _FFN],
    "xa1": [(n, 1) for n in _XA],
    "od": [("od_w_in", 0), ("od_w_out", 0)],
    "ffn0": [(n, 0) for n in _FFN],
    "xa0": [(n, 0) for n in _XA],
    "ev_out": [("ev_w_out", 0)],
    "ev_in": [("ev_w_in", 0)],
}


def _local_step(x, mem, loss_target, W, comm):
    grads = {}

    h0 = x
    (ev_w_in_t,), token = comm.weights("ev_in", None)
    n0 = _rms_fwd(h0, _row(W["g_mix"][0]), "ev_rms", after=token)
    z = _mm([(n0, ev_w_in_t, "nt")], "ev_in", tn=1280)
    token = comm.prefetch(["ffn0"], z)
    ab, ca = _conv_fwd(z, W["ev_a_conv_w"][0], W["ev_a_conv_b"], W["ev_a_ln_g"], W["ev_a_ln_b"],
                       W["ev_b_conv_w"][0], W["ev_b_conv_b"], "ev_conv", after=token)
    (ev_w_out, *xa_w0), _ = comm.weights("xa0", ab)
    h1, nq0 = _mm([(ab, ev_w_out, "nn")], "ev_out", res=h0, rms_g=_row(W["g_xattn"][0]))
    token = comm.prefetch(["od", "xa1"], nq0)
    h2, nf0, xa0 = _xattn_fwd(h1, nq0, mem, W["g_mem"][0], *xa_w0, W["g_ffn"][0], "l0", token)
    ffn_w0, _ = comm.weights("ffn0", nf0)
    token = comm.prefetch(["ffn1"], nf0)
    h3, n3, ff0 = _ffn_fwd(h2, nf0, *ffn_w0, W["g_mix"][1], "l0", token)

    (od_w_in_t, od_w_out), _ = comm.weights("od", n3)
    zp = _mm([(n3, od_w_in_t, "nt")], "od_in", tn=1024)
    D = x.shape[1]
    ws = W["od_w_s"][0].astype(BF16)
    wst = jnp.swapaxes(ws, 1, 2)
    bsb = jnp.repeat(jnp.transpose(W["od_b_s"][0]), D // C_GROUPS, axis=1)
    y_sgu = _sgu_fwd(zp, W["od_c_ln_g"], W["od_c_ln_b"], ws, bsb, "od_sgu")
    h4, nq1 = _mm([(y_sgu, od_w_out, "nn")], "od_out", res=h3, rms_g=_row(W["g_xattn"][1]))
    xa_w1, _ = comm.weights("xa1", nq1)
    h5, nf1, xa1 = _xattn_fwd(h4, nq1, mem, W["g_mem"][1], *xa_w1, W["g_ffn"][1], "l1", None)
    ffn_w1, _ = comm.weights("ffn1", nf1)
    (loss_row, dh6, d_gfinal), ff1 = _ffn_fwd(h5, nf1, *ffn_w1, None, "l1", None,
                                              loss=(_row(W["g_final"]), loss_target))
    grads["g_final"] = d_gfinal.reshape(-1)


    dh5, g_ff1 = _ffn_bwd(dh6, ff1, W["g_ffn"][1], *ffn_w1, "l1", lambda dws: comm.grads("ffn1", dws))
    dh4, g_xa1 = _xattn_bwd(dh5, xa1, mem, W["g_xattn"][1], W["g_mem"][1], *xa_w1, "l1",
                            lambda dws: comm.grads("xa1", dws))
    dy_sgu = _mm([(dh4, od_w_out, "nt")], "od_dy", tn=1024)
    d_od_out = _mm_tn(y_sgu, dh4, "od_dwout", tn=1024)
    dzp, d_ws, d_bsb, d_clng, d_clnb = _sgu_bwd(dy_sgu, zp, W["od_c_ln_g"], W["od_c_ln_b"], ws, wst, bsb, "od_sgu_bwd")
    grads["od_w_s"] = d_ws[None]
    grads["od_b_s"] = jnp.transpose(_group_sum(d_bsb, C_GROUPS, "od_dbs"))[None]
    grads["od_c_ln_g"], grads["od_c_ln_b"] = d_clng, d_clnb
    token = comm.grads("od", [_mm_tn(dzp, n3, "od_dwin", ts=1024, tn=1024), d_od_out])
    dh3, d_gmix1 = _mm([(dzp, od_w_in_t, "nn")], "od_dn", rms_bwd=(h3, _row(W["g_mix"][1]), dh4), tm=1024, after=token)

    dh2, g_ff0 = _ffn_bwd(dh3, ff0, W["g_ffn"][0], *ffn_w0, "l0", lambda dws: comm.grads("ffn0", dws))
    dh1, g_xa0 = _xattn_bwd(dh2, xa0, mem, W["g_xattn"][0], W["g_mem"][0], *xa_w0, "l0",
                            lambda dws: comm.grads("xa0", dws))
    token = comm.grads("ev_out", [_mm_tn(ab, dh1, "ev_dwout", tn=1024)])
    dab = _mm([(dh1, ev_w_out, "nt")], "ev_dab", tn=1024, after=token)
    dca, d_lng, d_lnb, d_ba = _conv_bwd_ln(dab, ca, W["ev_a_ln_g"], W["ev_a_ln_b"], "ev_conv_bwd_ln")
    dz, d_wa, d_wb, d_bb = _conv_bwd(z, dca, dab, W["ev_a_conv_w"][0], W["ev_b_conv_w"][0], W["ev_b_conv_b"],
                                     "ev_conv_bwd")
    grads.update(ev_a_ln_g=d_lng, ev_a_ln_b=d_lnb, ev_a_conv_b=d_ba, ev_b_conv_b=d_bb,
                 ev_a_conv_w=d_wa[None], ev_b_conv_w=d_wb[None])
    token = comm.grads("ev_in", [_mm_tn(dz, n0, "ev_dwin", ts=1024, tn=1024)])
    grad_x, d_gmix0 = _mm([(dz, ev_w_in_t, "nn")], "ev_dn", rms_bwd=(h0, _row(W["g_mix"][0]), dh1), tm=1024, after=token)

    grads["g_mix"] = jnp.concatenate([d_gmix0, d_gmix1], axis=0)
    for key in ("g_xattn", "g_mem"):
        grads[key] = jnp.concatenate([g_xa0[key], g_xa1[key]], axis=0)
    grads["g_ffn"] = jnp.concatenate([g_ff0["g_ffn"], g_ff1["g_ffn"]], axis=0)
    return loss_row, grad_x, grads


class _Exchanges:
    def __init__(self, shards, dev_idx, after):
        self.shards, self.dev_idx = shards, dev_idx
        self.gathering, self.scattering = {}, {}
        self.first = _all_gather_weights(self._pack(GATHERS["ev_in"]), self._rows(GATHERS["ev_in"]), "ag_ev_in",
                                         after=after)
        self.first_token = self.prefetch(["xa0"], self.first[0])

    def _rows(self, entries):
        return [self.shards[e].shape[0] for e in entries]

    def _pack(self, entries):
        return jnp.concatenate([self.shards[e] for e in entries], axis=0)

    def prefetch(self, gathers, after):
        for name in gathers:
            rows = self._rows(GATHERS[name])
            pack = self._pack(GATHERS[name])
            lands = [lax.empty((NDEV * r, pack.shape[1]), pack.dtype) for r in rows]
            plan, n = _gather_plan(rows)
            send, recv, srcs, lands, after = _split_start([pack], lands, plan, n, after, f"ag_{name}_start")
            self.gathering[name] = (send, recv, srcs, lands, plan, rows)
        return after

    def weights(self, name, after):
        if name == "ev_in":
            return self.first, self.first_token
        send, recv, srcs, lands, plan, rows = self.gathering.pop(name)
        _, lands = _split_wait(send, recv, srcs, lands, plan, after, f"ag_{name}_wait")
        return _place_own([self.shards[e] for e in GATHERS[name]], lands, self.dev_idx, f"ag_{name}_own"), None

    def grads(self, name, dws):
        rows = self._rows(SCATTERS[name])
        lands = [lax.empty((NDEV - 1, r, d.shape[1]), d.dtype) for r, d in zip(rows, dws)]
        plan, n = _scatter_plan(rows)
        send, recv, srcs, lands, token = _split_start(dws, lands, plan, n, None, f"rs_{name}_start")
        self.scattering[name] = (send, recv, srcs, lands, plan)
        return token

    def received(self, after):
        out = {}
        for name, (send, recv, srcs, lands, plan) in self.scattering.items():
            srcs, lands = _split_wait(send, recv, srcs, lands, plan, after, f"rs_{name}_wait")
            for entry, g, got in zip(SCATTERS[name], srcs, lands):
                out[entry] = (g, got)
        return out


def kernel(x, mem, g_mix, g_xattn, g_mem, g_ffn, g_final, ev_w_in, ev_a_conv_w, ev_a_conv_b, ev_a_ln_g, ev_a_ln_b, ev_b_conv_w, ev_b_conv_b, ev_w_out, od_w_in, od_c_ln_g, od_c_ln_b, od_w_s, od_b_s, od_w_out, xa_w_q, xa_w_k, xa_w_v, xa_w_o, ffn_w_gate, ffn_w_up, ffn_w_down, loss_target, m_g_mix, m_g_xattn, m_g_mem, m_g_ffn, m_g_final, m_ev_w_in, m_ev_a_conv_w, m_ev_a_conv_b, m_ev_a_ln_g, m_ev_a_ln_b, m_ev_b_conv_w, m_ev_b_conv_b, m_ev_w_out, m_od_w_in, m_od_c_ln_g, m_od_c_ln_b, m_od_w_s, m_od_b_s, m_od_w_out, m_xa_w_q, m_xa_w_k, m_xa_w_v, m_xa_w_o, m_ffn_w_gate, m_ffn_w_up, m_ffn_w_down, v_g_mix, v_g_xattn, v_g_mem, v_g_ffn, v_g_final, v_ev_w_in, v_ev_a_conv_w, v_ev_a_conv_b, v_ev_a_ln_g, v_ev_a_ln_b, v_ev_b_conv_w, v_ev_b_conv_b, v_ev_w_out, v_od_w_in, v_od_c_ln_g, v_od_c_ln_b, v_od_w_s, v_od_b_s, v_od_w_out, v_xa_w_q, v_xa_w_k, v_xa_w_v, v_xa_w_o, v_ffn_w_gate, v_ffn_w_up, v_ffn_w_down):
    local = dict(g_mix=g_mix, g_xattn=g_xattn, g_mem=g_mem, g_ffn=g_ffn, g_final=g_final, ev_w_in=ev_w_in, ev_a_conv_w=ev_a_conv_w, ev_a_conv_b=ev_a_conv_b, ev_a_ln_g=ev_a_ln_g, ev_a_ln_b=ev_a_ln_b, ev_b_conv_w=ev_b_conv_w, ev_b_conv_b=ev_b_conv_b, ev_w_out=ev_w_out, od_w_in=od_w_in, od_c_ln_g=od_c_ln_g, od_c_ln_b=od_c_ln_b, od_w_s=od_w_s, od_b_s=od_b_s, od_w_out=od_w_out, xa_w_q=xa_w_q, xa_w_k=xa_w_k, xa_w_v=xa_w_v, xa_w_o=xa_w_o, ffn_w_gate=ffn_w_gate, ffn_w_up=ffn_w_up, ffn_w_down=ffn_w_down)
    mom = dict(g_mix=m_g_mix, g_xattn=m_g_xattn, g_mem=m_g_mem, g_ffn=m_g_ffn, g_final=m_g_final, ev_w_in=m_ev_w_in, ev_a_conv_w=m_ev_a_conv_w, ev_a_conv_b=m_ev_a_conv_b, ev_a_ln_g=m_ev_a_ln_g, ev_a_ln_b=m_ev_a_ln_b, ev_b_conv_w=m_ev_b_conv_w, ev_b_conv_b=m_ev_b_conv_b, ev_w_out=m_ev_w_out, od_w_in=m_od_w_in, od_c_ln_g=m_od_c_ln_g, od_c_ln_b=m_od_c_ln_b, od_w_s=m_od_w_s, od_b_s=m_od_b_s, od_w_out=m_od_w_out, xa_w_q=m_xa_w_q, xa_w_k=m_xa_w_k, xa_w_v=m_xa_w_v, xa_w_o=m_xa_w_o, ffn_w_gate=m_ffn_w_gate, ffn_w_up=m_ffn_w_up, ffn_w_down=m_ffn_w_down)
    vel = dict(g_mix=v_g_mix, g_xattn=v_g_xattn, g_mem=v_g_mem, g_ffn=v_g_ffn, g_final=v_g_final, ev_w_in=v_ev_w_in, ev_a_conv_w=v_ev_a_conv_w, ev_a_conv_b=v_ev_a_conv_b, ev_a_ln_g=v_ev_a_ln_g, ev_a_ln_b=v_ev_a_ln_b, ev_b_conv_w=v_ev_b_conv_w, ev_b_conv_b=v_ev_b_conv_b, ev_w_out=v_ev_w_out, od_w_in=v_od_w_in, od_c_ln_g=v_od_c_ln_g, od_c_ln_b=v_od_c_ln_b, od_w_s=v_od_w_s, od_b_s=v_od_b_s, od_w_out=v_od_w_out, xa_w_q=v_xa_w_q, xa_w_k=v_xa_w_k, xa_w_v=v_xa_w_v, xa_w_o=v_xa_w_o, ffn_w_gate=v_ffn_w_gate, ffn_w_up=v_ffn_w_up, ffn_w_down=v_ffn_w_down)
    D = x.shape[-1]
    dev = 4 * lax.axis_index("x") + 2 * lax.axis_index("y") + lax.axis_index("c")

    def comm_layout(n, a):
        return jnp.transpose(a) if _shard_axis(n) == 2 else a

    shards = {(n, i): comm_layout(n, local[n][i]).astype(BF16) for n in BIG for i in range(local[n].shape[0])}
    small_sizes = [local[n].size for n in SMALL_SHARDED]
    small_block = _pad_rows(jnp.concatenate([local[n].reshape(-1) for n in SMALL_SHARDED]), 128, 8)
    small_all = _all_gather(small_block, "ag_small")
    comm = _Exchanges(shards, jnp.reshape(dev, (1,)).astype(jnp.int32), small_all)
    small_all = small_all.reshape(NDEV, -1)

    W = {n: local[n] for n in REPLICATED}
    o0 = 0
    for n, sz in zip(SMALL_SHARDED, small_sizes):
        blocks = small_all[:, o0:o0 + sz].reshape((NDEV,) + local[n].shape)
        W[n] = _full_from_blocks(blocks, _shard_axis(n))
        o0 += sz

    loss_row, grad_x, grads = _local_step(x[0], mem[0], loss_target[0], W, comm)

    received = comm.received(grad_x)
    rest = REPLICATED + SMALL_SHARDED
    rest_full_shapes = [grads[n].shape for n in rest]
    g_rest = _pad_rows(jnp.concatenate([grads[n].astype(F32).reshape(-1) for n in rest]), D, 8)
    small_rows = g_rest.shape[0]
    small_plan, small_n = _gather_plan([small_rows])
    small_send, small_recv, small_srcs, small_lands, token = _split_start(
        [g_rest], [lax.empty((NDEV * small_rows, D), F32)], small_plan, small_n, received["ev_w_in", 0][1],
        "ag_small_grads_start")

    gsh, delta, new_m, new_v = {}, {}, {}, {}
    def stacked_layout(n, a):
        return jnp.swapaxes(a, 1, 2) if _shard_axis(n) == 2 else a

    for n in BIG:
        parts = [received[n, i] for i in range(local[n].shape[0])]
        outs = _finish_weight([p[0] for p in parts], [p[1] for p in parts], comm.dev_idx,
                              *(stacked_layout(n, a) for a in (local[n], mom[n], vel[n])), f"finish_{n}", after=token)
        gsh[n], delta[n], new_m[n], new_v[n] = (stacked_layout(n, o) for o in outs)

    _, small_lands = _split_wait(small_send, small_recv, small_srcs, small_lands, small_plan,
                                 [delta[n] for n in BIG], "ag_small_grads_wait")
    partials = _place_own([g_rest], small_lands, comm.dev_idx, "ag_small_grads_own")[0]
    g_rest = _sum_slots(partials.reshape(NDEV, small_rows, D), "sum_small_grads").reshape(-1)
    o0 = 0
    for n, shp in zip(rest, rest_full_shapes):
        sz = 1
        for s in shp:
            sz *= s
        full = g_rest[o0:o0 + sz].reshape(shp)
        o0 += sz
        if n in SMALL_SHARDED:
            full = lax.dynamic_index_in_dim(_blocks_from_full(full, _shard_axis(n)), dev, 0, keepdims=False)
        gsh[n] = full.reshape(local[n].shape)

    small = _adamw_small([local[n] for n in rest], [gsh[n] for n in rest], [mom[n] for n in rest],
                         [vel[n] for n in rest], "adamw_small")
    for n, (d, nm, nv) in zip(rest, small):
        delta[n], new_m[n], new_v[n] = d, nm, nv

    loss = lax.psum(loss_row[0, 0], ("x", "y", "c"))
    return (loss, grad_x[None], *[gsh[n] for n in WEIGHTS], *[delta[n] for n in WEIGHTS],
            *[new_m[n] for n in WEIGHTS], *[new_v[n] for n in WEIGHTS])
```

```python
import jax
import jax.numpy as jnp
from jax import lax
from jax.experimental import pallas as pl
from jax.experimental.pallas import tpu as pltpu

F32, BF16 = jnp.float32, jnp.bfloat16
NDEV = 8
RMS_EPS = 1e-6
LN_EPS = 1e-5
CHUNK = 128
C_GROUPS = 8
XA_HEADS = 4
ADAM_LR, ADAM_B1, ADAM_B2, ADAM_EPS, ADAM_WD, ADAM_STEP = 0.001, 0.9, 0.999, 1e-08, 0.01, 10
HALO = 16
ROW_CHUNK = 32
ROW_CHUNK_FWD = 64
V7X_VMEM_LIMIT = 56 * 1024 * 1024
MESH = pl.DeviceIdType.MESH

TS_ROW = 512
TS_MM = 2048
TN_MM = 1408
TS_FFN = 512
MM_ROW_CHUNK = 256
FFN_COL_CHUNK = 256
TS_CONV = 512
TS_SGU = 512
TS_ATTN = 2048


def _cp(*sem):
    return pltpu.CompilerParams(dimension_semantics=sem, vmem_limit_bytes=V7X_VMEM_LIMIT)


def _pick(n, pref, align):
    for t in range(min(n, pref), 0, -1):
        if n % t == 0 and (t % align == 0 or t == n):
            return t
    return n


def _sigmoid(x):
    return 0.5 * jnp.tanh(0.5 * x) + 0.5


def _dot(a, b):
    return jnp.dot(a, b, preferred_element_type=F32)


def _dot_nt(a, b):
    return lax.dot_general(a, b, (((1,), (1,)), ((), ())), preferred_element_type=F32)


def _dot_tn(a, b):
    return lax.dot_general(a, b, (((0,), (0,)), ((), ())), preferred_element_type=F32)


_ANY = pl.BlockSpec(memory_space=pl.ANY)
_RESIDENT = pl.Buffered(1)


def _after(after):
    if after is None:
        return [], []
    ops = list(after) if isinstance(after, (list, tuple)) else [after]
    return [_ANY] * len(ops), ops


def _rms_fwd(h, g, name, after=None):
    S, D = h.shape
    ts = _pick(S, TS_MM, 16)
    after_specs, after_ops = _after(after)

    def body(h_ref, g_ref, *rest):
        o_ref = rest[-1]
        x = h_ref[...]
        r = lax.rsqrt(jnp.mean(x * x, axis=-1, keepdims=True) + RMS_EPS)
        o_ref[...] = ((x * r) * g_ref[...]).astype(o_ref.dtype)

    return pl.pallas_call(
        body, grid=(S // ts,),
        in_specs=[pl.BlockSpec((ts, D), lambda i: (i, 0)), pl.BlockSpec((1, D), lambda i: (0, 0))] + after_specs,
        out_specs=pl.BlockSpec((ts, D), lambda i: (i, 0)),
        out_shape=jax.ShapeDtypeStruct((S, D), BF16), compiler_params=_cp("parallel"), name=name)(h, g, *after_ops)


def _mm(pairs, name, out_dtype=F32, res=None, rms_g=None, rms_bwd=None, loss=None, tm=None, tn=None, after=None):
    M = pairs[0][0].shape[0]
    N = pairs[0][1].shape[1 if pairs[0][2] == "nn" else 0]
    whole_rows = rms_g is not None or rms_bwd is not None or loss is not None
    tm = _pick(M, tm or TS_MM, 16)
    tn = N if whole_rows else _pick(N, tn or TN_MM, 128)
    npair = len(pairs)
    modes = [p[2] for p in pairs]
    after_specs, after_ops = _after(after)

    rc = MM_ROW_CHUNK if whole_rows and tm % MM_ROW_CHUNK == 0 else tm

    def body(*refs):
        rest = refs[2 * npair + len(after_ops):]
        res_ref = None
        if res is not None:
            res_ref, rest = rest[0], rest[1:]
        if rms_bwd is not None:
            dg_ref = rest[4]

            @pl.when(pl.program_id(0) == 0)
            def _():
                dg_ref[...] = jnp.zeros_like(dg_ref)

        if loss is not None:
            g_ref, t_ref, loss_ref, dh_ref, dg_ref = rest

            @pl.when(pl.program_id(0) == 0)
            def _():
                dg_ref[...] = jnp.zeros_like(dg_ref)
                loss_ref[...] = jnp.zeros_like(loss_ref)

        for r0 in range(0, tm, rc):
            rows = pl.ds(r0, rc)
            acc = None
            for p in range(npair):
                a_ = refs[2 * p][rows, :].astype(BF16)
                d = _dot(a_, refs[2 * p + 1][...]) if modes[p] == "nn" else _dot_nt(a_, refs[2 * p + 1][...])
                acc = d if acc is None else acc + d
            if res_ref is not None:
                acc = acc + res_ref[rows, :]
            if rms_bwd is not None:
                h_ref, g_ref, dres_ref, dh_ref, _ = rest
                x = h_ref[rows, :]
                r = lax.rsqrt(jnp.mean(x * x, axis=-1, keepdims=True) + RMS_EPS)
                xr = x * r
                dg_ref[...] += jnp.sum(acc * xr, axis=0, keepdims=True)
                u = acc * g_ref[...]
                dh_ref[rows, :] = r * u - xr * (r * jnp.mean(u * xr, axis=-1, keepdims=True)) + dres_ref[rows, :]
            elif loss is not None:
                r = lax.rsqrt(jnp.mean(acc * acc, axis=-1, keepdims=True) + RMS_EPS)
                xr = acc * r
                gg = g_ref[...]
                e = xr * gg - t_ref[rows, :]
                chunk_loss = jnp.sum(jnp.sum(e * e, axis=0, keepdims=True), axis=1, keepdims=True) * (0.5 / N)
                loss_ref[...] += jnp.broadcast_to(chunk_loss, loss_ref.shape)
                dy = e * (1.0 / N)
                dg_ref[...] += jnp.sum(dy * xr, axis=0, keepdims=True)
                u = dy * gg
                dh_ref[rows, :] = r * u - xr * (r * jnp.mean(u * xr, axis=-1, keepdims=True))
            elif rms_g is not None:
                g_ref, o_ref, n_ref = rest
                o_ref[rows, :] = acc
                r = lax.rsqrt(jnp.mean(acc * acc, axis=-1, keepdims=True) + RMS_EPS)
                n_ref[rows, :] = ((acc * r) * g_ref[...]).astype(BF16)
            else:
                rest[0][rows, :] = acc.astype(rest[0].dtype)

    in_specs, ins = [], []
    for a, w, mode in pairs:
        K = a.shape[1]
        in_specs.append(pl.BlockSpec((tm, K), lambda i, j: (i, 0)))
        once = _RESIDENT if tn == N else None
        in_specs.append(pl.BlockSpec((K, tn), lambda i, j: (0, j), pipeline_mode=once) if mode == "nn"
                        else pl.BlockSpec((tn, K), lambda i, j: (j, 0), pipeline_mode=once))
        ins += [a, w]
    in_specs += after_specs
    ins += after_ops
    tile = pl.BlockSpec((tm, tn), lambda i, j: (i, j))
    vec = pl.BlockSpec((1, tn), lambda i, j: (0, j))
    if res is not None:
        in_specs.append(tile)
        ins.append(res)
    sem = ("parallel", "parallel")
    if rms_bwd is not None:
        in_specs += [tile, vec, tile]
        ins += list(rms_bwd)
        out_specs = [tile, vec]
        out_shape = [jax.ShapeDtypeStruct((M, N), F32), jax.ShapeDtypeStruct((1, N), F32)]
        sem = ("arbitrary", "arbitrary")
    elif loss is not None:
        in_specs += [vec, tile]
        ins += list(loss)
        out_specs = [pl.BlockSpec((1, 128), lambda i, j: (0, 0)), tile, vec]
        out_shape = [jax.ShapeDtypeStruct((1, 128), F32), jax.ShapeDtypeStruct((M, N), F32),
                     jax.ShapeDtypeStruct((1, N), F32)]
        sem = ("arbitrary", "arbitrary")
    elif rms_g is not None:
        in_specs.append(vec)
        ins.append(rms_g)
        out_specs = [tile, tile]
        out_shape = [jax.ShapeDtypeStruct((M, N), F32), jax.ShapeDtypeStruct((M, N), BF16)]
    else:
        out_specs = tile
        out_shape = jax.ShapeDtypeStruct((M, N), out_dtype)
    return pl.pallas_call(
        body, grid=(M // tm, N // tn), in_specs=in_specs, out_specs=out_specs, out_shape=out_shape,
        compiler_params=_cp(*sem), name=name)(*ins)


def _mm_tn(a, b, name, ts=None, tn=None):
    S, K = a.shape
    N = b.shape[1]
    ts = _pick(S, ts or TS_MM, 16)
    tn = _pick(N, tn or TN_MM, 128)
    nsteps = S // ts

    def body(a_ref, b_ref, o_ref, acc_ref):
        s = pl.program_id(1)

        @pl.when(s == 0)
        def _():
            acc_ref[...] = jnp.zeros_like(acc_ref)

        acc_ref[...] += _dot_tn(a_ref[...].astype(BF16), b_ref[...].astype(BF16))

        @pl.when(s == nsteps - 1)
        def _():
            o_ref[...] = acc_ref[...].astype(o_ref.dtype)

    return pl.pallas_call(
        body, grid=(N // tn, nsteps),
        in_specs=[pl.BlockSpec((ts, K), lambda j, s: (s, 0)), pl.BlockSpec((ts, tn), lambda j, s: (s, j))],
        out_specs=pl.BlockSpec((K, tn), lambda j, s: (0, j)), out_shape=jax.ShapeDtypeStruct((K, N), BF16),
        scratch_shapes=[pltpu.VMEM((K, tn), F32)],
        compiler_params=_cp("parallel", "arbitrary"), name=name)(a, b)


def _col_chunks(n):
    return [(c0, min(FFN_COL_CHUNK, n - c0)) for c0 in range(0, n, FFN_COL_CHUNK)]


def _ffn_up(n, wgt, wut, name, after=None):
    S, D = n.shape
    F = wgt.shape[0]
    tm = _pick(S, TS_FFN, 16)
    after_specs, after_ops = _after(after)

    def body(n_ref, wg_ref, wu_ref, *rest):
        a_ref, b_ref, hid_ref = rest[-3:]
        x = n_ref[...]
        for c0, ce in _col_chunks(F):
            a = _dot_nt(x, wg_ref[c0:c0 + ce, :])
            b = _dot_nt(x, wu_ref[c0:c0 + ce, :])
            a_ref[:, c0:c0 + ce] = a.astype(BF16)
            b_ref[:, c0:c0 + ce] = b.astype(BF16)
            hid_ref[:, c0:c0 + ce] = (a * _sigmoid(a) * b).astype(BF16)

    wspec = pl.BlockSpec((F, D), lambda i: (0, 0), pipeline_mode=_RESIDENT)
    ospec = pl.BlockSpec((tm, F), lambda i: (i, 0))
    osh = jax.ShapeDtypeStruct((S, F), BF16)
    return pl.pallas_call(
        body, grid=(S // tm,),
        in_specs=[pl.BlockSpec((tm, D), lambda i: (i, 0)), wspec, wspec] + after_specs,
        out_specs=[ospec, ospec, ospec], out_shape=[osh, osh, osh],
        compiler_params=_cp("parallel"), name=name)(n, wgt, wut, *after_ops)


def _ffn_dhid(dh, wd, a, b, name):
    S, D = dh.shape
    F = wd.shape[0]
    tm = _pick(S, TS_FFN, 16)

    def body(dh_ref, wd_ref, a_ref, b_ref, da_ref, db_ref):
        x = dh_ref[...].astype(BF16)
        for c0, ce in _col_chunks(F):
            g = _dot_nt(x, wd_ref[c0:c0 + ce, :]).astype(BF16)
            a_ = a_ref[:, c0:c0 + ce]
            sg = _sigmoid(a_)
            silu = a_ * sg
            da_ref[:, c0:c0 + ce] = (g * b_ref[:, c0:c0 + ce]) * (sg + silu * (1.0 - sg))
            db_ref[:, c0:c0 + ce] = g * silu

    tile = pl.BlockSpec((tm, F), lambda i: (i, 0))
    osh = jax.ShapeDtypeStruct((S, F), BF16)
    return pl.pallas_call(
        body, grid=(S // tm,),
        in_specs=[pl.BlockSpec((tm, D), lambda i: (i, 0)),
                  pl.BlockSpec((F, D), lambda i: (0, 0), pipeline_mode=_RESIDENT), tile, tile],
        out_specs=[tile, tile], out_shape=[osh, osh],
        compiler_params=_cp("parallel"), name=name)(dh, wd, a, b)


def _softmax_rows(s):
    m = jnp.max(s, axis=-1, keepdims=True)
    p = jnp.exp(s - m)
    return p / jnp.sum(p, axis=-1, keepdims=True)


def _attn_fwd(q, k, v, name):
    S, D = q.shape
    M = k.shape[0]
    hd = D // XA_HEADS
    scale = hd ** -0.5
    ts = _pick(S, TS_ATTN, 16)

    def body(q_ref, k_ref, v_ref, o_ref):
        for h in range(XA_HEADS):
            sl = slice(h * hd, (h + 1) * hd)
            p = _softmax_rows(_dot_nt(q_ref[:, sl], k_ref[:, sl]) * scale)
            o_ref[:, sl] = _dot(p.astype(BF16), v_ref[:, sl]).astype(BF16)

    tile = pl.BlockSpec((ts, D), lambda i: (i, 0))
    memspec = pl.BlockSpec((M, D), lambda i: (0, 0))
    return pl.pallas_call(
        body, grid=(S // ts,), in_specs=[tile, memspec, memspec], out_specs=tile,
        out_shape=jax.ShapeDtypeStruct((S, D), BF16), compiler_params=_cp("parallel"), name=name)(q, k, v)


def _attn_bwd(q, k, v, do, name):
    S, D = q.shape
    M = k.shape[0]
    hd = D // XA_HEADS
    scale = hd ** -0.5
    ts = _pick(S, TS_ATTN, 16)

    def body(q_ref, k_ref, v_ref, do_ref, dq_ref, dk_ref, dv_ref):
        @pl.when(pl.program_id(0) == 0)
        def _():
            dk_ref[...] = jnp.zeros_like(dk_ref)
            dv_ref[...] = jnp.zeros_like(dv_ref)

        for h in range(XA_HEADS):
            sl = slice(h * hd, (h + 1) * hd)
            qh, kh, vh, doh = q_ref[:, sl], k_ref[:, sl], v_ref[:, sl], do_ref[:, sl]
            p = _softmax_rows(_dot_nt(qh, kh) * scale)
            dp = _dot_nt(doh, vh)
            dv_ref[:, sl] += _dot_tn(p.astype(BF16), doh)
            delta = jnp.sum(dp * p, axis=-1, keepdims=True)
            ds = (p * (dp - delta) * scale).astype(BF16)
            dq_ref[:, sl] = _dot(ds, kh).astype(BF16)
            dk_ref[:, sl] += _dot_tn(ds, qh)

    tile = pl.BlockSpec((ts, D), lambda i: (i, 0))
    memspec = pl.BlockSpec((M, D), lambda i: (0, 0))
    return pl.pallas_call(
        body, grid=(S // ts,), in_specs=[tile, memspec, memspec, tile], out_specs=[tile, memspec, memspec],
        out_shape=[jax.ShapeDtypeStruct((S, D), BF16), jax.ShapeDtypeStruct((M, D), F32),
                   jax.ShapeDtypeStruct((M, D), F32)],
        compiler_params=_cp("arbitrary"), name=name)(q, k, v, do)


def _halo_specs(ts, col):
    per = ts // HALO

    def prev(i):
        return (jnp.maximum(i * per - 1, 0), col)

    def nxt(i, n_tiles):
        return (jnp.minimum((i + 1) * per, n_tiles * per - 1), col)

    return prev, nxt


def _fill_ext(ext_ref, prev_val, main_val, next_val, first, last, ts):
    ext_ref[pl.ds(0, HALO), :] = jnp.where(first, 0.0, prev_val)
    ext_ref[pl.ds(HALO, ts), :] = main_val
    ext_ref[pl.ds(HALO + ts, HALO), :] = jnp.where(last, 0.0, next_val)


SUBLANES = 8


def _fill_shifted(sh_ref, ts):
    n = ts + 2 * HALO - SUBLANES
    for s in range(1, SUBLANES):
        sh_ref[s, pl.ds(0, n), :] = sh_ref[0, pl.ds(s, n), :]


def _tap(sh_ref, r0, offset, rc):
    q, s = divmod(offset, SUBLANES)
    return sh_ref[s, pl.ds(pl.multiple_of(r0 + SUBLANES * q, SUBLANES), rc), :]


def _conv_fwd(z, wa, ba, lng, lnb, wb, bb, name, after=None):
    S = z.shape[0]
    C = z.shape[1] // 5
    KA, KB = wa.shape[0], wb.shape[0]
    pa, pb = KA // 2, KB // 2
    assert pa <= HALO and pb <= HALO
    ts = _pick(S, TS_CONV, ROW_CHUNK_FWD)
    nt = S // ts
    rc = ROW_CHUNK_FWD
    prev, nxt = _halo_specs(ts, 0)
    after_specs, after_ops = _after(after)

    def body(*refs):
        compute(*refs[:9], *refs[9 + len(after_ops):])

    def compute(z_ref, zp_ref, zn_ref, wa_ref, ba_ref, lng_ref, lnb_ref, wb_ref, bb_ref, ab_ref, ca_ref,
                ga_sh, tb_ext, win_b):
        i = pl.program_id(0)
        first, last = i == 0, i == nt - 1

        def glu(r):
            return r[:, 0:C] * _sigmoid(r[:, C:2 * C])

        def gcb(r):
            return r[:, 4 * C:5 * C] * r[:, 2 * C:3 * C]

        _fill_ext(ga_sh.at[0], glu(zp_ref), glu(z_ref), glu(zn_ref), first, last, ts)
        _fill_shifted(ga_sh, ts)
        _fill_ext(tb_ext, gcb(zp_ref), gcb(z_ref), gcb(zn_ref), first, last, ts)

        def chunk(c, carry):
            r0 = pl.multiple_of(c * rc, rc)
            win_b[...] = tb_ext[pl.ds(r0, rc + 2 * HALO), :]
            acc = jnp.zeros((rc, C), F32)
            for k in range(KA):
                acc = acc + wa_ref[k:k + 1, :] * _tap(ga_sh, r0, HALO - pa + k, rc)
            ca = acc + ba_ref[...]
            ca_ref[pl.ds(r0, rc), :] = ca
            mu = jnp.mean(ca, axis=-1, keepdims=True)
            xc = ca - mu
            var = jnp.mean(xc * xc, axis=-1, keepdims=True)
            ln = xc * lax.rsqrt(var + LN_EPS) * lng_ref[...] + lnb_ref[...]
            ab_ref[pl.ds(r0, rc), 0:C] = (ln * _sigmoid(ln)).astype(BF16)
            cb = jnp.zeros((rc, C), F32) + bb_ref[...]
            for k in range(KB):
                cb = cb + wb_ref[k:k + 1, :] * win_b[pl.ds(HALO - pb + k, rc), :]
            ab_ref[pl.ds(r0, rc), C:2 * C] = (z_ref[pl.ds(r0, rc), 3 * C:4 * C] * cb).astype(BF16)
            return carry

        lax.fori_loop(0, ts // rc, chunk, 0)

    zspec = pl.BlockSpec((ts, 5 * C), lambda i: (i, 0))
    zprev = pl.BlockSpec((HALO, 5 * C), prev)
    znext = pl.BlockSpec((HALO, 5 * C), lambda i: nxt(i, nt))

    def full(a):
        return pl.BlockSpec(a.shape, lambda i: (0, 0))

    return pl.pallas_call(
        body, grid=(nt,),
        in_specs=[zspec, zprev, znext, full(wa), full(ba), full(lng), full(lnb), full(wb), full(bb)] + after_specs,
        out_specs=[pl.BlockSpec((ts, 2 * C), lambda i: (i, 0)), pl.BlockSpec((ts, C), lambda i: (i, 0))],
        out_shape=[jax.ShapeDtypeStruct((S, 2 * C), BF16), jax.ShapeDtypeStruct((S, C), F32)],
        scratch_shapes=[pltpu.VMEM((SUBLANES, ts + 2 * HALO, C), F32), pltpu.VMEM((ts + 2 * HALO, C), F32),
                        pltpu.VMEM((rc + 2 * HALO, C), F32)],
        compiler_params=_cp("parallel"), name=name)(z, z, z, wa, ba, lng, lnb, wb, bb, *after_ops)


def _conv_bwd_ln(dab, ca, lng, lnb, name):
    S, C = ca.shape
    ts = _pick(S, TS_ROW, 8)

    def body(da_ref, ca_ref, lng_ref, lnb_ref, dca_ref, dg_ref, db_ref, dbias_ref):
        @pl.when(pl.program_id(0) == 0)
        def _():
            dg_ref[...] = jnp.zeros_like(dg_ref)
            db_ref[...] = jnp.zeros_like(db_ref)
            dbias_ref[...] = jnp.zeros_like(dbias_ref)

        ca_ = ca_ref[...]
        mu = jnp.mean(ca_, axis=-1, keepdims=True)
        xc = ca_ - mu
        rstd = lax.rsqrt(jnp.mean(xc * xc, axis=-1, keepdims=True) + LN_EPS)
        xh = xc * rstd
        ln = xh * lng_ref[...] + lnb_ref[...]
        sg = _sigmoid(ln)
        dln = da_ref[...].astype(F32) * (sg * (1.0 + ln * (1.0 - sg)))
        dg_ref[...] += jnp.sum(dln * xh, axis=0, keepdims=True)
        db_ref[...] += jnp.sum(dln, axis=0, keepdims=True)
        dxh = dln * lng_ref[...]
        dca = rstd * (dxh - jnp.mean(dxh, axis=-1, keepdims=True) - xh * jnp.mean(dxh * xh, axis=-1, keepdims=True))
        dca_ref[...] = dca
        dbias_ref[...] += jnp.sum(dca, axis=0, keepdims=True)

    tile = pl.BlockSpec((ts, C), lambda i: (i, 0))
    vec = pl.BlockSpec((1, C), lambda i: (0, 0))
    vsh = jax.ShapeDtypeStruct((1, C), F32)
    return pl.pallas_call(
        body, grid=(S // ts,), in_specs=[tile, tile, vec, vec], out_specs=[tile, vec, vec, vec],
        out_shape=[jax.ShapeDtypeStruct((S, C), F32), vsh, vsh, vsh],
        compiler_params=_cp("arbitrary"), name=name)(dab, ca, lng, lnb)


def _conv_bwd(z, dca, dab, wa, wb, bb, name):
    S = z.shape[0]
    C = z.shape[1] // 5
    KA, KB = wa.shape[0], wb.shape[0]
    pa, pb = KA // 2, KB // 2
    ts = _pick(S, TS_CONV, ROW_CHUNK)
    nt = S // ts
    rc = ROW_CHUNK
    prev0, nxt0 = _halo_specs(ts, 0)
    prev1, nxt1 = _halo_specs(ts, 1)

    def body(z_ref, zp_ref, zn_ref, dca_ref, dcap_ref, dcan_ref, db_ref, dbp_ref, dbn_ref, wa_ref, wb_ref, bb_ref,
             dz_ref, dwa_ref, dwb_ref, dbb_ref,
             ga_sh, dca_sh, tb_ext, dcb_ext, win_tb, win_dcb, acc_a, acc_b, acc_bias):
        i = pl.program_id(0)
        first, last = i == 0, i == nt - 1

        @pl.when(first)
        def _():
            acc_a[...] = jnp.zeros_like(acc_a)
            acc_b[...] = jnp.zeros_like(acc_b)
            acc_bias[...] = jnp.zeros_like(acc_bias)

        def glu(r):
            return r[:, 0:C] * _sigmoid(r[:, C:2 * C])

        def gcb(r):
            return r[:, 4 * C:5 * C] * r[:, 2 * C:3 * C]

        def dcb(d, r):
            return d[...].astype(F32) * r[:, 3 * C:4 * C]

        _fill_ext(ga_sh.at[0], glu(zp_ref), glu(z_ref), glu(zn_ref), first, last, ts)
        _fill_shifted(ga_sh, ts)
        _fill_ext(dca_sh.at[0], dcap_ref[...], dca_ref[...], dcan_ref[...], first, last, ts)
        _fill_shifted(dca_sh, ts)
        _fill_ext(tb_ext, gcb(zp_ref), gcb(z_ref), gcb(zn_ref), first, last, ts)
        _fill_ext(dcb_ext, dcb(dbp_ref, zp_ref), dcb(db_ref, z_ref), dcb(dbn_ref, zn_ref), first, last, ts)

        def fold(x):
            return jnp.sum(x.reshape(rc // 8, 8, C), axis=0)

        def chunk(c, carry):
            r0 = pl.multiple_of(c * rc, rc)
            win_tb[...] = tb_ext[pl.ds(r0, rc + 2 * HALO), :]
            win_dcb[...] = dcb_ext[pl.ds(r0, rc + 2 * HALO), :]
            dca_c = _tap(dca_sh, r0, HALO, rc)
            dglu = jnp.zeros((rc, C), F32)
            for k in range(KA):
                dglu = dglu + wa_ref[k:k + 1, :] * _tap(dca_sh, r0, HALO + pa - k, rc)
                acc_a[k] += fold(dca_c * _tap(ga_sh, r0, HALO - pa + k, rc))
            val = z_ref[pl.ds(r0, rc), 0:C]
            sg = _sigmoid(z_ref[pl.ds(r0, rc), C:2 * C])
            dz_ref[pl.ds(r0, rc), 0:C] = (dglu * sg).astype(BF16)
            dz_ref[pl.ds(r0, rc), C:2 * C] = (dglu * val * sg * (1.0 - sg)).astype(BF16)
            dcb_c = win_dcb[pl.ds(HALO, rc), :]
            cb = jnp.zeros((rc, C), F32) + bb_ref[...]
            dt = jnp.zeros((rc, C), F32)
            for k in range(KB):
                tb_k = win_tb[pl.ds(HALO - pb + k, rc), :]
                cb = cb + wb_ref[k:k + 1, :] * tb_k
                dt = dt + wb_ref[k:k + 1, :] * win_dcb[pl.ds(HALO + pb - k, rc), :]
                acc_b[k] += fold(dcb_c * tb_k)
            acc_bias[...] += fold(dcb_c)
            db_c = db_ref[pl.ds(r0, rc), :].astype(F32)
            dz_ref[pl.ds(r0, rc), 2 * C:3 * C] = (dt * z_ref[pl.ds(r0, rc), 4 * C:5 * C]).astype(BF16)
            dz_ref[pl.ds(r0, rc), 3 * C:4 * C] = (db_c * cb).astype(BF16)
            dz_ref[pl.ds(r0, rc), 4 * C:5 * C] = (dt * z_ref[pl.ds(r0, rc), 2 * C:3 * C]).astype(BF16)
            return carry

        lax.fori_loop(0, ts // rc, chunk, 0)

        @pl.when(last)
        def _():
            dwa_ref[...] = jnp.sum(acc_a[...], axis=1)
            dwb_ref[...] = jnp.sum(acc_b[...], axis=1)
            dbb_ref[...] = jnp.sum(acc_bias[...], axis=0, keepdims=True)

    zspec = pl.BlockSpec((ts, 5 * C), lambda i: (i, 0))
    zprev = pl.BlockSpec((HALO, 5 * C), prev0)
    znext = pl.BlockSpec((HALO, 5 * C), lambda i: nxt0(i, nt))
    dspec = pl.BlockSpec((ts, C), lambda i: (i, 0))
    dprev = pl.BlockSpec((HALO, C), prev0)
    dnext = pl.BlockSpec((HALO, C), lambda i: nxt0(i, nt))
    bspec = pl.BlockSpec((ts, C), lambda i: (i, 1))
    bprev = pl.BlockSpec((HALO, C), prev1)
    bnext = pl.BlockSpec((HALO, C), lambda i: nxt1(i, nt))

    def full(shape):
        return pl.BlockSpec(shape, lambda i: (0,) * len(shape))

    ext = pltpu.VMEM((ts + 2 * HALO, C), F32)
    shifted = pltpu.VMEM((SUBLANES, ts + 2 * HALO, C), F32)
    win = pltpu.VMEM((rc + 2 * HALO, C), F32)
    return pl.pallas_call(
        body, grid=(nt,),
        in_specs=[zspec, zprev, znext, dspec, dprev, dnext, bspec, bprev, bnext,
                  full(wa.shape), full(wb.shape), full(bb.shape)],
        out_specs=[pl.BlockSpec((ts, 5 * C), lambda i: (i, 0)), full((KA, C)), full((KB, C)), full((1, C))],
        out_shape=[jax.ShapeDtypeStruct((S, 5 * C), BF16), jax.ShapeDtypeStruct((KA, C), F32),
                   jax.ShapeDtypeStruct((KB, C), F32), jax.ShapeDtypeStruct((1, C), F32)],
        scratch_shapes=[shifted, shifted, ext, ext, win, win,
                        pltpu.VMEM((KA, 8, C), F32), pltpu.VMEM((KB, 8, C), F32), pltpu.VMEM((8, C), F32)],
        compiler_params=_cp("arbitrary"), name=name)(z, z, z, dca, dca, dca, dab, dab, dab, wa, wb, bb)


_GELU_C = 0.7978845608028654
_GELU_A = 0.044715


def _gelu(x):
    return 0.5 * x * (1.0 + jnp.tanh(_GELU_C * (x + _GELU_A * (x * x * x))))


def _gelu_and_grad(x):
    t = jnp.tanh(_GELU_C * (x + _GELU_A * (x * x * x)))
    hx = 0.5 * x
    return hx * (1.0 + t), 0.5 * (1.0 + t) + hx * (1.0 - t * t) * (_GELU_C * (1.0 + 3.0 * _GELU_A * x * x))


def _sgu_fwd(zp, lng, lnb, ws, bsb, name):
    S = zp.shape[0]
    D = zp.shape[1] // 2
    G = ws.shape[0]
    gd = D // G
    ts = _pick(S, TS_SGU, CHUNK)
    ncs = ts // CHUNK

    def body(zp_ref, lng_ref, lnb_ref, ws_ref, bsb_ref, y_ref, vb_ref):
        v = _gelu(zp_ref[:, D:2 * D])
        mu = jnp.mean(v, axis=-1, keepdims=True)
        xc = v - mu
        rstd = lax.rsqrt(jnp.mean(xc * xc, axis=-1, keepdims=True) + LN_EPS)
        vb_ref[...] = (xc * rstd * lng_ref[...] + lnb_ref[...]).astype(BF16)
        for c in range(ncs):
            rows = slice(c * CHUNK, (c + 1) * CHUNK)
            for g in range(G):
                cols = slice(g * gd, (g + 1) * gd)
                sv = _dot(ws_ref[g], vb_ref[rows, cols]) + bsb_ref[:, cols]
                y_ref[rows, cols] = (_gelu(zp_ref[rows, cols]) * sv).astype(BF16)

    def full(a):
        return pl.BlockSpec(a.shape, lambda i: (0,) * a.ndim)

    return pl.pallas_call(
        body, grid=(S // ts,),
        in_specs=[pl.BlockSpec((ts, 2 * D), lambda i: (i, 0)), full(lng), full(lnb), full(ws), full(bsb)],
        out_specs=pl.BlockSpec((ts, D), lambda i: (i, 0)), out_shape=jax.ShapeDtypeStruct((S, D), BF16),
        scratch_shapes=[pltpu.VMEM((ts, D), BF16)],
        compiler_params=_cp("parallel"), name=name)(zp, lng, lnb, ws, bsb)


def _sgu_bwd(dy, zp, lng, lnb, ws, wst, bsb, name):
    S = zp.shape[0]
    D = zp.shape[1] // 2
    G = ws.shape[0]
    gd = D // G
    ts = _pick(S, TS_SGU, CHUNK)
    ncs = ts // CHUNK

    def body(dy_ref, zp_ref, lng_ref, lnb_ref, ws_ref, wst_ref, bsb_ref,
             dzp_ref, dws_ref, dbs_ref, dg_ref, db_ref, vb_ref, dvln_ref, acc_bs):
        i = pl.program_id(0)

        @pl.when(i == 0)
        def _():
            dws_ref[...] = jnp.zeros_like(dws_ref)
            acc_bs[...] = jnp.zeros_like(acc_bs)
            dg_ref[...] = jnp.zeros_like(dg_ref)
            db_ref[...] = jnp.zeros_like(db_ref)

        v, dv_dz = _gelu_and_grad(zp_ref[:, D:2 * D])
        mu = jnp.mean(v, axis=-1, keepdims=True)
        xc = v - mu
        rstd = lax.rsqrt(jnp.mean(xc * xc, axis=-1, keepdims=True) + LN_EPS)
        xh = xc * rstd
        vb_ref[...] = (xh * lng_ref[...] + lnb_ref[...]).astype(BF16)
        for c in range(ncs):
            rows = slice(c * CHUNK, (c + 1) * CHUNK)
            for g in range(G):
                cols = slice(g * gd, (g + 1) * gd)
                u, du_dz = _gelu_and_grad(zp_ref[rows, cols])
                dy_ = dy_ref[rows, cols].astype(F32)
                sv = _dot(ws_ref[g], vb_ref[rows, cols]) + bsb_ref[:, cols]
                dzp_ref[rows, cols] = (dy_ * sv * du_dz).astype(BF16)
                dsv = dy_ * u
                acc_bs[:, cols] += dsv
                dsvb = dsv.astype(BF16)
                dws_ref[g] += _dot_nt(dsvb, vb_ref[rows, cols])
                dvln_ref[rows, cols] = _dot(wst_ref[g], dsvb)
        dvln = dvln_ref[...]
        dg_ref[...] += jnp.sum(dvln * xh, axis=0, keepdims=True)
        db_ref[...] += jnp.sum(dvln, axis=0, keepdims=True)
        dxh = dvln * lng_ref[...]
        dv = rstd * (dxh - jnp.mean(dxh, axis=-1, keepdims=True) - xh * jnp.mean(dxh * xh, axis=-1, keepdims=True))
        dzp_ref[:, D:2 * D] = (dv * dv_dz).astype(BF16)

        @pl.when(i == pl.num_programs(0) - 1)
        def _():
            dbs_ref[...] = acc_bs[...]

    def full(shape):
        return pl.BlockSpec(shape, lambda i: (0,) * len(shape))

    return pl.pallas_call(
        body, grid=(S // ts,),
        in_specs=[pl.BlockSpec((ts, D), lambda i: (i, 0)), pl.BlockSpec((ts, 2 * D), lambda i: (i, 0)),
                  full(lng.shape), full(lnb.shape), full(ws.shape), full(wst.shape), full(bsb.shape)],
        out_specs=[pl.BlockSpec((ts, 2 * D), lambda i: (i, 0)), full(ws.shape), full(bsb.shape),
                   full((1, D)), full((1, D))],
        out_shape=[jax.ShapeDtypeStruct((S, 2 * D), BF16), jax.ShapeDtypeStruct(ws.shape, F32),
                   jax.ShapeDtypeStruct(bsb.shape, F32), jax.ShapeDtypeStruct((1, D), F32),
                   jax.ShapeDtypeStruct((1, D), F32)],
        scratch_shapes=[pltpu.VMEM((ts, D), BF16), pltpu.VMEM((ts, D), F32),
                        pltpu.VMEM(bsb.shape, F32)],
        compiler_params=_cp("arbitrary"), name=name)(dy, zp, lng, lnb, ws, wst, bsb)


def _group_sum(x, groups, name):
    P, D = x.shape
    gd = D // groups

    def body(x_ref, o_ref):
        for g in range(groups):
            o_ref[:, g:g + 1] = jnp.sum(x_ref[:, g * gd:(g + 1) * gd], axis=1, keepdims=True)

    return pl.pallas_call(body, out_shape=jax.ShapeDtypeStruct((P, groups), F32), name=name)(x)


def _adamw_small(ws, gs, ms, vs, name):
    n = len(ws)
    shapes = [w.shape for w in ws]
    flat = [(w.size // w.shape[-1], w.shape[-1]) for w in ws]

    def body(*refs):
        for k in range(n):
            w_ref, g_ref, m_ref, v_ref = (refs[j * n + k] for j in range(4))
            d_ref, nm_ref, nv_ref = (refs[(4 + j) * n + k] for j in range(3))
            d_ref[...], nm_ref[...], nv_ref[...] = _adamw_math(w_ref[...], g_ref[...], m_ref[...], v_ref[...])

    outs = pl.pallas_call(
        body, out_shape=[jax.ShapeDtypeStruct(f, F32) for f in flat] * 3, name=name)(
            *(a.reshape(f) for group in (ws, gs, ms, vs) for a, f in zip(group, flat)))
    return [tuple(outs[j * n + k].reshape(shapes[k]) for j in range(3)) for k in range(n)]


_HBM = pl.BlockSpec(memory_space=pltpu.HBM)


def _remote(src, dst, send_sem, recv_sem, to):
    return pltpu.make_async_remote_copy(src_ref=src, dst_ref=dst, send_sem=send_sem, recv_sem=recv_sem,
                                        device_id=to, device_id_type=MESH)


def _all_gather(block, name):
    R, C = block.shape

    def body(x_ref, out_ref, send_sems, recv_sems, local_sem):
        x, y, c = lax.axis_index("x"), lax.axis_index("y"), lax.axis_index("c")
        me, sibling = (x, y, c), (x, y, 1 - c)
        chips = [(1 - x, y), (x, 1 - y), (1 - x, 1 - y)]

        def slot(px, py, pc):
            return out_ref.at[4 * px + 2 * py + pc]

        def copy(k, blk, to, src=None):
            return _remote(slot(*blk) if src is None else src, slot(*blk), send_sems.at[k], recv_sems.at[k], to)

        mine = pltpu.make_async_copy(x_ref, slot(*me), local_sem)
        mine.start()
        first = [copy(0, me, sibling, src=x_ref)]
        first += [copy(1 + j, me, (*chip, c), src=x_ref) for j, chip in enumerate(chips)]
        for cp in first:
            cp.start()
        passed = [copy(4 + j, (*chip, c), sibling) for j, chip in enumerate(chips)]
        for j, chip in enumerate(chips):
            copy(1 + j, (*chip, c), me).wait_recv()
            passed[j].start()
        copy(0, sibling, me).wait_recv()
        for j, chip in enumerate(chips):
            copy(4 + j, (*chip, 1 - c), me).wait_recv()
        for cp in first + passed:
            cp.wait_send()
        mine.wait()

    return pl.pallas_call(
        body, out_shape=jax.ShapeDtypeStruct((NDEV, R, C), block.dtype), in_specs=[_HBM], out_specs=_HBM,
        scratch_shapes=[pltpu.SemaphoreType.DMA((7,)), pltpu.SemaphoreType.DMA((7,)), pltpu.SemaphoreType.DMA],
        name=name)(block)


def _all_gather_weights(pack, rows, name, after=None):
    C = pack.shape[1]
    nw = len(rows)
    starts = [sum(rows[:w]) for w in range(nw)]
    after_specs, after_ops = _after(after)

    def body(pack_ref, *rest):
        rest = rest[len(after_ops):]
        outs = rest[:nw]
        send_sems, recv_sems, local_sem = rest[nw:]
        x, y, c = lax.axis_index("x"), lax.axis_index("y"), lax.axis_index("c")
        me, sibling = (x, y, c), (x, y, 1 - c)
        chips = [(1 - x, y), (x, 1 - y), (1 - x, 1 - y)]

        def block(w, px, py, pc):
            return outs[w].at[pl.ds((4 * px + 2 * py + pc) * rows[w], rows[w])]

        def mine(w):
            return pack_ref.at[pl.ds(starts[w], rows[w])]

        def all_of(k):
            return _remote(pack_ref, pack_ref, send_sems.at[k], recv_sems.at[k], me)

        for w in range(nw):
            pltpu.make_async_copy(mine(w), block(w, *me), local_sem).start()
        for k, to in enumerate([sibling] + [(*chip, c) for chip in chips]):
            for w in range(nw):
                _remote(mine(w), block(w, *me), send_sems.at[k], recv_sems.at[k], to).start()
        for j, chip in enumerate(chips):
            all_of(1 + j).wait_recv()
            for w in range(nw):
                _remote(block(w, *chip, c), block(w, *chip, c), send_sems.at[4 + j], recv_sems.at[4 + j], sibling).start()
        all_of(0).wait_recv()
        for j in range(3):
            all_of(4 + j).wait_recv()
        for k in range(7):
            all_of(k).wait_send()
        pltpu.make_async_copy(pack_ref, pack_ref, local_sem).wait()

    return pl.pallas_call(
        body, out_shape=[jax.ShapeDtypeStruct((NDEV * r, C), pack.dtype) for r in rows],
        in_specs=[_HBM] + after_specs, out_specs=[_HBM] * nw,
        scratch_shapes=[pltpu.SemaphoreType.DMA((7,)), pltpu.SemaphoreType.DMA((7,)), pltpu.SemaphoreType.DMA],
        name=name)(pack, *after_ops)


_SEM = pl.BlockSpec(memory_space=pltpu.SEMAPHORE)
_DATAFLOW = pltpu.SideEffectType.DATAFLOW_SIDE_EFFECTING


def _split_start(srcs, lands, plan, n, after, name):
    nbuf = len(srcs) + len(lands)
    after_specs, after_ops = _after(after)

    def body(*refs):
        src_refs, land_refs = refs[:len(srcs)], refs[len(srcs):nbuf]
        send_sems, recv_sems = refs[nbuf + len(after_ops)], refs[nbuf + len(after_ops) + 1]
        for k, (src, dst, to) in enumerate(plan(src_refs, land_refs)):
            _remote(src, dst, send_sems.at[k], recv_sems.at[k], to).start()
        refs[-1][...] = jnp.zeros_like(refs[-1])

    bufs = [pltpu.with_memory_space_constraint(a, pltpu.HBM) for a in list(srcs) + list(lands)]
    outs = pl.pallas_call(
        body, name=name,
        out_shape=(pltpu.SemaphoreType.DMA((n,)), pltpu.SemaphoreType.DMA((n,)),
                   *[pltpu.HBM(a.shape, a.dtype) for a in bufs], jax.ShapeDtypeStruct((8, 128), F32)),
        in_specs=[_HBM] * nbuf + after_specs,
        out_specs=(_SEM, _SEM, *[_HBM] * nbuf, pl.BlockSpec(memory_space=pltpu.VMEM)),
        input_output_aliases={i: 2 + i for i in range(nbuf)},
        compiler_params=pltpu.CompilerParams(has_side_effects=_DATAFLOW))(*bufs, *after_ops)
    return outs[0], outs[1], list(outs[2:2 + len(srcs)]), list(outs[2 + len(srcs):2 + nbuf]), outs[-1]


def _split_wait(send_sems, recv_sems, srcs, lands, plan, after, name):
    nbuf = len(srcs) + len(lands)
    after_specs, after_ops = _after(after)

    def body(*refs):
        src_refs, land_refs = refs[:len(srcs)], refs[len(srcs):nbuf]
        send_sems_ref, recv_sems_ref = refs[nbuf], refs[nbuf + 1]
        for k, (src, dst, to) in enumerate(plan(src_refs, land_refs)):
            copy = _remote(src, dst, send_sems_ref.at[k], recv_sems_ref.at[k], to)
            copy.wait_send()
            copy.wait_recv()

    outs = pl.pallas_call(
        body, name=name, out_shape=tuple(pltpu.HBM(a.shape, a.dtype) for a in list(srcs) + list(lands)),
        in_specs=[_HBM] * nbuf + [_SEM, _SEM] + after_specs, out_specs=tuple([_HBM] * nbuf),
        input_output_aliases={i: i for i in range(nbuf)},
        compiler_params=pltpu.CompilerParams(has_side_effects=_DATAFLOW))(*srcs, *lands, send_sems, recv_sems, *after_ops)
    return list(outs[:len(srcs)]), list(outs[len(srcs):])


def _peers(x, y, c):
    return [(mask, (1 - x if mask & 4 else x, 1 - y if mask & 2 else y, 1 - c if mask & 1 else c))
            for mask in range(1, NDEV)]


def _gather_plan(rows):
    starts = [sum(rows[:w]) for w in range(len(rows))]

    def plan(src_refs, land_refs):
        x, y, c = lax.axis_index("x"), lax.axis_index("y"), lax.axis_index("c")
        copies = []
        for w, r in enumerate(rows):
            mine = src_refs[0].at[pl.ds(starts[w], r)]
            dst = land_refs[w].at[pl.ds((4 * x + 2 * y + c) * r, r)]
            copies += [(mine, dst, peer) for _, peer in _peers(x, y, c)]
        return copies

    return plan, (NDEV - 1) * len(rows)


def _place_own(shards, fulls, dev_idx, name):
    nw = len(shards)

    def body(i_ref, *refs):
        for w in range(nw):
            refs[2 * nw + w][...] = refs[w][...]

    grid_spec = pltpu.PrefetchScalarGridSpec(
        num_scalar_prefetch=1, grid=(1,),
        in_specs=[pl.BlockSpec(s.shape, lambda t, i_ref: (0, 0)) for s in shards] + [_ANY] * nw,
        out_specs=[pl.BlockSpec(s.shape, lambda t, i_ref: (i_ref[0], 0)) for s in shards])
    outs = pl.pallas_call(
        body, grid_spec=grid_spec, out_shape=[jax.ShapeDtypeStruct(f.shape, f.dtype) for f in fulls],
        input_output_aliases={1 + nw + w: w for w in range(nw)}, name=name)(dev_idx, *shards, *fulls)
    return list(outs)


def _scatter_plan(rows):
    def plan(src_refs, land_refs):
        x, y, c = lax.axis_index("x"), lax.axis_index("y"), lax.axis_index("c")
        copies = []
        for w, r in enumerate(rows):
            for mask, (px, py, pc) in _peers(x, y, c):
                src = src_refs[w].at[pl.ds((4 * px + 2 * py + pc) * r, r)]
                copies.append((src, land_refs[w].at[mask - 1], (px, py, pc)))
        return copies

    return plan, (NDEV - 1) * len(rows)


def _adamw_math(w, g, m, v):
    nm = ADAM_B1 * m + (1.0 - ADAM_B1) * g
    nv = ADAM_B2 * v + (1.0 - ADAM_B2) * (g * g)
    bc1 = 1.0 - ADAM_B1 ** ADAM_STEP
    bc2 = 1.0 - ADAM_B2 ** ADAM_STEP
    return -ADAM_LR * ((nm / bc1) / (jnp.sqrt(nv / bc2) + ADAM_EPS) + ADAM_WD * w), nm, nv


def _finish_weight(gs, gots, dev_idx, w, m, v, name, after=None):
    L = len(gs)
    n1, r, C = gots[0].shape
    block = (None,) + w.shape[1:]

    after_specs, after_ops = _after(after)

    def body(i_ref, *refs):
        ins, (w_ref, m_ref, v_ref), (g_out, d_out, m_out, v_out) = refs[:2 * L], refs[2 * L:2 * L + 3], refs[-4:]
        for layer in range(L):
            @pl.when(pl.program_id(0) == layer)
            def _():
                g_ref, got_ref = ins[2 * layer], ins[2 * layer + 1]
                acc = g_ref[...].astype(F32)
                for k in range(n1):
                    acc = acc + got_ref[k].astype(F32)
                g_out[...] = acc
                d_out[...], m_out[...], v_out[...] = _adamw_math(w_ref[...], acc, m_ref[...], v_ref[...])

    in_specs, ins = [], []
    for g, got in zip(gs, gots):
        in_specs += [pl.BlockSpec((r, C), lambda t, i_ref: (i_ref[0], 0), pipeline_mode=_RESIDENT),
                     pl.BlockSpec((n1, r, C), lambda t, i_ref: (0, 0, 0), pipeline_mode=_RESIDENT)]
        ins += [g, got]
    per_layer = pl.BlockSpec(block, lambda t, i_ref: (t, 0, 0))
    grid_spec = pltpu.PrefetchScalarGridSpec(
        num_scalar_prefetch=1, grid=(L,), in_specs=in_specs + [per_layer] * 3 + after_specs,
        out_specs=[per_layer] * 4)
    return pl.pallas_call(
        body, grid_spec=grid_spec, out_shape=[jax.ShapeDtypeStruct(w.shape, F32)] * 4,
        compiler_params=_cp("arbitrary"), name=name)(dev_idx, *ins, w, m, v, *after_ops)


def _sum_slots(a, name):
    n, R, C = a.shape

    def body(a_ref, o_ref):
        acc = a_ref[0]
        for k in range(1, n):
            acc = acc + a_ref[k]
        o_ref[...] = acc

    return pl.pallas_call(body, out_shape=jax.ShapeDtypeStruct((R, C), F32), name=name)(a)


def _shard_axis(name):
    return {"ev_w_in": 2, "ev_a_conv_w": 2, "ev_b_conv_w": 2, "ev_w_out": 1, "od_w_in": 2, "od_c_ln_g": 1,
            "od_c_ln_b": 1, "od_w_out": 1, "xa_w_q": 1, "xa_w_k": 1, "xa_w_v": 1, "xa_w_o": 1,
            "ffn_w_gate": 2, "ffn_w_up": 2, "ffn_w_down": 1}[name]


BIG = ["ev_w_in", "ev_w_out", "od_w_in", "od_w_out", "xa_w_q", "xa_w_k", "xa_w_v", "xa_w_o",
       "ffn_w_gate", "ffn_w_up", "ffn_w_down"]
SMALL_SHARDED = ["ev_a_conv_w", "ev_b_conv_w", "od_c_ln_g", "od_c_ln_b"]
REPLICATED = ["g_mix", "g_xattn", "g_mem", "g_ffn", "g_final", "ev_a_conv_b", "ev_a_ln_g", "ev_a_ln_b",
              "ev_b_conv_b", "od_w_s", "od_b_s"]
WEIGHTS = ["g_mix", "g_xattn", "g_mem", "g_ffn", "g_final", "ev_w_in", "ev_a_conv_w", "ev_a_conv_b", "ev_a_ln_g",
           "ev_a_ln_b", "ev_b_conv_w", "ev_b_conv_b", "ev_w_out", "od_w_in", "od_c_ln_g", "od_c_ln_b", "od_w_s",
           "od_b_s", "od_w_out", "xa_w_q", "xa_w_k", "xa_w_v", "xa_w_o", "ffn_w_gate", "ffn_w_up", "ffn_w_down"]


def _full_from_blocks(blocks, axis):
    shard = blocks.shape[1:]
    full = jnp.moveaxis(blocks, 0, axis)
    return full.reshape(shard[:axis] + (NDEV * shard[axis],) + shard[axis + 1:])


def _blocks_from_full(full, axis):
    shp = full.shape
    split = full.reshape(shp[:axis] + (NDEV, shp[axis] // NDEV) + shp[axis + 1:])
    return jnp.moveaxis(split, axis, 0)


def _pad_rows(flat, width, row_align):
    per = width * row_align
    n = -(-flat.shape[0] // per) * per
    return jnp.pad(flat, (0, n - flat.shape[0])).reshape(n // width, width)


def _row(v):
    return v.reshape(1, -1)


def _mem_kv(mem, g_m, wk, wv, name):
    M, D = mem.shape

    def body(mem_ref, g_ref, wk_ref, wv_ref, n_ref, k_ref, v_ref):
        x = mem_ref[...]
        r = lax.rsqrt(jnp.mean(x * x, axis=-1, keepdims=True) + RMS_EPS)
        n = ((x * r) * g_ref[...]).astype(BF16)
        n_ref[...] = n
        k_ref[...] = _dot(n, wk_ref[...]).astype(BF16)
        v_ref[...] = _dot(n, wv_ref[...]).astype(BF16)

    sh = jax.ShapeDtypeStruct((M, D), BF16)
    return pl.pallas_call(body, out_shape=[sh, sh, sh], name=name)(mem, g_m, wk, wv)


def _mem_kv_bwd(dk, dv, mem, mem_n, g_m, wk, wv, name):
    M, D = mem.shape

    def body(dk_ref, dv_ref, mem_ref, n_ref, g_ref, wk_ref, wv_ref, dwk_ref, dwv_ref, dg_ref):
        dk_, dv_ = dk_ref[...].astype(BF16), dv_ref[...].astype(BF16)
        n = n_ref[...]
        dwk_ref[...] = _dot_tn(n, dk_).astype(BF16)
        dwv_ref[...] = _dot_tn(n, dv_).astype(BF16)
        dn = _dot_nt(dk_, wk_ref[...]) + _dot_nt(dv_, wv_ref[...])
        x = mem_ref[...]
        r = lax.rsqrt(jnp.mean(x * x, axis=-1, keepdims=True) + RMS_EPS)
        dg_ref[...] = jnp.sum(dn * (x * r), axis=0, keepdims=True)

    wsh = jax.ShapeDtypeStruct((D, D), BF16)
    return pl.pallas_call(body, out_shape=[wsh, wsh, jax.ShapeDtypeStruct((1, D), F32)], name=name)(
        dk, dv, mem, mem_n, g_m, wk, wv)


def _xattn_fwd(h, nq, mem, g_m, wq, wk, wv, wo, g_next, tag, after):
    q = _mm([(nq, wq, "nn")], f"xa_q_{tag}", out_dtype=BF16, after=after)
    mem_n, k, v = _mem_kv(mem, _row(g_m), wk, wv, f"xa_mem_{tag}")
    o = _attn_fwd(q, k, v, f"xa_attn_{tag}")
    h_new, n_next = _mm([(o, wo, "nn")], f"xa_o_{tag}", res=h, rms_g=_row(g_next))
    return h_new, n_next, (h, nq, mem_n, q, k, v, o)


def _xattn_bwd(dh_new, saved, mem, g_x, g_m, wq, wk, wv, wo, tag, push):
    h, nq, mem_n, q, k, v, o = saved
    do = _mm([(dh_new, wo, "nt")], f"xa_do_{tag}", out_dtype=BF16)
    d_wo = _mm_tn(o, dh_new, f"xa_dwo_{tag}")
    dq, dk, dv = _attn_bwd(q, k, v, do, f"xa_attn_bwd_{tag}")
    d_wq = _mm_tn(nq, dq, f"xa_dwq_{tag}")
    d_wk, d_wv, d_gm = _mem_kv_bwd(dk, dv, mem, mem_n, _row(g_m), wk, wv, f"xa_mem_bwd_{tag}")
    token = push([d_wq, d_wk, d_wv, d_wo])
    dh, d_gx = _mm([(dq, wq, "nt")], f"xa_dnq_{tag}", rms_bwd=(h, _row(g_x), dh_new), tm=1024, after=token)
    return dh, dict(g_xattn=d_gx, g_mem=d_gm)


def _ffn_fwd(h, n, wgt, wut, wd, g_next, tag, after, loss=None):
    a, b, hid = _ffn_up(n, wgt, wut, f"ffn_up_{tag}", after=after)
    saved = (h, n, a, b, hid)
    if loss is not None:
        return _mm([(hid, wd, "nn")], f"ffn_down_{tag}", res=h, loss=loss, tm=1024), saved
    h_new, n_next = _mm([(hid, wd, "nn")], f"ffn_down_{tag}", res=h, rms_g=_row(g_next), tm=1024)
    return h_new, n_next, saved


def _ffn_bwd(dh_new, saved, g_f, wgt, wut, wd, tag, push):
    h, n, a, b, hid = saved
    da, db = _ffn_dhid(dh_new, wd, a, b, f"ffn_dhid_{tag}")
    d_wd = _mm_tn(hid, dh_new, f"ffn_dwd_{tag}", ts=1024, tn=1024)
    d_wgt = _mm_tn(da, n, f"ffn_dwg_{tag}", ts=1024, tn=1024)
    d_wut = _mm_tn(db, n, f"ffn_dwu_{tag}", ts=1024, tn=1024)
    token = push([d_wgt, d_wut, d_wd])
    dh, d_gf = _mm([(da, wgt, "nn"), (db, wut, "nn")], f"ffn_dn_{tag}", rms_bwd=(h, _row(g_f), dh_new), tm=512,
                   after=token)
    return dh, dict(g_ffn=d_gf)


_XA = ["xa_w_q", "xa_w_k", "xa_w_v", "xa_w_o"]
_FFN = ["ffn_w_gate", "ffn_w_up", "ffn_w_down"]
GATHERS = {
    "ev_in": [("ev_w_in", 0)],
    "xa0": [("ev_w_out", 0)] + [(n, 0) for n in _XA],
    "ffn0": [(n, 0) for n in _FFN],
    "od": [("od_w_in", 0), ("od_w_out", 0)],
    "xa1": [(n, 1) for n in _XA],
    "ffn1": [(n, 1) for n in _FFN],
}
SCATTERS = {
    "ffn1": [(n, 1) for n in _FFN],
    "xa1": [(n, 1) for n in _XA],
    "od": [("od_w_in", 0), ("od_w_out", 0)],
    "ffn0": [(n, 0) for n in _FFN],
    "xa0": [(n, 0) for n in _XA],
    "ev_out": [("ev_w_out", 0)],
    "ev_in": [("ev_w_in", 0)],
}


def _local_step(x, mem, loss_target, W, comm):
    grads = {}

    h0 = x
    (ev_w_in_t,), token = comm.weights("ev_in", None)
    n0 = _rms_fwd(h0, _row(W["g_mix"][0]), "ev_rms", after=token)
    z = _mm([(n0, ev_w_in_t, "nt")], "ev_in", tn=1280)
    token = comm.prefetch(["ffn0"], z)
    ab, ca = _conv_fwd(z, W["ev_a_conv_w"][0], W["ev_a_conv_b"], W["ev_a_ln_g"], W["ev_a_ln_b"],
                       W["ev_b_conv_w"][0], W["ev_b_conv_b"], "ev_conv", after=token)
    (ev_w_out, *xa_w0), _ = comm.weights("xa0", ab)
    h1, nq0 = _mm([(ab, ev_w_out, "nn")], "ev_out", res=h0, rms_g=_row(W["g_xattn"][0]))
    token = comm.prefetch(["od", "xa1"], nq0)
    h2, nf0, xa0 = _xattn_fwd(h1, nq0, mem, W["g_mem"][0], *xa_w0, W["g_ffn"][0], "l0", token)
    ffn_w0, _ = comm.weights("ffn0", nf0)
    token = comm.prefetch(["ffn1"], nf0)
    h3, n3, ff0 = _ffn_fwd(h2, nf0, *ffn_w0, W["g_mix"][1], "l0", token)

    (od_w_in_t, od_w_out), _ = comm.weights("od", n3)
    zp = _mm([(n3, od_w_in_t, "nt")], "od_in", tn=1024)
    D = x.shape[1]
    ws = W["od_w_s"][0].astype(BF16)
    wst = jnp.swapaxes(ws, 1, 2)
    bsb = jnp.repeat(jnp.transpose(W["od_b_s"][0]), D // C_GROUPS, axis=1)
    y_sgu = _sgu_fwd(zp, W["od_c_ln_g"], W["od_c_ln_b"], ws, bsb, "od_sgu")
    h4, nq1 = _mm([(y_sgu, od_w_out, "nn")], "od_out", res=h3, rms_g=_row(W["g_xattn"][1]))
    xa_w1, _ = comm.weights("xa1", nq1)
    h5, nf1, xa1 = _xattn_fwd(h4, nq1, mem, W["g_mem"][1], *xa_w1, W["g_ffn"][1], "l1", None)
    ffn_w1, _ = comm.weights("ffn1", nf1)
    (loss_row, dh6, d_gfinal), ff1 = _ffn_fwd(h5, nf1, *ffn_w1, None, "l1", None,
                                              loss=(_row(W["g_final"]), loss_target))
    grads["g_final"] = d_gfinal.reshape(-1)


    dh5, g_ff1 = _ffn_bwd(dh6, ff1, W["g_ffn"][1], *ffn_w1, "l1", lambda dws: comm.grads("ffn1", dws))
    dh4, g_xa1 = _xattn_bwd(dh5, xa1, mem, W["g_xattn"][1], W["g_mem"][1], *xa_w1, "l1",
                            lambda dws: comm.grads("xa1", dws))
    dy_sgu = _mm([(dh4, od_w_out, "nt")], "od_dy", tn=1024)
    d_od_out = _mm_tn(y_sgu, dh4, "od_dwout", tn=1024)
    dzp, d_ws, d_bsb, d_clng, d_clnb = _sgu_bwd(dy_sgu, zp, W["od_c_ln_g"], W["od_c_ln_b"], ws, wst, bsb, "od_sgu_bwd")
    grads["od_w_s"] = d_ws[None]
    grads["od_b_s"] = jnp.transpose(_group_sum(d_bsb, C_GROUPS, "od_dbs"))[None]
    grads["od_c_ln_g"], grads["od_c_ln_b"] = d_clng, d_clnb
    token = comm.grads("od", [_mm_tn(dzp, n3, "od_dwin", ts=1024, tn=1024), d_od_out])
    dh3, d_gmix1 = _mm([(dzp, od_w_in_t, "nn")], "od_dn", rms_bwd=(h3, _row(W["g_mix"][1]), dh4), tm=1024, after=token)

    dh2, g_ff0 = _ffn_bwd(dh3, ff0, W["g_ffn"][0], *ffn_w0, "l0", lambda dws: comm.grads("ffn0", dws))
    dh1, g_xa0 = _xattn_bwd(dh2, xa0, mem, W["g_xattn"][0], W["g_mem"][0], *xa_w0, "l0",
                            lambda dws: comm.grads("xa0", dws))
    token = comm.grads("ev_out", [_mm_tn(ab, dh1, "ev_dwout", tn=1024)])
    dab = _mm([(dh1, ev_w_out, "nt")], "ev_dab", tn=1024, after=token)
    dca, d_lng, d_lnb, d_ba = _conv_bwd_ln(dab, ca, W["ev_a_ln_g"], W["ev_a_ln_b"], "ev_conv_bwd_ln")
    dz, d_wa, d_wb, d_bb = _conv_bwd(z, dca, dab, W["ev_a_conv_w"][0], W["ev_b_conv_w"][0], W["ev_b_conv_b"],
                                     "ev_conv_bwd")
    grads.update(ev_a_ln_g=d_lng, ev_a_ln_b=d_lnb, ev_a_conv_b=d_ba, ev_b_conv_b=d_bb,
                 ev_a_conv_w=d_wa[None], ev_b_conv_w=d_wb[None])
    token = comm.grads("ev_in", [_mm_tn(dz, n0, "ev_dwin", ts=1024, tn=1024)])
    grad_x, d_gmix0 = _mm([(dz, ev_w_in_t, "nn")], "ev_dn", rms_bwd=(h0, _row(W["g_mix"][0]), dh1), tm=1024, after=token)

    grads["g_mix"] = jnp.concatenate([d_gmix0, d_gmix1], axis=0)
    for key in ("g_xattn", "g_mem"):
        grads[key] = jnp.concatenate([g_xa0[key], g_xa1[key]], axis=0)
    grads["g_ffn"] = jnp.concatenate([g_ff0["g_ffn"], g_ff1["g_ffn"]], axis=0)
    return loss_row, grad_x, grads


class _Exchanges:
    def __init__(self, shards, dev_idx, after):
        self.shards, self.dev_idx = shards, dev_idx
        self.gathering, self.scattering = {}, {}
        self.first = _all_gather_weights(self._pack(GATHERS["ev_in"]), self._rows(GATHERS["ev_in"]), "ag_ev_in",
                                         after=after)
        self.first_token = self.prefetch(["xa0"], self.first[0])

    def _rows(self, entries):
        return [self.shards[e].shape[0] for e in entries]

    def _pack(self, entries):
        return jnp.concatenate([self.shards[e] for e in entries], axis=0)

    def prefetch(self, gathers, after):
        for name in gathers:
            rows = self._rows(GATHERS[name])
            pack = self._pack(GATHERS[name])
            lands = [lax.empty((NDEV * r, pack.shape[1]), pack.dtype) for r in rows]
            plan, n = _gather_plan(rows)
            send, recv, srcs, lands, after = _split_start([pack], lands, plan, n, after, f"ag_{name}_start")
            self.gathering[name] = (send, recv, srcs, lands, plan, rows)
        return after

    def weights(self, name, after):
        if name == "ev_in":
            return self.first, self.first_token
        send, recv, srcs, lands, plan, rows = self.gathering.pop(name)
        _, lands = _split_wait(send, recv, srcs, lands, plan, after, f"ag_{name}_wait")
        return _place_own([self.shards[e] for e in GATHERS[name]], lands, self.dev_idx, f"ag_{name}_own"), None

    def grads(self, name, dws):
        rows = self._rows(SCATTERS[name])
        lands = [lax.empty((NDEV - 1, r, d.shape[1]), d.dtype) for r, d in zip(rows, dws)]
        plan, n = _scatter_plan(rows)
        send, recv, srcs, lands, token = _split_start(dws, lands, plan, n, None, f"rs_{name}_start")
        self.scattering[name] = (send, recv, srcs, lands, plan)
        return token

    def received(self, after):
        out = {}
        for name, (send, recv, srcs, lands, plan) in self.scattering.items():
            srcs, lands = _split_wait(send, recv, srcs, lands, plan, after, f"rs_{name}_wait")
            for entry, g, got in zip(SCATTERS[name], srcs, lands):
                out[entry] = (g, got)
        return out


def kernel(x, mem, g_mix, g_xattn, g_mem, g_ffn, g_final, ev_w_in, ev_a_conv_w, ev_a_conv_b, ev_a_ln_g, ev_a_ln_b, ev_b_conv_w, ev_b_conv_b, ev_w_out, od_w_in, od_c_ln_g, od_c_ln_b, od_w_s, od_b_s, od_w_out, xa_w_q, xa_w_k, xa_w_v, xa_w_o, ffn_w_gate, ffn_w_up, ffn_w_down, loss_target, m_g_mix, m_g_xattn, m_g_mem, m_g_ffn, m_g_final, m_ev_w_in, m_ev_a_conv_w, m_ev_a_conv_b, m_ev_a_ln_g, m_ev_a_ln_b, m_ev_b_conv_w, m_ev_b_conv_b, m_ev_w_out, m_od_w_in, m_od_c_ln_g, m_od_c_ln_b, m_od_w_s, m_od_b_s, m_od_w_out, m_xa_w_q, m_xa_w_k, m_xa_w_v, m_xa_w_o, m_ffn_w_gate, m_ffn_w_up, m_ffn_w_down, v_g_mix, v_g_xattn, v_g_mem, v_g_ffn, v_g_final, v_ev_w_in, v_ev_a_conv_w, v_ev_a_conv_b, v_ev_a_ln_g, v_ev_a_ln_b, v_ev_b_conv_w, v_ev_b_conv_b, v_ev_w_out, v_od_w_in, v_od_c_ln_g, v_od_c_ln_b, v_od_w_s, v_od_b_s, v_od_w_out, v_xa_w_q, v_xa_w_k, v_xa_w_v, v_xa_w_o, v_ffn_w_gate, v_ffn_w_up, v_ffn_w_down):
    local = dict(g_mix=g_mix, g_xattn=g_xattn, g_mem=g_mem, g_ffn=g_ffn, g_final=g_final, ev_w_in=ev_w_in, ev_a_conv_w=ev_a_conv_w, ev_a_conv_b=ev_a_conv_b, ev_a_ln_g=ev_a_ln_g, ev_a_ln_b=ev_a_ln_b, ev_b_conv_w=ev_b_conv_w, ev_b_conv_b=ev_b_conv_b, ev_w_out=ev_w_out, od_w_in=od_w_in, od_c_ln_g=od_c_ln_g, od_c_ln_b=od_c_ln_b, od_w_s=od_w_s, od_b_s=od_b_s, od_w_out=od_w_out, xa_w_q=xa_w_q, xa_w_k=xa_w_k, xa_w_v=xa_w_v, xa_w_o=xa_w_o, ffn_w_gate=ffn_w_gate, ffn_w_up=ffn_w_up, ffn_w_down=ffn_w_down)
    mom = dict(g_mix=m_g_mix, g_xattn=m_g_xattn, g_mem=m_g_mem, g_ffn=m_g_ffn, g_final=m_g_final, ev_w_in=m_ev_w_in, ev_a_conv_w=m_ev_a_conv_w, ev_a_conv_b=m_ev_a_conv_b, ev_a_ln_g=m_ev_a_ln_g, ev_a_ln_b=m_ev_a_ln_b, ev_b_conv_w=m_ev_b_conv_w, ev_b_conv_b=m_ev_b_conv_b, ev_w_out=m_ev_w_out, od_w_in=m_od_w_in, od_c_ln_g=m_od_c_ln_g, od_c_ln_b=m_od_c_ln_b, od_w_s=m_od_w_s, od_b_s=m_od_b_s, od_w_out=m_od_w_out, xa_w_q=m_xa_w_q, xa_w_k=m_xa_w_k, xa_w_v=m_xa_w_v, xa_w_o=m_xa_w_o, ffn_w_gate=m_ffn_w_gate, ffn_w_up=m_ffn_w_up, ffn_w_down=m_ffn_w_down)
    vel = dict(g_mix=v_g_mix, g_xattn=v_g_xattn, g_mem=v_g_mem, g_ffn=v_g_ffn, g_final=v_g_final, ev_w_in=v_ev_w_in, ev_a_conv_w=v_ev_a_conv_w, ev_a_conv_b=v_ev_a_conv_b, ev_a_ln_g=v_ev_a_ln_g, ev_a_ln_b=v_ev_a_ln_b, ev_b_conv_w=v_ev_b_conv_w, ev_b_conv_b=v_ev_b_conv_b, ev_w_out=v_ev_w_out, od_w_in=v_od_w_in, od_c_ln_g=v_od_c_ln_g, od_c_ln_b=v_od_c_ln_b, od_w_s=v_od_w_s, od_b_s=v_od_b_s, od_w_out=v_od_w_out, xa_w_q=v_xa_w_q, xa_w_k=v_xa_w_k, xa_w_v=v_xa_w_v, xa_w_o=v_xa_w_o, ffn_w_gate=v_ffn_w_gate, ffn_w_up=v_ffn_w_up, ffn_w_down=v_ffn_w_down)
    D = x.shape[-1]
    dev = 4 * lax.axis_index("x") + 2 * lax.axis_index("y") + lax.axis_index("c")

    def comm_layout(n, a):
        return jnp.transpose(a) if _shard_axis(n) == 2 else a

    shards = {(n, i): comm_layout(n, local[n][i]).astype(BF16) for n in BIG for i in range(local[n].shape[0])}
    small_sizes = [local[n].size for n in SMALL_SHARDED]
    small_block = _pad_rows(jnp.concatenate([local[n].reshape(-1) for n in SMALL_SHARDED]), 128, 8)
    small_all = _all_gather(small_block, "ag_small")
    comm = _Exchanges(shards, jnp.reshape(dev, (1,)).astype(jnp.int32), small_all)
    small_all = small_all.reshape(NDEV, -1)

    W = {n: local[n] for n in REPLICATED}
    o0 = 0
    for n, sz in zip(SMALL_SHARDED, small_sizes):
        blocks = small_all[:, o0:o0 + sz].reshape((NDEV,) + local[n].shape)
        W[n] = _full_from_blocks(blocks, _shard_axis(n))
        o0 += sz

    loss_row, grad_x, grads = _local_step(x[0], mem[0], loss_target[0], W, comm)

    received = comm.received(grad_x)
    rest = REPLICATED + SMALL_SHARDED
    rest_full_shapes = [grads[n].shape for n in rest]
    g_rest = _pad_rows(jnp.concatenate([grads[n].astype(F32).reshape(-1) for n in rest]), D, 8)
    small_rows = g_rest.shape[0]
    small_plan, small_n = _gather_plan([small_rows])
    small_send, small_recv, small_srcs, small_lands, token = _split_start(
        [g_rest], [lax.empty((NDEV * small_rows, D), F32)], small_plan, small_n, received["ev_w_in", 0][1],
        "ag_small_grads_start")

    gsh, delta, new_m, new_v = {}, {}, {}, {}
    def stacked_layout(n, a):
        return jnp.swapaxes(a, 1, 2) if _shard_axis(n) == 2 else a

    for n in BIG:
        parts = [received[n, i] for i in range(local[n].shape[0])]
        outs = _finish_weight([p[0] for p in parts], [p[1] for p in parts], comm.dev_idx,
                              *(stacked_layout(n, a) for a in (local[n], mom[n], vel[n])), f"finish_{n}", after=token)
        gsh[n], delta[n], new_m[n], new_v[n] = (stacked_layout(n, o) for o in outs)

    _, small_lands = _split_wait(small_send, small_recv, small_srcs, small_lands, small_plan,
                                 [delta[n] for n in BIG], "ag_small_grads_wait")
    partials = _place_own([g_rest], small_lands, comm.dev_idx, "ag_small_grads_own")[0]
    g_rest = _sum_slots(partials.reshape(NDEV, small_rows, D), "sum_small_grads").reshape(-1)
    o0 = 0
    for n, shp in zip(rest, rest_full_shapes):
        sz = 1
        for s in shp:
            sz *= s
        full = g_rest[o0:o0 + sz].reshape(shp)
        o0 += sz
        if n in SMALL_SHARDED:
            full = lax.dynamic_index_in_dim(_blocks_from_full(full, _shard_axis(n)), dev, 0, keepdims=False)
        gsh[n] = full.reshape(local[n].shape)

    small = _adamw_small([local[n] for n in rest], [gsh[n] for n in rest], [mom[n] for n in rest],
                         [vel[n] for n in rest], "adamw_small")
    for n, (d, nm, nv) in zip(rest, small):
        delta[n], new_m[n], new_v[n] = d, nm, nv

    loss = lax.psum(loss_row[0, 0], ("x", "y", "c"))
    return (loss, grad_x[None], *[gsh[n] for n in WEIGHTS], *[delta[n] for n in WEIGHTS],
            *[new_m[n] for n in WEIGHTS], *[new_v[n] for n in WEIGHTS])
```

```python
import jax
import jax.numpy as jnp
from jax import lax
from jax.experimental import pallas as pl
from jax.experimental.pallas import tpu as pltpu

F32, BF16 = jnp.float32, jnp.bfloat16
NDEV = 8
RMS_EPS = 1e-6
LN_EPS = 1e-5
CHUNK = 128
C_GROUPS = 8
XA_HEADS = 4
ADAM_LR, ADAM_B1, ADAM_B2, ADAM_EPS, ADAM_WD, ADAM_STEP = 0.001, 0.9, 0.999, 1e-08, 0.01, 10
HALO = 16
ROW_CHUNK = 32
ROW_CHUNK_FWD = 64
V7X_VMEM_LIMIT = 56 * 1024 * 1024
MESH = pl.DeviceIdType.MESH

TS_ROW = 512
TS_MM = 2048
TN_MM = 1408
TS_FFN = 512
MM_ROW_CHUNK = 256
FFN_COL_CHUNK = 256
TS_CONV = 512
TS_SGU = 512
TS_ATTN = 2048


def _cp(*sem):
    return pltpu.CompilerParams(dimension_semantics=sem, vmem_limit_bytes=V7X_VMEM_LIMIT)


def _pick(n, pref, align):
    for t in range(min(n, pref), 0, -1):
        if n % t == 0 and (t % align == 0 or t == n):
            return t
    return n


def _sigmoid(x):
    return 0.5 * jnp.tanh(0.5 * x) + 0.5


def _dot(a, b):
    return jnp.dot(a, b, preferred_element_type=F32)


def _dot_nt(a, b):
    return lax.dot_general(a, b, (((1,), (1,)), ((), ())), preferred_element_type=F32)


def _dot_tn(a, b):
    return lax.dot_general(a, b, (((0,), (0,)), ((), ())), preferred_element_type=F32)


_ANY = pl.BlockSpec(memory_space=pl.ANY)
_RESIDENT = pl.Buffered(1)


def _after(after):
    if after is None:
        return [], []
    ops = list(after) if isinstance(after, (list, tuple)) else [after]
    return [_ANY] * len(ops), ops


def _rms_mm(x, g, wt, name, tn=None, after=None):
    S, D = x.shape
    N = wt.shape[0]
    tm = _pick(S, TS_MM // 2, 16)
    tn = _pick(N, tn or TN_MM, 128)
    rc = MM_ROW_CHUNK if tm % MM_ROW_CHUNK == 0 else tm
    after_specs, after_ops = _after(after)

    def body(x_ref, g_ref, w_ref, *rest):
        z_ref, n_ref = rest[-2:]
        for r0 in range(0, tm, rc):
            rows = pl.ds(r0, rc)
            xc = x_ref[rows, :]
            r = lax.rsqrt(jnp.mean(xc * xc, axis=-1, keepdims=True) + RMS_EPS)
            n = ((xc * r) * g_ref[...]).astype(BF16)
            n_ref[rows, :] = n
            z_ref[rows, :] = _dot_nt(n, w_ref[...])

    return pl.pallas_call(
        body, grid=(S // tm, N // tn),
        in_specs=[pl.BlockSpec((tm, D), lambda i, j: (i, 0)), pl.BlockSpec((1, D), lambda i, j: (0, 0)),
                  pl.BlockSpec((tn, D), lambda i, j: (j, 0))] + after_specs,
        out_specs=[pl.BlockSpec((tm, tn), lambda i, j: (i, j)), pl.BlockSpec((tm, D), lambda i, j: (i, 0))],
        out_shape=[jax.ShapeDtypeStruct((S, N), F32), jax.ShapeDtypeStruct((S, D), BF16)],
        compiler_params=_cp("parallel", "arbitrary"), name=name)(x, g, wt, *after_ops)


def _mm(pairs, name, out_dtype=F32, res=None, rms_g=None, rms_bwd=None, loss=None, tm=None, tn=None, after=None):
    M = pairs[0][0].shape[0]
    N = pairs[0][1].shape[1 if pairs[0][2] == "nn" else 0]
    whole_rows = rms_g is not None or rms_bwd is not None or loss is not None
    tm = _pick(M, tm or TS_MM, 16)
    tn = N if whole_rows else _pick(N, tn or TN_MM, 128)
    npair = len(pairs)
    modes = [p[2] for p in pairs]
    after_specs, after_ops = _after(after)

    rc = MM_ROW_CHUNK if whole_rows and tm % MM_ROW_CHUNK == 0 else tm

    def body(*refs):
        rest = refs[2 * npair + len(after_ops):]
        res_ref = None
        if res is not None:
            res_ref, rest = rest[0], rest[1:]
        if rms_bwd is not None:
            dg_ref = rest[4]

            @pl.when(pl.program_id(0) == 0)
            def _():
                dg_ref[...] = jnp.zeros_like(dg_ref)

        if loss is not None:
            g_ref, t_ref, loss_ref, dh_ref, dg_ref = rest

            @pl.when(pl.program_id(0) == 0)
            def _():
                dg_ref[...] = jnp.zeros_like(dg_ref)
                loss_ref[...] = jnp.zeros_like(loss_ref)

        for r0 in range(0, tm, rc):
            rows = pl.ds(r0, rc)
            acc = None
            for p in range(npair):
                a_ = refs[2 * p][rows, :].astype(BF16)
                d = _dot(a_, refs[2 * p + 1][...]) if modes[p] == "nn" else _dot_nt(a_, refs[2 * p + 1][...])
                acc = d if acc is None else acc + d
            if res_ref is not None:
                acc = acc + res_ref[rows, :]
            if rms_bwd is not None:
                h_ref, g_ref, dres_ref, dh_ref, _ = rest
                x = h_ref[rows, :]
                r = lax.rsqrt(jnp.mean(x * x, axis=-1, keepdims=True) + RMS_EPS)
                xr = x * r
                dg_ref[...] += jnp.sum(acc * xr, axis=0, keepdims=True)
                u = acc * g_ref[...]
                dh_ref[rows, :] = r * u - xr * (r * jnp.mean(u * xr, axis=-1, keepdims=True)) + dres_ref[rows, :]
            elif loss is not None:
                r = lax.rsqrt(jnp.mean(acc * acc, axis=-1, keepdims=True) + RMS_EPS)
                xr = acc * r
                gg = g_ref[...]
                e = xr * gg - t_ref[rows, :]
                chunk_loss = jnp.sum(jnp.sum(e * e, axis=0, keepdims=True), axis=1, keepdims=True) * (0.5 / N)
                loss_ref[...] += jnp.broadcast_to(chunk_loss, loss_ref.shape)
                dy = e * (1.0 / N)
                dg_ref[...] += jnp.sum(dy * xr, axis=0, keepdims=True)
                u = dy * gg
                dh_ref[rows, :] = r * u - xr * (r * jnp.mean(u * xr, axis=-1, keepdims=True))
            elif rms_g is not None:
                g_ref, o_ref, n_ref = rest
                o_ref[rows, :] = acc
                r = lax.rsqrt(jnp.mean(acc * acc, axis=-1, keepdims=True) + RMS_EPS)
                n_ref[rows, :] = ((acc * r) * g_ref[...]).astype(BF16)
            else:
                rest[0][rows, :] = acc.astype(rest[0].dtype)

    in_specs, ins = [], []
    for a, w, mode in pairs:
        K = a.shape[1]
        in_specs.append(pl.BlockSpec((tm, K), lambda i, j: (i, 0)))
        once = _RESIDENT if tn == N else None
        in_specs.append(pl.BlockSpec((K, tn), lambda i, j: (0, j), pipeline_mode=once) if mode == "nn"
                        else pl.BlockSpec((tn, K), lambda i, j: (j, 0), pipeline_mode=once))
        ins += [a, w]
    in_specs += after_specs
    ins += after_ops
    tile = pl.BlockSpec((tm, tn), lambda i, j: (i, j))
    vec = pl.BlockSpec((1, tn), lambda i, j: (0, j))
    if res is not None:
        in_specs.append(tile)
        ins.append(res)
    sem = ("parallel", "parallel")
    if rms_bwd is not None:
        in_specs += [tile, vec, tile]
        ins += list(rms_bwd)
        out_specs = [tile, vec]
        out_shape = [jax.ShapeDtypeStruct((M, N), F32), jax.ShapeDtypeStruct((1, N), F32)]
        sem = ("arbitrary", "arbitrary")
    elif loss is not None:
        in_specs += [vec, tile]
        ins += list(loss)
        out_specs = [pl.BlockSpec((1, 128), lambda i, j: (0, 0)), tile, vec]
        out_shape = [jax.ShapeDtypeStruct((1, 128), F32), jax.ShapeDtypeStruct((M, N), F32),
                     jax.ShapeDtypeStruct((1, N), F32)]
        sem = ("arbitrary", "arbitrary")
    elif rms_g is not None:
        in_specs.append(vec)
        ins.append(rms_g)
        out_specs = [tile, tile]
        out_shape = [jax.ShapeDtypeStruct((M, N), F32), jax.ShapeDtypeStruct((M, N), BF16)]
    else:
        out_specs = tile
        out_shape = jax.ShapeDtypeStruct((M, N), out_dtype)
    return pl.pallas_call(
        body, grid=(M // tm, N // tn), in_specs=in_specs, out_specs=out_specs, out_shape=out_shape,
        compiler_params=_cp(*sem), name=name)(*ins)


def _mm_tn(a, b, name, ts=None, tn=None):
    S, K = a.shape
    N = b.shape[1]
    ts = _pick(S, ts or TS_MM, 16)
    tn = _pick(N, tn or TN_MM, 128)
    nsteps = S // ts

    def body(a_ref, b_ref, o_ref, acc_ref):
        s = pl.program_id(1)

        @pl.when(s == 0)
        def _():
            acc_ref[...] = jnp.zeros_like(acc_ref)

        acc_ref[...] += _dot_tn(a_ref[...].astype(BF16), b_ref[...].astype(BF16))

        @pl.when(s == nsteps - 1)
        def _():
            o_ref[...] = acc_ref[...].astype(o_ref.dtype)

    return pl.pallas_call(
        body, grid=(N // tn, nsteps),
        in_specs=[pl.BlockSpec((ts, K), lambda j, s: (s, 0)), pl.BlockSpec((ts, tn), lambda j, s: (s, j))],
        out_specs=pl.BlockSpec((K, tn), lambda j, s: (0, j)), out_shape=jax.ShapeDtypeStruct((K, N), BF16),
        scratch_shapes=[pltpu.VMEM((K, tn), F32)],
        compiler_params=_cp("parallel", "arbitrary"), name=name)(a, b)


def _col_chunks(n):
    return [(c0, min(FFN_COL_CHUNK, n - c0)) for c0 in range(0, n, FFN_COL_CHUNK)]


def _ffn_up(n, wgt, wut, name, after=None):
    S, D = n.shape
    F = wgt.shape[0]
    tm = _pick(S, TS_FFN, 16)
    after_specs, after_ops = _after(after)

    def body(n_ref, wg_ref, wu_ref, *rest):
        a_ref, b_ref, hid_ref = rest[-3:]
        x = n_ref[...]
        for c0, ce in _col_chunks(F):
            a = _dot_nt(x, wg_ref[c0:c0 + ce, :])
            b = _dot_nt(x, wu_ref[c0:c0 + ce, :])
            a_ref[:, c0:c0 + ce] = a.astype(BF16)
            b_ref[:, c0:c0 + ce] = b.astype(BF16)
            hid_ref[:, c0:c0 + ce] = (a * _sigmoid(a) * b).astype(BF16)

    wspec = pl.BlockSpec((F, D), lambda i: (0, 0), pipeline_mode=_RESIDENT)
    ospec = pl.BlockSpec((tm, F), lambda i: (i, 0))
    osh = jax.ShapeDtypeStruct((S, F), BF16)
    return pl.pallas_call(
        body, grid=(S // tm,),
        in_specs=[pl.BlockSpec((tm, D), lambda i: (i, 0)), wspec, wspec] + after_specs,
        out_specs=[ospec, ospec, ospec], out_shape=[osh, osh, osh],
        compiler_params=_cp("parallel"), name=name)(n, wgt, wut, *after_ops)


def _ffn_dhid(dh, wd, a, b, name):
    S, D = dh.shape
    F = wd.shape[0]
    tm = _pick(S, TS_FFN, 16)

    def body(dh_ref, wd_ref, a_ref, b_ref, da_ref, db_ref):
        x = dh_ref[...].astype(BF16)
        for c0, ce in _col_chunks(F):
            g = _dot_nt(x, wd_ref[c0:c0 + ce, :]).astype(BF16)
            a_ = a_ref[:, c0:c0 + ce]
            sg = _sigmoid(a_)
            silu = a_ * sg
            da_ref[:, c0:c0 + ce] = (g * b_ref[:, c0:c0 + ce]) * (sg + silu * (1.0 - sg))
            db_ref[:, c0:c0 + ce] = g * silu

    tile = pl.BlockSpec((tm, F), lambda i: (i, 0))
    osh = jax.ShapeDtypeStruct((S, F), BF16)
    return pl.pallas_call(
        body, grid=(S // tm,),
        in_specs=[pl.BlockSpec((tm, D), lambda i: (i, 0)),
                  pl.BlockSpec((F, D), lambda i: (0, 0), pipeline_mode=_RESIDENT), tile, tile],
        out_specs=[tile, tile], out_shape=[osh, osh],
        compiler_params=_cp("parallel"), name=name)(dh, wd, a, b)


def _softmax_rows(s):
    m = jnp.max(s, axis=-1, keepdims=True)
    p = jnp.exp(s - m)
    return p / jnp.sum(p, axis=-1, keepdims=True)


def _attn_fwd(q, k, v, name):
    S, D = q.shape
    M = k.shape[0]
    hd = D // XA_HEADS
    scale = hd ** -0.5
    ts = _pick(S, TS_ATTN, 16)

    def body(q_ref, k_ref, v_ref, o_ref):
        for h in range(XA_HEADS):
            sl = slice(h * hd, (h + 1) * hd)
            p = _softmax_rows(_dot_nt(q_ref[:, sl], k_ref[:, sl]) * scale)
            o_ref[:, sl] = _dot(p.astype(BF16), v_ref[:, sl]).astype(BF16)

    tile = pl.BlockSpec((ts, D), lambda i: (i, 0))
    memspec = pl.BlockSpec((M, D), lambda i: (0, 0))
    return pl.pallas_call(
        body, grid=(S // ts,), in_specs=[tile, memspec, memspec], out_specs=tile,
        out_shape=jax.ShapeDtypeStruct((S, D), BF16), compiler_params=_cp("parallel"), name=name)(q, k, v)


def _attn_bwd(q, k, v, do, name):
    S, D = q.shape
    M = k.shape[0]
    hd = D // XA_HEADS
    scale = hd ** -0.5
    ts = _pick(S, TS_ATTN, 16)

    def body(q_ref, k_ref, v_ref, do_ref, dq_ref, dk_ref, dv_ref):
        @pl.when(pl.program_id(0) == 0)
        def _():
            dk_ref[...] = jnp.zeros_like(dk_ref)
            dv_ref[...] = jnp.zeros_like(dv_ref)

        for h in range(XA_HEADS):
            sl = slice(h * hd, (h + 1) * hd)
            qh, kh, vh, doh = q_ref[:, sl], k_ref[:, sl], v_ref[:, sl], do_ref[:, sl]
            p = _softmax_rows(_dot_nt(qh, kh) * scale)
            dp = _dot_nt(doh, vh)
            dv_ref[:, sl] += _dot_tn(p.astype(BF16), doh)
            delta = jnp.sum(dp * p, axis=-1, keepdims=True)
            ds = (p * (dp - delta) * scale).astype(BF16)
            dq_ref[:, sl] = _dot(ds, kh).astype(BF16)
            dk_ref[:, sl] += _dot_tn(ds, qh)

    tile = pl.BlockSpec((ts, D), lambda i: (i, 0))
    memspec = pl.BlockSpec((M, D), lambda i: (0, 0))
    return pl.pallas_call(
        body, grid=(S // ts,), in_specs=[tile, memspec, memspec, tile], out_specs=[tile, memspec, memspec],
        out_shape=[jax.ShapeDtypeStruct((S, D), BF16), jax.ShapeDtypeStruct((M, D), F32),
                   jax.ShapeDtypeStruct((M, D), F32)],
        compiler_params=_cp("arbitrary"), name=name)(q, k, v, do)


def _halo_specs(ts, col):
    per = ts // HALO

    def prev(i):
        return (jnp.maximum(i * per - 1, 0), col)

    def nxt(i, n_tiles):
        return (jnp.minimum((i + 1) * per, n_tiles * per - 1), col)

    return prev, nxt


def _fill_ext(ext_ref, prev_val, main_val, next_val, first, last, ts):
    ext_ref[pl.ds(0, HALO), :] = jnp.where(first, 0.0, prev_val)
    ext_ref[pl.ds(HALO, ts), :] = main_val
    ext_ref[pl.ds(HALO + ts, HALO), :] = jnp.where(last, 0.0, next_val)


SUBLANES = 8


def _fill_shifted(sh_ref, ts):
    n = ts + 2 * HALO - SUBLANES
    for s in range(1, SUBLANES):
        sh_ref[s, pl.ds(0, n), :] = sh_ref[0, pl.ds(s, n), :]


def _tap(sh_ref, r0, offset, rc):
    q, s = divmod(offset, SUBLANES)
    return sh_ref[s, pl.ds(pl.multiple_of(r0 + SUBLANES * q, SUBLANES), rc), :]


def _conv_fwd(z, wa, ba, lng, lnb, wb, bb, name, after=None):
    S = z.shape[0]
    C = z.shape[1] // 5
    KA, KB = wa.shape[0], wb.shape[0]
    pa, pb = KA // 2, KB // 2
    assert pa <= HALO and pb <= HALO
    ts = _pick(S, TS_CONV, ROW_CHUNK_FWD)
    nt = S // ts
    rc = ROW_CHUNK_FWD
    prev, nxt = _halo_specs(ts, 0)
    after_specs, after_ops = _after(after)

    def body(*refs):
        compute(*refs[:9], *refs[9 + len(after_ops):])

    def compute(z_ref, zp_ref, zn_ref, wa_ref, ba_ref, lng_ref, lnb_ref, wb_ref, bb_ref, ab_ref, ca_ref,
                ga_sh, tb_ext, win_b):
        i = pl.program_id(0)
        first, last = i == 0, i == nt - 1

        def glu(r):
            return r[:, 0:C] * _sigmoid(r[:, C:2 * C])

        def gcb(r):
            return r[:, 4 * C:5 * C] * r[:, 2 * C:3 * C]

        _fill_ext(ga_sh.at[0], glu(zp_ref), glu(z_ref), glu(zn_ref), first, last, ts)
        _fill_shifted(ga_sh, ts)
        _fill_ext(tb_ext, gcb(zp_ref), gcb(z_ref), gcb(zn_ref), first, last, ts)

        def chunk(c, carry):
            r0 = pl.multiple_of(c * rc, rc)
            win_b[...] = tb_ext[pl.ds(r0, rc + 2 * HALO), :]
            acc = jnp.zeros((rc, C), F32)
            for k in range(KA):
                acc = acc + wa_ref[k:k + 1, :] * _tap(ga_sh, r0, HALO - pa + k, rc)
            ca = acc + ba_ref[...]
            ca_ref[pl.ds(r0, rc), :] = ca
            mu = jnp.mean(ca, axis=-1, keepdims=True)
            xc = ca - mu
            var = jnp.mean(xc * xc, axis=-1, keepdims=True)
            ln = xc * lax.rsqrt(var + LN_EPS) * lng_ref[...] + lnb_ref[...]
            ab_ref[pl.ds(r0, rc), 0:C] = (ln * _sigmoid(ln)).astype(BF16)
            cb = jnp.zeros((rc, C), F32) + bb_ref[...]
            for k in range(KB):
                cb = cb + wb_ref[k:k + 1, :] * win_b[pl.ds(HALO - pb + k, rc), :]
            ab_ref[pl.ds(r0, rc), C:2 * C] = (z_ref[pl.ds(r0, rc), 3 * C:4 * C] * cb).astype(BF16)
            return carry

        lax.fori_loop(0, ts // rc, chunk, 0)

    zspec = pl.BlockSpec((ts, 5 * C), lambda i: (i, 0))
    zprev = pl.BlockSpec((HALO, 5 * C), prev)
    znext = pl.BlockSpec((HALO, 5 * C), lambda i: nxt(i, nt))

    def full(a):
        return pl.BlockSpec(a.shape, lambda i: (0, 0))

    return pl.pallas_call(
        body, grid=(nt,),
        in_specs=[zspec, zprev, znext, full(wa), full(ba), full(lng), full(lnb), full(wb), full(bb)] + after_specs,
        out_specs=[pl.BlockSpec((ts, 2 * C), lambda i: (i, 0)), pl.BlockSpec((ts, C), lambda i: (i, 0))],
        out_shape=[jax.ShapeDtypeStruct((S, 2 * C), BF16), jax.ShapeDtypeStruct((S, C), F32)],
        scratch_shapes=[pltpu.VMEM((SUBLANES, ts + 2 * HALO, C), F32), pltpu.VMEM((ts + 2 * HALO, C), F32),
                        pltpu.VMEM((rc + 2 * HALO, C), F32)],
        compiler_params=_cp("parallel"), name=name)(z, z, z, wa, ba, lng, lnb, wb, bb, *after_ops)


def _conv_bwd_ln(dab, ca, lng, lnb, name):
    S, C = ca.shape
    ts = _pick(S, TS_ROW, 8)

    def body(da_ref, ca_ref, lng_ref, lnb_ref, dca_ref, dg_ref, db_ref, dbias_ref):
        @pl.when(pl.program_id(0) == 0)
        def _():
            dg_ref[...] = jnp.zeros_like(dg_ref)
            db_ref[...] = jnp.zeros_like(db_ref)
            dbias_ref[...] = jnp.zeros_like(dbias_ref)

        ca_ = ca_ref[...]
        mu = jnp.mean(ca_, axis=-1, keepdims=True)
        xc = ca_ - mu
        rstd = lax.rsqrt(jnp.mean(xc * xc, axis=-1, keepdims=True) + LN_EPS)
        xh = xc * rstd
        ln = xh * lng_ref[...] + lnb_ref[...]
        sg = _sigmoid(ln)
        dln = da_ref[...].astype(F32) * (sg * (1.0 + ln * (1.0 - sg)))
        dg_ref[...] += jnp.sum(dln * xh, axis=0, keepdims=True)
        db_ref[...] += jnp.sum(dln, axis=0, keepdims=True)
        dxh = dln * lng_ref[...]
        dca = rstd * (dxh - jnp.mean(dxh, axis=-1, keepdims=True) - xh * jnp.mean(dxh * xh, axis=-1, keepdims=True))
        dca_ref[...] = dca
        dbias_ref[...] += jnp.sum(dca, axis=0, keepdims=True)

    tile = pl.BlockSpec((ts, C), lambda i: (i, 0))
    vec = pl.BlockSpec((1, C), lambda i: (0, 0))
    vsh = jax.ShapeDtypeStruct((1, C), F32)
    return pl.pallas_call(
        body, grid=(S // ts,), in_specs=[tile, tile, vec, vec], out_specs=[tile, vec, vec, vec],
        out_shape=[jax.ShapeDtypeStruct((S, C), F32), vsh, vsh, vsh],
        compiler_params=_cp("arbitrary"), name=name)(dab, ca, lng, lnb)


def _conv_bwd(z, dca, dab, wa, wb, bb, name):
    S = z.shape[0]
    C = z.shape[1] // 5
    KA, KB = wa.shape[0], wb.shape[0]
    pa, pb = KA // 2, KB // 2
    ts = _pick(S, TS_CONV, ROW_CHUNK)
    nt = S // ts
    rc = ROW_CHUNK
    prev0, nxt0 = _halo_specs(ts, 0)
    prev1, nxt1 = _halo_specs(ts, 1)

    def body(z_ref, zp_ref, zn_ref, dca_ref, dcap_ref, dcan_ref, db_ref, dbp_ref, dbn_ref, wa_ref, wb_ref, bb_ref,
             dz_ref, dwa_ref, dwb_ref, dbb_ref,
             ga_sh, dca_sh, tb_ext, dcb_ext, win_tb, win_dcb, acc_a, acc_b, acc_bias):
        i = pl.program_id(0)
        first, last = i == 0, i == nt - 1

        @pl.when(first)
        def _():
            acc_a[...] = jnp.zeros_like(acc_a)
            acc_b[...] = jnp.zeros_like(acc_b)
            acc_bias[...] = jnp.zeros_like(acc_bias)

        def glu(r):
            return r[:, 0:C] * _sigmoid(r[:, C:2 * C])

        def gcb(r):
            return r[:, 4 * C:5 * C] * r[:, 2 * C:3 * C]

        def dcb(d, r):
            return d[...].astype(F32) * r[:, 3 * C:4 * C]

        _fill_ext(ga_sh.at[0], glu(zp_ref), glu(z_ref), glu(zn_ref), first, last, ts)
        _fill_shifted(ga_sh, ts)
        _fill_ext(dca_sh.at[0], dcap_ref[...], dca_ref[...], dcan_ref[...], first, last, ts)
        _fill_shifted(dca_sh, ts)
        _fill_ext(tb_ext, gcb(zp_ref), gcb(z_ref), gcb(zn_ref), first, last, ts)
        _fill_ext(dcb_ext, dcb(dbp_ref, zp_ref), dcb(db_ref, z_ref), dcb(dbn_ref, zn_ref), first, last, ts)

        def fold(x):
            return jnp.sum(x.reshape(rc // 8, 8, C), axis=0)

        def chunk(c, carry):
            r0 = pl.multiple_of(c * rc, rc)
            win_tb[...] = tb_ext[pl.ds(r0, rc + 2 * HALO), :]
            win_dcb[...] = dcb_ext[pl.ds(r0, rc + 2 * HALO), :]
            dca_c = _tap(dca_sh, r0, HALO, rc)
            dglu = jnp.zeros((rc, C), F32)
            for k in range(KA):
                dglu = dglu + wa_ref[k:k + 1, :] * _tap(dca_sh, r0, HALO + pa - k, rc)
                acc_a[k] += fold(dca_c * _tap(ga_sh, r0, HALO - pa + k, rc))
            val = z_ref[pl.ds(r0, rc), 0:C]
            sg = _sigmoid(z_ref[pl.ds(r0, rc), C:2 * C])
            dz_ref[pl.ds(r0, rc), 0:C] = (dglu * sg).astype(BF16)
            dz_ref[pl.ds(r0, rc), C:2 * C] = (dglu * val * sg * (1.0 - sg)).astype(BF16)
            dcb_c = win_dcb[pl.ds(HALO, rc), :]
            cb = jnp.zeros((rc, C), F32) + bb_ref[...]
            dt = jnp.zeros((rc, C), F32)
            for k in range(KB):
                tb_k = win_tb[pl.ds(HALO - pb + k, rc), :]
                cb = cb + wb_ref[k:k + 1, :] * tb_k
                dt = dt + wb_ref[k:k + 1, :] * win_dcb[pl.ds(HALO + pb - k, rc), :]
                acc_b[k] += fold(dcb_c * tb_k)
            acc_bias[...] += fold(dcb_c)
            db_c = db_ref[pl.ds(r0, rc), :].astype(F32)
            dz_ref[pl.ds(r0, rc), 2 * C:3 * C] = (dt * z_ref[pl.ds(r0, rc), 4 * C:5 * C]).astype(BF16)
            dz_ref[pl.ds(r0, rc), 3 * C:4 * C] = (db_c * cb).astype(BF16)
            dz_ref[pl.ds(r0, rc), 4 * C:5 * C] = (dt * z_ref[pl.ds(r0, rc), 2 * C:3 * C]).astype(BF16)
            return carry

        lax.fori_loop(0, ts // rc, chunk, 0)

        @pl.when(last)
        def _():
            dwa_ref[...] = jnp.sum(acc_a[...], axis=1)
            dwb_ref[...] = jnp.sum(acc_b[...], axis=1)
            dbb_ref[...] = jnp.sum(acc_bias[...], axis=0, keepdims=True)

    zspec = pl.BlockSpec((ts, 5 * C), lambda i: (i, 0))
    zprev = pl.BlockSpec((HALO, 5 * C), prev0)
    znext = pl.BlockSpec((HALO, 5 * C), lambda i: nxt0(i, nt))
    dspec = pl.BlockSpec((ts, C), lambda i: (i, 0))
    dprev = pl.BlockSpec((HALO, C), prev0)
    dnext = pl.BlockSpec((HALO, C), lambda i: nxt0(i, nt))
    bspec = pl.BlockSpec((ts, C), lambda i: (i, 1))
    bprev = pl.BlockSpec((HALO, C), prev1)
    bnext = pl.BlockSpec((HALO, C), lambda i: nxt1(i, nt))

    def full(shape):
        return pl.BlockSpec(shape, lambda i: (0,) * len(shape))

    ext = pltpu.VMEM((ts + 2 * HALO, C), F32)
    shifted = pltpu.VMEM((SUBLANES, ts + 2 * HALO, C), F32)
    win = pltpu.VMEM((rc + 2 * HALO, C), F32)
    return pl.pallas_call(
        body, grid=(nt,),
        in_specs=[zspec, zprev, znext, dspec, dprev, dnext, bspec, bprev, bnext,
                  full(wa.shape), full(wb.shape), full(bb.shape)],
        out_specs=[pl.BlockSpec((ts, 5 * C), lambda i: (i, 0)), full((KA, C)), full((KB, C)), full((1, C))],
        out_shape=[jax.ShapeDtypeStruct((S, 5 * C), BF16), jax.ShapeDtypeStruct((KA, C), F32),
                   jax.ShapeDtypeStruct((KB, C), F32), jax.ShapeDtypeStruct((1, C), F32)],
        scratch_shapes=[shifted, shifted, ext, ext, win, win,
                        pltpu.VMEM((KA, 8, C), F32), pltpu.VMEM((KB, 8, C), F32), pltpu.VMEM((8, C), F32)],
        compiler_params=_cp("arbitrary"), name=name)(z, z, z, dca, dca, dca, dab, dab, dab, wa, wb, bb)


_GELU_C = 0.7978845608028654
_GELU_A = 0.044715


def _gelu(x):
    return 0.5 * x * (1.0 + jnp.tanh(_GELU_C * (x + _GELU_A * (x * x * x))))


def _gelu_and_grad(x):
    t = jnp.tanh(_GELU_C * (x + _GELU_A * (x * x * x)))
    hx = 0.5 * x
    return hx * (1.0 + t), 0.5 * (1.0 + t) + hx * (1.0 - t * t) * (_GELU_C * (1.0 + 3.0 * _GELU_A * x * x))


def _sgu_fwd(zp, lng, lnb, ws, bsb, name):
    S = zp.shape[0]
    D = zp.shape[1] // 2
    G = ws.shape[0]
    gd = D // G
    ts = _pick(S, TS_SGU, CHUNK)
    ncs = ts // CHUNK

    def body(zp_ref, lng_ref, lnb_ref, ws_ref, bsb_ref, y_ref, vb_ref):
        v = _gelu(zp_ref[:, D:2 * D])
        mu = jnp.mean(v, axis=-1, keepdims=True)
        xc = v - mu
        rstd = lax.rsqrt(jnp.mean(xc * xc, axis=-1, keepdims=True) + LN_EPS)
        vb_ref[...] = (xc * rstd * lng_ref[...] + lnb_ref[...]).astype(BF16)
        for c in range(ncs):
            rows = slice(c * CHUNK, (c + 1) * CHUNK)
            for g in range(G):
                cols = slice(g * gd, (g + 1) * gd)
                sv = _dot(ws_ref[g], vb_ref[rows, cols]) + bsb_ref[:, cols]
                y_ref[rows, cols] = (_gelu(zp_ref[rows, cols]) * sv).astype(BF16)

    def full(a):
        return pl.BlockSpec(a.shape, lambda i: (0,) * a.ndim)

    return pl.pallas_call(
        body, grid=(S // ts,),
        in_specs=[pl.BlockSpec((ts, 2 * D), lambda i: (i, 0)), full(lng), full(lnb), full(ws), full(bsb)],
        out_specs=pl.BlockSpec((ts, D), lambda i: (i, 0)), out_shape=jax.ShapeDtypeStruct((S, D), BF16),
        scratch_shapes=[pltpu.VMEM((ts, D), BF16)],
        compiler_params=_cp("parallel"), name=name)(zp, lng, lnb, ws, bsb)


def _sgu_bwd(dy, zp, lng, lnb, ws, wst, bsb, name):
    S = zp.shape[0]
    D = zp.shape[1] // 2
    G = ws.shape[0]
    gd = D // G
    ts = _pick(S, TS_SGU, CHUNK)
    ncs = ts // CHUNK

    def body(dy_ref, zp_ref, lng_ref, lnb_ref, ws_ref, wst_ref, bsb_ref,
             dzp_ref, dws_ref, dbs_ref, dg_ref, db_ref, vb_ref, dvln_ref, acc_bs):
        i = pl.program_id(0)

        @pl.when(i == 0)
        def _():
            dws_ref[...] = jnp.zeros_like(dws_ref)
            acc_bs[...] = jnp.zeros_like(acc_bs)
            dg_ref[...] = jnp.zeros_like(dg_ref)
            db_ref[...] = jnp.zeros_like(db_ref)

        v, dv_dz = _gelu_and_grad(zp_ref[:, D:2 * D])
        mu = jnp.mean(v, axis=-1, keepdims=True)
        xc = v - mu
        rstd = lax.rsqrt(jnp.mean(xc * xc, axis=-1, keepdims=True) + LN_EPS)
        xh = xc * rstd
        vb_ref[...] = (xh * lng_ref[...] + lnb_ref[...]).astype(BF16)
        for c in range(ncs):
            rows = slice(c * CHUNK, (c + 1) * CHUNK)
            for g in range(G):
                cols = slice(g * gd, (g + 1) * gd)
                u, du_dz = _gelu_and_grad(zp_ref[rows, cols])
                dy_ = dy_ref[rows, cols].astype(F32)
                sv = _dot(ws_ref[g], vb_ref[rows, cols]) + bsb_ref[:, cols]
                dzp_ref[rows, cols] = (dy_ * sv * du_dz).astype(BF16)
                dsv = dy_ * u
                acc_bs[:, cols] += dsv
                dsvb = dsv.astype(BF16)
                dws_ref[g] += _dot_nt(dsvb, vb_ref[rows, cols])
                dvln_ref[rows, cols] = _dot(wst_ref[g], dsvb)
        dvln = dvln_ref[...]
        dg_ref[...] += jnp.sum(dvln * xh, axis=0, keepdims=True)
        db_ref[...] += jnp.sum(dvln, axis=0, keepdims=True)
        dxh = dvln * lng_ref[...]
        dv = rstd * (dxh - jnp.mean(dxh, axis=-1, keepdims=True) - xh * jnp.mean(dxh * xh, axis=-1, keepdims=True))
        dzp_ref[:, D:2 * D] = (dv * dv_dz).astype(BF16)

        @pl.when(i == pl.num_programs(0) - 1)
        def _():
            dbs_ref[...] = acc_bs[...]

    def full(shape):
        return pl.BlockSpec(shape, lambda i: (0,) * len(shape))

    return pl.pallas_call(
        body, grid=(S // ts,),
        in_specs=[pl.BlockSpec((ts, D), lambda i: (i, 0)), pl.BlockSpec((ts, 2 * D), lambda i: (i, 0)),
                  full(lng.shape), full(lnb.shape), full(ws.shape), full(wst.shape), full(bsb.shape)],
        out_specs=[pl.BlockSpec((ts, 2 * D), lambda i: (i, 0)), full(ws.shape), full(bsb.shape),
                   full((1, D)), full((1, D))],
        out_shape=[jax.ShapeDtypeStruct((S, 2 * D), BF16), jax.ShapeDtypeStruct(ws.shape, F32),
                   jax.ShapeDtypeStruct(bsb.shape, F32), jax.ShapeDtypeStruct((1, D), F32),
                   jax.ShapeDtypeStruct((1, D), F32)],
        scratch_shapes=[pltpu.VMEM((ts, D), BF16), pltpu.VMEM((ts, D), F32),
                        pltpu.VMEM(bsb.shape, F32)],
        compiler_params=_cp("arbitrary"), name=name)(dy, zp, lng, lnb, ws, wst, bsb)


def _group_sum(x, groups, name):
    P, D = x.shape
    gd = D // groups

    def body(x_ref, o_ref):
        for g in range(groups):
            o_ref[:, g:g + 1] = jnp.sum(x_ref[:, g * gd:(g + 1) * gd], axis=1, keepdims=True)

    return pl.pallas_call(body, out_shape=jax.ShapeDtypeStruct((P, groups), F32), name=name)(x)


def _adamw_small(ws, gs, ms, vs, name):
    n = len(ws)
    shapes = [w.shape for w in ws]
    flat = [(w.size // w.shape[-1], w.shape[-1]) for w in ws]

    def body(*refs):
        for k in range(n):
            w_ref, g_ref, m_ref, v_ref = (refs[j * n + k] for j in range(4))
            d_ref, nm_ref, nv_ref = (refs[(4 + j) * n + k] for j in range(3))
            d_ref[...], nm_ref[...], nv_ref[...] = _adamw_math(w_ref[...], g_ref[...], m_ref[...], v_ref[...])

    outs = pl.pallas_call(
        body, out_shape=[jax.ShapeDtypeStruct(f, F32) for f in flat] * 3, name=name)(
            *(a.reshape(f) for group in (ws, gs, ms, vs) for a, f in zip(group, flat)))
    return [tuple(outs[j * n + k].reshape(shapes[k]) for j in range(3)) for k in range(n)]


_HBM = pl.BlockSpec(memory_space=pltpu.HBM)


def _remote(src, dst, send_sem, recv_sem, to):
    return pltpu.make_async_remote_copy(src_ref=src, dst_ref=dst, send_sem=send_sem, recv_sem=recv_sem,
                                        device_id=to, device_id_type=MESH)


def _all_gather(block, name):
    R, C = block.shape

    def body(x_ref, out_ref, send_sems, recv_sems, local_sem):
        x, y, c = lax.axis_index("x"), lax.axis_index("y"), lax.axis_index("c")
        me, sibling = (x, y, c), (x, y, 1 - c)
        chips = [(1 - x, y), (x, 1 - y), (1 - x, 1 - y)]

        def slot(px, py, pc):
            return out_ref.at[4 * px + 2 * py + pc]

        def copy(k, blk, to, src=None):
            return _remote(slot(*blk) if src is None else src, slot(*blk), send_sems.at[k], recv_sems.at[k], to)

        mine = pltpu.make_async_copy(x_ref, slot(*me), local_sem)
        mine.start()
        first = [copy(0, me, sibling, src=x_ref)]
        first += [copy(1 + j, me, (*chip, c), src=x_ref) for j, chip in enumerate(chips)]
        for cp in first:
            cp.start()
        passed = [copy(4 + j, (*chip, c), sibling) for j, chip in enumerate(chips)]
        for j, chip in enumerate(chips):
            copy(1 + j, (*chip, c), me).wait_recv()
            passed[j].start()
        copy(0, sibling, me).wait_recv()
        for j, chip in enumerate(chips):
            copy(4 + j, (*chip, 1 - c), me).wait_recv()
        for cp in first + passed:
            cp.wait_send()
        mine.wait()

    return pl.pallas_call(
        body, out_shape=jax.ShapeDtypeStruct((NDEV, R, C), block.dtype), in_specs=[_HBM], out_specs=_HBM,
        scratch_shapes=[pltpu.SemaphoreType.DMA((7,)), pltpu.SemaphoreType.DMA((7,)), pltpu.SemaphoreType.DMA],
        name=name)(block)


def _all_gather_weights(pack, rows, name, after=None):
    C = pack.shape[1]
    nw = len(rows)
    starts = [sum(rows[:w]) for w in range(nw)]
    after_specs, after_ops = _after(after)

    def body(pack_ref, *rest):
        rest = rest[len(after_ops):]
        outs = rest[:nw]
        send_sems, recv_sems, local_sem = rest[nw:]
        x, y, c = lax.axis_index("x"), lax.axis_index("y"), lax.axis_index("c")
        me, sibling = (x, y, c), (x, y, 1 - c)
        chips = [(1 - x, y), (x, 1 - y), (1 - x, 1 - y)]

        def block(w, px, py, pc):
            return outs[w].at[pl.ds((4 * px + 2 * py + pc) * rows[w], rows[w])]

        def mine(w):
            return pack_ref.at[pl.ds(starts[w], rows[w])]

        def all_of(k):
            return _remote(pack_ref, pack_ref, send_sems.at[k], recv_sems.at[k], me)

        for w in range(nw):
            pltpu.make_async_copy(mine(w), block(w, *me), local_sem).start()
        for k, to in enumerate([sibling] + [(*chip, c) for chip in chips]):
            for w in range(nw):
                _remote(mine(w), block(w, *me), send_sems.at[k], recv_sems.at[k], to).start()
        for j, chip in enumerate(chips):
            all_of(1 + j).wait_recv()
            for w in range(nw):
                _remote(block(w, *chip, c), block(w, *chip, c), send_sems.at[4 + j], recv_sems.at[4 + j], sibling).start()
        all_of(0).wait_recv()
        for j in range(3):
            all_of(4 + j).wait_recv()
        for k in range(7):
            all_of(k).wait_send()
        pltpu.make_async_copy(pack_ref, pack_ref, local_sem).wait()

    return pl.pallas_call(
        body, out_shape=[jax.ShapeDtypeStruct((NDEV * r, C), pack.dtype) for r in rows],
        in_specs=[_HBM] + after_specs, out_specs=[_HBM] * nw,
        scratch_shapes=[pltpu.SemaphoreType.DMA((7,)), pltpu.SemaphoreType.DMA((7,)), pltpu.SemaphoreType.DMA],
        name=name)(pack, *after_ops)


_SEM = pl.BlockSpec(memory_space=pltpu.SEMAPHORE)
_DATAFLOW = pltpu.SideEffectType.DATAFLOW_SIDE_EFFECTING


def _split_start(srcs, lands, plan, n, after, name):
    nbuf = len(srcs) + len(lands)
    after_specs, after_ops = _after(after)

    def body(*refs):
        src_refs, land_refs = refs[:len(srcs)], refs[len(srcs):nbuf]
        send_sems, recv_sems = refs[nbuf + len(after_ops)], refs[nbuf + len(after_ops) + 1]
        for k, (src, dst, to) in enumerate(plan(src_refs, land_refs)):
            _remote(src, dst, send_sems.at[k], recv_sems.at[k], to).start()
        refs[-1][...] = jnp.zeros_like(refs[-1])

    bufs = [pltpu.with_memory_space_constraint(a, pltpu.HBM) for a in list(srcs) + list(lands)]
    outs = pl.pallas_call(
        body, name=name,
        out_shape=(pltpu.SemaphoreType.DMA((n,)), pltpu.SemaphoreType.DMA((n,)),
                   *[pltpu.HBM(a.shape, a.dtype) for a in bufs], jax.ShapeDtypeStruct((8, 128), F32)),
        in_specs=[_HBM] * nbuf + after_specs,
        out_specs=(_SEM, _SEM, *[_HBM] * nbuf, pl.BlockSpec(memory_space=pltpu.VMEM)),
        input_output_aliases={i: 2 + i for i in range(nbuf)},
        compiler_params=pltpu.CompilerParams(has_side_effects=_DATAFLOW))(*bufs, *after_ops)
    return outs[0], outs[1], list(outs[2:2 + len(srcs)]), list(outs[2 + len(srcs):2 + nbuf]), outs[-1]


def _split_wait(send_sems, recv_sems, srcs, lands, plan, after, name):
    nbuf = len(srcs) + len(lands)
    after_specs, after_ops = _after(after)

    def body(*refs):
        src_refs, land_refs = refs[:len(srcs)], refs[len(srcs):nbuf]
        send_sems_ref, recv_sems_ref = refs[nbuf], refs[nbuf + 1]
        for k, (src, dst, to) in enumerate(plan(src_refs, land_refs)):
            copy = _remote(src, dst, send_sems_ref.at[k], recv_sems_ref.at[k], to)
            copy.wait_send()
            copy.wait_recv()

    outs = pl.pallas_call(
        body, name=name, out_shape=tuple(pltpu.HBM(a.shape, a.dtype) for a in list(srcs) + list(lands)),
        in_specs=[_HBM] * nbuf + [_SEM, _SEM] + after_specs, out_specs=tuple([_HBM] * nbuf),
        input_output_aliases={i: i for i in range(nbuf)},
        compiler_params=pltpu.CompilerParams(has_side_effects=_DATAFLOW))(*srcs, *lands, send_sems, recv_sems, *after_ops)
    return list(outs[:len(srcs)]), list(outs[len(srcs):])


def _peers(x, y, c):
    return [(mask, (1 - x if mask & 4 else x, 1 - y if mask & 2 else y, 1 - c if mask & 1 else c))
            for mask in range(1, NDEV)]


def _gather_plan(rows):
    starts = [sum(rows[:w]) for w in range(len(rows))]

    def plan(src_refs, land_refs):
        x, y, c = lax.axis_index("x"), lax.axis_index("y"), lax.axis_index("c")
        copies = []
        for w, r in enumerate(rows):
            mine = src_refs[0].at[pl.ds(starts[w], r)]
            dst = land_refs[w].at[pl.ds((4 * x + 2 * y + c) * r, r)]
            copies += [(mine, dst, peer) for _, peer in _peers(x, y, c)]
        return copies

    return plan, (NDEV - 1) * len(rows)


def _place_own(shards, fulls, dev_idx, name):
    nw = len(shards)

    def body(i_ref, *refs):
        for w in range(nw):
            refs[2 * nw + w][...] = refs[w][...]

    grid_spec = pltpu.PrefetchScalarGridSpec(
        num_scalar_prefetch=1, grid=(1,),
        in_specs=[pl.BlockSpec(s.shape, lambda t, i_ref: (0, 0)) for s in shards] + [_ANY] * nw,
        out_specs=[pl.BlockSpec(s.shape, lambda t, i_ref: (i_ref[0], 0)) for s in shards])
    outs = pl.pallas_call(
        body, grid_spec=grid_spec, out_shape=[jax.ShapeDtypeStruct(f.shape, f.dtype) for f in fulls],
        input_output_aliases={1 + nw + w: w for w in range(nw)}, name=name)(dev_idx, *shards, *fulls)
    return list(outs)


def _scatter_plan(rows):
    def plan(src_refs, land_refs):
        x, y, c = lax.axis_index("x"), lax.axis_index("y"), lax.axis_index("c")
        copies = []
        for w, r in enumerate(rows):
            for mask, (px, py, pc) in _peers(x, y, c):
                src = src_refs[w].at[pl.ds((4 * px + 2 * py + pc) * r, r)]
                copies.append((src, land_refs[w].at[mask - 1], (px, py, pc)))
        return copies

    return plan, (NDEV - 1) * len(rows)


def _adamw_math(w, g, m, v):
    nm = ADAM_B1 * m + (1.0 - ADAM_B1) * g
    nv = ADAM_B2 * v + (1.0 - ADAM_B2) * (g * g)
    bc1 = 1.0 - ADAM_B1 ** ADAM_STEP
    bc2 = 1.0 - ADAM_B2 ** ADAM_STEP
    return -ADAM_LR * ((nm / bc1) / (jnp.sqrt(nv / bc2) + ADAM_EPS) + ADAM_WD * w), nm, nv


def _finish_weight(gs, gots, dev_idx, w, m, v, name, after=None):
    L = len(gs)
    n1, r, C = gots[0].shape
    block = (None,) + w.shape[1:]

    after_specs, after_ops = _after(after)

    def body(i_ref, *refs):
        ins, (w_ref, m_ref, v_ref), (g_out, d_out, m_out, v_out) = refs[:2 * L], refs[2 * L:2 * L + 3], refs[-4:]
        for layer in range(L):
            @pl.when(pl.program_id(0) == layer)
            def _():
                g_ref, got_ref = ins[2 * layer], ins[2 * layer + 1]
                acc = g_ref[...].astype(F32)
                for k in range(n1):
                    acc = acc + got_ref[k].astype(F32)
                g_out[...] = acc
                d_out[...], m_out[...], v_out[...] = _adamw_math(w_ref[...], acc, m_ref[...], v_ref[...])

    in_specs, ins = [], []
    for g, got in zip(gs, gots):
        in_specs += [pl.BlockSpec((r, C), lambda t, i_ref: (i_ref[0], 0), pipeline_mode=_RESIDENT),
                     pl.BlockSpec((n1, r, C), lambda t, i_ref: (0, 0, 0), pipeline_mode=_RESIDENT)]
        ins += [g, got]
    per_layer = pl.BlockSpec(block, lambda t, i_ref: (t, 0, 0))
    grid_spec = pltpu.PrefetchScalarGridSpec(
        num_scalar_prefetch=1, grid=(L,), in_specs=in_specs + [per_layer] * 3 + after_specs,
        out_specs=[per_layer] * 4)
    return pl.pallas_call(
        body, grid_spec=grid_spec, out_shape=[jax.ShapeDtypeStruct(w.shape, F32)] * 4,
        compiler_params=_cp("arbitrary"), name=name)(dev_idx, *ins, w, m, v, *after_ops)


def _sum_slots(a, name):
    n, R, C = a.shape

    def body(a_ref, o_ref):
        acc = a_ref[0]
        for k in range(1, n):
            acc = acc + a_ref[k]
        o_ref[...] = acc

    return pl.pallas_call(body, out_shape=jax.ShapeDtypeStruct((R, C), F32), name=name)(a)


def _shard_axis(name):
    return {"ev_w_in": 2, "ev_a_conv_w": 2, "ev_b_conv_w": 2, "ev_w_out": 1, "od_w_in": 2, "od_c_ln_g": 1,
            "od_c_ln_b": 1, "od_w_out": 1, "xa_w_q": 1, "xa_w_k": 1, "xa_w_v": 1, "xa_w_o": 1,
            "ffn_w_gate": 2, "ffn_w_up": 2, "ffn_w_down": 1}[name]


BIG = ["ev_w_in", "ev_w_out", "od_w_in", "od_w_out", "xa_w_q", "xa_w_k", "xa_w_v", "xa_w_o",
       "ffn_w_gate", "ffn_w_up", "ffn_w_down"]
SMALL_SHARDED = ["ev_a_conv_w", "ev_b_conv_w", "od_c_ln_g", "od_c_ln_b"]
REPLICATED = ["g_mix", "g_xattn", "g_mem", "g_ffn", "g_final", "ev_a_conv_b", "ev_a_ln_g", "ev_a_ln_b",
              "ev_b_conv_b", "od_w_s", "od_b_s"]
WEIGHTS = ["g_mix", "g_xattn", "g_mem", "g_ffn", "g_final", "ev_w_in", "ev_a_conv_w", "ev_a_conv_b", "ev_a_ln_g",
           "ev_a_ln_b", "ev_b_conv_w", "ev_b_conv_b", "ev_w_out", "od_w_in", "od_c_ln_g", "od_c_ln_b", "od_w_s",
           "od_b_s", "od_w_out", "xa_w_q", "xa_w_k", "xa_w_v", "xa_w_o", "ffn_w_gate", "ffn_w_up", "ffn_w_down"]


def _full_from_blocks(blocks, axis):
    shard = blocks.shape[1:]
    full = jnp.moveaxis(blocks, 0, axis)
    return full.reshape(shard[:axis] + (NDEV * shard[axis],) + shard[axis + 1:])


def _blocks_from_full(full, axis):
    shp = full.shape
    split = full.reshape(shp[:axis] + (NDEV, shp[axis] // NDEV) + shp[axis + 1:])
    return jnp.moveaxis(split, axis, 0)


def _pad_rows(flat, width, row_align):
    per = width * row_align
    n = -(-flat.shape[0] // per) * per
    return jnp.pad(flat, (0, n - flat.shape[0])).reshape(n // width, width)


def _row(v):
    return v.reshape(1, -1)


def _mem_kv(mem, g_m, wk, wv, name):
    M, D = mem.shape

    def body(mem_ref, g_ref, wk_ref, wv_ref, n_ref, k_ref, v_ref):
        x = mem_ref[...]
        r = lax.rsqrt(jnp.mean(x * x, axis=-1, keepdims=True) + RMS_EPS)
        n = ((x * r) * g_ref[...]).astype(BF16)
        n_ref[...] = n
        k_ref[...] = _dot(n, wk_ref[...]).astype(BF16)
        v_ref[...] = _dot(n, wv_ref[...]).astype(BF16)

    sh = jax.ShapeDtypeStruct((M, D), BF16)
    return pl.pallas_call(body, out_shape=[sh, sh, sh], name=name)(mem, g_m, wk, wv)


def _mem_kv_bwd(dk, dv, mem, mem_n, g_m, wk, wv, name):
    M, D = mem.shape

    def body(dk_ref, dv_ref, mem_ref, n_ref, g_ref, wk_ref, wv_ref, dwk_ref, dwv_ref, dg_ref):
        dk_, dv_ = dk_ref[...].astype(BF16), dv_ref[...].astype(BF16)
        n = n_ref[...]
        dwk_ref[...] = _dot_tn(n, dk_).astype(BF16)
        dwv_ref[...] = _dot_tn(n, dv_).astype(BF16)
        dn = _dot_nt(dk_, wk_ref[...]) + _dot_nt(dv_, wv_ref[...])
        x = mem_ref[...]
        r = lax.rsqrt(jnp.mean(x * x, axis=-1, keepdims=True) + RMS_EPS)
        dg_ref[...] = jnp.sum(dn * (x * r), axis=0, keepdims=True)

    wsh = jax.ShapeDtypeStruct((D, D), BF16)
    return pl.pallas_call(body, out_shape=[wsh, wsh, jax.ShapeDtypeStruct((1, D), F32)], name=name)(
        dk, dv, mem, mem_n, g_m, wk, wv)


def _xattn_fwd(h, nq, mem, g_m, wq, wk, wv, wo, g_next, tag, after):
    q = _mm([(nq, wq, "nn")], f"xa_q_{tag}", out_dtype=BF16, after=after)
    mem_n, k, v = _mem_kv(mem, _row(g_m), wk, wv, f"xa_mem_{tag}")
    o = _attn_fwd(q, k, v, f"xa_attn_{tag}")
    h_new, n_next = _mm([(o, wo, "nn")], f"xa_o_{tag}", res=h, rms_g=_row(g_next))
    return h_new, n_next, (h, nq, mem_n, q, k, v, o)


def _xattn_bwd(dh_new, saved, mem, g_x, g_m, wq, wk, wv, wo, tag, push):
    h, nq, mem_n, q, k, v, o = saved
    do = _mm([(dh_new, wo, "nt")], f"xa_do_{tag}", out_dtype=BF16)
    d_wo = _mm_tn(o, dh_new, f"xa_dwo_{tag}")
    dq, dk, dv = _attn_bwd(q, k, v, do, f"xa_attn_bwd_{tag}")
    d_wq = _mm_tn(nq, dq, f"xa_dwq_{tag}")
    d_wk, d_wv, d_gm = _mem_kv_bwd(dk, dv, mem, mem_n, _row(g_m), wk, wv, f"xa_mem_bwd_{tag}")
    token = push([d_wq, d_wk, d_wv, d_wo])
    dh, d_gx = _mm([(dq, wq, "nt")], f"xa_dnq_{tag}", rms_bwd=(h, _row(g_x), dh_new), tm=1024, after=token)
    return dh, dict(g_xattn=d_gx, g_mem=d_gm)


def _ffn_fwd(h, n, wgt, wut, wd, g_next, tag, after, loss=None):
    a, b, hid = _ffn_up(n, wgt, wut, f"ffn_up_{tag}", after=after)
    saved = (h, n, a, b, hid)
    if loss is not None:
        return _mm([(hid, wd, "nn")], f"ffn_down_{tag}", res=h, loss=loss, tm=1024), saved
    h_new, n_next = _mm([(hid, wd, "nn")], f"ffn_down_{tag}", res=h, rms_g=_row(g_next), tm=1024)
    return h_new, n_next, saved


def _ffn_bwd(dh_new, saved, g_f, wgt, wut, wd, tag, push):
    h, n, a, b, hid = saved
    da, db = _ffn_dhid(dh_new, wd, a, b, f"ffn_dhid_{tag}")
    d_wd = _mm_tn(hid, dh_new, f"ffn_dwd_{tag}", ts=1024, tn=1024)
    d_wgt = _mm_tn(da, n, f"ffn_dwg_{tag}", ts=1024, tn=1024)
    d_wut = _mm_tn(db, n, f"ffn_dwu_{tag}", ts=1024, tn=1024)
    token = push([d_wgt, d_wut, d_wd])
    dh, d_gf = _mm([(da, wgt, "nn"), (db, wut, "nn")], f"ffn_dn_{tag}", rms_bwd=(h, _row(g_f), dh_new), tm=512,
                   after=token)
    return dh, dict(g_ffn=d_gf)


_XA = ["xa_w_q", "xa_w_k", "xa_w_v", "xa_w_o"]
_FFN = ["ffn_w_gate", "ffn_w_up", "ffn_w_down"]
GATHERS = {
    "ev_in": [("ev_w_in", 0)],
    "xa0": [("ev_w_out", 0)] + [(n, 0) for n in _XA],
    "ffn0": [(n, 0) for n in _FFN],
    "od": [("od_w_in", 0), ("od_w_out", 0)],
    "xa1": [(n, 1) for n in _XA],
    "ffn1": [(n, 1) for n in _FFN],
}
SCATTERS = {
    "ffn1": [(n, 1) for n in _FFN],
    "xa1": [(n, 1) for n in _XA],
    "od": [("od_w_in", 0), ("od_w_out", 0)],
    "ffn0": [(n, 0) for n in _FFN],
    "xa0": [(n, 0) for n in _XA],
    "ev_out": [("ev_w_out", 0)],
    "ev_in": [("ev_w_in", 0)],
}


def _local_step(x, mem, loss_target, W, comm):
    grads = {}

    h0 = x
    (ev_w_in_t,), token = comm.weights("ev_in", None)
    z, n0 = _rms_mm(h0, _row(W["g_mix"][0]), ev_w_in_t, "ev_in", tn=1280, after=token)
    token = comm.prefetch(["ffn0"], z)
    ab, ca = _conv_fwd(z, W["ev_a_conv_w"][0], W["ev_a_conv_b"], W["ev_a_ln_g"], W["ev_a_ln_b"],
                       W["ev_b_conv_w"][0], W["ev_b_conv_b"], "ev_conv", after=token)
    (ev_w_out, *xa_w0), _ = comm.weights("xa0", ab)
    h1, nq0 = _mm([(ab, ev_w_out, "nn")], "ev_out", res=h0, rms_g=_row(W["g_xattn"][0]))
    token = comm.prefetch(["od", "xa1"], nq0)
    h2, nf0, xa0 = _xattn_fwd(h1, nq0, mem, W["g_mem"][0], *xa_w0, W["g_ffn"][0], "l0", token)
    ffn_w0, _ = comm.weights("ffn0", nf0)
    token = comm.prefetch(["ffn1"], nf0)
    h3, n3, ff0 = _ffn_fwd(h2, nf0, *ffn_w0, W["g_mix"][1], "l0", token)

    (od_w_in_t, od_w_out), _ = comm.weights("od", n3)
    zp = _mm([(n3, od_w_in_t, "nt")], "od_in", tn=1024)
    D = x.shape[1]
    ws = W["od_w_s"][0].astype(BF16)
    wst = jnp.swapaxes(ws, 1, 2)
    bsb = jnp.repeat(jnp.transpose(W["od_b_s"][0]), D // C_GROUPS, axis=1)
    y_sgu = _sgu_fwd(zp, W["od_c_ln_g"], W["od_c_ln_b"], ws, bsb, "od_sgu")
    h4, nq1 = _mm([(y_sgu, od_w_out, "nn")], "od_out", res=h3, rms_g=_row(W["g_xattn"][1]))
    xa_w1, _ = comm.weights("xa1", nq1)
    h5, nf1, xa1 = _xattn_fwd(h4, nq1, mem, W["g_mem"][1], *xa_w1, W["g_ffn"][1], "l1", None)
    ffn_w1, _ = comm.weights("ffn1", nf1)
    (loss_row, dh6, d_gfinal), ff1 = _ffn_fwd(h5, nf1, *ffn_w1, None, "l1", None,
                                              loss=(_row(W["g_final"]), loss_target))
    grads["g_final"] = d_gfinal.reshape(-1)


    dh5, g_ff1 = _ffn_bwd(dh6, ff1, W["g_ffn"][1], *ffn_w1, "l1", lambda dws: comm.grads("ffn1", dws))
    dh4, g_xa1 = _xattn_bwd(dh5, xa1, mem, W["g_xattn"][1], W["g_mem"][1], *xa_w1, "l1",
                            lambda dws: comm.grads("xa1", dws))
    dy_sgu = _mm([(dh4, od_w_out, "nt")], "od_dy", tn=1024)
    d_od_out = _mm_tn(y_sgu, dh4, "od_dwout", tn=1024)
    dzp, d_ws, d_bsb, d_clng, d_clnb = _sgu_bwd(dy_sgu, zp, W["od_c_ln_g"], W["od_c_ln_b"], ws, wst, bsb, "od_sgu_bwd")
    grads["od_w_s"] = d_ws[None]
    grads["od_b_s"] = jnp.transpose(_group_sum(d_bsb, C_GROUPS, "od_dbs"))[None]
    grads["od_c_ln_g"], grads["od_c_ln_b"] = d_clng, d_clnb
    token = comm.grads("od", [_mm_tn(dzp, n3, "od_dwin", ts=1024, tn=1024), d_od_out])
    dh3, d_gmix1 = _mm([(dzp, od_w_in_t, "nn")], "od_dn", rms_bwd=(h3, _row(W["g_mix"][1]), dh4), tm=1024, after=token)

    dh2, g_ff0 = _ffn_bwd(dh3, ff0, W["g_ffn"][0], *ffn_w0, "l0", lambda dws: comm.grads("ffn0", dws))
    dh1, g_xa0 = _xattn_bwd(dh2, xa0, mem, W["g_xattn"][0], W["g_mem"][0], *xa_w0, "l0",
                            lambda dws: comm.grads("xa0", dws))
    token = comm.grads("ev_out", [_mm_tn(ab, dh1, "ev_dwout", tn=1024)])
    dab = _mm([(dh1, ev_w_out, "nt")], "ev_dab", tn=1024, after=token)
    dca, d_lng, d_lnb, d_ba = _conv_bwd_ln(dab, ca, W["ev_a_ln_g"], W["ev_a_ln_b"], "ev_conv_bwd_ln")
    dz, d_wa, d_wb, d_bb = _conv_bwd(z, dca, dab, W["ev_a_conv_w"][0], W["ev_b_conv_w"][0], W["ev_b_conv_b"],
                                     "ev_conv_bwd")
    grads.update(ev_a_ln_g=d_lng, ev_a_ln_b=d_lnb, ev_a_conv_b=d_ba, ev_b_conv_b=d_bb,
                 ev_a_conv_w=d_wa[None], ev_b_conv_w=d_wb[None])
    token = comm.grads("ev_in", [_mm_tn(dz, n0, "ev_dwin", ts=1024, tn=1024)])
    grad_x, d_gmix0 = _mm([(dz, ev_w_in_t, "nn")], "ev_dn", rms_bwd=(h0, _row(W["g_mix"][0]), dh1), tm=1024, after=token)

    grads["g_mix"] = jnp.concatenate([d_gmix0, d_gmix1], axis=0)
    for key in ("g_xattn", "g_mem"):
        grads[key] = jnp.concatenate([g_xa0[key], g_xa1[key]], axis=0)
    grads["g_ffn"] = jnp.concatenate([g_ff0["g_ffn"], g_ff1["g_ffn"]], axis=0)
    return loss_row, grad_x, grads


class _Exchanges:
    def __init__(self, shards, dev_idx, after):
        self.shards, self.dev_idx = shards, dev_idx
        self.gathering, self.scattering = {}, {}
        self.first = _all_gather_weights(self._pack(GATHERS["ev_in"]), self._rows(GATHERS["ev_in"]), "ag_ev_in",
                                         after=after)
        self.first_token = self.prefetch(["xa0"], self.first[0])

    def _rows(self, entries):
        return [self.shards[e].shape[0] for e in entries]

    def _pack(self, entries):
        return jnp.concatenate([self.shards[e] for e in entries], axis=0)

    def prefetch(self, gathers, after):
        for name in gathers:
            rows = self._rows(GATHERS[name])
            pack = self._pack(GATHERS[name])
            lands = [lax.empty((NDEV * r, pack.shape[1]), pack.dtype) for r in rows]
            plan, n = _gather_plan(rows)
            send, recv, srcs, lands, after = _split_start([pack], lands, plan, n, after, f"ag_{name}_start")
            self.gathering[name] = (send, recv, srcs, lands, plan, rows)
        return after

    def weights(self, name, after):
        if name == "ev_in":
            return self.first, self.first_token
        send, recv, srcs, lands, plan, rows = self.gathering.pop(name)
        _, lands = _split_wait(send, recv, srcs, lands, plan, after, f"ag_{name}_wait")
        return _place_own([self.shards[e] for e in GATHERS[name]], lands, self.dev_idx, f"ag_{name}_own"), None

    def grads(self, name, dws):
        rows = self._rows(SCATTERS[name])
        lands = [lax.empty((NDEV - 1, r, d.shape[1]), d.dtype) for r, d in zip(rows, dws)]
        plan, n = _scatter_plan(rows)
        send, recv, srcs, lands, token = _split_start(dws, lands, plan, n, None, f"rs_{name}_start")
        self.scattering[name] = (send, recv, srcs, lands, plan)
        return token

    def received(self, after):
        out = {}
        for name, (send, recv, srcs, lands, plan) in self.scattering.items():
            srcs, lands = _split_wait(send, recv, srcs, lands, plan, after, f"rs_{name}_wait")
            for entry, g, got in zip(SCATTERS[name], srcs, lands):
                out[entry] = (g, got)
        return out


def kernel(x, mem, g_mix, g_xattn, g_mem, g_ffn, g_final, ev_w_in, ev_a_conv_w, ev_a_conv_b, ev_a_ln_g, ev_a_ln_b, ev_b_conv_w, ev_b_conv_b, ev_w_out, od_w_in, od_c_ln_g, od_c_ln_b, od_w_s, od_b_s, od_w_out, xa_w_q, xa_w_k, xa_w_v, xa_w_o, ffn_w_gate, ffn_w_up, ffn_w_down, loss_target, m_g_mix, m_g_xattn, m_g_mem, m_g_ffn, m_g_final, m_ev_w_in, m_ev_a_conv_w, m_ev_a_conv_b, m_ev_a_ln_g, m_ev_a_ln_b, m_ev_b_conv_w, m_ev_b_conv_b, m_ev_w_out, m_od_w_in, m_od_c_ln_g, m_od_c_ln_b, m_od_w_s, m_od_b_s, m_od_w_out, m_xa_w_q, m_xa_w_k, m_xa_w_v, m_xa_w_o, m_ffn_w_gate, m_ffn_w_up, m_ffn_w_down, v_g_mix, v_g_xattn, v_g_mem, v_g_ffn, v_g_final, v_ev_w_in, v_ev_a_conv_w, v_ev_a_conv_b, v_ev_a_ln_g, v_ev_a_ln_b, v_ev_b_conv_w, v_ev_b_conv_b, v_ev_w_out, v_od_w_in, v_od_c_ln_g, v_od_c_ln_b, v_od_w_s, v_od_b_s, v_od_w_out, v_xa_w_q, v_xa_w_k, v_xa_w_v, v_xa_w_o, v_ffn_w_gate, v_ffn_w_up, v_ffn_w_down):
    local = dict(g_mix=g_mix, g_xattn=g_xattn, g_mem=g_mem, g_ffn=g_ffn, g_final=g_final, ev_w_in=ev_w_in, ev_a_conv_w=ev_a_conv_w, ev_a_conv_b=ev_a_conv_b, ev_a_ln_g=ev_a_ln_g, ev_a_ln_b=ev_a_ln_b, ev_b_conv_w=ev_b_conv_w, ev_b_conv_b=ev_b_conv_b, ev_w_out=ev_w_out, od_w_in=od_w_in, od_c_ln_g=od_c_ln_g, od_c_ln_b=od_c_ln_b, od_w_s=od_w_s, od_b_s=od_b_s, od_w_out=od_w_out, xa_w_q=xa_w_q, xa_w_k=xa_w_k, xa_w_v=xa_w_v, xa_w_o=xa_w_o, ffn_w_gate=ffn_w_gate, ffn_w_up=ffn_w_up, ffn_w_down=ffn_w_down)
    mom = dict(g_mix=m_g_mix, g_xattn=m_g_xattn, g_mem=m_g_mem, g_ffn=m_g_ffn, g_final=m_g_final, ev_w_in=m_ev_w_in, ev_a_conv_w=m_ev_a_conv_w, ev_a_conv_b=m_ev_a_conv_b, ev_a_ln_g=m_ev_a_ln_g, ev_a_ln_b=m_ev_a_ln_b, ev_b_conv_w=m_ev_b_conv_w, ev_b_conv_b=m_ev_b_conv_b, ev_w_out=m_ev_w_out, od_w_in=m_od_w_in, od_c_ln_g=m_od_c_ln_g, od_c_ln_b=m_od_c_ln_b, od_w_s=m_od_w_s, od_b_s=m_od_b_s, od_w_out=m_od_w_out, xa_w_q=m_xa_w_q, xa_w_k=m_xa_w_k, xa_w_v=m_xa_w_v, xa_w_o=m_xa_w_o, ffn_w_gate=m_ffn_w_gate, ffn_w_up=m_ffn_w_up, ffn_w_down=m_ffn_w_down)
    vel = dict(g_mix=v_g_mix, g_xattn=v_g_xattn, g_mem=v_g_mem, g_ffn=v_g_ffn, g_final=v_g_final, ev_w_in=v_ev_w_in, ev_a_conv_w=v_ev_a_conv_w, ev_a_conv_b=v_ev_a_conv_b, ev_a_ln_g=v_ev_a_ln_g, ev_a_ln_b=v_ev_a_ln_b, ev_b_conv_w=v_ev_b_conv_w, ev_b_conv_b=v_ev_b_conv_b, ev_w_out=v_ev_w_out, od_w_in=v_od_w_in, od_c_ln_g=v_od_c_ln_g, od_c_ln_b=v_od_c_ln_b, od_w_s=v_od_w_s, od_b_s=v_od_b_s, od_w_out=v_od_w_out, xa_w_q=v_xa_w_q, xa_w_k=v_xa_w_k, xa_w_v=v_xa_w_v, xa_w_o=v_xa_w_o, ffn_w_gate=v_ffn_w_gate, ffn_w_up=v_ffn_w_up, ffn_w_down=v_ffn_w_down)
    D = x.shape[-1]
    dev = 4 * lax.axis_index("x") + 2 * lax.axis_index("y") + lax.axis_index("c")

    def comm_layout(n, a):
        return jnp.transpose(a) if _shard_axis(n) == 2 else a

    shards = {(n, i): comm_layout(n, local[n][i]).astype(BF16) for n in BIG for i in range(local[n].shape[0])}
    small_sizes = [local[n].size for n in SMALL_SHARDED]
    small_block = _pad_rows(jnp.concatenate([local[n].reshape(-1) for n in SMALL_SHARDED]), 128, 8)
    small_all = _all_gather(small_block, "ag_small")
    comm = _Exchanges(shards, jnp.reshape(dev, (1,)).astype(jnp.int32), small_all)
    small_all = small_all.reshape(NDEV, -1)

    W = {n: local[n] for n in REPLICATED}
    o0 = 0
    for n, sz in zip(SMALL_SHARDED, small_sizes):
        blocks = small_all[:, o0:o0 + sz].reshape((NDEV,) + local[n].shape)
        W[n] = _full_from_blocks(blocks, _shard_axis(n))
        o0 += sz

    loss_row, grad_x, grads = _local_step(x[0], mem[0], loss_target[0], W, comm)

    received = comm.received(grad_x)
    rest = REPLICATED + SMALL_SHARDED
    rest_full_shapes = [grads[n].shape for n in rest]
    g_rest = _pad_rows(jnp.concatenate([grads[n].astype(F32).reshape(-1) for n in rest]), D, 8)
    small_rows = g_rest.shape[0]
    small_plan, small_n = _gather_plan([small_rows])
    small_send, small_recv, small_srcs, small_lands, token = _split_start(
        [g_rest], [lax.empty((NDEV * small_rows, D), F32)], small_plan, small_n, received["ev_w_in", 0][1],
        "ag_small_grads_start")

    gsh, delta, new_m, new_v = {}, {}, {}, {}
    def stacked_layout(n, a):
        return jnp.swapaxes(a, 1, 2) if _shard_axis(n) == 2 else a

    for n in BIG:
        parts = [received[n, i] for i in range(local[n].shape[0])]
        outs = _finish_weight([p[0] for p in parts], [p[1] for p in parts], comm.dev_idx,
                              *(stacked_layout(n, a) for a in (local[n], mom[n], vel[n])), f"finish_{n}", after=token)
        gsh[n], delta[n], new_m[n], new_v[n] = (stacked_layout(n, o) for o in outs)

    _, small_lands = _split_wait(small_send, small_recv, small_srcs, small_lands, small_plan,
                                 [delta[n] for n in BIG], "ag_small_grads_wait")
    partials = _place_own([g_rest], small_lands, comm.dev_idx, "ag_small_grads_own")[0]
    g_rest = _sum_slots(partials.reshape(NDEV, small_rows, D), "sum_small_grads").reshape(-1)
    o0 = 0
    for n, shp in zip(rest, rest_full_shapes):
        sz = 1
        for s in shp:
            sz *= s
        full = g_rest[o0:o0 + sz].reshape(shp)
        o0 += sz
        if n in SMALL_SHARDED:
            full = lax.dynamic_index_in_dim(_blocks_from_full(full, _shard_axis(n)), dev, 0, keepdims=False)
        gsh[n] = full.reshape(local[n].shape)

    small = _adamw_small([local[n] for n in rest], [gsh[n] for n in rest], [mom[n] for n in rest],
                         [vel[n] for n in rest], "adamw_small")
    for n, (d, nm, nv) in zip(rest, small):
        delta[n], new_m[n], new_v[n] = d, nm, nv

    loss = lax.psum(loss_row[0, 0], ("x", "y", "c"))
    return (loss, grad_x[None], *[gsh[n] for n in WEIGHTS], *[delta[n] for n in WEIGHTS],
            *[new_m[n] for n in WEIGHTS], *[new_v[n] for n in WEIGHTS])
```

```python
import jax
import jax.numpy as jnp
from jax import lax
from jax.experimental import pallas as pl
from jax.experimental.pallas import tpu as pltpu

F32, BF16 = jnp.float32, jnp.bfloat16
NDEV = 8
RMS_EPS = 1e-6
LN_EPS = 1e-5
CHUNK = 128
C_GROUPS = 8
XA_HEADS = 4
ADAM_LR, ADAM_B1, ADAM_B2, ADAM_EPS, ADAM_WD, ADAM_STEP = 0.001, 0.9, 0.999, 1e-08, 0.01, 10
HALO = 16
ROW_CHUNK = 32
ROW_CHUNK_FWD = 64
V7X_VMEM_LIMIT = 56 * 1024 * 1024
MESH = pl.DeviceIdType.MESH

TS_MM = 2048
TN_MM = 1408
TS_FFN = 512
MM_ROW_CHUNK = 256
FFN_COL_CHUNK = 256
TS_CONV = 512
TS_SGU = 512
TS_ATTN = 2048


def _cp(*sem):
    return pltpu.CompilerParams(dimension_semantics=sem, vmem_limit_bytes=V7X_VMEM_LIMIT)


def _pick(n, pref, align):
    for t in range(min(n, pref), 0, -1):
        if n % t == 0 and (t % align == 0 or t == n):
            return t
    return n


def _sigmoid(x):
    return 0.5 * jnp.tanh(0.5 * x) + 0.5


def _dot(a, b):
    return jnp.dot(a, b, preferred_element_type=F32)


def _dot_nt(a, b):
    return lax.dot_general(a, b, (((1,), (1,)), ((), ())), preferred_element_type=F32)


def _dot_tn(a, b):
    return lax.dot_general(a, b, (((0,), (0,)), ((), ())), preferred_element_type=F32)


_ANY = pl.BlockSpec(memory_space=pl.ANY)
_RESIDENT = pl.Buffered(1)


def _after(after):
    if after is None:
        return [], []
    ops = list(after) if isinstance(after, (list, tuple)) else [after]
    return [_ANY] * len(ops), ops


def _rms_fwd(h, g, name, after=None):
    S, D = h.shape
    ts = _pick(S, TS_MM, 16)
    after_specs, after_ops = _after(after)

    def body(h_ref, g_ref, *rest):
        o_ref = rest[-1]
        x = h_ref[...]
        r = lax.rsqrt(jnp.mean(x * x, axis=-1, keepdims=True) + RMS_EPS)
        o_ref[...] = ((x * r) * g_ref[...]).astype(o_ref.dtype)

    return pl.pallas_call(
        body, grid=(S // ts,),
        in_specs=[pl.BlockSpec((ts, D), lambda i: (i, 0)), pl.BlockSpec((1, D), lambda i: (0, 0))] + after_specs,
        out_specs=pl.BlockSpec((ts, D), lambda i: (i, 0)),
        out_shape=jax.ShapeDtypeStruct((S, D), BF16), compiler_params=_cp("parallel"), name=name)(h, g, *after_ops)


def _mm(pairs, name, out_dtype=F32, res=None, rms_g=None, rms_bwd=None, loss=None, tm=None, tn=None, after=None):
    M = pairs[0][0].shape[0]
    N = pairs[0][1].shape[1 if pairs[0][2] == "nn" else 0]
    whole_rows = rms_g is not None or rms_bwd is not None or loss is not None
    tm = _pick(M, tm or TS_MM, 16)
    tn = N if whole_rows else _pick(N, tn or TN_MM, 128)
    npair = len(pairs)
    modes = [p[2] for p in pairs]
    after_specs, after_ops = _after(after)

    rc = MM_ROW_CHUNK if whole_rows and tm % MM_ROW_CHUNK == 0 else tm

    def body(*refs):
        rest = refs[2 * npair + len(after_ops):]
        res_ref = None
        if res is not None:
            res_ref, rest = rest[0], rest[1:]
        if rms_bwd is not None:
            dg_ref = rest[4]

            @pl.when(pl.program_id(0) == 0)
            def _():
                dg_ref[...] = jnp.zeros_like(dg_ref)

        if loss is not None:
            g_ref, t_ref, loss_ref, dh_ref, dg_ref = rest

            @pl.when(pl.program_id(0) == 0)
            def _():
                dg_ref[...] = jnp.zeros_like(dg_ref)
                loss_ref[...] = jnp.zeros_like(loss_ref)

        for r0 in range(0, tm, rc):
            rows = pl.ds(r0, rc)
            acc = None
            for p in range(npair):
                a_ = refs[2 * p][rows, :].astype(BF16)
                d = _dot(a_, refs[2 * p + 1][...]) if modes[p] == "nn" else _dot_nt(a_, refs[2 * p + 1][...])
                acc = d if acc is None else acc + d
            if res_ref is not None:
                acc = acc + res_ref[rows, :]
            if rms_bwd is not None:
                h_ref, g_ref, dres_ref, dh_ref, _ = rest
                x = h_ref[rows, :]
                r = lax.rsqrt(jnp.mean(x * x, axis=-1, keepdims=True) + RMS_EPS)
                xr = x * r
                dg_ref[...] += jnp.sum(acc * xr, axis=0, keepdims=True)
                u = acc * g_ref[...]
                dh_ref[rows, :] = r * u - xr * (r * jnp.mean(u * xr, axis=-1, keepdims=True)) + dres_ref[rows, :]
            elif loss is not None:
                r = lax.rsqrt(jnp.mean(acc * acc, axis=-1, keepdims=True) + RMS_EPS)
                xr = acc * r
                gg = g_ref[...]
                e = xr * gg - t_ref[rows, :]
                chunk_loss = jnp.sum(jnp.sum(e * e, axis=0, keepdims=True), axis=1, keepdims=True) * (0.5 / N)
                loss_ref[...] += jnp.broadcast_to(chunk_loss, loss_ref.shape)
                dy = e * (1.0 / N)
                dg_ref[...] += jnp.sum(dy * xr, axis=0, keepdims=True)
                u = dy * gg
                dh_ref[rows, :] = r * u - xr * (r * jnp.mean(u * xr, axis=-1, keepdims=True))
            elif rms_g is not None:
                g_ref, o_ref, n_ref = rest
                o_ref[rows, :] = acc
                r = lax.rsqrt(jnp.mean(acc * acc, axis=-1, keepdims=True) + RMS_EPS)
                n_ref[rows, :] = ((acc * r) * g_ref[...]).astype(BF16)
            else:
                rest[0][rows, :] = acc.astype(rest[0].dtype)

    in_specs, ins = [], []
    for a, w, mode in pairs:
        K = a.shape[1]
        in_specs.append(pl.BlockSpec((tm, K), lambda i, j: (i, 0)))
        once = _RESIDENT if tn == N else None
        in_specs.append(pl.BlockSpec((K, tn), lambda i, j: (0, j), pipeline_mode=once) if mode == "nn"
                        else pl.BlockSpec((tn, K), lambda i, j: (j, 0), pipeline_mode=once))
        ins += [a, w]
    in_specs += after_specs
    ins += after_ops
    tile = pl.BlockSpec((tm, tn), lambda i, j: (i, j))
    vec = pl.BlockSpec((1, tn), lambda i, j: (0, j))
    if res is not None:
        in_specs.append(tile)
        ins.append(res)
    sem = ("parallel", "parallel")
    if rms_bwd is not None:
        in_specs += [tile, vec, tile]
        ins += list(rms_bwd)
        out_specs = [tile, vec]
        out_shape = [jax.ShapeDtypeStruct((M, N), F32), jax.ShapeDtypeStruct((1, N), F32)]
        sem = ("arbitrary", "arbitrary")
    elif loss is not None:
        in_specs += [vec, tile]
        ins += list(loss)
        out_specs = [pl.BlockSpec((1, 128), lambda i, j: (0, 0)), tile, vec]
        out_shape = [jax.ShapeDtypeStruct((1, 128), F32), jax.ShapeDtypeStruct((M, N), F32),
                     jax.ShapeDtypeStruct((1, N), F32)]
        sem = ("arbitrary", "arbitrary")
    elif rms_g is not None:
        in_specs.append(vec)
        ins.append(rms_g)
        out_specs = [tile, tile]
        out_shape = [jax.ShapeDtypeStruct((M, N), F32), jax.ShapeDtypeStruct((M, N), BF16)]
    else:
        out_specs = tile
        out_shape = jax.ShapeDtypeStruct((M, N), out_dtype)
    return pl.pallas_call(
        body, grid=(M // tm, N // tn), in_specs=in_specs, out_specs=out_specs, out_shape=out_shape,
        compiler_params=_cp(*sem), name=name)(*ins)


def _mm_tn(a, b, name, ts=None, tn=None):
    S, K = a.shape
    N = b.shape[1]
    ts = _pick(S, ts or TS_MM, 16)
    tn = _pick(N, tn or TN_MM, 128)
    nsteps = S // ts

    def body(a_ref, b_ref, o_ref, acc_ref):
        s = pl.program_id(1)

        @pl.when(s == 0)
        def _():
            acc_ref[...] = jnp.zeros_like(acc_ref)

        acc_ref[...] += _dot_tn(a_ref[...].astype(BF16), b_ref[...].astype(BF16))

        @pl.when(s == nsteps - 1)
        def _():
            o_ref[...] = acc_ref[...].astype(o_ref.dtype)

    return pl.pallas_call(
        body, grid=(N // tn, nsteps),
        in_specs=[pl.BlockSpec((ts, K), lambda j, s: (s, 0)), pl.BlockSpec((ts, tn), lambda j, s: (s, j))],
        out_specs=pl.BlockSpec((K, tn), lambda j, s: (0, j)), out_shape=jax.ShapeDtypeStruct((K, N), BF16),
        scratch_shapes=[pltpu.VMEM((K, tn), F32)],
        compiler_params=_cp("parallel", "arbitrary"), name=name)(a, b)


def _col_chunks(n):
    return [(c0, min(FFN_COL_CHUNK, n - c0)) for c0 in range(0, n, FFN_COL_CHUNK)]


def _ffn_up(n, wgt, wut, name, after=None):
    S, D = n.shape
    F = wgt.shape[0]
    tm = _pick(S, TS_FFN, 16)
    after_specs, after_ops = _after(after)

    def body(n_ref, wg_ref, wu_ref, *rest):
        a_ref, b_ref, hid_ref = rest[-3:]
        x = n_ref[...]
        for c0, ce in _col_chunks(F):
            a = _dot_nt(x, wg_ref[c0:c0 + ce, :])
            b = _dot_nt(x, wu_ref[c0:c0 + ce, :])
            a_ref[:, c0:c0 + ce] = a.astype(BF16)
            b_ref[:, c0:c0 + ce] = b.astype(BF16)
            hid_ref[:, c0:c0 + ce] = (a * _sigmoid(a) * b).astype(BF16)

    wspec = pl.BlockSpec((F, D), lambda i: (0, 0), pipeline_mode=_RESIDENT)
    ospec = pl.BlockSpec((tm, F), lambda i: (i, 0))
    osh = jax.ShapeDtypeStruct((S, F), BF16)
    return pl.pallas_call(
        body, grid=(S // tm,),
        in_specs=[pl.BlockSpec((tm, D), lambda i: (i, 0)), wspec, wspec] + after_specs,
        out_specs=[ospec, ospec, ospec], out_shape=[osh, osh, osh],
        compiler_params=_cp("parallel"), name=name)(n, wgt, wut, *after_ops)


def _ffn_dhid(dh, wd, a, b, name):
    S, D = dh.shape
    F = wd.shape[0]
    tm = _pick(S, TS_FFN, 16)

    def body(dh_ref, wd_ref, a_ref, b_ref, da_ref, db_ref):
        x = dh_ref[...].astype(BF16)
        for c0, ce in _col_chunks(F):
            g = _dot_nt(x, wd_ref[c0:c0 + ce, :]).astype(BF16)
            a_ = a_ref[:, c0:c0 + ce]
            sg = _sigmoid(a_)
            silu = a_ * sg
            da_ref[:, c0:c0 + ce] = (g * b_ref[:, c0:c0 + ce]) * (sg + silu * (1.0 - sg))
            db_ref[:, c0:c0 + ce] = g * silu

    tile = pl.BlockSpec((tm, F), lambda i: (i, 0))
    osh = jax.ShapeDtypeStruct((S, F), BF16)
    return pl.pallas_call(
        body, grid=(S // tm,),
        in_specs=[pl.BlockSpec((tm, D), lambda i: (i, 0)),
                  pl.BlockSpec((F, D), lambda i: (0, 0), pipeline_mode=_RESIDENT), tile, tile],
        out_specs=[tile, tile], out_shape=[osh, osh],
        compiler_params=_cp("parallel"), name=name)(dh, wd, a, b)


def _softmax_rows(s):
    m = jnp.max(s, axis=-1, keepdims=True)
    p = jnp.exp(s - m)
    return p / jnp.sum(p, axis=-1, keepdims=True)


def _attn_fwd(q, k, v, name):
    S, D = q.shape
    M = k.shape[0]
    hd = D // XA_HEADS
    scale = hd ** -0.5
    ts = _pick(S, TS_ATTN, 16)

    def body(q_ref, k_ref, v_ref, o_ref):
        for h in range(XA_HEADS):
            sl = slice(h * hd, (h + 1) * hd)
            p = _softmax_rows(_dot_nt(q_ref[:, sl], k_ref[:, sl]) * scale)
            o_ref[:, sl] = _dot(p.astype(BF16), v_ref[:, sl]).astype(BF16)

    tile = pl.BlockSpec((ts, D), lambda i: (i, 0))
    memspec = pl.BlockSpec((M, D), lambda i: (0, 0))
    return pl.pallas_call(
        body, grid=(S // ts,), in_specs=[tile, memspec, memspec], out_specs=tile,
        out_shape=jax.ShapeDtypeStruct((S, D), BF16), compiler_params=_cp("parallel"), name=name)(q, k, v)


def _attn_bwd(q, k, v, do, name):
    S, D = q.shape
    M = k.shape[0]
    hd = D // XA_HEADS
    scale = hd ** -0.5
    ts = _pick(S, TS_ATTN, 16)

    def body(q_ref, k_ref, v_ref, do_ref, dq_ref, dk_ref, dv_ref):
        @pl.when(pl.program_id(0) == 0)
        def _():
            dk_ref[...] = jnp.zeros_like(dk_ref)
            dv_ref[...] = jnp.zeros_like(dv_ref)

        for h in range(XA_HEADS):
            sl = slice(h * hd, (h + 1) * hd)
            qh, kh, vh, doh = q_ref[:, sl], k_ref[:, sl], v_ref[:, sl], do_ref[:, sl]
            p = _softmax_rows(_dot_nt(qh, kh) * scale)
            dp = _dot_nt(doh, vh)
            dv_ref[:, sl] += _dot_tn(p.astype(BF16), doh)
            delta = jnp.sum(dp * p, axis=-1, keepdims=True)
            ds = (p * (dp - delta) * scale).astype(BF16)
            dq_ref[:, sl] = _dot(ds, kh).astype(BF16)
            dk_ref[:, sl] += _dot_tn(ds, qh)

    tile = pl.BlockSpec((ts, D), lambda i: (i, 0))
    memspec = pl.BlockSpec((M, D), lambda i: (0, 0))
    return pl.pallas_call(
        body, grid=(S // ts,), in_specs=[tile, memspec, memspec, tile], out_specs=[tile, memspec, memspec],
        out_shape=[jax.ShapeDtypeStruct((S, D), BF16), jax.ShapeDtypeStruct((M, D), F32),
                   jax.ShapeDtypeStruct((M, D), F32)],
        compiler_params=_cp("arbitrary"), name=name)(q, k, v, do)


def _halo_specs(ts, col):
    per = ts // HALO

    def prev(i):
        return (jnp.maximum(i * per - 1, 0), col)

    def nxt(i, n_tiles):
        return (jnp.minimum((i + 1) * per, n_tiles * per - 1), col)

    return prev, nxt


def _fill_ext(ext_ref, prev_val, main_val, next_val, first, last, ts):
    ext_ref[pl.ds(0, HALO), :] = jnp.where(first, 0.0, prev_val)
    ext_ref[pl.ds(HALO, ts), :] = main_val
    ext_ref[pl.ds(HALO + ts, HALO), :] = jnp.where(last, 0.0, next_val)


SUBLANES = 8


def _fill_shifted(sh_ref, ts):
    n = ts + 2 * HALO - SUBLANES
    for s in range(1, SUBLANES):
        sh_ref[s, pl.ds(0, n), :] = sh_ref[0, pl.ds(s, n), :]


def _tap(sh_ref, r0, offset, rc):
    q, s = divmod(offset, SUBLANES)
    return sh_ref[s, pl.ds(pl.multiple_of(r0 + SUBLANES * q, SUBLANES), rc), :]


def _conv_fwd(z, wa, ba, lng, lnb, wb, bb, name, after=None):
    S = z.shape[0]
    C = z.shape[1] // 5
    KA, KB = wa.shape[0], wb.shape[0]
    pa, pb = KA // 2, KB // 2
    assert pa <= HALO and pb <= HALO
    ts = _pick(S, TS_CONV, ROW_CHUNK_FWD)
    nt = S // ts
    rc = ROW_CHUNK_FWD
    prev, nxt = _halo_specs(ts, 0)
    after_specs, after_ops = _after(after)

    def body(*refs):
        compute(*refs[:9], *refs[9 + len(after_ops):])

    def compute(z_ref, zp_ref, zn_ref, wa_ref, ba_ref, lng_ref, lnb_ref, wb_ref, bb_ref, ab_ref, ca_ref,
                ga_sh, tb_ext, win_b):
        i = pl.program_id(0)
        first, last = i == 0, i == nt - 1

        def glu(r):
            return r[:, 0:C] * _sigmoid(r[:, C:2 * C])

        def gcb(r):
            return r[:, 4 * C:5 * C] * r[:, 2 * C:3 * C]

        _fill_ext(ga_sh.at[0], glu(zp_ref), glu(z_ref), glu(zn_ref), first, last, ts)
        _fill_shifted(ga_sh, ts)
        _fill_ext(tb_ext, gcb(zp_ref), gcb(z_ref), gcb(zn_ref), first, last, ts)

        def chunk(c, carry):
            r0 = pl.multiple_of(c * rc, rc)
            win_b[...] = tb_ext[pl.ds(r0, rc + 2 * HALO), :]
            acc = jnp.zeros((rc, C), F32)
            for k in range(KA):
                acc = acc + wa_ref[k:k + 1, :] * _tap(ga_sh, r0, HALO - pa + k, rc)
            ca = acc + ba_ref[...]
            ca_ref[pl.ds(r0, rc), :] = ca
            mu = jnp.mean(ca, axis=-1, keepdims=True)
            xc = ca - mu
            var = jnp.mean(xc * xc, axis=-1, keepdims=True)
            ln = xc * lax.rsqrt(var + LN_EPS) * lng_ref[...] + lnb_ref[...]
            ab_ref[pl.ds(r0, rc), 0:C] = (ln * _sigmoid(ln)).astype(BF16)
            cb = jnp.zeros((rc, C), F32) + bb_ref[...]
            for k in range(KB):
                cb = cb + wb_ref[k:k + 1, :] * win_b[pl.ds(HALO - pb + k, rc), :]
            ab_ref[pl.ds(r0, rc), C:2 * C] = (z_ref[pl.ds(r0, rc), 3 * C:4 * C] * cb).astype(BF16)
            return carry

        lax.fori_loop(0, ts // rc, chunk, 0)

    zspec = pl.BlockSpec((ts, 5 * C), lambda i: (i, 0))
    zprev = pl.BlockSpec((HALO, 5 * C), prev)
    znext = pl.BlockSpec((HALO, 5 * C), lambda i: nxt(i, nt))

    def full(a):
        return pl.BlockSpec(a.shape, lambda i: (0, 0))

    return pl.pallas_call(
        body, grid=(nt,),
        in_specs=[zspec, zprev, znext, full(wa), full(ba), full(lng), full(lnb), full(wb), full(bb)] + after_specs,
        out_specs=[pl.BlockSpec((ts, 2 * C), lambda i: (i, 0)), pl.BlockSpec((ts, C), lambda i: (i, 0))],
        out_shape=[jax.ShapeDtypeStruct((S, 2 * C), BF16), jax.ShapeDtypeStruct((S, C), F32)],
        scratch_shapes=[pltpu.VMEM((SUBLANES, ts + 2 * HALO, C), F32), pltpu.VMEM((ts + 2 * HALO, C), F32),
                        pltpu.VMEM((rc + 2 * HALO, C), F32)],
        compiler_params=_cp("parallel"), name=name)(z, z, z, wa, ba, lng, lnb, wb, bb, *after_ops)


def _conv_dab_ln(dh, w_out, ca, lng, lnb, name, after=None):
    S, D = dh.shape
    C = ca.shape[1]
    tm = _pick(S, TS_MM // 2, 16)
    rc = MM_ROW_CHUNK if tm % MM_ROW_CHUNK == 0 else tm
    after_specs, after_ops = _after(after)

    def body(dh_ref, w_ref, ca_ref, lng_ref, lnb_ref, *rest):
        dca_ref, dbo_ref, dg_ref, db_ref, dbias_ref = rest[-5:]

        @pl.when(pl.program_id(0) == 0)
        def _():
            dg_ref[...] = jnp.zeros_like(dg_ref)
            db_ref[...] = jnp.zeros_like(db_ref)
            dbias_ref[...] = jnp.zeros_like(dbias_ref)

        for r0 in range(0, tm, rc):
            rows = pl.ds(r0, rc)
            dab = _dot_nt(dh_ref[rows, :].astype(BF16), w_ref[...])
            dbo_ref[rows, :] = dab[:, C:2 * C]
            ca_ = ca_ref[rows, :]
            mu = jnp.mean(ca_, axis=-1, keepdims=True)
            xc = ca_ - mu
            rstd = lax.rsqrt(jnp.mean(xc * xc, axis=-1, keepdims=True) + LN_EPS)
            xh = xc * rstd
            ln = xh * lng_ref[...] + lnb_ref[...]
            sg = _sigmoid(ln)
            dln = dab[:, 0:C] * (sg * (1.0 + ln * (1.0 - sg)))
            dg_ref[...] += jnp.sum(dln * xh, axis=0, keepdims=True)
            db_ref[...] += jnp.sum(dln, axis=0, keepdims=True)
            dxh = dln * lng_ref[...]
            dca = rstd * (dxh - jnp.mean(dxh, axis=-1, keepdims=True) - xh * jnp.mean(dxh * xh, axis=-1, keepdims=True))
            dca_ref[rows, :] = dca
            dbias_ref[...] += jnp.sum(dca, axis=0, keepdims=True)

    tile = pl.BlockSpec((tm, C), lambda i: (i, 0))
    vec = pl.BlockSpec((1, C), lambda i: (0, 0))
    vsh = jax.ShapeDtypeStruct((1, C), F32)
    return pl.pallas_call(
        body, grid=(S // tm,),
        in_specs=[pl.BlockSpec((tm, D), lambda i: (i, 0)),
                  pl.BlockSpec(w_out.shape, lambda i: (0, 0), pipeline_mode=_RESIDENT), tile, vec, vec] + after_specs,
        out_specs=[tile, tile, vec, vec, vec],
        out_shape=[jax.ShapeDtypeStruct((S, C), F32), jax.ShapeDtypeStruct((S, C), F32), vsh, vsh, vsh],
        compiler_params=_cp("arbitrary"), name=name)(dh, w_out, ca, lng, lnb, *after_ops)


def _conv_bwd(z, dca, db, wa, wb, bb, name):
    S = z.shape[0]
    C = z.shape[1] // 5
    KA, KB = wa.shape[0], wb.shape[0]
    pa, pb = KA // 2, KB // 2
    ts = _pick(S, TS_CONV, ROW_CHUNK)
    nt = S // ts
    rc = ROW_CHUNK
    prev0, nxt0 = _halo_specs(ts, 0)

    def body(z_ref, zp_ref, zn_ref, dca_ref, dcap_ref, dcan_ref, db_ref, dbp_ref, dbn_ref, wa_ref, wb_ref, bb_ref,
             dz_ref, dwa_ref, dwb_ref, dbb_ref,
             ga_sh, dca_sh, tb_ext, dcb_ext, win_tb, win_dcb, acc_a, acc_b, acc_bias):
        i = pl.program_id(0)
        first, last = i == 0, i == nt - 1

        @pl.when(first)
        def _():
            acc_a[...] = jnp.zeros_like(acc_a)
            acc_b[...] = jnp.zeros_like(acc_b)
            acc_bias[...] = jnp.zeros_like(acc_bias)

        def glu(r):
            return r[:, 0:C] * _sigmoid(r[:, C:2 * C])

        def gcb(r):
            return r[:, 4 * C:5 * C] * r[:, 2 * C:3 * C]

        def dcb(d, r):
            return d[...].astype(F32) * r[:, 3 * C:4 * C]

        _fill_ext(ga_sh.at[0], glu(zp_ref), glu(z_ref), glu(zn_ref), first, last, ts)
        _fill_shifted(ga_sh, ts)
        _fill_ext(dca_sh.at[0], dcap_ref[...], dca_ref[...], dcan_ref[...], first, last, ts)
        _fill_shifted(dca_sh, ts)
        _fill_ext(tb_ext, gcb(zp_ref), gcb(z_ref), gcb(zn_ref), first, last, ts)
        _fill_ext(dcb_ext, dcb(dbp_ref, zp_ref), dcb(db_ref, z_ref), dcb(dbn_ref, zn_ref), first, last, ts)

        def fold(x):
            return jnp.sum(x.reshape(rc // 8, 8, C), axis=0)

        def chunk(c, carry):
            r0 = pl.multiple_of(c * rc, rc)
            win_tb[...] = tb_ext[pl.ds(r0, rc + 2 * HALO), :]
            win_dcb[...] = dcb_ext[pl.ds(r0, rc + 2 * HALO), :]
            dca_c = _tap(dca_sh, r0, HALO, rc)
            dglu = jnp.zeros((rc, C), F32)
            for k in range(KA):
                dglu = dglu + wa_ref[k:k + 1, :] * _tap(dca_sh, r0, HALO + pa - k, rc)
                acc_a[k] += fold(dca_c * _tap(ga_sh, r0, HALO - pa + k, rc))
            val = z_ref[pl.ds(r0, rc), 0:C]
            sg = _sigmoid(z_ref[pl.ds(r0, rc), C:2 * C])
            dz_ref[pl.ds(r0, rc), 0:C] = (dglu * sg).astype(BF16)
            dz_ref[pl.ds(r0, rc), C:2 * C] = (dglu * val * sg * (1.0 - sg)).astype(BF16)
            dcb_c = win_dcb[pl.ds(HALO, rc), :]
            cb = jnp.zeros((rc, C), F32) + bb_ref[...]
            dt = jnp.zeros((rc, C), F32)
            for k in range(KB):
                tb_k = win_tb[pl.ds(HALO - pb + k, rc), :]
                cb = cb + wb_ref[k:k + 1, :] * tb_k
                dt = dt + wb_ref[k:k + 1, :] * win_dcb[pl.ds(HALO + pb - k, rc), :]
                acc_b[k] += fold(dcb_c * tb_k)
            acc_bias[...] += fold(dcb_c)
            db_c = db_ref[pl.ds(r0, rc), :].astype(F32)
            dz_ref[pl.ds(r0, rc), 2 * C:3 * C] = (dt * z_ref[pl.ds(r0, rc), 4 * C:5 * C]).astype(BF16)
            dz_ref[pl.ds(r0, rc), 3 * C:4 * C] = (db_c * cb).astype(BF16)
            dz_ref[pl.ds(r0, rc), 4 * C:5 * C] = (dt * z_ref[pl.ds(r0, rc), 2 * C:3 * C]).astype(BF16)
            return carry

        lax.fori_loop(0, ts // rc, chunk, 0)

        @pl.when(last)
        def _():
            dwa_ref[...] = jnp.sum(acc_a[...], axis=1)
            dwb_ref[...] = jnp.sum(acc_b[...], axis=1)
            dbb_ref[...] = jnp.sum(acc_bias[...], axis=0, keepdims=True)

    zspec = pl.BlockSpec((ts, 5 * C), lambda i: (i, 0))
    zprev = pl.BlockSpec((HALO, 5 * C), prev0)
    znext = pl.BlockSpec((HALO, 5 * C), lambda i: nxt0(i, nt))
    dspec = pl.BlockSpec((ts, C), lambda i: (i, 0))
    dprev = pl.BlockSpec((HALO, C), prev0)
    dnext = pl.BlockSpec((HALO, C), lambda i: nxt0(i, nt))

    def full(shape):
        return pl.BlockSpec(shape, lambda i: (0,) * len(shape))

    ext = pltpu.VMEM((ts + 2 * HALO, C), F32)
    shifted = pltpu.VMEM((SUBLANES, ts + 2 * HALO, C), F32)
    win = pltpu.VMEM((rc + 2 * HALO, C), F32)
    return pl.pallas_call(
        body, grid=(nt,),
        in_specs=[zspec, zprev, znext, dspec, dprev, dnext, dspec, dprev, dnext,
                  full(wa.shape), full(wb.shape), full(bb.shape)],
        out_specs=[pl.BlockSpec((ts, 5 * C), lambda i: (i, 0)), full((KA, C)), full((KB, C)), full((1, C))],
        out_shape=[jax.ShapeDtypeStruct((S, 5 * C), BF16), jax.ShapeDtypeStruct((KA, C), F32),
                   jax.ShapeDtypeStruct((KB, C), F32), jax.ShapeDtypeStruct((1, C), F32)],
        scratch_shapes=[shifted, shifted, ext, ext, win, win,
                        pltpu.VMEM((KA, 8, C), F32), pltpu.VMEM((KB, 8, C), F32), pltpu.VMEM((8, C), F32)],
        compiler_params=_cp("arbitrary"), name=name)(z, z, z, dca, dca, dca, db, db, db, wa, wb, bb)


_GELU_C = 0.7978845608028654
_GELU_A = 0.044715


def _gelu(x):
    return 0.5 * x * (1.0 + jnp.tanh(_GELU_C * (x + _GELU_A * (x * x * x))))


def _gelu_and_grad(x):
    t = jnp.tanh(_GELU_C * (x + _GELU_A * (x * x * x)))
    hx = 0.5 * x
    return hx * (1.0 + t), 0.5 * (1.0 + t) + hx * (1.0 - t * t) * (_GELU_C * (1.0 + 3.0 * _GELU_A * x * x))


def _sgu_fwd(zp, lng, lnb, ws, bsb, name):
    S = zp.shape[0]
    D = zp.shape[1] // 2
    G = ws.shape[0]
    gd = D // G
    ts = _pick(S, TS_SGU, CHUNK)
    ncs = ts // CHUNK

    def body(zp_ref, lng_ref, lnb_ref, ws_ref, bsb_ref, y_ref, vb_ref):
        v = _gelu(zp_ref[:, D:2 * D])
        mu = jnp.mean(v, axis=-1, keepdims=True)
        xc = v - mu
        rstd = lax.rsqrt(jnp.mean(xc * xc, axis=-1, keepdims=True) + LN_EPS)
        vb_ref[...] = (xc * rstd * lng_ref[...] + lnb_ref[...]).astype(BF16)
        for c in range(ncs):
            rows = slice(c * CHUNK, (c + 1) * CHUNK)
            for g in range(G):
                cols = slice(g * gd, (g + 1) * gd)
                sv = _dot(ws_ref[g], vb_ref[rows, cols]) + bsb_ref[:, cols]
                y_ref[rows, cols] = (_gelu(zp_ref[rows, cols]) * sv).astype(BF16)

    def full(a):
        return pl.BlockSpec(a.shape, lambda i: (0,) * a.ndim)

    return pl.pallas_call(
        body, grid=(S // ts,),
        in_specs=[pl.BlockSpec((ts, 2 * D), lambda i: (i, 0)), full(lng), full(lnb), full(ws), full(bsb)],
        out_specs=pl.BlockSpec((ts, D), lambda i: (i, 0)), out_shape=jax.ShapeDtypeStruct((S, D), BF16),
        scratch_shapes=[pltpu.VMEM((ts, D), BF16)],
        compiler_params=_cp("parallel"), name=name)(zp, lng, lnb, ws, bsb)


def _sgu_bwd(dy, zp, lng, lnb, ws, wst, bsb, name):
    S = zp.shape[0]
    D = zp.shape[1] // 2
    G = ws.shape[0]
    gd = D // G
    ts = _pick(S, TS_SGU, CHUNK)
    ncs = ts // CHUNK

    def body(dy_ref, zp_ref, lng_ref, lnb_ref, ws_ref, wst_ref, bsb_ref,
             dzp_ref, dws_ref, dbs_ref, dg_ref, db_ref, vb_ref, dvln_ref, acc_bs):
        i = pl.program_id(0)

        @pl.when(i == 0)
        def _():
            dws_ref[...] = jnp.zeros_like(dws_ref)
            acc_bs[...] = jnp.zeros_like(acc_bs)
            dg_ref[...] = jnp.zeros_like(dg_ref)
            db_ref[...] = jnp.zeros_like(db_ref)

        v, dv_dz = _gelu_and_grad(zp_ref[:, D:2 * D])
        mu = jnp.mean(v, axis=-1, keepdims=True)
        xc = v - mu
        rstd = lax.rsqrt(jnp.mean(xc * xc, axis=-1, keepdims=True) + LN_EPS)
        xh = xc * rstd
        vb_ref[...] = (xh * lng_ref[...] + lnb_ref[...]).astype(BF16)
        for c in range(ncs):
            rows = slice(c * CHUNK, (c + 1) * CHUNK)
            for g in range(G):
                cols = slice(g * gd, (g + 1) * gd)
                u, du_dz = _gelu_and_grad(zp_ref[rows, cols])
                dy_ = dy_ref[rows, cols].astype(F32)
                sv = _dot(ws_ref[g], vb_ref[rows, cols]) + bsb_ref[:, cols]
                dzp_ref[rows, cols] = (dy_ * sv * du_dz).astype(BF16)
                dsv = dy_ * u
                acc_bs[:, cols] += dsv
                dsvb = dsv.astype(BF16)
                dws_ref[g] += _dot_nt(dsvb, vb_ref[rows, cols])
                dvln_ref[rows, cols] = _dot(wst_ref[g], dsvb)
        dvln = dvln_ref[...]
        dg_ref[...] += jnp.sum(dvln * xh, axis=0, keepdims=True)
        db_ref[...] += jnp.sum(dvln, axis=0, keepdims=True)
        dxh = dvln * lng_ref[...]
        dv = rstd * (dxh - jnp.mean(dxh, axis=-1, keepdims=True) - xh * jnp.mean(dxh * xh, axis=-1, keepdims=True))
        dzp_ref[:, D:2 * D] = (dv * dv_dz).astype(BF16)

        @pl.when(i == pl.num_programs(0) - 1)
        def _():
            dbs_ref[...] = acc_bs[...]

    def full(shape):
        return pl.BlockSpec(shape, lambda i: (0,) * len(shape))

    return pl.pallas_call(
        body, grid=(S // ts,),
        in_specs=[pl.BlockSpec((ts, D), lambda i: (i, 0)), pl.BlockSpec((ts, 2 * D), lambda i: (i, 0)),
                  full(lng.shape), full(lnb.shape), full(ws.shape), full(wst.shape), full(bsb.shape)],
        out_specs=[pl.BlockSpec((ts, 2 * D), lambda i: (i, 0)), full(ws.shape), full(bsb.shape),
                   full((1, D)), full((1, D))],
        out_shape=[jax.ShapeDtypeStruct((S, 2 * D), BF16), jax.ShapeDtypeStruct(ws.shape, F32),
                   jax.ShapeDtypeStruct(bsb.shape, F32), jax.ShapeDtypeStruct((1, D), F32),
                   jax.ShapeDtypeStruct((1, D), F32)],
        scratch_shapes=[pltpu.VMEM((ts, D), BF16), pltpu.VMEM((ts, D), F32),
                        pltpu.VMEM(bsb.shape, F32)],
        compiler_params=_cp("arbitrary"), name=name)(dy, zp, lng, lnb, ws, wst, bsb)


def _group_sum(x, groups, name):
    P, D = x.shape
    gd = D // groups

    def body(x_ref, o_ref):
        for g in range(groups):
            o_ref[:, g:g + 1] = jnp.sum(x_ref[:, g * gd:(g + 1) * gd], axis=1, keepdims=True)

    return pl.pallas_call(body, out_shape=jax.ShapeDtypeStruct((P, groups), F32), name=name)(x)


def _adamw_small(ws, gs, ms, vs, name):
    n = len(ws)
    shapes = [w.shape for w in ws]
    flat = [(w.size // w.shape[-1], w.shape[-1]) for w in ws]

    def body(*refs):
        for k in range(n):
            w_ref, g_ref, m_ref, v_ref = (refs[j * n + k] for j in range(4))
            d_ref, nm_ref, nv_ref = (refs[(4 + j) * n + k] for j in range(3))
            d_ref[...], nm_ref[...], nv_ref[...] = _adamw_math(w_ref[...], g_ref[...], m_ref[...], v_ref[...])

    outs = pl.pallas_call(
        body, out_shape=[jax.ShapeDtypeStruct(f, F32) for f in flat] * 3, name=name)(
            *(a.reshape(f) for group in (ws, gs, ms, vs) for a, f in zip(group, flat)))
    return [tuple(outs[j * n + k].reshape(shapes[k]) for j in range(3)) for k in range(n)]


_HBM = pl.BlockSpec(memory_space=pltpu.HBM)


def _remote(src, dst, send_sem, recv_sem, to):
    return pltpu.make_async_remote_copy(src_ref=src, dst_ref=dst, send_sem=send_sem, recv_sem=recv_sem,
                                        device_id=to, device_id_type=MESH)


def _all_gather(block, name):
    R, C = block.shape

    def body(x_ref, out_ref, send_sems, recv_sems, local_sem):
        x, y, c = lax.axis_index("x"), lax.axis_index("y"), lax.axis_index("c")
        me, sibling = (x, y, c), (x, y, 1 - c)
        chips = [(1 - x, y), (x, 1 - y), (1 - x, 1 - y)]

        def slot(px, py, pc):
            return out_ref.at[4 * px + 2 * py + pc]

        def copy(k, blk, to, src=None):
            return _remote(slot(*blk) if src is None else src, slot(*blk), send_sems.at[k], recv_sems.at[k], to)

        mine = pltpu.make_async_copy(x_ref, slot(*me), local_sem)
        mine.start()
        first = [copy(0, me, sibling, src=x_ref)]
        first += [copy(1 + j, me, (*chip, c), src=x_ref) for j, chip in enumerate(chips)]
        for cp in first:
            cp.start()
        passed = [copy(4 + j, (*chip, c), sibling) for j, chip in enumerate(chips)]
        for j, chip in enumerate(chips):
            copy(1 + j, (*chip, c), me).wait_recv()
            passed[j].start()
        copy(0, sibling, me).wait_recv()
        for j, chip in enumerate(chips):
            copy(4 + j, (*chip, 1 - c), me).wait_recv()
        for cp in first + passed:
            cp.wait_send()
        mine.wait()

    return pl.pallas_call(
        body, out_shape=jax.ShapeDtypeStruct((NDEV, R, C), block.dtype), in_specs=[_HBM], out_specs=_HBM,
        scratch_shapes=[pltpu.SemaphoreType.DMA((7,)), pltpu.SemaphoreType.DMA((7,)), pltpu.SemaphoreType.DMA],
        name=name)(block)


def _all_gather_weights(pack, rows, name, after=None):
    C = pack.shape[1]
    nw = len(rows)
    starts = [sum(rows[:w]) for w in range(nw)]
    after_specs, after_ops = _after(after)

    def body(pack_ref, *rest):
        rest = rest[len(after_ops):]
        outs = rest[:nw]
        send_sems, recv_sems, local_sem = rest[nw:]
        x, y, c = lax.axis_index("x"), lax.axis_index("y"), lax.axis_index("c")
        me, sibling = (x, y, c), (x, y, 1 - c)
        chips = [(1 - x, y), (x, 1 - y), (1 - x, 1 - y)]

        def block(w, px, py, pc):
            return outs[w].at[pl.ds((4 * px + 2 * py + pc) * rows[w], rows[w])]

        def mine(w):
            return pack_ref.at[pl.ds(starts[w], rows[w])]

        def all_of(k):
            return _remote(pack_ref, pack_ref, send_sems.at[k], recv_sems.at[k], me)

        for w in range(nw):
            pltpu.make_async_copy(mine(w), block(w, *me), local_sem).start()
        for k, to in enumerate([sibling] + [(*chip, c) for chip in chips]):
            for w in range(nw):
                _remote(mine(w), block(w, *me), send_sems.at[k], recv_sems.at[k], to).start()
        for j, chip in enumerate(chips):
            all_of(1 + j).wait_recv()
            for w in range(nw):
                _remote(block(w, *chip, c), block(w, *chip, c), send_sems.at[4 + j], recv_sems.at[4 + j], sibling).start()
        all_of(0).wait_recv()
        for j in range(3):
            all_of(4 + j).wait_recv()
        for k in range(7):
            all_of(k).wait_send()
        pltpu.make_async_copy(pack_ref, pack_ref, local_sem).wait()

    return pl.pallas_call(
        body, out_shape=[jax.ShapeDtypeStruct((NDEV * r, C), pack.dtype) for r in rows],
        in_specs=[_HBM] + after_specs, out_specs=[_HBM] * nw,
        scratch_shapes=[pltpu.SemaphoreType.DMA((7,)), pltpu.SemaphoreType.DMA((7,)), pltpu.SemaphoreType.DMA],
        name=name)(pack, *after_ops)


_SEM = pl.BlockSpec(memory_space=pltpu.SEMAPHORE)
_DATAFLOW = pltpu.SideEffectType.DATAFLOW_SIDE_EFFECTING


def _split_start(srcs, lands, plan, n, after, name):
    nbuf = len(srcs) + len(lands)
    after_specs, after_ops = _after(after)

    def body(*refs):
        src_refs, land_refs = refs[:len(srcs)], refs[len(srcs):nbuf]
        send_sems, recv_sems = refs[nbuf + len(after_ops)], refs[nbuf + len(after_ops) + 1]
        for k, (src, dst, to) in enumerate(plan(src_refs, land_refs)):
            _remote(src, dst, send_sems.at[k], recv_sems.at[k], to).start()
        refs[-1][...] = jnp.zeros_like(refs[-1])

    bufs = [pltpu.with_memory_space_constraint(a, pltpu.HBM) for a in list(srcs) + list(lands)]
    outs = pl.pallas_call(
        body, name=name,
        out_shape=(pltpu.SemaphoreType.DMA((n,)), pltpu.SemaphoreType.DMA((n,)),
                   *[pltpu.HBM(a.shape, a.dtype) for a in bufs], jax.ShapeDtypeStruct((8, 128), F32)),
        in_specs=[_HBM] * nbuf + after_specs,
        out_specs=(_SEM, _SEM, *[_HBM] * nbuf, pl.BlockSpec(memory_space=pltpu.VMEM)),
        input_output_aliases={i: 2 + i for i in range(nbuf)},
        compiler_params=pltpu.CompilerParams(has_side_effects=_DATAFLOW))(*bufs, *after_ops)
    return outs[0], outs[1], list(outs[2:2 + len(srcs)]), list(outs[2 + len(srcs):2 + nbuf]), outs[-1]


def _split_wait(send_sems, recv_sems, srcs, lands, plan, after, name):
    nbuf = len(srcs) + len(lands)
    after_specs, after_ops = _after(after)

    def body(*refs):
        src_refs, land_refs = refs[:len(srcs)], refs[len(srcs):nbuf]
        send_sems_ref, recv_sems_ref = refs[nbuf], refs[nbuf + 1]
        for k, (src, dst, to) in enumerate(plan(src_refs, land_refs)):
            copy = _remote(src, dst, send_sems_ref.at[k], recv_sems_ref.at[k], to)
            copy.wait_send()
            copy.wait_recv()

    outs = pl.pallas_call(
        body, name=name, out_shape=tuple(pltpu.HBM(a.shape, a.dtype) for a in list(srcs) + list(lands)),
        in_specs=[_HBM] * nbuf + [_SEM, _SEM] + after_specs, out_specs=tuple([_HBM] * nbuf),
        input_output_aliases={i: i for i in range(nbuf)},
        compiler_params=pltpu.CompilerParams(has_side_effects=_DATAFLOW))(*srcs, *lands, send_sems, recv_sems, *after_ops)
    return list(outs[:len(srcs)]), list(outs[len(srcs):])


def _peers(x, y, c):
    return [(mask, (1 - x if mask & 4 else x, 1 - y if mask & 2 else y, 1 - c if mask & 1 else c))
            for mask in range(1, NDEV)]


def _gather_plan(rows):
    starts = [sum(rows[:w]) for w in range(len(rows))]

    def plan(src_refs, land_refs):
        x, y, c = lax.axis_index("x"), lax.axis_index("y"), lax.axis_index("c")
        copies = []
        for w, r in enumerate(rows):
            mine = src_refs[0].at[pl.ds(starts[w], r)]
            dst = land_refs[w].at[pl.ds((4 * x + 2 * y + c) * r, r)]
            copies += [(mine, dst, peer) for _, peer in _peers(x, y, c)]
        return copies

    return plan, (NDEV - 1) * len(rows)


def _place_own(shards, fulls, dev_idx, name):
    nw = len(shards)

    def body(i_ref, *refs):
        for w in range(nw):
            refs[2 * nw + w][...] = refs[w][...]

    grid_spec = pltpu.PrefetchScalarGridSpec(
        num_scalar_prefetch=1, grid=(1,),
        in_specs=[pl.BlockSpec(s.shape, lambda t, i_ref: (0, 0)) for s in shards] + [_ANY] * nw,
        out_specs=[pl.BlockSpec(s.shape, lambda t, i_ref: (i_ref[0], 0)) for s in shards])
    outs = pl.pallas_call(
        body, grid_spec=grid_spec, out_shape=[jax.ShapeDtypeStruct(f.shape, f.dtype) for f in fulls],
        input_output_aliases={1 + nw + w: w for w in range(nw)}, name=name)(dev_idx, *shards, *fulls)
    return list(outs)


def _scatter_plan(rows):
    def plan(src_refs, land_refs):
        x, y, c = lax.axis_index("x"), lax.axis_index("y"), lax.axis_index("c")
        copies = []
        for w, r in enumerate(rows):
            for mask, (px, py, pc) in _peers(x, y, c):
                src = src_refs[w].at[pl.ds((4 * px + 2 * py + pc) * r, r)]
                copies.append((src, land_refs[w].at[mask - 1], (px, py, pc)))
        return copies

    return plan, (NDEV - 1) * len(rows)


def _adamw_math(w, g, m, v):
    nm = ADAM_B1 * m + (1.0 - ADAM_B1) * g
    nv = ADAM_B2 * v + (1.0 - ADAM_B2) * (g * g)
    bc1 = 1.0 - ADAM_B1 ** ADAM_STEP
    bc2 = 1.0 - ADAM_B2 ** ADAM_STEP
    return -ADAM_LR * ((nm / bc1) / (jnp.sqrt(nv / bc2) + ADAM_EPS) + ADAM_WD * w), nm, nv


def _finish_weight(gs, gots, dev_idx, w, m, v, name, after=None):
    L = len(gs)
    n1, r, C = gots[0].shape
    block = (None,) + w.shape[1:]

    after_specs, after_ops = _after(after)

    def body(i_ref, *refs):
        ins, (w_ref, m_ref, v_ref), (g_out, d_out, m_out, v_out) = refs[:2 * L], refs[2 * L:2 * L + 3], refs[-4:]
        for layer in range(L):
            @pl.when(pl.program_id(0) == layer)
            def _():
                g_ref, got_ref = ins[2 * layer], ins[2 * layer + 1]
                acc = g_ref[...].astype(F32)
                for k in range(n1):
                    acc = acc + got_ref[k].astype(F32)
                g_out[...] = acc
                d_out[...], m_out[...], v_out[...] = _adamw_math(w_ref[...], acc, m_ref[...], v_ref[...])

    in_specs, ins = [], []
    for g, got in zip(gs, gots):
        in_specs += [pl.BlockSpec((r, C), lambda t, i_ref: (i_ref[0], 0), pipeline_mode=_RESIDENT),
                     pl.BlockSpec((n1, r, C), lambda t, i_ref: (0, 0, 0), pipeline_mode=_RESIDENT)]
        ins += [g, got]
    per_layer = pl.BlockSpec(block, lambda t, i_ref: (t, 0, 0))
    grid_spec = pltpu.PrefetchScalarGridSpec(
        num_scalar_prefetch=1, grid=(L,), in_specs=in_specs + [per_layer] * 3 + after_specs,
        out_specs=[per_layer] * 4)
    return pl.pallas_call(
        body, grid_spec=grid_spec, out_shape=[jax.ShapeDtypeStruct(w.shape, F32)] * 4,
        compiler_params=_cp("arbitrary"), name=name)(dev_idx, *ins, w, m, v, *after_ops)


def _sum_slots(a, name):
    n, R, C = a.shape

    def body(a_ref, o_ref):
        acc = a_ref[0]
        for k in range(1, n):
            acc = acc + a_ref[k]
        o_ref[...] = acc

    return pl.pallas_call(body, out_shape=jax.ShapeDtypeStruct((R, C), F32), name=name)(a)


def _shard_axis(name):
    return {"ev_w_in": 2, "ev_a_conv_w": 2, "ev_b_conv_w": 2, "ev_w_out": 1, "od_w_in": 2, "od_c_ln_g": 1,
            "od_c_ln_b": 1, "od_w_out": 1, "xa_w_q": 1, "xa_w_k": 1, "xa_w_v": 1, "xa_w_o": 1,
            "ffn_w_gate": 2, "ffn_w_up": 2, "ffn_w_down": 1}[name]


BIG = ["ev_w_in", "ev_w_out", "od_w_in", "od_w_out", "xa_w_q", "xa_w_k", "xa_w_v", "xa_w_o",
       "ffn_w_gate", "ffn_w_up", "ffn_w_down"]
SMALL_SHARDED = ["ev_a_conv_w", "ev_b_conv_w", "od_c_ln_g", "od_c_ln_b"]
REPLICATED = ["g_mix", "g_xattn", "g_mem", "g_ffn", "g_final", "ev_a_conv_b", "ev_a_ln_g", "ev_a_ln_b",
              "ev_b_conv_b", "od_w_s", "od_b_s"]
WEIGHTS = ["g_mix", "g_xattn", "g_mem", "g_ffn", "g_final", "ev_w_in", "ev_a_conv_w", "ev_a_conv_b", "ev_a_ln_g",
           "ev_a_ln_b", "ev_b_conv_w", "ev_b_conv_b", "ev_w_out", "od_w_in", "od_c_ln_g", "od_c_ln_b", "od_w_s",
           "od_b_s", "od_w_out", "xa_w_q", "xa_w_k", "xa_w_v", "xa_w_o", "ffn_w_gate", "ffn_w_up", "ffn_w_down"]


def _full_from_blocks(blocks, axis):
    shard = blocks.shape[1:]
    full = jnp.moveaxis(blocks, 0, axis)
    return full.reshape(shard[:axis] + (NDEV * shard[axis],) + shard[axis + 1:])


def _blocks_from_full(full, axis):
    shp = full.shape
    split = full.reshape(shp[:axis] + (NDEV, shp[axis] // NDEV) + shp[axis + 1:])
    return jnp.moveaxis(split, axis, 0)


def _pad_rows(flat, width, row_align):
    per = width * row_align
    n = -(-flat.shape[0] // per) * per
    return jnp.pad(flat, (0, n - flat.shape[0])).reshape(n // width, width)


def _row(v):
    return v.reshape(1, -1)


def _mem_kv(mem, g_m, wk, wv, name):
    M, D = mem.shape

    def body(mem_ref, g_ref, wk_ref, wv_ref, n_ref, k_ref, v_ref):
        x = mem_ref[...]
        r = lax.rsqrt(jnp.mean(x * x, axis=-1, keepdims=True) + RMS_EPS)
        n = ((x * r) * g_ref[...]).astype(BF16)
        n_ref[...] = n
        k_ref[...] = _dot(n, wk_ref[...]).astype(BF16)
        v_ref[...] = _dot(n, wv_ref[...]).astype(BF16)

    sh = jax.ShapeDtypeStruct((M, D), BF16)
    return pl.pallas_call(body, out_shape=[sh, sh, sh], name=name)(mem, g_m, wk, wv)


def _mem_kv_bwd(dk, dv, mem, mem_n, g_m, wk, wv, name):
    M, D = mem.shape

    def body(dk_ref, dv_ref, mem_ref, n_ref, g_ref, wk_ref, wv_ref, dwk_ref, dwv_ref, dg_ref):
        dk_, dv_ = dk_ref[...].astype(BF16), dv_ref[...].astype(BF16)
        n = n_ref[...]
        dwk_ref[...] = _dot_tn(n, dk_).astype(BF16)
        dwv_ref[...] = _dot_tn(n, dv_).astype(BF16)
        dn = _dot_nt(dk_, wk_ref[...]) + _dot_nt(dv_, wv_ref[...])
        x = mem_ref[...]
        r = lax.rsqrt(jnp.mean(x * x, axis=-1, keepdims=True) + RMS_EPS)
        dg_ref[...] = jnp.sum(dn * (x * r), axis=0, keepdims=True)

    wsh = jax.ShapeDtypeStruct((D, D), BF16)
    return pl.pallas_call(body, out_shape=[wsh, wsh, jax.ShapeDtypeStruct((1, D), F32)], name=name)(
        dk, dv, mem, mem_n, g_m, wk, wv)


def _xattn_fwd(h, nq, mem, g_m, wq, wk, wv, wo, g_next, tag, after):
    q = _mm([(nq, wq, "nn")], f"xa_q_{tag}", out_dtype=BF16, after=after)
    mem_n, k, v = _mem_kv(mem, _row(g_m), wk, wv, f"xa_mem_{tag}")
    o = _attn_fwd(q, k, v, f"xa_attn_{tag}")
    h_new, n_next = _mm([(o, wo, "nn")], f"xa_o_{tag}", res=h, rms_g=_row(g_next))
    return h_new, n_next, (h, nq, mem_n, q, k, v, o)


def _xattn_bwd(dh_new, saved, mem, g_x, g_m, wq, wk, wv, wo, tag, push):
    h, nq, mem_n, q, k, v, o = saved
    do = _mm([(dh_new, wo, "nt")], f"xa_do_{tag}", out_dtype=BF16)
    d_wo = _mm_tn(o, dh_new, f"xa_dwo_{tag}")
    dq, dk, dv = _attn_bwd(q, k, v, do, f"xa_attn_bwd_{tag}")
    d_wq = _mm_tn(nq, dq, f"xa_dwq_{tag}")
    d_wk, d_wv, d_gm = _mem_kv_bwd(dk, dv, mem, mem_n, _row(g_m), wk, wv, f"xa_mem_bwd_{tag}")
    token = push([d_wq, d_wk, d_wv, d_wo])
    dh, d_gx = _mm([(dq, wq, "nt")], f"xa_dnq_{tag}", rms_bwd=(h, _row(g_x), dh_new), tm=1024, after=token)
    return dh, dict(g_xattn=d_gx, g_mem=d_gm)


def _ffn_fwd(h, n, wgt, wut, wd, g_next, tag, after, loss=None):
    a, b, hid = _ffn_up(n, wgt, wut, f"ffn_up_{tag}", after=after)
    saved = (h, n, a, b, hid)
    if loss is not None:
        return _mm([(hid, wd, "nn")], f"ffn_down_{tag}", res=h, loss=loss, tm=1024), saved
    h_new, n_next = _mm([(hid, wd, "nn")], f"ffn_down_{tag}", res=h, rms_g=_row(g_next), tm=1024)
    return h_new, n_next, saved


def _ffn_bwd(dh_new, saved, g_f, wgt, wut, wd, tag, push):
    h, n, a, b, hid = saved
    da, db = _ffn_dhid(dh_new, wd, a, b, f"ffn_dhid_{tag}")
    d_wd = _mm_tn(hid, dh_new, f"ffn_dwd_{tag}", ts=1024, tn=1024)
    d_wgt = _mm_tn(da, n, f"ffn_dwg_{tag}", ts=1024, tn=1024)
    d_wut = _mm_tn(db, n, f"ffn_dwu_{tag}", ts=1024, tn=1024)
    token = push([d_wgt, d_wut, d_wd])
    dh, d_gf = _mm([(da, wgt, "nn"), (db, wut, "nn")], f"ffn_dn_{tag}", rms_bwd=(h, _row(g_f), dh_new), tm=512,
                   after=token)
    return dh, dict(g_ffn=d_gf)


_XA = ["xa_w_q", "xa_w_k", "xa_w_v", "xa_w_o"]
_FFN = ["ffn_w_gate", "ffn_w_up", "ffn_w_down"]
GATHERS = {
    "ev_in": [("ev_w_in", 0)],
    "xa0": [("ev_w_out", 0)] + [(n, 0) for n in _XA],
    "ffn0": [(n, 0) for n in _FFN],
    "od": [("od_w_in", 0), ("od_w_out", 0)],
    "xa1": [(n, 1) for n in _XA],
    "ffn1": [(n, 1) for n in _FFN],
}
SCATTERS = {
    "ffn1": [(n, 1) for n in _FFN],
    "xa1": [(n, 1) for n in _XA],
    "od": [("od_w_in", 0), ("od_w_out", 0)],
    "ffn0": [(n, 0) for n in _FFN],
    "xa0": [(n, 0) for n in _XA],
    "ev_out": [("ev_w_out", 0)],
    "ev_in": [("ev_w_in", 0)],
}


def _local_step(x, mem, loss_target, W, comm):
    grads = {}

    h0 = x
    (ev_w_in_t,), token = comm.weights("ev_in", None)
    n0 = _rms_fwd(h0, _row(W["g_mix"][0]), "ev_rms", after=token)
    z = _mm([(n0, ev_w_in_t, "nt")], "ev_in", tn=1280)
    token = comm.prefetch(["ffn0"], z)
    ab, ca = _conv_fwd(z, W["ev_a_conv_w"][0], W["ev_a_conv_b"], W["ev_a_ln_g"], W["ev_a_ln_b"],
                       W["ev_b_conv_w"][0], W["ev_b_conv_b"], "ev_conv", after=token)
    (ev_w_out, *xa_w0), _ = comm.weights("xa0", ab)
    h1, nq0 = _mm([(ab, ev_w_out, "nn")], "ev_out", res=h0, rms_g=_row(W["g_xattn"][0]))
    token = comm.prefetch(["od", "xa1"], nq0)
    h2, nf0, xa0 = _xattn_fwd(h1, nq0, mem, W["g_mem"][0], *xa_w0, W["g_ffn"][0], "l0", token)
    ffn_w0, _ = comm.weights("ffn0", nf0)
    token = comm.prefetch(["ffn1"], nf0)
    h3, n3, ff0 = _ffn_fwd(h2, nf0, *ffn_w0, W["g_mix"][1], "l0", token)

    (od_w_in_t, od_w_out), _ = comm.weights("od", n3)
    zp = _mm([(n3, od_w_in_t, "nt")], "od_in", tn=1024)
    D = x.shape[1]
    ws = W["od_w_s"][0].astype(BF16)
    wst = jnp.swapaxes(ws, 1, 2)
    bsb = jnp.repeat(jnp.transpose(W["od_b_s"][0]), D // C_GROUPS, axis=1)
    y_sgu = _sgu_fwd(zp, W["od_c_ln_g"], W["od_c_ln_b"], ws, bsb, "od_sgu")
    h4, nq1 = _mm([(y_sgu, od_w_out, "nn")], "od_out", res=h3, rms_g=_row(W["g_xattn"][1]))
    xa_w1, _ = comm.weights("xa1", nq1)
    h5, nf1, xa1 = _xattn_fwd(h4, nq1, mem, W["g_mem"][1], *xa_w1, W["g_ffn"][1], "l1", None)
    ffn_w1, _ = comm.weights("ffn1", nf1)
    (loss_row, dh6, d_gfinal), ff1 = _ffn_fwd(h5, nf1, *ffn_w1, None, "l1", None,
                                              loss=(_row(W["g_final"]), loss_target))
    grads["g_final"] = d_gfinal.reshape(-1)


    dh5, g_ff1 = _ffn_bwd(dh6, ff1, W["g_ffn"][1], *ffn_w1, "l1", lambda dws: comm.grads("ffn1", dws))
    dh4, g_xa1 = _xattn_bwd(dh5, xa1, mem, W["g_xattn"][1], W["g_mem"][1], *xa_w1, "l1",
                            lambda dws: comm.grads("xa1", dws))
    dy_sgu = _mm([(dh4, od_w_out, "nt")], "od_dy", tn=1024)
    d_od_out = _mm_tn(y_sgu, dh4, "od_dwout", tn=1024)
    dzp, d_ws, d_bsb, d_clng, d_clnb = _sgu_bwd(dy_sgu, zp, W["od_c_ln_g"], W["od_c_ln_b"], ws, wst, bsb, "od_sgu_bwd")
    grads["od_w_s"] = d_ws[None]
    grads["od_b_s"] = jnp.transpose(_group_sum(d_bsb, C_GROUPS, "od_dbs"))[None]
    grads["od_c_ln_g"], grads["od_c_ln_b"] = d_clng, d_clnb
    token = comm.grads("od", [_mm_tn(dzp, n3, "od_dwin", ts=1024, tn=1024), d_od_out])
    dh3, d_gmix1 = _mm([(dzp, od_w_in_t, "nn")], "od_dn", rms_bwd=(h3, _row(W["g_mix"][1]), dh4), tm=1024, after=token)

    dh2, g_ff0 = _ffn_bwd(dh3, ff0, W["g_ffn"][0], *ffn_w0, "l0", lambda dws: comm.grads("ffn0", dws))
    dh1, g_xa0 = _xattn_bwd(dh2, xa0, mem, W["g_xattn"][0], W["g_mem"][0], *xa_w0, "l0",
                            lambda dws: comm.grads("xa0", dws))
    token = comm.grads("ev_out", [_mm_tn(ab, dh1, "ev_dwout", tn=1024)])
    dca, db, d_lng, d_lnb, d_ba = _conv_dab_ln(dh1, ev_w_out, ca, W["ev_a_ln_g"], W["ev_a_ln_b"], "ev_dab",
                                               after=token)
    dz, d_wa, d_wb, d_bb = _conv_bwd(z, dca, db, W["ev_a_conv_w"][0], W["ev_b_conv_w"][0], W["ev_b_conv_b"],
                                     "ev_conv_bwd")
    grads.update(ev_a_ln_g=d_lng, ev_a_ln_b=d_lnb, ev_a_conv_b=d_ba, ev_b_conv_b=d_bb,
                 ev_a_conv_w=d_wa[None], ev_b_conv_w=d_wb[None])
    token = comm.grads("ev_in", [_mm_tn(dz, n0, "ev_dwin", ts=1024, tn=1024)])
    grad_x, d_gmix0 = _mm([(dz, ev_w_in_t, "nn")], "ev_dn", rms_bwd=(h0, _row(W["g_mix"][0]), dh1), tm=1024, after=token)

    grads["g_mix"] = jnp.concatenate([d_gmix0, d_gmix1], axis=0)
    for key in ("g_xattn", "g_mem"):
        grads[key] = jnp.concatenate([g_xa0[key], g_xa1[key]], axis=0)
    grads["g_ffn"] = jnp.concatenate([g_ff0["g_ffn"], g_ff1["g_ffn"]], axis=0)
    return loss_row, grad_x, grads


class _Exchanges:
    def __init__(self, shards, dev_idx, after):
        self.shards, self.dev_idx = shards, dev_idx
        self.gathering, self.scattering = {}, {}
        self.first = _all_gather_weights(self._pack(GATHERS["ev_in"]), self._rows(GATHERS["ev_in"]), "ag_ev_in",
                                         after=after)
        self.first_token = self.prefetch(["xa0"], self.first[0])

    def _rows(self, entries):
        return [self.shards[e].shape[0] for e in entries]

    def _pack(self, entries):
        return jnp.concatenate([self.shards[e] for e in entries], axis=0)

    def prefetch(self, gathers, after):
        for name in gathers:
            rows = self._rows(GATHERS[name])
            pack = self._pack(GATHERS[name])
            lands = [lax.empty((NDEV * r, pack.shape[1]), pack.dtype) for r in rows]
            plan, n = _gather_plan(rows)
            send, recv, srcs, lands, after = _split_start([pack], lands, plan, n, after, f"ag_{name}_start")
            self.gathering[name] = (send, recv, srcs, lands, plan, rows)
        return after

    def weights(self, name, after):
        if name == "ev_in":
            return self.first, self.first_token
        send, recv, srcs, lands, plan, rows = self.gathering.pop(name)
        _, lands = _split_wait(send, recv, srcs, lands, plan, after, f"ag_{name}_wait")
        return _place_own([self.shards[e] for e in GATHERS[name]], lands, self.dev_idx, f"ag_{name}_own"), None

    def grads(self, name, dws):
        rows = self._rows(SCATTERS[name])
        lands = [lax.empty((NDEV - 1, r, d.shape[1]), d.dtype) for r, d in zip(rows, dws)]
        plan, n = _scatter_plan(rows)
        send, recv, srcs, lands, token = _split_start(dws, lands, plan, n, None, f"rs_{name}_start")
        self.scattering[name] = (send, recv, srcs, lands, plan)
        return token

    def received(self, after):
        out = {}
        for name, (send, recv, srcs, lands, plan) in self.scattering.items():
            srcs, lands = _split_wait(send, recv, srcs, lands, plan, after, f"rs_{name}_wait")
            for entry, g, got in zip(SCATTERS[name], srcs, lands):
                out[entry] = (g, got)
        return out


def kernel(x, mem, g_mix, g_xattn, g_mem, g_ffn, g_final, ev_w_in, ev_a_conv_w, ev_a_conv_b, ev_a_ln_g, ev_a_ln_b, ev_b_conv_w, ev_b_conv_b, ev_w_out, od_w_in, od_c_ln_g, od_c_ln_b, od_w_s, od_b_s, od_w_out, xa_w_q, xa_w_k, xa_w_v, xa_w_o, ffn_w_gate, ffn_w_up, ffn_w_down, loss_target, m_g_mix, m_g_xattn, m_g_mem, m_g_ffn, m_g_final, m_ev_w_in, m_ev_a_conv_w, m_ev_a_conv_b, m_ev_a_ln_g, m_ev_a_ln_b, m_ev_b_conv_w, m_ev_b_conv_b, m_ev_w_out, m_od_w_in, m_od_c_ln_g, m_od_c_ln_b, m_od_w_s, m_od_b_s, m_od_w_out, m_xa_w_q, m_xa_w_k, m_xa_w_v, m_xa_w_o, m_ffn_w_gate, m_ffn_w_up, m_ffn_w_down, v_g_mix, v_g_xattn, v_g_mem, v_g_ffn, v_g_final, v_ev_w_in, v_ev_a_conv_w, v_ev_a_conv_b, v_ev_a_ln_g, v_ev_a_ln_b, v_ev_b_conv_w, v_ev_b_conv_b, v_ev_w_out, v_od_w_in, v_od_c_ln_g, v_od_c_ln_b, v_od_w_s, v_od_b_s, v_od_w_out, v_xa_w_q, v_xa_w_k, v_xa_w_v, v_xa_w_o, v_ffn_w_gate, v_ffn_w_up, v_ffn_w_down):
    local = dict(g_mix=g_mix, g_xattn=g_xattn, g_mem=g_mem, g_ffn=g_ffn, g_final=g_final, ev_w_in=ev_w_in, ev_a_conv_w=ev_a_conv_w, ev_a_conv_b=ev_a_conv_b, ev_a_ln_g=ev_a_ln_g, ev_a_ln_b=ev_a_ln_b, ev_b_conv_w=ev_b_conv_w, ev_b_conv_b=ev_b_conv_b, ev_w_out=ev_w_out, od_w_in=od_w_in, od_c_ln_g=od_c_ln_g, od_c_ln_b=od_c_ln_b, od_w_s=od_w_s, od_b_s=od_b_s, od_w_out=od_w_out, xa_w_q=xa_w_q, xa_w_k=xa_w_k, xa_w_v=xa_w_v, xa_w_o=xa_w_o, ffn_w_gate=ffn_w_gate, ffn_w_up=ffn_w_up, ffn_w_down=ffn_w_down)
    mom = dict(g_mix=m_g_mix, g_xattn=m_g_xattn, g_mem=m_g_mem, g_ffn=m_g_ffn, g_final=m_g_final, ev_w_in=m_ev_w_in, ev_a_conv_w=m_ev_a_conv_w, ev_a_conv_b=m_ev_a_conv_b, ev_a_ln_g=m_ev_a_ln_g, ev_a_ln_b=m_ev_a_ln_b, ev_b_conv_w=m_ev_b_conv_w, ev_b_conv_b=m_ev_b_conv_b, ev_w_out=m_ev_w_out, od_w_in=m_od_w_in, od_c_ln_g=m_od_c_ln_g, od_c_ln_b=m_od_c_ln_b, od_w_s=m_od_w_s, od_b_s=m_od_b_s, od_w_out=m_od_w_out, xa_w_q=m_xa_w_q, xa_w_k=m_xa_w_k, xa_w_v=m_xa_w_v, xa_w_o=m_xa_w_o, ffn_w_gate=m_ffn_w_gate, ffn_w_up=m_ffn_w_up, ffn_w_down=m_ffn_w_down)
    vel = dict(g_mix=v_g_mix, g_xattn=v_g_xattn, g_mem=v_g_mem, g_ffn=v_g_ffn, g_final=v_g_final, ev_w_in=v_ev_w_in, ev_a_conv_w=v_ev_a_conv_w, ev_a_conv_b=v_ev_a_conv_b, ev_a_ln_g=v_ev_a_ln_g, ev_a_ln_b=v_ev_a_ln_b, ev_b_conv_w=v_ev_b_conv_w, ev_b_conv_b=v_ev_b_conv_b, ev_w_out=v_ev_w_out, od_w_in=v_od_w_in, od_c_ln_g=v_od_c_ln_g, od_c_ln_b=v_od_c_ln_b, od_w_s=v_od_w_s, od_b_s=v_od_b_s, od_w_out=v_od_w_out, xa_w_q=v_xa_w_q, xa_w_k=v_xa_w_k, xa_w_v=v_xa_w_v, xa_w_o=v_xa_w_o, ffn_w_gate=v_ffn_w_gate, ffn_w_up=v_ffn_w_up, ffn_w_down=v_ffn_w_down)
    D = x.shape[-1]
    dev = 4 * lax.axis_index("x") + 2 * lax.axis_index("y") + lax.axis_index("c")

    def comm_layout(n, a):
        return jnp.transpose(a) if _shard_axis(n) == 2 else a

    shards = {(n, i): comm_layout(n, local[n][i]).astype(BF16) for n in BIG for i in range(local[n].shape[0])}
    small_sizes = [local[n].size for n in SMALL_SHARDED]
    small_block = _pad_rows(jnp.concatenate([local[n].reshape(-1) for n in SMALL_SHARDED]), 128, 8)
    small_all = _all_gather(small_block, "ag_small")
    comm = _Exchanges(shards, jnp.reshape(dev, (1,)).astype(jnp.int32), small_all)
    small_all = small_all.reshape(NDEV, -1)

    W = {n: local[n] for n in REPLICATED}
    o0 = 0
    for n, sz in zip(SMALL_SHARDED, small_sizes):
        blocks = small_all[:, o0:o0 + sz].reshape((NDEV,) + local[n].shape)
        W[n] = _full_from_blocks(blocks, _shard_axis(n))
        o0 += sz

    loss_row, grad_x, grads = _local_step(x[0], mem[0], loss_target[0], W, comm)

    received = comm.received(grad_x)
    rest = REPLICATED + SMALL_SHARDED
    rest_full_shapes = [grads[n].shape for n in rest]
    g_rest = _pad_rows(jnp.concatenate([grads[n].astype(F32).reshape(-1) for n in rest]), D, 8)
    small_rows = g_rest.shape[0]
    small_plan, small_n = _gather_plan([small_rows])
    small_send, small_recv, small_srcs, small_lands, token = _split_start(
        [g_rest], [lax.empty((NDEV * small_rows, D), F32)], small_plan, small_n, received["ev_w_in", 0][1],
        "ag_small_grads_start")

    gsh, delta, new_m, new_v = {}, {}, {}, {}
    def stacked_layout(n, a):
        return jnp.swapaxes(a, 1, 2) if _shard_axis(n) == 2 else a

    for n in BIG:
        parts = [received[n, i] for i in range(local[n].shape[0])]
        outs = _finish_weight([p[0] for p in parts], [p[1] for p in parts], comm.dev_idx,
                              *(stacked_layout(n, a) for a in (local[n], mom[n], vel[n])), f"finish_{n}", after=token)
        gsh[n], delta[n], new_m[n], new_v[n] = (stacked_layout(n, o) for o in outs)

    _, small_lands = _split_wait(small_send, small_recv, small_srcs, small_lands, small_plan,
                                 [delta[n] for n in BIG], "ag_small_grads_wait")
    partials = _place_own([g_rest], small_lands, comm.dev_idx, "ag_small_grads_own")[0]
    g_rest = _sum_slots(partials.reshape(NDEV, small_rows, D), "sum_small_grads").reshape(-1)
    o0 = 0
    for n, shp in zip(rest, rest_full_shapes):
        sz = 1
        for s in shp:
            sz *= s
        full = g_rest[o0:o0 + sz].reshape(shp)
        o0 += sz
        if n in SMALL_SHARDED:
            full = lax.dynamic_index_in_dim(_blocks_from_full(full, _shard_axis(n)), dev, 0, keepdims=False)
        gsh[n] = full.reshape(local[n].shape)

    small = _adamw_small([local[n] for n in rest], [gsh[n] for n in rest], [mom[n] for n in rest],
                         [vel[n] for n in rest], "adamw_small")
    for n, (d, nm, nv) in zip(rest, small):
        delta[n], new_m[n], new_v[n] = d, nm, nv

    loss = lax.psum(loss_row[0, 0], ("x", "y", "c"))
    return (loss, grad_x[None], *[gsh[n] for n in WEIGHTS], *[delta[n] for n in WEIGHTS],
            *[new_m[n] for n in WEIGHTS], *[new_v[n] for n in WEIGHTS])
```

```python
import jax
import jax.numpy as jnp
from jax import lax
from jax.experimental import pallas as pl
from jax.experimental.pallas import tpu as pltpu

F32, BF16 = jnp.float32, jnp.bfloat16
NDEV = 8
RMS_EPS = 1e-6
LN_EPS = 1e-5
CHUNK = 128
C_GROUPS = 8
XA_HEADS = 4
ADAM_LR, ADAM_B1, ADAM_B2, ADAM_EPS, ADAM_WD, ADAM_STEP = 0.001, 0.9, 0.999, 1e-08, 0.01, 10
HALO = 16
ROW_CHUNK = 32
ROW_CHUNK_FWD = 64
V7X_VMEM_LIMIT = 56 * 1024 * 1024
MESH = pl.DeviceIdType.MESH

TS_MM = 2048
TN_MM = 1408
TS_FFN = 512
MM_ROW_CHUNK = 256
FFN_COL_CHUNK = 256
TS_CONV = 512
TS_SGU = 512
TS_ATTN = 2048


def _cp(*sem):
    return pltpu.CompilerParams(dimension_semantics=sem, vmem_limit_bytes=V7X_VMEM_LIMIT)


def _pick(n, pref, align):
    for t in range(min(n, pref), 0, -1):
        if n % t == 0 and (t % align == 0 or t == n):
            return t
    return n


def _sigmoid(x):
    return 0.5 * jnp.tanh(0.5 * x) + 0.5


def _dot(a, b):
    return jnp.dot(a, b, preferred_element_type=F32)


def _dot_nt(a, b):
    return lax.dot_general(a, b, (((1,), (1,)), ((), ())), preferred_element_type=F32)


def _dot_tn(a, b):
    return lax.dot_general(a, b, (((0,), (0,)), ((), ())), preferred_element_type=F32)


_ANY = pl.BlockSpec(memory_space=pl.ANY)
_RESIDENT = pl.Buffered(1)


def _after(after):
    if after is None:
        return [], []
    ops = list(after) if isinstance(after, (list, tuple)) else [after]
    return [_ANY] * len(ops), ops


def _rms_fwd(h, g, name, after=None):
    S, D = h.shape
    ts = _pick(S, TS_MM, 16)
    after_specs, after_ops = _after(after)

    def body(h_ref, g_ref, *rest):
        o_ref = rest[-1]
        x = h_ref[...]
        r = lax.rsqrt(jnp.mean(x * x, axis=-1, keepdims=True) + RMS_EPS)
        o_ref[...] = ((x * r) * g_ref[...]).astype(o_ref.dtype)

    return pl.pallas_call(
        body, grid=(S // ts,),
        in_specs=[pl.BlockSpec((ts, D), lambda i: (i, 0)), pl.BlockSpec((1, D), lambda i: (0, 0))] + after_specs,
        out_specs=pl.BlockSpec((ts, D), lambda i: (i, 0)),
        out_shape=jax.ShapeDtypeStruct((S, D), BF16), compiler_params=_cp("parallel"), name=name)(h, g, *after_ops)


def _mm(pairs, name, out_dtype=F32, res=None, rms_g=None, rms_bwd=None, loss=None, tm=None, tn=None, after=None):
    M = pairs[0][0].shape[0]
    N = pairs[0][1].shape[1 if pairs[0][2] == "nn" else 0]
    whole_rows = rms_g is not None or rms_bwd is not None or loss is not None
    tm = _pick(M, tm or TS_MM, 16)
    tn = N if whole_rows else _pick(N, tn or TN_MM, 128)
    npair = len(pairs)
    modes = [p[2] for p in pairs]
    after_specs, after_ops = _after(after)

    rc = MM_ROW_CHUNK if whole_rows and tm % MM_ROW_CHUNK == 0 else tm

    def body(*refs):
        rest = refs[2 * npair + len(after_ops):]
        res_ref = None
        if res is not None:
            res_ref, rest = rest[0], rest[1:]
        if rms_bwd is not None:
            dg_ref = rest[4]

            @pl.when(pl.program_id(0) == 0)
            def _():
                dg_ref[...] = jnp.zeros_like(dg_ref)

        if loss is not None:
            g_ref, t_ref, loss_ref, dh_ref, dg_ref = rest

            @pl.when(pl.program_id(0) == 0)
            def _():
                dg_ref[...] = jnp.zeros_like(dg_ref)
                loss_ref[...] = jnp.zeros_like(loss_ref)

        for r0 in range(0, tm, rc):
            rows = pl.ds(r0, rc)
            acc = None
            for p in range(npair):
                a_ = refs[2 * p][rows, :].astype(BF16)
                d = _dot(a_, refs[2 * p + 1][...]) if modes[p] == "nn" else _dot_nt(a_, refs[2 * p + 1][...])
                acc = d if acc is None else acc + d
            if res_ref is not None:
                acc = acc + res_ref[rows, :]
            if rms_bwd is not None:
                h_ref, g_ref, dres_ref, dh_ref, _ = rest
                x = h_ref[rows, :]
                r = lax.rsqrt(jnp.mean(x * x, axis=-1, keepdims=True) + RMS_EPS)
                xr = x * r
                dg_ref[...] += jnp.sum(acc * xr, axis=0, keepdims=True)
                u = acc * g_ref[...]
                dh_ref[rows, :] = r * u - xr * (r * jnp.mean(u * xr, axis=-1, keepdims=True)) + dres_ref[rows, :]
            elif loss is not None:
                r = lax.rsqrt(jnp.mean(acc * acc, axis=-1, keepdims=True) + RMS_EPS)
                xr = acc * r
                gg = g_ref[...]
                e = xr * gg - t_ref[rows, :]
                chunk_loss = jnp.sum(jnp.sum(e * e, axis=0, keepdims=True), axis=1, keepdims=True) * (0.5 / N)
                loss_ref[...] += jnp.broadcast_to(chunk_loss, loss_ref.shape)
                dy = e * (1.0 / N)
                dg_ref[...] += jnp.sum(dy * xr, axis=0, keepdims=True)
                u = dy * gg
                dh_ref[rows, :] = r * u - xr * (r * jnp.mean(u * xr, axis=-1, keepdims=True))
            elif rms_g is not None:
                g_ref, o_ref, n_ref = rest
                o_ref[rows, :] = acc
                r = lax.rsqrt(jnp.mean(acc * acc, axis=-1, keepdims=True) + RMS_EPS)
                n_ref[rows, :] = ((acc * r) * g_ref[...]).astype(BF16)
            else:
                rest[0][rows, :] = acc.astype(rest[0].dtype)

    in_specs, ins = [], []
    for a, w, mode in pairs:
        K = a.shape[1]
        in_specs.append(pl.BlockSpec((tm, K), lambda i, j: (i, 0)))
        once = _RESIDENT if tn == N else None
        in_specs.append(pl.BlockSpec((K, tn), lambda i, j: (0, j), pipeline_mode=once) if mode == "nn"
                        else pl.BlockSpec((tn, K), lambda i, j: (j, 0), pipeline_mode=once))
        ins += [a, w]
    in_specs += after_specs
    ins += after_ops
    tile = pl.BlockSpec((tm, tn), lambda i, j: (i, j))
    vec = pl.BlockSpec((1, tn), lambda i, j: (0, j))
    if res is not None:
        in_specs.append(tile)
        ins.append(res)
    sem = ("parallel", "parallel")
    if rms_bwd is not None:
        in_specs += [tile, vec, tile]
        ins += list(rms_bwd)
        out_specs = [tile, vec]
        out_shape = [jax.ShapeDtypeStruct((M, N), F32), jax.ShapeDtypeStruct((1, N), F32)]
        sem = ("arbitrary", "arbitrary")
    elif loss is not None:
        in_specs += [vec, tile]
        ins += list(loss)
        out_specs = [pl.BlockSpec((1, 128), lambda i, j: (0, 0)), tile, vec]
        out_shape = [jax.ShapeDtypeStruct((1, 128), F32), jax.ShapeDtypeStruct((M, N), F32),
                     jax.ShapeDtypeStruct((1, N), F32)]
        sem = ("arbitrary", "arbitrary")
    elif rms_g is not None:
        in_specs.append(vec)
        ins.append(rms_g)
        out_specs = [tile, tile]
        out_shape = [jax.ShapeDtypeStruct((M, N), F32), jax.ShapeDtypeStruct((M, N), BF16)]
    else:
        out_specs = tile
        out_shape = jax.ShapeDtypeStruct((M, N), out_dtype)
    return pl.pallas_call(
        body, grid=(M // tm, N // tn), in_specs=in_specs, out_specs=out_specs, out_shape=out_shape,
        compiler_params=_cp(*sem), name=name)(*ins)


def _mm_tn(a, b, name, ts=None, tn=None):
    S, K = a.shape
    N = b.shape[1]
    ts = _pick(S, ts or TS_MM, 16)
    tn = _pick(N, tn or TN_MM, 128)
    nsteps = S // ts

    def body(a_ref, b_ref, o_ref, acc_ref):
        s = pl.program_id(1)

        @pl.when(s == 0)
        def _():
            acc_ref[...] = jnp.zeros_like(acc_ref)

        acc_ref[...] += _dot_tn(a_ref[...].astype(BF16), b_ref[...].astype(BF16))

        @pl.when(s == nsteps - 1)
        def _():
            o_ref[...] = acc_ref[...].astype(o_ref.dtype)

    return pl.pallas_call(
        body, grid=(N // tn, nsteps),
        in_specs=[pl.BlockSpec((ts, K), lambda j, s: (s, 0)), pl.BlockSpec((ts, tn), lambda j, s: (s, j))],
        out_specs=pl.BlockSpec((K, tn), lambda j, s: (0, j)), out_shape=jax.ShapeDtypeStruct((K, N), BF16),
        scratch_shapes=[pltpu.VMEM((K, tn), F32)],
        compiler_params=_cp("parallel", "arbitrary"), name=name)(a, b)


def _col_chunks(n):
    return [(c0, min(FFN_COL_CHUNK, n - c0)) for c0 in range(0, n, FFN_COL_CHUNK)]


def _ffn_up(n, wgt, wut, name, after=None):
    S, D = n.shape
    F = wgt.shape[0]
    tm = _pick(S, TS_FFN, 16)
    after_specs, after_ops = _after(after)

    def body(n_ref, wg_ref, wu_ref, *rest):
        a_ref, b_ref, hid_ref = rest[-3:]
        x = n_ref[...]
        for c0, ce in _col_chunks(F):
            a = _dot_nt(x, wg_ref[c0:c0 + ce, :])
            b = _dot_nt(x, wu_ref[c0:c0 + ce, :])
            a_ref[:, c0:c0 + ce] = a.astype(BF16)
            b_ref[:, c0:c0 + ce] = b.astype(BF16)
            hid_ref[:, c0:c0 + ce] = (a * _sigmoid(a) * b).astype(BF16)

    wspec = pl.BlockSpec((F, D), lambda i: (0, 0), pipeline_mode=_RESIDENT)
    ospec = pl.BlockSpec((tm, F), lambda i: (i, 0))
    osh = jax.ShapeDtypeStruct((S, F), BF16)
    return pl.pallas_call(
        body, grid=(S // tm,),
        in_specs=[pl.BlockSpec((tm, D), lambda i: (i, 0)), wspec, wspec] + after_specs,
        out_specs=[ospec, ospec, ospec], out_shape=[osh, osh, osh],
        compiler_params=_cp("parallel"), name=name)(n, wgt, wut, *after_ops)


def _ffn_dhid(dh, wd, a, b, name):
    S, D = dh.shape
    F = wd.shape[0]
    tm = _pick(S, TS_FFN, 16)

    def body(dh_ref, wd_ref, a_ref, b_ref, da_ref, db_ref):
        x = dh_ref[...].astype(BF16)
        for c0, ce in _col_chunks(F):
            g = _dot_nt(x, wd_ref[c0:c0 + ce, :]).astype(BF16)
            a_ = a_ref[:, c0:c0 + ce]
            sg = _sigmoid(a_)
            silu = a_ * sg
            da_ref[:, c0:c0 + ce] = (g * b_ref[:, c0:c0 + ce]) * (sg + silu * (1.0 - sg))
            db_ref[:, c0:c0 + ce] = g * silu

    tile = pl.BlockSpec((tm, F), lambda i: (i, 0))
    osh = jax.ShapeDtypeStruct((S, F), BF16)
    return pl.pallas_call(
        body, grid=(S // tm,),
        in_specs=[pl.BlockSpec((tm, D), lambda i: (i, 0)),
                  pl.BlockSpec((F, D), lambda i: (0, 0), pipeline_mode=_RESIDENT), tile, tile],
        out_specs=[tile, tile], out_shape=[osh, osh],
        compiler_params=_cp("parallel"), name=name)(dh, wd, a, b)


def _softmax_rows(s):
    m = jnp.max(s, axis=-1, keepdims=True)
    p = jnp.exp(s - m)
    return p / jnp.sum(p, axis=-1, keepdims=True)


def _attn_fwd(q, k, v, name):
    S, D = q.shape
    M = k.shape[0]
    hd = D // XA_HEADS
    scale = hd ** -0.5
    ts = _pick(S, TS_ATTN, 16)

    def body(q_ref, k_ref, v_ref, o_ref):
        for h in range(XA_HEADS):
            sl = slice(h * hd, (h + 1) * hd)
            p = _softmax_rows(_dot_nt(q_ref[:, sl], k_ref[:, sl]) * scale)
            o_ref[:, sl] = _dot(p.astype(BF16), v_ref[:, sl]).astype(BF16)

    tile = pl.BlockSpec((ts, D), lambda i: (i, 0))
    memspec = pl.BlockSpec((M, D), lambda i: (0, 0))
    return pl.pallas_call(
        body, grid=(S // ts,), in_specs=[tile, memspec, memspec], out_specs=tile,
        out_shape=jax.ShapeDtypeStruct((S, D), BF16), compiler_params=_cp("parallel"), name=name)(q, k, v)


def _attn_bwd(q, k, v, do, name):
    S, D = q.shape
    M = k.shape[0]
    hd = D // XA_HEADS
    scale = hd ** -0.5
    ts = _pick(S, TS_ATTN, 16)

    def body(q_ref, k_ref, v_ref, do_ref, dq_ref, dk_ref, dv_ref):
        @pl.when(pl.program_id(0) == 0)
        def _():
            dk_ref[...] = jnp.zeros_like(dk_ref)
            dv_ref[...] = jnp.zeros_like(dv_ref)

        for h in range(XA_HEADS):
            sl = slice(h * hd, (h + 1) * hd)
            qh, kh, vh, doh = q_ref[:, sl], k_ref[:, sl], v_ref[:, sl], do_ref[:, sl]
            p = _softmax_rows(_dot_nt(qh, kh) * scale)
            dp = _dot_nt(doh, vh)
            dv_ref[:, sl] += _dot_tn(p.astype(BF16), doh)
            delta = jnp.sum(dp * p, axis=-1, keepdims=True)
            ds = (p * (dp - delta) * scale).astype(BF16)
            dq_ref[:, sl] = _dot(ds, kh).astype(BF16)
            dk_ref[:, sl] += _dot_tn(ds, qh)

    tile = pl.BlockSpec((ts, D), lambda i: (i, 0))
    memspec = pl.BlockSpec((M, D), lambda i: (0, 0))
    return pl.pallas_call(
        body, grid=(S // ts,), in_specs=[tile, memspec, memspec, tile], out_specs=[tile, memspec, memspec],
        out_shape=[jax.ShapeDtypeStruct((S, D), BF16), jax.ShapeDtypeStruct((M, D), F32),
                   jax.ShapeDtypeStruct((M, D), F32)],
        compiler_params=_cp("arbitrary"), name=name)(q, k, v, do)


def _halo_specs(ts, col):
    per = ts // HALO

    def prev(i):
        return (jnp.maximum(i * per - 1, 0), col)

    def nxt(i, n_tiles):
        return (jnp.minimum((i + 1) * per, n_tiles * per - 1), col)

    return prev, nxt


def _fill_ext(ext_ref, prev_val, main_val, next_val, first, last, ts):
    ext_ref[pl.ds(0, HALO), :] = jnp.where(first, 0.0, prev_val)
    ext_ref[pl.ds(HALO, ts), :] = main_val
    ext_ref[pl.ds(HALO + ts, HALO), :] = jnp.where(last, 0.0, next_val)


SUBLANES = 8


def _fill_shifted(sh_ref, ts):
    n = ts + 2 * HALO - SUBLANES
    for s in range(1, SUBLANES):
        sh_ref[s, pl.ds(0, n), :] = sh_ref[0, pl.ds(s, n), :]


def _tap(sh_ref, r0, offset, rc):
    q, s = divmod(offset, SUBLANES)
    return sh_ref[s, pl.ds(pl.multiple_of(r0 + SUBLANES * q, SUBLANES), rc), :]


def _conv_fwd(z, wa, ba, lng, lnb, wb, bb, name, after=None):
    S = z.shape[0]
    C = z.shape[1] // 5
    KA, KB = wa.shape[0], wb.shape[0]
    pa, pb = KA // 2, KB // 2
    assert pa <= HALO and pb <= HALO
    ts = _pick(S, TS_CONV, ROW_CHUNK_FWD)
    nt = S // ts
    rc = ROW_CHUNK_FWD
    prev, nxt = _halo_specs(ts, 0)
    after_specs, after_ops = _after(after)

    def body(*refs):
        compute(*refs[:9], *refs[9 + len(after_ops):])

    def compute(z_ref, zp_ref, zn_ref, wa_ref, ba_ref, lng_ref, lnb_ref, wb_ref, bb_ref, ab_ref, ca_ref,
                ga_sh, tb_ext, win_b):
        i = pl.program_id(0)
        first, last = i == 0, i == nt - 1

        def glu(r):
            return r[:, 0:C] * _sigmoid(r[:, C:2 * C])

        def gcb(r):
            return r[:, 4 * C:5 * C] * r[:, 2 * C:3 * C]

        _fill_ext(ga_sh.at[0], glu(zp_ref), glu(z_ref), glu(zn_ref), first, last, ts)
        _fill_shifted(ga_sh, ts)
        _fill_ext(tb_ext, gcb(zp_ref), gcb(z_ref), gcb(zn_ref), first, last, ts)

        def chunk(c, carry):
            r0 = pl.multiple_of(c * rc, rc)
            win_b[...] = tb_ext[pl.ds(r0, rc + 2 * HALO), :]
            acc = jnp.zeros((rc, C), F32)
            for k in range(KA):
                acc = acc + wa_ref[k:k + 1, :] * _tap(ga_sh, r0, HALO - pa + k, rc)
            ca = acc + ba_ref[...]
            ca_ref[pl.ds(r0, rc), :] = ca
            mu = jnp.mean(ca, axis=-1, keepdims=True)
            xc = ca - mu
            var = jnp.mean(xc * xc, axis=-1, keepdims=True)
            ln = xc * lax.rsqrt(var + LN_EPS) * lng_ref[...] + lnb_ref[...]
            ab_ref[pl.ds(r0, rc), 0:C] = (ln * _sigmoid(ln)).astype(BF16)
            cb = jnp.zeros((rc, C), F32) + bb_ref[...]
            for k in range(KB):
                cb = cb + wb_ref[k:k + 1, :] * win_b[pl.ds(HALO - pb + k, rc), :]
            ab_ref[pl.ds(r0, rc), C:2 * C] = (z_ref[pl.ds(r0, rc), 3 * C:4 * C] * cb).astype(BF16)
            return carry

        lax.fori_loop(0, ts // rc, chunk, 0)

    zspec = pl.BlockSpec((ts, 5 * C), lambda i: (i, 0))
    zprev = pl.BlockSpec((HALO, 5 * C), prev)
    znext = pl.BlockSpec((HALO, 5 * C), lambda i: nxt(i, nt))

    def full(a):
        return pl.BlockSpec(a.shape, lambda i: (0, 0))

    return pl.pallas_call(
        body, grid=(nt,),
        in_specs=[zspec, zprev, znext, full(wa), full(ba), full(lng), full(lnb), full(wb), full(bb)] + after_specs,
        out_specs=[pl.BlockSpec((ts, 2 * C), lambda i: (i, 0)), pl.BlockSpec((ts, C), lambda i: (i, 0))],
        out_shape=[jax.ShapeDtypeStruct((S, 2 * C), BF16), jax.ShapeDtypeStruct((S, C), F32)],
        scratch_shapes=[pltpu.VMEM((SUBLANES, ts + 2 * HALO, C), F32), pltpu.VMEM((ts + 2 * HALO, C), F32),
                        pltpu.VMEM((rc + 2 * HALO, C), F32)],
        compiler_params=_cp("parallel"), name=name)(z, z, z, wa, ba, lng, lnb, wb, bb, *after_ops)


def _conv_dab_ln(dh, w_out, ca, lng, lnb, name, after=None):
    S, D = dh.shape
    C = ca.shape[1]
    tm = _pick(S, TS_MM // 2, 16)
    rc = MM_ROW_CHUNK if tm % MM_ROW_CHUNK == 0 else tm
    after_specs, after_ops = _after(after)

    def body(dh_ref, w_ref, ca_ref, lng_ref, lnb_ref, *rest):
        dca_ref, dbo_ref, dg_ref, db_ref, dbias_ref = rest[-5:]

        @pl.when(pl.program_id(0) == 0)
        def _():
            dg_ref[...] = jnp.zeros_like(dg_ref)
            db_ref[...] = jnp.zeros_like(db_ref)
            dbias_ref[...] = jnp.zeros_like(dbias_ref)

        for r0 in range(0, tm, rc):
            rows = pl.ds(r0, rc)
            dab = _dot_nt(dh_ref[rows, :].astype(BF16), w_ref[...])
            dbo_ref[rows, :] = dab[:, C:2 * C]
            ca_ = ca_ref[rows, :]
            mu = jnp.mean(ca_, axis=-1, keepdims=True)
            xc = ca_ - mu
            rstd = lax.rsqrt(jnp.mean(xc * xc, axis=-1, keepdims=True) + LN_EPS)
            xh = xc * rstd
            ln = xh * lng_ref[...] + lnb_ref[...]
            sg = _sigmoid(ln)
            dln = dab[:, 0:C] * (sg * (1.0 + ln * (1.0 - sg)))
            dg_ref[...] += jnp.sum(dln * xh, axis=0, keepdims=True)
            db_ref[...] += jnp.sum(dln, axis=0, keepdims=True)
            dxh = dln * lng_ref[...]
            dca = rstd * (dxh - jnp.mean(dxh, axis=-1, keepdims=True) - xh * jnp.mean(dxh * xh, axis=-1, keepdims=True))
            dca_ref[rows, :] = dca
            dbias_ref[...] += jnp.sum(dca, axis=0, keepdims=True)

    tile = pl.BlockSpec((tm, C), lambda i: (i, 0))
    vec = pl.BlockSpec((1, C), lambda i: (0, 0))
    vsh = jax.ShapeDtypeStruct((1, C), F32)
    return pl.pallas_call(
        body, grid=(S // tm,),
        in_specs=[pl.BlockSpec((tm, D), lambda i: (i, 0)),
                  pl.BlockSpec(w_out.shape, lambda i: (0, 0), pipeline_mode=_RESIDENT), tile, vec, vec] + after_specs,
        out_specs=[tile, tile, vec, vec, vec],
        out_shape=[jax.ShapeDtypeStruct((S, C), F32), jax.ShapeDtypeStruct((S, C), F32), vsh, vsh, vsh],
        compiler_params=_cp("arbitrary"), name=name)(dh, w_out, ca, lng, lnb, *after_ops)


def _conv_bwd(z, dca, db, wa, wb, bb, name):
    S = z.shape[0]
    C = z.shape[1] // 5
    KA, KB = wa.shape[0], wb.shape[0]
    pa, pb = KA // 2, KB // 2
    ts = _pick(S, TS_CONV, ROW_CHUNK)
    nt = S // ts
    rc = ROW_CHUNK
    prev0, nxt0 = _halo_specs(ts, 0)

    def body(z_ref, zp_ref, zn_ref, dca_ref, dcap_ref, dcan_ref, db_ref, dbp_ref, dbn_ref, wa_ref, wb_ref, bb_ref,
             dz_ref, dwa_ref, dwb_ref, dbb_ref,
             ga_sh, dca_sh, tb_ext, dcb_ext, win_tb, win_dcb, acc_a, acc_b, acc_bias):
        i = pl.program_id(0)
        first, last = i == 0, i == nt - 1

        @pl.when(first)
        def _():
            acc_a[...] = jnp.zeros_like(acc_a)
            acc_b[...] = jnp.zeros_like(acc_b)
            acc_bias[...] = jnp.zeros_like(acc_bias)

        def glu(r):
            return r[:, 0:C] * _sigmoid(r[:, C:2 * C])

        def gcb(r):
            return r[:, 4 * C:5 * C] * r[:, 2 * C:3 * C]

        def dcb(d, r):
            return d[...].astype(F32) * r[:, 3 * C:4 * C]

        _fill_ext(ga_sh.at[0], glu(zp_ref), glu(z_ref), glu(zn_ref), first, last, ts)
        _fill_shifted(ga_sh, ts)
        _fill_ext(dca_sh.at[0], dcap_ref[...], dca_ref[...], dcan_ref[...], first, last, ts)
        _fill_shifted(dca_sh, ts)
        _fill_ext(tb_ext, gcb(zp_ref), gcb(z_ref), gcb(zn_ref), first, last, ts)
        _fill_ext(dcb_ext, dcb(dbp_ref, zp_ref), dcb(db_ref, z_ref), dcb(dbn_ref, zn_ref), first, last, ts)

        def fold(x):
            return jnp.sum(x.reshape(rc // 8, 8, C), axis=0)

        def chunk(c, carry):
            r0 = pl.multiple_of(c * rc, rc)
            win_tb[...] = tb_ext[pl.ds(r0, rc + 2 * HALO), :]
            win_dcb[...] = dcb_ext[pl.ds(r0, rc + 2 * HALO), :]
            dca_c = _tap(dca_sh, r0, HALO, rc)
            dglu = jnp.zeros((rc, C), F32)
            for k in range(KA):
                dglu = dglu + wa_ref[k:k + 1, :] * _tap(dca_sh, r0, HALO + pa - k, rc)
                acc_a[k] += fold(dca_c * _tap(ga_sh, r0, HALO - pa + k, rc))
            val = z_ref[pl.ds(r0, rc), 0:C]
            sg = _sigmoid(z_ref[pl.ds(r0, rc), C:2 * C])
            dz_ref[pl.ds(r0, rc), 0:C] = (dglu * sg).astype(BF16)
            dz_ref[pl.ds(r0, rc), C:2 * C] = (dglu * val * sg * (1.0 - sg)).astype(BF16)
            dcb_c = win_dcb[pl.ds(HALO, rc), :]
            cb = jnp.zeros((rc, C), F32) + bb_ref[...]
            dt = jnp.zeros((rc, C), F32)
            for k in range(KB):
                tb_k = win_tb[pl.ds(HALO - pb + k, rc), :]
                cb = cb + wb_ref[k:k + 1, :] * tb_k
                dt = dt + wb_ref[k:k + 1, :] * win_dcb[pl.ds(HALO + pb - k, rc), :]
                acc_b[k] += fold(dcb_c * tb_k)
            acc_bias[...] += fold(dcb_c)
            db_c = db_ref[pl.ds(r0, rc), :].astype(F32)
            dz_ref[pl.ds(r0, rc), 2 * C:3 * C] = (dt * z_ref[pl.ds(r0, rc), 4 * C:5 * C]).astype(BF16)
            dz_ref[pl.ds(r0, rc), 3 * C:4 * C] = (db_c * cb).astype(BF16)
            dz_ref[pl.ds(r0, rc), 4 * C:5 * C] = (dt * z_ref[pl.ds(r0, rc), 2 * C:3 * C]).astype(BF16)
            return carry

        lax.fori_loop(0, ts // rc, chunk, 0)

        @pl.when(last)
        def _():
            dwa_ref[...] = jnp.sum(acc_a[...], axis=1)
            dwb_ref[...] = jnp.sum(acc_b[...], axis=1)
            dbb_ref[...] = jnp.sum(acc_bias[...], axis=0, keepdims=True)

    zspec = pl.BlockSpec((ts, 5 * C), lambda i: (i, 0))
    zprev = pl.BlockSpec((HALO, 5 * C), prev0)
    znext = pl.BlockSpec((HALO, 5 * C), lambda i: nxt0(i, nt))
    dspec = pl.BlockSpec((ts, C), lambda i: (i, 0))
    dprev = pl.BlockSpec((HALO, C), prev0)
    dnext = pl.BlockSpec((HALO, C), lambda i: nxt0(i, nt))

    def full(shape):
        return pl.BlockSpec(shape, lambda i: (0,) * len(shape))

    ext = pltpu.VMEM((ts + 2 * HALO, C), F32)
    shifted = pltpu.VMEM((SUBLANES, ts + 2 * HALO, C), F32)
    win = pltpu.VMEM((rc + 2 * HALO, C), F32)
    return pl.pallas_call(
        body, grid=(nt,),
        in_specs=[zspec, zprev, znext, dspec, dprev, dnext, dspec, dprev, dnext,
                  full(wa.shape), full(wb.shape), full(bb.shape)],
        out_specs=[pl.BlockSpec((ts, 5 * C), lambda i: (i, 0)), full((KA, C)), full((KB, C)), full((1, C))],
        out_shape=[jax.ShapeDtypeStruct((S, 5 * C), BF16), jax.ShapeDtypeStruct((KA, C), F32),
                   jax.ShapeDtypeStruct((KB, C), F32), jax.ShapeDtypeStruct((1, C), F32)],
        scratch_shapes=[shifted, shifted, ext, ext, win, win,
                        pltpu.VMEM((KA, 8, C), F32), pltpu.VMEM((KB, 8, C), F32), pltpu.VMEM((8, C), F32)],
        compiler_params=_cp("arbitrary"), name=name)(z, z, z, dca, dca, dca, db, db, db, wa, wb, bb)


_GELU_C = 0.7978845608028654
_GELU_A = 0.044715


def _gelu(x):
    return 0.5 * x * (1.0 + jnp.tanh(_GELU_C * (x + _GELU_A * (x * x * x))))


def _gelu_and_grad(x):
    t = jnp.tanh(_GELU_C * (x + _GELU_A * (x * x * x)))
    hx = 0.5 * x
    return hx * (1.0 + t), 0.5 * (1.0 + t) + hx * (1.0 - t * t) * (_GELU_C * (1.0 + 3.0 * _GELU_A * x * x))


def _sgu_fwd(zp, lng, lnb, ws, bsb, name):
    S = zp.shape[0]
    D = zp.shape[1] // 2
    G = ws.shape[0]
    gd = D // G
    ts = _pick(S, TS_SGU, CHUNK)
    ncs = ts // CHUNK

    def body(zp_ref, lng_ref, lnb_ref, ws_ref, bsb_ref, y_ref, vb_ref):
        v = _gelu(zp_ref[:, D:2 * D])
        mu = jnp.mean(v, axis=-1, keepdims=True)
        xc = v - mu
        rstd = lax.rsqrt(jnp.mean(xc * xc, axis=-1, keepdims=True) + LN_EPS)
        vb_ref[...] = (xc * rstd * lng_ref[...] + lnb_ref[...]).astype(BF16)
        for c in range(ncs):
            rows = slice(c * CHUNK, (c + 1) * CHUNK)
            for g in range(G):
                cols = slice(g * gd, (g + 1) * gd)
                sv = _dot(ws_ref[g], vb_ref[rows, cols]) + bsb_ref[:, cols]
                y_ref[rows, cols] = (_gelu(zp_ref[rows, cols]) * sv).astype(BF16)

    def full(a):
        return pl.BlockSpec(a.shape, lambda i: (0,) * a.ndim)

    return pl.pallas_call(
        body, grid=(S // ts,),
        in_specs=[pl.BlockSpec((ts, 2 * D), lambda i: (i, 0)), full(lng), full(lnb), full(ws), full(bsb)],
        out_specs=pl.BlockSpec((ts, D), lambda i: (i, 0)), out_shape=jax.ShapeDtypeStruct((S, D), BF16),
        scratch_shapes=[pltpu.VMEM((ts, D), BF16)],
        compiler_params=_cp("parallel"), name=name)(zp, lng, lnb, ws, bsb)


def _sgu_bwd(dh, w_out, zp, lng, lnb, ws, wst, bsb, name):
    S = zp.shape[0]
    D = zp.shape[1] // 2
    G = ws.shape[0]
    gd = D // G
    ts = _pick(S, TS_SGU, CHUNK)
    ncs = ts // CHUNK

    def body(dh_ref, wo_ref, zp_ref, lng_ref, lnb_ref, ws_ref, wst_ref, bsb_ref,
             dzp_ref, dws_ref, dbs_ref, dg_ref, db_ref, vb_ref, dvln_ref, acc_bs, dy_ref):
        i = pl.program_id(0)

        @pl.when(i == 0)
        def _():
            dws_ref[...] = jnp.zeros_like(dws_ref)
            acc_bs[...] = jnp.zeros_like(acc_bs)
            dg_ref[...] = jnp.zeros_like(dg_ref)
            db_ref[...] = jnp.zeros_like(db_ref)

        dy_ref[...] = _dot_nt(dh_ref[...].astype(BF16), wo_ref[...])
        v, dv_dz = _gelu_and_grad(zp_ref[:, D:2 * D])
        mu = jnp.mean(v, axis=-1, keepdims=True)
        xc = v - mu
        rstd = lax.rsqrt(jnp.mean(xc * xc, axis=-1, keepdims=True) + LN_EPS)
        xh = xc * rstd
        vb_ref[...] = (xh * lng_ref[...] + lnb_ref[...]).astype(BF16)
        for c in range(ncs):
            rows = slice(c * CHUNK, (c + 1) * CHUNK)
            for g in range(G):
                cols = slice(g * gd, (g + 1) * gd)
                u, du_dz = _gelu_and_grad(zp_ref[rows, cols])
                dy_ = dy_ref[rows, cols]
                sv = _dot(ws_ref[g], vb_ref[rows, cols]) + bsb_ref[:, cols]
                dzp_ref[rows, cols] = (dy_ * sv * du_dz).astype(BF16)
                dsv = dy_ * u
                acc_bs[:, cols] += dsv
                dsvb = dsv.astype(BF16)
                dws_ref[g] += _dot_nt(dsvb, vb_ref[rows, cols])
                dvln_ref[rows, cols] = _dot(wst_ref[g], dsvb)
        dvln = dvln_ref[...]
        dg_ref[...] += jnp.sum(dvln * xh, axis=0, keepdims=True)
        db_ref[...] += jnp.sum(dvln, axis=0, keepdims=True)
        dxh = dvln * lng_ref[...]
        dv = rstd * (dxh - jnp.mean(dxh, axis=-1, keepdims=True) - xh * jnp.mean(dxh * xh, axis=-1, keepdims=True))
        dzp_ref[:, D:2 * D] = (dv * dv_dz).astype(BF16)

        @pl.when(i == pl.num_programs(0) - 1)
        def _():
            dbs_ref[...] = acc_bs[...]

    def full(shape):
        return pl.BlockSpec(shape, lambda i: (0,) * len(shape))

    return pl.pallas_call(
        body, grid=(S // ts,),
        in_specs=[pl.BlockSpec((ts, D), lambda i: (i, 0)), full(w_out.shape),
                  pl.BlockSpec((ts, 2 * D), lambda i: (i, 0)),
                  full(lng.shape), full(lnb.shape), full(ws.shape), full(wst.shape), full(bsb.shape)],
        out_specs=[pl.BlockSpec((ts, 2 * D), lambda i: (i, 0)), full(ws.shape), full(bsb.shape),
                   full((1, D)), full((1, D))],
        out_shape=[jax.ShapeDtypeStruct((S, 2 * D), BF16), jax.ShapeDtypeStruct(ws.shape, F32),
                   jax.ShapeDtypeStruct(bsb.shape, F32), jax.ShapeDtypeStruct((1, D), F32),
                   jax.ShapeDtypeStruct((1, D), F32)],
        scratch_shapes=[pltpu.VMEM((ts, D), BF16), pltpu.VMEM((ts, D), F32),
                        pltpu.VMEM(bsb.shape, F32), pltpu.VMEM((ts, D), F32)],
        compiler_params=_cp("arbitrary"), name=name)(dh, w_out, zp, lng, lnb, ws, wst, bsb)


def _group_sum(x, groups, name):
    P, D = x.shape
    gd = D // groups

    def body(x_ref, o_ref):
        for g in range(groups):
            o_ref[:, g:g + 1] = jnp.sum(x_ref[:, g * gd:(g + 1) * gd], axis=1, keepdims=True)

    return pl.pallas_call(body, out_shape=jax.ShapeDtypeStruct((P, groups), F32), name=name)(x)


def _adamw_small(ws, gs, ms, vs, name):
    n = len(ws)
    shapes = [w.shape for w in ws]
    flat = [(w.size // w.shape[-1], w.shape[-1]) for w in ws]

    def body(*refs):
        for k in range(n):
            w_ref, g_ref, m_ref, v_ref = (refs[j * n + k] for j in range(4))
            d_ref, nm_ref, nv_ref = (refs[(4 + j) * n + k] for j in range(3))
            d_ref[...], nm_ref[...], nv_ref[...] = _adamw_math(w_ref[...], g_ref[...], m_ref[...], v_ref[...])

    outs = pl.pallas_call(
        body, out_shape=[jax.ShapeDtypeStruct(f, F32) for f in flat] * 3, name=name)(
            *(a.reshape(f) for group in (ws, gs, ms, vs) for a, f in zip(group, flat)))
    return [tuple(outs[j * n + k].reshape(shapes[k]) for j in range(3)) for k in range(n)]


_HBM = pl.BlockSpec(memory_space=pltpu.HBM)


def _remote(src, dst, send_sem, recv_sem, to):
    return pltpu.make_async_remote_copy(src_ref=src, dst_ref=dst, send_sem=send_sem, recv_sem=recv_sem,
                                        device_id=to, device_id_type=MESH)


def _all_gather(block, name):
    R, C = block.shape

    def body(x_ref, out_ref, send_sems, recv_sems, local_sem):
        x, y, c = lax.axis_index("x"), lax.axis_index("y"), lax.axis_index("c")
        me, sibling = (x, y, c), (x, y, 1 - c)
        chips = [(1 - x, y), (x, 1 - y), (1 - x, 1 - y)]

        def slot(px, py, pc):
            return out_ref.at[4 * px + 2 * py + pc]

        def copy(k, blk, to, src=None):
            return _remote(slot(*blk) if src is None else src, slot(*blk), send_sems.at[k], recv_sems.at[k], to)

        mine = pltpu.make_async_copy(x_ref, slot(*me), local_sem)
        mine.start()
        first = [copy(0, me, sibling, src=x_ref)]
        first += [copy(1 + j, me, (*chip, c), src=x_ref) for j, chip in enumerate(chips)]
        for cp in first:
            cp.start()
        passed = [copy(4 + j, (*chip, c), sibling) for j, chip in enumerate(chips)]
        for j, chip in enumerate(chips):
            copy(1 + j, (*chip, c), me).wait_recv()
            passed[j].start()
        copy(0, sibling, me).wait_recv()
        for j, chip in enumerate(chips):
            copy(4 + j, (*chip, 1 - c), me).wait_recv()
        for cp in first + passed:
            cp.wait_send()
        mine.wait()

    return pl.pallas_call(
        body, out_shape=jax.ShapeDtypeStruct((NDEV, R, C), block.dtype), in_specs=[_HBM], out_specs=_HBM,
        scratch_shapes=[pltpu.SemaphoreType.DMA((7,)), pltpu.SemaphoreType.DMA((7,)), pltpu.SemaphoreType.DMA],
        name=name)(block)


def _all_gather_weights(pack, rows, name, after=None):
    C = pack.shape[1]
    nw = len(rows)
    starts = [sum(rows[:w]) for w in range(nw)]
    after_specs, after_ops = _after(after)

    def body(pack_ref, *rest):
        rest = rest[len(after_ops):]
        outs = rest[:nw]
        send_sems, recv_sems, local_sem = rest[nw:]
        x, y, c = lax.axis_index("x"), lax.axis_index("y"), lax.axis_index("c")
        me, sibling = (x, y, c), (x, y, 1 - c)
        chips = [(1 - x, y), (x, 1 - y), (1 - x, 1 - y)]

        def block(w, px, py, pc):
            return outs[w].at[pl.ds((4 * px + 2 * py + pc) * rows[w], rows[w])]

        def mine(w):
            return pack_ref.at[pl.ds(starts[w], rows[w])]

        def all_of(k):
            return _remote(pack_ref, pack_ref, send_sems.at[k], recv_sems.at[k], me)

        for w in range(nw):
            pltpu.make_async_copy(mine(w), block(w, *me), local_sem).start()
        for k, to in enumerate([sibling] + [(*chip, c) for chip in chips]):
            for w in range(nw):
                _remote(mine(w), block(w, *me), send_sems.at[k], recv_sems.at[k], to).start()
        for j, chip in enumerate(chips):
            all_of(1 + j).wait_recv()
            for w in range(nw):
                _remote(block(w, *chip, c), block(w, *chip, c), send_sems.at[4 + j], recv_sems.at[4 + j], sibling).start()
        all_of(0).wait_recv()
        for j in range(3):
            all_of(4 + j).wait_recv()
        for k in range(7):
            all_of(k).wait_send()
        pltpu.make_async_copy(pack_ref, pack_ref, local_sem).wait()

    return pl.pallas_call(
        body, out_shape=[jax.ShapeDtypeStruct((NDEV * r, C), pack.dtype) for r in rows],
        in_specs=[_HBM] + after_specs, out_specs=[_HBM] * nw,
        scratch_shapes=[pltpu.SemaphoreType.DMA((7,)), pltpu.SemaphoreType.DMA((7,)), pltpu.SemaphoreType.DMA],
        name=name)(pack, *after_ops)


_SEM = pl.BlockSpec(memory_space=pltpu.SEMAPHORE)
_DATAFLOW = pltpu.SideEffectType.DATAFLOW_SIDE_EFFECTING


def _split_start(srcs, lands, plan, n, after, name):
    nbuf = len(srcs) + len(lands)
    after_specs, after_ops = _after(after)

    def body(*refs):
        src_refs, land_refs = refs[:len(srcs)], refs[len(srcs):nbuf]
        send_sems, recv_sems = refs[nbuf + len(after_ops)], refs[nbuf + len(after_ops) + 1]
        for k, (src, dst, to) in enumerate(plan(src_refs, land_refs)):
            _remote(src, dst, send_sems.at[k], recv_sems.at[k], to).start()
        refs[-1][...] = jnp.zeros_like(refs[-1])

    bufs = [pltpu.with_memory_space_constraint(a, pltpu.HBM) for a in list(srcs) + list(lands)]
    outs = pl.pallas_call(
        body, name=name,
        out_shape=(pltpu.SemaphoreType.DMA((n,)), pltpu.SemaphoreType.DMA((n,)),
                   *[pltpu.HBM(a.shape, a.dtype) for a in bufs], jax.ShapeDtypeStruct((8, 128), F32)),
        in_specs=[_HBM] * nbuf + after_specs,
        out_specs=(_SEM, _SEM, *[_HBM] * nbuf, pl.BlockSpec(memory_space=pltpu.VMEM)),
        input_output_aliases={i: 2 + i for i in range(nbuf)},
        compiler_params=pltpu.CompilerParams(has_side_effects=_DATAFLOW))(*bufs, *after_ops)
    return outs[0], outs[1], list(outs[2:2 + len(srcs)]), list(outs[2 + len(srcs):2 + nbuf]), outs[-1]


def _split_wait(send_sems, recv_sems, srcs, lands, plan, after, name):
    nbuf = len(srcs) + len(lands)
    after_specs, after_ops = _after(after)

    def body(*refs):
        src_refs, land_refs = refs[:len(srcs)], refs[len(srcs):nbuf]
        send_sems_ref, recv_sems_ref = refs[nbuf], refs[nbuf + 1]
        for k, (src, dst, to) in enumerate(plan(src_refs, land_refs)):
            copy = _remote(src, dst, send_sems_ref.at[k], recv_sems_ref.at[k], to)
            copy.wait_send()
            copy.wait_recv()

    outs = pl.pallas_call(
        body, name=name, out_shape=tuple(pltpu.HBM(a.shape, a.dtype) for a in list(srcs) + list(lands)),
        in_specs=[_HBM] * nbuf + [_SEM, _SEM] + after_specs, out_specs=tuple([_HBM] * nbuf),
        input_output_aliases={i: i for i in range(nbuf)},
        compiler_params=pltpu.CompilerParams(has_side_effects=_DATAFLOW))(*srcs, *lands, send_sems, recv_sems, *after_ops)
    return list(outs[:len(srcs)]), list(outs[len(srcs):])


def _peers(x, y, c):
    return [(mask, (1 - x if mask & 4 else x, 1 - y if mask & 2 else y, 1 - c if mask & 1 else c))
            for mask in range(1, NDEV)]


def _gather_plan(rows):
    starts = [sum(rows[:w]) for w in range(len(rows))]

    def plan(src_refs, land_refs):
        x, y, c = lax.axis_index("x"), lax.axis_index("y"), lax.axis_index("c")
        copies = []
        for w, r in enumerate(rows):
            mine = src_refs[0].at[pl.ds(starts[w], r)]
            dst = land_refs[w].at[pl.ds((4 * x + 2 * y + c) * r, r)]
            copies += [(mine, dst, peer) for _, peer in _peers(x, y, c)]
        return copies

    return plan, (NDEV - 1) * len(rows)


def _place_own(shards, fulls, dev_idx, name):
    nw = len(shards)

    def body(i_ref, *refs):
        for w in range(nw):
            refs[2 * nw + w][...] = refs[w][...]

    grid_spec = pltpu.PrefetchScalarGridSpec(
        num_scalar_prefetch=1, grid=(1,),
        in_specs=[pl.BlockSpec(s.shape, lambda t, i_ref: (0, 0)) for s in shards] + [_ANY] * nw,
        out_specs=[pl.BlockSpec(s.shape, lambda t, i_ref: (i_ref[0], 0)) for s in shards])
    outs = pl.pallas_call(
        body, grid_spec=grid_spec, out_shape=[jax.ShapeDtypeStruct(f.shape, f.dtype) for f in fulls],
        input_output_aliases={1 + nw + w: w for w in range(nw)}, name=name)(dev_idx, *shards, *fulls)
    return list(outs)


def _scatter_plan(rows):
    def plan(src_refs, land_refs):
        x, y, c = lax.axis_index("x"), lax.axis_index("y"), lax.axis_index("c")
        copies = []
        for w, r in enumerate(rows):
            for mask, (px, py, pc) in _peers(x, y, c):
                src = src_refs[w].at[pl.ds((4 * px + 2 * py + pc) * r, r)]
                copies.append((src, land_refs[w].at[mask - 1], (px, py, pc)))
        return copies

    return plan, (NDEV - 1) * len(rows)


def _adamw_math(w, g, m, v):
    nm = ADAM_B1 * m + (1.0 - ADAM_B1) * g
    nv = ADAM_B2 * v + (1.0 - ADAM_B2) * (g * g)
    bc1 = 1.0 - ADAM_B1 ** ADAM_STEP
    bc2 = 1.0 - ADAM_B2 ** ADAM_STEP
    return -ADAM_LR * ((nm / bc1) / (jnp.sqrt(nv / bc2) + ADAM_EPS) + ADAM_WD * w), nm, nv


def _finish_weight(gs, gots, dev_idx, w, m, v, name, after=None):
    L = len(gs)
    n1, r, C = gots[0].shape
    block = (None,) + w.shape[1:]

    after_specs, after_ops = _after(after)

    def body(i_ref, *refs):
        ins, (w_ref, m_ref, v_ref), (g_out, d_out, m_out, v_out) = refs[:2 * L], refs[2 * L:2 * L + 3], refs[-4:]
        for layer in range(L):
            @pl.when(pl.program_id(0) == layer)
            def _():
                g_ref, got_ref = ins[2 * layer], ins[2 * layer + 1]
                acc = g_ref[...].astype(F32)
                for k in range(n1):
                    acc = acc + got_ref[k].astype(F32)
                g_out[...] = acc
                d_out[...], m_out[...], v_out[...] = _adamw_math(w_ref[...], acc, m_ref[...], v_ref[...])

    in_specs, ins = [], []
    for g, got in zip(gs, gots):
        in_specs += [pl.BlockSpec((r, C), lambda t, i_ref: (i_ref[0], 0), pipeline_mode=_RESIDENT),
                     pl.BlockSpec((n1, r, C), lambda t, i_ref: (0, 0, 0), pipeline_mode=_RESIDENT)]
        ins += [g, got]
    per_layer = pl.BlockSpec(block, lambda t, i_ref: (t, 0, 0))
    grid_spec = pltpu.PrefetchScalarGridSpec(
        num_scalar_prefetch=1, grid=(L,), in_specs=in_specs + [per_layer] * 3 + after_specs,
        out_specs=[per_layer] * 4)
    return pl.pallas_call(
        body, grid_spec=grid_spec, out_shape=[jax.ShapeDtypeStruct(w.shape, F32)] * 4,
        compiler_params=_cp("arbitrary"), name=name)(dev_idx, *ins, w, m, v, *after_ops)


def _sum_slots(a, name):
    n, R, C = a.shape

    def body(a_ref, o_ref):
        acc = a_ref[0]
        for k in range(1, n):
            acc = acc + a_ref[k]
        o_ref[...] = acc

    return pl.pallas_call(body, out_shape=jax.ShapeDtypeStruct((R, C), F32), name=name)(a)


def _shard_axis(name):
    return {"ev_w_in": 2, "ev_a_conv_w": 2, "ev_b_conv_w": 2, "ev_w_out": 1, "od_w_in": 2, "od_c_ln_g": 1,
            "od_c_ln_b": 1, "od_w_out": 1, "xa_w_q": 1, "xa_w_k": 1, "xa_w_v": 1, "xa_w_o": 1,
            "ffn_w_gate": 2, "ffn_w_up": 2, "ffn_w_down": 1}[name]


BIG = ["ev_w_in", "ev_w_out", "od_w_in", "od_w_out", "xa_w_q", "xa_w_k", "xa_w_v", "xa_w_o",
       "ffn_w_gate", "ffn_w_up", "ffn_w_down"]
SMALL_SHARDED = ["ev_a_conv_w", "ev_b_conv_w", "od_c_ln_g", "od_c_ln_b"]
REPLICATED = ["g_mix", "g_xattn", "g_mem", "g_ffn", "g_final", "ev_a_conv_b", "ev_a_ln_g", "ev_a_ln_b",
              "ev_b_conv_b", "od_w_s", "od_b_s"]
WEIGHTS = ["g_mix", "g_xattn", "g_mem", "g_ffn", "g_final", "ev_w_in", "ev_a_conv_w", "ev_a_conv_b", "ev_a_ln_g",
           "ev_a_ln_b", "ev_b_conv_w", "ev_b_conv_b", "ev_w_out", "od_w_in", "od_c_ln_g", "od_c_ln_b", "od_w_s",
           "od_b_s", "od_w_out", "xa_w_q", "xa_w_k", "xa_w_v", "xa_w_o", "ffn_w_gate", "ffn_w_up", "ffn_w_down"]


def _full_from_blocks(blocks, axis):
    shard = blocks.shape[1:]
    full = jnp.moveaxis(blocks, 0, axis)
    return full.reshape(shard[:axis] + (NDEV * shard[axis],) + shard[axis + 1:])


def _blocks_from_full(full, axis):
    shp = full.shape
    split = full.reshape(shp[:axis] + (NDEV, shp[axis] // NDEV) + shp[axis + 1:])
    return jnp.moveaxis(split, axis, 0)


def _pad_rows(flat, width, row_align):
    per = width * row_align
    n = -(-flat.shape[0] // per) * per
    return jnp.pad(flat, (0, n - flat.shape[0])).reshape(n // width, width)


def _row(v):
    return v.reshape(1, -1)


def _mem_kv(mem, g_m, wk, wv, name):
    M, D = mem.shape

    def body(mem_ref, g_ref, wk_ref, wv_ref, n_ref, k_ref, v_ref):
        x = mem_ref[...]
        r = lax.rsqrt(jnp.mean(x * x, axis=-1, keepdims=True) + RMS_EPS)
        n = ((x * r) * g_ref[...]).astype(BF16)
        n_ref[...] = n
        k_ref[...] = _dot(n, wk_ref[...]).astype(BF16)
        v_ref[...] = _dot(n, wv_ref[...]).astype(BF16)

    sh = jax.ShapeDtypeStruct((M, D), BF16)
    return pl.pallas_call(body, out_shape=[sh, sh, sh], name=name)(mem, g_m, wk, wv)


def _mem_kv_bwd(dk, dv, mem, mem_n, g_m, wk, wv, name):
    M, D = mem.shape

    def body(dk_ref, dv_ref, mem_ref, n_ref, g_ref, wk_ref, wv_ref, dwk_ref, dwv_ref, dg_ref):
        dk_, dv_ = dk_ref[...].astype(BF16), dv_ref[...].astype(BF16)
        n = n_ref[...]
        dwk_ref[...] = _dot_tn(n, dk_).astype(BF16)
        dwv_ref[...] = _dot_tn(n, dv_).astype(BF16)
        dn = _dot_nt(dk_, wk_ref[...]) + _dot_nt(dv_, wv_ref[...])
        x = mem_ref[...]
        r = lax.rsqrt(jnp.mean(x * x, axis=-1, keepdims=True) + RMS_EPS)
        dg_ref[...] = jnp.sum(dn * (x * r), axis=0, keepdims=True)

    wsh = jax.ShapeDtypeStruct((D, D), BF16)
    return pl.pallas_call(body, out_shape=[wsh, wsh, jax.ShapeDtypeStruct((1, D), F32)], name=name)(
        dk, dv, mem, mem_n, g_m, wk, wv)


def _xattn_fwd(h, nq, mem, g_m, wq, wk, wv, wo, g_next, tag, after):
    q = _mm([(nq, wq, "nn")], f"xa_q_{tag}", out_dtype=BF16, after=after)
    mem_n, k, v = _mem_kv(mem, _row(g_m), wk, wv, f"xa_mem_{tag}")
    o = _attn_fwd(q, k, v, f"xa_attn_{tag}")
    h_new, n_next = _mm([(o, wo, "nn")], f"xa_o_{tag}", res=h, rms_g=_row(g_next))
    return h_new, n_next, (h, nq, mem_n, q, k, v, o)


def _xattn_bwd(dh_new, saved, mem, g_x, g_m, wq, wk, wv, wo, tag, push):
    h, nq, mem_n, q, k, v, o = saved
    do = _mm([(dh_new, wo, "nt")], f"xa_do_{tag}", out_dtype=BF16)
    d_wo = _mm_tn(o, dh_new, f"xa_dwo_{tag}")
    dq, dk, dv = _attn_bwd(q, k, v, do, f"xa_attn_bwd_{tag}")
    d_wq = _mm_tn(nq, dq, f"xa_dwq_{tag}")
    d_wk, d_wv, d_gm = _mem_kv_bwd(dk, dv, mem, mem_n, _row(g_m), wk, wv, f"xa_mem_bwd_{tag}")
    token = push([d_wq, d_wk, d_wv, d_wo])
    dh, d_gx = _mm([(dq, wq, "nt")], f"xa_dnq_{tag}", rms_bwd=(h, _row(g_x), dh_new), tm=1024, after=token)
    return dh, dict(g_xattn=d_gx, g_mem=d_gm)


def _ffn_fwd(h, n, wgt, wut, wd, g_next, tag, after, loss=None):
    a, b, hid = _ffn_up(n, wgt, wut, f"ffn_up_{tag}", after=after)
    saved = (h, n, a, b, hid)
    if loss is not None:
        return _mm([(hid, wd, "nn")], f"ffn_down_{tag}", res=h, loss=loss, tm=1024), saved
    h_new, n_next = _mm([(hid, wd, "nn")], f"ffn_down_{tag}", res=h, rms_g=_row(g_next), tm=1024)
    return h_new, n_next, saved


def _ffn_bwd(dh_new, saved, g_f, wgt, wut, wd, tag, push):
    h, n, a, b, hid = saved
    da, db = _ffn_dhid(dh_new, wd, a, b, f"ffn_dhid_{tag}")
    d_wd = _mm_tn(hid, dh_new, f"ffn_dwd_{tag}", ts=1024, tn=1024)
    d_wgt = _mm_tn(da, n, f"ffn_dwg_{tag}", ts=1024, tn=1024)
    d_wut = _mm_tn(db, n, f"ffn_dwu_{tag}", ts=1024, tn=1024)
    token = push([d_wgt, d_wut, d_wd])
    dh, d_gf = _mm([(da, wgt, "nn"), (db, wut, "nn")], f"ffn_dn_{tag}", rms_bwd=(h, _row(g_f), dh_new), tm=512,
                   after=token)
    return dh, dict(g_ffn=d_gf)


_XA = ["xa_w_q", "xa_w_k", "xa_w_v", "xa_w_o"]
_FFN = ["ffn_w_gate", "ffn_w_up", "ffn_w_down"]
GATHERS = {
    "ev_in": [("ev_w_in", 0)],
    "xa0": [("ev_w_out", 0)] + [(n, 0) for n in _XA],
    "ffn0": [(n, 0) for n in _FFN],
    "od": [("od_w_in", 0), ("od_w_out", 0)],
    "xa1": [(n, 1) for n in _XA],
    "ffn1": [(n, 1) for n in _FFN],
}
SCATTERS = {
    "ffn1": [(n, 1) for n in _FFN],
    "xa1": [(n, 1) for n in _XA],
    "od": [("od_w_in", 0), ("od_w_out", 0)],
    "ffn0": [(n, 0) for n in _FFN],
    "xa0": [(n, 0) for n in _XA],
    "ev_out": [("ev_w_out", 0)],
    "ev_in": [("ev_w_in", 0)],
}


def _local_step(x, mem, loss_target, W, comm):
    grads = {}

    h0 = x
    (ev_w_in_t,), token = comm.weights("ev_in", None)
    n0 = _rms_fwd(h0, _row(W["g_mix"][0]), "ev_rms", after=token)
    z = _mm([(n0, ev_w_in_t, "nt")], "ev_in", tn=1280)
    token = comm.prefetch(["ffn0"], z)
    ab, ca = _conv_fwd(z, W["ev_a_conv_w"][0], W["ev_a_conv_b"], W["ev_a_ln_g"], W["ev_a_ln_b"],
                       W["ev_b_conv_w"][0], W["ev_b_conv_b"], "ev_conv", after=token)
    (ev_w_out, *xa_w0), _ = comm.weights("xa0", ab)
    h1, nq0 = _mm([(ab, ev_w_out, "nn")], "ev_out", res=h0, rms_g=_row(W["g_xattn"][0]))
    token = comm.prefetch(["od", "xa1"], nq0)
    h2, nf0, xa0 = _xattn_fwd(h1, nq0, mem, W["g_mem"][0], *xa_w0, W["g_ffn"][0], "l0", token)
    ffn_w0, _ = comm.weights("ffn0", nf0)
    token = comm.prefetch(["ffn1"], nf0)
    h3, n3, ff0 = _ffn_fwd(h2, nf0, *ffn_w0, W["g_mix"][1], "l0", token)

    (od_w_in_t, od_w_out), _ = comm.weights("od", n3)
    zp = _mm([(n3, od_w_in_t, "nt")], "od_in", tn=1024)
    D = x.shape[1]
    ws = W["od_w_s"][0].astype(BF16)
    wst = jnp.swapaxes(ws, 1, 2)
    bsb = jnp.repeat(jnp.transpose(W["od_b_s"][0]), D // C_GROUPS, axis=1)
    y_sgu = _sgu_fwd(zp, W["od_c_ln_g"], W["od_c_ln_b"], ws, bsb, "od_sgu")
    h4, nq1 = _mm([(y_sgu, od_w_out, "nn")], "od_out", res=h3, rms_g=_row(W["g_xattn"][1]))
    xa_w1, _ = comm.weights("xa1", nq1)
    h5, nf1, xa1 = _xattn_fwd(h4, nq1, mem, W["g_mem"][1], *xa_w1, W["g_ffn"][1], "l1", None)
    ffn_w1, _ = comm.weights("ffn1", nf1)
    (loss_row, dh6, d_gfinal), ff1 = _ffn_fwd(h5, nf1, *ffn_w1, None, "l1", None,
                                              loss=(_row(W["g_final"]), loss_target))
    grads["g_final"] = d_gfinal.reshape(-1)


    dh5, g_ff1 = _ffn_bwd(dh6, ff1, W["g_ffn"][1], *ffn_w1, "l1", lambda dws: comm.grads("ffn1", dws))
    dh4, g_xa1 = _xattn_bwd(dh5, xa1, mem, W["g_xattn"][1], W["g_mem"][1], *xa_w1, "l1",
                            lambda dws: comm.grads("xa1", dws))
    d_od_out = _mm_tn(y_sgu, dh4, "od_dwout", tn=1024)
    dzp, d_ws, d_bsb, d_clng, d_clnb = _sgu_bwd(dh4, od_w_out, zp, W["od_c_ln_g"], W["od_c_ln_b"], ws, wst, bsb,
                                                "od_sgu_bwd")
    grads["od_w_s"] = d_ws[None]
    grads["od_b_s"] = jnp.transpose(_group_sum(d_bsb, C_GROUPS, "od_dbs"))[None]
    grads["od_c_ln_g"], grads["od_c_ln_b"] = d_clng, d_clnb
    token = comm.grads("od", [_mm_tn(dzp, n3, "od_dwin", ts=1024, tn=1024), d_od_out])
    dh3, d_gmix1 = _mm([(dzp, od_w_in_t, "nn")], "od_dn", rms_bwd=(h3, _row(W["g_mix"][1]), dh4), tm=1024, after=token)

    dh2, g_ff0 = _ffn_bwd(dh3, ff0, W["g_ffn"][0], *ffn_w0, "l0", lambda dws: comm.grads("ffn0", dws))
    dh1, g_xa0 = _xattn_bwd(dh2, xa0, mem, W["g_xattn"][0], W["g_mem"][0], *xa_w0, "l0",
                            lambda dws: comm.grads("xa0", dws))
    token = comm.grads("ev_out", [_mm_tn(ab, dh1, "ev_dwout", tn=1024)])
    dca, db, d_lng, d_lnb, d_ba = _conv_dab_ln(dh1, ev_w_out, ca, W["ev_a_ln_g"], W["ev_a_ln_b"], "ev_dab",
                                               after=token)
    dz, d_wa, d_wb, d_bb = _conv_bwd(z, dca, db, W["ev_a_conv_w"][0], W["ev_b_conv_w"][0], W["ev_b_conv_b"],
                                     "ev_conv_bwd")
    grads.update(ev_a_ln_g=d_lng, ev_a_ln_b=d_lnb, ev_a_conv_b=d_ba, ev_b_conv_b=d_bb,
                 ev_a_conv_w=d_wa[None], ev_b_conv_w=d_wb[None])
    token = comm.grads("ev_in", [_mm_tn(dz, n0, "ev_dwin", ts=1024, tn=1024)])
    grad_x, d_gmix0 = _mm([(dz, ev_w_in_t, "nn")], "ev_dn", rms_bwd=(h0, _row(W["g_mix"][0]), dh1), tm=1024, after=token)

    grads["g_mix"] = jnp.concatenate([d_gmix0, d_gmix1], axis=0)
    for key in ("g_xattn", "g_mem"):
        grads[key] = jnp.concatenate([g_xa0[key], g_xa1[key]], axis=0)
    grads["g_ffn"] = jnp.concatenate([g_ff0["g_ffn"], g_ff1["g_ffn"]], axis=0)
    return loss_row, grad_x, grads


class _Exchanges:
    def __init__(self, shards, dev_idx, after):
        self.shards, self.dev_idx = shards, dev_idx
        self.gathering, self.scattering = {}, {}
        self.first = _all_gather_weights(self._pack(GATHERS["ev_in"]), self._rows(GATHERS["ev_in"]), "ag_ev_in",
                                         after=after)
        self.first_token = self.prefetch(["xa0"], self.first[0])

    def _rows(self, entries):
        return [self.shards[e].shape[0] for e in entries]

    def _pack(self, entries):
        return jnp.concatenate([self.shards[e] for e in entries], axis=0)

    def prefetch(self, gathers, after):
        for name in gathers:
            rows = self._rows(GATHERS[name])
            pack = self._pack(GATHERS[name])
            lands = [lax.empty((NDEV * r, pack.shape[1]), pack.dtype) for r in rows]
            plan, n = _gather_plan(rows)
            send, recv, srcs, lands, after = _split_start([pack], lands, plan, n, after, f"ag_{name}_start")
            self.gathering[name] = (send, recv, srcs, lands, plan, rows)
        return after

    def weights(self, name, after):
        if name == "ev_in":
            return self.first, self.first_token
        send, recv, srcs, lands, plan, rows = self.gathering.pop(name)
        _, lands = _split_wait(send, recv, srcs, lands, plan, after, f"ag_{name}_wait")
        return _place_own([self.shards[e] for e in GATHERS[name]], lands, self.dev_idx, f"ag_{name}_own"), None

    def grads(self, name, dws):
        rows = self._rows(SCATTERS[name])
        lands = [lax.empty((NDEV - 1, r, d.shape[1]), d.dtype) for r, d in zip(rows, dws)]
        plan, n = _scatter_plan(rows)
        send, recv, srcs, lands, token = _split_start(dws, lands, plan, n, None, f"rs_{name}_start")
        self.scattering[name] = (send, recv, srcs, lands, plan)
        return token

    def received(self, after):
        out = {}
        for name, (send, recv, srcs, lands, plan) in self.scattering.items():
            srcs, lands = _split_wait(send, recv, srcs, lands, plan, after, f"rs_{name}_wait")
            for entry, g, got in zip(SCATTERS[name], srcs, lands):
                out[entry] = (g, got)
        return out


def kernel(x, mem, g_mix, g_xattn, g_mem, g_ffn, g_final, ev_w_in, ev_a_conv_w, ev_a_conv_b, ev_a_ln_g, ev_a_ln_b, ev_b_conv_w, ev_b_conv_b, ev_w_out, od_w_in, od_c_ln_g, od_c_ln_b, od_w_s, od_b_s, od_w_out, xa_w_q, xa_w_k, xa_w_v, xa_w_o, ffn_w_gate, ffn_w_up, ffn_w_down, loss_target, m_g_mix, m_g_xattn, m_g_mem, m_g_ffn, m_g_final, m_ev_w_in, m_ev_a_conv_w, m_ev_a_conv_b, m_ev_a_ln_g, m_ev_a_ln_b, m_ev_b_conv_w, m_ev_b_conv_b, m_ev_w_out, m_od_w_in, m_od_c_ln_g, m_od_c_ln_b, m_od_w_s, m_od_b_s, m_od_w_out, m_xa_w_q, m_xa_w_k, m_xa_w_v, m_xa_w_o, m_ffn_w_gate, m_ffn_w_up, m_ffn_w_down, v_g_mix, v_g_xattn, v_g_mem, v_g_ffn, v_g_final, v_ev_w_in, v_ev_a_conv_w, v_ev_a_conv_b, v_ev_a_ln_g, v_ev_a_ln_b, v_ev_b_conv_w, v_ev_b_conv_b, v_ev_w_out, v_od_w_in, v_od_c_ln_g, v_od_c_ln_b, v_od_w_s, v_od_b_s, v_od_w_out, v_xa_w_q, v_xa_w_k, v_xa_w_v, v_xa_w_o, v_ffn_w_gate, v_ffn_w_up, v_ffn_w_down):
    local = dict(g_mix=g_mix, g_xattn=g_xattn, g_mem=g_mem, g_ffn=g_ffn, g_final=g_final, ev_w_in=ev_w_in, ev_a_conv_w=ev_a_conv_w, ev_a_conv_b=ev_a_conv_b, ev_a_ln_g=ev_a_ln_g, ev_a_ln_b=ev_a_ln_b, ev_b_conv_w=ev_b_conv_w, ev_b_conv_b=ev_b_conv_b, ev_w_out=ev_w_out, od_w_in=od_w_in, od_c_ln_g=od_c_ln_g, od_c_ln_b=od_c_ln_b, od_w_s=od_w_s, od_b_s=od_b_s, od_w_out=od_w_out, xa_w_q=xa_w_q, xa_w_k=xa_w_k, xa_w_v=xa_w_v, xa_w_o=xa_w_o, ffn_w_gate=ffn_w_gate, ffn_w_up=ffn_w_up, ffn_w_down=ffn_w_down)
    mom = dict(g_mix=m_g_mix, g_xattn=m_g_xattn, g_mem=m_g_mem, g_ffn=m_g_ffn, g_final=m_g_final, ev_w_in=m_ev_w_in, ev_a_conv_w=m_ev_a_conv_w, ev_a_conv_b=m_ev_a_conv_b, ev_a_ln_g=m_ev_a_ln_g, ev_a_ln_b=m_ev_a_ln_b, ev_b_conv_w=m_ev_b_conv_w, ev_b_conv_b=m_ev_b_conv_b, ev_w_out=m_ev_w_out, od_w_in=m_od_w_in, od_c_ln_g=m_od_c_ln_g, od_c_ln_b=m_od_c_ln_b, od_w_s=m_od_w_s, od_b_s=m_od_b_s, od_w_out=m_od_w_out, xa_w_q=m_xa_w_q, xa_w_k=m_xa_w_k, xa_w_v=m_xa_w_v, xa_w_o=m_xa_w_o, ffn_w_gate=m_ffn_w_gate, ffn_w_up=m_ffn_w_up, ffn_w_down=m_ffn_w_down)
    vel = dict(g_mix=v_g_mix, g_xattn=v_g_xattn, g_mem=v_g_mem, g_ffn=v_g_ffn, g_final=v_g_final, ev_w_in=v_ev_w_in, ev_a_conv_w=v_ev_a_conv_w, ev_a_conv_b=v_ev_a_conv_b, ev_a_ln_g=v_ev_a_ln_g, ev_a_ln_b=v_ev_a_ln_b, ev_b_conv_w=v_ev_b_conv_w, ev_b_conv_b=v_ev_b_conv_b, ev_w_out=v_ev_w_out, od_w_in=v_od_w_in, od_c_ln_g=v_od_c_ln_g, od_c_ln_b=v_od_c_ln_b, od_w_s=v_od_w_s, od_b_s=v_od_b_s, od_w_out=v_od_w_out, xa_w_q=v_xa_w_q, xa_w_k=v_xa_w_k, xa_w_v=v_xa_w_v, xa_w_o=v_xa_w_o, ffn_w_gate=v_ffn_w_gate, ffn_w_up=v_ffn_w_up, ffn_w_down=v_ffn_w_down)
    D = x.shape[-1]
    dev = 4 * lax.axis_index("x") + 2 * lax.axis_index("y") + lax.axis_index("c")

    def comm_layout(n, a):
        return jnp.transpose(a) if _shard_axis(n) == 2 else a

    shards = {(n, i): comm_layout(n, local[n][i]).astype(BF16) for n in BIG for i in range(local[n].shape[0])}
    small_sizes = [local[n].size for n in SMALL_SHARDED]
    small_block = _pad_rows(jnp.concatenate([local[n].reshape(-1) for n in SMALL_SHARDED]), 128, 8)
    small_all = _all_gather(small_block, "ag_small")
    comm = _Exchanges(shards, jnp.reshape(dev, (1,)).astype(jnp.int32), small_all)
    small_all = small_all.reshape(NDEV, -1)

    W = {n: local[n] for n in REPLICATED}
    o0 = 0
    for n, sz in zip(SMALL_SHARDED, small_sizes):
        blocks = small_all[:, o0:o0 + sz].reshape((NDEV,) + local[n].shape)
        W[n] = _full_from_blocks(blocks, _shard_axis(n))
        o0 += sz

    loss_row, grad_x, grads = _local_step(x[0], mem[0], loss_target[0], W, comm)

    received = comm.received(grad_x)
    rest = REPLICATED + SMALL_SHARDED
    rest_full_shapes = [grads[n].shape for n in rest]
    g_rest = _pad_rows(jnp.concatenate([grads[n].astype(F32).reshape(-1) for n in rest]), D, 8)
    small_rows = g_rest.shape[0]
    small_plan, small_n = _gather_plan([small_rows])
    small_send, small_recv, small_srcs, small_lands, token = _split_start(
        [g_rest], [lax.empty((NDEV * small_rows, D), F32)], small_plan, small_n, received["ev_w_in", 0][1],
        "ag_small_grads_start")

    gsh, delta, new_m, new_v = {}, {}, {}, {}
    def stacked_layout(n, a):
        return jnp.swapaxes(a, 1, 2) if _shard_axis(n) == 2 else a

    for n in BIG:
        parts = [received[n, i] for i in range(local[n].shape[0])]
        outs = _finish_weight([p[0] for p in parts], [p[1] for p in parts], comm.dev_idx,
                              *(stacked_layout(n, a) for a in (local[n], mom[n], vel[n])), f"finish_{n}", after=token)
        gsh[n], delta[n], new_m[n], new_v[n] = (stacked_layout(n, o) for o in outs)

    _, small_lands = _split_wait(small_send, small_recv, small_srcs, small_lands, small_plan,
                                 [delta[n] for n in BIG], "ag_small_grads_wait")
    partials = _place_own([g_rest], small_lands, comm.dev_idx, "ag_small_grads_own")[0]
    g_rest = _sum_slots(partials.reshape(NDEV, small_rows, D), "sum_small_grads").reshape(-1)
    o0 = 0
    for n, shp in zip(rest, rest_full_shapes):
        sz = 1
        for s in shp:
            sz *= s
        full = g_rest[o0:o0 + sz].reshape(shp)
        o0 += sz
        if n in SMALL_SHARDED:
            full = lax.dynamic_index_in_dim(_blocks_from_full(full, _shard_axis(n)), dev, 0, keepdims=False)
        gsh[n] = full.reshape(local[n].shape)

    small = _adamw_small([local[n] for n in rest], [gsh[n] for n in rest], [mom[n] for n in rest],
                         [vel[n] for n in rest], "adamw_small")
    for n, (d, nm, nv) in zip(rest, small):
        delta[n], new_m[n], new_v[n] = d, nm, nv

    loss = lax.psum(loss_row[0, 0], ("x", "y", "c"))
    return (loss, grad_x[None], *[gsh[n] for n in WEIGHTS], *[delta[n] for n in WEIGHTS],
            *[new_m[n] for n in WEIGHTS], *[new_v[n] for n in WEIGHTS])
```

```python
import jax
import jax.numpy as jnp
from jax import lax
from jax.experimental import pallas as pl
from jax.experimental.pallas import tpu as pltpu

F32, BF16 = jnp.float32, jnp.bfloat16
NDEV = 8
RMS_EPS = 1e-6
LN_EPS = 1e-5
CHUNK = 128
C_GROUPS = 8
XA_HEADS = 4
ADAM_LR, ADAM_B1, ADAM_B2, ADAM_EPS, ADAM_WD, ADAM_STEP = 0.001, 0.9, 0.999, 1e-08, 0.01, 10
HALO = 16
ROW_CHUNK = 32
ROW_CHUNK_FWD = 64
V7X_VMEM_LIMIT = 56 * 1024 * 1024
MESH = pl.DeviceIdType.MESH

TS_MM = 2048
TN_MM = 1408
TS_FFN = 512
MM_ROW_CHUNK = 256
FFN_COL_CHUNK = 256
TS_CONV = 512
TS_SGU = 512
TS_ATTN = 2048


def _cp(*sem):
    return pltpu.CompilerParams(dimension_semantics=sem, vmem_limit_bytes=V7X_VMEM_LIMIT)


def _pick(n, pref, align):
    for t in range(min(n, pref), 0, -1):
        if n % t == 0 and (t % align == 0 or t == n):
            return t
    return n


def _sigmoid(x):
    return 0.5 * jnp.tanh(0.5 * x) + 0.5


def _dot(a, b):
    return jnp.dot(a, b, preferred_element_type=F32)


def _dot_nt(a, b):
    return lax.dot_general(a, b, (((1,), (1,)), ((), ())), preferred_element_type=F32)


def _dot_tn(a, b):
    return lax.dot_general(a, b, (((0,), (0,)), ((), ())), preferred_element_type=F32)


_ANY = pl.BlockSpec(memory_space=pl.ANY)
_RESIDENT = pl.Buffered(1)


def _after(after):
    if after is None:
        return [], []
    ops = list(after) if isinstance(after, (list, tuple)) else [after]
    return [_ANY] * len(ops), ops


def _rms_fwd(h, g, name, after=None):
    S, D = h.shape
    ts = _pick(S, TS_MM, 16)
    after_specs, after_ops = _after(after)

    def body(h_ref, g_ref, *rest):
        o_ref = rest[-1]
        x = h_ref[...]
        r = lax.rsqrt(jnp.mean(x * x, axis=-1, keepdims=True) + RMS_EPS)
        o_ref[...] = ((x * r) * g_ref[...]).astype(o_ref.dtype)

    return pl.pallas_call(
        body, grid=(S // ts,),
        in_specs=[pl.BlockSpec((ts, D), lambda i: (i, 0)), pl.BlockSpec((1, D), lambda i: (0, 0))] + after_specs,
        out_specs=pl.BlockSpec((ts, D), lambda i: (i, 0)),
        out_shape=jax.ShapeDtypeStruct((S, D), BF16), compiler_params=_cp("parallel"), name=name)(h, g, *after_ops)


def _mm(pairs, name, out_dtype=F32, res=None, rms_g=None, rms_bwd=None, loss=None, tm=None, tn=None, after=None):
    M = pairs[0][0].shape[0]
    N = pairs[0][1].shape[1 if pairs[0][2] == "nn" else 0]
    whole_rows = rms_g is not None or rms_bwd is not None or loss is not None
    tm = _pick(M, tm or TS_MM, 16)
    tn = N if whole_rows else _pick(N, tn or TN_MM, 128)
    npair = len(pairs)
    modes = [p[2] for p in pairs]
    after_specs, after_ops = _after(after)

    rc = MM_ROW_CHUNK if whole_rows and tm % MM_ROW_CHUNK == 0 else tm

    def body(*refs):
        rest = refs[2 * npair + len(after_ops):]
        res_ref = None
        if res is not None:
            res_ref, rest = rest[0], rest[1:]
        if rms_bwd is not None:
            dg_ref = rest[4]

            @pl.when(pl.program_id(0) == 0)
            def _():
                dg_ref[...] = jnp.zeros_like(dg_ref)

        if loss is not None:
            g_ref, t_ref, loss_ref, dh_ref, dg_ref = rest

            @pl.when(pl.program_id(0) == 0)
            def _():
                dg_ref[...] = jnp.zeros_like(dg_ref)
                loss_ref[...] = jnp.zeros_like(loss_ref)

        for r0 in range(0, tm, rc):
            rows = pl.ds(r0, rc)
            acc = None
            for p in range(npair):
                a_ = refs[2 * p][rows, :].astype(BF16)
                d = _dot(a_, refs[2 * p + 1][...]) if modes[p] == "nn" else _dot_nt(a_, refs[2 * p + 1][...])
                acc = d if acc is None else acc + d
            if res_ref is not None:
                acc = acc + res_ref[rows, :]
            if rms_bwd is not None:
                h_ref, g_ref, dres_ref, dh_ref, _ = rest
                x = h_ref[rows, :]
                r = lax.rsqrt(jnp.mean(x * x, axis=-1, keepdims=True) + RMS_EPS)
                xr = x * r
                dg_ref[...] += jnp.sum(acc * xr, axis=0, keepdims=True)
                u = acc * g_ref[...]
                dh_ref[rows, :] = r * u - xr * (r * jnp.mean(u * xr, axis=-1, keepdims=True)) + dres_ref[rows, :]
            elif loss is not None:
                r = lax.rsqrt(jnp.mean(acc * acc, axis=-1, keepdims=True) + RMS_EPS)
                xr = acc * r
                gg = g_ref[...]
                e = xr * gg - t_ref[rows, :]
                chunk_loss = jnp.sum(jnp.sum(e * e, axis=0, keepdims=True), axis=1, keepdims=True) * (0.5 / N)
                loss_ref[...] += jnp.broadcast_to(chunk_loss, loss_ref.shape)
                dy = e * (1.0 / N)
                dg_ref[...] += jnp.sum(dy * xr, axis=0, keepdims=True)
                u = dy * gg
                dh_ref[rows, :] = r * u - xr * (r * jnp.mean(u * xr, axis=-1, keepdims=True))
            elif rms_g is not None:
                g_ref, o_ref, n_ref = rest
                o_ref[rows, :] = acc
                r = lax.rsqrt(jnp.mean(acc * acc, axis=-1, keepdims=True) + RMS_EPS)
                n_ref[rows, :] = ((acc * r) * g_ref[...]).astype(BF16)
            else:
                rest[0][rows, :] = acc.astype(rest[0].dtype)

    in_specs, ins = [], []
    for a, w, mode in pairs:
        K = a.shape[1]
        in_specs.append(pl.BlockSpec((tm, K), lambda i, j: (i, 0)))
        once = _RESIDENT if tn == N else None
        in_specs.append(pl.BlockSpec((K, tn), lambda i, j: (0, j), pipeline_mode=once) if mode == "nn"
                        else pl.BlockSpec((tn, K), lambda i, j: (j, 0), pipeline_mode=once))
        ins += [a, w]
    in_specs += after_specs
    ins += after_ops
    tile = pl.BlockSpec((tm, tn), lambda i, j: (i, j))
    vec = pl.BlockSpec((1, tn), lambda i, j: (0, j))
    if res is not None:
        in_specs.append(tile)
        ins.append(res)
    sem = ("parallel", "parallel")
    if rms_bwd is not None:
        in_specs += [tile, vec, tile]
        ins += list(rms_bwd)
        out_specs = [tile, vec]
        out_shape = [jax.ShapeDtypeStruct((M, N), F32), jax.ShapeDtypeStruct((1, N), F32)]
        sem = ("arbitrary", "arbitrary")
    elif loss is not None:
        in_specs += [vec, tile]
        ins += list(loss)
        out_specs = [pl.BlockSpec((1, 128), lambda i, j: (0, 0)), tile, vec]
        out_shape = [jax.ShapeDtypeStruct((1, 128), F32), jax.ShapeDtypeStruct((M, N), F32),
                     jax.ShapeDtypeStruct((1, N), F32)]
        sem = ("arbitrary", "arbitrary")
    elif rms_g is not None:
        in_specs.append(vec)
        ins.append(rms_g)
        out_specs = [tile, tile]
        out_shape = [jax.ShapeDtypeStruct((M, N), F32), jax.ShapeDtypeStruct((M, N), BF16)]
    else:
        out_specs = tile
        out_shape = jax.ShapeDtypeStruct((M, N), out_dtype)
    return pl.pallas_call(
        body, grid=(M // tm, N // tn), in_specs=in_specs, out_specs=out_specs, out_shape=out_shape,
        compiler_params=_cp(*sem), name=name)(*ins)


def _mm_tn(a, b, name, ts=None, tn=None):
    S, K = a.shape
    N = b.shape[1]
    ts = _pick(S, ts or TS_MM, 16)
    tn = _pick(N, tn or TN_MM, 128)
    nsteps = S // ts

    def body(a_ref, b_ref, o_ref, acc_ref):
        s = pl.program_id(1)

        @pl.when(s == 0)
        def _():
            acc_ref[...] = jnp.zeros_like(acc_ref)

        acc_ref[...] += _dot_tn(a_ref[...].astype(BF16), b_ref[...].astype(BF16))

        @pl.when(s == nsteps - 1)
        def _():
            o_ref[...] = acc_ref[...].astype(o_ref.dtype)

    return pl.pallas_call(
        body, grid=(N // tn, nsteps),
        in_specs=[pl.BlockSpec((ts, K), lambda j, s: (s, 0)), pl.BlockSpec((ts, tn), lambda j, s: (s, j))],
        out_specs=pl.BlockSpec((K, tn), lambda j, s: (0, j)), out_shape=jax.ShapeDtypeStruct((K, N), BF16),
        scratch_shapes=[pltpu.VMEM((K, tn), F32)],
        compiler_params=_cp("parallel", "arbitrary"), name=name)(a, b)


def _col_chunks(n):
    return [(c0, min(FFN_COL_CHUNK, n - c0)) for c0 in range(0, n, FFN_COL_CHUNK)]


def _ffn_up(n, wgt, wut, name, after=None):
    S, D = n.shape
    F = wgt.shape[0]
    tm = _pick(S, TS_FFN, 16)
    after_specs, after_ops = _after(after)

    def body(n_ref, wg_ref, wu_ref, *rest):
        a_ref, b_ref, hid_ref = rest[-3:]
        x = n_ref[...]
        for c0, ce in _col_chunks(F):
            a = _dot_nt(x, wg_ref[c0:c0 + ce, :])
            b = _dot_nt(x, wu_ref[c0:c0 + ce, :])
            a_ref[:, c0:c0 + ce] = a.astype(BF16)
            b_ref[:, c0:c0 + ce] = b.astype(BF16)
            hid_ref[:, c0:c0 + ce] = (a * _sigmoid(a) * b).astype(BF16)

    wspec = pl.BlockSpec((F, D), lambda i: (0, 0), pipeline_mode=_RESIDENT)
    ospec = pl.BlockSpec((tm, F), lambda i: (i, 0))
    osh = jax.ShapeDtypeStruct((S, F), BF16)
    return pl.pallas_call(
        body, grid=(S // tm,),
        in_specs=[pl.BlockSpec((tm, D), lambda i: (i, 0)), wspec, wspec] + after_specs,
        out_specs=[ospec, ospec, ospec], out_shape=[osh, osh, osh],
        compiler_params=_cp("parallel"), name=name)(n, wgt, wut, *after_ops)


def _ffn_dhid(dh, wd, a, b, name):
    S, D = dh.shape
    F = wd.shape[0]
    tm = _pick(S, TS_FFN, 16)

    def body(dh_ref, wd_ref, a_ref, b_ref, da_ref, db_ref):
        x = dh_ref[...].astype(BF16)
        for c0, ce in _col_chunks(F):
            g = _dot_nt(x, wd_ref[c0:c0 + ce, :]).astype(BF16)
            a_ = a_ref[:, c0:c0 + ce]
            sg = _sigmoid(a_)
            silu = a_ * sg
            da_ref[:, c0:c0 + ce] = (g * b_ref[:, c0:c0 + ce]) * (sg + silu * (1.0 - sg))
            db_ref[:, c0:c0 + ce] = g * silu

    tile = pl.BlockSpec((tm, F), lambda i: (i, 0))
    osh = jax.ShapeDtypeStruct((S, F), BF16)
    return pl.pallas_call(
        body, grid=(S // tm,),
        in_specs=[pl.BlockSpec((tm, D), lambda i: (i, 0)),
                  pl.BlockSpec((F, D), lambda i: (0, 0), pipeline_mode=_RESIDENT), tile, tile],
        out_specs=[tile, tile], out_shape=[osh, osh],
        compiler_params=_cp("parallel"), name=name)(dh, wd, a, b)


def _softmax_rows(s):
    m = jnp.max(s, axis=-1, keepdims=True)
    p = jnp.exp(s - m)
    return p / jnp.sum(p, axis=-1, keepdims=True)


def _attn_fwd(q, k, v, name):
    S, D = q.shape
    M = k.shape[0]
    hd = D // XA_HEADS
    scale = hd ** -0.5
    ts = _pick(S, TS_ATTN, 16)

    def body(q_ref, k_ref, v_ref, o_ref):
        for h in range(XA_HEADS):
            sl = slice(h * hd, (h + 1) * hd)
            p = _softmax_rows(_dot_nt(q_ref[:, sl], k_ref[:, sl]) * scale)
            o_ref[:, sl] = _dot(p.astype(BF16), v_ref[:, sl]).astype(BF16)

    tile = pl.BlockSpec((ts, D), lambda i: (i, 0))
    memspec = pl.BlockSpec((M, D), lambda i: (0, 0))
    return pl.pallas_call(
        body, grid=(S // ts,), in_specs=[tile, memspec, memspec], out_specs=tile,
        out_shape=jax.ShapeDtypeStruct((S, D), BF16), compiler_params=_cp("parallel"), name=name)(q, k, v)


def _attn_bwd(q, k, v, do, name):
    S, D = q.shape
    M = k.shape[0]
    hd = D // XA_HEADS
    scale = hd ** -0.5
    ts = _pick(S, TS_ATTN, 16)

    def body(q_ref, k_ref, v_ref, do_ref, dq_ref, dk_ref, dv_ref):
        @pl.when(pl.program_id(0) == 0)
        def _():
            dk_ref[...] = jnp.zeros_like(dk_ref)
            dv_ref[...] = jnp.zeros_like(dv_ref)

        for h in range(XA_HEADS):
            sl = slice(h * hd, (h + 1) * hd)
            qh, kh, vh, doh = q_ref[:, sl], k_ref[:, sl], v_ref[:, sl], do_ref[:, sl]
            p = _softmax_rows(_dot_nt(qh, kh) * scale)
            dp = _dot_nt(doh, vh)
            dv_ref[:, sl] += _dot_tn(p.astype(BF16), doh)
            delta = jnp.sum(dp * p, axis=-1, keepdims=True)
            ds = (p * (dp - delta) * scale).astype(BF16)
            dq_ref[:, sl] = _dot(ds, kh).astype(BF16)
            dk_ref[:, sl] += _dot_tn(ds, qh)

    tile = pl.BlockSpec((ts, D), lambda i: (i, 0))
    memspec = pl.BlockSpec((M, D), lambda i: (0, 0))
    return pl.pallas_call(
        body, grid=(S // ts,), in_specs=[tile, memspec, memspec, tile], out_specs=[tile, memspec, memspec],
        out_shape=[jax.ShapeDtypeStruct((S, D), BF16), jax.ShapeDtypeStruct((M, D), F32),
                   jax.ShapeDtypeStruct((M, D), F32)],
        compiler_params=_cp("arbitrary"), name=name)(q, k, v, do)


def _halo_specs(ts, col):
    per = ts // HALO

    def prev(i):
        return (jnp.maximum(i * per - 1, 0), col)

    def nxt(i, n_tiles):
        return (jnp.minimum((i + 1) * per, n_tiles * per - 1), col)

    return prev, nxt


def _fill_ext(ext_ref, prev_val, main_val, next_val, first, last, ts):
    ext_ref[pl.ds(0, HALO), :] = jnp.where(first, 0.0, prev_val)
    ext_ref[pl.ds(HALO, ts), :] = main_val
    ext_ref[pl.ds(HALO + ts, HALO), :] = jnp.where(last, 0.0, next_val)


SUBLANES = 8


def _fill_shifted(sh_ref, ts):
    n = ts + 2 * HALO - SUBLANES
    for s in range(1, SUBLANES):
        sh_ref[s, pl.ds(0, n), :] = sh_ref[0, pl.ds(s, n), :]


def _tap(sh_ref, r0, offset, rc):
    q, s = divmod(offset, SUBLANES)
    return sh_ref[s, pl.ds(pl.multiple_of(r0 + SUBLANES * q, SUBLANES), rc), :]


def _conv_fwd(z, wa, ba, lng, lnb, wb, bb, name, after=None):
    S = z.shape[0]
    C = z.shape[1] // 5
    KA, KB = wa.shape[0], wb.shape[0]
    pa, pb = KA // 2, KB // 2
    assert pa <= HALO and pb <= HALO
    ts = _pick(S, TS_CONV, ROW_CHUNK_FWD)
    nt = S // ts
    rc = ROW_CHUNK_FWD
    prev, nxt = _halo_specs(ts, 0)
    after_specs, after_ops = _after(after)

    def body(*refs):
        compute(*refs[:9], *refs[9 + len(after_ops):])

    def compute(z_ref, zp_ref, zn_ref, wa_ref, ba_ref, lng_ref, lnb_ref, wb_ref, bb_ref, ab_ref, ca_ref,
                ga_sh, tb_ext, win_b):
        i = pl.program_id(0)
        first, last = i == 0, i == nt - 1

        def glu(r):
            return r[:, 0:C] * _sigmoid(r[:, C:2 * C])

        def gcb(r):
            return r[:, 4 * C:5 * C] * r[:, 2 * C:3 * C]

        _fill_ext(ga_sh.at[0], glu(zp_ref), glu(z_ref), glu(zn_ref), first, last, ts)
        _fill_shifted(ga_sh, ts)
        _fill_ext(tb_ext, gcb(zp_ref), gcb(z_ref), gcb(zn_ref), first, last, ts)

        def chunk(c, carry):
            r0 = pl.multiple_of(c * rc, rc)
            win_b[...] = tb_ext[pl.ds(r0, rc + 2 * HALO), :]
            acc = jnp.zeros((rc, C), F32)
            for k in range(KA):
                acc = acc + wa_ref[k:k + 1, :] * _tap(ga_sh, r0, HALO - pa + k, rc)
            ca = acc + ba_ref[...]
            ca_ref[pl.ds(r0, rc), :] = ca
            mu = jnp.mean(ca, axis=-1, keepdims=True)
            xc = ca - mu
            var = jnp.mean(xc * xc, axis=-1, keepdims=True)
            ln = xc * lax.rsqrt(var + LN_EPS) * lng_ref[...] + lnb_ref[...]
            ab_ref[pl.ds(r0, rc), 0:C] = (ln * _sigmoid(ln)).astype(BF16)
            cb = jnp.zeros((rc, C), F32) + bb_ref[...]
            for k in range(KB):
                cb = cb + wb_ref[k:k + 1, :] * win_b[pl.ds(HALO - pb + k, rc), :]
            ab_ref[pl.ds(r0, rc), C:2 * C] = (z_ref[pl.ds(r0, rc), 3 * C:4 * C] * cb).astype(BF16)
            return carry

        lax.fori_loop(0, ts // rc, chunk, 0)

    zspec = pl.BlockSpec((ts, 5 * C), lambda i: (i, 0))
    zprev = pl.BlockSpec((HALO, 5 * C), prev)
    znext = pl.BlockSpec((HALO, 5 * C), lambda i: nxt(i, nt))

    def full(a):
        return pl.BlockSpec(a.shape, lambda i: (0, 0))

    return pl.pallas_call(
        body, grid=(nt,),
        in_specs=[zspec, zprev, znext, full(wa), full(ba), full(lng), full(lnb), full(wb), full(bb)] + after_specs,
        out_specs=[pl.BlockSpec((ts, 2 * C), lambda i: (i, 0)), pl.BlockSpec((ts, C), lambda i: (i, 0))],
        out_shape=[jax.ShapeDtypeStruct((S, 2 * C), BF16), jax.ShapeDtypeStruct((S, C), F32)],
        scratch_shapes=[pltpu.VMEM((SUBLANES, ts + 2 * HALO, C), F32), pltpu.VMEM((ts + 2 * HALO, C), F32),
                        pltpu.VMEM((rc + 2 * HALO, C), F32)],
        compiler_params=_cp("parallel"), name=name)(z, z, z, wa, ba, lng, lnb, wb, bb, *after_ops)


def _conv_dab_ln(dh, w_out, ca, lng, lnb, name, after=None):
    S, D = dh.shape
    C = ca.shape[1]
    tm = _pick(S, TS_MM // 2, 16)
    rc = MM_ROW_CHUNK if tm % MM_ROW_CHUNK == 0 else tm
    after_specs, after_ops = _after(after)

    def body(dh_ref, w_ref, ca_ref, lng_ref, lnb_ref, *rest):
        dca_ref, dbo_ref, dg_ref, db_ref, dbias_ref = rest[-5:]

        @pl.when(pl.program_id(0) == 0)
        def _():
            dg_ref[...] = jnp.zeros_like(dg_ref)
            db_ref[...] = jnp.zeros_like(db_ref)
            dbias_ref[...] = jnp.zeros_like(dbias_ref)

        for r0 in range(0, tm, rc):
            rows = pl.ds(r0, rc)
            dab = _dot_nt(dh_ref[rows, :].astype(BF16), w_ref[...])
            dbo_ref[rows, :] = dab[:, C:2 * C]
            ca_ = ca_ref[rows, :]
            mu = jnp.mean(ca_, axis=-1, keepdims=True)
            xc = ca_ - mu
            rstd = lax.rsqrt(jnp.mean(xc * xc, axis=-1, keepdims=True) + LN_EPS)
            xh = xc * rstd
            ln = xh * lng_ref[...] + lnb_ref[...]
            sg = _sigmoid(ln)
            dln = dab[:, 0:C] * (sg * (1.0 + ln * (1.0 - sg)))
            dg_ref[...] += jnp.sum(dln * xh, axis=0, keepdims=True)
            db_ref[...] += jnp.sum(dln, axis=0, keepdims=True)
            dxh = dln * lng_ref[...]
            dca = rstd * (dxh - jnp.mean(dxh, axis=-1, keepdims=True) - xh * jnp.mean(dxh * xh, axis=-1, keepdims=True))
            dca_ref[rows, :] = dca
            dbias_ref[...] += jnp.sum(dca, axis=0, keepdims=True)

    tile = pl.BlockSpec((tm, C), lambda i: (i, 0))
    vec = pl.BlockSpec((1, C), lambda i: (0, 0))
    vsh = jax.ShapeDtypeStruct((1, C), F32)
    return pl.pallas_call(
        body, grid=(S // tm,),
        in_specs=[pl.BlockSpec((tm, D), lambda i: (i, 0)),
                  pl.BlockSpec(w_out.shape, lambda i: (0, 0), pipeline_mode=_RESIDENT), tile, vec, vec] + after_specs,
        out_specs=[tile, tile, vec, vec, vec],
        out_shape=[jax.ShapeDtypeStruct((S, C), F32), jax.ShapeDtypeStruct((S, C), F32), vsh, vsh, vsh],
        compiler_params=_cp("arbitrary"), name=name)(dh, w_out, ca, lng, lnb, *after_ops)


def _conv_bwd(z, dca, db, wa, wb, bb, name):
    S = z.shape[0]
    C = z.shape[1] // 5
    KA, KB = wa.shape[0], wb.shape[0]
    pa, pb = KA // 2, KB // 2
    ts = _pick(S, TS_CONV, ROW_CHUNK)
    nt = S // ts
    rc = ROW_CHUNK
    prev0, nxt0 = _halo_specs(ts, 0)

    def body(z_ref, zp_ref, zn_ref, dca_ref, dcap_ref, dcan_ref, db_ref, dbp_ref, dbn_ref, wa_ref, wb_ref, bb_ref,
             dz_ref, dwa_ref, dwb_ref, dbb_ref,
             ga_sh, dca_sh, tb_ext, dcb_ext, win_tb, win_dcb, acc_a, acc_b, acc_bias):
        i = pl.program_id(0)
        first, last = i == 0, i == nt - 1

        @pl.when(first)
        def _():
            acc_a[...] = jnp.zeros_like(acc_a)
            acc_b[...] = jnp.zeros_like(acc_b)
            acc_bias[...] = jnp.zeros_like(acc_bias)

        def glu(r):
            return r[:, 0:C] * _sigmoid(r[:, C:2 * C])

        def gcb(r):
            return r[:, 4 * C:5 * C] * r[:, 2 * C:3 * C]

        def dcb(d, r):
            return d[...].astype(F32) * r[:, 3 * C:4 * C]

        _fill_ext(ga_sh.at[0], glu(zp_ref), glu(z_ref), glu(zn_ref), first, last, ts)
        _fill_shifted(ga_sh, ts)
        _fill_ext(dca_sh.at[0], dcap_ref[...], dca_ref[...], dcan_ref[...], first, last, ts)
        _fill_shifted(dca_sh, ts)
        _fill_ext(tb_ext, gcb(zp_ref), gcb(z_ref), gcb(zn_ref), first, last, ts)
        _fill_ext(dcb_ext, dcb(dbp_ref, zp_ref), dcb(db_ref, z_ref), dcb(dbn_ref, zn_ref), first, last, ts)

        def fold(x):
            return jnp.sum(x.reshape(rc // 8, 8, C), axis=0)

        def chunk(c, carry):
            r0 = pl.multiple_of(c * rc, rc)
            win_tb[...] = tb_ext[pl.ds(r0, rc + 2 * HALO), :]
            win_dcb[...] = dcb_ext[pl.ds(r0, rc + 2 * HALO), :]
            dca_c = _tap(dca_sh, r0, HALO, rc)
            dglu = jnp.zeros((rc, C), F32)
            for k in range(KA):
                dglu = dglu + wa_ref[k:k + 1, :] * _tap(dca_sh, r0, HALO + pa - k, rc)
                acc_a[k] += fold(dca_c * _tap(ga_sh, r0, HALO - pa + k, rc))
            val = z_ref[pl.ds(r0, rc), 0:C]
            sg = _sigmoid(z_ref[pl.ds(r0, rc), C:2 * C])
            dz_ref[pl.ds(r0, rc), 0:C] = (dglu * sg).astype(BF16)
            dz_ref[pl.ds(r0, rc), C:2 * C] = (dglu * val * sg * (1.0 - sg)).astype(BF16)
            dcb_c = win_dcb[pl.ds(HALO, rc), :]
            cb = jnp.zeros((rc, C), F32) + bb_ref[...]
            dt = jnp.zeros((rc, C), F32)
            for k in range(KB):
                tb_k = win_tb[pl.ds(HALO - pb + k, rc), :]
                cb = cb + wb_ref[k:k + 1, :] * tb_k
                dt = dt + wb_ref[k:k + 1, :] * win_dcb[pl.ds(HALO + pb - k, rc), :]
                acc_b[k] += fold(dcb_c * tb_k)
            acc_bias[...] += fold(dcb_c)
            db_c = db_ref[pl.ds(r0, rc), :].astype(F32)
            dz_ref[pl.ds(r0, rc), 2 * C:3 * C] = (dt * z_ref[pl.ds(r0, rc), 4 * C:5 * C]).astype(BF16)
            dz_ref[pl.ds(r0, rc), 3 * C:4 * C] = (db_c * cb).astype(BF16)
            dz_ref[pl.ds(r0, rc), 4 * C:5 * C] = (dt * z_ref[pl.ds(r0, rc), 2 * C:3 * C]).astype(BF16)
            return carry

        lax.fori_loop(0, ts // rc, chunk, 0)

        @pl.when(last)
        def _():
            dwa_ref[...] = jnp.sum(acc_a[...], axis=1)
            dwb_ref[...] = jnp.sum(acc_b[...], axis=1)
            dbb_ref[...] = jnp.sum(acc_bias[...], axis=0, keepdims=True)

    zspec = pl.BlockSpec((ts, 5 * C), lambda i: (i, 0))
    zprev = pl.BlockSpec((HALO, 5 * C), prev0)
    znext = pl.BlockSpec((HALO, 5 * C), lambda i: nxt0(i, nt))
    dspec = pl.BlockSpec((ts, C), lambda i: (i, 0))
    dprev = pl.BlockSpec((HALO, C), prev0)
    dnext = pl.BlockSpec((HALO, C), lambda i: nxt0(i, nt))

    def full(shape):
        return pl.BlockSpec(shape, lambda i: (0,) * len(shape))

    ext = pltpu.VMEM((ts + 2 * HALO, C), F32)
    shifted = pltpu.VMEM((SUBLANES, ts + 2 * HALO, C), F32)
    win = pltpu.VMEM((rc + 2 * HALO, C), F32)
    return pl.pallas_call(
        body, grid=(nt,),
        in_specs=[zspec, zprev, znext, dspec, dprev, dnext, dspec, dprev, dnext,
                  full(wa.shape), full(wb.shape), full(bb.shape)],
        out_specs=[pl.BlockSpec((ts, 5 * C), lambda i: (i, 0)), full((KA, C)), full((KB, C)), full((1, C))],
        out_shape=[jax.ShapeDtypeStruct((S, 5 * C), BF16), jax.ShapeDtypeStruct((KA, C), F32),
                   jax.ShapeDtypeStruct((KB, C), F32), jax.ShapeDtypeStruct((1, C), F32)],
        scratch_shapes=[shifted, shifted, ext, ext, win, win,
                        pltpu.VMEM((KA, 8, C), F32), pltpu.VMEM((KB, 8, C), F32), pltpu.VMEM((8, C), F32)],
        compiler_params=_cp("arbitrary"), name=name)(z, z, z, dca, dca, dca, db, db, db, wa, wb, bb)


_GELU_C = 0.7978845608028654
_GELU_A = 0.044715


def _gelu(x):
    return 0.5 * x * (1.0 + jnp.tanh(_GELU_C * (x + _GELU_A * (x * x * x))))


def _gelu_and_grad(x):
    t = jnp.tanh(_GELU_C * (x + _GELU_A * (x * x * x)))
    hx = 0.5 * x
    return hx * (1.0 + t), 0.5 * (1.0 + t) + hx * (1.0 - t * t) * (_GELU_C * (1.0 + 3.0 * _GELU_A * x * x))


def _sgu_fwd(zp, lng, lnb, ws, bsb, name):
    S = zp.shape[0]
    D = zp.shape[1] // 2
    G = ws.shape[0]
    gd = D // G
    ts = _pick(S, TS_SGU, CHUNK)
    ncs = ts // CHUNK

    def body(zp_ref, lng_ref, lnb_ref, ws_ref, bsb_ref, y_ref, vb_ref):
        v = _gelu(zp_ref[:, D:2 * D])
        mu = jnp.mean(v, axis=-1, keepdims=True)
        xc = v - mu
        rstd = lax.rsqrt(jnp.mean(xc * xc, axis=-1, keepdims=True) + LN_EPS)
        vb_ref[...] = (xc * rstd * lng_ref[...] + lnb_ref[...]).astype(BF16)
        for c in range(ncs):
            rows = slice(c * CHUNK, (c + 1) * CHUNK)
            for g in range(G):
                cols = slice(g * gd, (g + 1) * gd)
                sv = _dot(ws_ref[g], vb_ref[rows, cols]) + bsb_ref[:, cols]
                y_ref[rows, cols] = (_gelu(zp_ref[rows, cols]) * sv).astype(BF16)

    def full(a):
        return pl.BlockSpec(a.shape, lambda i: (0,) * a.ndim)

    return pl.pallas_call(
        body, grid=(S // ts,),
        in_specs=[pl.BlockSpec((ts, 2 * D), lambda i: (i, 0)), full(lng), full(lnb), full(ws), full(bsb)],
        out_specs=pl.BlockSpec((ts, D), lambda i: (i, 0)), out_shape=jax.ShapeDtypeStruct((S, D), BF16),
        scratch_shapes=[pltpu.VMEM((ts, D), BF16)],
        compiler_params=_cp("parallel"), name=name)(zp, lng, lnb, ws, bsb)


def _sgu_bwd(dh, w_out, zp, lng, lnb, ws, wst, bsb, name):
    S = zp.shape[0]
    D = zp.shape[1] // 2
    G = ws.shape[0]
    gd = D // G
    ts = _pick(S, TS_SGU, CHUNK)
    ncs = ts // CHUNK

    def body(dh_ref, wo_ref, zp_ref, lng_ref, lnb_ref, ws_ref, wst_ref, bsb_ref,
             dzp_ref, dws_ref, dbs_ref, dg_ref, db_ref, vb_ref, dvln_ref, acc_bs):
        i = pl.program_id(0)

        @pl.when(i == 0)
        def _():
            dws_ref[...] = jnp.zeros_like(dws_ref)
            acc_bs[...] = jnp.zeros_like(acc_bs)
            dg_ref[...] = jnp.zeros_like(dg_ref)
            db_ref[...] = jnp.zeros_like(db_ref)

        v, dv_dz = _gelu_and_grad(zp_ref[:, D:2 * D])
        mu = jnp.mean(v, axis=-1, keepdims=True)
        xc = v - mu
        rstd = lax.rsqrt(jnp.mean(xc * xc, axis=-1, keepdims=True) + LN_EPS)
        xh = xc * rstd
        vb_ref[...] = (xh * lng_ref[...] + lnb_ref[...]).astype(BF16)
        for c in range(ncs):
            rows = slice(c * CHUNK, (c + 1) * CHUNK)
            dy_c = _dot_nt(dh_ref[rows, :].astype(BF16), wo_ref[...])
            for g in range(G):
                cols = slice(g * gd, (g + 1) * gd)
                u, du_dz = _gelu_and_grad(zp_ref[rows, cols])
                dy_ = dy_c[:, cols]
                sv = _dot(ws_ref[g], vb_ref[rows, cols]) + bsb_ref[:, cols]
                dzp_ref[rows, cols] = (dy_ * sv * du_dz).astype(BF16)
                dsv = dy_ * u
                acc_bs[:, cols] += dsv
                dsvb = dsv.astype(BF16)
                dws_ref[g] += _dot_nt(dsvb, vb_ref[rows, cols])
                dvln_ref[rows, cols] = _dot(wst_ref[g], dsvb)
        dvln = dvln_ref[...]
        dg_ref[...] += jnp.sum(dvln * xh, axis=0, keepdims=True)
        db_ref[...] += jnp.sum(dvln, axis=0, keepdims=True)
        dxh = dvln * lng_ref[...]
        dv = rstd * (dxh - jnp.mean(dxh, axis=-1, keepdims=True) - xh * jnp.mean(dxh * xh, axis=-1, keepdims=True))
        dzp_ref[:, D:2 * D] = (dv * dv_dz).astype(BF16)

        @pl.when(i == pl.num_programs(0) - 1)
        def _():
            dbs_ref[...] = acc_bs[...]

    def full(shape):
        return pl.BlockSpec(shape, lambda i: (0,) * len(shape))

    return pl.pallas_call(
        body, grid=(S // ts,),
        in_specs=[pl.BlockSpec((ts, D), lambda i: (i, 0)), full(w_out.shape),
                  pl.BlockSpec((ts, 2 * D), lambda i: (i, 0)),
                  full(lng.shape), full(lnb.shape), full(ws.shape), full(wst.shape), full(bsb.shape)],
        out_specs=[pl.BlockSpec((ts, 2 * D), lambda i: (i, 0)), full(ws.shape), full(bsb.shape),
                   full((1, D)), full((1, D))],
        out_shape=[jax.ShapeDtypeStruct((S, 2 * D), BF16), jax.ShapeDtypeStruct(ws.shape, F32),
                   jax.ShapeDtypeStruct(bsb.shape, F32), jax.ShapeDtypeStruct((1, D), F32),
                   jax.ShapeDtypeStruct((1, D), F32)],
        scratch_shapes=[pltpu.VMEM((ts, D), BF16), pltpu.VMEM((ts, D), F32),
                        pltpu.VMEM(bsb.shape, F32)],
        compiler_params=_cp("arbitrary"), name=name)(dh, w_out, zp, lng, lnb, ws, wst, bsb)


def _group_sum(x, groups, name):
    P, D = x.shape
    gd = D // groups

    def body(x_ref, o_ref):
        for g in range(groups):
            o_ref[:, g:g + 1] = jnp.sum(x_ref[:, g * gd:(g + 1) * gd], axis=1, keepdims=True)

    return pl.pallas_call(body, out_shape=jax.ShapeDtypeStruct((P, groups), F32), name=name)(x)


def _adamw_small(ws, gs, ms, vs, name):
    n = len(ws)
    shapes = [w.shape for w in ws]
    flat = [(w.size // w.shape[-1], w.shape[-1]) for w in ws]

    def body(*refs):
        for k in range(n):
            w_ref, g_ref, m_ref, v_ref = (refs[j * n + k] for j in range(4))
            d_ref, nm_ref, nv_ref = (refs[(4 + j) * n + k] for j in range(3))
            d_ref[...], nm_ref[...], nv_ref[...] = _adamw_math(w_ref[...], g_ref[...], m_ref[...], v_ref[...])

    outs = pl.pallas_call(
        body, out_shape=[jax.ShapeDtypeStruct(f, F32) for f in flat] * 3, name=name)(
            *(a.reshape(f) for group in (ws, gs, ms, vs) for a, f in zip(group, flat)))
    return [tuple(outs[j * n + k].reshape(shapes[k]) for j in range(3)) for k in range(n)]


_HBM = pl.BlockSpec(memory_space=pltpu.HBM)


def _remote(src, dst, send_sem, recv_sem, to):
    return pltpu.make_async_remote_copy(src_ref=src, dst_ref=dst, send_sem=send_sem, recv_sem=recv_sem,
                                        device_id=to, device_id_type=MESH)


def _all_gather(block, name):
    R, C = block.shape

    def body(x_ref, out_ref, send_sems, recv_sems, local_sem):
        x, y, c = lax.axis_index("x"), lax.axis_index("y"), lax.axis_index("c")
        me, sibling = (x, y, c), (x, y, 1 - c)
        chips = [(1 - x, y), (x, 1 - y), (1 - x, 1 - y)]

        def slot(px, py, pc):
            return out_ref.at[4 * px + 2 * py + pc]

        def copy(k, blk, to, src=None):
            return _remote(slot(*blk) if src is None else src, slot(*blk), send_sems.at[k], recv_sems.at[k], to)

        mine = pltpu.make_async_copy(x_ref, slot(*me), local_sem)
        mine.start()
        first = [copy(0, me, sibling, src=x_ref)]
        first += [copy(1 + j, me, (*chip, c), src=x_ref) for j, chip in enumerate(chips)]
        for cp in first:
            cp.start()
        passed = [copy(4 + j, (*chip, c), sibling) for j, chip in enumerate(chips)]
        for j, chip in enumerate(chips):
            copy(1 + j, (*chip, c), me).wait_recv()
            passed[j].start()
        copy(0, sibling, me).wait_recv()
        for j, chip in enumerate(chips):
            copy(4 + j, (*chip, 1 - c), me).wait_recv()
        for cp in first + passed:
            cp.wait_send()
        mine.wait()

    return pl.pallas_call(
        body, out_shape=jax.ShapeDtypeStruct((NDEV, R, C), block.dtype), in_specs=[_HBM], out_specs=_HBM,
        scratch_shapes=[pltpu.SemaphoreType.DMA((7,)), pltpu.SemaphoreType.DMA((7,)), pltpu.SemaphoreType.DMA],
        name=name)(block)


def _all_gather_weights(pack, rows, name, after=None):
    C = pack.shape[1]
    nw = len(rows)
    starts = [sum(rows[:w]) for w in range(nw)]
    after_specs, after_ops = _after(after)

    def body(pack_ref, *rest):
        rest = rest[len(after_ops):]
        outs = rest[:nw]
        send_sems, recv_sems, local_sem = rest[nw:]
        x, y, c = lax.axis_index("x"), lax.axis_index("y"), lax.axis_index("c")
        me, sibling = (x, y, c), (x, y, 1 - c)
        chips = [(1 - x, y), (x, 1 - y), (1 - x, 1 - y)]

        def block(w, px, py, pc):
            return outs[w].at[pl.ds((4 * px + 2 * py + pc) * rows[w], rows[w])]

        def mine(w):
            return pack_ref.at[pl.ds(starts[w], rows[w])]

        def all_of(k):
            return _remote(pack_ref, pack_ref, send_sems.at[k], recv_sems.at[k], me)

        for w in range(nw):
            pltpu.make_async_copy(mine(w), block(w, *me), local_sem).start()
        for k, to in enumerate([sibling] + [(*chip, c) for chip in chips]):
            for w in range(nw):
                _remote(mine(w), block(w, *me), send_sems.at[k], recv_sems.at[k], to).start()
        for j, chip in enumerate(chips):
            all_of(1 + j).wait_recv()
            for w in range(nw):
                _remote(block(w, *chip, c), block(w, *chip, c), send_sems.at[4 + j], recv_sems.at[4 + j], sibling).start()
        all_of(0).wait_recv()
        for j in range(3):
            all_of(4 + j).wait_recv()
        for k in range(7):
            all_of(k).wait_send()
        pltpu.make_async_copy(pack_ref, pack_ref, local_sem).wait()

    return pl.pallas_call(
        body, out_shape=[jax.ShapeDtypeStruct((NDEV * r, C), pack.dtype) for r in rows],
        in_specs=[_HBM] + after_specs, out_specs=[_HBM] * nw,
        scratch_shapes=[pltpu.SemaphoreType.DMA((7,)), pltpu.SemaphoreType.DMA((7,)), pltpu.SemaphoreType.DMA],
        name=name)(pack, *after_ops)


_SEM = pl.BlockSpec(memory_space=pltpu.SEMAPHORE)
_DATAFLOW = pltpu.SideEffectType.DATAFLOW_SIDE_EFFECTING


def _split_start(srcs, lands, plan, n, after, name):
    nbuf = len(srcs) + len(lands)
    after_specs, after_ops = _after(after)

    def body(*refs):
        src_refs, land_refs = refs[:len(srcs)], refs[len(srcs):nbuf]
        send_sems, recv_sems = refs[nbuf + len(after_ops)], refs[nbuf + len(after_ops) + 1]
        for k, (src, dst, to) in enumerate(plan(src_refs, land_refs)):
            _remote(src, dst, send_sems.at[k], recv_sems.at[k], to).start()
        refs[-1][...] = jnp.zeros_like(refs[-1])

    bufs = [pltpu.with_memory_space_constraint(a, pltpu.HBM) for a in list(srcs) + list(lands)]
    outs = pl.pallas_call(
        body, name=name,
        out_shape=(pltpu.SemaphoreType.DMA((n,)), pltpu.SemaphoreType.DMA((n,)),
                   *[pltpu.HBM(a.shape, a.dtype) for a in bufs], jax.ShapeDtypeStruct((8, 128), F32)),
        in_specs=[_HBM] * nbuf + after_specs,
        out_specs=(_SEM, _SEM, *[_HBM] * nbuf, pl.BlockSpec(memory_space=pltpu.VMEM)),
        input_output_aliases={i: 2 + i for i in range(nbuf)},
        compiler_params=pltpu.CompilerParams(has_side_effects=_DATAFLOW))(*bufs, *after_ops)
    return outs[0], outs[1], list(outs[2:2 + len(srcs)]), list(outs[2 + len(srcs):2 + nbuf]), outs[-1]


def _split_wait(send_sems, recv_sems, srcs, lands, plan, after, name):
    nbuf = len(srcs) + len(lands)
    after_specs, after_ops = _after(after)

    def body(*refs):
        src_refs, land_refs = refs[:len(srcs)], refs[len(srcs):nbuf]
        send_sems_ref, recv_sems_ref = refs[nbuf], refs[nbuf + 1]
        for k, (src, dst, to) in enumerate(plan(src_refs, land_refs)):
            copy = _remote(src, dst, send_sems_ref.at[k], recv_sems_ref.at[k], to)
            copy.wait_send()
            copy.wait_recv()

    outs = pl.pallas_call(
        body, name=name, out_shape=tuple(pltpu.HBM(a.shape, a.dtype) for a in list(srcs) + list(lands)),
        in_specs=[_HBM] * nbuf + [_SEM, _SEM] + after_specs, out_specs=tuple([_HBM] * nbuf),
        input_output_aliases={i: i for i in range(nbuf)},
        compiler_params=pltpu.CompilerParams(has_side_effects=_DATAFLOW))(*srcs, *lands, send_sems, recv_sems, *after_ops)
    return list(outs[:len(srcs)]), list(outs[len(srcs):])


def _peers(x, y, c):
    return [(mask, (1 - x if mask & 4 else x, 1 - y if mask & 2 else y, 1 - c if mask & 1 else c))
            for mask in range(1, NDEV)]


def _gather_plan(rows):
    starts = [sum(rows[:w]) for w in range(len(rows))]

    def plan(src_refs, land_refs):
        x, y, c = lax.axis_index("x"), lax.axis_index("y"), lax.axis_index("c")
        copies = []
        for w, r in enumerate(rows):
            mine = src_refs[0].at[pl.ds(starts[w], r)]
            dst = land_refs[w].at[pl.ds((4 * x + 2 * y + c) * r, r)]
            copies += [(mine, dst, peer) for _, peer in _peers(x, y, c)]
        return copies

    return plan, (NDEV - 1) * len(rows)


def _place_own(shards, fulls, dev_idx, name):
    nw = len(shards)

    def body(i_ref, *refs):
        for w in range(nw):
            refs[2 * nw + w][...] = refs[w][...]

    grid_spec = pltpu.PrefetchScalarGridSpec(
        num_scalar_prefetch=1, grid=(1,),
        in_specs=[pl.BlockSpec(s.shape, lambda t, i_ref: (0, 0)) for s in shards] + [_ANY] * nw,
        out_specs=[pl.BlockSpec(s.shape, lambda t, i_ref: (i_ref[0], 0)) for s in shards])
    outs = pl.pallas_call(
        body, grid_spec=grid_spec, out_shape=[jax.ShapeDtypeStruct(f.shape, f.dtype) for f in fulls],
        input_output_aliases={1 + nw + w: w for w in range(nw)}, name=name)(dev_idx, *shards, *fulls)
    return list(outs)


def _scatter_plan(rows):
    def plan(src_refs, land_refs):
        x, y, c = lax.axis_index("x"), lax.axis_index("y"), lax.axis_index("c")
        copies = []
        for w, r in enumerate(rows):
            for mask, (px, py, pc) in _peers(x, y, c):
                src = src_refs[w].at[pl.ds((4 * px + 2 * py + pc) * r, r)]
                copies.append((src, land_refs[w].at[mask - 1], (px, py, pc)))
        return copies

    return plan, (NDEV - 1) * len(rows)


def _adamw_math(w, g, m, v):
    nm = ADAM_B1 * m + (1.0 - ADAM_B1) * g
    nv = ADAM_B2 * v + (1.0 - ADAM_B2) * (g * g)
    bc1 = 1.0 - ADAM_B1 ** ADAM_STEP
    bc2 = 1.0 - ADAM_B2 ** ADAM_STEP
    return -ADAM_LR * ((nm / bc1) / (jnp.sqrt(nv / bc2) + ADAM_EPS) + ADAM_WD * w), nm, nv


def _finish_weight(gs, gots, dev_idx, w, m, v, name, after=None):
    L = len(gs)
    n1, r, C = gots[0].shape
    block = (None,) + w.shape[1:]

    after_specs, after_ops = _after(after)

    def body(i_ref, *refs):
        ins, (w_ref, m_ref, v_ref), (g_out, d_out, m_out, v_out) = refs[:2 * L], refs[2 * L:2 * L + 3], refs[-4:]
        for layer in range(L):
            @pl.when(pl.program_id(0) == layer)
            def _():
                g_ref, got_ref = ins[2 * layer], ins[2 * layer + 1]
                acc = g_ref[...].astype(F32)
                for k in range(n1):
                    acc = acc + got_ref[k].astype(F32)
                g_out[...] = acc
                d_out[...], m_out[...], v_out[...] = _adamw_math(w_ref[...], acc, m_ref[...], v_ref[...])

    in_specs, ins = [], []
    for g, got in zip(gs, gots):
        in_specs += [pl.BlockSpec((r, C), lambda t, i_ref: (i_ref[0], 0), pipeline_mode=_RESIDENT),
                     pl.BlockSpec((n1, r, C), lambda t, i_ref: (0, 0, 0), pipeline_mode=_RESIDENT)]
        ins += [g, got]
    per_layer = pl.BlockSpec(block, lambda t, i_ref: (t, 0, 0))
    grid_spec = pltpu.PrefetchScalarGridSpec(
        num_scalar_prefetch=1, grid=(L,), in_specs=in_specs + [per_layer] * 3 + after_specs,
        out_specs=[per_layer] * 4)
    return pl.pallas_call(
        body, grid_spec=grid_spec, out_shape=[jax.ShapeDtypeStruct(w.shape, F32)] * 4,
        compiler_params=_cp("arbitrary"), name=name)(dev_idx, *ins, w, m, v, *after_ops)


def _sum_slots(a, name):
    n, R, C = a.shape

    def body(a_ref, o_ref):
        acc = a_ref[0]
        for k in range(1, n):
            acc = acc + a_ref[k]
        o_ref[...] = acc

    return pl.pallas_call(body, out_shape=jax.ShapeDtypeStruct((R, C), F32), name=name)(a)


def _shard_axis(name):
    return {"ev_w_in": 2, "ev_a_conv_w": 2, "ev_b_conv_w": 2, "ev_w_out": 1, "od_w_in": 2, "od_c_ln_g": 1,
            "od_c_ln_b": 1, "od_w_out": 1, "xa_w_q": 1, "xa_w_k": 1, "xa_w_v": 1, "xa_w_o": 1,
            "ffn_w_gate": 2, "ffn_w_up": 2, "ffn_w_down": 1}[name]


BIG = ["ev_w_in", "ev_w_out", "od_w_in", "od_w_out", "xa_w_q", "xa_w_k", "xa_w_v", "xa_w_o",
       "ffn_w_gate", "ffn_w_up", "ffn_w_down"]
SMALL_SHARDED = ["ev_a_conv_w", "ev_b_conv_w", "od_c_ln_g", "od_c_ln_b"]
REPLICATED = ["g_mix", "g_xattn", "g_mem", "g_ffn", "g_final", "ev_a_conv_b", "ev_a_ln_g", "ev_a_ln_b",
              "ev_b_conv_b", "od_w_s", "od_b_s"]
WEIGHTS = ["g_mix", "g_xattn", "g_mem", "g_ffn", "g_final", "ev_w_in", "ev_a_conv_w", "ev_a_conv_b", "ev_a_ln_g",
           "ev_a_ln_b", "ev_b_conv_w", "ev_b_conv_b", "ev_w_out", "od_w_in", "od_c_ln_g", "od_c_ln_b", "od_w_s",
           "od_b_s", "od_w_out", "xa_w_q", "xa_w_k", "xa_w_v", "xa_w_o", "ffn_w_gate", "ffn_w_up", "ffn_w_down"]


def _full_from_blocks(blocks, axis):
    shard = blocks.shape[1:]
    full = jnp.moveaxis(blocks, 0, axis)
    return full.reshape(shard[:axis] + (NDEV * shard[axis],) + shard[axis + 1:])


def _blocks_from_full(full, axis):
    shp = full.shape
    split = full.reshape(shp[:axis] + (NDEV, shp[axis] // NDEV) + shp[axis + 1:])
    return jnp.moveaxis(split, axis, 0)


def _pad_rows(flat, width, row_align):
    per = width * row_align
    n = -(-flat.shape[0] // per) * per
    return jnp.pad(flat, (0, n - flat.shape[0])).reshape(n // width, width)


def _row(v):
    return v.reshape(1, -1)


def _mem_kv(mem, g_m, wk, wv, name):
    M, D = mem.shape

    def body(mem_ref, g_ref, wk_ref, wv_ref, n_ref, k_ref, v_ref):
        x = mem_ref[...]
        r = lax.rsqrt(jnp.mean(x * x, axis=-1, keepdims=True) + RMS_EPS)
        n = ((x * r) * g_ref[...]).astype(BF16)
        n_ref[...] = n
        k_ref[...] = _dot(n, wk_ref[...]).astype(BF16)
        v_ref[...] = _dot(n, wv_ref[...]).astype(BF16)

    sh = jax.ShapeDtypeStruct((M, D), BF16)
    return pl.pallas_call(body, out_shape=[sh, sh, sh], name=name)(mem, g_m, wk, wv)


def _mem_kv_bwd(dk, dv, mem, mem_n, g_m, wk, wv, name):
    M, D = mem.shape

    def body(dk_ref, dv_ref, mem_ref, n_ref, g_ref, wk_ref, wv_ref, dwk_ref, dwv_ref, dg_ref):
        dk_, dv_ = dk_ref[...].astype(BF16), dv_ref[...].astype(BF16)
        n = n_ref[...]
        dwk_ref[...] = _dot_tn(n, dk_).astype(BF16)
        dwv_ref[...] = _dot_tn(n, dv_).astype(BF16)
        dn = _dot_nt(dk_, wk_ref[...]) + _dot_nt(dv_, wv_ref[...])
        x = mem_ref[...]
        r = lax.rsqrt(jnp.mean(x * x, axis=-1, keepdims=True) + RMS_EPS)
        dg_ref[...] = jnp.sum(dn * (x * r), axis=0, keepdims=True)

    wsh = jax.ShapeDtypeStruct((D, D), BF16)
    return pl.pallas_call(body, out_shape=[wsh, wsh, jax.ShapeDtypeStruct((1, D), F32)], name=name)(
        dk, dv, mem, mem_n, g_m, wk, wv)


def _xattn_fwd(h, nq, mem, g_m, wq, wk, wv, wo, g_next, tag, after):
    q = _mm([(nq, wq, "nn")], f"xa_q_{tag}", out_dtype=BF16, after=after)
    mem_n, k, v = _mem_kv(mem, _row(g_m), wk, wv, f"xa_mem_{tag}")
    o = _attn_fwd(q, k, v, f"xa_attn_{tag}")
    h_new, n_next = _mm([(o, wo, "nn")], f"xa_o_{tag}", res=h, rms_g=_row(g_next))
    return h_new, n_next, (h, nq, mem_n, q, k, v, o)


def _xattn_bwd(dh_new, saved, mem, g_x, g_m, wq, wk, wv, wo, tag, push):
    h, nq, mem_n, q, k, v, o = saved
    do = _mm([(dh_new, wo, "nt")], f"xa_do_{tag}", out_dtype=BF16)
    d_wo = _mm_tn(o, dh_new, f"xa_dwo_{tag}")
    dq, dk, dv = _attn_bwd(q, k, v, do, f"xa_attn_bwd_{tag}")
    d_wq = _mm_tn(nq, dq, f"xa_dwq_{tag}")
    d_wk, d_wv, d_gm = _mem_kv_bwd(dk, dv, mem, mem_n, _row(g_m), wk, wv, f"xa_mem_bwd_{tag}")
    token = push([d_wq, d_wk, d_wv, d_wo])
    dh, d_gx = _mm([(dq, wq, "nt")], f"xa_dnq_{tag}", rms_bwd=(h, _row(g_x), dh_new), tm=1024, after=token)
    return dh, dict(g_xattn=d_gx, g_mem=d_gm)


def _ffn_fwd(h, n, wgt, wut, wd, g_next, tag, after, loss=None):
    a, b, hid = _ffn_up(n, wgt, wut, f"ffn_up_{tag}", after=after)
    saved = (h, n, a, b, hid)
    if loss is not None:
        return _mm([(hid, wd, "nn")], f"ffn_down_{tag}", res=h, loss=loss, tm=1024), saved
    h_new, n_next = _mm([(hid, wd, "nn")], f"ffn_down_{tag}", res=h, rms_g=_row(g_next), tm=1024)
    return h_new, n_next, saved


def _ffn_bwd(dh_new, saved, g_f, wgt, wut, wd, tag, push):
    h, n, a, b, hid = saved
    da, db = _ffn_dhid(dh_new, wd, a, b, f"ffn_dhid_{tag}")
    d_wd = _mm_tn(hid, dh_new, f"ffn_dwd_{tag}", ts=1024, tn=1024)
    d_wgt = _mm_tn(da, n, f"ffn_dwg_{tag}", ts=1024, tn=1024)
    d_wut = _mm_tn(db, n, f"ffn_dwu_{tag}", ts=1024, tn=1024)
    token = push([d_wgt, d_wut, d_wd])
    dh, d_gf = _mm([(da, wgt, "nn"), (db, wut, "nn")], f"ffn_dn_{tag}", rms_bwd=(h, _row(g_f), dh_new), tm=512,
                   after=token)
    return dh, dict(g_ffn=d_gf)


_XA = ["xa_w_q", "xa_w_k", "xa_w_v", "xa_w_o"]
_FFN = ["ffn_w_gate", "ffn_w_up", "ffn_w_down"]
GATHERS = {
    "ev_in": [("ev_w_in", 0)],
    "xa0": [("ev_w_out", 0)] + [(n, 0) for n in _XA],
    "ffn0": [(n, 0) for n in _FFN],
    "od": [("od_w_in", 0), ("od_w_out", 0)],
    "xa1": [(n, 1) for n in _XA],
    "ffn1": [(n, 1) for n in _FFN],
}
SCATTERS = {
    "ffn1": [(n, 1) for n in _FFN],
    "xa1": [(n, 1) for n in _XA],
    "od": [("od_w_in", 0), ("od_w_out", 0)],
    "ffn0": [(n, 0) for n in _FFN],
    "xa0": [(n, 0) for n in _XA],
    "ev_out": [("ev_w_out", 0)],
    "ev_in": [("ev_w_in", 0)],
}


def _local_step(x, mem, loss_target, W, comm):
    grads = {}

    h0 = x
    (ev_w_in_t,), token = comm.weights("ev_in", None)
    n0 = _rms_fwd(h0, _row(W["g_mix"][0]), "ev_rms", after=token)
    z = _mm([(n0, ev_w_in_t, "nt")], "ev_in", tn=1280)
    token = comm.prefetch(["ffn0"], z)
    ab, ca = _conv_fwd(z, W["ev_a_conv_w"][0], W["ev_a_conv_b"], W["ev_a_ln_g"], W["ev_a_ln_b"],
                       W["ev_b_conv_w"][0], W["ev_b_conv_b"], "ev_conv", after=token)
    (ev_w_out, *xa_w0), _ = comm.weights("xa0", ab)
    h1, nq0 = _mm([(ab, ev_w_out, "nn")], "ev_out", res=h0, rms_g=_row(W["g_xattn"][0]))
    token = comm.prefetch(["od", "xa1"], nq0)
    h2, nf0, xa0 = _xattn_fwd(h1, nq0, mem, W["g_mem"][0], *xa_w0, W["g_ffn"][0], "l0", token)
    ffn_w0, _ = comm.weights("ffn0", nf0)
    token = comm.prefetch(["ffn1"], nf0)
    h3, n3, ff0 = _ffn_fwd(h2, nf0, *ffn_w0, W["g_mix"][1], "l0", token)

    (od_w_in_t, od_w_out), _ = comm.weights("od", n3)
    zp = _mm([(n3, od_w_in_t, "nt")], "od_in", tn=1024)
    D = x.shape[1]
    ws = W["od_w_s"][0].astype(BF16)
    wst = jnp.swapaxes(ws, 1, 2)
    bsb = jnp.repeat(jnp.transpose(W["od_b_s"][0]), D // C_GROUPS, axis=1)
    y_sgu = _sgu_fwd(zp, W["od_c_ln_g"], W["od_c_ln_b"], ws, bsb, "od_sgu")
    h4, nq1 = _mm([(y_sgu, od_w_out, "nn")], "od_out", res=h3, rms_g=_row(W["g_xattn"][1]))
    xa_w1, _ = comm.weights("xa1", nq1)
    h5, nf1, xa1 = _xattn_fwd(h4, nq1, mem, W["g_mem"][1], *xa_w1, W["g_ffn"][1], "l1", None)
    ffn_w1, _ = comm.weights("ffn1", nf1)
    (loss_row, dh6, d_gfinal), ff1 = _ffn_fwd(h5, nf1, *ffn_w1, None, "l1", None,
                                              loss=(_row(W["g_final"]), loss_target))
    grads["g_final"] = d_gfinal.reshape(-1)


    dh5, g_ff1 = _ffn_bwd(dh6, ff1, W["g_ffn"][1], *ffn_w1, "l1", lambda dws: comm.grads("ffn1", dws))
    dh4, g_xa1 = _xattn_bwd(dh5, xa1, mem, W["g_xattn"][1], W["g_mem"][1], *xa_w1, "l1",
                            lambda dws: comm.grads("xa1", dws))
    d_od_out = _mm_tn(y_sgu, dh4, "od_dwout", tn=1024)
    dzp, d_ws, d_bsb, d_clng, d_clnb = _sgu_bwd(dh4, od_w_out, zp, W["od_c_ln_g"], W["od_c_ln_b"], ws, wst, bsb,
                                                "od_sgu_bwd")
    grads["od_w_s"] = d_ws[None]
    grads["od_b_s"] = jnp.transpose(_group_sum(d_bsb, C_GROUPS, "od_dbs"))[None]
    grads["od_c_ln_g"], grads["od_c_ln_b"] = d_clng, d_clnb
    token = comm.grads("od", [_mm_tn(dzp, n3, "od_dwin", ts=1024, tn=1024), d_od_out])
    dh3, d_gmix1 = _mm([(dzp, od_w_in_t, "nn")], "od_dn", rms_bwd=(h3, _row(W["g_mix"][1]), dh4), tm=1024, after=token)

    dh2, g_ff0 = _ffn_bwd(dh3, ff0, W["g_ffn"][0], *ffn_w0, "l0", lambda dws: comm.grads("ffn0", dws))
    dh1, g_xa0 = _xattn_bwd(dh2, xa0, mem, W["g_xattn"][0], W["g_mem"][0], *xa_w0, "l0",
                            lambda dws: comm.grads("xa0", dws))
    token = comm.grads("ev_out", [_mm_tn(ab, dh1, "ev_dwout", tn=1024)])
    dca, db, d_lng, d_lnb, d_ba = _conv_dab_ln(dh1, ev_w_out, ca, W["ev_a_ln_g"], W["ev_a_ln_b"], "ev_dab",
                                               after=token)
    dz, d_wa, d_wb, d_bb = _conv_bwd(z, dca, db, W["ev_a_conv_w"][0], W["ev_b_conv_w"][0], W["ev_b_conv_b"],
                                     "ev_conv_bwd")
    grads.update(ev_a_ln_g=d_lng, ev_a_ln_b=d_lnb, ev_a_conv_b=d_ba, ev_b_conv_b=d_bb,
                 ev_a_conv_w=d_wa[None], ev_b_conv_w=d_wb[None])
    token = comm.grads("ev_in", [_mm_tn(dz, n0, "ev_dwin", ts=1024, tn=1024)])
    grad_x, d_gmix0 = _mm([(dz, ev_w_in_t, "nn")], "ev_dn", rms_bwd=(h0, _row(W["g_mix"][0]), dh1), tm=1024, after=token)

    grads["g_mix"] = jnp.concatenate([d_gmix0, d_gmix1], axis=0)
    for key in ("g_xattn", "g_mem"):
        grads[key] = jnp.concatenate([g_xa0[key], g_xa1[key]], axis=0)
    grads["g_ffn"] = jnp.concatenate([g_ff0["g_ffn"], g_ff1["g_ffn"]], axis=0)
    return loss_row, grad_x, grads


class _Exchanges:
    def __init__(self, shards, dev_idx, after):
        self.shards, self.dev_idx = shards, dev_idx
        self.gathering, self.scattering = {}, {}
        self.first = _all_gather_weights(self._pack(GATHERS["ev_in"]), self._rows(GATHERS["ev_in"]), "ag_ev_in",
                                         after=after)
        self.first_token = self.prefetch(["xa0"], self.first[0])

    def _rows(self, entries):
        return [self.shards[e].shape[0] for e in entries]

    def _pack(self, entries):
        return jnp.concatenate([self.shards[e] for e in entries], axis=0)

    def prefetch(self, gathers, after):
        for name in gathers:
            rows = self._rows(GATHERS[name])
            pack = self._pack(GATHERS[name])
            lands = [lax.empty((NDEV * r, pack.shape[1]), pack.dtype) for r in rows]
            plan, n = _gather_plan(rows)
            send, recv, srcs, lands, after = _split_start([pack], lands, plan, n, after, f"ag_{name}_start")
            self.gathering[name] = (send, recv, srcs, lands, plan, rows)
        return after

    def weights(self, name, after):
        if name == "ev_in":
            return self.first, self.first_token
        send, recv, srcs, lands, plan, rows = self.gathering.pop(name)
        _, lands = _split_wait(send, recv, srcs, lands, plan, after, f"ag_{name}_wait")
        return _place_own([self.shards[e] for e in GATHERS[name]], lands, self.dev_idx, f"ag_{name}_own"), None

    def grads(self, name, dws):
        rows = self._rows(SCATTERS[name])
        lands = [lax.empty((NDEV - 1, r, d.shape[1]), d.dtype) for r, d in zip(rows, dws)]
        plan, n = _scatter_plan(rows)
        send, recv, srcs, lands, token = _split_start(dws, lands, plan, n, None, f"rs_{name}_start")
        self.scattering[name] = (send, recv, srcs, lands, plan)
        return token

    def received(self, after):
        out = {}
        for name, (send, recv, srcs, lands, plan) in self.scattering.items():
            srcs, lands = _split_wait(send, recv, srcs, lands, plan, after, f"rs_{name}_wait")
            for entry, g, got in zip(SCATTERS[name], srcs, lands):
                out[entry] = (g, got)
        return out


def kernel(x, mem, g_mix, g_xattn, g_mem, g_ffn, g_final, ev_w_in, ev_a_conv_w, ev_a_conv_b, ev_a_ln_g, ev_a_ln_b, ev_b_conv_w, ev_b_conv_b, ev_w_out, od_w_in, od_c_ln_g, od_c_ln_b, od_w_s, od_b_s, od_w_out, xa_w_q, xa_w_k, xa_w_v, xa_w_o, ffn_w_gate, ffn_w_up, ffn_w_down, loss_target, m_g_mix, m_g_xattn, m_g_mem, m_g_ffn, m_g_final, m_ev_w_in, m_ev_a_conv_w, m_ev_a_conv_b, m_ev_a_ln_g, m_ev_a_ln_b, m_ev_b_conv_w, m_ev_b_conv_b, m_ev_w_out, m_od_w_in, m_od_c_ln_g, m_od_c_ln_b, m_od_w_s, m_od_b_s, m_od_w_out, m_xa_w_q, m_xa_w_k, m_xa_w_v, m_xa_w_o, m_ffn_w_gate, m_ffn_w_up, m_ffn_w_down, v_g_mix, v_g_xattn, v_g_mem, v_g_ffn, v_g_final, v_ev_w_in, v_ev_a_conv_w, v_ev_a_conv_b, v_ev_a_ln_g, v_ev_a_ln_b, v_ev_b_conv_w, v_ev_b_conv_b, v_ev_w_out, v_od_w_in, v_od_c_ln_g, v_od_c_ln_b, v_od_w_s, v_od_b_s, v_od_w_out, v_xa_w_q, v_xa_w_k, v_xa_w_v, v_xa_w_o, v_ffn_w_gate, v_ffn_w_up, v_ffn_w_down):
    local = dict(g_mix=g_mix, g_xattn=g_xattn, g_mem=g_mem, g_ffn=g_ffn, g_final=g_final, ev_w_in=ev_w_in, ev_a_conv_w=ev_a_conv_w, ev_a_conv_b=ev_a_conv_b, ev_a_ln_g=ev_a_ln_g, ev_a_ln_b=ev_a_ln_b, ev_b_conv_w=ev_b_conv_w, ev_b_conv_b=ev_b_conv_b, ev_w_out=ev_w_out, od_w_in=od_w_in, od_c_ln_g=od_c_ln_g, od_c_ln_b=od_c_ln_b, od_w_s=od_w_s, od_b_s=od_b_s, od_w_out=od_w_out, xa_w_q=xa_w_q, xa_w_k=xa_w_k, xa_w_v=xa_w_v, xa_w_o=xa_w_o, ffn_w_gate=ffn_w_gate, ffn_w_up=ffn_w_up, ffn_w_down=ffn_w_down)
    mom = dict(g_mix=m_g_mix, g_xattn=m_g_xattn, g_mem=m_g_mem, g_ffn=m_g_ffn, g_final=m_g_final, ev_w_in=m_ev_w_in, ev_a_conv_w=m_ev_a_conv_w, ev_a_conv_b=m_ev_a_conv_b, ev_a_ln_g=m_ev_a_ln_g, ev_a_ln_b=m_ev_a_ln_b, ev_b_conv_w=m_ev_b_conv_w, ev_b_conv_b=m_ev_b_conv_b, ev_w_out=m_ev_w_out, od_w_in=m_od_w_in, od_c_ln_g=m_od_c_ln_g, od_c_ln_b=m_od_c_ln_b, od_w_s=m_od_w_s, od_b_s=m_od_b_s, od_w_out=m_od_w_out, xa_w_q=m_xa_w_q, xa_w_k=m_xa_w_k, xa_w_v=m_xa_w_v, xa_w_o=m_xa_w_o, ffn_w_gate=m_ffn_w_gate, ffn_w_up=m_ffn_w_up, ffn_w_down=m_ffn_w_down)
    vel = dict(g_mix=v_g_mix, g_xattn=v_g_xattn, g_mem=v_g_mem, g_ffn=v_g_ffn, g_final=v_g_final, ev_w_in=v_ev_w_in, ev_a_conv_w=v_ev_a_conv_w, ev_a_conv_b=v_ev_a_conv_b, ev_a_ln_g=v_ev_a_ln_g, ev_a_ln_b=v_ev_a_ln_b, ev_b_conv_w=v_ev_b_conv_w, ev_b_conv_b=v_ev_b_conv_b, ev_w_out=v_ev_w_out, od_w_in=v_od_w_in, od_c_ln_g=v_od_c_ln_g, od_c_ln_b=v_od_c_ln_b, od_w_s=v_od_w_s, od_b_s=v_od_b_s, od_w_out=v_od_w_out, xa_w_q=v_xa_w_q, xa_w_k=v_xa_w_k, xa_w_v=v_xa_w_v, xa_w_o=v_xa_w_o, ffn_w_gate=v_ffn_w_gate, ffn_w_up=v_ffn_w_up, ffn_w_down=v_ffn_w_down)
    D = x.shape[-1]
    dev = 4 * lax.axis_index("x") + 2 * lax.axis_index("y") + lax.axis_index("c")

    def comm_layout(n, a):
        return jnp.transpose(a) if _shard_axis(n) == 2 else a

    shards = {(n, i): comm_layout(n, local[n][i]).astype(BF16) for n in BIG for i in range(local[n].shape[0])}
    small_sizes = [local[n].size for n in SMALL_SHARDED]
    small_block = _pad_rows(jnp.concatenate([local[n].reshape(-1) for n in SMALL_SHARDED]), 128, 8)
    small_all = _all_gather(small_block, "ag_small")
    comm = _Exchanges(shards, jnp.reshape(dev, (1,)).astype(jnp.int32), small_all)
    small_all = small_all.reshape(NDEV, -1)

    W = {n: local[n] for n in REPLICATED}
    o0 = 0
    for n, sz in zip(SMALL_SHARDED, small_sizes):
        blocks = small_all[:, o0:o0 + sz].reshape((NDEV,) + local[n].shape)
        W[n] = _full_from_blocks(blocks, _shard_axis(n))
        o0 += sz

    loss_row, grad_x, grads = _local_step(x[0], mem[0], loss_target[0], W, comm)

    received = comm.received(grad_x)
    rest = REPLICATED + SMALL_SHARDED
    rest_full_shapes = [grads[n].shape for n in rest]
    g_rest = _pad_rows(jnp.concatenate([grads[n].astype(F32).reshape(-1) for n in rest]), D, 8)
    small_rows = g_rest.shape[0]
    small_plan, small_n = _gather_plan([small_rows])
    small_send, small_recv, small_srcs, small_lands, token = _split_start(
        [g_rest], [lax.empty((NDEV * small_rows, D), F32)], small_plan, small_n, received["ev_w_in", 0][1],
        "ag_small_grads_start")

    gsh, delta, new_m, new_v = {}, {}, {}, {}
    def stacked_layout(n, a):
        return jnp.swapaxes(a, 1, 2) if _shard_axis(n) == 2 else a

    for n in BIG:
        parts = [received[n, i] for i in range(local[n].shape[0])]
        outs = _finish_weight([p[0] for p in parts], [p[1] for p in parts], comm.dev_idx,
                              *(stacked_layout(n, a) for a in (local[n], mom[n], vel[n])), f"finish_{n}", after=token)
        gsh[n], delta[n], new_m[n], new_v[n] = (stacked_layout(n, o) for o in outs)

    _, small_lands = _split_wait(small_send, small_recv, small_srcs, small_lands, small_plan,
                                 [delta[n] for n in BIG], "ag_small_grads_wait")
    partials = _place_own([g_rest], small_lands, comm.dev_idx, "ag_small_grads_own")[0]
    g_rest = _sum_slots(partials.reshape(NDEV, small_rows, D), "sum_small_grads").reshape(-1)
    o0 = 0
    for n, shp in zip(rest, rest_full_shapes):
        sz = 1
        for s in shp:
            sz *= s
        full = g_rest[o0:o0 + sz].reshape(shp)
        o0 += sz
        if n in SMALL_SHARDED:
            full = lax.dynamic_index_in_dim(_blocks_from_full(full, _shard_axis(n)), dev, 0, keepdims=False)
        gsh[n] = full.reshape(local[n].shape)

    small = _adamw_small([local[n] for n in rest], [gsh[n] for n in rest], [mom[n] for n in rest],
                         [vel[n] for n in rest], "adamw_small")
    for n, (d, nm, nv) in zip(rest, small):
        delta[n], new_m[n], new_v[n] = d, nm, nv

    loss = lax.psum(loss_row[0, 0], ("x", "y", "c"))
    return (loss, grad_x[None], *[gsh[n] for n in WEIGHTS], *[delta[n] for n in WEIGHTS],
            *[new_m[n] for n in WEIGHTS], *[new_v[n] for n in WEIGHTS])
```

```python
import jax
import jax.numpy as jnp
from jax import lax
from jax.experimental import pallas as pl
from jax.experimental.pallas import tpu as pltpu

F32, BF16 = jnp.float32, jnp.bfloat16
NDEV = 8
RMS_EPS = 1e-6
LN_EPS = 1e-5
CHUNK = 128
C_GROUPS = 8
XA_HEADS = 4
ADAM_LR, ADAM_B1, ADAM_B2, ADAM_EPS, ADAM_WD, ADAM_STEP = 0.001, 0.9, 0.999, 1e-08, 0.01, 10
HALO = 16
ROW_CHUNK = 32
ROW_CHUNK_FWD = 64
V7X_VMEM_LIMIT = 56 * 1024 * 1024
MESH = pl.DeviceIdType.MESH

TS_MM = 2048
TN_MM = 1408
TS_FFN = 512
MM_ROW_CHUNK = 256
FFN_COL_CHUNK = 256
TS_CONV = 512
TS_SGU = 512
TS_ATTN = 2048


def _cp(*sem):
    return pltpu.CompilerParams(dimension_semantics=sem, vmem_limit_bytes=V7X_VMEM_LIMIT)


def _pick(n, pref, align):
    for t in range(min(n, pref), 0, -1):
        if n % t == 0 and (t % align == 0 or t == n):
            return t
    return n


def _sigmoid(x):
    return 0.5 * jnp.tanh(0.5 * x) + 0.5


def _dot(a, b):
    return jnp.dot(a, b, preferred_element_type=F32)


def _dot_nt(a, b):
    return lax.dot_general(a, b, (((1,), (1,)), ((), ())), preferred_element_type=F32)


def _dot_tn(a, b):
    return lax.dot_general(a, b, (((0,), (0,)), ((), ())), preferred_element_type=F32)


_ANY = pl.BlockSpec(memory_space=pl.ANY)
_RESIDENT = pl.Buffered(1)


def _after(after):
    if after is None:
        return [], []
    ops = list(after) if isinstance(after, (list, tuple)) else [after]
    return [_ANY] * len(ops), ops


def _rms_fwd(h, g, name, after=None):
    S, D = h.shape
    ts = _pick(S, TS_MM, 16)
    after_specs, after_ops = _after(after)

    def body(h_ref, g_ref, *rest):
        o_ref = rest[-1]
        x = h_ref[...]
        r = lax.rsqrt(jnp.mean(x * x, axis=-1, keepdims=True) + RMS_EPS)
        o_ref[...] = ((x * r) * g_ref[...]).astype(o_ref.dtype)

    return pl.pallas_call(
        body, grid=(S // ts,),
        in_specs=[pl.BlockSpec((ts, D), lambda i: (i, 0)), pl.BlockSpec((1, D), lambda i: (0, 0))] + after_specs,
        out_specs=pl.BlockSpec((ts, D), lambda i: (i, 0)),
        out_shape=jax.ShapeDtypeStruct((S, D), BF16), compiler_params=_cp("parallel"), name=name)(h, g, *after_ops)


def _mm(pairs, name, out_dtype=F32, res=None, rms_g=None, rms_bwd=None, loss=None, tm=None, tn=None, after=None):
    M = pairs[0][0].shape[0]
    N = pairs[0][1].shape[1 if pairs[0][2] == "nn" else 0]
    whole_rows = rms_g is not None or rms_bwd is not None or loss is not None
    tm = _pick(M, tm or TS_MM, 16)
    tn = N if whole_rows else _pick(N, tn or TN_MM, 128)
    npair = len(pairs)
    modes = [p[2] for p in pairs]
    after_specs, after_ops = _after(after)

    rc = MM_ROW_CHUNK if whole_rows and tm % MM_ROW_CHUNK == 0 else tm

    def body(*refs):
        rest = refs[2 * npair + len(after_ops):]
        res_ref = None
        if res is not None:
            res_ref, rest = rest[0], rest[1:]
        if rms_bwd is not None:
            dg_ref = rest[4]

            @pl.when(pl.program_id(0) == 0)
            def _():
                dg_ref[...] = jnp.zeros_like(dg_ref)

        if loss is not None:
            g_ref, t_ref, loss_ref, dh_ref, dg_ref = rest

            @pl.when(pl.program_id(0) == 0)
            def _():
                dg_ref[...] = jnp.zeros_like(dg_ref)
                loss_ref[...] = jnp.zeros_like(loss_ref)

        for r0 in range(0, tm, rc):
            rows = pl.ds(r0, rc)
            acc = None
            for p in range(npair):
                a_ = refs[2 * p][rows, :].astype(BF16)
                d = _dot(a_, refs[2 * p + 1][...]) if modes[p] == "nn" else _dot_nt(a_, refs[2 * p + 1][...])
                acc = d if acc is None else acc + d
            if res_ref is not None:
                acc = acc + res_ref[rows, :]
            if rms_bwd is not None:
                h_ref, g_ref, dres_ref, dh_ref, _ = rest
                x = h_ref[rows, :]
                r = lax.rsqrt(jnp.mean(x * x, axis=-1, keepdims=True) + RMS_EPS)
                xr = x * r
                dg_ref[...] += jnp.sum(acc * xr, axis=0, keepdims=True)
                u = acc * g_ref[...]
                dh_ref[rows, :] = r * u - xr * (r * jnp.mean(u * xr, axis=-1, keepdims=True)) + dres_ref[rows, :]
            elif loss is not None:
                r = lax.rsqrt(jnp.mean(acc * acc, axis=-1, keepdims=True) + RMS_EPS)
                xr = acc * r
                gg = g_ref[...]
                e = xr * gg - t_ref[rows, :]
                chunk_loss = jnp.sum(jnp.sum(e * e, axis=0, keepdims=True), axis=1, keepdims=True) * (0.5 / N)
                loss_ref[...] += jnp.broadcast_to(chunk_loss, loss_ref.shape)
                dy = e * (1.0 / N)
                dg_ref[...] += jnp.sum(dy * xr, axis=0, keepdims=True)
                u = dy * gg
                dh_ref[rows, :] = r * u - xr * (r * jnp.mean(u * xr, axis=-1, keepdims=True))
            elif rms_g is not None:
                g_ref, o_ref, n_ref = rest
                o_ref[rows, :] = acc
                r = lax.rsqrt(jnp.mean(acc * acc, axis=-1, keepdims=True) + RMS_EPS)
                n_ref[rows, :] = ((acc * r) * g_ref[...]).astype(BF16)
            else:
                rest[0][rows, :] = acc.astype(rest[0].dtype)

    in_specs, ins = [], []
    for a, w, mode in pairs:
        K = a.shape[1]
        in_specs.append(pl.BlockSpec((tm, K), lambda i, j: (i, 0)))
        once = _RESIDENT if tn == N else None
        in_specs.append(pl.BlockSpec((K, tn), lambda i, j: (0, j), pipeline_mode=once) if mode == "nn"
                        else pl.BlockSpec((tn, K), lambda i, j: (j, 0), pipeline_mode=once))
        ins += [a, w]
    in_specs += after_specs
    ins += after_ops
    tile = pl.BlockSpec((tm, tn), lambda i, j: (i, j))
    vec = pl.BlockSpec((1, tn), lambda i, j: (0, j))
    if res is not None:
        in_specs.append(tile)
        ins.append(res)
    sem = ("parallel", "parallel")
    if rms_bwd is not None:
        in_specs += [tile, vec, tile]
        ins += list(rms_bwd)
        out_specs = [tile, vec]
        out_shape = [jax.ShapeDtypeStruct((M, N), F32), jax.ShapeDtypeStruct((1, N), F32)]
        sem = ("arbitrary", "arbitrary")
    elif loss is not None:
        in_specs += [vec, tile]
        ins += list(loss)
        out_specs = [pl.BlockSpec((1, 128), lambda i, j: (0, 0)), tile, vec]
        out_shape = [jax.ShapeDtypeStruct((1, 128), F32), jax.ShapeDtypeStruct((M, N), F32),
                     jax.ShapeDtypeStruct((1, N), F32)]
        sem = ("arbitrary", "arbitrary")
    elif rms_g is not None:
        in_specs.append(vec)
        ins.append(rms_g)
        out_specs = [tile, tile]
        out_shape = [jax.ShapeDtypeStruct((M, N), F32), jax.ShapeDtypeStruct((M, N), BF16)]
    else:
        out_specs = tile
        out_shape = jax.ShapeDtypeStruct((M, N), out_dtype)
    return pl.pallas_call(
        body, grid=(M // tm, N // tn), in_specs=in_specs, out_specs=out_specs, out_shape=out_shape,
        compiler_params=_cp(*sem), name=name)(*ins)


def _mm_tn(a, b, name, ts=None, tn=None):
    S, K = a.shape
    N = b.shape[1]
    ts = _pick(S, ts or TS_MM, 16)
    tn = _pick(N, tn or TN_MM, 128)
    nsteps = S // ts

    def body(a_ref, b_ref, o_ref, acc_ref):
        s = pl.program_id(1)

        @pl.when(s == 0)
        def _():
            acc_ref[...] = jnp.zeros_like(acc_ref)

        acc_ref[...] += _dot_tn(a_ref[...].astype(BF16), b_ref[...].astype(BF16))

        @pl.when(s == nsteps - 1)
        def _():
            o_ref[...] = acc_ref[...].astype(o_ref.dtype)

    return pl.pallas_call(
        body, grid=(N // tn, nsteps),
        in_specs=[pl.BlockSpec((ts, K), lambda j, s: (s, 0)), pl.BlockSpec((ts, tn), lambda j, s: (s, j))],
        out_specs=pl.BlockSpec((K, tn), lambda j, s: (0, j)), out_shape=jax.ShapeDtypeStruct((K, N), BF16),
        scratch_shapes=[pltpu.VMEM((K, tn), F32)],
        compiler_params=_cp("parallel", "arbitrary"), name=name)(a, b)


def _col_chunks(n):
    return [(c0, min(FFN_COL_CHUNK, n - c0)) for c0 in range(0, n, FFN_COL_CHUNK)]


def _ffn_up(n, wgt, wut, name, after=None):
    S, D = n.shape
    F = wgt.shape[0]
    tm = _pick(S, TS_FFN, 16)
    after_specs, after_ops = _after(after)

    def body(n_ref, wg_ref, wu_ref, *rest):
        a_ref, b_ref, hid_ref = rest[-3:]
        x = n_ref[...]
        for c0, ce in _col_chunks(F):
            a = _dot_nt(x, wg_ref[c0:c0 + ce, :])
            b = _dot_nt(x, wu_ref[c0:c0 + ce, :])
            a_ref[:, c0:c0 + ce] = a.astype(BF16)
            b_ref[:, c0:c0 + ce] = b.astype(BF16)
            hid_ref[:, c0:c0 + ce] = (a * _sigmoid(a) * b).astype(BF16)

    wspec = pl.BlockSpec((F, D), lambda i: (0, 0), pipeline_mode=_RESIDENT)
    ospec = pl.BlockSpec((tm, F), lambda i: (i, 0))
    osh = jax.ShapeDtypeStruct((S, F), BF16)
    return pl.pallas_call(
        body, grid=(S // tm,),
        in_specs=[pl.BlockSpec((tm, D), lambda i: (i, 0)), wspec, wspec] + after_specs,
        out_specs=[ospec, ospec, ospec], out_shape=[osh, osh, osh],
        compiler_params=_cp("parallel"), name=name)(n, wgt, wut, *after_ops)


def _ffn_dhid(dh, wd, a, b, name):
    S, D = dh.shape
    F = wd.shape[0]
    tm = _pick(S, TS_FFN, 16)

    def body(dh_ref, wd_ref, a_ref, b_ref, da_ref, db_ref):
        x = dh_ref[...].astype(BF16)
        for c0, ce in _col_chunks(F):
            g = _dot_nt(x, wd_ref[c0:c0 + ce, :]).astype(BF16)
            a_ = a_ref[:, c0:c0 + ce]
            sg = _sigmoid(a_)
            silu = a_ * sg
            da_ref[:, c0:c0 + ce] = (g * b_ref[:, c0:c0 + ce]) * (sg + silu * (1.0 - sg))
            db_ref[:, c0:c0 + ce] = g * silu

    tile = pl.BlockSpec((tm, F), lambda i: (i, 0))
    osh = jax.ShapeDtypeStruct((S, F), BF16)
    return pl.pallas_call(
        body, grid=(S // tm,),
        in_specs=[pl.BlockSpec((tm, D), lambda i: (i, 0)),
                  pl.BlockSpec((F, D), lambda i: (0, 0), pipeline_mode=_RESIDENT), tile, tile],
        out_specs=[tile, tile], out_shape=[osh, osh],
        compiler_params=_cp("parallel"), name=name)(dh, wd, a, b)


def _softmax_rows(s):
    m = jnp.max(s, axis=-1, keepdims=True)
    p = jnp.exp(s - m)
    return p / jnp.sum(p, axis=-1, keepdims=True)


def _attn_fwd(q, k, v, name):
    S, D = q.shape
    M = k.shape[0]
    hd = D // XA_HEADS
    scale = hd ** -0.5
    ts = _pick(S, TS_ATTN, 16)

    def body(q_ref, k_ref, v_ref, o_ref):
        for h in range(XA_HEADS):
            sl = slice(h * hd, (h + 1) * hd)
            p = _softmax_rows(_dot_nt(q_ref[:, sl], k_ref[:, sl]) * scale)
            o_ref[:, sl] = _dot(p.astype(BF16), v_ref[:, sl]).astype(BF16)

    tile = pl.BlockSpec((ts, D), lambda i: (i, 0))
    memspec = pl.BlockSpec((M, D), lambda i: (0, 0))
    return pl.pallas_call(
        body, grid=(S // ts,), in_specs=[tile, memspec, memspec], out_specs=tile,
        out_shape=jax.ShapeDtypeStruct((S, D), BF16), compiler_params=_cp("parallel"), name=name)(q, k, v)


def _attn_bwd(q, k, v, do, name):
    S, D = q.shape
    M = k.shape[0]
    hd = D // XA_HEADS
    scale = hd ** -0.5
    ts = _pick(S, TS_ATTN, 16)

    def body(q_ref, k_ref, v_ref, do_ref, dq_ref, dk_ref, dv_ref):
        @pl.when(pl.program_id(0) == 0)
        def _():
            dk_ref[...] = jnp.zeros_like(dk_ref)
            dv_ref[...] = jnp.zeros_like(dv_ref)

        for h in range(XA_HEADS):
            sl = slice(h * hd, (h + 1) * hd)
            qh, kh, vh, doh = q_ref[:, sl], k_ref[:, sl], v_ref[:, sl], do_ref[:, sl]
            p = _softmax_rows(_dot_nt(qh, kh) * scale)
            dp = _dot_nt(doh, vh)
            dv_ref[:, sl] += _dot_tn(p.astype(BF16), doh)
            delta = jnp.sum(dp * p, axis=-1, keepdims=True)
            ds = (p * (dp - delta) * scale).astype(BF16)
            dq_ref[:, sl] = _dot(ds, kh).astype(BF16)
            dk_ref[:, sl] += _dot_tn(ds, qh)

    tile = pl.BlockSpec((ts, D), lambda i: (i, 0))
    memspec = pl.BlockSpec((M, D), lambda i: (0, 0))
    return pl.pallas_call(
        body, grid=(S // ts,), in_specs=[tile, memspec, memspec, tile], out_specs=[tile, memspec, memspec],
        out_shape=[jax.ShapeDtypeStruct((S, D), BF16), jax.ShapeDtypeStruct((M, D), F32),
                   jax.ShapeDtypeStruct((M, D), F32)],
        compiler_params=_cp("arbitrary"), name=name)(q, k, v, do)


def _halo_specs(ts, col):
    per = ts // HALO

    def prev(i):
        return (jnp.maximum(i * per - 1, 0), col)

    def nxt(i, n_tiles):
        return (jnp.minimum((i + 1) * per, n_tiles * per - 1), col)

    return prev, nxt


def _fill_ext(ext_ref, prev_val, main_val, next_val, first, last, ts):
    ext_ref[pl.ds(0, HALO), :] = jnp.where(first, 0.0, prev_val)
    ext_ref[pl.ds(HALO, ts), :] = main_val
    ext_ref[pl.ds(HALO + ts, HALO), :] = jnp.where(last, 0.0, next_val)


SUBLANES = 8


def _fill_shifted(sh_ref, ts):
    n = ts + 2 * HALO - SUBLANES
    for s in range(1, SUBLANES):
        sh_ref[s, pl.ds(0, n), :] = sh_ref[0, pl.ds(s, n), :]


def _tap(sh_ref, r0, offset, rc):
    q, s = divmod(offset, SUBLANES)
    return sh_ref[s, pl.ds(pl.multiple_of(r0 + SUBLANES * q, SUBLANES), rc), :]


def _conv_fwd(z, wa, ba, lng, lnb, wb, bb, name, after=None):
    S = z.shape[0]
    C = z.shape[1] // 5
    KA, KB = wa.shape[0], wb.shape[0]
    pa, pb = KA // 2, KB // 2
    assert pa <= HALO and pb <= HALO
    ts = _pick(S, TS_CONV, ROW_CHUNK_FWD)
    nt = S // ts
    rc = ROW_CHUNK_FWD
    prev, nxt = _halo_specs(ts, 0)
    after_specs, after_ops = _after(after)

    def body(*refs):
        compute(*refs[:9], *refs[9 + len(after_ops):])

    def compute(z_ref, zp_ref, zn_ref, wa_ref, ba_ref, lng_ref, lnb_ref, wb_ref, bb_ref, ab_ref, ca_ref,
                ga_sh, tb_ext, win_b):
        i = pl.program_id(0)
        first, last = i == 0, i == nt - 1

        def glu(r):
            return r[:, 0:C] * _sigmoid(r[:, C:2 * C])

        def gcb(r):
            return r[:, 4 * C:5 * C] * r[:, 2 * C:3 * C]

        _fill_ext(ga_sh.at[0], glu(zp_ref), glu(z_ref), glu(zn_ref), first, last, ts)
        _fill_shifted(ga_sh, ts)
        _fill_ext(tb_ext, gcb(zp_ref), gcb(z_ref), gcb(zn_ref), first, last, ts)

        def chunk(c, carry):
            r0 = pl.multiple_of(c * rc, rc)
            win_b[...] = tb_ext[pl.ds(r0, rc + 2 * HALO), :]
            acc = jnp.zeros((rc, C), F32)
            for k in range(KA):
                acc = acc + wa_ref[k:k + 1, :] * _tap(ga_sh, r0, HALO - pa + k, rc)
            ca = acc + ba_ref[...]
            ca_ref[pl.ds(r0, rc), :] = ca
            mu = jnp.mean(ca, axis=-1, keepdims=True)
            xc = ca - mu
            var = jnp.mean(xc * xc, axis=-1, keepdims=True)
            ln = xc * lax.rsqrt(var + LN_EPS) * lng_ref[...] + lnb_ref[...]
            ab_ref[pl.ds(r0, rc), 0:C] = (ln * _sigmoid(ln)).astype(BF16)
            cb = jnp.zeros((rc, C), F32) + bb_ref[...]
            for k in range(KB):
                cb = cb + wb_ref[k:k + 1, :] * win_b[pl.ds(HALO - pb + k, rc), :]
            ab_ref[pl.ds(r0, rc), C:2 * C] = (z_ref[pl.ds(r0, rc), 3 * C:4 * C] * cb).astype(BF16)
            return carry

        lax.fori_loop(0, ts // rc, chunk, 0)

    zspec = pl.BlockSpec((ts, 5 * C), lambda i: (i, 0))
    zprev = pl.BlockSpec((HALO, 5 * C), prev)
    znext = pl.BlockSpec((HALO, 5 * C), lambda i: nxt(i, nt))

    def full(a):
        return pl.BlockSpec(a.shape, lambda i: (0, 0))

    return pl.pallas_call(
        body, grid=(nt,),
        in_specs=[zspec, zprev, znext, full(wa), full(ba), full(lng), full(lnb), full(wb), full(bb)] + after_specs,
        out_specs=[pl.BlockSpec((ts, 2 * C), lambda i: (i, 0)), pl.BlockSpec((ts, C), lambda i: (i, 0))],
        out_shape=[jax.ShapeDtypeStruct((S, 2 * C), BF16), jax.ShapeDtypeStruct((S, C), F32)],
        scratch_shapes=[pltpu.VMEM((SUBLANES, ts + 2 * HALO, C), F32), pltpu.VMEM((ts + 2 * HALO, C), F32),
                        pltpu.VMEM((rc + 2 * HALO, C), F32)],
        compiler_params=_cp("parallel"), name=name)(z, z, z, wa, ba, lng, lnb, wb, bb, *after_ops)


def _conv_dab_ln(dh, w_out, ca, lng, lnb, name, after=None):
    S, D = dh.shape
    C = ca.shape[1]
    tm = _pick(S, TS_MM // 2, 16)
    rc = MM_ROW_CHUNK if tm % MM_ROW_CHUNK == 0 else tm
    after_specs, after_ops = _after(after)

    def body(dh_ref, w_ref, ca_ref, lng_ref, lnb_ref, *rest):
        dca_ref, dbo_ref, dg_ref, db_ref, dbias_ref = rest[-5:]

        @pl.when(pl.program_id(0) == 0)
        def _():
            dg_ref[...] = jnp.zeros_like(dg_ref)
            db_ref[...] = jnp.zeros_like(db_ref)
            dbias_ref[...] = jnp.zeros_like(dbias_ref)

        for r0 in range(0, tm, rc):
            rows = pl.ds(r0, rc)
            dab = _dot_nt(dh_ref[rows, :].astype(BF16), w_ref[...])
            dbo_ref[rows, :] = dab[:, C:2 * C]
            ca_ = ca_ref[rows, :]
            mu = jnp.mean(ca_, axis=-1, keepdims=True)
            xc = ca_ - mu
            rstd = lax.rsqrt(jnp.mean(xc * xc, axis=-1, keepdims=True) + LN_EPS)
            xh = xc * rstd
            ln = xh * lng_ref[...] + lnb_ref[...]
            sg = _sigmoid(ln)
            dln = dab[:, 0:C] * (sg * (1.0 + ln * (1.0 - sg)))
            dg_ref[...] += jnp.sum(dln * xh, axis=0, keepdims=True)
            db_ref[...] += jnp.sum(dln, axis=0, keepdims=True)
            dxh = dln * lng_ref[...]
            dca = rstd * (dxh - jnp.mean(dxh, axis=-1, keepdims=True) - xh * jnp.mean(dxh * xh, axis=-1, keepdims=True))
            dca_ref[rows, :] = dca
            dbias_ref[...] += jnp.sum(dca, axis=0, keepdims=True)

    tile = pl.BlockSpec((tm, C), lambda i: (i, 0))
    vec = pl.BlockSpec((1, C), lambda i: (0, 0))
    vsh = jax.ShapeDtypeStruct((1, C), F32)
    return pl.pallas_call(
        body, grid=(S // tm,),
        in_specs=[pl.BlockSpec((tm, D), lambda i: (i, 0)),
                  pl.BlockSpec(w_out.shape, lambda i: (0, 0), pipeline_mode=_RESIDENT), tile, vec, vec] + after_specs,
        out_specs=[tile, tile, vec, vec, vec],
        out_shape=[jax.ShapeDtypeStruct((S, C), F32), jax.ShapeDtypeStruct((S, C), F32), vsh, vsh, vsh],
        compiler_params=_cp("arbitrary"), name=name)(dh, w_out, ca, lng, lnb, *after_ops)


def _conv_bwd(z, dca, db, wa, wb, bb, name):
    S = z.shape[0]
    C = z.shape[1] // 5
    KA, KB = wa.shape[0], wb.shape[0]
    pa, pb = KA // 2, KB // 2
    ts = _pick(S, TS_CONV, ROW_CHUNK)
    nt = S // ts
    rc = ROW_CHUNK
    prev0, nxt0 = _halo_specs(ts, 0)

    def body(z_ref, zp_ref, zn_ref, dca_ref, dcap_ref, dcan_ref, db_ref, dbp_ref, dbn_ref, wa_ref, wb_ref, bb_ref,
             dz_ref, dwa_ref, dwb_ref, dbb_ref,
             ga_sh, dca_sh, tb_ext, dcb_ext, win_tb, win_dcb, acc_a, acc_b, acc_bias):
        i = pl.program_id(0)
        first, last = i == 0, i == nt - 1

        @pl.when(first)
        def _():
            acc_a[...] = jnp.zeros_like(acc_a)
            acc_b[...] = jnp.zeros_like(acc_b)
            acc_bias[...] = jnp.zeros_like(acc_bias)

        def glu(r):
            return r[:, 0:C] * _sigmoid(r[:, C:2 * C])

        def gcb(r):
            return r[:, 4 * C:5 * C] * r[:, 2 * C:3 * C]

        def dcb(d, r):
            return d[...].astype(F32) * r[:, 3 * C:4 * C]

        _fill_ext(ga_sh.at[0], glu(zp_ref), glu(z_ref), glu(zn_ref), first, last, ts)
        _fill_shifted(ga_sh, ts)
        _fill_ext(dca_sh.at[0], dcap_ref[...], dca_ref[...], dcan_ref[...], first, last, ts)
        _fill_shifted(dca_sh, ts)
        _fill_ext(tb_ext, gcb(zp_ref), gcb(z_ref), gcb(zn_ref), first, last, ts)
        _fill_ext(dcb_ext, dcb(dbp_ref, zp_ref), dcb(db_ref, z_ref), dcb(dbn_ref, zn_ref), first, last, ts)

        def fold(x):
            return jnp.sum(x.reshape(rc // 8, 8, C), axis=0)

        def chunk(c, carry):
            r0 = pl.multiple_of(c * rc, rc)
            win_tb[...] = tb_ext[pl.ds(r0, rc + 2 * HALO), :]
            win_dcb[...] = dcb_ext[pl.ds(r0, rc + 2 * HALO), :]
            dca_c = _tap(dca_sh, r0, HALO, rc)
            dglu = jnp.zeros((rc, C), F32)
            for k in range(KA):
                dglu = dglu + wa_ref[k:k + 1, :] * _tap(dca_sh, r0, HALO + pa - k, rc)
                acc_a[k] += fold(dca_c * _tap(ga_sh, r0, HALO - pa + k, rc))
            val = z_ref[pl.ds(r0, rc), 0:C]
            sg = _sigmoid(z_ref[pl.ds(r0, rc), C:2 * C])
            dz_ref[pl.ds(r0, rc), 0:C] = (dglu * sg).astype(BF16)
            dz_ref[pl.ds(r0, rc), C:2 * C] = (dglu * val * sg * (1.0 - sg)).astype(BF16)
            dcb_c = win_dcb[pl.ds(HALO, rc), :]
            cb = jnp.zeros((rc, C), F32) + bb_ref[...]
            dt = jnp.zeros((rc, C), F32)
            for k in range(KB):
                tb_k = win_tb[pl.ds(HALO - pb + k, rc), :]
                cb = cb + wb_ref[k:k + 1, :] * tb_k
                dt = dt + wb_ref[k:k + 1, :] * win_dcb[pl.ds(HALO + pb - k, rc), :]
                acc_b[k] += fold(dcb_c * tb_k)
            acc_bias[...] += fold(dcb_c)
            db_c = db_ref[pl.ds(r0, rc), :].astype(F32)
            dz_ref[pl.ds(r0, rc), 2 * C:3 * C] = (dt * z_ref[pl.ds(r0, rc), 4 * C:5 * C]).astype(BF16)
            dz_ref[pl.ds(r0, rc), 3 * C:4 * C] = (db_c * cb).astype(BF16)
            dz_ref[pl.ds(r0, rc), 4 * C:5 * C] = (dt * z_ref[pl.ds(r0, rc), 2 * C:3 * C]).astype(BF16)
            return carry

        lax.fori_loop(0, ts // rc, chunk, 0)

        @pl.when(last)
        def _():
            dwa_ref[...] = jnp.sum(acc_a[...], axis=1)
            dwb_ref[...] = jnp.sum(acc_b[...], axis=1)
            dbb_ref[...] = jnp.sum(acc_bias[...], axis=0, keepdims=True)

    zspec = pl.BlockSpec((ts, 5 * C), lambda i: (i, 0))
    zprev = pl.BlockSpec((HALO, 5 * C), prev0)
    znext = pl.BlockSpec((HALO, 5 * C), lambda i: nxt0(i, nt))
    dspec = pl.BlockSpec((ts, C), lambda i: (i, 0))
    dprev = pl.BlockSpec((HALO, C), prev0)
    dnext = pl.BlockSpec((HALO, C), lambda i: nxt0(i, nt))

    def full(shape):
        return pl.BlockSpec(shape, lambda i: (0,) * len(shape))

    ext = pltpu.VMEM((ts + 2 * HALO, C), F32)
    shifted = pltpu.VMEM((SUBLANES, ts + 2 * HALO, C), F32)
    win = pltpu.VMEM((rc + 2 * HALO, C), F32)
    return pl.pallas_call(
        body, grid=(nt,),
        in_specs=[zspec, zprev, znext, dspec, dprev, dnext, dspec, dprev, dnext,
                  full(wa.shape), full(wb.shape), full(bb.shape)],
        out_specs=[pl.BlockSpec((ts, 5 * C), lambda i: (i, 0)), full((KA, C)), full((KB, C)), full((1, C))],
        out_shape=[jax.ShapeDtypeStruct((S, 5 * C), BF16), jax.ShapeDtypeStruct((KA, C), F32),
                   jax.ShapeDtypeStruct((KB, C), F32), jax.ShapeDtypeStruct((1, C), F32)],
        scratch_shapes=[shifted, shifted, ext, ext, win, win,
                        pltpu.VMEM((KA, 8, C), F32), pltpu.VMEM((KB, 8, C), F32), pltpu.VMEM((8, C), F32)],
        compiler_params=_cp("arbitrary"), name=name)(z, z, z, dca, dca, dca, db, db, db, wa, wb, bb)


_GELU_C = 0.7978845608028654
_GELU_A = 0.044715


def _gelu(x):
    return 0.5 * x * (1.0 + jnp.tanh(_GELU_C * (x + _GELU_A * (x * x * x))))


def _gelu_and_grad(x):
    t = jnp.tanh(_GELU_C * (x + _GELU_A * (x * x * x)))
    hx = 0.5 * x
    return hx * (1.0 + t), 0.5 * (1.0 + t) + hx * (1.0 - t * t) * (_GELU_C * (1.0 + 3.0 * _GELU_A * x * x))


def _sgu_fwd(zp, lng, lnb, ws, bsb, name):
    S = zp.shape[0]
    D = zp.shape[1] // 2
    G = ws.shape[0]
    gd = D // G
    ts = _pick(S, TS_SGU, CHUNK)
    ncs = ts // CHUNK

    def body(zp_ref, lng_ref, lnb_ref, ws_ref, bsb_ref, y_ref, vb_ref):
        v = _gelu(zp_ref[:, D:2 * D])
        mu = jnp.mean(v, axis=-1, keepdims=True)
        xc = v - mu
        rstd = lax.rsqrt(jnp.mean(xc * xc, axis=-1, keepdims=True) + LN_EPS)
        vb_ref[...] = (xc * rstd * lng_ref[...] + lnb_ref[...]).astype(BF16)
        for c in range(ncs):
            rows = slice(c * CHUNK, (c + 1) * CHUNK)
            for g in range(G):
                cols = slice(g * gd, (g + 1) * gd)
                sv = _dot(ws_ref[g], vb_ref[rows, cols]) + bsb_ref[:, cols]
                y_ref[rows, cols] = (_gelu(zp_ref[rows, cols]) * sv).astype(BF16)

    def full(a):
        return pl.BlockSpec(a.shape, lambda i: (0,) * a.ndim)

    return pl.pallas_call(
        body, grid=(S // ts,),
        in_specs=[pl.BlockSpec((ts, 2 * D), lambda i: (i, 0)), full(lng), full(lnb), full(ws), full(bsb)],
        out_specs=pl.BlockSpec((ts, D), lambda i: (i, 0)), out_shape=jax.ShapeDtypeStruct((S, D), BF16),
        scratch_shapes=[pltpu.VMEM((ts, D), BF16)],
        compiler_params=_cp("parallel"), name=name)(zp, lng, lnb, ws, bsb)


def _sgu_bwd(dh, w_out, zp, lng, lnb, ws, wst, bsb, name):
    S = zp.shape[0]
    D = zp.shape[1] // 2
    G = ws.shape[0]
    gd = D // G
    ts = _pick(S, TS_SGU, CHUNK)
    ncs = ts // CHUNK

    def body(dh_ref, wo_ref, zp_ref, lng_ref, lnb_ref, ws_ref, wst_ref, bsb_ref,
             dzp_ref, dws_ref, dbs_ref, dg_ref, db_ref, vb_ref, dvln_ref, acc_bs, dy_ref):
        i = pl.program_id(0)

        @pl.when(i == 0)
        def _():
            dws_ref[...] = jnp.zeros_like(dws_ref)
            acc_bs[...] = jnp.zeros_like(acc_bs)
            dg_ref[...] = jnp.zeros_like(dg_ref)
            db_ref[...] = jnp.zeros_like(db_ref)

        dy_ref[...] = _dot_nt(dh_ref[...].astype(BF16), wo_ref[...])
        v, dv_dz = _gelu_and_grad(zp_ref[:, D:2 * D])
        mu = jnp.mean(v, axis=-1, keepdims=True)
        xc = v - mu
        rstd = lax.rsqrt(jnp.mean(xc * xc, axis=-1, keepdims=True) + LN_EPS)
        xh = xc * rstd
        vb_ref[...] = (xh * lng_ref[...] + lnb_ref[...]).astype(BF16)
        for c in range(ncs):
            rows = slice(c * CHUNK, (c + 1) * CHUNK)
            for g in range(G):
                cols = slice(g * gd, (g + 1) * gd)
                u, du_dz = _gelu_and_grad(zp_ref[rows, cols])
                dy_ = dy_ref[rows, cols]
                sv = _dot(ws_ref[g], vb_ref[rows, cols]) + bsb_ref[:, cols]
                dzp_ref[rows, cols] = (dy_ * sv * du_dz).astype(BF16)
                dsv = dy_ * u
                acc_bs[:, cols] += dsv
                dsvb = dsv.astype(BF16)
                dws_ref[g] += _dot_nt(dsvb, vb_ref[rows, cols])
                dvln_ref[rows, cols] = _dot(wst_ref[g], dsvb)
        dvln = dvln_ref[...]
        dg_ref[...] += jnp.sum(dvln * xh, axis=0, keepdims=True)
        db_ref[...] += jnp.sum(dvln, axis=0, keepdims=True)
        dxh = dvln * lng_ref[...]
        dv = rstd * (dxh - jnp.mean(dxh, axis=-1, keepdims=True) - xh * jnp.mean(dxh * xh, axis=-1, keepdims=True))
        dzp_ref[:, D:2 * D] = (dv * dv_dz).astype(BF16)

        @pl.when(i == pl.num_programs(0) - 1)
        def _():
            dbs_ref[...] = acc_bs[...]

    def full(shape):
        return pl.BlockSpec(shape, lambda i: (0,) * len(shape))

    return pl.pallas_call(
        body, grid=(S // ts,),
        in_specs=[pl.BlockSpec((ts, D), lambda i: (i, 0)), full(w_out.shape),
                  pl.BlockSpec((ts, 2 * D), lambda i: (i, 0)),
                  full(lng.shape), full(lnb.shape), full(ws.shape), full(wst.shape), full(bsb.shape)],
        out_specs=[pl.BlockSpec((ts, 2 * D), lambda i: (i, 0)), full(ws.shape), full(bsb.shape),
                   full((1, D)), full((1, D))],
        out_shape=[jax.ShapeDtypeStruct((S, 2 * D), BF16), jax.ShapeDtypeStruct(ws.shape, F32),
                   jax.ShapeDtypeStruct(bsb.shape, F32), jax.ShapeDtypeStruct((1, D), F32),
                   jax.ShapeDtypeStruct((1, D), F32)],
        scratch_shapes=[pltpu.VMEM((ts, D), BF16), pltpu.VMEM((ts, D), F32),
                        pltpu.VMEM(bsb.shape, F32), pltpu.VMEM((ts, D), F32)],
        compiler_params=_cp("arbitrary"), name=name)(dh, w_out, zp, lng, lnb, ws, wst, bsb)


def _group_sum(x, groups, name):
    P, D = x.shape
    gd = D // groups

    def body(x_ref, o_ref):
        for g in range(groups):
            o_ref[:, g:g + 1] = jnp.sum(x_ref[:, g * gd:(g + 1) * gd], axis=1, keepdims=True)

    return pl.pallas_call(body, out_shape=jax.ShapeDtypeStruct((P, groups), F32), name=name)(x)


def _adamw_small(ws, gs, ms, vs, name):
    n = len(ws)
    shapes = [w.shape for w in ws]
    flat = [(w.size // w.shape[-1], w.shape[-1]) for w in ws]

    def body(*refs):
        for k in range(n):
            w_ref, g_ref, m_ref, v_ref = (refs[j * n + k] for j in range(4))
            d_ref, nm_ref, nv_ref = (refs[(4 + j) * n + k] for j in range(3))
            d_ref[...], nm_ref[...], nv_ref[...] = _adamw_math(w_ref[...], g_ref[...], m_ref[...], v_ref[...])

    outs = pl.pallas_call(
        body, out_shape=[jax.ShapeDtypeStruct(f, F32) for f in flat] * 3, name=name)(
            *(a.reshape(f) for group in (ws, gs, ms, vs) for a, f in zip(group, flat)))
    return [tuple(outs[j * n + k].reshape(shapes[k]) for j in range(3)) for k in range(n)]


_HBM = pl.BlockSpec(memory_space=pltpu.HBM)


def _remote(src, dst, send_sem, recv_sem, to):
    return pltpu.make_async_remote_copy(src_ref=src, dst_ref=dst, send_sem=send_sem, recv_sem=recv_sem,
                                        device_id=to, device_id_type=MESH)


def _all_gather(block, name):
    R, C = block.shape

    def body(x_ref, out_ref, send_sems, recv_sems, local_sem):
        x, y, c = lax.axis_index("x"), lax.axis_index("y"), lax.axis_index("c")
        me, sibling = (x, y, c), (x, y, 1 - c)
        chips = [(1 - x, y), (x, 1 - y), (1 - x, 1 - y)]

        def slot(px, py, pc):
            return out_ref.at[4 * px + 2 * py + pc]

        def copy(k, blk, to, src=None):
            return _remote(slot(*blk) if src is None else src, slot(*blk), send_sems.at[k], recv_sems.at[k], to)

        mine = pltpu.make_async_copy(x_ref, slot(*me), local_sem)
        mine.start()
        first = [copy(0, me, sibling, src=x_ref)]
        first += [copy(1 + j, me, (*chip, c), src=x_ref) for j, chip in enumerate(chips)]
        for cp in first:
            cp.start()
        passed = [copy(4 + j, (*chip, c), sibling) for j, chip in enumerate(chips)]
        for j, chip in enumerate(chips):
            copy(1 + j, (*chip, c), me).wait_recv()
            passed[j].start()
        copy(0, sibling, me).wait_recv()
        for j, chip in enumerate(chips):
            copy(4 + j, (*chip, 1 - c), me).wait_recv()
        for cp in first + passed:
            cp.wait_send()
        mine.wait()

    return pl.pallas_call(
        body, out_shape=jax.ShapeDtypeStruct((NDEV, R, C), block.dtype), in_specs=[_HBM], out_specs=_HBM,
        scratch_shapes=[pltpu.SemaphoreType.DMA((7,)), pltpu.SemaphoreType.DMA((7,)), pltpu.SemaphoreType.DMA],
        name=name)(block)


def _all_gather_weights(pack, rows, name, after=None):
    C = pack.shape[1]
    nw = len(rows)
    starts = [sum(rows[:w]) for w in range(nw)]
    after_specs, after_ops = _after(after)

    def body(pack_ref, *rest):
        rest = rest[len(after_ops):]
        outs = rest[:nw]
        send_sems, recv_sems, local_sem = rest[nw:]
        x, y, c = lax.axis_index("x"), lax.axis_index("y"), lax.axis_index("c")
        me, sibling = (x, y, c), (x, y, 1 - c)
        chips = [(1 - x, y), (x, 1 - y), (1 - x, 1 - y)]

        def block(w, px, py, pc):
            return outs[w].at[pl.ds((4 * px + 2 * py + pc) * rows[w], rows[w])]

        def mine(w):
            return pack_ref.at[pl.ds(starts[w], rows[w])]

        def all_of(k):
            return _remote(pack_ref, pack_ref, send_sems.at[k], recv_sems.at[k], me)

        for w in range(nw):
            pltpu.make_async_copy(mine(w), block(w, *me), local_sem).start()
        for k, to in enumerate([sibling] + [(*chip, c) for chip in chips]):
            for w in range(nw):
                _remote(mine(w), block(w, *me), send_sems.at[k], recv_sems.at[k], to).start()
        for j, chip in enumerate(chips):
            all_of(1 + j).wait_recv()
            for w in range(nw):
                _remote(block(w, *chip, c), block(w, *chip, c), send_sems.at[4 + j], recv_sems.at[4 + j], sibling).start()
        all_of(0).wait_recv()
        for j in range(3):
            all_of(4 + j).wait_recv()
        for k in range(7):
            all_of(k).wait_send()
        pltpu.make_async_copy(pack_ref, pack_ref, local_sem).wait()

    return pl.pallas_call(
        body, out_shape=[jax.ShapeDtypeStruct((NDEV * r, C), pack.dtype) for r in rows],
        in_specs=[_HBM] + after_specs, out_specs=[_HBM] * nw,
        scratch_shapes=[pltpu.SemaphoreType.DMA((7,)), pltpu.SemaphoreType.DMA((7,)), pltpu.SemaphoreType.DMA],
        name=name)(pack, *after_ops)


_SEM = pl.BlockSpec(memory_space=pltpu.SEMAPHORE)
_DATAFLOW = pltpu.SideEffectType.DATAFLOW_SIDE_EFFECTING


def _split_start(srcs, lands, plan, n, after, name):
    nbuf = len(srcs) + len(lands)
    after_specs, after_ops = _after(after)

    def body(*refs):
        src_refs, land_refs = refs[:len(srcs)], refs[len(srcs):nbuf]
        send_sems, recv_sems = refs[nbuf + len(after_ops)], refs[nbuf + len(after_ops) + 1]
        for k, (src, dst, to) in enumerate(plan(src_refs, land_refs)):
            _remote(src, dst, send_sems.at[k], recv_sems.at[k], to).start()
        refs[-1][...] = jnp.zeros_like(refs[-1])

    bufs = [pltpu.with_memory_space_constraint(a, pltpu.HBM) for a in list(srcs) + list(lands)]
    outs = pl.pallas_call(
        body, name=name,
        out_shape=(pltpu.SemaphoreType.DMA((n,)), pltpu.SemaphoreType.DMA((n,)),
                   *[pltpu.HBM(a.shape, a.dtype) for a in bufs], jax.ShapeDtypeStruct((8, 128), F32)),
        in_specs=[_HBM] * nbuf + after_specs,
        out_specs=(_SEM, _SEM, *[_HBM] * nbuf, pl.BlockSpec(memory_space=pltpu.VMEM)),
        input_output_aliases={i: 2 + i for i in range(nbuf)},
        compiler_params=pltpu.CompilerParams(has_side_effects=_DATAFLOW))(*bufs, *after_ops)
    return outs[0], outs[1], list(outs[2:2 + len(srcs)]), list(outs[2 + len(srcs):2 + nbuf]), outs[-1]


def _split_wait(send_sems, recv_sems, srcs, lands, plan, after, name):
    nbuf = len(srcs) + len(lands)
    after_specs, after_ops = _after(after)

    def body(*refs):
        src_refs, land_refs = refs[:len(srcs)], refs[len(srcs):nbuf]
        send_sems_ref, recv_sems_ref = refs[nbuf], refs[nbuf + 1]
        for k, (src, dst, to) in enumerate(plan(src_refs, land_refs)):
            copy = _remote(src, dst, send_sems_ref.at[k], recv_sems_ref.at[k], to)
            copy.wait_send()
            copy.wait_recv()

    outs = pl.pallas_call(
        body, name=name, out_shape=tuple(pltpu.HBM(a.shape, a.dtype) for a in list(srcs) + list(lands)),
        in_specs=[_HBM] * nbuf + [_SEM, _SEM] + after_specs, out_specs=tuple([_HBM] * nbuf),
        input_output_aliases={i: i for i in range(nbuf)},
        compiler_params=pltpu.CompilerParams(has_side_effects=_DATAFLOW))(*srcs, *lands, send_sems, recv_sems, *after_ops)
    return list(outs[:len(srcs)]), list(outs[len(srcs):])


def _peers(x, y, c):
    return [(mask, (1 - x if mask & 4 else x, 1 - y if mask & 2 else y, 1 - c if mask & 1 else c))
            for mask in range(1, NDEV)]


def _gather_plan(rows):
    starts = [sum(rows[:w]) for w in range(len(rows))]

    def plan(src_refs, land_refs):
        x, y, c = lax.axis_index("x"), lax.axis_index("y"), lax.axis_index("c")
        copies = []
        for w, r in enumerate(rows):
            mine = src_refs[0].at[pl.ds(starts[w], r)]
            dst = land_refs[w].at[pl.ds((4 * x + 2 * y + c) * r, r)]
            copies += [(mine, dst, peer) for _, peer in _peers(x, y, c)]
        return copies

    return plan, (NDEV - 1) * len(rows)


def _place_own(shards, fulls, dev_idx, name):
    nw = len(shards)

    def body(i_ref, *refs):
        for w in range(nw):
            refs[2 * nw + w][...] = refs[w][...]

    grid_spec = pltpu.PrefetchScalarGridSpec(
        num_scalar_prefetch=1, grid=(1,),
        in_specs=[pl.BlockSpec(s.shape, lambda t, i_ref: (0, 0)) for s in shards] + [_ANY] * nw,
        out_specs=[pl.BlockSpec(s.shape, lambda t, i_ref: (i_ref[0], 0)) for s in shards])
    outs = pl.pallas_call(
        body, grid_spec=grid_spec, out_shape=[jax.ShapeDtypeStruct(f.shape, f.dtype) for f in fulls],
        input_output_aliases={1 + nw + w: w for w in range(nw)}, name=name)(dev_idx, *shards, *fulls)
    return list(outs)


def _scatter_plan(rows):
    def plan(src_refs, land_refs):
        x, y, c = lax.axis_index("x"), lax.axis_index("y"), lax.axis_index("c")
        copies = []
        for w, r in enumerate(rows):
            for mask, (px, py, pc) in _peers(x, y, c):
                src = src_refs[w].at[pl.ds((4 * px + 2 * py + pc) * r, r)]
                copies.append((src, land_refs[w].at[mask - 1], (px, py, pc)))
        return copies

    return plan, (NDEV - 1) * len(rows)


def _adamw_math(w, g, m, v):
    nm = ADAM_B1 * m + (1.0 - ADAM_B1) * g
    nv = ADAM_B2 * v + (1.0 - ADAM_B2) * (g * g)
    bc1 = 1.0 - ADAM_B1 ** ADAM_STEP
    bc2 = 1.0 - ADAM_B2 ** ADAM_STEP
    return -ADAM_LR * ((nm / bc1) / (jnp.sqrt(nv / bc2) + ADAM_EPS) + ADAM_WD * w), nm, nv


def _finish_weight(gs, gots, dev_idx, w, m, v, name, after=None):
    L = len(gs)
    n1, r, C = gots[0].shape
    block = (None,) + w.shape[1:]

    after_specs, after_ops = _after(after)

    def body(i_ref, *refs):
        ins, (w_ref, m_ref, v_ref), (g_out, d_out, m_out, v_out) = refs[:2 * L], refs[2 * L:2 * L + 3], refs[-4:]
        for layer in range(L):
            @pl.when(pl.program_id(0) == layer)
            def _():
                g_ref, got_ref = ins[2 * layer], ins[2 * layer + 1]
                acc = g_ref[...].astype(F32)
                for k in range(n1):
                    acc = acc + got_ref[k].astype(F32)
                g_out[...] = acc
                d_out[...], m_out[...], v_out[...] = _adamw_math(w_ref[...], acc, m_ref[...], v_ref[...])

    in_specs, ins = [], []
    for g, got in zip(gs, gots):
        in_specs += [pl.BlockSpec((r, C), lambda t, i_ref: (i_ref[0], 0), pipeline_mode=_RESIDENT),
                     pl.BlockSpec((n1, r, C), lambda t, i_ref: (0, 0, 0), pipeline_mode=_RESIDENT)]
        ins += [g, got]
    per_layer = pl.BlockSpec(block, lambda t, i_ref: (t, 0, 0))
    grid_spec = pltpu.PrefetchScalarGridSpec(
        num_scalar_prefetch=1, grid=(L,), in_specs=in_specs + [per_layer] * 3 + after_specs,
        out_specs=[per_layer] * 4)
    return pl.pallas_call(
        body, grid_spec=grid_spec, out_shape=[jax.ShapeDtypeStruct(w.shape, F32)] * 4,
        compiler_params=_cp("arbitrary"), name=name)(dev_idx, *ins, w, m, v, *after_ops)


def _sum_slots(a, name):
    n, R, C = a.shape

    def body(a_ref, o_ref):
        acc = a_ref[0]
        for k in range(1, n):
            acc = acc + a_ref[k]
        o_ref[...] = acc

    return pl.pallas_call(body, out_shape=jax.ShapeDtypeStruct((R, C), F32), name=name)(a)


def _shard_axis(name):
    return {"ev_w_in": 2, "ev_a_conv_w": 2, "ev_b_conv_w": 2, "ev_w_out": 1, "od_w_in": 2, "od_c_ln_g": 1,
            "od_c_ln_b": 1, "od_w_out": 1, "xa_w_q": 1, "xa_w_k": 1, "xa_w_v": 1, "xa_w_o": 1,
            "ffn_w_gate": 2, "ffn_w_up": 2, "ffn_w_down": 1}[name]


BIG = ["ev_w_in", "ev_w_out", "od_w_in", "od_w_out", "xa_w_q", "xa_w_k", "xa_w_v", "xa_w_o",
       "ffn_w_gate", "ffn_w_up", "ffn_w_down"]
SMALL_SHARDED = ["ev_a_conv_w", "ev_b_conv_w", "od_c_ln_g", "od_c_ln_b"]
REPLICATED = ["g_mix", "g_xattn", "g_mem", "g_ffn", "g_final", "ev_a_conv_b", "ev_a_ln_g", "ev_a_ln_b",
              "ev_b_conv_b", "od_w_s", "od_b_s"]
WEIGHTS = ["g_mix", "g_xattn", "g_mem", "g_ffn", "g_final", "ev_w_in", "ev_a_conv_w", "ev_a_conv_b", "ev_a_ln_g",
           "ev_a_ln_b", "ev_b_conv_w", "ev_b_conv_b", "ev_w_out", "od_w_in", "od_c_ln_g", "od_c_ln_b", "od_w_s",
           "od_b_s", "od_w_out", "xa_w_q", "xa_w_k", "xa_w_v", "xa_w_o", "ffn_w_gate", "ffn_w_up", "ffn_w_down"]


def _full_from_blocks(blocks, axis):
    shard = blocks.shape[1:]
    full = jnp.moveaxis(blocks, 0, axis)
    return full.reshape(shard[:axis] + (NDEV * shard[axis],) + shard[axis + 1:])


def _blocks_from_full(full, axis):
    shp = full.shape
    split = full.reshape(shp[:axis] + (NDEV, shp[axis] // NDEV) + shp[axis + 1:])
    return jnp.moveaxis(split, axis, 0)


def _pad_rows(flat, width, row_align):
    per = width * row_align
    n = -(-flat.shape[0] // per) * per
    return jnp.pad(flat, (0, n - flat.shape[0])).reshape(n // width, width)


def _row(v):
    return v.reshape(1, -1)


def _mem_kv(mem, g_m, wk, wv, name):
    M, D = mem.shape

    def body(mem_ref, g_ref, wk_ref, wv_ref, n_ref, k_ref, v_ref):
        x = mem_ref[...]
        r = lax.rsqrt(jnp.mean(x * x, axis=-1, keepdims=True) + RMS_EPS)
        n = ((x * r) * g_ref[...]).astype(BF16)
        n_ref[...] = n
        k_ref[...] = _dot(n, wk_ref[...]).astype(BF16)
        v_ref[...] = _dot(n, wv_ref[...]).astype(BF16)

    sh = jax.ShapeDtypeStruct((M, D), BF16)
    return pl.pallas_call(body, out_shape=[sh, sh, sh], name=name)(mem, g_m, wk, wv)


def _mem_kv_bwd(dk, dv, mem, mem_n, g_m, wk, wv, name):
    M, D = mem.shape

    def body(dk_ref, dv_ref, mem_ref, n_ref, g_ref, wk_ref, wv_ref, dwk_ref, dwv_ref, dg_ref):
        dk_, dv_ = dk_ref[...].astype(BF16), dv_ref[...].astype(BF16)
        n = n_ref[...]
        dwk_ref[...] = _dot_tn(n, dk_).astype(BF16)
        dwv_ref[...] = _dot_tn(n, dv_).astype(BF16)
        dn = _dot_nt(dk_, wk_ref[...]) + _dot_nt(dv_, wv_ref[...])
        x = mem_ref[...]
        r = lax.rsqrt(jnp.mean(x * x, axis=-1, keepdims=True) + RMS_EPS)
        dg_ref[...] = jnp.sum(dn * (x * r), axis=0, keepdims=True)

    wsh = jax.ShapeDtypeStruct((D, D), BF16)
    return pl.pallas_call(body, out_shape=[wsh, wsh, jax.ShapeDtypeStruct((1, D), F32)], name=name)(
        dk, dv, mem, mem_n, g_m, wk, wv)


def _xattn_fwd(h, nq, mem, g_m, wq, wk, wv, wo, g_next, tag, after):
    q = _mm([(nq, wq, "nn")], f"xa_q_{tag}", out_dtype=BF16, after=after)
    mem_n, k, v = _mem_kv(mem, _row(g_m), wk, wv, f"xa_mem_{tag}")
    o = _attn_fwd(q, k, v, f"xa_attn_{tag}")
    h_new, n_next = _mm([(o, wo, "nn")], f"xa_o_{tag}", res=h, rms_g=_row(g_next))
    return h_new, n_next, (h, nq, mem_n, q, k, v, o)


def _xattn_bwd(dh_new, saved, mem, g_x, g_m, wq, wk, wv, wo, tag, push):
    h, nq, mem_n, q, k, v, o = saved
    do = _mm([(dh_new, wo, "nt")], f"xa_do_{tag}", out_dtype=BF16)
    d_wo = _mm_tn(o, dh_new, f"xa_dwo_{tag}")
    dq, dk, dv = _attn_bwd(q, k, v, do, f"xa_attn_bwd_{tag}")
    d_wq = _mm_tn(nq, dq, f"xa_dwq_{tag}")
    d_wk, d_wv, d_gm = _mem_kv_bwd(dk, dv, mem, mem_n, _row(g_m), wk, wv, f"xa_mem_bwd_{tag}")
    token = push([d_wq, d_wk, d_wv, d_wo])
    dh, d_gx = _mm([(dq, wq, "nt")], f"xa_dnq_{tag}", rms_bwd=(h, _row(g_x), dh_new), tm=1024, after=token)
    return dh, dict(g_xattn=d_gx, g_mem=d_gm)


def _ffn_fwd(h, n, wgt, wut, wd, g_next, tag, after, loss=None):
    a, b, hid = _ffn_up(n, wgt, wut, f"ffn_up_{tag}", after=after)
    saved = (h, n, a, b, hid)
    if loss is not None:
        return _mm([(hid, wd, "nn")], f"ffn_down_{tag}", res=h, loss=loss, tm=1024), saved
    h_new, n_next = _mm([(hid, wd, "nn")], f"ffn_down_{tag}", res=h, rms_g=_row(g_next), tm=1024)
    return h_new, n_next, saved


def _ffn_bwd(dh_new, saved, g_f, wgt, wut, wd, tag, push):
    h, n, a, b, hid = saved
    da, db = _ffn_dhid(dh_new, wd, a, b, f"ffn_dhid_{tag}")
    d_wd = _mm_tn(hid, dh_new, f"ffn_dwd_{tag}", ts=1024, tn=1024)
    d_wgt = _mm_tn(da, n, f"ffn_dwg_{tag}", ts=1024, tn=1024)
    d_wut = _mm_tn(db, n, f"ffn_dwu_{tag}", ts=1024, tn=1024)
    token = push([d_wgt, d_wut, d_wd])
    dh, d_gf = _mm([(da, wgt, "nn"), (db, wut, "nn")], f"ffn_dn_{tag}", rms_bwd=(h, _row(g_f), dh_new), tm=512,
                   after=token)
    return dh, dict(g_ffn=d_gf)


_XA = ["xa_w_q", "xa_w_k", "xa_w_v", "xa_w_o"]
_FFN = ["ffn_w_gate", "ffn_w_up", "ffn_w_down"]
GATHERS = {
    "ev_in": [("ev_w_in", 0)],
    "xa0": [("ev_w_out", 0)] + [(n, 0) for n in _XA],
    "ffn0": [(n, 0) for n in _FFN],
    "od": [("od_w_in", 0), ("od_w_out", 0)],
    "xa1": [(n, 1) for n in _XA],
    "ffn1": [(n, 1) for n in _FFN],
}
SCATTERS = {
    "ffn1": [(n, 1) for n in _FFN],
    "xa1": [(n, 1) for n in _XA],
    "od": [("od_w_in", 0), ("od_w_out", 0)],
    "ffn0": [(n, 0) for n in _FFN],
    "xa0": [(n, 0) for n in _XA],
    "ev_out": [("ev_w_out", 0)],
    "ev_in": [("ev_w_in", 0)],
}


def _local_step(x, mem, loss_target, W, comm):
    grads = {}

    h0 = x
    (ev_w_in_t,), token = comm.weights("ev_in", None)
    n0 = _rms_fwd(h0, _row(W["g_mix"][0]), "ev_rms", after=token)
    z = _mm([(n0, ev_w_in_t, "nt")], "ev_in", tn=1280)
    token = comm.prefetch(["ffn0"], z)
    ab, ca = _conv_fwd(z, W["ev_a_conv_w"][0], W["ev_a_conv_b"], W["ev_a_ln_g"], W["ev_a_ln_b"],
                       W["ev_b_conv_w"][0], W["ev_b_conv_b"], "ev_conv", after=token)
    (ev_w_out, *xa_w0), _ = comm.weights("xa0", ab)
    h1, nq0 = _mm([(ab, ev_w_out, "nn")], "ev_out", res=h0, rms_g=_row(W["g_xattn"][0]))
    token = comm.prefetch(["od", "xa1"], nq0)
    h2, nf0, xa0 = _xattn_fwd(h1, nq0, mem, W["g_mem"][0], *xa_w0, W["g_ffn"][0], "l0", token)
    ffn_w0, _ = comm.weights("ffn0", nf0)
    token = comm.prefetch(["ffn1"], nf0)
    h3, n3, ff0 = _ffn_fwd(h2, nf0, *ffn_w0, W["g_mix"][1], "l0", token)

    (od_w_in_t, od_w_out), _ = comm.weights("od", n3)
    zp = _mm([(n3, od_w_in_t, "nt")], "od_in", tn=1024)
    D = x.shape[1]
    ws = W["od_w_s"][0].astype(BF16)
    wst = jnp.swapaxes(ws, 1, 2)
    bsb = jnp.repeat(jnp.transpose(W["od_b_s"][0]), D // C_GROUPS, axis=1)
    y_sgu = _sgu_fwd(zp, W["od_c_ln_g"], W["od_c_ln_b"], ws, bsb, "od_sgu")
    h4, nq1 = _mm([(y_sgu, od_w_out, "nn")], "od_out", res=h3, rms_g=_row(W["g_xattn"][1]))
    xa_w1, _ = comm.weights("xa1", nq1)
    h5, nf1, xa1 = _xattn_fwd(h4, nq1, mem, W["g_mem"][1], *xa_w1, W["g_ffn"][1], "l1", None)
    ffn_w1, _ = comm.weights("ffn1", nf1)
    (loss_row, dh6, d_gfinal), ff1 = _ffn_fwd(h5, nf1, *ffn_w1, None, "l1", None,
                                              loss=(_row(W["g_final"]), loss_target))
    grads["g_final"] = d_gfinal.reshape(-1)


    dh5, g_ff1 = _ffn_bwd(dh6, ff1, W["g_ffn"][1], *ffn_w1, "l1", lambda dws: comm.grads("ffn1", dws))
    dh4, g_xa1 = _xattn_bwd(dh5, xa1, mem, W["g_xattn"][1], W["g_mem"][1], *xa_w1, "l1",
                            lambda dws: comm.grads("xa1", dws))
    d_od_out = _mm_tn(y_sgu, dh4, "od_dwout", tn=1024)
    dzp, d_ws, d_bsb, d_clng, d_clnb = _sgu_bwd(dh4, od_w_out, zp, W["od_c_ln_g"], W["od_c_ln_b"], ws, wst, bsb,
                                                "od_sgu_bwd")
    grads["od_w_s"] = d_ws[None]
    grads["od_b_s"] = jnp.transpose(_group_sum(d_bsb, C_GROUPS, "od_dbs"))[None]
    grads["od_c_ln_g"], grads["od_c_ln_b"] = d_clng, d_clnb
    token = comm.grads("od", [_mm_tn(dzp, n3, "od_dwin", ts=1024, tn=1024), d_od_out])
    dh3, d_gmix1 = _mm([(dzp, od_w_in_t, "nn")], "od_dn", rms_bwd=(h3, _row(W["g_mix"][1]), dh4), tm=1024, after=token)

    dh2, g_ff0 = _ffn_bwd(dh3, ff0, W["g_ffn"][0], *ffn_w0, "l0", lambda dws: comm.grads("ffn0", dws))
    dh1, g_xa0 = _xattn_bwd(dh2, xa0, mem, W["g_xattn"][0], W["g_mem"][0], *xa_w0, "l0",
                            lambda dws: comm.grads("xa0", dws))
    token = comm.grads("ev_out", [_mm_tn(ab, dh1, "ev_dwout", tn=1024)])
    dca, db, d_lng, d_lnb, d_ba = _conv_dab_ln(dh1, ev_w_out, ca, W["ev_a_ln_g"], W["ev_a_ln_b"], "ev_dab",
                                               after=token)
    dz, d_wa, d_wb, d_bb = _conv_bwd(z, dca, db, W["ev_a_conv_w"][0], W["ev_b_conv_w"][0], W["ev_b_conv_b"],
                                     "ev_conv_bwd")
    grads.update(ev_a_ln_g=d_lng, ev_a_ln_b=d_lnb, ev_a_conv_b=d_ba, ev_b_conv_b=d_bb,
                 ev_a_conv_w=d_wa[None], ev_b_conv_w=d_wb[None])
    token = comm.grads("ev_in", [_mm_tn(dz, n0, "ev_dwin", ts=1024, tn=1024)])
    grad_x, d_gmix0 = _mm([(dz, ev_w_in_t, "nn")], "ev_dn", rms_bwd=(h0, _row(W["g_mix"][0]), dh1), tm=1024, after=token)

    grads["g_mix"] = jnp.concatenate([d_gmix0, d_gmix1], axis=0)
    for key in ("g_xattn", "g_mem"):
        grads[key] = jnp.concatenate([g_xa0[key], g_xa1[key]], axis=0)
    grads["g_ffn"] = jnp.concatenate([g_ff0["g_ffn"], g_ff1["g_ffn"]], axis=0)
    return loss_row, grad_x, grads


class _Exchanges:
    def __init__(self, shards, dev_idx, after):
        self.shards, self.dev_idx = shards, dev_idx
        self.gathering, self.scattering = {}, {}
        self.first = _all_gather_weights(self._pack(GATHERS["ev_in"]), self._rows(GATHERS["ev_in"]), "ag_ev_in",
                                         after=after)
        self.first_token = self.prefetch(["xa0"], self.first[0])

    def _rows(self, entries):
        return [self.shards[e].shape[0] for e in entries]

    def _pack(self, entries):
        return jnp.concatenate([self.shards[e] for e in entries], axis=0)

    def prefetch(self, gathers, after):
        for name in gathers:
            rows = self._rows(GATHERS[name])
            pack = self._pack(GATHERS[name])
            lands = [lax.empty((NDEV * r, pack.shape[1]), pack.dtype) for r in rows]
            plan, n = _gather_plan(rows)
            send, recv, srcs, lands, after = _split_start([pack], lands, plan, n, after, f"ag_{name}_start")
            self.gathering[name] = (send, recv, srcs, lands, plan, rows)
        return after

    def weights(self, name, after):
        if name == "ev_in":
            return self.first, self.first_token
        send, recv, srcs, lands, plan, rows = self.gathering.pop(name)
        _, lands = _split_wait(send, recv, srcs, lands, plan, after, f"ag_{name}_wait")
        return _place_own([self.shards[e] for e in GATHERS[name]], lands, self.dev_idx, f"ag_{name}_own"), None

    def grads(self, name, dws):
        rows = self._rows(SCATTERS[name])
        lands = [lax.empty((NDEV - 1, r, d.shape[1]), d.dtype) for r, d in zip(rows, dws)]
        plan, n = _scatter_plan(rows)
        send, recv, srcs, lands, token = _split_start(dws, lands, plan, n, None, f"rs_{name}_start")
        self.scattering[name] = (send, recv, srcs, lands, plan)
        return token

    def received(self, after):
        out = {}
        for name, (send, recv, srcs, lands, plan) in self.scattering.items():
            srcs, lands = _split_wait(send, recv, srcs, lands, plan, after, f"rs_{name}_wait")
            for entry, g, got in zip(SCATTERS[name], srcs, lands):
                out[entry] = (g, got)
        return out


def kernel(x, mem, g_mix, g_xattn, g_mem, g_ffn, g_final, ev_w_in, ev_a_conv_w, ev_a_conv_b, ev_a_ln_g, ev_a_ln_b, ev_b_conv_w, ev_b_conv_b, ev_w_out, od_w_in, od_c_ln_g, od_c_ln_b, od_w_s, od_b_s, od_w_out, xa_w_q, xa_w_k, xa_w_v, xa_w_o, ffn_w_gate, ffn_w_up, ffn_w_down, loss_target, m_g_mix, m_g_xattn, m_g_mem, m_g_ffn, m_g_final, m_ev_w_in, m_ev_a_conv_w, m_ev_a_conv_b, m_ev_a_ln_g, m_ev_a_ln_b, m_ev_b_conv_w, m_ev_b_conv_b, m_ev_w_out, m_od_w_in, m_od_c_ln_g, m_od_c_ln_b, m_od_w_s, m_od_b_s, m_od_w_out, m_xa_w_q, m_xa_w_k, m_xa_w_v, m_xa_w_o, m_ffn_w_gate, m_ffn_w_up, m_ffn_w_down, v_g_mix, v_g_xattn, v_g_mem, v_g_ffn, v_g_final, v_ev_w_in, v_ev_a_conv_w, v_ev_a_conv_b, v_ev_a_ln_g, v_ev_a_ln_b, v_ev_b_conv_w, v_ev_b_conv_b, v_ev_w_out, v_od_w_in, v_od_c_ln_g, v_od_c_ln_b, v_od_w_s, v_od_b_s, v_od_w_out, v_xa_w_q, v_xa_w_k, v_xa_w_v, v_xa_w_o, v_ffn_w_gate, v_ffn_w_up, v_ffn_w_down):
    local = dict(g_mix=g_mix, g_xattn=g_xattn, g_mem=g_mem, g_ffn=g_ffn, g_final=g_final, ev_w_in=ev_w_in, ev_a_conv_w=ev_a_conv_w, ev_a_conv_b=ev_a_conv_b, ev_a_ln_g=ev_a_ln_g, ev_a_ln_b=ev_a_ln_b, ev_b_conv_w=ev_b_conv_w, ev_b_conv_b=ev_b_conv_b, ev_w_out=ev_w_out, od_w_in=od_w_in, od_c_ln_g=od_c_ln_g, od_c_ln_b=od_c_ln_b, od_w_s=od_w_s, od_b_s=od_b_s, od_w_out=od_w_out, xa_w_q=xa_w_q, xa_w_k=xa_w_k, xa_w_v=xa_w_v, xa_w_o=xa_w_o, ffn_w_gate=ffn_w_gate, ffn_w_up=ffn_w_up, ffn_w_down=ffn_w_down)
    mom = dict(g_mix=m_g_mix, g_xattn=m_g_xattn, g_mem=m_g_mem, g_ffn=m_g_ffn, g_final=m_g_final, ev_w_in=m_ev_w_in, ev_a_conv_w=m_ev_a_conv_w, ev_a_conv_b=m_ev_a_conv_b, ev_a_ln_g=m_ev_a_ln_g, ev_a_ln_b=m_ev_a_ln_b, ev_b_conv_w=m_ev_b_conv_w, ev_b_conv_b=m_ev_b_conv_b, ev_w_out=m_ev_w_out, od_w_in=m_od_w_in, od_c_ln_g=m_od_c_ln_g, od_c_ln_b=m_od_c_ln_b, od_w_s=m_od_w_s, od_b_s=m_od_b_s, od_w_out=m_od_w_out, xa_w_q=m_xa_w_q, xa_w_k=m_xa_w_k, xa_w_v=m_xa_w_v, xa_w_o=m_xa_w_o, ffn_w_gate=m_ffn_w_gate, ffn_w_up=m_ffn_w_up, ffn_w_down=m_ffn_w_down)
    vel = dict(g_mix=v_g_mix, g_xattn=v_g_xattn, g_mem=v_g_mem, g_ffn=v_g_ffn, g_final=v_g_final, ev_w_in=v_ev_w_in, ev_a_conv_w=v_ev_a_conv_w, ev_a_conv_b=v_ev_a_conv_b, ev_a_ln_g=v_ev_a_ln_g, ev_a_ln_b=v_ev_a_ln_b, ev_b_conv_w=v_ev_b_conv_w, ev_b_conv_b=v_ev_b_conv_b, ev_w_out=v_ev_w_out, od_w_in=v_od_w_in, od_c_ln_g=v_od_c_ln_g, od_c_ln_b=v_od_c_ln_b, od_w_s=v_od_w_s, od_b_s=v_od_b_s, od_w_out=v_od_w_out, xa_w_q=v_xa_w_q, xa_w_k=v_xa_w_k, xa_w_v=v_xa_w_v, xa_w_o=v_xa_w_o, ffn_w_gate=v_ffn_w_gate, ffn_w_up=v_ffn_w_up, ffn_w_down=v_ffn_w_down)
    D = x.shape[-1]
    dev = 4 * lax.axis_index("x") + 2 * lax.axis_index("y") + lax.axis_index("c")

    def comm_layout(n, a):
        return jnp.transpose(a) if _shard_axis(n) == 2 else a

    shards = {(n, i): comm_layout(n, local[n][i]).astype(BF16) for n in BIG for i in range(local[n].shape[0])}
    small_sizes = [local[n].size for n in SMALL_SHARDED]
    small_block = _pad_rows(jnp.concatenate([local[n].reshape(-1) for n in SMALL_SHARDED]), 128, 8)
    small_all = _all_gather(small_block, "ag_small")
    comm = _Exchanges(shards, jnp.reshape(dev, (1,)).astype(jnp.int32), small_all)
    small_all = small_all.reshape(NDEV, -1)

    W = {n: local[n] for n in REPLICATED}
    o0 = 0
    for n, sz in zip(SMALL_SHARDED, small_sizes):
        blocks = small_all[:, o0:o0 + sz].reshape((NDEV,) + local[n].shape)
        W[n] = _full_from_blocks(blocks, _shard_axis(n))
        o0 += sz

    loss_row, grad_x, grads = _local_step(x[0], mem[0], loss_target[0], W, comm)

    received = comm.received(grad_x)
    rest = REPLICATED + SMALL_SHARDED
    rest_full_shapes = [grads[n].shape for n in rest]
    loss_at = sum(grads[n].size for n in rest)
    g_rest = _pad_rows(jnp.concatenate([grads[n].astype(F32).reshape(-1) for n in rest] + [loss_row[0, :1]]), D, 8)
    small_rows = g_rest.shape[0]
    small_plan, small_n = _gather_plan([small_rows])
    small_send, small_recv, small_srcs, small_lands, token = _split_start(
        [g_rest], [lax.empty((NDEV * small_rows, D), F32)], small_plan, small_n, received["ev_w_in", 0][1],
        "ag_small_grads_start")

    gsh, delta, new_m, new_v = {}, {}, {}, {}
    def stacked_layout(n, a):
        return jnp.swapaxes(a, 1, 2) if _shard_axis(n) == 2 else a

    for n in BIG:
        parts = [received[n, i] for i in range(local[n].shape[0])]
        outs = _finish_weight([p[0] for p in parts], [p[1] for p in parts], comm.dev_idx,
                              *(stacked_layout(n, a) for a in (local[n], mom[n], vel[n])), f"finish_{n}", after=token)
        gsh[n], delta[n], new_m[n], new_v[n] = (stacked_layout(n, o) for o in outs)

    _, small_lands = _split_wait(small_send, small_recv, small_srcs, small_lands, small_plan,
                                 [delta[n] for n in BIG], "ag_small_grads_wait")
    partials = _place_own([g_rest], small_lands, comm.dev_idx, "ag_small_grads_own")[0]
    g_rest = _sum_slots(partials.reshape(NDEV, small_rows, D), "sum_small_grads").reshape(-1)
    o0 = 0
    for n, shp in zip(rest, rest_full_shapes):
        sz = 1
        for s in shp:
            sz *= s
        full = g_rest[o0:o0 + sz].reshape(shp)
        o0 += sz
        if n in SMALL_SHARDED:
            full = lax.dynamic_index_in_dim(_blocks_from_full(full, _shard_axis(n)), dev, 0, keepdims=False)
        gsh[n] = full.reshape(local[n].shape)

    small = _adamw_small([local[n] for n in rest], [gsh[n] for n in rest], [mom[n] for n in rest],
                         [vel[n] for n in rest], "adamw_small")
    for n, (d, nm, nv) in zip(rest, small):
        delta[n], new_m[n], new_v[n] = d, nm, nv

    loss = g_rest[loss_at]
    return (loss, grad_x[None], *[gsh[n] for n in WEIGHTS], *[delta[n] for n in WEIGHTS],
            *[new_m[n] for n in WEIGHTS], *[new_v[n] for n in WEIGHTS])
```
